```python
import jax, jax.numpy as jnp
from jax import lax
import numpy as np

D_MODEL = 2048
BATCH = 8
SEQ = 2048
DEPTH = 2

N_A = DEPTH // 2
N_B = DEPTH - N_A
CONV_WIDTH = 31
N_HEADS = 16
QK_NOPE_DIM = 128
QK_ROPE_DIM = 64
V_HEAD_DIM = 128
KV_LORA_RANK = 512
Q_LORA_RANK = 512
ROPE_THETA = 10000.0
Q_BLOCK = 128
D_FF = 4 * D_MODEL
EPS = 1e-6
NEG = -1e30

kernel_name = "yoco_conformer_mla_hybrid"


def rms_norm(x, g):
    xf = x.astype(jnp.float32)
    y = xf * lax.rsqrt(jnp.mean(xf * xf, axis=-1, keepdims=True) + EPS)
    return (y * g.astype(jnp.float32)).astype(x.dtype)


def layer_norm(x, g, b):
    xf = x.astype(jnp.float32)
    mu = jnp.mean(xf, axis=-1, keepdims=True)
    var = jnp.mean(jnp.square(xf - mu), axis=-1, keepdims=True)
    y = (xf - mu) * lax.rsqrt(var + EPS)
    return (y * g.astype(jnp.float32) + b.astype(jnp.float32)).astype(x.dtype)


def adaln(c, w, b, n):
    m = jax.nn.silu(c) @ w + b
    return jnp.split(m, n, axis=-1)


def modulate(h, shift, scale):
    return h * (1.0 + scale[:, None, :]) + shift[:, None, :]


def rope_tables(positions, dtype):
    inv = 1.0 / (ROPE_THETA ** (jnp.arange(0, QK_ROPE_DIM, 2, dtype=jnp.float32) / QK_ROPE_DIM))
    ang = positions.astype(jnp.float32)[..., None] * inv
    return jnp.cos(ang).astype(dtype), jnp.sin(ang).astype(dtype)


def apply_rope(x, cos, sin):
    x1, x2 = jnp.split(x, 2, axis=-1)
    return jnp.concatenate([x1 * cos - x2 * sin, x2 * cos + x1 * sin], axis=-1)


def conformer_conv(h, w_in, b_in, dw, dw_b, ln_g, ln_b, w_out, b_out):
    u = jax.nn.glu(h @ w_in + b_in, axis=-1)
    u = lax.conv_general_dilated(
        u, dw[:, None, :], window_strides=(1,), padding=[(CONV_WIDTH - 1, 0)],
        dimension_numbers=("NWC", "WIO", "NWC"), feature_group_count=D_MODEL) + dw_b
    u = jax.nn.silu(layer_norm(u, ln_g, ln_b))
    return u @ w_out + b_out


def mla_attention(q_nope, q_rope, k_nope, k_rope, v):
    b, s, h, _ = q_nope.shape
    nblk = s // Q_BLOCK
    scale = (QK_NOPE_DIM + QK_ROPE_DIM) ** -0.5
    qn = q_nope.reshape(b, nblk, Q_BLOCK, h, QK_NOPE_DIM).transpose(1, 0, 2, 3, 4)
    qr = q_rope.reshape(b, nblk, Q_BLOCK, h, QK_ROPE_DIM).transpose(1, 0, 2, 3, 4)
    key_pos = jnp.arange(s)

    def one_block(args):
        i, qn_b, qr_b = args
        sc = (jnp.einsum("bqhd,bkhd->bhqk", qn_b, k_nope)
              + jnp.einsum("bqhd,bkd->bhqk", qr_b, k_rope)).astype(jnp.float32) * scale
        q_pos = i * Q_BLOCK + jnp.arange(Q_BLOCK)
        mask = key_pos[None, :] <= q_pos[:, None]
        sc = jnp.where(mask[None, None], sc, NEG)
        p = jax.nn.softmax(sc, axis=-1).astype(v.dtype)
        return jnp.einsum("bhqk,bkhd->bqhd", p, v)

    out = lax.map(one_block, (jnp.arange(nblk), qn, qr))
    return out.transpose(1, 0, 2, 3, 4).reshape(b, s, h * V_HEAD_DIM)


def _fwd_setup_inputs(seed: int = 0) -> dict:
    key = jax.random.key(seed)
    ks = iter(jax.random.split(key, 64))
    D = D_MODEL

    def nrm(shape, scale):
        return jax.random.normal(next(ks), shape, jnp.float32) * scale

    def gain(shape):
        return 1.0 + nrm(shape, 0.02)

    x = jax.random.normal(next(ks), (BATCH, SEQ, D), jnp.float32)
    c = jax.random.normal(next(ks), (BATCH, D), jnp.float32)
    positions = (jnp.arange(SEQ, dtype=jnp.int32)[None, :]
                 + jax.random.randint(next(ks), (BATCH, 1), 0, 1024, dtype=jnp.int32))
    ada_s = 0.1 * D ** -0.5
    return {
        "x": x, "c": c, "positions": positions,
        "w_ada_mix": nrm((DEPTH, D, 3 * D), ada_s), "b_ada_mix": nrm((DEPTH, 3 * D), 0.01),
        "w_ada_mlp": nrm((DEPTH, D, 3 * D), ada_s), "b_ada_mlp": nrm((DEPTH, 3 * D), 0.01),
        "g_pre_mix": gain((DEPTH, D)), "g_post_mix": gain((DEPTH, D)),
        "g_pre_mlp": gain((DEPTH, D)), "g_post_mlp": gain((DEPTH, D)),
        "conv_w_in": nrm((N_A, D, 2 * D), D ** -0.5), "conv_b_in": nrm((N_A, 2 * D), 0.01),
        "conv_dw": nrm((N_A, CONV_WIDTH, D), CONV_WIDTH ** -0.5), "conv_dw_b": nrm((N_A, D), 0.01),
        "conv_ln_g": gain((N_A, D)), "conv_ln_b": nrm((N_A, D), 0.01),
        "conv_w_out": nrm((N_A, D, D), D ** -0.5), "conv_b_out": nrm((N_A, D), 0.01),
        "w_ada_kv": nrm((D, 2 * D), ada_s), "b_ada_kv": nrm((2 * D,), 0.01), "g_kv": gain((D,)),
        "w_dkv": nrm((D, KV_LORA_RANK), D ** -0.5), "g_ckv": gain((KV_LORA_RANK,)),
        "w_kr": nrm((D, QK_ROPE_DIM), D ** -0.5),
        "w_uk": nrm((KV_LORA_RANK, N_HEADS * QK_NOPE_DIM), KV_LORA_RANK ** -0.5),
        "w_uv": nrm((KV_LORA_RANK, N_HEADS * V_HEAD_DIM), KV_LORA_RANK ** -0.5),
        "w_dq": nrm((N_B, D, Q_LORA_RANK), D ** -0.5), "g_cq": gain((N_B, Q_LORA_RANK)),
        "w_uq": nrm((N_B, Q_LORA_RANK, N_HEADS * (QK_NOPE_DIM + QK_ROPE_DIM)), Q_LORA_RANK ** -0.5),
        "w_o": nrm((N_B, N_HEADS * V_HEAD_DIM, D), (N_HEADS * V_HEAD_DIM) ** -0.5),
        "mlp_w_up": nrm((DEPTH, D, D_FF), D ** -0.5),
        "mlp_w_down": nrm((DEPTH, D_FF, D), D_FF ** -0.5),
    }


def _fwd_reference(x, c, positions,
              w_ada_mix, b_ada_mix, w_ada_mlp, b_ada_mlp,
              g_pre_mix, g_post_mix, g_pre_mlp, g_post_mlp,
              conv_w_in, conv_b_in, conv_dw, conv_dw_b, conv_ln_g, conv_ln_b, conv_w_out, conv_b_out,
              w_ada_kv, b_ada_kv, g_kv,
              w_dkv, g_ckv, w_kr, w_uk, w_uv,
              w_dq, g_cq, w_uq, w_o,
              mlp_w_up, mlp_w_down):
    b, s, _ = x.shape
    cos, sin = rope_tables(positions, x.dtype)
    h = x
    k_nope = k_rope = v = None
    for l in range(DEPTH):
        shift, scale, gate = adaln(c, w_ada_mix[l], b_ada_mix[l], 3)
        hn = modulate(rms_norm(h, g_pre_mix[l]), shift, scale)
        if l < N_A:
            y = conformer_conv(hn, conv_w_in[l], conv_b_in[l], conv_dw[l], conv_dw_b[l],
                               conv_ln_g[l], conv_ln_b[l], conv_w_out[l], conv_b_out[l])
        else:
            j = l - N_A
            if j == 0:
                kv_shift, kv_scale = adaln(c, w_ada_kv, b_ada_kv, 2)
                kvn = modulate(rms_norm(h, g_kv), kv_shift, kv_scale)
                c_kv = rms_norm(kvn @ w_dkv, g_ckv)
                k_nope = (c_kv @ w_uk).reshape(b, s, N_HEADS, QK_NOPE_DIM)
                v = (c_kv @ w_uv).reshape(b, s, N_HEADS, V_HEAD_DIM)
                k_rope = apply_rope(kvn @ w_kr, cos, sin)
            c_q = rms_norm(hn @ w_dq[j], g_cq[j])
            q = (c_q @ w_uq[j]).reshape(b, s, N_HEADS, QK_NOPE_DIM + QK_ROPE_DIM)
            q_nope = q[..., :QK_NOPE_DIM]
            q_rope = apply_rope(q[..., QK_NOPE_DIM:], cos[:, :, None, :], sin[:, :, None, :])
            y = mla_attention(q_nope, q_rope, k_nope, k_rope, v) @ w_o[j]
        h = h + gate[:, None, :] * rms_norm(y, g_post_mix[l])
        shift, scale, gate = adaln(c, w_ada_mlp[l], b_ada_mlp[l], 3)
        hn = modulate(rms_norm(h, g_pre_mlp[l]), shift, scale)
        y = jnp.square(jax.nn.relu(hn @ mlp_w_up[l])) @ mlp_w_down[l]
        h = h + gate[:, None, :] * rms_norm(y, g_post_mlp[l])
    return h


import jax as _jax
import jax.numpy as _jnp

TWIN_FORMAT = 'train_step'
FWD_PARAMS = ['x', 'c', 'positions', 'w_ada_mix', 'b_ada_mix', 'w_ada_mlp', 'b_ada_mlp', 'g_pre_mix', 'g_post_mix', 'g_pre_mlp', 'g_post_mlp', 'conv_w_in', 'conv_b_in', 'conv_dw', 'conv_dw_b', 'conv_ln_g', 'conv_ln_b', 'conv_w_out', 'conv_b_out', 'w_ada_kv', 'b_ada_kv', 'g_kv', 'w_dkv', 'g_ckv', 'w_kr', 'w_uk', 'w_uv', 'w_dq', 'g_cq', 'w_uq', 'w_o', 'mlp_w_up', 'mlp_w_down']
TWIN_WEIGHTS = ['w_ada_mix', 'b_ada_mix', 'w_ada_mlp', 'b_ada_mlp', 'g_pre_mix', 'g_post_mix', 'g_pre_mlp', 'g_post_mlp', 'conv_w_in', 'conv_b_in', 'conv_dw', 'conv_dw_b', 'conv_ln_g', 'conv_ln_b', 'conv_w_out', 'conv_b_out', 'w_ada_kv', 'b_ada_kv', 'g_kv', 'w_dkv', 'g_ckv', 'w_kr', 'w_uk', 'w_uv', 'w_dq', 'g_cq', 'w_uq', 'w_o', 'mlp_w_up', 'mlp_w_down']
TWIN_DIFF_INPUT = 'x'
TWIN_INPUTS = ['x', 'c', 'positions', 'w_ada_mix', 'b_ada_mix', 'w_ada_mlp', 'b_ada_mlp', 'g_pre_mix', 'g_post_mix', 'g_pre_mlp', 'g_post_mlp', 'conv_w_in', 'conv_b_in', 'conv_dw', 'conv_dw_b', 'conv_ln_g', 'conv_ln_b', 'conv_w_out', 'conv_b_out', 'w_ada_kv', 'b_ada_kv', 'g_kv', 'w_dkv', 'g_ckv', 'w_kr', 'w_uk', 'w_uv', 'w_dq', 'g_cq', 'w_uq', 'w_o', 'mlp_w_up', 'mlp_w_down', 'loss_target', 'm_w_ada_mix', 'm_b_ada_mix', 'm_w_ada_mlp', 'm_b_ada_mlp', 'm_g_pre_mix', 'm_g_post_mix', 'm_g_pre_mlp', 'm_g_post_mlp', 'm_conv_w_in', 'm_conv_b_in', 'm_conv_dw', 'm_conv_dw_b', 'm_conv_ln_g', 'm_conv_ln_b', 'm_conv_w_out', 'm_conv_b_out', 'm_w_ada_kv', 'm_b_ada_kv', 'm_g_kv', 'm_w_dkv', 'm_g_ckv', 'm_w_kr', 'm_w_uk', 'm_w_uv', 'm_w_dq', 'm_g_cq', 'm_w_uq', 'm_w_o', 'm_mlp_w_up', 'm_mlp_w_down', 'v_w_ada_mix', 'v_b_ada_mix', 'v_w_ada_mlp', 'v_b_ada_mlp', 'v_g_pre_mix', 'v_g_post_mix', 'v_g_pre_mlp', 'v_g_post_mlp', 'v_conv_w_in', 'v_conv_b_in', 'v_conv_dw', 'v_conv_dw_b', 'v_conv_ln_g', 'v_conv_ln_b', 'v_conv_w_out', 'v_conv_b_out', 'v_w_ada_kv', 'v_b_ada_kv', 'v_g_kv', 'v_w_dkv', 'v_g_ckv', 'v_w_kr', 'v_w_uk', 'v_w_uv', 'v_w_dq', 'v_g_cq', 'v_w_uq', 'v_w_o', 'v_mlp_w_up', 'v_mlp_w_down']
TWIN_OUTPUTS = ['loss', 'grad_x', 'grad_w_ada_mix', 'grad_b_ada_mix', 'grad_w_ada_mlp', 'grad_b_ada_mlp', 'grad_g_pre_mix', 'grad_g_post_mix', 'grad_g_pre_mlp', 'grad_g_post_mlp', 'grad_conv_w_in', 'grad_conv_b_in', 'grad_conv_dw', 'grad_conv_dw_b', 'grad_conv_ln_g', 'grad_conv_ln_b', 'grad_conv_w_out', 'grad_conv_b_out', 'grad_w_ada_kv', 'grad_b_ada_kv', 'grad_g_kv', 'grad_w_dkv', 'grad_g_ckv', 'grad_w_kr', 'grad_w_uk', 'grad_w_uv', 'grad_w_dq', 'grad_g_cq', 'grad_w_uq', 'grad_w_o', 'grad_mlp_w_up', 'grad_mlp_w_down', 'delta_w_ada_mix', 'delta_b_ada_mix', 'delta_w_ada_mlp', 'delta_b_ada_mlp', 'delta_g_pre_mix', 'delta_g_post_mix', 'delta_g_pre_mlp', 'delta_g_post_mlp', 'delta_conv_w_in', 'delta_conv_b_in', 'delta_conv_dw', 'delta_conv_dw_b', 'delta_conv_ln_g', 'delta_conv_ln_b', 'delta_conv_w_out', 'delta_conv_b_out', 'delta_w_ada_kv', 'delta_b_ada_kv', 'delta_g_kv', 'delta_w_dkv', 'delta_g_ckv', 'delta_w_kr', 'delta_w_uk', 'delta_w_uv', 'delta_w_dq', 'delta_g_cq', 'delta_w_uq', 'delta_w_o', 'delta_mlp_w_up', 'delta_mlp_w_down', 'new_m_w_ada_mix', 'new_m_b_ada_mix', 'new_m_w_ada_mlp', 'new_m_b_ada_mlp', 'new_m_g_pre_mix', 'new_m_g_post_mix', 'new_m_g_pre_mlp', 'new_m_g_post_mlp', 'new_m_conv_w_in', 'new_m_conv_b_in', 'new_m_conv_dw', 'new_m_conv_dw_b', 'new_m_conv_ln_g', 'new_m_conv_ln_b', 'new_m_conv_w_out', 'new_m_conv_b_out', 'new_m_w_ada_kv', 'new_m_b_ada_kv', 'new_m_g_kv', 'new_m_w_dkv', 'new_m_g_ckv', 'new_m_w_kr', 'new_m_w_uk', 'new_m_w_uv', 'new_m_w_dq', 'new_m_g_cq', 'new_m_w_uq', 'new_m_w_o', 'new_m_mlp_w_up', 'new_m_mlp_w_down', 'new_v_w_ada_mix', 'new_v_b_ada_mix', 'new_v_w_ada_mlp', 'new_v_b_ada_mlp', 'new_v_g_pre_mix', 'new_v_g_post_mix', 'new_v_g_pre_mlp', 'new_v_g_post_mlp', 'new_v_conv_w_in', 'new_v_conv_b_in', 'new_v_conv_dw', 'new_v_conv_dw_b', 'new_v_conv_ln_g', 'new_v_conv_ln_b', 'new_v_conv_w_out', 'new_v_conv_b_out', 'new_v_w_ada_kv', 'new_v_b_ada_kv', 'new_v_g_kv', 'new_v_w_dkv', 'new_v_g_ckv', 'new_v_w_kr', 'new_v_w_uk', 'new_v_w_uv', 'new_v_w_dq', 'new_v_g_cq', 'new_v_w_uq', 'new_v_w_o', 'new_v_mlp_w_up', 'new_v_mlp_w_down']
TWIN_LEAF_KINDS = {'loss': 'loss', 'grad_x': 'grad_x', 'grad_w_ada_mix': 'grad_w', 'grad_b_ada_mix': 'grad_w', 'grad_w_ada_mlp': 'grad_w', 'grad_b_ada_mlp': 'grad_w', 'grad_g_pre_mix': 'grad_w', 'grad_g_post_mix': 'grad_w', 'grad_g_pre_mlp': 'grad_w', 'grad_g_post_mlp': 'grad_w', 'grad_conv_w_in': 'grad_w', 'grad_conv_b_in': 'grad_w', 'grad_conv_dw': 'grad_w', 'grad_conv_dw_b': 'grad_w', 'grad_conv_ln_g': 'grad_w', 'grad_conv_ln_b': 'grad_w', 'grad_conv_w_out': 'grad_w', 'grad_conv_b_out': 'grad_w', 'grad_w_ada_kv': 'grad_w', 'grad_b_ada_kv': 'grad_w', 'grad_g_kv': 'grad_w', 'grad_w_dkv': 'grad_w', 'grad_g_ckv': 'grad_w', 'grad_w_kr': 'grad_w', 'grad_w_uk': 'grad_w', 'grad_w_uv': 'grad_w', 'grad_w_dq': 'grad_w', 'grad_g_cq': 'grad_w', 'grad_w_uq': 'grad_w', 'grad_w_o': 'grad_w', 'grad_mlp_w_up': 'grad_w', 'grad_mlp_w_down': 'grad_w', 'delta_w_ada_mix': 'delta_w', 'delta_b_ada_mix': 'delta_w', 'delta_w_ada_mlp': 'delta_w', 'delta_b_ada_mlp': 'delta_w', 'delta_g_pre_mix': 'delta_w', 'delta_g_post_mix': 'delta_w', 'delta_g_pre_mlp': 'delta_w', 'delta_g_post_mlp': 'delta_w', 'delta_conv_w_in': 'delta_w', 'delta_conv_b_in': 'delta_w', 'delta_conv_dw': 'delta_w', 'delta_conv_dw_b': 'delta_w', 'delta_conv_ln_g': 'delta_w', 'delta_conv_ln_b': 'delta_w', 'delta_conv_w_out': 'delta_w', 'delta_conv_b_out': 'delta_w', 'delta_w_ada_kv': 'delta_w', 'delta_b_ada_kv': 'delta_w', 'delta_g_kv': 'delta_w', 'delta_w_dkv': 'delta_w', 'delta_g_ckv': 'delta_w', 'delta_w_kr': 'delta_w', 'delta_w_uk': 'delta_w', 'delta_w_uv': 'delta_w', 'delta_w_dq': 'delta_w', 'delta_g_cq': 'delta_w', 'delta_w_uq': 'delta_w', 'delta_w_o': 'delta_w', 'delta_mlp_w_up': 'delta_w', 'delta_mlp_w_down': 'delta_w', 'new_m_w_ada_mix': 'new_m', 'new_m_b_ada_mix': 'new_m', 'new_m_w_ada_mlp': 'new_m', 'new_m_b_ada_mlp': 'new_m', 'new_m_g_pre_mix': 'new_m', 'new_m_g_post_mix': 'new_m', 'new_m_g_pre_mlp': 'new_m', 'new_m_g_post_mlp': 'new_m', 'new_m_conv_w_in': 'new_m', 'new_m_conv_b_in': 'new_m', 'new_m_conv_dw': 'new_m', 'new_m_conv_dw_b': 'new_m', 'new_m_conv_ln_g': 'new_m', 'new_m_conv_ln_b': 'new_m', 'new_m_conv_w_out': 'new_m', 'new_m_conv_b_out': 'new_m', 'new_m_w_ada_kv': 'new_m', 'new_m_b_ada_kv': 'new_m', 'new_m_g_kv': 'new_m', 'new_m_w_dkv': 'new_m', 'new_m_g_ckv': 'new_m', 'new_m_w_kr': 'new_m', 'new_m_w_uk': 'new_m', 'new_m_w_uv': 'new_m', 'new_m_w_dq': 'new_m', 'new_m_g_cq': 'new_m', 'new_m_w_uq': 'new_m', 'new_m_w_o': 'new_m', 'new_m_mlp_w_up': 'new_m', 'new_m_mlp_w_down': 'new_m', 'new_v_w_ada_mix': 'new_v', 'new_v_b_ada_mix': 'new_v', 'new_v_w_ada_mlp': 'new_v', 'new_v_b_ada_mlp': 'new_v', 'new_v_g_pre_mix': 'new_v', 'new_v_g_post_mix': 'new_v', 'new_v_g_pre_mlp': 'new_v', 'new_v_g_post_mlp': 'new_v', 'new_v_conv_w_in': 'new_v', 'new_v_conv_b_in': 'new_v', 'new_v_conv_dw': 'new_v', 'new_v_conv_dw_b': 'new_v', 'new_v_conv_ln_g': 'new_v', 'new_v_conv_ln_b': 'new_v', 'new_v_conv_w_out': 'new_v', 'new_v_conv_b_out': 'new_v', 'new_v_w_ada_kv': 'new_v', 'new_v_b_ada_kv': 'new_v', 'new_v_g_kv': 'new_v', 'new_v_w_dkv': 'new_v', 'new_v_g_ckv': 'new_v', 'new_v_w_kr': 'new_v', 'new_v_w_uk': 'new_v', 'new_v_w_uv': 'new_v', 'new_v_w_dq': 'new_v', 'new_v_g_cq': 'new_v', 'new_v_w_uq': 'new_v', 'new_v_w_o': 'new_v', 'new_v_mlp_w_up': 'new_v', 'new_v_mlp_w_down': 'new_v'}


def _forward(args):
    return _fwd_reference(*[args[k] for k in FWD_PARAMS])


def _output_shape():
    out = _jax.eval_shape(lambda: _forward(_fwd_setup_inputs(0)))
    return out.shape, out.dtype

N_MICROBATCH = 1
ADAM_LR = 0.001
ADAM_B1 = 0.9
ADAM_B2 = 0.999
ADAM_EPS = 1e-08
ADAM_WD = 0.01
ADAM_STEP = 10
PER_EXAMPLE_BATCH_AXIS = {'x': 0, 'c': 0, 'positions': 0, 'loss_target': 0}
SHARED_INPUTS = []
_WEIGHT_DTYPES = {'w_ada_mix': _jnp.float32, 'b_ada_mix': _jnp.float32, 'w_ada_mlp': _jnp.float32, 'b_ada_mlp': _jnp.float32, 'g_pre_mix': _jnp.float32, 'g_post_mix': _jnp.float32, 'g_pre_mlp': _jnp.float32, 'g_post_mlp': _jnp.float32, 'conv_w_in': _jnp.float32, 'conv_b_in': _jnp.float32, 'conv_dw': _jnp.float32, 'conv_dw_b': _jnp.float32, 'conv_ln_g': _jnp.float32, 'conv_ln_b': _jnp.float32, 'conv_w_out': _jnp.float32, 'conv_b_out': _jnp.float32, 'w_ada_kv': _jnp.float32, 'b_ada_kv': _jnp.float32, 'g_kv': _jnp.float32, 'w_dkv': _jnp.float32, 'g_ckv': _jnp.float32, 'w_kr': _jnp.float32, 'w_uk': _jnp.float32, 'w_uv': _jnp.float32, 'w_dq': _jnp.float32, 'g_cq': _jnp.float32, 'w_uq': _jnp.float32, 'w_o': _jnp.float32, 'mlp_w_up': _jnp.float32, 'mlp_w_down': _jnp.float32}
MOMENT_SCALE = {'w_ada_mix': 9.484515e-02, 'b_ada_mix': 1.755709e-01, 'w_ada_mlp': 8.530057e-02, 'b_ada_mlp': 1.671274e-01, 'g_pre_mix': 5.419876e-03, 'g_post_mix': 3.992366e-02, 'g_pre_mlp': 7.110158e-03, 'g_post_mlp': 3.629927e-02, 'conv_w_in': 4.367024e-03, 'conv_b_in': 9.579289e-03, 'conv_dw': 5.798544e-03, 'conv_dw_b': 2.490066e-02, 'conv_ln_g': 1.046325e-02, 'conv_ln_b': 1.422661e-02, 'conv_w_out': 7.252191e-03, 'conv_b_out': 2.837506e-02, 'w_ada_kv': 6.901857e-02, 'b_ada_kv': 1.154577e-01, 'g_kv': 7.252865e-03, 'w_dkv': 2.319620e-02, 'g_ckv': 2.206544e-02, 'w_kr': 1.510025e-02, 'w_uk': 3.949041e-03, 'w_uv': 1.223780e-02, 'w_dq': 9.532215e-03, 'g_cq': 9.446957e-03, 'w_uq': 3.884051e-03, 'w_o': 1.231124e-02, 'mlp_w_up': 3.463709e-03, 'mlp_w_down': 8.168258e-03}


def _to_microbatches(a, axis):
    t = _jnp.moveaxis(a, axis, 0)
    t = t.reshape((N_MICROBATCH, t.shape[0] // N_MICROBATCH) + t.shape[1:])
    return _jnp.moveaxis(t, 1, axis + 1)


def setup_inputs(seed: int = 0) -> dict:
    inp = _fwd_setup_inputs(seed)
    key = _jax.random.fold_in(_jax.random.key(seed), 7919)
    shape, _ = _output_shape()
    out = dict(inp)
    out["loss_target"] = _jax.random.normal(_jax.random.fold_in(key, 0), shape, _jnp.float32)
    for i, name in enumerate(TWIN_WEIGHTS):
        w = inp[name].astype(_jnp.float32)
        if MOMENT_SCALE is None:
            s = _jnp.sqrt(_jnp.mean(_jnp.square(w)) + 1e-30)
        else:
            s = MOMENT_SCALE[name]
        km, kv = _jax.random.split(_jax.random.fold_in(key, i + 1))
        out[name] = w
        out["m_" + name] = s * _jax.random.normal(km, w.shape, _jnp.float32)
        out["v_" + name] = (s * s) * _jax.random.uniform(kv, w.shape, _jnp.float32, 0.5, 1.5)
    if N_MICROBATCH > 1:
        for name, axis in PER_EXAMPLE_BATCH_AXIS.items():
            out[name] = _to_microbatches(out[name], axis)
    return {'x': out['x'], 'c': out['c'], 'positions': out['positions'], 'w_ada_mix': out['w_ada_mix'], 'b_ada_mix': out['b_ada_mix'], 'w_ada_mlp': out['w_ada_mlp'], 'b_ada_mlp': out['b_ada_mlp'], 'g_pre_mix': out['g_pre_mix'], 'g_post_mix': out['g_post_mix'], 'g_pre_mlp': out['g_pre_mlp'], 'g_post_mlp': out['g_post_mlp'], 'conv_w_in': out['conv_w_in'], 'conv_b_in': out['conv_b_in'], 'conv_dw': out['conv_dw'], 'conv_dw_b': out['conv_dw_b'], 'conv_ln_g': out['conv_ln_g'], 'conv_ln_b': out['conv_ln_b'], 'conv_w_out': out['conv_w_out'], 'conv_b_out': out['conv_b_out'], 'w_ada_kv': out['w_ada_kv'], 'b_ada_kv': out['b_ada_kv'], 'g_kv': out['g_kv'], 'w_dkv': out['w_dkv'], 'g_ckv': out['g_ckv'], 'w_kr': out['w_kr'], 'w_uk': out['w_uk'], 'w_uv': out['w_uv'], 'w_dq': out['w_dq'], 'g_cq': out['g_cq'], 'w_uq': out['w_uq'], 'w_o': out['w_o'], 'mlp_w_up': out['mlp_w_up'], 'mlp_w_down': out['mlp_w_down'], 'loss_target': out['loss_target'], 'm_w_ada_mix': out['m_w_ada_mix'], 'm_b_ada_mix': out['m_b_ada_mix'], 'm_w_ada_mlp': out['m_w_ada_mlp'], 'm_b_ada_mlp': out['m_b_ada_mlp'], 'm_g_pre_mix': out['m_g_pre_mix'], 'm_g_post_mix': out['m_g_post_mix'], 'm_g_pre_mlp': out['m_g_pre_mlp'], 'm_g_post_mlp': out['m_g_post_mlp'], 'm_conv_w_in': out['m_conv_w_in'], 'm_conv_b_in': out['m_conv_b_in'], 'm_conv_dw': out['m_conv_dw'], 'm_conv_dw_b': out['m_conv_dw_b'], 'm_conv_ln_g': out['m_conv_ln_g'], 'm_conv_ln_b': out['m_conv_ln_b'], 'm_conv_w_out': out['m_conv_w_out'], 'm_conv_b_out': out['m_conv_b_out'], 'm_w_ada_kv': out['m_w_ada_kv'], 'm_b_ada_kv': out['m_b_ada_kv'], 'm_g_kv': out['m_g_kv'], 'm_w_dkv': out['m_w_dkv'], 'm_g_ckv': out['m_g_ckv'], 'm_w_kr': out['m_w_kr'], 'm_w_uk': out['m_w_uk'], 'm_w_uv': out['m_w_uv'], 'm_w_dq': out['m_w_dq'], 'm_g_cq': out['m_g_cq'], 'm_w_uq': out['m_w_uq'], 'm_w_o': out['m_w_o'], 'm_mlp_w_up': out['m_mlp_w_up'], 'm_mlp_w_down': out['m_mlp_w_down'], 'v_w_ada_mix': out['v_w_ada_mix'], 'v_b_ada_mix': out['v_b_ada_mix'], 'v_w_ada_mlp': out['v_w_ada_mlp'], 'v_b_ada_mlp': out['v_b_ada_mlp'], 'v_g_pre_mix': out['v_g_pre_mix'], 'v_g_post_mix': out['v_g_post_mix'], 'v_g_pre_mlp': out['v_g_pre_mlp'], 'v_g_post_mlp': out['v_g_post_mlp'], 'v_conv_w_in': out['v_conv_w_in'], 'v_conv_b_in': out['v_conv_b_in'], 'v_conv_dw': out['v_conv_dw'], 'v_conv_dw_b': out['v_conv_dw_b'], 'v_conv_ln_g': out['v_conv_ln_g'], 'v_conv_ln_b': out['v_conv_ln_b'], 'v_conv_w_out': out['v_conv_w_out'], 'v_conv_b_out': out['v_conv_b_out'], 'v_w_ada_kv': out['v_w_ada_kv'], 'v_b_ada_kv': out['v_b_ada_kv'], 'v_g_kv': out['v_g_kv'], 'v_w_dkv': out['v_w_dkv'], 'v_g_ckv': out['v_g_ckv'], 'v_w_kr': out['v_w_kr'], 'v_w_uk': out['v_w_uk'], 'v_w_uv': out['v_w_uv'], 'v_w_dq': out['v_w_dq'], 'v_g_cq': out['v_g_cq'], 'v_w_uq': out['v_w_uq'], 'v_w_o': out['v_w_o'], 'v_mlp_w_up': out['v_mlp_w_up'], 'v_mlp_w_down': out['v_mlp_w_down']}


def _loss(weights, diff, rest, loss_target):
    with _jax.named_scope("forward"):
        args = {**rest, TWIN_DIFF_INPUT: diff, **{k: w.astype(_WEIGHT_DTYPES[k]) for k, w in weights.items()}}
        y = _forward(args)
    with _jax.named_scope("loss_head"):
        err = _jnp.square(y.astype(_jnp.float32) - loss_target)
        return 0.5 * _jnp.sum(_jnp.mean(err, axis=-1)) if err.ndim else 0.5 * err


def _adamw(w, g, m, v):
    m = ADAM_B1 * m + (1.0 - ADAM_B1) * g
    v = ADAM_B2 * v + (1.0 - ADAM_B2) * _jnp.square(g)
    m_hat = m / (1.0 - ADAM_B1 ** ADAM_STEP)
    v_hat = v / (1.0 - ADAM_B2 ** ADAM_STEP)
    delta = -ADAM_LR * (m_hat / (_jnp.sqrt(v_hat) + ADAM_EPS) + ADAM_WD * w)
    return delta, m, v


def reference(x, c, positions, w_ada_mix, b_ada_mix, w_ada_mlp, b_ada_mlp, g_pre_mix, g_post_mix, g_pre_mlp, g_post_mlp, conv_w_in, conv_b_in, conv_dw, conv_dw_b, conv_ln_g, conv_ln_b, conv_w_out, conv_b_out, w_ada_kv, b_ada_kv, g_kv, w_dkv, g_ckv, w_kr, w_uk, w_uv, w_dq, g_cq, w_uq, w_o, mlp_w_up, mlp_w_down, loss_target, m_w_ada_mix, m_b_ada_mix, m_w_ada_mlp, m_b_ada_mlp, m_g_pre_mix, m_g_post_mix, m_g_pre_mlp, m_g_post_mlp, m_conv_w_in, m_conv_b_in, m_conv_dw, m_conv_dw_b, m_conv_ln_g, m_conv_ln_b, m_conv_w_out, m_conv_b_out, m_w_ada_kv, m_b_ada_kv, m_g_kv, m_w_dkv, m_g_ckv, m_w_kr, m_w_uk, m_w_uv, m_w_dq, m_g_cq, m_w_uq, m_w_o, m_mlp_w_up, m_mlp_w_down, v_w_ada_mix, v_b_ada_mix, v_w_ada_mlp, v_b_ada_mlp, v_g_pre_mix, v_g_post_mix, v_g_pre_mlp, v_g_post_mlp, v_conv_w_in, v_conv_b_in, v_conv_dw, v_conv_dw_b, v_conv_ln_g, v_conv_ln_b, v_conv_w_out, v_conv_b_out, v_w_ada_kv, v_b_ada_kv, v_g_kv, v_w_dkv, v_g_ckv, v_w_kr, v_w_uk, v_w_uv, v_w_dq, v_g_cq, v_w_uq, v_w_o, v_mlp_w_up, v_mlp_w_down):
    given = dict(x=x, c=c, positions=positions, w_ada_mix=w_ada_mix, b_ada_mix=b_ada_mix, w_ada_mlp=w_ada_mlp, b_ada_mlp=b_ada_mlp, g_pre_mix=g_pre_mix, g_post_mix=g_post_mix, g_pre_mlp=g_pre_mlp, g_post_mlp=g_post_mlp, conv_w_in=conv_w_in, conv_b_in=conv_b_in, conv_dw=conv_dw, conv_dw_b=conv_dw_b, conv_ln_g=conv_ln_g, conv_ln_b=conv_ln_b, conv_w_out=conv_w_out, conv_b_out=conv_b_out, w_ada_kv=w_ada_kv, b_ada_kv=b_ada_kv, g_kv=g_kv, w_dkv=w_dkv, g_ckv=g_ckv, w_kr=w_kr, w_uk=w_uk, w_uv=w_uv, w_dq=w_dq, g_cq=g_cq, w_uq=w_uq, w_o=w_o, mlp_w_up=mlp_w_up, mlp_w_down=mlp_w_down, loss_target=loss_target, m_w_ada_mix=m_w_ada_mix, m_b_ada_mix=m_b_ada_mix, m_w_ada_mlp=m_w_ada_mlp, m_b_ada_mlp=m_b_ada_mlp, m_g_pre_mix=m_g_pre_mix, m_g_post_mix=m_g_post_mix, m_g_pre_mlp=m_g_pre_mlp, m_g_post_mlp=m_g_post_mlp, m_conv_w_in=m_conv_w_in, m_conv_b_in=m_conv_b_in, m_conv_dw=m_conv_dw, m_conv_dw_b=m_conv_dw_b, m_conv_ln_g=m_conv_ln_g, m_conv_ln_b=m_conv_ln_b, m_conv_w_out=m_conv_w_out, m_conv_b_out=m_conv_b_out, m_w_ada_kv=m_w_ada_kv, m_b_ada_kv=m_b_ada_kv, m_g_kv=m_g_kv, m_w_dkv=m_w_dkv, m_g_ckv=m_g_ckv, m_w_kr=m_w_kr, m_w_uk=m_w_uk, m_w_uv=m_w_uv, m_w_dq=m_w_dq, m_g_cq=m_g_cq, m_w_uq=m_w_uq, m_w_o=m_w_o, m_mlp_w_up=m_mlp_w_up, m_mlp_w_down=m_mlp_w_down, v_w_ada_mix=v_w_ada_mix, v_b_ada_mix=v_b_ada_mix, v_w_ada_mlp=v_w_ada_mlp, v_b_ada_mlp=v_b_ada_mlp, v_g_pre_mix=v_g_pre_mix, v_g_post_mix=v_g_post_mix, v_g_pre_mlp=v_g_pre_mlp, v_g_post_mlp=v_g_post_mlp, v_conv_w_in=v_conv_w_in, v_conv_b_in=v_conv_b_in, v_conv_dw=v_conv_dw, v_conv_dw_b=v_conv_dw_b, v_conv_ln_g=v_conv_ln_g, v_conv_ln_b=v_conv_ln_b, v_conv_w_out=v_conv_w_out, v_conv_b_out=v_conv_b_out, v_w_ada_kv=v_w_ada_kv, v_b_ada_kv=v_b_ada_kv, v_g_kv=v_g_kv, v_w_dkv=v_w_dkv, v_g_ckv=v_g_ckv, v_w_kr=v_w_kr, v_w_uk=v_w_uk, v_w_uv=v_w_uv, v_w_dq=v_w_dq, v_g_cq=v_g_cq, v_w_uq=v_w_uq, v_w_o=v_w_o, v_mlp_w_up=v_mlp_w_up, v_mlp_w_down=v_mlp_w_down)
    weights = {n: given[n] for n in TWIN_WEIGHTS}
    shared = {n: given[n] for n in SHARED_INPUTS}
    per_example = {n: given[n] for n in ['x', 'c', 'positions']}
    grad_fn = _jax.value_and_grad(_loss, argnums=(0, 1))

    def one_microbatch(ex, loss_target):
        ex = dict(ex)
        diff = ex.pop(TWIN_DIFF_INPUT)
        return grad_fn(weights, diff, {**shared, **ex}, loss_target)

    if N_MICROBATCH == 1:
        loss, (grad_w, grad_x) = one_microbatch(per_example, given["loss_target"])
    else:
        def body(carry, xs):
            loss_sum, grad_sum = carry
            l_k, (gw_k, gx_k) = one_microbatch(xs[0], xs[1])
            with _jax.named_scope("update"):
                return (loss_sum + l_k, _jax.tree.map(_jnp.add, grad_sum, gw_k)), gx_k

        init = (_jnp.zeros((), _jnp.float32), _jax.tree.map(_jnp.zeros_like, weights))
        (loss, grad_w), grad_x = _jax.lax.scan(body, init, (per_example, given["loss_target"]))
    with _jax.named_scope("update"):
        delta_w, new_m, new_v = {}, {}, {}
        for n in TWIN_WEIGHTS:
            delta_w[n], new_m[n], new_v[n] = _adamw(weights[n], grad_w[n], given["m_" + n], given["v_" + n])
    return (loss, grad_x, *[grad_w[n] for n in TWIN_WEIGHTS], *[delta_w[n] for n in TWIN_WEIGHTS],
            *[new_m[n] for n in TWIN_WEIGHTS], *[new_v[n] for n in TWIN_WEIGHTS])
```

```python
import functools

import jax
import jax.numpy as jnp
from jax import lax
from jax.experimental import pallas as pl
from jax.experimental.pallas import tpu as pltpu

F32 = jnp.float32
BF = jnp.bfloat16
MXU_DTYPE = BF

EPS = 1e-6
NEG = -1e30
ROPE_THETA = 10000.0
QK_NOPE = 128
QK_ROPE = 64
V_HEAD = 128
CONV_WIDTH = 31
ADAM_LR, ADAM_B1, ADAM_B2, ADAM_EPS, ADAM_WD, ADAM_STEP = 0.001, 0.9, 0.999, 1e-08, 0.01, 10

N_CHIPS = 4
N_DEV = 8
LANES = 128
VMEM_LIMIT = 56 * 2 ** 20
MESH = pl.DeviceIdType.MESH
ANY = pl.BlockSpec(memory_space=pl.ANY)


def _params(sem=None):
    return pltpu.CompilerParams(dimension_semantics=sem, vmem_limit_bytes=VMEM_LIMIT)


def _tile(n, pref, unit=LANES):
    if n <= pref:
        return n
    t = (pref // unit) * unit
    while t > unit and n % t:
        t -= unit
    assert n % t == 0, (n, pref)
    return t


def _dg(a, b, ca, cb):
    return lax.dot_general(a.astype(MXU_DTYPE), b.astype(MXU_DTYPE), (((ca,), (cb,)), ((), ())),
                           preferred_element_type=F32)


@jax.custom_vjp
def _dot_nn(a, b):
    return _dg(a, b, 1, 0)


def _dot_nn_fwd(a, b):
    return _dg(a, b, 1, 0), (a, b)


def _dot_nn_bwd(res, g):
    a, b = res
    return _dg(g, b, 1, 1).astype(a.dtype), _dg(a, g, 0, 0).astype(b.dtype)


_dot_nn.defvjp(_dot_nn_fwd, _dot_nn_bwd)


@jax.custom_vjp
def _dot_nt(a, b):
    return _dg(a, b, 1, 1)


def _dot_nt_fwd(a, b):
    return _dg(a, b, 1, 1), (a, b)


def _dot_nt_bwd(res, g):
    a, b = res
    return _dg(g, b, 1, 0).astype(a.dtype), _dg(g, a, 0, 0).astype(b.dtype)


_dot_nt.defvjp(_dot_nt_fwd, _dot_nt_bwd)


def _matmul(name, a, b, *, ca, cb, grid, a_blk, a_map, b_blk, b_map, o_shape, o_dtypes, o_blk, o_map,
            extras=(), epilogue=None):
    nk = grid[-1]
    n_ex, n_out = len(extras), len(o_dtypes)

    def body(*refs):
        a_ref, b_ref = refs[0], refs[1]
        ex_refs = refs[2:2 + n_ex]
        out_refs = refs[2 + n_ex:2 + n_ex + n_out]
        kk = pl.program_id(len(grid) - 1)

        def finish(acc):
            outs = (acc,) if epilogue is None else epilogue(acc, *[r[...] for r in ex_refs])
            for r, o in zip(out_refs, outs):
                r[...] = o.astype(r.dtype)

        part = _dg(a_ref[...], b_ref[...], ca, cb)
        if nk == 1:
            finish(part)
        else:
            acc_ref = refs[-1]

            @pl.when(kk == 0)
            def _():
                acc_ref[...] = part

            @pl.when(kk > 0)
            def _():
                acc_ref[...] += part

            @pl.when(kk == nk - 1)
            def _():
                finish(acc_ref[...])

    o_spec = pl.BlockSpec(o_blk, o_map)
    acc_shape = tuple(d for d in o_blk if d is not None)
    res = pl.pallas_call(
        body, grid=grid,
        in_specs=[pl.BlockSpec(a_blk, a_map), pl.BlockSpec(b_blk, b_map)] + [o_spec] * n_ex,
        out_specs=[o_spec] * n_out,
        out_shape=[jax.ShapeDtypeStruct(o_shape, dt) for dt in o_dtypes],
        scratch_shapes=[] if nk == 1 else [pltpu.VMEM(acc_shape, F32)],
        compiler_params=_params(("parallel",) * (len(grid) - 1) + ("arbitrary",)),
        name=name,
    )(a, b, *extras)
    return res[0] if n_out == 1 else res


TM, TN, TK = 1024, 512, 1024


def _mm_nn(name, a, w, layout, o_dtypes=(F32,), extras=(), epilogue=None):
    m, k = a.shape
    tm, tk = _tile(m, TM, 8), _tile(k, TK)
    if layout == "row":
        n = w.shape[1]
        tn = _tile(n, TN)
        grid = (m // tm, n // tn, k // tk)
        b_blk, b_map = (tk, tn), (lambda i, j, kk: (kk, j))
    else:
        nl = w.shape[2]
        n = nl * w.shape[0]
        tn = _tile(nl, TN)
        nb = nl // tn
        grid = (m // tm, n // tn, k // tk)
        b_blk, b_map = (None, tk, tn), (lambda i, j, kk: (j // nb, kk, j % nb))
    return _matmul(name, a, w, ca=1, cb=0, grid=grid, a_blk=(tm, tk), a_map=lambda i, j, kk: (i, kk),
                   b_blk=b_blk, b_map=b_map, o_shape=(m, n), o_dtypes=o_dtypes, o_blk=(tm, tn),
                   o_map=lambda i, j, kk: (i, j), extras=extras, epilogue=epilogue)


def _mm_nt(name, g, w, layout, o_dtypes=(F32,), extras=(), epilogue=None):
    m, n = g.shape
    tm, tn = _tile(m, TM, 8), None
    if layout == "row":
        k = w.shape[0]
        tn = _tile(n, TK)
        tk = _tile(k, TN)
        grid = (m // tm, k // tk, n // tn)
        b_blk, b_map = (tk, tn), (lambda i, j, kk: (j, kk))
    else:
        k, nl = w.shape[1], w.shape[2]
        tn = _tile(nl, TK)
        nb = nl // tn
        tk = _tile(k, TN)
        grid = (m // tm, k // tk, n // tn)
        b_blk, b_map = (None, tk, tn), (lambda i, j, kk: (kk // nb, j, kk % nb))
    return _matmul(name, g, w, ca=1, cb=1, grid=grid, a_blk=(tm, tn), a_map=lambda i, j, kk: (i, kk),
                   b_blk=b_blk, b_map=b_map, o_shape=(m, k), o_dtypes=o_dtypes, o_blk=(tm, tk),
                   o_map=lambda i, j, kk: (i, j), extras=extras, epilogue=epilogue)


def _mm_tn(name, x, g, layout, o_dtype=F32):
    t, k = x.shape
    n = g.shape[1]
    tt = _tile(t, TK, 8)
    tk = _tile(k, TM)
    if layout == "row":
        tn = _tile(n, TN)
        o_shape, o_blk, o_map = (k, n), (tk, tn), (lambda i, j, kk: (i, j))
    else:
        nl = n // N_CHIPS
        tn = _tile(nl, TN)
        nb = nl // tn
        o_shape, o_blk, o_map = (N_CHIPS, k, nl), (None, tk, tn), (lambda i, j, kk: (j // nb, i, j % nb))
    grid = (k // tk, n // tn, t // tt)
    return _matmul(name, x, g, ca=0, cb=0, grid=grid, a_blk=(tt, tk), a_map=lambda i, j, kk: (kk, i),
                   b_blk=(tt, tn), b_map=lambda i, j, kk: (kk, j), o_shape=o_shape, o_dtypes=(o_dtype,),
                   o_blk=o_blk, o_map=o_map)


TR = 256


def _rw_fwd(name, fn, rows, vecs, o_dtypes, tr=TR):
    t = rows[0].shape[0]
    tr = min(tr, t)
    n_r, n_v = len(rows), len(vecs)
    o_sds = jax.eval_shape(fn, *[jax.ShapeDtypeStruct((tr, r.shape[1]), r.dtype) for r in rows],
                           *[jax.ShapeDtypeStruct(v.shape, v.dtype) for v in vecs])

    def body(*refs):
        outs = fn(*[r[...] for r in refs[:n_r + n_v]])
        for r, o in zip(refs[n_r + n_v:], outs):
            r[...] = o.astype(r.dtype)

    res = pl.pallas_call(
        body, grid=(t // tr,),
        in_specs=[pl.BlockSpec((tr, r.shape[1]), lambda i: (i, 0)) for r in rows]
        + [pl.BlockSpec(v.shape, lambda i: (0, 0)) for v in vecs],
        out_specs=[pl.BlockSpec((tr, o.shape[1]), lambda i: (i, 0)) for o in o_sds],
        out_shape=[jax.ShapeDtypeStruct((t, o.shape[1]), dt) for o, dt in zip(o_sds, o_dtypes)],
        compiler_params=_params(("parallel",)), name=name,
    )(*rows, *vecs)
    return res


def _rw_bwd(name, fn, rows, vecs, cots, row_grad_dtypes, add_rows=None, tr=TR):
    t = rows[0].shape[0]
    tr = min(tr, t)
    add_rows = add_rows or {}
    n_r, n_v, n_c = len(rows), len(vecs), len(cots)
    diff = [i for i, dt in enumerate(row_grad_dtypes) if dt is not None]
    adds = [add_rows[i] for i in diff if i in add_rows]
    n_a = len(adds)

    def body(*refs):
        rvals = [r[...] for r in refs[:n_r]]
        vvals = [r[...] for r in refs[n_r:n_r + n_v]]
        c_refs = refs[n_r + n_v:n_r + n_v + n_c]
        a_refs = list(refs[n_r + n_v + n_c:n_r + n_v + n_c + n_a])
        o_refs = refs[n_r + n_v + n_c + n_a:]

        def f(*d):
            full = list(rvals)
            for i, x in zip(diff, d[:len(diff)]):
                full[i] = x
            return fn(*full, *d[len(diff):])

        outs, vjp = jax.vjp(f, *[rvals[i] for i in diff], *vvals)
        grads = vjp(tuple(c[...].astype(o.dtype) for c, o in zip(c_refs, outs)))
        for n, i in enumerate(diff):
            gr = grads[n].astype(F32)
            if i in add_rows:
                gr = gr + a_refs.pop(0)[...].astype(F32)
            o_refs[n][...] = gr.astype(o_refs[n].dtype)
        first = pl.program_id(0) == 0
        for r, gv in zip(o_refs[len(diff):], grads[len(diff):]):
            @pl.when(first)
            def _(r=r, gv=gv):
                r[...] = gv

            @pl.when(jnp.logical_not(first))
            def _(r=r, gv=gv):
                r[...] += gv

    row_spec = lambda a: pl.BlockSpec((tr, a.shape[1]), lambda i: (i, 0))
    vec_spec = lambda a: pl.BlockSpec(a.shape, lambda i: (0, 0))
    res = pl.pallas_call(
        body, grid=(t // tr,),
        in_specs=[row_spec(r) for r in rows] + [vec_spec(v) for v in vecs] + [row_spec(c) for c in cots]
        + [row_spec(a) for a in adds],
        out_specs=[row_spec(rows[i]) for i in diff] + [vec_spec(v) for v in vecs],
        out_shape=[jax.ShapeDtypeStruct(rows[i].shape, row_grad_dtypes[i]) for i in diff]
        + [jax.ShapeDtypeStruct(v.shape, F32) for v in vecs],
        compiler_params=_params(("arbitrary",)), name=name,
    )(*rows, *vecs, *cots, *adds)
    return res[:len(diff)], res[len(diff):]


def _rms(x, g):
    return x * lax.rsqrt(jnp.mean(x * x, axis=-1, keepdims=True) + EPS) * g


def _f_pre(h, g, shift, scale):
    return (_rms(h, g) * (1.0 + scale) + shift,)


def _f_pre2(h, g1, sh1, sc1, g2, sh2, sc2):
    return _f_pre(h, g1, sh1, sc1) + _f_pre(h, g2, sh2, sc2)


def _f_post(y, gate, g):
    return (gate * _rms(y, g),)


def _f_post_bias(y, bias, gate, g):
    return (gate * _rms(y + bias, g),)


def _f_glu(z, bias):
    z = z + bias
    half = z.shape[1] // 2
    return (z[:, :half] * jax.nn.sigmoid(z[:, half:]),)


def _f_ln_silu(u, g, b):
    mu = jnp.mean(u, axis=-1, keepdims=True)
    var = jnp.mean(jnp.square(u - mu), axis=-1, keepdims=True)
    y = (u - mu) * lax.rsqrt(var + EPS) * g + b
    return (y * jax.nn.sigmoid(y),)


def _rope_raw(x, cos, sin):
    n = x.shape[1]
    reps = n // LANES
    if reps > 1:
        cos = jnp.concatenate([cos] * reps, axis=1)
        sin = jnp.concatenate([sin] * reps, axis=1)
    lane = lax.broadcasted_iota(jnp.int32, x.shape, 1)
    half = QK_ROPE // 2
    partner = jnp.where((lane % QK_ROPE) < half, pltpu.roll(x, n - half, 1), pltpu.roll(x, half, 1))
    return x * cos + partner * sin


@jax.custom_vjp
def _rope(x, cos, sin):
    return _rope_raw(x, cos, sin)


def _rope_fwd(x, cos, sin):
    return _rope_raw(x, cos, sin), (cos, sin)


def _rope_bwd(res, g):
    cos, sin = res
    return _rope_raw(g, cos, -sin), jnp.zeros_like(cos), jnp.zeros_like(sin)


_rope.defvjp(_rope_fwd, _rope_bwd)


def _make_f_lat(r_kv):
    def f(pre_kv, pre_q, cos, sin, g_ckv, g_cq):
        c_kv = _rms(pre_kv[:, :r_kv], g_ckv)
        kr = _rope(pre_kv[:, r_kv:], cos, sin)
        return c_kv, kr, _rms(pre_q, g_cq)
    return f


def _f_rope(x, cos, sin):
    return (_rope(x, cos, sin),)


CONV_TC = 128
CONV_TT = 256
PADR = 32


def _conv_fwd(u, dw, dw_b):
    t, d = u.shape
    tc, tt = min(CONV_TC, d), min(CONV_TT, t)
    off = PADR - (CONV_WIDTH - 1)

    def body(u_ref, w_ref, b_ref, o_ref, pad_ref):
        pad_ref[pl.ds(0, PADR), :] = jnp.zeros((PADR, tc), F32)
        pad_ref[pl.ds(PADR, t), :] = u_ref[...]
        for t0 in range(0, t, tt):
            acc = jnp.zeros((tt, tc), F32) + b_ref[...]
            for j in range(CONV_WIDTH):
                acc = acc + pad_ref[pl.ds(t0 + off + j, tt), :] * w_ref[pl.ds(j, 1), :]
            o_ref[pl.ds(t0, tt), :] = acc

    return pl.pallas_call(
        body, grid=(d // tc,),
        in_specs=[pl.BlockSpec((t, tc), lambda j: (0, j)), pl.BlockSpec((CONV_WIDTH, tc), lambda j: (0, j)),
                  pl.BlockSpec((1, tc), lambda j: (0, j))],
        out_specs=pl.BlockSpec((t, tc), lambda j: (0, j)),
        out_shape=jax.ShapeDtypeStruct((t, d), F32),
        scratch_shapes=[pltpu.VMEM((t + PADR, tc), F32)],
        compiler_params=_params(("parallel",)), name="conv_fwd",
    )(u, dw, dw_b)


def _conv_bwd(u, duc, dw):
    t, d = u.shape
    tc, tt = min(CONV_TC, d), min(CONV_TT, t)
    off = PADR - (CONV_WIDTH - 1)

    def body(u_ref, g_ref, w_ref, du_ref, dwt_ref, padu_ref, padg_ref):
        padu_ref[pl.ds(0, PADR), :] = jnp.zeros((PADR, tc), F32)
        padu_ref[pl.ds(PADR, t), :] = u_ref[...]
        padg_ref[pl.ds(t, PADR), :] = jnp.zeros((PADR, tc), F32)
        padg_ref[pl.ds(0, t), :] = g_ref[...]
        for t0 in range(0, t, tt):
            acc = jnp.zeros((tt, tc), F32)
            for j in range(CONV_WIDTH):
                acc = acc + padg_ref[pl.ds(t0 + (CONV_WIDTH - 1) - j, tt), :] * w_ref[pl.ds(j, 1), :]
            du_ref[pl.ds(t0, tt), :] = acc
        for j in range(CONV_WIDTH):
            acc = jnp.zeros((tt, tc), F32)
            for t0 in range(0, t, tt):
                acc = acc + g_ref[pl.ds(t0, tt), :] * padu_ref[pl.ds(t0 + off + j, tt), :]
            dwt_ref[pl.ds(j, 1), :] = jnp.sum(acc, axis=0, keepdims=True)
        acc = jnp.zeros((tt, tc), F32)
        for t0 in range(0, t, tt):
            acc = acc + g_ref[pl.ds(t0, tt), :]
        dwt_ref[pl.ds(CONV_WIDTH, 1), :] = jnp.sum(acc, axis=0, keepdims=True)

    col = lambda r: pl.BlockSpec((r, tc), lambda j: (0, j))
    return pl.pallas_call(
        body, grid=(d // tc,),
        in_specs=[col(t), col(t), col(CONV_WIDTH)],
        out_specs=[col(t), col(CONV_WIDTH + 1)],
        out_shape=[jax.ShapeDtypeStruct((t, d), F32), jax.ShapeDtypeStruct((CONV_WIDTH + 1, d), F32)],
        scratch_shapes=[pltpu.VMEM((t + PADR, tc), F32), pltpu.VMEM((t + PADR, tc), F32)],
        compiler_params=_params(("parallel",)), name="conv_bwd",
    )(u, duc, dw)


@jax.custom_vjp
def _swap_halves(x):
    return pltpu.roll(x, LANES // 2, 1)


_swap_halves.defvjp(lambda x: (pltpu.roll(x, LANES // 2, 1), None), lambda _, g: (pltpu.roll(g, LANES // 2, 1),))


def _attn_block(qn, qr, kn, kr, v, q0):
    scale = (QK_NOPE + QK_ROPE) ** -0.5
    lane = lax.broadcasted_iota(jnp.int32, kr.shape, 1)
    kr_a = kr * (lane < QK_ROPE).astype(kr.dtype)
    kr_b = _swap_halves(kr_a)
    outs = []
    for hh, kr_h in ((0, kr_a), (1, kr_b)):
        sl = slice(hh * QK_NOPE, (hh + 1) * QK_NOPE)
        s = (_dot_nt(qn[:, sl], kn[:, sl]) + _dot_nt(qr, kr_h)) * scale
        row = lax.broadcasted_iota(jnp.int32, s.shape, 0) + q0
        col = lax.broadcasted_iota(jnp.int32, s.shape, 1)
        s = jnp.where(col <= row, s, NEG)
        e = jnp.exp(s - jnp.max(s, axis=-1, keepdims=True))
        p = e / jnp.sum(e, axis=-1, keepdims=True)
        outs.append(_dot_nn(p, v[:, sl]))
    return jnp.concatenate(outs, axis=1)


def _attn_fwd(qn, qr, kn, kr, v, tq=512):
    t, w = qn.shape
    pairs = w // (2 * QK_NOPE)
    tq = min(tq, t)
    pw = 2 * QK_NOPE

    def body(qn_ref, qr_ref, kn_ref, kr_ref, v_ref, o_ref):
        for q0 in range(0, t, tq):
            l = q0 + tq
            o_ref[pl.ds(q0, tq), :] = _attn_block(
                qn_ref[pl.ds(q0, tq), :], qr_ref[pl.ds(q0, tq), :], kn_ref[pl.ds(0, l), :], kr_ref[pl.ds(0, l), :],
                v_ref[pl.ds(0, l), :], q0).astype(o_ref.dtype)

    pair = lambda wd: pl.BlockSpec((t, wd), lambda p: (0, p))
    return pl.pallas_call(
        body, grid=(pairs,),
        in_specs=[pair(pw), pair(LANES), pair(pw), pl.BlockSpec((t, LANES), lambda p: (0, 0)), pair(pw)],
        out_specs=pair(pw), out_shape=jax.ShapeDtypeStruct((t, w), BF),
        compiler_params=_params(("parallel",)), name="attn_fwd",
    )(qn, qr, kn, kr, v)


def _attn_bwd(qn, qr, kn, kr, v, do, tq=256):
    t, w = qn.shape
    pairs = w // (2 * QK_NOPE)
    tq = min(tq, t)
    pw = 2 * QK_NOPE

    def body(qn_ref, qr_ref, kn_ref, kr_ref, v_ref, do_ref, dqn_ref, dqr_ref, dkn_ref, dv_ref, dkr_ref,
             akn_ref, av_ref, akr_ref):
        akn_ref[...] = jnp.zeros_like(akn_ref)
        av_ref[...] = jnp.zeros_like(av_ref)
        akr_ref[...] = jnp.zeros_like(akr_ref)
        for q0 in range(0, t, tq):
            l = q0 + tq
            rows, keys = pl.ds(q0, tq), pl.ds(0, l)
            _, vjp = jax.vjp(functools.partial(_attn_block, q0=q0), qn_ref[rows, :], qr_ref[rows, :],
                             kn_ref[keys, :], kr_ref[keys, :], v_ref[keys, :])
            dqn, dqr, dkn, dkr, dv = vjp(do_ref[rows, :].astype(F32))
            dqn_ref[rows, :] = dqn.astype(dqn_ref.dtype)
            dqr_ref[rows, :] = dqr.astype(dqr_ref.dtype)
            akn_ref[keys, :] += dkn.astype(F32)
            av_ref[keys, :] += dv.astype(F32)
            akr_ref[keys, :] += dkr.astype(F32)
        dkn_ref[...] = akn_ref[...].astype(dkn_ref.dtype)
        dv_ref[...] = av_ref[...].astype(dv_ref.dtype)
        first = pl.program_id(0) == 0

        @pl.when(first)
        def _():
            dkr_ref[...] = akr_ref[...]

        @pl.when(jnp.logical_not(first))
        def _():
            dkr_ref[...] += akr_ref[...]

    pair = lambda wd: pl.BlockSpec((t, wd), lambda p: (0, p))
    shared = pl.BlockSpec((t, LANES), lambda p: (0, 0))
    sds = jax.ShapeDtypeStruct
    return pl.pallas_call(
        body, grid=(pairs,),
        in_specs=[pair(pw), pair(LANES), pair(pw), shared, pair(pw), pair(pw)],
        out_specs=[pair(pw), pair(LANES), pair(pw), pair(pw), shared],
        out_shape=[sds((t, w), BF), sds((t, pairs * LANES), F32), sds((t, w), BF), sds((t, w), BF), sds((t, LANES), F32)],
        scratch_shapes=[pltpu.VMEM((t, pw), F32), pltpu.VMEM((t, pw), F32), pltpu.VMEM((t, LANES), F32)],
        compiler_params=_params(("arbitrary",)), name="attn_bwd",
    )(qn, qr, kn, kr, v, do)


def _ew(name, fn, ins, o_dtypes, max_bytes=2 ** 21):
    r, c = ins[0].shape
    tr = r
    if r * c * 4 > max_bytes:
        tr = max(16, (max_bytes // (c * 4)) // 16 * 16)
        while r % tr:
            tr -= 16
    n_in = len(ins)

    def body(*refs):
        outs = fn(*[x[...] for x in refs[:n_in]])
        for o_ref, o in zip(refs[n_in:], outs):
            o_ref[...] = o.astype(o_ref.dtype)

    spec = pl.BlockSpec((tr, c), lambda i: (i, 0))
    return pl.pallas_call(
        body, grid=(r // tr,), in_specs=[spec] * n_in, out_specs=[spec] * len(o_dtypes),
        out_shape=[jax.ShapeDtypeStruct((r, c), dt) for dt in o_dtypes],
        compiler_params=_params(("parallel",)), name=name,
    )(*ins)


def _adamw_math(w, g, m, v):
    m = ADAM_B1 * m + (1.0 - ADAM_B1) * g
    v = ADAM_B2 * v + (1.0 - ADAM_B2) * jnp.square(g)
    m_hat = m / (1.0 - ADAM_B1 ** ADAM_STEP)
    v_hat = v / (1.0 - ADAM_B2 ** ADAM_STEP)
    delta = -ADAM_LR * (m_hat / (jnp.sqrt(v_hat) + ADAM_EPS) + ADAM_WD * w)
    return delta, m, v


def _adamw(name, w, g, m, v):
    shape = w.shape
    to2 = lambda a: a.reshape(-1, shape[-1]) if a.ndim > 1 else a.reshape(1, -1)
    d, nm, nv = _ew("adamw_" + name, _adamw_math, [to2(w), to2(g), to2(m), to2(v)], (F32, F32, F32))
    return d.reshape(shape), nm.reshape(shape), nv.reshape(shape)


def _rope_tables(positions, t):
    half = QK_ROPE // 2
    inv = 1.0 / (ROPE_THETA ** (jnp.arange(0, QK_ROPE, 2, dtype=F32) / QK_ROPE))
    inv_l = jnp.tile(inv, LANES // half).reshape(1, LANES)
    sign = jnp.tile(jnp.concatenate([-jnp.ones((half,), F32), jnp.ones((half,), F32)]), LANES // QK_ROPE).reshape(1, LANES)

    def body(p_ref, inv_ref, sg_ref, c_ref, s_ref):
        ang = p_ref[...].astype(F32) * inv_ref[...]
        c_ref[...] = jnp.cos(ang)
        s_ref[...] = jnp.sin(ang) * sg_ref[...]

    return pl.pallas_call(body, out_shape=[jax.ShapeDtypeStruct((t, LANES), F32)] * 2, name="rope_tables")(
        positions.reshape(t, 1), inv_l, sign)


def _loss_grad(h, target):
    t, d = h.shape
    tr = min(TR, t)

    def body(h_ref, y_ref, dh_ref, l_ref):
        err = h_ref[...] - y_ref[...]
        dh_ref[...] = err * (1.0 / d)
        part = 0.5 * jnp.sum(jnp.mean(jnp.square(err), axis=-1, keepdims=True), axis=0, keepdims=True)
        first = pl.program_id(0) == 0

        @pl.when(first)
        def _():
            l_ref[...] = part

        @pl.when(jnp.logical_not(first))
        def _():
            l_ref[...] += part

    row = pl.BlockSpec((tr, d), lambda i: (i, 0))
    return pl.pallas_call(
        body, grid=(t // tr,), in_specs=[row, row],
        out_specs=[row, pl.BlockSpec((1, 1), lambda i: (0, 0))],
        out_shape=[jax.ShapeDtypeStruct((t, d), F32), jax.ShapeDtypeStruct((1, 1), F32)],
        compiler_params=_params(("arbitrary",)), name="loss_grad",
    )(h, target)


def _sum_devices(g):
    def fn(*parts):
        acc = parts[0]
        for p in parts[1:]:
            acc = acc + p
        return (acc,)
    return _ew("sum_devices", fn, [g[i] for i in range(g.shape[0])], (F32,))[0]


def _place():
    x, y, c = lax.axis_index("x"), lax.axis_index("y"), lax.axis_index("c")
    return x, y, c, [(1 - x, y), (x, 1 - y), (1 - x, 1 - y)]


def _all_gather_small(name, v):
    r, n = v.shape

    def body(x_ref, out_ref, send_sems, recv_sems, local_sem):
        x, y, c, chips = _place()
        me, sibling = (x, y, c), (x, y, 1 - c)

        def rows(px, py, pc):
            return out_ref.at[4 * px + 2 * py + pc]

        def copy(k, block, to, src=None):
            return pltpu.make_async_remote_copy(
                src_ref=rows(*block) if src is None else src, dst_ref=rows(*block), send_sem=send_sems.at[k],
                recv_sem=recv_sems.at[k], device_id=to, device_id_type=MESH)

        mine = pltpu.make_async_copy(x_ref, rows(*me), local_sem)
        mine.start()
        first = [copy(0, me, sibling, src=x_ref)]
        first += [copy(1 + j, me, (*chip, c), src=x_ref) for j, chip in enumerate(chips)]
        for cp in first:
            cp.start()
        passed = [copy(4 + j, (*chip, c), sibling) for j, chip in enumerate(chips)]
        for j, chip in enumerate(chips):
            copy(1 + j, (*chip, c), me).wait_recv()
            passed[j].start()
        copy(0, sibling, me).wait_recv()
        for j, chip in enumerate(chips):
            copy(4 + j, (*chip, 1 - c), me).wait_recv()
        for cp in first + passed:
            cp.wait_send()
        mine.wait()

    return pl.pallas_call(
        body, out_shape=jax.ShapeDtypeStruct((N_DEV, r, n), v.dtype),
        in_specs=[pl.BlockSpec(memory_space=pltpu.VMEM)], out_specs=pl.BlockSpec(memory_space=pltpu.VMEM),
        scratch_shapes=[pltpu.SemaphoreType.DMA((7,)), pltpu.SemaphoreType.DMA((7,)), pltpu.SemaphoreType.DMA],
        compiler_params=pltpu.CompilerParams(vmem_limit_bytes=VMEM_LIMIT), name=name,
    )(v)


def _gather_weights(shards):
    n_w = len(shards)

    def body(*refs):
        ins, outs = refs[:n_w], refs[n_w:2 * n_w]
        send_sems, recv_sems, local_sems = refs[2 * n_w:]
        x, y, c, chips = _place()
        k = 2 * x + y
        local = []
        for i in range(n_w):
            cp = pltpu.make_async_copy(ins[i], outs[i].at[k], local_sems.at[i])
            cp.start()
            local.append(cp)

        def half(i, which):
            h = ins[i].shape[0] // 2
            return pl.ds(which * h, h)

        def copy(i, s, chip, which, to, src=None):
            dst = outs[i].at[2 * chip[0] + chip[1], half(i, which)]
            return pltpu.make_async_remote_copy(
                src_ref=dst if src is None else src, dst_ref=dst, send_sem=send_sems.at[6 * i + s],
                recv_sem=recv_sems.at[6 * i + s], device_id=to, device_id_type=MESH)

        sends = []
        for i in range(n_w):
            for j, chip in enumerate(chips):
                cp = copy(i, j, (x, y), c, (*chip, c), src=ins[i].at[half(i, c)])
                cp.start()
                sends.append(cp)
        for i in range(n_w):
            for j, chip in enumerate(chips):
                copy(i, j, chip, c, (x, y, c)).wait_recv()
                cp = copy(i, 3 + j, chip, c, (x, y, 1 - c))
                cp.start()
                sends.append(cp)
        for i in range(n_w):
            for j, chip in enumerate(chips):
                copy(i, 3 + j, chip, 1 - c, (x, y, c)).wait_recv()
        for cp in sends:
            cp.wait_send()
        for cp in local:
            cp.wait()

    return pl.pallas_call(
        body, in_specs=[ANY] * n_w, out_specs=[ANY] * n_w,
        out_shape=[jax.ShapeDtypeStruct((N_CHIPS,) + s.shape, s.dtype) for s in shards],
        scratch_shapes=[pltpu.SemaphoreType.DMA((6 * n_w,)), pltpu.SemaphoreType.DMA((6 * n_w,)),
                        pltpu.SemaphoreType.DMA((n_w,))],
        name="gather_weights",
    )(*shards)


def _swap_other_half(grads):
    n_w = len(grads)

    def body(*refs):
        ins, outs = refs[:n_w], refs[n_w:2 * n_w]
        send_sems, recv_sems = refs[2 * n_w:]
        x, y, c, _ = _place()
        cps = []
        for i in range(n_w):
            h = ins[i].shape[1] // 2
            cp = pltpu.make_async_remote_copy(
                src_ref=ins[i].at[:, pl.ds((1 - c) * h, h), :], dst_ref=outs[i], send_sem=send_sems.at[i],
                recv_sem=recv_sems.at[i], device_id=(x, y, 1 - c), device_id_type=MESH)
            cp.start()
            cps.append(cp)
        for cp in cps:
            cp.wait()

    return pl.pallas_call(
        body, in_specs=[ANY] * n_w, out_specs=[ANY] * n_w,
        out_shape=[jax.ShapeDtypeStruct((g.shape[0], g.shape[1] // 2, g.shape[2]), g.dtype) for g in grads],
        scratch_shapes=[pltpu.SemaphoreType.DMA((n_w,)), pltpu.SemaphoreType.DMA((n_w,))],
        name="swap_other_half",
    )(*grads)


def _add_my_half(name, g, s, c_idx, o_dtype):
    nc, r, n = g.shape
    h = r // 2
    tr = h
    while tr * n * 4 > 2 ** 21 and tr % 32 == 0:
        tr //= 2
    nb = h // tr

    def body(c_ref, g_ref, s_ref, o_ref):
        o_ref[...] = (g_ref[...] + s_ref[...].astype(F32)).astype(o_ref.dtype)

    return pl.pallas_call(
        body,
        grid_spec=pltpu.PrefetchScalarGridSpec(
            num_scalar_prefetch=1, grid=(nc, nb),
            in_specs=[pl.BlockSpec((None, tr, n), lambda j, i, c_ref: (j, c_ref[0] * nb + i, 0)),
                      pl.BlockSpec((None, tr, n), lambda j, i, c_ref: (j, i, 0))],
            out_specs=pl.BlockSpec((None, tr, n), lambda j, i, c_ref: (j, i, 0))),
        out_shape=jax.ShapeDtypeStruct((nc, h, n), o_dtype),
        compiler_params=_params(("parallel", "parallel")), name=name,
    )(c_idx, g, s)


def _scatter_partials(parts):
    n_w = len(parts)

    def body(*refs):
        ins, outs = refs[:n_w], refs[n_w:2 * n_w]
        send_sems, recv_sems, local_sems = refs[2 * n_w:]
        x, y, c, chips = _place()
        k = 2 * x + y
        local, sends = [], []
        for i in range(n_w):
            cp = pltpu.make_async_copy(ins[i].at[k], outs[i].at[k], local_sems.at[i])
            cp.start()
            local.append(cp)
        for i in range(n_w):
            for j, chip in enumerate(chips):
                cp = pltpu.make_async_remote_copy(
                    src_ref=ins[i].at[2 * chip[0] + chip[1]], dst_ref=outs[i].at[k], send_sem=send_sems.at[3 * i + j],
                    recv_sem=recv_sems.at[3 * i + j], device_id=(*chip, c), device_id_type=MESH)
                cp.start()
                sends.append(cp)
        for i in range(n_w):
            for j, chip in enumerate(chips):
                blk = outs[i].at[2 * chip[0] + chip[1]]
                pltpu.make_async_remote_copy(
                    src_ref=blk, dst_ref=blk, send_sem=send_sems.at[3 * i + j], recv_sem=recv_sems.at[3 * i + j],
                    device_id=(*chip, c), device_id_type=MESH).wait_recv()
        for cp in sends:
            cp.wait_send()
        for cp in local:
            cp.wait()

    return pl.pallas_call(
        body, in_specs=[ANY] * n_w, out_specs=[ANY] * n_w,
        out_shape=[jax.ShapeDtypeStruct(p.shape, p.dtype) for p in parts],
        scratch_shapes=[pltpu.SemaphoreType.DMA((3 * n_w,)), pltpu.SemaphoreType.DMA((3 * n_w,)),
                        pltpu.SemaphoreType.DMA((n_w,))],
        name="scatter_partials",
    )(*parts)


def _sum_chips(name, q):
    nc, h, n = q.shape
    tr = h
    while tr * n * 4 > 2 ** 20 and tr % 32 == 0:
        tr //= 2

    def body(q_ref, o_ref):
        acc = q_ref[0].astype(F32)
        for j in range(1, nc):
            acc = acc + q_ref[j].astype(F32)
        o_ref[...] = acc

    return pl.pallas_call(
        body, grid=(h // tr,), in_specs=[pl.BlockSpec((nc, tr, n), lambda i: (0, i, 0))],
        out_specs=pl.BlockSpec((tr, n), lambda i: (i, 0)), out_shape=jax.ShapeDtypeStruct((h, n), F32),
        compiler_params=_params(("parallel",)), name=name,
    )(q)


def _join_halves(halves):
    n_w = len(halves)

    def body(*refs):
        ins, outs = refs[:n_w], refs[n_w:2 * n_w]
        send_sems, recv_sems, local_sems = refs[2 * n_w:]
        x, y, c, _ = _place()
        cps, local = [], []
        for i in range(n_w):
            h = ins[i].shape[0]
            mine = outs[i].at[pl.ds(c * h, h)]
            lc = pltpu.make_async_copy(ins[i], mine, local_sems.at[i])
            lc.start()
            local.append(lc)
            cp = pltpu.make_async_remote_copy(
                src_ref=ins[i], dst_ref=mine, send_sem=send_sems.at[i], recv_sem=recv_sems.at[i],
                device_id=(x, y, 1 - c), device_id_type=MESH)
            cp.start()
            cps.append(cp)
        for i, cp in enumerate(cps):
            cp.wait_send()
            h = ins[i].shape[0]
            theirs = outs[i].at[pl.ds((1 - c) * h, h)]
            pltpu.make_async_remote_copy(
                src_ref=theirs, dst_ref=theirs, send_sem=send_sems.at[i], recv_sem=recv_sems.at[i],
                device_id=(x, y, 1 - c), device_id_type=MESH).wait_recv()
        for lc in local:
            lc.wait()

    return pl.pallas_call(
        body, in_specs=[ANY] * n_w, out_specs=[ANY] * n_w,
        out_shape=[jax.ShapeDtypeStruct((2 * s.shape[0], s.shape[1]), s.dtype) for s in halves],
        scratch_shapes=[pltpu.SemaphoreType.DMA((n_w,)), pltpu.SemaphoreType.DMA((n_w,)),
                        pltpu.SemaphoreType.DMA((n_w,))],
        name="join_halves",
    )(*halves)


def _reduce_scatter(grads, c_idx):
    from_sibling = _swap_other_half(grads)
    parts = [_add_my_half("add_my_half_%d" % i, g, s, c_idx, BF) for i, (g, s) in enumerate(zip(grads, from_sibling))]
    received = _scatter_partials(parts)
    halves = [_sum_chips("sum_chips_%d" % i, q) for i, q in enumerate(received)]
    return _join_halves(halves)


def _cast(name, w):
    return _ew("cast_" + name, lambda a: (a,), [w], (BF,))[0]


def _relu2_epilogue(acc):
    r = jnp.maximum(acc, 0.0)
    return acc, r * r


def _relu2_bwd_epilogue(acc, a):
    return (acc * (2.0 * jnp.maximum(a, 0.0)),)


def _add_epilogue(acc, other):
    return (acc + other,)


def kernel(x, c, positions, w_ada_mix, b_ada_mix, w_ada_mlp, b_ada_mlp, g_pre_mix, g_post_mix, g_pre_mlp, g_post_mlp, conv_w_in, conv_b_in, conv_dw, conv_dw_b, conv_ln_g, conv_ln_b, conv_w_out, conv_b_out, w_ada_kv, b_ada_kv, g_kv, w_dkv, g_ckv, w_kr, w_uk, w_uv, w_dq, g_cq, w_uq, w_o, mlp_w_up, mlp_w_down, loss_target, m_w_ada_mix, m_b_ada_mix, m_w_ada_mlp, m_b_ada_mlp, m_g_pre_mix, m_g_post_mix, m_g_pre_mlp, m_g_post_mlp, m_conv_w_in, m_conv_b_in, m_conv_dw, m_conv_dw_b, m_conv_ln_g, m_conv_ln_b, m_conv_w_out, m_conv_b_out, m_w_ada_kv, m_b_ada_kv, m_g_kv, m_w_dkv, m_g_ckv, m_w_kr, m_w_uk, m_w_uv, m_w_dq, m_g_cq, m_w_uq, m_w_o, m_mlp_w_up, m_mlp_w_down, v_w_ada_mix, v_b_ada_mix, v_w_ada_mlp, v_b_ada_mlp, v_g_pre_mix, v_g_post_mix, v_g_pre_mlp, v_g_post_mlp, v_conv_w_in, v_conv_b_in, v_conv_dw, v_conv_dw_b, v_conv_ln_g, v_conv_ln_b, v_conv_w_out, v_conv_b_out, v_w_ada_kv, v_b_ada_kv, v_g_kv, v_w_dkv, v_g_ckv, v_w_kr, v_w_uk, v_w_uv, v_w_dq, v_g_cq, v_w_uq, v_w_o, v_mlp_w_up, v_mlp_w_down):
    weights = dict(w_ada_mix=w_ada_mix, b_ada_mix=b_ada_mix, w_ada_mlp=w_ada_mlp, b_ada_mlp=b_ada_mlp, g_pre_mix=g_pre_mix, g_post_mix=g_post_mix, g_pre_mlp=g_pre_mlp, g_post_mlp=g_post_mlp, conv_w_in=conv_w_in, conv_b_in=conv_b_in, conv_dw=conv_dw, conv_dw_b=conv_dw_b, conv_ln_g=conv_ln_g, conv_ln_b=conv_ln_b, conv_w_out=conv_w_out, conv_b_out=conv_b_out, w_ada_kv=w_ada_kv, b_ada_kv=b_ada_kv, g_kv=g_kv, w_dkv=w_dkv, g_ckv=g_ckv, w_kr=w_kr, w_uk=w_uk, w_uv=w_uv, w_dq=w_dq, g_cq=g_cq, w_uq=w_uq, w_o=w_o, mlp_w_up=mlp_w_up, mlp_w_down=mlp_w_down)
    m_in = dict(w_ada_mix=m_w_ada_mix, b_ada_mix=m_b_ada_mix, w_ada_mlp=m_w_ada_mlp, b_ada_mlp=m_b_ada_mlp, g_pre_mix=m_g_pre_mix, g_post_mix=m_g_post_mix, g_pre_mlp=m_g_pre_mlp, g_post_mlp=m_g_post_mlp, conv_w_in=m_conv_w_in, conv_b_in=m_conv_b_in, conv_dw=m_conv_dw, conv_dw_b=m_conv_dw_b, conv_ln_g=m_conv_ln_g, conv_ln_b=m_conv_ln_b, conv_w_out=m_conv_w_out, conv_b_out=m_conv_b_out, w_ada_kv=m_w_ada_kv, b_ada_kv=m_b_ada_kv, g_kv=m_g_kv, w_dkv=m_w_dkv, g_ckv=m_g_ckv, w_kr=m_w_kr, w_uk=m_w_uk, w_uv=m_w_uv, w_dq=m_w_dq, g_cq=m_g_cq, w_uq=m_w_uq, w_o=m_w_o, mlp_w_up=m_mlp_w_up, mlp_w_down=m_mlp_w_down)
    v_in = dict(w_ada_mix=v_w_ada_mix, b_ada_mix=v_b_ada_mix, w_ada_mlp=v_w_ada_mlp, b_ada_mlp=v_b_ada_mlp, g_pre_mix=v_g_pre_mix, g_post_mix=v_g_post_mix, g_pre_mlp=v_g_pre_mlp, g_post_mlp=v_g_post_mlp, conv_w_in=v_conv_w_in, conv_b_in=v_conv_b_in, conv_dw=v_conv_dw, conv_dw_b=v_conv_dw_b, conv_ln_g=v_conv_ln_g, conv_ln_b=v_conv_ln_b, conv_w_out=v_conv_w_out, conv_b_out=v_conv_b_out, w_ada_kv=v_w_ada_kv, b_ada_kv=v_b_ada_kv, g_kv=v_g_kv, w_dkv=v_w_dkv, g_ckv=v_g_ckv, w_kr=v_w_kr, w_uk=v_w_uk, w_uv=v_w_uv, w_dq=v_w_dq, g_cq=v_g_cq, w_uq=v_w_uq, w_o=v_w_o, mlp_w_up=v_mlp_w_up, mlp_w_down=v_mlp_w_down)
    grads = _step_grads(x, c, positions, loss_target, weights)
    loss, grad_x = grads.pop("loss"), grads.pop("x")
    names = list(weights)
    upd = {n: _adamw(n, weights[n], grads[n], m_in[n], v_in[n]) for n in names}
    return (loss, grad_x, *[grads[n] for n in names], *[upd[n][0] for n in names], *[upd[n][1] for n in names],
            *[upd[n][2] for n in names])


def _step_grads(x, c, positions, loss_target, w):
    xi, yi, ci = lax.axis_index("x"), lax.axis_index("y"), lax.axis_index("c")
    chip = 2 * xi + yi
    dev = 2 * chip + ci
    c_idx = jnp.reshape(ci, (1,)).astype(jnp.int32)
    t, d = x.shape[1], x.shape[2]
    dl = d // N_CHIPS
    r_kv, r_q = w["w_dkv"].shape[1], w["w_dq"].shape[2]
    n_ada, n_kvada = w["w_ada_mix"].shape[2], w["w_ada_kv"].shape[1]
    heads_l = w["w_uq"].shape[2] // (QK_NOPE + QK_ROPE)
    assert dl == r_kv == r_q and dl % LANES == 0 and heads_l * N_CHIPS % 2 == 0

    def chip_cols(a, width):
        return lax.dynamic_slice_in_dim(a, chip * width, width, axis=a.ndim - 1)

    pack_rows = [w["conv_dw"][0], w["conv_dw_b"], w["conv_ln_g"], w["conv_ln_b"], w["conv_b_out"],
                 w["conv_b_in"].reshape(2, dl), c.reshape(N_CHIPS, dl)]
    n_pack = sum(p.shape[0] for p in pack_rows)
    pad_rows = (-n_pack) % 8
    pack = jnp.concatenate(pack_rows + [jnp.zeros((pad_rows, dl), F32)], axis=0)
    packs = _all_gather_small("gather_params", pack)
    by_chip = packs[0::2]

    def full_width(r0, nr):
        return jnp.transpose(by_chip[:, r0:r0 + nr, :], (1, 0, 2)).reshape(nr, d)

    cw = CONV_WIDTH
    dw_full, dwb_full = full_width(0, cw), full_width(cw, 1)
    lng_full, lnb_full, bout_full = full_width(cw + 1, 1), full_width(cw + 2, 1), full_width(cw + 3, 1)
    bin_full = jnp.transpose(by_chip[:, cw + 4:cw + 6, :], (0, 1, 2)).reshape(1, 2 * d)
    c_all = packs[:, cw + 6:cw + 10, :].reshape(N_DEV, d)

    sc_all = _ew("silu_c", lambda a: (a * jax.nn.sigmoid(a),), [c_all], (F32,))[0]
    ada_w = [w["w_ada_mix"][0], w["w_ada_mix"][1], w["w_ada_mlp"][0], w["w_ada_mlp"][1], w["w_ada_kv"]]
    ada_b = [chip_cols(w["b_ada_mix"][0:1], n_ada), chip_cols(w["b_ada_mix"][1:2], n_ada),
             chip_cols(w["b_ada_mlp"][0:1], n_ada), chip_cols(w["b_ada_mlp"][1:2], n_ada),
             chip_cols(w["b_ada_kv"].reshape(1, -1), n_kvada)]
    mods = [_mm_nn("ada_fwd_%d" % i, sc_all, wi, "row", extras=(jnp.broadcast_to(bi, (N_DEV, bi.shape[1])),),
                   epilogue=_add_epilogue) for i, (wi, bi) in enumerate(zip(ada_w, ada_b))]
    mods_all = _all_gather_small("gather_mods", jnp.concatenate(mods, axis=1))
    mine = lax.dynamic_index_in_dim(mods_all[0::2], dev, axis=1, keepdims=False)
    offs = [0]
    for m_ in mods:
        offs.append(offs[-1] + m_.shape[1])
    mod_vec = [mine[:, offs[i]:offs[i + 1]].reshape(1, -1) for i in range(5)]
    split3 = lambda v: (v[:, :d], v[:, d:2 * d], v[:, 2 * d:])
    mix = [split3(mod_vec[0]), split3(mod_vec[1])]
    mlp = [split3(mod_vec[2]), split3(mod_vec[3])]
    kv_shift, kv_scale = mod_vec[4][:, :d], mod_vec[4][:, d:]

    uq = w["w_uq"][0].reshape(r_q, heads_l, QK_NOPE + QK_ROPE)
    uq_nope, uq_rope = uq[:, :, :QK_NOPE].reshape(r_q, -1), uq[:, :, QK_NOPE:].reshape(r_q, -1)
    kr_pad = jnp.pad(w["w_kr"], ((0, 0), (0, LANES - QK_ROPE)))
    small = jnp.concatenate([w["w_dkv"], kr_pad, w["w_dq"][0], w["w_uk"], w["w_uv"], uq_nope, uq_rope], axis=1)
    widths = [r_kv, LANES, r_q, w["w_uk"].shape[1], w["w_uv"].shape[1], uq_nope.shape[1], uq_rope.shape[1]]
    so = [0]
    for wd in widths:
        so.append(so[-1] + wd)
    big = dict(conv_w_in=w["conv_w_in"][0], conv_w_out=w["conv_w_out"][0], w_o=w["w_o"][0],
               up0=w["mlp_w_up"][0], up1=w["mlp_w_up"][1], down0=w["mlp_w_down"][0], down1=w["mlp_w_down"][1],
               small=small)
    big_names = list(big)
    gathered = dict(zip(big_names, _gather_weights([_cast(n, big[n]) for n in big_names])))
    rowl = lambda a: a.reshape(-1, a.shape[2])
    gs = gathered["small"]
    w_dkvkr = rowl(gs[:, :, so[0]:so[2]])
    w_dq_f = rowl(gs[:, :, so[2]:so[3]])
    w_uk_f, w_uv_f = gs[:, :, so[3]:so[4]], gs[:, :, so[4]:so[5]]
    w_qn_f, w_qr_f = gs[:, :, so[5]:so[6]], gs[:, :, so[6]:so[7]]
    w_in_f, w_out_f, w_o_f = gathered["conv_w_in"], rowl(gathered["conv_w_out"]), rowl(gathered["w_o"])
    w_up_f = [gathered["up0"], gathered["up1"]]
    w_down_f = [rowl(gathered["down0"]), rowl(gathered["down1"])]

    cos_t, sin_t = _rope_tables(positions, t)
    vec = lambda a: a.reshape(1, -1)
    gpm, gqm = w["g_pre_mix"], w["g_post_mix"]
    gpl, gql = w["g_pre_mlp"], w["g_post_mlp"]
    h0 = x[0]

    def mlp_fwd(l, h):
        sh, sc, _ = mlp[l]
        (hn,) = _rw_fwd("mlp_pre_%d" % l, _f_pre, [h], [vec(gpl[l]), sh, sc], (BF,))
        a, s = _mm_nn("mlp_up_%d" % l, hn, w_up_f[l], "col", o_dtypes=(F32, BF), epilogue=_relu2_epilogue)
        y = _mm_nn("mlp_down_%d" % l, s, w_down_f[l], "row")
        return hn, a, s, y

    def post_fwd(name, h, y, gate, g, bias=None):
        if bias is None:
            return _rw_fwd(name, lambda h_, y_, gt, g_: (h_ + _f_post(y_, gt, g_)[0],), [h, y], [gate, g], (F32,))[0]
        return _rw_fwd(name, lambda h_, y_, b_, gt, g_: (h_ + _f_post_bias(y_, b_, gt, g_)[0],), [h, y],
                       [bias, gate, g], (F32,))[0]

    (hn0,) = _rw_fwd("conv_pre", _f_pre, [h0], [vec(gpm[0]), mix[0][0], mix[0][1]], (BF,))
    z0 = _mm_nn("conv_in", hn0, w_in_f, "col")
    (u0,) = _rw_fwd("conv_glu", _f_glu, [z0], [bin_full], (F32,))
    uc0 = _conv_fwd(u0, dw_full, dwb_full)
    (n0,) = _rw_fwd("conv_ln", _f_ln_silu, [uc0], [lng_full, lnb_full], (BF,))
    y0 = _mm_nn("conv_out", n0, w_out_f, "row")
    h1 = post_fwd("conv_post", h0, y0, mix[0][2], vec(gqm[0]), bias=bout_full)
    hn1, a1, s1, y1 = mlp_fwd(0, h1)
    h2 = post_fwd("mlp_post_0", h1, y1, mlp[0][2], vec(gql[0]))

    kvn, hn2 = _rw_fwd("mla_pre", _f_pre2, [h2], [vec(w["g_kv"]), kv_shift, kv_scale, vec(gpm[1]), mix[1][0], mix[1][1]],
                       (BF, BF))
    pre_kv = _mm_nn("mla_dkv", kvn, w_dkvkr, "row")
    pre_q = _mm_nn("mla_dq", hn2, w_dq_f, "row")
    f_lat = _make_f_lat(r_kv)
    c_kv, kr, c_q = _rw_fwd("mla_latent", f_lat, [pre_kv, pre_q, cos_t, sin_t], [vec(w["g_ckv"]), vec(w["g_cq"][0])],
                            (BF, BF, BF))
    kn = _mm_nn("mla_uk", c_kv, w_uk_f, "col", o_dtypes=(BF,))
    vv = _mm_nn("mla_uv", c_kv, w_uv_f, "col", o_dtypes=(BF,))
    qn = _mm_nn("mla_uq_nope", c_q, w_qn_f, "col", o_dtypes=(BF,))
    qr_pre = _mm_nn("mla_uq_rope", c_q, w_qr_f, "col")
    (qr,) = _rw_fwd("mla_q_rope", _f_rope, [qr_pre, cos_t, sin_t], [], (BF,))
    att = _attn_fwd(qn, qr, kn, kr, vv)
    y2 = _mm_nn("mla_o", att, w_o_f, "row")
    h3 = post_fwd("mla_post", h2, y2, mix[1][2], vec(gqm[1]))
    hn3, a3, s3, y3 = mlp_fwd(1, h3)
    h4 = post_fwd("mlp_post_1", h3, y3, mlp[1][2], vec(gql[1]))

    dh4, loss_part = _loss_grad(h4, loss_target[0])
    loss = lax.psum(loss_part[0, 0], ("x", "y", "c"))

    gw = {}
    gvec = {}

    def mlp_bwd(l, h_in, hn, a, s, y, dh):
        sh, sc, gate = mlp[l]
        (dy,), (dgate, dgq) = _rw_bwd("mlp_post_bwd_%d" % l, _f_post, [y], [gate, vec(gql[l])], [dh], [BF])
        gw["down%d" % l] = _mm_tn("mlp_down_dw_%d" % l, s, dy, "row")
        da = _mm_nt("mlp_down_dx_%d" % l, dy, w_down_f[l], "row", o_dtypes=(BF,), extras=(a,),
                    epilogue=_relu2_bwd_epilogue)
        gw["up%d" % l] = _mm_tn("mlp_up_dw_%d" % l, hn, da, "col")
        dhn = _mm_nt("mlp_up_dx_%d" % l, da, w_up_f[l], "col")
        (dh_in,), (dgp, dsh, dsc) = _rw_bwd("mlp_pre_bwd_%d" % l, _f_pre, [h_in], [vec(gpl[l]), sh, sc], [dhn], [F32],
                                            add_rows={0: dh})
        gvec["mlp%d" % l] = (dsh, dsc, dgate)
        gvec["g_pre_mlp%d" % l], gvec["g_post_mlp%d" % l] = dgp, dgq
        return dh_in

    dh3 = mlp_bwd(1, h3, hn3, a3, s3, y3, dh4)

    (dy2,), (dgate, dgq) = _rw_bwd("mla_post_bwd", _f_post, [y2], [mix[1][2], vec(gqm[1])], [dh3], [BF])
    gvec["g_post_mix1"] = dgq
    gw["w_o"] = _mm_tn("mla_o_dw", att, dy2, "row")
    datt = _mm_nt("mla_o_dx", dy2, w_o_f, "row", o_dtypes=(BF,))
    dqn, dqr, dkn, dvv, dkr = _attn_bwd(qn, qr, kn, kr, vv, datt)
    (dqr_pre,), _ = _rw_bwd("mla_q_rope_bwd", _f_rope, [qr_pre, cos_t, sin_t], [], [dqr], [BF, None, None])
    g_qn = _mm_tn("mla_uq_nope_dw", c_q, dqn, "col")
    g_qr = _mm_tn("mla_uq_rope_dw", c_q, dqr_pre, "col")
    dc_q = _mm_nt("mla_uq_nope_dx", dqn, w_qn_f, "col")
    dc_q = _mm_nt("mla_uq_rope_dx", dqr_pre, w_qr_f, "col", extras=(dc_q,), epilogue=_add_epilogue)
    g_uk = _mm_tn("mla_uk_dw", c_kv, dkn, "col")
    g_uv = _mm_tn("mla_uv_dw", c_kv, dvv, "col")
    dc_kv = _mm_nt("mla_uk_dx", dkn, w_uk_f, "col")
    dc_kv = _mm_nt("mla_uv_dx", dvv, w_uv_f, "col", extras=(dc_kv,), epilogue=_add_epilogue)
    (dpre_kv, dpre_q), (dg_ckv, dg_cq) = _rw_bwd(
        "mla_latent_bwd", f_lat, [pre_kv, pre_q, cos_t, sin_t], [vec(w["g_ckv"]), vec(w["g_cq"][0])],
        [dc_kv, dkr, dc_q], [BF, BF, None, None])
    gvec["g_ckv"], gvec["g_cq"] = dg_ckv, dg_cq
    g_dkvkr = _mm_tn("mla_dkv_dw", kvn, dpre_kv, "row")
    g_dq = _mm_tn("mla_dq_dw", hn2, dpre_q, "row")
    dkvn = _mm_nt("mla_dkv_dx", dpre_kv, w_dkvkr, "row")
    dhn2 = _mm_nt("mla_dq_dx", dpre_q, w_dq_f, "row")
    (dh2,), (dg_kv, dkvsh, dkvsc, dgp, dsh, dsc) = _rw_bwd(
        "mla_pre_bwd", _f_pre2, [h2], [vec(w["g_kv"]), kv_shift, kv_scale, vec(gpm[1]), mix[1][0], mix[1][1]],
        [dkvn, dhn2], [F32], add_rows={0: dh3})
    gvec["mix1"] = (dsh, dsc, dgate)
    gvec["kv"] = (dkvsh, dkvsc)
    gvec["g_kv"], gvec["g_pre_mix1"] = dg_kv, dgp
    chunked = lambda a: a.reshape(N_CHIPS, a.shape[0] // N_CHIPS, a.shape[1])
    gw["small"] = jnp.concatenate([chunked(g_dkvkr), chunked(g_dq), g_uk, g_uv, g_qn, g_qr], axis=2)

    dh1 = mlp_bwd(0, h1, hn1, a1, s1, y1, dh2)

    (dy0,), (dbout, dgate, dgq) = _rw_bwd("conv_post_bwd", _f_post_bias, [y0], [bout_full, mix[0][2], vec(gqm[0])],
                                          [dh1], [BF])
    gvec["g_post_mix0"] = dgq
    gw["conv_w_out"] = _mm_tn("conv_out_dw", n0, dy0, "row")
    dn0 = _mm_nt("conv_out_dx", dy0, w_out_f, "row")
    (duc0,), (dlng, dlnb) = _rw_bwd("conv_ln_bwd", _f_ln_silu, [uc0], [lng_full, lnb_full], [dn0], [F32])
    du0, ddw = _conv_bwd(u0, duc0, dw_full)
    (dz0,), (dbin,) = _rw_bwd("conv_glu_bwd", _f_glu, [z0], [bin_full], [du0], [BF])
    gw["conv_w_in"] = _mm_tn("conv_in_dw", hn0, dz0, "col")
    dhn0 = _mm_nt("conv_in_dx", dz0, w_in_f, "col")
    (dx,), (dgp, dsh, dsc) = _rw_bwd("conv_pre_bwd", _f_pre, [h0], [vec(gpm[0]), mix[0][0], mix[0][1]], [dhn0], [F32],
                                     add_rows={0: dh1})
    gvec["mix0"] = (dsh, dsc, dgate)
    gvec["g_pre_mix0"] = dgp

    vec_list = [*gvec["mix0"], *gvec["mix1"], *gvec["mlp0"], *gvec["mlp1"], *gvec["kv"],
                gvec["g_pre_mix0"], gvec["g_pre_mix1"], gvec["g_post_mix0"], gvec["g_post_mix1"],
                gvec["g_pre_mlp0"], gvec["g_pre_mlp1"], gvec["g_post_mlp0"], gvec["g_post_mlp1"],
                gvec["g_kv"], gvec["g_ckv"], gvec["g_cq"], dbin, dlng, dlnb, dbout, ddw.reshape(1, -1)]
    vo = [0]
    for v_ in vec_list:
        vo.append(vo[-1] + v_.shape[1])
    vec_pad = (-vo[-1]) % (8 * LANES)
    n_vec = vo[-1] + vec_pad
    flat = jnp.concatenate(vec_list + [jnp.zeros((1, vec_pad), F32)], axis=1).reshape(8, n_vec // 8)
    all_vecs = _all_gather_small("gather_vector_grads", flat).reshape(N_DEV, 8, n_vec // 8)
    summed = _sum_devices(all_vecs).reshape(1, n_vec)
    per_dev = all_vecs.reshape(N_DEV, n_vec)
    seg = lambda a, i: a[:, vo[i]:vo[i + 1]]

    out = {"loss": loss, "x": dx.reshape(x.shape)}
    dm_mix = [jnp.concatenate([seg(per_dev, 3 * l + i) for i in range(3)], axis=1) for l in range(2)]
    dm_mlp = [jnp.concatenate([seg(per_dev, 6 + 3 * l + i) for i in range(3)], axis=1) for l in range(2)]
    dm_kv = jnp.concatenate([seg(per_dev, 12), seg(per_dev, 13)], axis=1)
    ada_dw = lambda name, dm, width: _mm_tn(name, sc_all, chip_cols(dm, width), "row")
    out["w_ada_mix"] = jnp.stack([ada_dw("ada_mix_dw_%d" % l, dm_mix[l], n_ada) for l in range(2)])
    out["w_ada_mlp"] = jnp.stack([ada_dw("ada_mlp_dw_%d" % l, dm_mlp[l], n_ada) for l in range(2)])
    out["w_ada_kv"] = ada_dw("ada_kv_dw", dm_kv, n_kvada)
    sum_seg = lambda i: seg(summed, i)
    out["b_ada_mix"] = jnp.concatenate([jnp.concatenate([sum_seg(3 * l + i) for i in range(3)], axis=1) for l in range(2)], axis=0)
    out["b_ada_mlp"] = jnp.concatenate([jnp.concatenate([sum_seg(6 + 3 * l + i) for i in range(3)], axis=1) for l in range(2)], axis=0)
    out["b_ada_kv"] = jnp.concatenate([sum_seg(12), sum_seg(13)], axis=1).reshape(-1)
    out["g_pre_mix"] = jnp.concatenate([sum_seg(14), sum_seg(15)], axis=0)
    out["g_post_mix"] = jnp.concatenate([sum_seg(16), sum_seg(17)], axis=0)
    out["g_pre_mlp"] = jnp.concatenate([sum_seg(18), sum_seg(19)], axis=0)
    out["g_post_mlp"] = jnp.concatenate([sum_seg(20), sum_seg(21)], axis=0)
    out["g_kv"] = sum_seg(22).reshape(-1)
    out["g_ckv"] = sum_seg(23).reshape(-1)
    out["g_cq"] = sum_seg(24)
    out["conv_b_in"] = chip_cols(sum_seg(25), 2 * dl)
    out["conv_ln_g"] = chip_cols(sum_seg(26), dl)
    out["conv_ln_b"] = chip_cols(sum_seg(27), dl)
    out["conv_b_out"] = chip_cols(sum_seg(28), dl)
    ddw_sum = chip_cols(sum_seg(29).reshape(CONV_WIDTH + 1, d), dl)
    out["conv_dw"] = ddw_sum[:CONV_WIDTH].reshape(1, CONV_WIDTH, dl)
    out["conv_dw_b"] = ddw_sum[CONV_WIDTH:]

    to_chunks = lambda a: a if a.ndim == 3 else chunked(a)
    red = dict(zip(big_names, _reduce_scatter([to_chunks(gw[n]) for n in big_names], c_idx)))
    out["conv_w_in"] = red["conv_w_in"][None]
    out["conv_w_out"] = red["conv_w_out"][None]
    out["w_o"] = red["w_o"][None]
    out["mlp_w_up"] = jnp.stack([red["up0"], red["up1"]])
    out["mlp_w_down"] = jnp.stack([red["down0"], red["down1"]])
    rs = red["small"]
    piece = lambda i: rs[:, so[i]:so[i + 1]]
    out["w_dkv"] = piece(0)
    out["w_kr"] = piece(1)[:, :QK_ROPE]
    out["w_dq"] = piece(2)[None]
    out["w_uk"], out["w_uv"] = piece(3), piece(4)
    out["w_uq"] = jnp.concatenate([piece(5).reshape(r_q, heads_l, QK_NOPE), piece(6).reshape(r_q, heads_l, QK_ROPE)],
                                  axis=2).reshape(1, r_q, -1)
    return out
```

```python
import functools

import jax
import jax.numpy as jnp
from jax import lax
from jax.experimental import pallas as pl
from jax.experimental.pallas import tpu as pltpu

F32 = jnp.float32
BF = jnp.bfloat16
MXU_DTYPE = BF

EPS = 1e-6
NEG = -1e30
ROPE_THETA = 10000.0
QK_NOPE = 128
QK_ROPE = 64
V_HEAD = 128
CONV_WIDTH = 31
ADAM_LR, ADAM_B1, ADAM_B2, ADAM_EPS, ADAM_WD, ADAM_STEP = 0.001, 0.9, 0.999, 1e-08, 0.01, 10

N_CHIPS = 4
N_DEV = 8
LANES = 128
VMEM_LIMIT = 56 * 2 ** 20
MESH = pl.DeviceIdType.MESH
ANY = pl.BlockSpec(memory_space=pl.ANY)


def _params(sem=None):
    return pltpu.CompilerParams(dimension_semantics=sem, vmem_limit_bytes=VMEM_LIMIT)


def _tile(n, pref, unit=LANES):
    if n <= pref:
        return n
    t = (pref // unit) * unit
    while t > unit and n % t:
        t -= unit
    assert n % t == 0, (n, pref)
    return t


def _dg(a, b, ca, cb):
    return lax.dot_general(a.astype(MXU_DTYPE), b.astype(MXU_DTYPE), (((ca,), (cb,)), ((), ())),
                           preferred_element_type=F32)


@jax.custom_vjp
def _dot_nn(a, b):
    return _dg(a, b, 1, 0)


def _dot_nn_fwd(a, b):
    return _dg(a, b, 1, 0), (a, b)


def _dot_nn_bwd(res, g):
    a, b = res
    return _dg(g, b, 1, 1).astype(a.dtype), _dg(a, g, 0, 0).astype(b.dtype)


_dot_nn.defvjp(_dot_nn_fwd, _dot_nn_bwd)


@jax.custom_vjp
def _dot_nt(a, b):
    return _dg(a, b, 1, 1)


def _dot_nt_fwd(a, b):
    return _dg(a, b, 1, 1), (a, b)


def _dot_nt_bwd(res, g):
    a, b = res
    return _dg(g, b, 1, 0).astype(a.dtype), _dg(g, a, 0, 0).astype(b.dtype)


_dot_nt.defvjp(_dot_nt_fwd, _dot_nt_bwd)


def _matmul(name, a, b, *, ca, cb, grid, a_blk, a_map, b_blk, b_map, o_shape, o_dtypes, o_blk, o_map,
            extras=(), epilogue=None):
    nk = grid[-1]
    n_ex, n_out = len(extras), len(o_dtypes)

    def body(*refs):
        a_ref, b_ref = refs[0], refs[1]
        ex_refs = refs[2:2 + n_ex]
        out_refs = refs[2 + n_ex:2 + n_ex + n_out]
        kk = pl.program_id(len(grid) - 1)

        def finish(acc):
            outs = (acc,) if epilogue is None else epilogue(acc, *[r[...] for r in ex_refs])
            for r, o in zip(out_refs, outs):
                r[...] = o.astype(r.dtype)

        part = _dg(a_ref[...], b_ref[...], ca, cb)
        if nk == 1:
            finish(part)
        else:
            acc_ref = refs[-1]

            @pl.when(kk == 0)
            def _():
                acc_ref[...] = part

            @pl.when(kk > 0)
            def _():
                acc_ref[...] += part

            @pl.when(kk == nk - 1)
            def _():
                finish(acc_ref[...])

    o_spec = pl.BlockSpec(o_blk, o_map)
    acc_shape = tuple(d for d in o_blk if d is not None)
    res = pl.pallas_call(
        body, grid=grid,
        in_specs=[pl.BlockSpec(a_blk, a_map), pl.BlockSpec(b_blk, b_map)] + [o_spec] * n_ex,
        out_specs=[o_spec] * n_out,
        out_shape=[jax.ShapeDtypeStruct(o_shape, dt) for dt in o_dtypes],
        scratch_shapes=[] if nk == 1 else [pltpu.VMEM(acc_shape, F32)],
        compiler_params=_params(("parallel",) * (len(grid) - 1) + ("arbitrary",)),
        name=name,
    )(a, b, *extras)
    return res[0] if n_out == 1 else res


TM, TN, TK = 1024, 512, 1024


def _mm_nn(name, a, w, layout, o_dtypes=(F32,), extras=(), epilogue=None):
    m, k = a.shape
    tm, tk = _tile(m, TM, 8), _tile(k, TK)
    if layout == "row":
        n = w.shape[1]
        tn = _tile(n, TN)
        grid = (m // tm, n // tn, k // tk)
        b_blk, b_map = (tk, tn), (lambda i, j, kk: (kk, j))
    else:
        nl = w.shape[2]
        n = nl * w.shape[0]
        tn = _tile(nl, TN)
        nb = nl // tn
        grid = (m // tm, n // tn, k // tk)
        b_blk, b_map = (None, tk, tn), (lambda i, j, kk: (j // nb, kk, j % nb))
    return _matmul(name, a, w, ca=1, cb=0, grid=grid, a_blk=(tm, tk), a_map=lambda i, j, kk: (i, kk),
                   b_blk=b_blk, b_map=b_map, o_shape=(m, n), o_dtypes=o_dtypes, o_blk=(tm, tn),
                   o_map=lambda i, j, kk: (i, j), extras=extras, epilogue=epilogue)


def _mm_nt(name, g, w, layout, o_dtypes=(F32,), extras=(), epilogue=None):
    m, n = g.shape
    tm, tn = _tile(m, TM, 8), None
    if layout == "row":
        k = w.shape[0]
        tn = _tile(n, TK)
        tk = _tile(k, TN)
        grid = (m // tm, k // tk, n // tn)
        b_blk, b_map = (tk, tn), (lambda i, j, kk: (j, kk))
    else:
        k, nl = w.shape[1], w.shape[2]
        tn = _tile(nl, TK)
        nb = nl // tn
        tk = _tile(k, TN)
        grid = (m // tm, k // tk, n // tn)
        b_blk, b_map = (None, tk, tn), (lambda i, j, kk: (kk // nb, j, kk % nb))
    return _matmul(name, g, w, ca=1, cb=1, grid=grid, a_blk=(tm, tn), a_map=lambda i, j, kk: (i, kk),
                   b_blk=b_blk, b_map=b_map, o_shape=(m, k), o_dtypes=o_dtypes, o_blk=(tm, tk),
                   o_map=lambda i, j, kk: (i, j), extras=extras, epilogue=epilogue)


def _mm_tn(name, x, g, layout, o_dtype=F32):
    t, k = x.shape
    n = g.shape[1]
    tt = _tile(t, TK, 8)
    tk = _tile(k, TM)
    if layout == "row":
        tn = _tile(n, TN)
        o_shape, o_blk, o_map = (k, n), (tk, tn), (lambda i, j, kk: (i, j))
    else:
        nl = n // N_CHIPS
        tn = _tile(nl, TN)
        nb = nl // tn
        o_shape, o_blk, o_map = (N_CHIPS, k, nl), (None, tk, tn), (lambda i, j, kk: (j // nb, i, j % nb))
    grid = (k // tk, n // tn, t // tt)
    return _matmul(name, x, g, ca=0, cb=0, grid=grid, a_blk=(tt, tk), a_map=lambda i, j, kk: (kk, i),
                   b_blk=(tt, tn), b_map=lambda i, j, kk: (kk, j), o_shape=o_shape, o_dtypes=(o_dtype,),
                   o_blk=o_blk, o_map=o_map)


TR = 256


def _rw_fwd(name, fn, rows, vecs, o_dtypes, tr=TR):
    t = rows[0].shape[0]
    tr = min(tr, t)
    n_r, n_v = len(rows), len(vecs)
    o_sds = jax.eval_shape(fn, *[jax.ShapeDtypeStruct((tr, r.shape[1]), r.dtype) for r in rows],
                           *[jax.ShapeDtypeStruct(v.shape, v.dtype) for v in vecs])

    def body(*refs):
        outs = fn(*[r[...] for r in refs[:n_r + n_v]])
        for r, o in zip(refs[n_r + n_v:], outs):
            r[...] = o.astype(r.dtype)

    res = pl.pallas_call(
        body, grid=(t // tr,),
        in_specs=[pl.BlockSpec((tr, r.shape[1]), lambda i: (i, 0)) for r in rows]
        + [pl.BlockSpec(v.shape, lambda i: (0, 0)) for v in vecs],
        out_specs=[pl.BlockSpec((tr, o.shape[1]), lambda i: (i, 0)) for o in o_sds],
        out_shape=[jax.ShapeDtypeStruct((t, o.shape[1]), dt) for o, dt in zip(o_sds, o_dtypes)],
        compiler_params=_params(("parallel",)), name=name,
    )(*rows, *vecs)
    return res


def _rw_bwd(name, fn, rows, vecs, cots, row_grad_dtypes, add_rows=None, tr=TR):
    t = rows[0].shape[0]
    tr = min(tr, t)
    add_rows = add_rows or {}
    n_r, n_v, n_c = len(rows), len(vecs), len(cots)
    diff = [i for i, dt in enumerate(row_grad_dtypes) if dt is not None]
    adds = [add_rows[i] for i in diff if i in add_rows]
    n_a = len(adds)

    def body(*refs):
        rvals = [r[...] for r in refs[:n_r]]
        vvals = [r[...] for r in refs[n_r:n_r + n_v]]
        c_refs = refs[n_r + n_v:n_r + n_v + n_c]
        a_refs = list(refs[n_r + n_v + n_c:n_r + n_v + n_c + n_a])
        o_refs = refs[n_r + n_v + n_c + n_a:]

        def f(*d):
            full = list(rvals)
            for i, x in zip(diff, d[:len(diff)]):
                full[i] = x
            return fn(*full, *d[len(diff):])

        outs, vjp = jax.vjp(f, *[rvals[i] for i in diff], *vvals)
        grads = vjp(tuple(c[...].astype(o.dtype) for c, o in zip(c_refs, outs)))
        for n, i in enumerate(diff):
            gr = grads[n].astype(F32)
            if i in add_rows:
                gr = gr + a_refs.pop(0)[...].astype(F32)
            o_refs[n][...] = gr.astype(o_refs[n].dtype)
        first = pl.program_id(0) == 0
        for r, gv in zip(o_refs[len(diff):], grads[len(diff):]):
            @pl.when(first)
            def _(r=r, gv=gv):
                r[...] = gv

            @pl.when(jnp.logical_not(first))
            def _(r=r, gv=gv):
                r[...] += gv

    row_spec = lambda a: pl.BlockSpec((tr, a.shape[1]), lambda i: (i, 0))
    vec_spec = lambda a: pl.BlockSpec(a.shape, lambda i: (0, 0))
    res = pl.pallas_call(
        body, grid=(t // tr,),
        in_specs=[row_spec(r) for r in rows] + [vec_spec(v) for v in vecs] + [row_spec(c) for c in cots]
        + [row_spec(a) for a in adds],
        out_specs=[row_spec(rows[i]) for i in diff] + [vec_spec(v) for v in vecs],
        out_shape=[jax.ShapeDtypeStruct(rows[i].shape, row_grad_dtypes[i]) for i in diff]
        + [jax.ShapeDtypeStruct(v.shape, F32) for v in vecs],
        compiler_params=_params(("arbitrary",)), name=name,
    )(*rows, *vecs, *cots, *adds)
    return res[:len(diff)], res[len(diff):]


def _rms(x, g):
    return x * lax.rsqrt(jnp.mean(x * x, axis=-1, keepdims=True) + EPS) * g


def _f_pre(h, g, shift, scale):
    return (_rms(h, g) * (1.0 + scale) + shift,)


def _f_pre2(h, g1, sh1, sc1, g2, sh2, sc2):
    return _f_pre(h, g1, sh1, sc1) + _f_pre(h, g2, sh2, sc2)


def _f_post(y, gate, g):
    return (gate * _rms(y, g),)


def _f_post_bias(y, bias, gate, g):
    return (gate * _rms(y + bias, g),)


def _f_glu(z, bias):
    z = z + bias
    half = z.shape[1] // 2
    return (z[:, :half] * jax.nn.sigmoid(z[:, half:]),)


def _f_ln_silu(u, g, b):
    mu = jnp.mean(u, axis=-1, keepdims=True)
    var = jnp.mean(jnp.square(u - mu), axis=-1, keepdims=True)
    y = (u - mu) * lax.rsqrt(var + EPS) * g + b
    return (y * jax.nn.sigmoid(y),)


def _rope_raw(x, cos, sin):
    n = x.shape[1]
    reps = n // LANES
    if reps > 1:
        cos = jnp.concatenate([cos] * reps, axis=1)
        sin = jnp.concatenate([sin] * reps, axis=1)
    lane = lax.broadcasted_iota(jnp.int32, x.shape, 1)
    half = QK_ROPE // 2
    partner = jnp.where((lane % QK_ROPE) < half, pltpu.roll(x, n - half, 1), pltpu.roll(x, half, 1))
    return x * cos + partner * sin


@jax.custom_vjp
def _rope(x, cos, sin):
    return _rope_raw(x, cos, sin)


def _rope_fwd(x, cos, sin):
    return _rope_raw(x, cos, sin), (cos, sin)


def _rope_bwd(res, g):
    cos, sin = res
    return _rope_raw(g, cos, -sin), jnp.zeros_like(cos), jnp.zeros_like(sin)


_rope.defvjp(_rope_fwd, _rope_bwd)


def _make_f_lat(r_kv):
    def f(pre_kv, pre_q, cos, sin, g_ckv, g_cq):
        c_kv = _rms(pre_kv[:, :r_kv], g_ckv)
        kr = _rope(pre_kv[:, r_kv:], cos, sin)
        return c_kv, kr, _rms(pre_q, g_cq)
    return f


def _f_rope(x, cos, sin):
    return (_rope(x, cos, sin),)


CONV_TC = 128
CONV_TT = 256
PADR = 32


def _conv_fwd(u, dw, dw_b):
    t, d = u.shape
    tc, tt = min(CONV_TC, d), min(CONV_TT, t)
    off = PADR - (CONV_WIDTH - 1)

    def body(u_ref, w_ref, b_ref, o_ref, pad_ref):
        pad_ref[pl.ds(0, PADR), :] = jnp.zeros((PADR, tc), F32)
        pad_ref[pl.ds(PADR, t), :] = u_ref[...]
        for t0 in range(0, t, tt):
            acc = jnp.zeros((tt, tc), F32) + b_ref[...]
            for j in range(CONV_WIDTH):
                acc = acc + pad_ref[pl.ds(t0 + off + j, tt), :] * w_ref[pl.ds(j, 1), :]
            o_ref[pl.ds(t0, tt), :] = acc

    return pl.pallas_call(
        body, grid=(d // tc,),
        in_specs=[pl.BlockSpec((t, tc), lambda j: (0, j)), pl.BlockSpec((CONV_WIDTH, tc), lambda j: (0, j)),
                  pl.BlockSpec((1, tc), lambda j: (0, j))],
        out_specs=pl.BlockSpec((t, tc), lambda j: (0, j)),
        out_shape=jax.ShapeDtypeStruct((t, d), F32),
        scratch_shapes=[pltpu.VMEM((t + PADR, tc), F32)],
        compiler_params=_params(("parallel",)), name="conv_fwd",
    )(u, dw, dw_b)


def _conv_bwd(u, duc, dw):
    t, d = u.shape
    tc, tt = min(CONV_TC, d), min(CONV_TT, t)
    off = PADR - (CONV_WIDTH - 1)

    def body(u_ref, g_ref, w_ref, du_ref, dwt_ref, padu_ref, padg_ref):
        padu_ref[pl.ds(0, PADR), :] = jnp.zeros((PADR, tc), F32)
        padu_ref[pl.ds(PADR, t), :] = u_ref[...]
        padg_ref[pl.ds(t, PADR), :] = jnp.zeros((PADR, tc), F32)
        padg_ref[pl.ds(0, t), :] = g_ref[...]
        for t0 in range(0, t, tt):
            acc = jnp.zeros((tt, tc), F32)
            for j in range(CONV_WIDTH):
                acc = acc + padg_ref[pl.ds(t0 + (CONV_WIDTH - 1) - j, tt), :] * w_ref[pl.ds(j, 1), :]
            du_ref[pl.ds(t0, tt), :] = acc
        for j in range(CONV_WIDTH):
            acc = jnp.zeros((tt, tc), F32)
            for t0 in range(0, t, tt):
                acc = acc + g_ref[pl.ds(t0, tt), :] * padu_ref[pl.ds(t0 + off + j, tt), :]
            dwt_ref[pl.ds(j, 1), :] = jnp.sum(acc, axis=0, keepdims=True)
        acc = jnp.zeros((tt, tc), F32)
        for t0 in range(0, t, tt):
            acc = acc + g_ref[pl.ds(t0, tt), :]
        dwt_ref[pl.ds(CONV_WIDTH, 1), :] = jnp.sum(acc, axis=0, keepdims=True)

    col = lambda r: pl.BlockSpec((r, tc), lambda j: (0, j))
    return pl.pallas_call(
        body, grid=(d // tc,),
        in_specs=[col(t), col(t), col(CONV_WIDTH)],
        out_specs=[col(t), col(CONV_WIDTH + 1)],
        out_shape=[jax.ShapeDtypeStruct((t, d), F32), jax.ShapeDtypeStruct((CONV_WIDTH + 1, d), F32)],
        scratch_shapes=[pltpu.VMEM((t + PADR, tc), F32), pltpu.VMEM((t + PADR, tc), F32)],
        compiler_params=_params(("parallel",)), name="conv_bwd",
    )(u, duc, dw)


@jax.custom_vjp
def _swap_halves(x):
    return pltpu.roll(x, LANES // 2, 1)


_swap_halves.defvjp(lambda x: (pltpu.roll(x, LANES // 2, 1), None), lambda _, g: (pltpu.roll(g, LANES // 2, 1),))


def _attn_block(qn, qr, kn, kr, v, q0):
    scale = (QK_NOPE + QK_ROPE) ** -0.5
    lane = lax.broadcasted_iota(jnp.int32, kr.shape, 1)
    kr_a = kr * (lane < QK_ROPE).astype(kr.dtype)
    kr_b = _swap_halves(kr_a)
    outs = []
    for hh, kr_h in ((0, kr_a), (1, kr_b)):
        sl = slice(hh * QK_NOPE, (hh + 1) * QK_NOPE)
        s = (_dot_nt(qn[:, sl], kn[:, sl]) + _dot_nt(qr, kr_h)) * scale
        row = lax.broadcasted_iota(jnp.int32, s.shape, 0) + q0
        col = lax.broadcasted_iota(jnp.int32, s.shape, 1)
        s = jnp.where(col <= row, s, NEG)
        e = jnp.exp(s - jnp.max(s, axis=-1, keepdims=True))
        p = e / jnp.sum(e, axis=-1, keepdims=True)
        outs.append(_dot_nn(p, v[:, sl]))
    return jnp.concatenate(outs, axis=1)


def _attn_fwd(qn, qr, kn, kr, v, tq=512):
    t, w = qn.shape
    pairs = w // (2 * QK_NOPE)
    tq = min(tq, t)
    pw = 2 * QK_NOPE

    def body(qn_ref, qr_ref, kn_ref, kr_ref, v_ref, o_ref):
        for q0 in range(0, t, tq):
            l = q0 + tq
            o_ref[pl.ds(q0, tq), :] = _attn_block(
                qn_ref[pl.ds(q0, tq), :], qr_ref[pl.ds(q0, tq), :], kn_ref[pl.ds(0, l), :], kr_ref[pl.ds(0, l), :],
                v_ref[pl.ds(0, l), :], q0).astype(o_ref.dtype)

    pair = lambda wd: pl.BlockSpec((t, wd), lambda p: (0, p))
    return pl.pallas_call(
        body, grid=(pairs,),
        in_specs=[pair(pw), pair(LANES), pair(pw), pl.BlockSpec((t, LANES), lambda p: (0, 0)), pair(pw)],
        out_specs=pair(pw), out_shape=jax.ShapeDtypeStruct((t, w), BF),
        compiler_params=_params(("parallel",)), name="attn_fwd",
    )(qn, qr, kn, kr, v)


def _attn_bwd(qn, qr, kn, kr, v, do, tq=256):
    t, w = qn.shape
    pairs = w // (2 * QK_NOPE)
    tq = min(tq, t)
    pw = 2 * QK_NOPE

    def body(qn_ref, qr_ref, kn_ref, kr_ref, v_ref, do_ref, dqn_ref, dqr_ref, dkn_ref, dv_ref, dkr_ref,
             akn_ref, av_ref, akr_ref):
        akn_ref[...] = jnp.zeros_like(akn_ref)
        av_ref[...] = jnp.zeros_like(av_ref)
        akr_ref[...] = jnp.zeros_like(akr_ref)
        for q0 in range(0, t, tq):
            l = q0 + tq
            rows, keys = pl.ds(q0, tq), pl.ds(0, l)
            _, vjp = jax.vjp(functools.partial(_attn_block, q0=q0), qn_ref[rows, :], qr_ref[rows, :],
                             kn_ref[keys, :], kr_ref[keys, :], v_ref[keys, :])
            dqn, dqr, dkn, dkr, dv = vjp(do_ref[rows, :].astype(F32))
            dqn_ref[rows, :] = dqn.astype(dqn_ref.dtype)
            dqr_ref[rows, :] = dqr.astype(dqr_ref.dtype)
            akn_ref[keys, :] += dkn.astype(F32)
            av_ref[keys, :] += dv.astype(F32)
            akr_ref[keys, :] += dkr.astype(F32)
        dkn_ref[...] = akn_ref[...].astype(dkn_ref.dtype)
        dv_ref[...] = av_ref[...].astype(dv_ref.dtype)
        first = pl.program_id(0) == 0

        @pl.when(first)
        def _():
            dkr_ref[...] = akr_ref[...]

        @pl.when(jnp.logical_not(first))
        def _():
            dkr_ref[...] += akr_ref[...]

    pair = lambda wd: pl.BlockSpec((t, wd), lambda p: (0, p))
    shared = pl.BlockSpec((t, LANES), lambda p: (0, 0))
    sds = jax.ShapeDtypeStruct
    return pl.pallas_call(
        body, grid=(pairs,),
        in_specs=[pair(pw), pair(LANES), pair(pw), shared, pair(pw), pair(pw)],
        out_specs=[pair(pw), pair(LANES), pair(pw), pair(pw), shared],
        out_shape=[sds((t, w), BF), sds((t, pairs * LANES), F32), sds((t, w), BF), sds((t, w), BF), sds((t, LANES), F32)],
        scratch_shapes=[pltpu.VMEM((t, pw), F32), pltpu.VMEM((t, pw), F32), pltpu.VMEM((t, LANES), F32)],
        compiler_params=_params(("arbitrary",)), name="attn_bwd",
    )(qn, qr, kn, kr, v, do)


def _ew(name, fn, ins, o_dtypes, max_bytes=2 ** 21):
    r, c = ins[0].shape
    tr = r
    if r * c * 4 > max_bytes:
        tr = max(16, (max_bytes // (c * 4)) // 16 * 16)
        while r % tr:
            tr -= 16
    n_in = len(ins)

    def body(*refs):
        outs = fn(*[x[...] for x in refs[:n_in]])
        for o_ref, o in zip(refs[n_in:], outs):
            o_ref[...] = o.astype(o_ref.dtype)

    spec = pl.BlockSpec((tr, c), lambda i: (i, 0))
    return pl.pallas_call(
        body, grid=(r // tr,), in_specs=[spec] * n_in, out_specs=[spec] * len(o_dtypes),
        out_shape=[jax.ShapeDtypeStruct((r, c), dt) for dt in o_dtypes],
        compiler_params=_params(("parallel",)), name=name,
    )(*ins)


def _adamw_math(w, g, m, v):
    m = ADAM_B1 * m + (1.0 - ADAM_B1) * g
    v = ADAM_B2 * v + (1.0 - ADAM_B2) * jnp.square(g)
    m_hat = m / (1.0 - ADAM_B1 ** ADAM_STEP)
    v_hat = v / (1.0 - ADAM_B2 ** ADAM_STEP)
    delta = -ADAM_LR * (m_hat / (jnp.sqrt(v_hat) + ADAM_EPS) + ADAM_WD * w)
    return delta, m, v


def _adamw(name, w, g, m, v):
    shape = w.shape
    to2 = lambda a: a.reshape(-1, shape[-1]) if a.ndim > 1 else a.reshape(1, -1)
    d, nm, nv = _ew("adamw_" + name, _adamw_math, [to2(w), to2(g), to2(m), to2(v)], (F32, F32, F32))
    return d.reshape(shape), nm.reshape(shape), nv.reshape(shape)


def _rope_tables(positions, t):
    half = QK_ROPE // 2
    inv = 1.0 / (ROPE_THETA ** (jnp.arange(0, QK_ROPE, 2, dtype=F32) / QK_ROPE))
    inv_l = jnp.tile(inv, LANES // half).reshape(1, LANES)
    sign = jnp.tile(jnp.concatenate([-jnp.ones((half,), F32), jnp.ones((half,), F32)]), LANES // QK_ROPE).reshape(1, LANES)

    def body(p_ref, inv_ref, sg_ref, c_ref, s_ref):
        ang = p_ref[...].astype(F32) * inv_ref[...]
        c_ref[...] = jnp.cos(ang)
        s_ref[...] = jnp.sin(ang) * sg_ref[...]

    return pl.pallas_call(body, out_shape=[jax.ShapeDtypeStruct((t, LANES), F32)] * 2, name="rope_tables")(
        positions.reshape(t, 1), inv_l, sign)


def _loss_grad(h, target):
    t, d = h.shape
    tr = min(TR, t)

    def body(h_ref, y_ref, dh_ref, l_ref):
        err = h_ref[...] - y_ref[...]
        dh_ref[...] = err * (1.0 / d)
        part = 0.5 * jnp.sum(jnp.mean(jnp.square(err), axis=-1, keepdims=True), axis=0, keepdims=True)
        first = pl.program_id(0) == 0

        @pl.when(first)
        def _():
            l_ref[...] = part

        @pl.when(jnp.logical_not(first))
        def _():
            l_ref[...] += part

    row = pl.BlockSpec((tr, d), lambda i: (i, 0))
    return pl.pallas_call(
        body, grid=(t // tr,), in_specs=[row, row],
        out_specs=[row, pl.BlockSpec((1, 1), lambda i: (0, 0))],
        out_shape=[jax.ShapeDtypeStruct((t, d), F32), jax.ShapeDtypeStruct((1, 1), F32)],
        compiler_params=_params(("arbitrary",)), name="loss_grad",
    )(h, target)


def _sum_devices(g):
    def fn(*parts):
        acc = parts[0]
        for p in parts[1:]:
            acc = acc + p
        return (acc,)
    return _ew("sum_devices", fn, [g[i] for i in range(g.shape[0])], (F32,))[0]


def _place():
    x, y, c = lax.axis_index("x"), lax.axis_index("y"), lax.axis_index("c")
    return x, y, c, [(1 - x, y), (x, 1 - y), (1 - x, 1 - y)]


def _all_gather_small(name, v):
    r, n = v.shape

    def body(x_ref, out_ref, send_sems, recv_sems, local_sem):
        x, y, c, chips = _place()
        me, sibling = (x, y, c), (x, y, 1 - c)

        def rows(px, py, pc):
            return out_ref.at[4 * px + 2 * py + pc]

        def copy(k, block, to, src=None):
            return pltpu.make_async_remote_copy(
                src_ref=rows(*block) if src is None else src, dst_ref=rows(*block), send_sem=send_sems.at[k],
                recv_sem=recv_sems.at[k], device_id=to, device_id_type=MESH)

        mine = pltpu.make_async_copy(x_ref, rows(*me), local_sem)
        mine.start()
        first = [copy(0, me, sibling, src=x_ref)]
        first += [copy(1 + j, me, (*chip, c), src=x_ref) for j, chip in enumerate(chips)]
        for cp in first:
            cp.start()
        passed = [copy(4 + j, (*chip, c), sibling) for j, chip in enumerate(chips)]
        for j, chip in enumerate(chips):
            copy(1 + j, (*chip, c), me).wait_recv()
            passed[j].start()
        copy(0, sibling, me).wait_recv()
        for j, chip in enumerate(chips):
            copy(4 + j, (*chip, 1 - c), me).wait_recv()
        for cp in first + passed:
            cp.wait_send()
        mine.wait()

    return pl.pallas_call(
        body, out_shape=jax.ShapeDtypeStruct((N_DEV, r, n), v.dtype),
        in_specs=[pl.BlockSpec(memory_space=pltpu.VMEM)], out_specs=pl.BlockSpec(memory_space=pltpu.VMEM),
        scratch_shapes=[pltpu.SemaphoreType.DMA((7,)), pltpu.SemaphoreType.DMA((7,)), pltpu.SemaphoreType.DMA],
        compiler_params=pltpu.CompilerParams(vmem_limit_bytes=VMEM_LIMIT), name=name,
    )(v)


def _cast_into_slot(name, w, idx):
    r, n = w.shape
    tr = r
    while tr * n * 4 > 2 ** 21 and tr % 32 == 0:
        tr //= 2

    def body(idx_ref, w_ref, o_ref):
        o_ref[...] = w_ref[...].astype(o_ref.dtype)

    return pl.pallas_call(
        body,
        grid_spec=pltpu.PrefetchScalarGridSpec(
            num_scalar_prefetch=1, grid=(r // tr,),
            in_specs=[pl.BlockSpec((tr, n), lambda i, idx_ref: (i, 0))],
            out_specs=pl.BlockSpec((None, tr, n), lambda i, idx_ref: (idx_ref[1], i, 0))),
        out_shape=jax.ShapeDtypeStruct((N_CHIPS, r, n), BF),
        compiler_params=_params(("parallel",)), name="cast_" + name,
    )(idx, w)


def _gather_weights(bufs):
    n_w = len(bufs)

    def body(*refs):
        outs = refs[n_w:2 * n_w]
        send_sems, recv_sems = refs[2 * n_w:]
        x, y, c, chips = _place()

        def copy(i, s, chip, which, to):
            h = outs[i].shape[1] // 2
            blk = outs[i].at[2 * chip[0] + chip[1], pl.ds(which * h, h)]
            return pltpu.make_async_remote_copy(
                src_ref=blk, dst_ref=blk, send_sem=send_sems.at[6 * i + s], recv_sem=recv_sems.at[6 * i + s],
                device_id=to, device_id_type=MESH)

        sends = []
        for i in range(n_w):
            for j, chip in enumerate(chips):
                cp = copy(i, j, (x, y), c, (*chip, c))
                cp.start()
                sends.append(cp)
        for i in range(n_w):
            for j, chip in enumerate(chips):
                copy(i, j, chip, c, (x, y, c)).wait_recv()
                cp = copy(i, 3 + j, chip, c, (x, y, 1 - c))
                cp.start()
                sends.append(cp)
        for i in range(n_w):
            for j, chip in enumerate(chips):
                copy(i, 3 + j, chip, 1 - c, (x, y, c)).wait_recv()
        for cp in sends:
            cp.wait_send()

    return pl.pallas_call(
        body, in_specs=[ANY] * n_w, out_specs=[ANY] * n_w,
        out_shape=[jax.ShapeDtypeStruct(b.shape, b.dtype) for b in bufs],
        input_output_aliases={i: i for i in range(n_w)},
        scratch_shapes=[pltpu.SemaphoreType.DMA((6 * n_w,)), pltpu.SemaphoreType.DMA((6 * n_w,))],
        name="gather_weights",
    )(*bufs)


def _swap_other_half(grads):
    n_w = len(grads)

    def body(*refs):
        ins, outs = refs[:n_w], refs[n_w:2 * n_w]
        send_sems, recv_sems = refs[2 * n_w:]
        x, y, c, _ = _place()
        cps = []
        for i in range(n_w):
            h = ins[i].shape[1] // 2
            cp = pltpu.make_async_remote_copy(
                src_ref=ins[i].at[:, pl.ds((1 - c) * h, h), :], dst_ref=outs[i], send_sem=send_sems.at[i],
                recv_sem=recv_sems.at[i], device_id=(x, y, 1 - c), device_id_type=MESH)
            cp.start()
            cps.append(cp)
        for cp in cps:
            cp.wait()

    return pl.pallas_call(
        body, in_specs=[ANY] * n_w, out_specs=[ANY] * n_w,
        out_shape=[jax.ShapeDtypeStruct((g.shape[0], g.shape[1] // 2, g.shape[2]), g.dtype) for g in grads],
        scratch_shapes=[pltpu.SemaphoreType.DMA((n_w,)), pltpu.SemaphoreType.DMA((n_w,))],
        name="swap_other_half",
    )(*grads)


def _add_my_half(name, g, s, idx, o_dtype):
    nc, r, n = g.shape
    h = r // 2
    tr = h
    while tr * n * 4 > 2 ** 21 and tr % 32 == 0:
        tr //= 2
    nb = h // tr

    def body(idx_ref, g_ref, s_ref, o_ref, own_ref):
        val = (g_ref[...] + s_ref[...].astype(F32)).astype(o_ref.dtype)
        o_ref[...] = val

        @pl.when(pl.program_id(1) == idx_ref[1])
        def _():
            own_ref[...] = val

    return pl.pallas_call(
        body,
        grid_spec=pltpu.PrefetchScalarGridSpec(
            num_scalar_prefetch=1, grid=(nb, nc),
            in_specs=[pl.BlockSpec((None, tr, n), lambda i, j, idx_ref: (j, idx_ref[0] * nb + i, 0)),
                      pl.BlockSpec((None, tr, n), lambda i, j, idx_ref: (j, i, 0))],
            out_specs=[pl.BlockSpec((None, tr, n), lambda i, j, idx_ref: (j, i, 0)),
                       pl.BlockSpec((None, tr, n), lambda i, j, idx_ref: (idx_ref[1], i, 0))]),
        out_shape=[jax.ShapeDtypeStruct((nc, h, n), o_dtype)] * 2,
        compiler_params=_params(("parallel", "arbitrary")), name=name,
    )(idx, g, s)


def _scatter_partials(parts, bufs):
    n_w = len(parts)

    def body(*refs):
        ins, outs = refs[:n_w], refs[2 * n_w:3 * n_w]
        send_sems, recv_sems = refs[3 * n_w:]
        x, y, c, chips = _place()
        k = 2 * x + y
        sends = []
        for i in range(n_w):
            for j, chip in enumerate(chips):
                cp = pltpu.make_async_remote_copy(
                    src_ref=ins[i].at[2 * chip[0] + chip[1]], dst_ref=outs[i].at[k], send_sem=send_sems.at[3 * i + j],
                    recv_sem=recv_sems.at[3 * i + j], device_id=(*chip, c), device_id_type=MESH)
                cp.start()
                sends.append(cp)
        for i in range(n_w):
            for j, chip in enumerate(chips):
                blk = outs[i].at[2 * chip[0] + chip[1]]
                pltpu.make_async_remote_copy(
                    src_ref=blk, dst_ref=blk, send_sem=send_sems.at[3 * i + j], recv_sem=recv_sems.at[3 * i + j],
                    device_id=(*chip, c), device_id_type=MESH).wait_recv()
        for cp in sends:
            cp.wait_send()

    return pl.pallas_call(
        body, in_specs=[ANY] * (2 * n_w), out_specs=[ANY] * n_w,
        out_shape=[jax.ShapeDtypeStruct(b.shape, b.dtype) for b in bufs],
        input_output_aliases={n_w + i: i for i in range(n_w)},
        scratch_shapes=[pltpu.SemaphoreType.DMA((3 * n_w,)), pltpu.SemaphoreType.DMA((3 * n_w,))],
        name="scatter_partials",
    )(*parts, *bufs)


def _sum_chips(name, q, idx):
    nc, h, n = q.shape
    tr = h
    while tr * n * 4 > 2 ** 20 and tr % 32 == 0:
        tr //= 2
    nb = h // tr

    def body(idx_ref, q_ref, o_ref):
        acc = q_ref[0].astype(F32)
        for j in range(1, nc):
            acc = acc + q_ref[j].astype(F32)
        o_ref[...] = acc

    return pl.pallas_call(
        body,
        grid_spec=pltpu.PrefetchScalarGridSpec(
            num_scalar_prefetch=1, grid=(nb,),
            in_specs=[pl.BlockSpec((nc, tr, n), lambda i, idx_ref: (0, i, 0))],
            out_specs=pl.BlockSpec((tr, n), lambda i, idx_ref: (idx_ref[0] * nb + i, 0))),
        out_shape=jax.ShapeDtypeStruct((2 * h, n), F32),
        compiler_params=_params(("parallel",)), name=name,
    )(idx, q)


def _join_halves(bufs):
    n_w = len(bufs)

    def body(*refs):
        outs = refs[n_w:2 * n_w]
        send_sems, recv_sems = refs[2 * n_w:]
        x, y, c, _ = _place()

        def copy(i, which):
            h = outs[i].shape[0] // 2
            rows = outs[i].at[pl.ds(which * h, h)]
            return pltpu.make_async_remote_copy(
                src_ref=rows, dst_ref=rows, send_sem=send_sems.at[i], recv_sem=recv_sems.at[i],
                device_id=(x, y, 1 - c), device_id_type=MESH)

        cps = [copy(i, c) for i in range(n_w)]
        for cp in cps:
            cp.start()
        for i, cp in enumerate(cps):
            cp.wait_send()
            copy(i, 1 - c).wait_recv()

    return pl.pallas_call(
        body, in_specs=[ANY] * n_w, out_specs=[ANY] * n_w,
        out_shape=[jax.ShapeDtypeStruct(b.shape, b.dtype) for b in bufs],
        input_output_aliases={i: i for i in range(n_w)},
        scratch_shapes=[pltpu.SemaphoreType.DMA((n_w,)), pltpu.SemaphoreType.DMA((n_w,))],
        name="join_halves",
    )(*bufs)


def _reduce_scatter(grads, idx):
    from_sibling = _swap_other_half(grads)
    pairs = [_add_my_half("add_my_half_%d" % i, g, s, idx, BF) for i, (g, s) in enumerate(zip(grads, from_sibling))]
    received = _scatter_partials([p[0] for p in pairs], [p[1] for p in pairs])
    halves = [_sum_chips("sum_chips_%d" % i, q, idx) for i, q in enumerate(received)]
    return _join_halves(halves)


def _relu2_epilogue(acc):
    r = jnp.maximum(acc, 0.0)
    return acc, r * r


def _relu2_bwd_epilogue(acc, a):
    return (acc * (2.0 * jnp.maximum(a, 0.0)),)


def _add_epilogue(acc, other):
    return (acc + other,)


def kernel(x, c, positions, w_ada_mix, b_ada_mix, w_ada_mlp, b_ada_mlp, g_pre_mix, g_post_mix, g_pre_mlp, g_post_mlp, conv_w_in, conv_b_in, conv_dw, conv_dw_b, conv_ln_g, conv_ln_b, conv_w_out, conv_b_out, w_ada_kv, b_ada_kv, g_kv, w_dkv, g_ckv, w_kr, w_uk, w_uv, w_dq, g_cq, w_uq, w_o, mlp_w_up, mlp_w_down, loss_target, m_w_ada_mix, m_b_ada_mix, m_w_ada_mlp, m_b_ada_mlp, m_g_pre_mix, m_g_post_mix, m_g_pre_mlp, m_g_post_mlp, m_conv_w_in, m_conv_b_in, m_conv_dw, m_conv_dw_b, m_conv_ln_g, m_conv_ln_b, m_conv_w_out, m_conv_b_out, m_w_ada_kv, m_b_ada_kv, m_g_kv, m_w_dkv, m_g_ckv, m_w_kr, m_w_uk, m_w_uv, m_w_dq, m_g_cq, m_w_uq, m_w_o, m_mlp_w_up, m_mlp_w_down, v_w_ada_mix, v_b_ada_mix, v_w_ada_mlp, v_b_ada_mlp, v_g_pre_mix, v_g_post_mix, v_g_pre_mlp, v_g_post_mlp, v_conv_w_in, v_conv_b_in, v_conv_dw, v_conv_dw_b, v_conv_ln_g, v_conv_ln_b, v_conv_w_out, v_conv_b_out, v_w_ada_kv, v_b_ada_kv, v_g_kv, v_w_dkv, v_g_ckv, v_w_kr, v_w_uk, v_w_uv, v_w_dq, v_g_cq, v_w_uq, v_w_o, v_mlp_w_up, v_mlp_w_down):
    weights = dict(w_ada_mix=w_ada_mix, b_ada_mix=b_ada_mix, w_ada_mlp=w_ada_mlp, b_ada_mlp=b_ada_mlp, g_pre_mix=g_pre_mix, g_post_mix=g_post_mix, g_pre_mlp=g_pre_mlp, g_post_mlp=g_post_mlp, conv_w_in=conv_w_in, conv_b_in=conv_b_in, conv_dw=conv_dw, conv_dw_b=conv_dw_b, conv_ln_g=conv_ln_g, conv_ln_b=conv_ln_b, conv_w_out=conv_w_out, conv_b_out=conv_b_out, w_ada_kv=w_ada_kv, b_ada_kv=b_ada_kv, g_kv=g_kv, w_dkv=w_dkv, g_ckv=g_ckv, w_kr=w_kr, w_uk=w_uk, w_uv=w_uv, w_dq=w_dq, g_cq=g_cq, w_uq=w_uq, w_o=w_o, mlp_w_up=mlp_w_up, mlp_w_down=mlp_w_down)
    m_in = dict(w_ada_mix=m_w_ada_mix, b_ada_mix=m_b_ada_mix, w_ada_mlp=m_w_ada_mlp, b_ada_mlp=m_b_ada_mlp, g_pre_mix=m_g_pre_mix, g_post_mix=m_g_post_mix, g_pre_mlp=m_g_pre_mlp, g_post_mlp=m_g_post_mlp, conv_w_in=m_conv_w_in, conv_b_in=m_conv_b_in, conv_dw=m_conv_dw, conv_dw_b=m_conv_dw_b, conv_ln_g=m_conv_ln_g, conv_ln_b=m_conv_ln_b, conv_w_out=m_conv_w_out, conv_b_out=m_conv_b_out, w_ada_kv=m_w_ada_kv, b_ada_kv=m_b_ada_kv, g_kv=m_g_kv, w_dkv=m_w_dkv, g_ckv=m_g_ckv, w_kr=m_w_kr, w_uk=m_w_uk, w_uv=m_w_uv, w_dq=m_w_dq, g_cq=m_g_cq, w_uq=m_w_uq, w_o=m_w_o, mlp_w_up=m_mlp_w_up, mlp_w_down=m_mlp_w_down)
    v_in = dict(w_ada_mix=v_w_ada_mix, b_ada_mix=v_b_ada_mix, w_ada_mlp=v_w_ada_mlp, b_ada_mlp=v_b_ada_mlp, g_pre_mix=v_g_pre_mix, g_post_mix=v_g_post_mix, g_pre_mlp=v_g_pre_mlp, g_post_mlp=v_g_post_mlp, conv_w_in=v_conv_w_in, conv_b_in=v_conv_b_in, conv_dw=v_conv_dw, conv_dw_b=v_conv_dw_b, conv_ln_g=v_conv_ln_g, conv_ln_b=v_conv_ln_b, conv_w_out=v_conv_w_out, conv_b_out=v_conv_b_out, w_ada_kv=v_w_ada_kv, b_ada_kv=v_b_ada_kv, g_kv=v_g_kv, w_dkv=v_w_dkv, g_ckv=v_g_ckv, w_kr=v_w_kr, w_uk=v_w_uk, w_uv=v_w_uv, w_dq=v_w_dq, g_cq=v_g_cq, w_uq=v_w_uq, w_o=v_w_o, mlp_w_up=v_mlp_w_up, mlp_w_down=v_mlp_w_down)
    grads = _step_grads(x, c, positions, loss_target, weights)
    loss, grad_x = grads.pop("loss"), grads.pop("x")
    names = list(weights)
    upd = {n: _adamw(n, weights[n], grads[n], m_in[n], v_in[n]) for n in names}
    return (loss, grad_x, *[grads[n] for n in names], *[upd[n][0] for n in names], *[upd[n][1] for n in names],
            *[upd[n][2] for n in names])


def _step_grads(x, c, positions, loss_target, w):
    xi, yi, ci = lax.axis_index("x"), lax.axis_index("y"), lax.axis_index("c")
    chip = 2 * xi + yi
    dev = 2 * chip + ci
    place_idx = jnp.stack([ci, chip]).astype(jnp.int32)
    t, d = x.shape[1], x.shape[2]
    dl = d // N_CHIPS
    r_kv, r_q = w["w_dkv"].shape[1], w["w_dq"].shape[2]
    n_ada, n_kvada = w["w_ada_mix"].shape[2], w["w_ada_kv"].shape[1]
    heads_l = w["w_uq"].shape[2] // (QK_NOPE + QK_ROPE)
    assert dl == r_kv == r_q and dl % LANES == 0 and heads_l * N_CHIPS % 2 == 0

    def chip_cols(a, width):
        return lax.dynamic_slice_in_dim(a, chip * width, width, axis=a.ndim - 1)

    pack_rows = [w["conv_dw"][0], w["conv_dw_b"], w["conv_ln_g"], w["conv_ln_b"], w["conv_b_out"],
                 w["conv_b_in"].reshape(2, dl), c.reshape(N_CHIPS, dl)]
    n_pack = sum(p.shape[0] for p in pack_rows)
    pad_rows = (-n_pack) % 8
    pack = jnp.concatenate(pack_rows + [jnp.zeros((pad_rows, dl), F32)], axis=0)
    packs = _all_gather_small("gather_params", pack)
    by_chip = packs[0::2]

    def full_width(r0, nr):
        return jnp.transpose(by_chip[:, r0:r0 + nr, :], (1, 0, 2)).reshape(nr, d)

    cw = CONV_WIDTH
    dw_full, dwb_full = full_width(0, cw), full_width(cw, 1)
    lng_full, lnb_full, bout_full = full_width(cw + 1, 1), full_width(cw + 2, 1), full_width(cw + 3, 1)
    bin_full = jnp.transpose(by_chip[:, cw + 4:cw + 6, :], (0, 1, 2)).reshape(1, 2 * d)
    c_all = packs[:, cw + 6:cw + 10, :].reshape(N_DEV, d)

    sc_all = _ew("silu_c", lambda a: (a * jax.nn.sigmoid(a),), [c_all], (F32,))[0]
    ada_w = [w["w_ada_mix"][0], w["w_ada_mix"][1], w["w_ada_mlp"][0], w["w_ada_mlp"][1], w["w_ada_kv"]]
    ada_b = [chip_cols(w["b_ada_mix"][0:1], n_ada), chip_cols(w["b_ada_mix"][1:2], n_ada),
             chip_cols(w["b_ada_mlp"][0:1], n_ada), chip_cols(w["b_ada_mlp"][1:2], n_ada),
             chip_cols(w["b_ada_kv"].reshape(1, -1), n_kvada)]
    mods = [_mm_nn("ada_fwd_%d" % i, sc_all, wi, "row", extras=(jnp.broadcast_to(bi, (N_DEV, bi.shape[1])),),
                   epilogue=_add_epilogue) for i, (wi, bi) in enumerate(zip(ada_w, ada_b))]
    mods_all = _all_gather_small("gather_mods", jnp.concatenate(mods, axis=1))
    mine = lax.dynamic_index_in_dim(mods_all[0::2], dev, axis=1, keepdims=False)
    offs = [0]
    for m_ in mods:
        offs.append(offs[-1] + m_.shape[1])
    mod_vec = [mine[:, offs[i]:offs[i + 1]].reshape(1, -1) for i in range(5)]
    split3 = lambda v: (v[:, :d], v[:, d:2 * d], v[:, 2 * d:])
    mix = [split3(mod_vec[0]), split3(mod_vec[1])]
    mlp = [split3(mod_vec[2]), split3(mod_vec[3])]
    kv_shift, kv_scale = mod_vec[4][:, :d], mod_vec[4][:, d:]

    uq = w["w_uq"][0].reshape(r_q, heads_l, QK_NOPE + QK_ROPE)
    uq_nope, uq_rope = uq[:, :, :QK_NOPE].reshape(r_q, -1), uq[:, :, QK_NOPE:].reshape(r_q, -1)
    kr_pad = jnp.pad(w["w_kr"], ((0, 0), (0, LANES - QK_ROPE)))
    small = jnp.concatenate([w["w_dkv"], kr_pad, w["w_dq"][0], w["w_uk"], w["w_uv"], uq_nope, uq_rope], axis=1)
    widths = [r_kv, LANES, r_q, w["w_uk"].shape[1], w["w_uv"].shape[1], uq_nope.shape[1], uq_rope.shape[1]]
    so = [0]
    for wd in widths:
        so.append(so[-1] + wd)
    big = dict(conv_w_in=w["conv_w_in"][0], conv_w_out=w["conv_w_out"][0], w_o=w["w_o"][0],
               up0=w["mlp_w_up"][0], up1=w["mlp_w_up"][1], down0=w["mlp_w_down"][0], down1=w["mlp_w_down"][1],
               small=small)
    big_names = list(big)
    gathered = dict(zip(big_names, _gather_weights([_cast_into_slot(n, big[n], place_idx) for n in big_names])))
    rowl = lambda a: a.reshape(-1, a.shape[2])
    gs = gathered["small"]
    w_dkvkr = rowl(gs[:, :, so[0]:so[2]])
    w_dq_f = rowl(gs[:, :, so[2]:so[3]])
    w_uk_f, w_uv_f = gs[:, :, so[3]:so[4]], gs[:, :, so[4]:so[5]]
    w_qn_f, w_qr_f = gs[:, :, so[5]:so[6]], gs[:, :, so[6]:so[7]]
    w_in_f, w_out_f, w_o_f = gathered["conv_w_in"], rowl(gathered["conv_w_out"]), rowl(gathered["w_o"])
    w_up_f = [gathered["up0"], gathered["up1"]]
    w_down_f = [rowl(gathered["down0"]), rowl(gathered["down1"])]

    cos_t, sin_t = _rope_tables(positions, t)
    vec = lambda a: a.reshape(1, -1)
    gpm, gqm = w["g_pre_mix"], w["g_post_mix"]
    gpl, gql = w["g_pre_mlp"], w["g_post_mlp"]
    h0 = x[0]

    def mlp_fwd(l, h):
        sh, sc, _ = mlp[l]
        (hn,) = _rw_fwd("mlp_pre_%d" % l, _f_pre, [h], [vec(gpl[l]), sh, sc], (BF,))
        a, s = _mm_nn("mlp_up_%d" % l, hn, w_up_f[l], "col", o_dtypes=(F32, BF), epilogue=_relu2_epilogue)
        y = _mm_nn("mlp_down_%d" % l, s, w_down_f[l], "row")
        return hn, a, s, y

    def post_fwd(name, h, y, gate, g, bias=None):
        if bias is None:
            return _rw_fwd(name, lambda h_, y_, gt, g_: (h_ + _f_post(y_, gt, g_)[0],), [h, y], [gate, g], (F32,))[0]
        return _rw_fwd(name, lambda h_, y_, b_, gt, g_: (h_ + _f_post_bias(y_, b_, gt, g_)[0],), [h, y],
                       [bias, gate, g], (F32,))[0]

    (hn0,) = _rw_fwd("conv_pre", _f_pre, [h0], [vec(gpm[0]), mix[0][0], mix[0][1]], (BF,))
    z0 = _mm_nn("conv_in", hn0, w_in_f, "col")
    (u0,) = _rw_fwd("conv_glu", _f_glu, [z0], [bin_full], (F32,))
    uc0 = _conv_fwd(u0, dw_full, dwb_full)
    (n0,) = _rw_fwd("conv_ln", _f_ln_silu, [uc0], [lng_full, lnb_full], (BF,))
    y0 = _mm_nn("conv_out", n0, w_out_f, "row")
    h1 = post_fwd("conv_post", h0, y0, mix[0][2], vec(gqm[0]), bias=bout_full)
    hn1, a1, s1, y1 = mlp_fwd(0, h1)
    h2 = post_fwd("mlp_post_0", h1, y1, mlp[0][2], vec(gql[0]))

    kvn, hn2 = _rw_fwd("mla_pre", _f_pre2, [h2], [vec(w["g_kv"]), kv_shift, kv_scale, vec(gpm[1]), mix[1][0], mix[1][1]],
                       (BF, BF))
    pre_kv = _mm_nn("mla_dkv", kvn, w_dkvkr, "row")
    pre_q = _mm_nn("mla_dq", hn2, w_dq_f, "row")
    f_lat = _make_f_lat(r_kv)
    c_kv, kr, c_q = _rw_fwd("mla_latent", f_lat, [pre_kv, pre_q, cos_t, sin_t], [vec(w["g_ckv"]), vec(w["g_cq"][0])],
                            (BF, BF, BF))
    kn = _mm_nn("mla_uk", c_kv, w_uk_f, "col", o_dtypes=(BF,))
    vv = _mm_nn("mla_uv", c_kv, w_uv_f, "col", o_dtypes=(BF,))
    qn = _mm_nn("mla_uq_nope", c_q, w_qn_f, "col", o_dtypes=(BF,))
    qr_pre = _mm_nn("mla_uq_rope", c_q, w_qr_f, "col")
    (qr,) = _rw_fwd("mla_q_rope", _f_rope, [qr_pre, cos_t, sin_t], [], (BF,))
    att = _attn_fwd(qn, qr, kn, kr, vv)
    y2 = _mm_nn("mla_o", att, w_o_f, "row")
    h3 = post_fwd("mla_post", h2, y2, mix[1][2], vec(gqm[1]))
    hn3, a3, s3, y3 = mlp_fwd(1, h3)
    h4 = post_fwd("mlp_post_1", h3, y3, mlp[1][2], vec(gql[1]))

    dh4, loss_part = _loss_grad(h4, loss_target[0])
    loss = lax.psum(loss_part[0, 0], ("x", "y", "c"))

    gw = {}
    gvec = {}

    def mlp_bwd(l, h_in, hn, a, s, y, dh):
        sh, sc, gate = mlp[l]
        (dy,), (dgate, dgq) = _rw_bwd("mlp_post_bwd_%d" % l, _f_post, [y], [gate, vec(gql[l])], [dh], [BF])
        gw["down%d" % l] = _mm_tn("mlp_down_dw_%d" % l, s, dy, "row")
        da = _mm_nt("mlp_down_dx_%d" % l, dy, w_down_f[l], "row", o_dtypes=(BF,), extras=(a,),
                    epilogue=_relu2_bwd_epilogue)
        gw["up%d" % l] = _mm_tn("mlp_up_dw_%d" % l, hn, da, "col")
        dhn = _mm_nt("mlp_up_dx_%d" % l, da, w_up_f[l], "col")
        (dh_in,), (dgp, dsh, dsc) = _rw_bwd("mlp_pre_bwd_%d" % l, _f_pre, [h_in], [vec(gpl[l]), sh, sc], [dhn], [F32],
                                            add_rows={0: dh})
        gvec["mlp%d" % l] = (dsh, dsc, dgate)
        gvec["g_pre_mlp%d" % l], gvec["g_post_mlp%d" % l] = dgp, dgq
        return dh_in

    dh3 = mlp_bwd(1, h3, hn3, a3, s3, y3, dh4)

    (dy2,), (dgate, dgq) = _rw_bwd("mla_post_bwd", _f_post, [y2], [mix[1][2], vec(gqm[1])], [dh3], [BF])
    gvec["g_post_mix1"] = dgq
    gw["w_o"] = _mm_tn("mla_o_dw", att, dy2, "row")
    datt = _mm_nt("mla_o_dx", dy2, w_o_f, "row", o_dtypes=(BF,))
    dqn, dqr, dkn, dvv, dkr = _attn_bwd(qn, qr, kn, kr, vv, datt)
    (dqr_pre,), _ = _rw_bwd("mla_q_rope_bwd", _f_rope, [qr_pre, cos_t, sin_t], [], [dqr], [BF, None, None])
    g_qn = _mm_tn("mla_uq_nope_dw", c_q, dqn, "col")
    g_qr = _mm_tn("mla_uq_rope_dw", c_q, dqr_pre, "col")
    dc_q = _mm_nt("mla_uq_nope_dx", dqn, w_qn_f, "col")
    dc_q = _mm_nt("mla_uq_rope_dx", dqr_pre, w_qr_f, "col", extras=(dc_q,), epilogue=_add_epilogue)
    g_uk = _mm_tn("mla_uk_dw", c_kv, dkn, "col")
    g_uv = _mm_tn("mla_uv_dw", c_kv, dvv, "col")
    dc_kv = _mm_nt("mla_uk_dx", dkn, w_uk_f, "col")
    dc_kv = _mm_nt("mla_uv_dx", dvv, w_uv_f, "col", extras=(dc_kv,), epilogue=_add_epilogue)
    (dpre_kv, dpre_q), (dg_ckv, dg_cq) = _rw_bwd(
        "mla_latent_bwd", f_lat, [pre_kv, pre_q, cos_t, sin_t], [vec(w["g_ckv"]), vec(w["g_cq"][0])],
        [dc_kv, dkr, dc_q], [BF, BF, None, None])
    gvec["g_ckv"], gvec["g_cq"] = dg_ckv, dg_cq
    g_dkvkr = _mm_tn("mla_dkv_dw", kvn, dpre_kv, "row")
    g_dq = _mm_tn("mla_dq_dw", hn2, dpre_q, "row")
    dkvn = _mm_nt("mla_dkv_dx", dpre_kv, w_dkvkr, "row")
    dhn2 = _mm_nt("mla_dq_dx", dpre_q, w_dq_f, "row")
    (dh2,), (dg_kv, dkvsh, dkvsc, dgp, dsh, dsc) = _rw_bwd(
        "mla_pre_bwd", _f_pre2, [h2], [vec(w["g_kv"]), kv_shift, kv_scale, vec(gpm[1]), mix[1][0], mix[1][1]],
        [dkvn, dhn2], [F32], add_rows={0: dh3})
    gvec["mix1"] = (dsh, dsc, dgate)
    gvec["kv"] = (dkvsh, dkvsc)
    gvec["g_kv"], gvec["g_pre_mix1"] = dg_kv, dgp
    chunked = lambda a: a.reshape(N_CHIPS, a.shape[0] // N_CHIPS, a.shape[1])
    gw["small"] = jnp.concatenate([chunked(g_dkvkr), chunked(g_dq), g_uk, g_uv, g_qn, g_qr], axis=2)

    dh1 = mlp_bwd(0, h1, hn1, a1, s1, y1, dh2)

    (dy0,), (dbout, dgate, dgq) = _rw_bwd("conv_post_bwd", _f_post_bias, [y0], [bout_full, mix[0][2], vec(gqm[0])],
                                          [dh1], [BF])
    gvec["g_post_mix0"] = dgq
    gw["conv_w_out"] = _mm_tn("conv_out_dw", n0, dy0, "row")
    dn0 = _mm_nt("conv_out_dx", dy0, w_out_f, "row")
    (duc0,), (dlng, dlnb) = _rw_bwd("conv_ln_bwd", _f_ln_silu, [uc0], [lng_full, lnb_full], [dn0], [F32])
    du0, ddw = _conv_bwd(u0, duc0, dw_full)
    (dz0,), (dbin,) = _rw_bwd("conv_glu_bwd", _f_glu, [z0], [bin_full], [du0], [BF])
    gw["conv_w_in"] = _mm_tn("conv_in_dw", hn0, dz0, "col")
    dhn0 = _mm_nt("conv_in_dx", dz0, w_in_f, "col")
    (dx,), (dgp, dsh, dsc) = _rw_bwd("conv_pre_bwd", _f_pre, [h0], [vec(gpm[0]), mix[0][0], mix[0][1]], [dhn0], [F32],
                                     add_rows={0: dh1})
    gvec["mix0"] = (dsh, dsc, dgate)
    gvec["g_pre_mix0"] = dgp

    vec_list = [*gvec["mix0"], *gvec["mix1"], *gvec["mlp0"], *gvec["mlp1"], *gvec["kv"],
                gvec["g_pre_mix0"], gvec["g_pre_mix1"], gvec["g_post_mix0"], gvec["g_post_mix1"],
                gvec["g_pre_mlp0"], gvec["g_pre_mlp1"], gvec["g_post_mlp0"], gvec["g_post_mlp1"],
                gvec["g_kv"], gvec["g_ckv"], gvec["g_cq"], dbin, dlng, dlnb, dbout, ddw.reshape(1, -1)]
    vo = [0]
    for v_ in vec_list:
        vo.append(vo[-1] + v_.shape[1])
    vec_pad = (-vo[-1]) % (8 * LANES)
    n_vec = vo[-1] + vec_pad
    flat = jnp.concatenate(vec_list + [jnp.zeros((1, vec_pad), F32)], axis=1).reshape(8, n_vec // 8)
    all_vecs = _all_gather_small("gather_vector_grads", flat).reshape(N_DEV, 8, n_vec // 8)
    summed = _sum_devices(all_vecs).reshape(1, n_vec)
    per_dev = all_vecs.reshape(N_DEV, n_vec)
    seg = lambda a, i: a[:, vo[i]:vo[i + 1]]

    out = {"loss": loss, "x": dx.reshape(x.shape)}
    dm_mix = [jnp.concatenate([seg(per_dev, 3 * l + i) for i in range(3)], axis=1) for l in range(2)]
    dm_mlp = [jnp.concatenate([seg(per_dev, 6 + 3 * l + i) for i in range(3)], axis=1) for l in range(2)]
    dm_kv = jnp.concatenate([seg(per_dev, 12), seg(per_dev, 13)], axis=1)
    ada_dw = lambda name, dm, width: _mm_tn(name, sc_all, chip_cols(dm, width), "row")
    out["w_ada_mix"] = jnp.stack([ada_dw("ada_mix_dw_%d" % l, dm_mix[l], n_ada) for l in range(2)])
    out["w_ada_mlp"] = jnp.stack([ada_dw("ada_mlp_dw_%d" % l, dm_mlp[l], n_ada) for l in range(2)])
    out["w_ada_kv"] = ada_dw("ada_kv_dw", dm_kv, n_kvada)
    sum_seg = lambda i: seg(summed, i)
    out["b_ada_mix"] = jnp.concatenate([jnp.concatenate([sum_seg(3 * l + i) for i in range(3)], axis=1) for l in range(2)], axis=0)
    out["b_ada_mlp"] = jnp.concatenate([jnp.concatenate([sum_seg(6 + 3 * l + i) for i in range(3)], axis=1) for l in range(2)], axis=0)
    out["b_ada_kv"] = jnp.concatenate([sum_seg(12), sum_seg(13)], axis=1).reshape(-1)
    out["g_pre_mix"] = jnp.concatenate([sum_seg(14), sum_seg(15)], axis=0)
    out["g_post_mix"] = jnp.concatenate([sum_seg(16), sum_seg(17)], axis=0)
    out["g_pre_mlp"] = jnp.concatenate([sum_seg(18), sum_seg(19)], axis=0)
    out["g_post_mlp"] = jnp.concatenate([sum_seg(20), sum_seg(21)], axis=0)
    out["g_kv"] = sum_seg(22).reshape(-1)
    out["g_ckv"] = sum_seg(23).reshape(-1)
    out["g_cq"] = sum_seg(24)
    out["conv_b_in"] = chip_cols(sum_seg(25), 2 * dl)
    out["conv_ln_g"] = chip_cols(sum_seg(26), dl)
    out["conv_ln_b"] = chip_cols(sum_seg(27), dl)
    out["conv_b_out"] = chip_cols(sum_seg(28), dl)
    ddw_sum = chip_cols(sum_seg(29).reshape(CONV_WIDTH + 1, d), dl)
    out["conv_dw"] = ddw_sum[:CONV_WIDTH].reshape(1, CONV_WIDTH, dl)
    out["conv_dw_b"] = ddw_sum[CONV_WIDTH:]

    to_chunks = lambda a: a if a.ndim == 3 else chunked(a)
    red = dict(zip(big_names, _reduce_scatter([to_chunks(gw[n]) for n in big_names], place_idx)))
    out["conv_w_in"] = red["conv_w_in"][None]
    out["conv_w_out"] = red["conv_w_out"][None]
    out["w_o"] = red["w_o"][None]
    out["mlp_w_up"] = jnp.stack([red["up0"], red["up1"]])
    out["mlp_w_down"] = jnp.stack([red["down0"], red["down1"]])
    rs = red["small"]
    piece = lambda i: rs[:, so[i]:so[i + 1]]
    out["w_dkv"] = piece(0)
    out["w_kr"] = piece(1)[:, :QK_ROPE]
    out["w_dq"] = piece(2)[None]
    out["w_uk"], out["w_uv"] = piece(3), piece(4)
    out["w_uq"] = jnp.concatenate([piece(5).reshape(r_q, heads_l, QK_NOPE), piece(6).reshape(r_q, heads_l, QK_ROPE)],
                                  axis=2).reshape(1, r_q, -1)
    return out
```

```python
import functools

import jax
import jax.numpy as jnp
from jax import lax
from jax.experimental import pallas as pl
from jax.experimental.pallas import tpu as pltpu

F32 = jnp.float32
BF = jnp.bfloat16
MXU_DTYPE = BF

EPS = 1e-6
NEG = -1e30
ROPE_THETA = 10000.0
QK_NOPE = 128
QK_ROPE = 64
V_HEAD = 128
CONV_WIDTH = 31
ADAM_LR, ADAM_B1, ADAM_B2, ADAM_EPS, ADAM_WD, ADAM_STEP = 0.001, 0.9, 0.999, 1e-08, 0.01, 10

N_CHIPS = 4
N_DEV = 8
LANES = 128
VMEM_LIMIT = 56 * 2 ** 20
MESH = pl.DeviceIdType.MESH
ANY = pl.BlockSpec(memory_space=pl.ANY)


def _params(sem=None):
    return pltpu.CompilerParams(dimension_semantics=sem, vmem_limit_bytes=VMEM_LIMIT)


def _tile(n, pref, unit=LANES):
    if n <= pref:
        return n
    t = (pref // unit) * unit
    while t > unit and n % t:
        t -= unit
    assert n % t == 0, (n, pref)
    return t


def _dg(a, b, ca, cb):
    return lax.dot_general(a.astype(MXU_DTYPE), b.astype(MXU_DTYPE), (((ca,), (cb,)), ((), ())),
                           preferred_element_type=F32)


@jax.custom_vjp
def _dot_nn(a, b):
    return _dg(a, b, 1, 0)


def _dot_nn_fwd(a, b):
    return _dg(a, b, 1, 0), (a, b)


def _dot_nn_bwd(res, g):
    a, b = res
    return _dg(g, b, 1, 1).astype(a.dtype), _dg(a, g, 0, 0).astype(b.dtype)


_dot_nn.defvjp(_dot_nn_fwd, _dot_nn_bwd)


@jax.custom_vjp
def _dot_nt(a, b):
    return _dg(a, b, 1, 1)


def _dot_nt_fwd(a, b):
    return _dg(a, b, 1, 1), (a, b)


def _dot_nt_bwd(res, g):
    a, b = res
    return _dg(g, b, 1, 0).astype(a.dtype), _dg(g, a, 0, 0).astype(b.dtype)


_dot_nt.defvjp(_dot_nt_fwd, _dot_nt_bwd)


def _matmul(name, a, b, *, ca, cb, grid, a_blk, a_map, b_blk, b_map, o_shape, o_dtypes, o_blk, o_map,
            extras=(), epilogue=None):
    nk = grid[-1]
    n_ex, n_out = len(extras), len(o_dtypes)

    def body(*refs):
        a_ref, b_ref = refs[0], refs[1]
        ex_refs = refs[2:2 + n_ex]
        out_refs = refs[2 + n_ex:2 + n_ex + n_out]
        kk = pl.program_id(len(grid) - 1)

        def finish(acc):
            outs = (acc,) if epilogue is None else epilogue(acc, *[r[...] for r in ex_refs])
            for r, o in zip(out_refs, outs):
                r[...] = o.astype(r.dtype)

        part = _dg(a_ref[...], b_ref[...], ca, cb)
        if nk == 1:
            finish(part)
        else:
            acc_ref = refs[-1]

            @pl.when(kk == 0)
            def _():
                acc_ref[...] = part

            @pl.when(kk > 0)
            def _():
                acc_ref[...] += part

            @pl.when(kk == nk - 1)
            def _():
                finish(acc_ref[...])

    o_spec = pl.BlockSpec(o_blk, o_map)
    acc_shape = tuple(d for d in o_blk if d is not None)
    res = pl.pallas_call(
        body, grid=grid,
        in_specs=[pl.BlockSpec(a_blk, a_map), pl.BlockSpec(b_blk, b_map)] + [o_spec] * n_ex,
        out_specs=[o_spec] * n_out,
        out_shape=[jax.ShapeDtypeStruct(o_shape, dt) for dt in o_dtypes],
        scratch_shapes=[] if nk == 1 else [pltpu.VMEM(acc_shape, F32)],
        compiler_params=_params(("parallel",) * (len(grid) - 1) + ("arbitrary",)),
        name=name,
    )(a, b, *extras)
    return res[0] if n_out == 1 else res


TM, TN, TK = 1024, 512, 1024


def _mm_nn(name, a, w, layout, o_dtypes=(F32,), extras=(), epilogue=None):
    m, k = a.shape
    tm, tk = _tile(m, TM, 8), _tile(k, TK)
    if layout == "row":
        n = w.shape[1]
        tn = _tile(n, TN)
        grid = (m // tm, n // tn, k // tk)
        b_blk, b_map = (tk, tn), (lambda i, j, kk: (kk, j))
    else:
        nl = w.shape[2]
        n = nl * w.shape[0]
        tn = _tile(nl, TN)
        nb = nl // tn
        grid = (m // tm, n // tn, k // tk)
        b_blk, b_map = (None, tk, tn), (lambda i, j, kk: (j // nb, kk, j % nb))
    return _matmul(name, a, w, ca=1, cb=0, grid=grid, a_blk=(tm, tk), a_map=lambda i, j, kk: (i, kk),
                   b_blk=b_blk, b_map=b_map, o_shape=(m, n), o_dtypes=o_dtypes, o_blk=(tm, tn),
                   o_map=lambda i, j, kk: (i, j), extras=extras, epilogue=epilogue)


def _mm_nt(name, g, w, layout, o_dtypes=(F32,), extras=(), epilogue=None):
    m, n = g.shape
    tm, tn = _tile(m, TM, 8), None
    if layout == "row":
        k = w.shape[0]
        tn = _tile(n, TK)
        tk = _tile(k, TN)
        grid = (m // tm, k // tk, n // tn)
        b_blk, b_map = (tk, tn), (lambda i, j, kk: (j, kk))
    else:
        k, nl = w.shape[1], w.shape[2]
        tn = _tile(nl, TK)
        nb = nl // tn
        tk = _tile(k, TN)
        grid = (m // tm, k // tk, n // tn)
        b_blk, b_map = (None, tk, tn), (lambda i, j, kk: (kk // nb, j, kk % nb))
    return _matmul(name, g, w, ca=1, cb=1, grid=grid, a_blk=(tm, tn), a_map=lambda i, j, kk: (i, kk),
                   b_blk=b_blk, b_map=b_map, o_shape=(m, k), o_dtypes=o_dtypes, o_blk=(tm, tk),
                   o_map=lambda i, j, kk: (i, j), extras=extras, epilogue=epilogue)


def _mm_tn(name, x, g, layout, o_dtype=F32):
    t, k = x.shape
    n = g.shape[1]
    tt = _tile(t, TK, 8)
    tk = _tile(k, TM)
    if layout == "row":
        tn = _tile(n, TN)
        o_shape, o_blk, o_map = (k, n), (tk, tn), (lambda i, j, kk: (i, j))
    else:
        nl = n // N_CHIPS
        tn = _tile(nl, TN)
        nb = nl // tn
        o_shape, o_blk, o_map = (N_CHIPS, k, nl), (None, tk, tn), (lambda i, j, kk: (j // nb, i, j % nb))
    grid = (k // tk, n // tn, t // tt)
    return _matmul(name, x, g, ca=0, cb=0, grid=grid, a_blk=(tt, tk), a_map=lambda i, j, kk: (kk, i),
                   b_blk=(tt, tn), b_map=lambda i, j, kk: (kk, j), o_shape=o_shape, o_dtypes=(o_dtype,),
                   o_blk=o_blk, o_map=o_map)


TR = 256


def _rw_fwd(name, fn, rows, vecs, o_dtypes, tr=TR):
    t = rows[0].shape[0]
    tr = min(tr, t)
    n_r, n_v = len(rows), len(vecs)
    o_sds = jax.eval_shape(fn, *[jax.ShapeDtypeStruct((tr, r.shape[1]), r.dtype) for r in rows],
                           *[jax.ShapeDtypeStruct(v.shape, v.dtype) for v in vecs])

    def body(*refs):
        outs = fn(*[r[...] for r in refs[:n_r + n_v]])
        for r, o in zip(refs[n_r + n_v:], outs):
            r[...] = o.astype(r.dtype)

    res = pl.pallas_call(
        body, grid=(t // tr,),
        in_specs=[pl.BlockSpec((tr, r.shape[1]), lambda i: (i, 0)) for r in rows]
        + [pl.BlockSpec(v.shape, lambda i: (0, 0)) for v in vecs],
        out_specs=[pl.BlockSpec((tr, o.shape[1]), lambda i: (i, 0)) for o in o_sds],
        out_shape=[jax.ShapeDtypeStruct((t, o.shape[1]), dt) for o, dt in zip(o_sds, o_dtypes)],
        compiler_params=_params(("parallel",)), name=name,
    )(*rows, *vecs)
    return res


def _rw_bwd(name, fn, rows, vecs, cots, row_grad_dtypes, add_rows=None, tr=TR):
    t = rows[0].shape[0]
    tr = min(tr, t)
    add_rows = add_rows or {}
    n_r, n_v, n_c = len(rows), len(vecs), len(cots)
    diff = [i for i, dt in enumerate(row_grad_dtypes) if dt is not None]
    adds = [add_rows[i] for i in diff if i in add_rows]
    n_a = len(adds)

    def body(*refs):
        rvals = [r[...] for r in refs[:n_r]]
        vvals = [r[...] for r in refs[n_r:n_r + n_v]]
        c_refs = refs[n_r + n_v:n_r + n_v + n_c]
        a_refs = list(refs[n_r + n_v + n_c:n_r + n_v + n_c + n_a])
        o_refs = refs[n_r + n_v + n_c + n_a:]

        def f(*d):
            full = list(rvals)
            for i, x in zip(diff, d[:len(diff)]):
                full[i] = x
            return fn(*full, *d[len(diff):])

        outs, vjp = jax.vjp(f, *[rvals[i] for i in diff], *vvals)
        grads = vjp(tuple(c[...].astype(o.dtype) for c, o in zip(c_refs, outs)))
        for n, i in enumerate(diff):
            gr = grads[n].astype(F32)
            if i in add_rows:
                gr = gr + a_refs.pop(0)[...].astype(F32)
            o_refs[n][...] = gr.astype(o_refs[n].dtype)
        first = pl.program_id(0) == 0
        for r, gv in zip(o_refs[len(diff):], grads[len(diff):]):
            @pl.when(first)
            def _(r=r, gv=gv):
                r[...] = gv

            @pl.when(jnp.logical_not(first))
            def _(r=r, gv=gv):
                r[...] += gv

    row_spec = lambda a: pl.BlockSpec((tr, a.shape[1]), lambda i: (i, 0))
    vec_spec = lambda a: pl.BlockSpec(a.shape, lambda i: (0, 0))
    res = pl.pallas_call(
        body, grid=(t // tr,),
        in_specs=[row_spec(r) for r in rows] + [vec_spec(v) for v in vecs] + [row_spec(c) for c in cots]
        + [row_spec(a) for a in adds],
        out_specs=[row_spec(rows[i]) for i in diff] + [vec_spec(v) for v in vecs],
        out_shape=[jax.ShapeDtypeStruct(rows[i].shape, row_grad_dtypes[i]) for i in diff]
        + [jax.ShapeDtypeStruct(v.shape, F32) for v in vecs],
        compiler_params=_params(("arbitrary",)), name=name,
    )(*rows, *vecs, *cots, *adds)
    return res[:len(diff)], res[len(diff):]


def _rms(x, g):
    return x * lax.rsqrt(jnp.mean(x * x, axis=-1, keepdims=True) + EPS) * g


def _f_pre(h, g, shift, scale):
    return (_rms(h, g) * (1.0 + scale) + shift,)


def _f_pre2(h, g1, sh1, sc1, g2, sh2, sc2):
    return _f_pre(h, g1, sh1, sc1) + _f_pre(h, g2, sh2, sc2)


def _f_post(y, gate, g):
    return (gate * _rms(y, g),)


def _f_post_bias(y, bias, gate, g):
    return (gate * _rms(y + bias, g),)


def _f_glu(z, bias):
    z = z + bias
    half = z.shape[1] // 2
    return (z[:, :half] * jax.nn.sigmoid(z[:, half:]),)


def _f_ln_silu(u, g, b):
    mu = jnp.mean(u, axis=-1, keepdims=True)
    var = jnp.mean(jnp.square(u - mu), axis=-1, keepdims=True)
    y = (u - mu) * lax.rsqrt(var + EPS) * g + b
    return (y * jax.nn.sigmoid(y),)


def _rope_raw(x, cos, sin):
    n = x.shape[1]
    reps = n // LANES
    if reps > 1:
        cos = jnp.concatenate([cos] * reps, axis=1)
        sin = jnp.concatenate([sin] * reps, axis=1)
    lane = lax.broadcasted_iota(jnp.int32, x.shape, 1)
    half = QK_ROPE // 2
    partner = jnp.where((lane % QK_ROPE) < half, pltpu.roll(x, n - half, 1), pltpu.roll(x, half, 1))
    return x * cos + partner * sin


@jax.custom_vjp
def _rope(x, cos, sin):
    return _rope_raw(x, cos, sin)


def _rope_fwd(x, cos, sin):
    return _rope_raw(x, cos, sin), (cos, sin)


def _rope_bwd(res, g):
    cos, sin = res
    return _rope_raw(g, cos, -sin), jnp.zeros_like(cos), jnp.zeros_like(sin)


_rope.defvjp(_rope_fwd, _rope_bwd)


def _make_f_lat(r_kv):
    def f(pre_kv, pre_q, cos, sin, g_ckv, g_cq):
        c_kv = _rms(pre_kv[:, :r_kv], g_ckv)
        kr = _rope(pre_kv[:, r_kv:], cos, sin)
        return c_kv, kr, _rms(pre_q, g_cq)
    return f


def _f_rope(x, cos, sin):
    return (_rope(x, cos, sin),)


CONV_TC = 128
CONV_TT = 256
PADR = 32


def _conv_fwd(u, dw, dw_b):
    t, d = u.shape
    tc, tt = min(CONV_TC, d), min(CONV_TT, t)
    off = PADR - (CONV_WIDTH - 1)

    def body(u_ref, w_ref, b_ref, o_ref, pad_ref):
        pad_ref[pl.ds(0, PADR), :] = jnp.zeros((PADR, tc), F32)
        pad_ref[pl.ds(PADR, t), :] = u_ref[...]
        for t0 in range(0, t, tt):
            acc = jnp.zeros((tt, tc), F32) + b_ref[...]
            for j in range(CONV_WIDTH):
                acc = acc + pad_ref[pl.ds(t0 + off + j, tt), :] * w_ref[pl.ds(j, 1), :]
            o_ref[pl.ds(t0, tt), :] = acc

    return pl.pallas_call(
        body, grid=(d // tc,),
        in_specs=[pl.BlockSpec((t, tc), lambda j: (0, j)), pl.BlockSpec((CONV_WIDTH, tc), lambda j: (0, j)),
                  pl.BlockSpec((1, tc), lambda j: (0, j))],
        out_specs=pl.BlockSpec((t, tc), lambda j: (0, j)),
        out_shape=jax.ShapeDtypeStruct((t, d), F32),
        scratch_shapes=[pltpu.VMEM((t + PADR, tc), F32)],
        compiler_params=_params(("parallel",)), name="conv_fwd",
    )(u, dw, dw_b)


def _conv_bwd(u, duc, dw):
    t, d = u.shape
    tc, tt = min(CONV_TC, d), min(CONV_TT, t)
    off = PADR - (CONV_WIDTH - 1)

    def body(u_ref, g_ref, w_ref, du_ref, dwt_ref, padu_ref, padg_ref):
        padu_ref[pl.ds(0, PADR), :] = jnp.zeros((PADR, tc), F32)
        padu_ref[pl.ds(PADR, t), :] = u_ref[...]
        padg_ref[pl.ds(t, PADR), :] = jnp.zeros((PADR, tc), F32)
        padg_ref[pl.ds(0, t), :] = g_ref[...]
        for t0 in range(0, t, tt):
            acc = jnp.zeros((tt, tc), F32)
            for j in range(CONV_WIDTH):
                acc = acc + padg_ref[pl.ds(t0 + (CONV_WIDTH - 1) - j, tt), :] * w_ref[pl.ds(j, 1), :]
            du_ref[pl.ds(t0, tt), :] = acc
        for j in range(CONV_WIDTH):
            acc = jnp.zeros((tt, tc), F32)
            for t0 in range(0, t, tt):
                acc = acc + g_ref[pl.ds(t0, tt), :] * padu_ref[pl.ds(t0 + off + j, tt), :]
            dwt_ref[pl.ds(j, 1), :] = jnp.sum(acc, axis=0, keepdims=True)
        acc = jnp.zeros((tt, tc), F32)
        for t0 in range(0, t, tt):
            acc = acc + g_ref[pl.ds(t0, tt), :]
        dwt_ref[pl.ds(CONV_WIDTH, 1), :] = jnp.sum(acc, axis=0, keepdims=True)

    col = lambda r: pl.BlockSpec((r, tc), lambda j: (0, j))
    return pl.pallas_call(
        body, grid=(d // tc,),
        in_specs=[col(t), col(t), col(CONV_WIDTH)],
        out_specs=[col(t), col(CONV_WIDTH + 1)],
        out_shape=[jax.ShapeDtypeStruct((t, d), F32), jax.ShapeDtypeStruct((CONV_WIDTH + 1, d), F32)],
        scratch_shapes=[pltpu.VMEM((t + PADR, tc), F32), pltpu.VMEM((t + PADR, tc), F32)],
        compiler_params=_params(("parallel",)), name="conv_bwd",
    )(u, duc, dw)


@jax.custom_vjp
def _swap_halves(x):
    return pltpu.roll(x, LANES // 2, 1)


_swap_halves.defvjp(lambda x: (pltpu.roll(x, LANES // 2, 1), None), lambda _, g: (pltpu.roll(g, LANES // 2, 1),))


def _attn_block(qn, qr, kn, kr, v, q0):
    scale = (QK_NOPE + QK_ROPE) ** -0.5
    lane = lax.broadcasted_iota(jnp.int32, kr.shape, 1)
    kr_a = kr * (lane < QK_ROPE).astype(kr.dtype)
    kr_b = _swap_halves(kr_a)
    outs = []
    for hh, kr_h in ((0, kr_a), (1, kr_b)):
        sl = slice(hh * QK_NOPE, (hh + 1) * QK_NOPE)
        s = (_dot_nt(qn[:, sl], kn[:, sl]) + _dot_nt(qr, kr_h)) * scale
        row = lax.broadcasted_iota(jnp.int32, s.shape, 0) + q0
        col = lax.broadcasted_iota(jnp.int32, s.shape, 1)
        s = jnp.where(col <= row, s, NEG)
        e = jnp.exp(s - jnp.max(s, axis=-1, keepdims=True))
        p = e / jnp.sum(e, axis=-1, keepdims=True)
        outs.append(_dot_nn(p, v[:, sl]))
    return jnp.concatenate(outs, axis=1)


def _attn_fwd(qn, qr, kn, kr, v, tq=512):
    t, w = qn.shape
    pairs = w // (2 * QK_NOPE)
    tq = min(tq, t)
    pw = 2 * QK_NOPE

    def body(qn_ref, qr_ref, kn_ref, kr_ref, v_ref, o_ref):
        for q0 in range(0, t, tq):
            l = q0 + tq
            o_ref[pl.ds(q0, tq), :] = _attn_block(
                qn_ref[pl.ds(q0, tq), :], qr_ref[pl.ds(q0, tq), :], kn_ref[pl.ds(0, l), :], kr_ref[pl.ds(0, l), :],
                v_ref[pl.ds(0, l), :], q0).astype(o_ref.dtype)

    pair = lambda wd: pl.BlockSpec((t, wd), lambda p: (0, p))
    return pl.pallas_call(
        body, grid=(pairs,),
        in_specs=[pair(pw), pair(LANES), pair(pw), pl.BlockSpec((t, LANES), lambda p: (0, 0)), pair(pw)],
        out_specs=pair(pw), out_shape=jax.ShapeDtypeStruct((t, w), BF),
        compiler_params=_params(("parallel",)), name="attn_fwd",
    )(qn, qr, kn, kr, v)


def _attn_bwd(qn, qr, kn, kr, v, do, tq=256):
    t, w = qn.shape
    pairs = w // (2 * QK_NOPE)
    tq = min(tq, t)
    pw = 2 * QK_NOPE

    def body(qn_ref, qr_ref, kn_ref, kr_ref, v_ref, do_ref, dqn_ref, dqr_ref, dkn_ref, dv_ref, dkr_ref,
             akn_ref, av_ref, akr_ref):
        akn_ref[...] = jnp.zeros_like(akn_ref)
        av_ref[...] = jnp.zeros_like(av_ref)
        akr_ref[...] = jnp.zeros_like(akr_ref)
        for q0 in range(0, t, tq):
            l = q0 + tq
            rows, keys = pl.ds(q0, tq), pl.ds(0, l)
            _, vjp = jax.vjp(functools.partial(_attn_block, q0=q0), qn_ref[rows, :], qr_ref[rows, :],
                             kn_ref[keys, :], kr_ref[keys, :], v_ref[keys, :])
            dqn, dqr, dkn, dkr, dv = vjp(do_ref[rows, :].astype(F32))
            dqn_ref[rows, :] = dqn.astype(dqn_ref.dtype)
            dqr_ref[rows, :] = dqr.astype(dqr_ref.dtype)
            akn_ref[keys, :] += dkn.astype(F32)
            av_ref[keys, :] += dv.astype(F32)
            akr_ref[keys, :] += dkr.astype(F32)
        dkn_ref[...] = akn_ref[...].astype(dkn_ref.dtype)
        dv_ref[...] = av_ref[...].astype(dv_ref.dtype)
        first = pl.program_id(0) == 0

        @pl.when(first)
        def _():
            dkr_ref[...] = akr_ref[...]

        @pl.when(jnp.logical_not(first))
        def _():
            dkr_ref[...] += akr_ref[...]

    pair = lambda wd: pl.BlockSpec((t, wd), lambda p: (0, p))
    shared = pl.BlockSpec((t, LANES), lambda p: (0, 0))
    sds = jax.ShapeDtypeStruct
    return pl.pallas_call(
        body, grid=(pairs,),
        in_specs=[pair(pw), pair(LANES), pair(pw), shared, pair(pw), pair(pw)],
        out_specs=[pair(pw), pair(LANES), pair(pw), pair(pw), shared],
        out_shape=[sds((t, w), BF), sds((t, pairs * LANES), F32), sds((t, w), BF), sds((t, w), BF), sds((t, LANES), F32)],
        scratch_shapes=[pltpu.VMEM((t, pw), F32), pltpu.VMEM((t, pw), F32), pltpu.VMEM((t, LANES), F32)],
        compiler_params=_params(("arbitrary",)), name="attn_bwd",
    )(qn, qr, kn, kr, v, do)


def _ew(name, fn, ins, o_dtypes, max_bytes=2 ** 21):
    r, c = ins[0].shape
    tr = r
    if r * c * 4 > max_bytes:
        tr = max(16, (max_bytes // (c * 4)) // 16 * 16)
        while r % tr:
            tr -= 16
    n_in = len(ins)

    def body(*refs):
        outs = fn(*[x[...] for x in refs[:n_in]])
        for o_ref, o in zip(refs[n_in:], outs):
            o_ref[...] = o.astype(o_ref.dtype)

    spec = pl.BlockSpec((tr, c), lambda i: (i, 0))
    return pl.pallas_call(
        body, grid=(r // tr,), in_specs=[spec] * n_in, out_specs=[spec] * len(o_dtypes),
        out_shape=[jax.ShapeDtypeStruct((r, c), dt) for dt in o_dtypes],
        compiler_params=_params(("parallel",)), name=name,
    )(*ins)


def _adamw_math(w, g, m, v):
    m = ADAM_B1 * m + (1.0 - ADAM_B1) * g
    v = ADAM_B2 * v + (1.0 - ADAM_B2) * jnp.square(g)
    m_hat = m / (1.0 - ADAM_B1 ** ADAM_STEP)
    v_hat = v / (1.0 - ADAM_B2 ** ADAM_STEP)
    delta = -ADAM_LR * (m_hat / (jnp.sqrt(v_hat) + ADAM_EPS) + ADAM_WD * w)
    return delta, m, v


def _adamw(name, w, g, m, v):
    shape = w.shape
    to2 = lambda a: a.reshape(-1, shape[-1]) if a.ndim > 1 else a.reshape(1, -1)
    d, nm, nv = _ew("adamw_" + name, _adamw_math, [to2(w), to2(g), to2(m), to2(v)], (F32, F32, F32))
    return d.reshape(shape), nm.reshape(shape), nv.reshape(shape)


def _rope_tables(positions, t):
    half = QK_ROPE // 2
    inv = 1.0 / (ROPE_THETA ** (jnp.arange(0, QK_ROPE, 2, dtype=F32) / QK_ROPE))
    inv_l = jnp.tile(inv, LANES // half).reshape(1, LANES)
    sign = jnp.tile(jnp.concatenate([-jnp.ones((half,), F32), jnp.ones((half,), F32)]), LANES // QK_ROPE).reshape(1, LANES)

    def body(p_ref, inv_ref, sg_ref, c_ref, s_ref):
        ang = p_ref[...].astype(F32) * inv_ref[...]
        c_ref[...] = jnp.cos(ang)
        s_ref[...] = jnp.sin(ang) * sg_ref[...]

    return pl.pallas_call(body, out_shape=[jax.ShapeDtypeStruct((t, LANES), F32)] * 2, name="rope_tables")(
        positions.reshape(t, 1), inv_l, sign)


def _loss_grad(h, target):
    t, d = h.shape
    tr = min(TR, t)

    def body(h_ref, y_ref, dh_ref, l_ref):
        err = h_ref[...] - y_ref[...]
        dh_ref[...] = err * (1.0 / d)
        part = 0.5 * jnp.sum(jnp.mean(jnp.square(err), axis=-1, keepdims=True), axis=0, keepdims=True)
        first = pl.program_id(0) == 0

        @pl.when(first)
        def _():
            l_ref[...] = part

        @pl.when(jnp.logical_not(first))
        def _():
            l_ref[...] += part

    row = pl.BlockSpec((tr, d), lambda i: (i, 0))
    return pl.pallas_call(
        body, grid=(t // tr,), in_specs=[row, row],
        out_specs=[row, pl.BlockSpec((1, 1), lambda i: (0, 0))],
        out_shape=[jax.ShapeDtypeStruct((t, d), F32), jax.ShapeDtypeStruct((1, 1), F32)],
        compiler_params=_params(("arbitrary",)), name="loss_grad",
    )(h, target)


def _sum_devices(g):
    def fn(*parts):
        acc = parts[0]
        for p in parts[1:]:
            acc = acc + p
        return (acc,)
    return _ew("sum_devices", fn, [g[i] for i in range(g.shape[0])], (F32,))[0]


def _place():
    x, y, c = lax.axis_index("x"), lax.axis_index("y"), lax.axis_index("c")
    return x, y, c, [(1 - x, y), (x, 1 - y), (1 - x, 1 - y)]


def _all_gather_small(name, v):
    r, n = v.shape

    def body(x_ref, out_ref, send_sems, recv_sems, local_sem):
        x, y, c, chips = _place()
        me, sibling = (x, y, c), (x, y, 1 - c)

        def rows(px, py, pc):
            return out_ref.at[4 * px + 2 * py + pc]

        def copy(k, block, to, src=None):
            return pltpu.make_async_remote_copy(
                src_ref=rows(*block) if src is None else src, dst_ref=rows(*block), send_sem=send_sems.at[k],
                recv_sem=recv_sems.at[k], device_id=to, device_id_type=MESH)

        mine = pltpu.make_async_copy(x_ref, rows(*me), local_sem)
        mine.start()
        first = [copy(0, me, sibling, src=x_ref)]
        first += [copy(1 + j, me, (*chip, c), src=x_ref) for j, chip in enumerate(chips)]
        for cp in first:
            cp.start()
        passed = [copy(4 + j, (*chip, c), sibling) for j, chip in enumerate(chips)]
        for j, chip in enumerate(chips):
            copy(1 + j, (*chip, c), me).wait_recv()
            passed[j].start()
        copy(0, sibling, me).wait_recv()
        for j, chip in enumerate(chips):
            copy(4 + j, (*chip, 1 - c), me).wait_recv()
        for cp in first + passed:
            cp.wait_send()
        mine.wait()

    return pl.pallas_call(
        body, out_shape=jax.ShapeDtypeStruct((N_DEV, r, n), v.dtype),
        in_specs=[pl.BlockSpec(memory_space=pltpu.VMEM)], out_specs=pl.BlockSpec(memory_space=pltpu.VMEM),
        scratch_shapes=[pltpu.SemaphoreType.DMA((7,)), pltpu.SemaphoreType.DMA((7,)), pltpu.SemaphoreType.DMA],
        compiler_params=pltpu.CompilerParams(vmem_limit_bytes=VMEM_LIMIT), name=name,
    )(v)


def _cast_into_slot(name, w, idx):
    r, n = w.shape
    tr = r
    while tr * n * 4 > 2 ** 21 and tr % 32 == 0:
        tr //= 2

    def body(idx_ref, w_ref, o_ref):
        o_ref[...] = w_ref[...].astype(o_ref.dtype)

    return pl.pallas_call(
        body,
        grid_spec=pltpu.PrefetchScalarGridSpec(
            num_scalar_prefetch=1, grid=(r // tr,),
            in_specs=[pl.BlockSpec((tr, n), lambda i, idx_ref: (i, 0))],
            out_specs=pl.BlockSpec((None, tr, n), lambda i, idx_ref: (idx_ref[1], i, 0))),
        out_shape=jax.ShapeDtypeStruct((N_CHIPS, r, n), BF),
        compiler_params=_params(("parallel",)), name="cast_" + name,
    )(idx, w)


def _gather_weights(bufs):
    n_w = len(bufs)

    def body(*refs):
        outs = refs[n_w:2 * n_w]
        send_sems, recv_sems = refs[2 * n_w:]
        x, y, c, chips = _place()

        def copy(i, s, chip, which, to):
            h = outs[i].shape[1] // 2
            blk = outs[i].at[2 * chip[0] + chip[1], pl.ds(which * h, h)]
            return pltpu.make_async_remote_copy(
                src_ref=blk, dst_ref=blk, send_sem=send_sems.at[6 * i + s], recv_sem=recv_sems.at[6 * i + s],
                device_id=to, device_id_type=MESH)

        sends = []
        for i in range(n_w):
            for j, chip in enumerate(chips):
                cp = copy(i, j, (x, y), c, (*chip, c))
                cp.start()
                sends.append(cp)
        for i in range(n_w):
            for j, chip in enumerate(chips):
                copy(i, j, chip, c, (x, y, c)).wait_recv()
                cp = copy(i, 3 + j, chip, c, (x, y, 1 - c))
                cp.start()
                sends.append(cp)
        for i in range(n_w):
            for j, chip in enumerate(chips):
                copy(i, 3 + j, chip, 1 - c, (x, y, c)).wait_recv()
        for cp in sends:
            cp.wait_send()

    return pl.pallas_call(
        body, in_specs=[ANY] * n_w, out_specs=[ANY] * n_w,
        out_shape=[jax.ShapeDtypeStruct(b.shape, b.dtype) for b in bufs],
        input_output_aliases={i: i for i in range(n_w)},
        scratch_shapes=[pltpu.SemaphoreType.DMA((6 * n_w,)), pltpu.SemaphoreType.DMA((6 * n_w,))],
        name="gather_weights",
    )(*bufs)


def _swap_other_half(name, grads):
    n_w = len(grads)

    def body(*refs):
        ins, outs = refs[:n_w], refs[n_w:2 * n_w]
        send_sems, recv_sems = refs[2 * n_w:]
        x, y, c, _ = _place()
        cps = []
        for i in range(n_w):
            h = ins[i].shape[1] // 2
            cp = pltpu.make_async_remote_copy(
                src_ref=ins[i].at[:, pl.ds((1 - c) * h, h), :], dst_ref=outs[i], send_sem=send_sems.at[i],
                recv_sem=recv_sems.at[i], device_id=(x, y, 1 - c), device_id_type=MESH)
            cp.start()
            cps.append(cp)
        for cp in cps:
            cp.wait()

    return pl.pallas_call(
        body, in_specs=[ANY] * n_w, out_specs=[ANY] * n_w,
        out_shape=[jax.ShapeDtypeStruct((g.shape[0], g.shape[1] // 2, g.shape[2]), g.dtype) for g in grads],
        scratch_shapes=[pltpu.SemaphoreType.DMA((n_w,)), pltpu.SemaphoreType.DMA((n_w,))],
        name=name,
    )(*grads)


def _add_my_half(name, g, s, idx, o_dtype):
    nc, r, n = g.shape
    h = r // 2
    tr = h
    while tr * n * 4 > 2 ** 21 and tr % 32 == 0:
        tr //= 2
    nb = h // tr

    def body(idx_ref, g_ref, s_ref, o_ref, own_ref):
        val = (g_ref[...] + s_ref[...].astype(F32)).astype(o_ref.dtype)
        o_ref[...] = val

        @pl.when(pl.program_id(1) == idx_ref[1])
        def _():
            own_ref[...] = val

    return pl.pallas_call(
        body,
        grid_spec=pltpu.PrefetchScalarGridSpec(
            num_scalar_prefetch=1, grid=(nb, nc),
            in_specs=[pl.BlockSpec((None, tr, n), lambda i, j, idx_ref: (j, idx_ref[0] * nb + i, 0)),
                      pl.BlockSpec((None, tr, n), lambda i, j, idx_ref: (j, i, 0))],
            out_specs=[pl.BlockSpec((None, tr, n), lambda i, j, idx_ref: (j, i, 0)),
                       pl.BlockSpec((None, tr, n), lambda i, j, idx_ref: (idx_ref[1], i, 0))]),
        out_shape=[jax.ShapeDtypeStruct((nc, h, n), o_dtype)] * 2,
        compiler_params=_params(("parallel", "arbitrary")), name=name,
    )(idx, g, s)


def _scatter_partials(parts, bufs):
    n_w = len(parts)

    def body(*refs):
        ins, outs = refs[:n_w], refs[2 * n_w:3 * n_w]
        send_sems, recv_sems = refs[3 * n_w:]
        x, y, c, chips = _place()
        k = 2 * x + y
        sends = []
        for i in range(n_w):
            for j, chip in enumerate(chips):
                cp = pltpu.make_async_remote_copy(
                    src_ref=ins[i].at[2 * chip[0] + chip[1]], dst_ref=outs[i].at[k], send_sem=send_sems.at[3 * i + j],
                    recv_sem=recv_sems.at[3 * i + j], device_id=(*chip, c), device_id_type=MESH)
                cp.start()
                sends.append(cp)
        for i in range(n_w):
            for j, chip in enumerate(chips):
                blk = outs[i].at[2 * chip[0] + chip[1]]
                pltpu.make_async_remote_copy(
                    src_ref=blk, dst_ref=blk, send_sem=send_sems.at[3 * i + j], recv_sem=recv_sems.at[3 * i + j],
                    device_id=(*chip, c), device_id_type=MESH).wait_recv()
        for cp in sends:
            cp.wait_send()

    return pl.pallas_call(
        body, in_specs=[ANY] * (2 * n_w), out_specs=[ANY] * n_w,
        out_shape=[jax.ShapeDtypeStruct(b.shape, b.dtype) for b in bufs],
        input_output_aliases={n_w + i: i for i in range(n_w)},
        scratch_shapes=[pltpu.SemaphoreType.DMA((3 * n_w,)), pltpu.SemaphoreType.DMA((3 * n_w,))],
        name="scatter_partials",
    )(*parts, *bufs)


def _sum_chips(name, q, idx):
    nc, h, n = q.shape
    tr = h
    while tr * n * 4 > 2 ** 20 and tr % 32 == 0:
        tr //= 2
    nb = h // tr

    def body(idx_ref, q_ref, o_ref):
        acc = q_ref[0].astype(F32)
        for j in range(1, nc):
            acc = acc + q_ref[j].astype(F32)
        o_ref[...] = acc

    return pl.pallas_call(
        body,
        grid_spec=pltpu.PrefetchScalarGridSpec(
            num_scalar_prefetch=1, grid=(nb,),
            in_specs=[pl.BlockSpec((nc, tr, n), lambda i, idx_ref: (0, i, 0))],
            out_specs=pl.BlockSpec((tr, n), lambda i, idx_ref: (idx_ref[0] * nb + i, 0))),
        out_shape=jax.ShapeDtypeStruct((2 * h, n), F32),
        compiler_params=_params(("parallel",)), name=name,
    )(idx, q)


def _join_halves(name, bufs):
    n_w = len(bufs)

    def body(*refs):
        outs = refs[n_w:2 * n_w]
        send_sems, recv_sems = refs[2 * n_w:]
        x, y, c, _ = _place()

        def copy(i, which):
            h = outs[i].shape[0] // 2
            rows = outs[i].at[pl.ds(which * h, h)]
            return pltpu.make_async_remote_copy(
                src_ref=rows, dst_ref=rows, send_sem=send_sems.at[i], recv_sem=recv_sems.at[i],
                device_id=(x, y, 1 - c), device_id_type=MESH)

        cps = [copy(i, c) for i in range(n_w)]
        for cp in cps:
            cp.start()
        for i, cp in enumerate(cps):
            cp.wait_send()
            copy(i, 1 - c).wait_recv()

    return pl.pallas_call(
        body, in_specs=[ANY] * n_w, out_specs=[ANY] * n_w,
        out_shape=[jax.ShapeDtypeStruct(b.shape, b.dtype) for b in bufs],
        input_output_aliases={i: i for i in range(n_w)},
        scratch_shapes=[pltpu.SemaphoreType.DMA((n_w,)), pltpu.SemaphoreType.DMA((n_w,))],
        name=name,
    )(*bufs)


HBM_SPEC = pl.BlockSpec(memory_space=pltpu.HBM)
SEM_SPEC = pl.BlockSpec(memory_space=pltpu.SEMAPHORE)
VMEM_SPEC = pl.BlockSpec(memory_space=pltpu.VMEM)
SIDE_EFFECT = pltpu.SideEffectType.DATAFLOW_SIDE_EFFECTING


def _in_hbm(arrays):
    return [pltpu.with_memory_space_constraint(a, pltpu.HBM) for a in arrays]


def _ici_gather_copy(ref, i, j, chip, send_sems, recv_sems):
    x, y, c, _ = _place()
    h = ref.shape[1] // 2
    mine = ref.at[2 * x + y, pl.ds(c * h, h)]
    return pltpu.make_async_remote_copy(
        src_ref=mine, dst_ref=mine, send_sem=send_sems.at[3 * i + j], recv_sem=recv_sems.at[3 * i + j],
        device_id=(*chip, c), device_id_type=MESH)


def _gather_start(name, bufs):
    n_w = len(bufs)

    def body(*refs):
        ins, send_sems, recv_sems, token = refs[:n_w], refs[n_w], refs[n_w + 1], refs[-1]
        chips = _place()[3]
        for i in range(n_w):
            for j, chip in enumerate(chips):
                _ici_gather_copy(ins[i], i, j, chip, send_sems, recv_sems).start()
        token[...] = jnp.zeros_like(token)

    res = pl.pallas_call(
        body, name=name, in_specs=[HBM_SPEC] * n_w,
        out_shape=(pltpu.SemaphoreType.DMA((3 * n_w,)), pltpu.SemaphoreType.DMA((3 * n_w,)),
                   *[pltpu.HBM(b.shape, b.dtype) for b in bufs], jax.ShapeDtypeStruct((8, LANES), F32)),
        out_specs=(SEM_SPEC, SEM_SPEC, *[HBM_SPEC] * n_w, VMEM_SPEC),
        input_output_aliases={i: 2 + i for i in range(n_w)},
        compiler_params=pltpu.CompilerParams(has_side_effects=SIDE_EFFECT),
    )(*_in_hbm(bufs))
    return res[0], res[1], list(res[2:2 + n_w]), res[-1]


def _gather_wait(name, send_sems, recv_sems, bufs, after):
    n_w = len(bufs)

    def body(*refs):
        ins, s_sems, r_sems = refs[:n_w], refs[n_w], refs[n_w + 1]
        chips = _place()[3]
        for i in range(n_w):
            for j, chip in enumerate(chips):
                cp = _ici_gather_copy(ins[i], i, j, chip, s_sems, r_sems)
                cp.wait_send()
                cp.wait_recv()

    return pl.pallas_call(
        body, name=name, in_specs=[HBM_SPEC] * n_w + [SEM_SPEC, SEM_SPEC, ANY],
        out_shape=[pltpu.HBM(b.shape, b.dtype) for b in bufs], out_specs=[HBM_SPEC] * n_w,
        input_output_aliases={i: i for i in range(n_w)},
        compiler_params=pltpu.CompilerParams(has_side_effects=SIDE_EFFECT),
    )(*bufs, send_sems, recv_sems, after)


def _forward_to_sibling(name, bufs):
    n_w = len(bufs)

    def body(*refs):
        outs = refs[n_w:2 * n_w]
        send_sems, recv_sems = refs[2 * n_w:]
        x, y, c, chips = _place()

        def copy(i, j, chip, which):
            h = outs[i].shape[1] // 2
            blk = outs[i].at[2 * chip[0] + chip[1], pl.ds(which * h, h)]
            return pltpu.make_async_remote_copy(
                src_ref=blk, dst_ref=blk, send_sem=send_sems.at[3 * i + j], recv_sem=recv_sems.at[3 * i + j],
                device_id=(x, y, 1 - c), device_id_type=MESH)

        sends = [copy(i, j, chip, c) for i in range(n_w) for j, chip in enumerate(chips)]
        for cp in sends:
            cp.start()
        for i in range(n_w):
            for j, chip in enumerate(chips):
                copy(i, j, chip, 1 - c).wait_recv()
        for cp in sends:
            cp.wait_send()

    return pl.pallas_call(
        body, in_specs=[ANY] * n_w, out_specs=[ANY] * n_w,
        out_shape=[jax.ShapeDtypeStruct(b.shape, b.dtype) for b in bufs],
        input_output_aliases={i: i for i in range(n_w)},
        scratch_shapes=[pltpu.SemaphoreType.DMA((3 * n_w,)), pltpu.SemaphoreType.DMA((3 * n_w,))],
        name=name,
    )(*bufs)


def _ici_scatter_copy(part, buf, i, j, chip, send_sems, recv_sems):
    x, y, c, _ = _place()
    return pltpu.make_async_remote_copy(
        src_ref=part.at[2 * chip[0] + chip[1]], dst_ref=buf.at[2 * x + y], send_sem=send_sems.at[3 * i + j],
        recv_sem=recv_sems.at[3 * i + j], device_id=(*chip, c), device_id_type=MESH)


def _ici_scatter_landing(buf, i, j, chip, send_sems, recv_sems):
    x, y, c, _ = _place()
    blk = buf.at[2 * chip[0] + chip[1]]
    return pltpu.make_async_remote_copy(
        src_ref=blk, dst_ref=blk, send_sem=send_sems.at[3 * i + j], recv_sem=recv_sems.at[3 * i + j],
        device_id=(*chip, c), device_id_type=MESH)


def _scatter_start(name, parts, bufs):
    n_w = len(parts)

    def body(*refs):
        ps, bs = refs[:n_w], refs[n_w:2 * n_w]
        send_sems, recv_sems, token = refs[2 * n_w], refs[2 * n_w + 1], refs[-1]
        chips = _place()[3]
        for i in range(n_w):
            for j, chip in enumerate(chips):
                _ici_scatter_copy(ps[i], bs[i], i, j, chip, send_sems, recv_sems).start()
        token[...] = jnp.zeros_like(token)

    both = list(parts) + list(bufs)
    res = pl.pallas_call(
        body, name=name, in_specs=[HBM_SPEC] * (2 * n_w),
        out_shape=(pltpu.SemaphoreType.DMA((3 * n_w,)), pltpu.SemaphoreType.DMA((3 * n_w,)),
                   *[pltpu.HBM(b.shape, b.dtype) for b in both], jax.ShapeDtypeStruct((8, LANES), F32)),
        out_specs=(SEM_SPEC, SEM_SPEC, *[HBM_SPEC] * (2 * n_w), VMEM_SPEC),
        input_output_aliases={i: 2 + i for i in range(2 * n_w)},
        compiler_params=pltpu.CompilerParams(has_side_effects=SIDE_EFFECT),
    )(*_in_hbm(both))
    return res[0], res[1], list(res[2:2 + n_w]), list(res[2 + n_w:2 + 2 * n_w]), res[-1]


def _scatter_wait(name, send_sems, recv_sems, parts, bufs, after):
    n_w = len(parts)

    def body(*refs):
        ps, bs = refs[:n_w], refs[n_w:2 * n_w]
        s_sems, r_sems = refs[2 * n_w], refs[2 * n_w + 1]
        chips = _place()[3]
        for i in range(n_w):
            for j, chip in enumerate(chips):
                _ici_scatter_copy(ps[i], bs[i], i, j, chip, s_sems, r_sems).wait_send()
                _ici_scatter_landing(bs[i], i, j, chip, s_sems, r_sems).wait_recv()

    both = list(parts) + list(bufs)
    res = pl.pallas_call(
        body, name=name, in_specs=[HBM_SPEC] * (2 * n_w) + [SEM_SPEC, SEM_SPEC, ANY],
        out_shape=[pltpu.HBM(b.shape, b.dtype) for b in both], out_specs=[HBM_SPEC] * (2 * n_w),
        input_output_aliases={i: i for i in range(2 * n_w)},
        compiler_params=pltpu.CompilerParams(has_side_effects=SIDE_EFFECT),
    )(*both, send_sems, recv_sems, after)
    return list(res[n_w:])


def _reduce_scatter_begin(tag, grads, idx):
    from_sibling = _swap_other_half("swap_other_half_" + tag, grads)
    pairs = [_add_my_half("add_my_half_%s_%d" % (tag, i), g, s, idx, BF)
             for i, (g, s) in enumerate(zip(grads, from_sibling))]
    s_sems, r_sems, parts, bufs, token = _scatter_start("scatter_start_" + tag, [p[0] for p in pairs],
                                                        [p[1] for p in pairs])
    return (tag, s_sems, r_sems, parts, bufs, idx), token


def _reduce_scatter_end(state, after):
    tag, s_sems, r_sems, parts, bufs, idx = state
    received = _scatter_wait("scatter_wait_" + tag, s_sems, r_sems, parts, bufs, after)
    halves = [_sum_chips("sum_chips_%s_%d" % (tag, i), q, idx) for i, q in enumerate(received)]
    return _join_halves("join_halves_" + tag, halves)


def _relu2_epilogue(acc):
    r = jnp.maximum(acc, 0.0)
    return acc, r * r


def _relu2_bwd_epilogue(acc, a):
    return (acc * (2.0 * jnp.maximum(a, 0.0)),)


def _add_epilogue(acc, other):
    return (acc + other,)


def kernel(x, c, positions, w_ada_mix, b_ada_mix, w_ada_mlp, b_ada_mlp, g_pre_mix, g_post_mix, g_pre_mlp, g_post_mlp, conv_w_in, conv_b_in, conv_dw, conv_dw_b, conv_ln_g, conv_ln_b, conv_w_out, conv_b_out, w_ada_kv, b_ada_kv, g_kv, w_dkv, g_ckv, w_kr, w_uk, w_uv, w_dq, g_cq, w_uq, w_o, mlp_w_up, mlp_w_down, loss_target, m_w_ada_mix, m_b_ada_mix, m_w_ada_mlp, m_b_ada_mlp, m_g_pre_mix, m_g_post_mix, m_g_pre_mlp, m_g_post_mlp, m_conv_w_in, m_conv_b_in, m_conv_dw, m_conv_dw_b, m_conv_ln_g, m_conv_ln_b, m_conv_w_out, m_conv_b_out, m_w_ada_kv, m_b_ada_kv, m_g_kv, m_w_dkv, m_g_ckv, m_w_kr, m_w_uk, m_w_uv, m_w_dq, m_g_cq, m_w_uq, m_w_o, m_mlp_w_up, m_mlp_w_down, v_w_ada_mix, v_b_ada_mix, v_w_ada_mlp, v_b_ada_mlp, v_g_pre_mix, v_g_post_mix, v_g_pre_mlp, v_g_post_mlp, v_conv_w_in, v_conv_b_in, v_conv_dw, v_conv_dw_b, v_conv_ln_g, v_conv_ln_b, v_conv_w_out, v_conv_b_out, v_w_ada_kv, v_b_ada_kv, v_g_kv, v_w_dkv, v_g_ckv, v_w_kr, v_w_uk, v_w_uv, v_w_dq, v_g_cq, v_w_uq, v_w_o, v_mlp_w_up, v_mlp_w_down):
    weights = dict(w_ada_mix=w_ada_mix, b_ada_mix=b_ada_mix, w_ada_mlp=w_ada_mlp, b_ada_mlp=b_ada_mlp, g_pre_mix=g_pre_mix, g_post_mix=g_post_mix, g_pre_mlp=g_pre_mlp, g_post_mlp=g_post_mlp, conv_w_in=conv_w_in, conv_b_in=conv_b_in, conv_dw=conv_dw, conv_dw_b=conv_dw_b, conv_ln_g=conv_ln_g, conv_ln_b=conv_ln_b, conv_w_out=conv_w_out, conv_b_out=conv_b_out, w_ada_kv=w_ada_kv, b_ada_kv=b_ada_kv, g_kv=g_kv, w_dkv=w_dkv, g_ckv=g_ckv, w_kr=w_kr, w_uk=w_uk, w_uv=w_uv, w_dq=w_dq, g_cq=g_cq, w_uq=w_uq, w_o=w_o, mlp_w_up=mlp_w_up, mlp_w_down=mlp_w_down)
    m_in = dict(w_ada_mix=m_w_ada_mix, b_ada_mix=m_b_ada_mix, w_ada_mlp=m_w_ada_mlp, b_ada_mlp=m_b_ada_mlp, g_pre_mix=m_g_pre_mix, g_post_mix=m_g_post_mix, g_pre_mlp=m_g_pre_mlp, g_post_mlp=m_g_post_mlp, conv_w_in=m_conv_w_in, conv_b_in=m_conv_b_in, conv_dw=m_conv_dw, conv_dw_b=m_conv_dw_b, conv_ln_g=m_conv_ln_g, conv_ln_b=m_conv_ln_b, conv_w_out=m_conv_w_out, conv_b_out=m_conv_b_out, w_ada_kv=m_w_ada_kv, b_ada_kv=m_b_ada_kv, g_kv=m_g_kv, w_dkv=m_w_dkv, g_ckv=m_g_ckv, w_kr=m_w_kr, w_uk=m_w_uk, w_uv=m_w_uv, w_dq=m_w_dq, g_cq=m_g_cq, w_uq=m_w_uq, w_o=m_w_o, mlp_w_up=m_mlp_w_up, mlp_w_down=m_mlp_w_down)
    v_in = dict(w_ada_mix=v_w_ada_mix, b_ada_mix=v_b_ada_mix, w_ada_mlp=v_w_ada_mlp, b_ada_mlp=v_b_ada_mlp, g_pre_mix=v_g_pre_mix, g_post_mix=v_g_post_mix, g_pre_mlp=v_g_pre_mlp, g_post_mlp=v_g_post_mlp, conv_w_in=v_conv_w_in, conv_b_in=v_conv_b_in, conv_dw=v_conv_dw, conv_dw_b=v_conv_dw_b, conv_ln_g=v_conv_ln_g, conv_ln_b=v_conv_ln_b, conv_w_out=v_conv_w_out, conv_b_out=v_conv_b_out, w_ada_kv=v_w_ada_kv, b_ada_kv=v_b_ada_kv, g_kv=v_g_kv, w_dkv=v_w_dkv, g_ckv=v_g_ckv, w_kr=v_w_kr, w_uk=v_w_uk, w_uv=v_w_uv, w_dq=v_w_dq, g_cq=v_g_cq, w_uq=v_w_uq, w_o=v_w_o, mlp_w_up=v_mlp_w_up, mlp_w_down=v_mlp_w_down)
    grads = _step_grads(x, c, positions, loss_target, weights)
    loss, grad_x = grads.pop("loss"), grads.pop("x")
    names = list(weights)
    upd = {n: _adamw(n, weights[n], grads[n], m_in[n], v_in[n]) for n in names}
    return (loss, grad_x, *[grads[n] for n in names], *[upd[n][0] for n in names], *[upd[n][1] for n in names],
            *[upd[n][2] for n in names])


def _step_grads(x, c, positions, loss_target, w):
    xi, yi, ci = lax.axis_index("x"), lax.axis_index("y"), lax.axis_index("c")
    chip = 2 * xi + yi
    dev = 2 * chip + ci
    place_idx = jnp.stack([ci, chip]).astype(jnp.int32)
    t, d = x.shape[1], x.shape[2]
    dl = d // N_CHIPS
    r_kv, r_q = w["w_dkv"].shape[1], w["w_dq"].shape[2]
    n_ada, n_kvada = w["w_ada_mix"].shape[2], w["w_ada_kv"].shape[1]
    heads_l = w["w_uq"].shape[2] // (QK_NOPE + QK_ROPE)
    assert dl == r_kv == r_q and dl % LANES == 0 and heads_l * N_CHIPS % 2 == 0

    def chip_cols(a, width):
        return lax.dynamic_slice_in_dim(a, chip * width, width, axis=a.ndim - 1)

    pack_rows = [w["conv_dw"][0], w["conv_dw_b"], w["conv_ln_g"], w["conv_ln_b"], w["conv_b_out"],
                 w["conv_b_in"].reshape(2, dl), c.reshape(N_CHIPS, dl)]
    pack_rows = [jnp.pad(p, ((0, (-p.shape[0]) % 8), (0, 0))) for p in pack_rows]
    po = [0]
    for p in pack_rows:
        po.append(po[-1] + p.shape[0])
    packs = _all_gather_small("gather_params", jnp.concatenate(pack_rows, axis=0))
    by_chip = packs[0::2]

    def full_width(r0, nr):
        return jnp.transpose(by_chip[:, r0:r0 + nr, :], (1, 0, 2)).reshape(nr, d)

    dw_full, dwb_full = full_width(po[0], CONV_WIDTH), full_width(po[1], 1)
    lng_full, lnb_full, bout_full = full_width(po[2], 1), full_width(po[3], 1), full_width(po[4], 1)
    bin_full = by_chip[:, po[5]:po[5] + 2, :].reshape(1, 2 * d)
    c_all = packs[:, po[6]:po[6] + N_CHIPS, :].reshape(N_DEV, d)

    sc_all = _ew("silu_c", lambda a: (a * jax.nn.sigmoid(a),), [c_all], (F32,))[0]
    ada_w = [w["w_ada_mix"][0], w["w_ada_mix"][1], w["w_ada_mlp"][0], w["w_ada_mlp"][1], w["w_ada_kv"]]
    ada_b = [chip_cols(w["b_ada_mix"][0:1], n_ada), chip_cols(w["b_ada_mix"][1:2], n_ada),
             chip_cols(w["b_ada_mlp"][0:1], n_ada), chip_cols(w["b_ada_mlp"][1:2], n_ada),
             chip_cols(w["b_ada_kv"].reshape(1, -1), n_kvada)]
    mods = [_mm_nn("ada_fwd_%d" % i, sc_all, wi, "row", extras=(jnp.broadcast_to(bi, (N_DEV, bi.shape[1])),),
                   epilogue=_add_epilogue) for i, (wi, bi) in enumerate(zip(ada_w, ada_b))]
    mods_all = _all_gather_small("gather_mods", jnp.concatenate(mods, axis=1))
    mine = lax.dynamic_index_in_dim(mods_all[0::2], dev, axis=1, keepdims=False)
    offs = [0]
    for m_ in mods:
        offs.append(offs[-1] + m_.shape[1])
    mod_vec = [mine[:, offs[i]:offs[i + 1]].reshape(1, -1) for i in range(5)]
    split3 = lambda v: (v[:, :d], v[:, d:2 * d], v[:, 2 * d:])
    mix = [split3(mod_vec[0]), split3(mod_vec[1])]
    mlp = [split3(mod_vec[2]), split3(mod_vec[3])]
    kv_shift, kv_scale = mod_vec[4][:, :d], mod_vec[4][:, d:]

    uq = w["w_uq"][0].reshape(r_q, heads_l, QK_NOPE + QK_ROPE)
    uq_nope, uq_rope = uq[:, :, :QK_NOPE].reshape(r_q, -1), uq[:, :, QK_NOPE:].reshape(r_q, -1)
    kr_pad = jnp.pad(w["w_kr"], ((0, 0), (0, LANES - QK_ROPE)))
    small = jnp.concatenate([w["w_dkv"], kr_pad, w["w_dq"][0], w["w_uk"], w["w_uv"], uq_nope, uq_rope], axis=1)
    widths = [r_kv, LANES, r_q, w["w_uk"].shape[1], w["w_uv"].shape[1], uq_nope.shape[1], uq_rope.shape[1]]
    so = [0]
    for wd in widths:
        so.append(so[-1] + wd)
    big = dict(conv_w_in=w["conv_w_in"][0], conv_w_out=w["conv_w_out"][0], w_o=w["w_o"][0],
               up0=w["mlp_w_up"][0], up1=w["mlp_w_up"][1], down0=w["mlp_w_down"][0], down1=w["mlp_w_down"][1],
               small=small)
    groups = dict(g0=["conv_w_in", "conv_w_out"], g1=["up0", "down0"], g2=["small", "w_o"], g3=["up1", "down1"])
    slots = {n: _cast_into_slot(n, big[n], place_idx) for n in big}
    gathered = dict(zip(groups["g0"], _gather_weights([slots[n] for n in groups["g0"]])))
    pending, started = {}, jnp.zeros((1, 1), F32)
    for gname in ("g1", "g2", "g3"):
        s_sems, r_sems, bufs, token = _gather_start("gather_start_" + gname, [slots[n] for n in groups[gname]])
        pending[gname] = (s_sems, r_sems, bufs)
        started = started + token[0:1, 0:1]

    def finish_gather(gname, after):
        s_sems, r_sems, bufs = pending[gname]
        bufs = _gather_wait("gather_wait_" + gname, s_sems, r_sems, bufs, after)
        gathered.update(zip(groups[gname], _forward_to_sibling("forward_" + gname, bufs)))

    rowl = lambda a: a.reshape(-1, a.shape[2])
    w_in_f, w_out_f = gathered["conv_w_in"], rowl(gathered["conv_w_out"])
    w_up_f = lambda l: gathered["up%d" % l]
    w_down_f = lambda l: rowl(gathered["down%d" % l])

    cos_t, sin_t = _rope_tables(positions, t)
    vec = lambda a: a.reshape(1, -1)
    gpm, gqm = w["g_pre_mix"], w["g_post_mix"]
    gpl, gql = w["g_pre_mlp"], w["g_post_mlp"]
    h0 = x[0]
    after_token = lambda v, token: v + token[0:1, 0:1]

    def mlp_fwd(l, h):
        sh, sc, _ = mlp[l]
        (hn,) = _rw_fwd("mlp_pre_%d" % l, _f_pre, [h], [vec(gpl[l]), sh, sc], (BF,))
        a, s = _mm_nn("mlp_up_%d" % l, hn, w_up_f(l), "col", o_dtypes=(F32, BF), epilogue=_relu2_epilogue)
        y = _mm_nn("mlp_down_%d" % l, s, w_down_f(l), "row")
        return hn, a, s, y

    def post_fwd(name, h, y, gate, g, bias=None):
        if bias is None:
            return _rw_fwd(name, lambda h_, y_, gt, g_: (h_ + _f_post(y_, gt, g_)[0],), [h, y], [gate, g], (F32,))[0]
        return _rw_fwd(name, lambda h_, y_, b_, gt, g_: (h_ + _f_post_bias(y_, b_, gt, g_)[0],), [h, y],
                       [bias, gate, g], (F32,))[0]

    (hn0,) = _rw_fwd("conv_pre", _f_pre, [h0], [vec(gpm[0]), mix[0][0] + started, mix[0][1]], (BF,))
    z0 = _mm_nn("conv_in", hn0, w_in_f, "col")
    (u0,) = _rw_fwd("conv_glu", _f_glu, [z0], [bin_full], (F32,))
    uc0 = _conv_fwd(u0, dw_full, dwb_full)
    (n0,) = _rw_fwd("conv_ln", _f_ln_silu, [uc0], [lng_full, lnb_full], (BF,))
    y0 = _mm_nn("conv_out", n0, w_out_f, "row")
    h1 = post_fwd("conv_post", h0, y0, mix[0][2], vec(gqm[0]), bias=bout_full)
    finish_gather("g1", h1)
    hn1, a1, s1, y1 = mlp_fwd(0, h1)
    h2 = post_fwd("mlp_post_0", h1, y1, mlp[0][2], vec(gql[0]))

    finish_gather("g2", h2)
    gs = gathered["small"]
    w_dkvkr = rowl(gs[:, :, so[0]:so[2]])
    w_dq_f = rowl(gs[:, :, so[2]:so[3]])
    w_uk_f, w_uv_f = gs[:, :, so[3]:so[4]], gs[:, :, so[4]:so[5]]
    w_qn_f, w_qr_f = gs[:, :, so[5]:so[6]], gs[:, :, so[6]:so[7]]
    w_o_f = rowl(gathered["w_o"])
    kvn, hn2 = _rw_fwd("mla_pre", _f_pre2, [h2], [vec(w["g_kv"]), kv_shift, kv_scale, vec(gpm[1]), mix[1][0], mix[1][1]],
                       (BF, BF))
    pre_kv = _mm_nn("mla_dkv", kvn, w_dkvkr, "row")
    pre_q = _mm_nn("mla_dq", hn2, w_dq_f, "row")
    f_lat = _make_f_lat(r_kv)
    c_kv, kr, c_q = _rw_fwd("mla_latent", f_lat, [pre_kv, pre_q, cos_t, sin_t], [vec(w["g_ckv"]), vec(w["g_cq"][0])],
                            (BF, BF, BF))
    kn = _mm_nn("mla_uk", c_kv, w_uk_f, "col", o_dtypes=(BF,))
    vv = _mm_nn("mla_uv", c_kv, w_uv_f, "col", o_dtypes=(BF,))
    qn = _mm_nn("mla_uq_nope", c_q, w_qn_f, "col", o_dtypes=(BF,))
    qr_pre = _mm_nn("mla_uq_rope", c_q, w_qr_f, "col")
    (qr,) = _rw_fwd("mla_q_rope", _f_rope, [qr_pre, cos_t, sin_t], [], (BF,))
    att = _attn_fwd(qn, qr, kn, kr, vv)
    y2 = _mm_nn("mla_o", att, w_o_f, "row")
    h3 = post_fwd("mla_post", h2, y2, mix[1][2], vec(gqm[1]))
    finish_gather("g3", h3)
    hn3, a3, s3, y3 = mlp_fwd(1, h3)
    h4 = post_fwd("mlp_post_1", h3, y3, mlp[1][2], vec(gql[1]))

    dh4, loss_part = _loss_grad(h4, loss_target[0])
    loss = lax.psum(loss_part[0, 0], ("x", "y", "c"))

    gw = {}
    gvec = {}

    def mlp_bwd(l, h_in, hn, a, s, y, dh, token=None):
        sh, sc, gate = mlp[l]
        if token is not None:
            gate = after_token(gate, token)
        (dy,), (dgate, dgq) = _rw_bwd("mlp_post_bwd_%d" % l, _f_post, [y], [gate, vec(gql[l])], [dh], [BF])
        gw["down%d" % l] = _mm_tn("mlp_down_dw_%d" % l, s, dy, "row")
        da = _mm_nt("mlp_down_dx_%d" % l, dy, w_down_f(l), "row", o_dtypes=(BF,), extras=(a,),
                    epilogue=_relu2_bwd_epilogue)
        gw["up%d" % l] = _mm_tn("mlp_up_dw_%d" % l, hn, da, "col")
        dhn = _mm_nt("mlp_up_dx_%d" % l, da, w_up_f(l), "col")
        (dh_in,), (dgp, dsh, dsc) = _rw_bwd("mlp_pre_bwd_%d" % l, _f_pre, [h_in], [vec(gpl[l]), sh, sc], [dhn], [F32],
                                            add_rows={0: dh})
        gvec["mlp%d" % l] = (dsh, dsc, dgate)
        gvec["g_pre_mlp%d" % l], gvec["g_post_mlp%d" % l] = dgp, dgq
        return dh_in

    chunked = lambda a: a.reshape(N_CHIPS, a.shape[0] // N_CHIPS, a.shape[1])
    to_chunks = lambda a: a if a.ndim == 3 else chunked(a)
    red = {}

    def reduce_begin(tag, names):
        state, token = _reduce_scatter_begin(tag, [to_chunks(gw[n]) for n in names], place_idx)
        return (names, state), token

    def reduce_end(handle, after):
        names, state = handle
        red.update(zip(names, _reduce_scatter_end(state, after)))

    dh3 = mlp_bwd(1, h3, hn3, a3, s3, y3, dh4)
    rs1, token = reduce_begin("r1", ["up1", "down1"])

    (dy2,), (dgate, dgq) = _rw_bwd("mla_post_bwd", _f_post, [y2], [after_token(mix[1][2], token), vec(gqm[1])],
                                   [dh3], [BF])
    gvec["g_post_mix1"] = dgq
    gw["w_o"] = _mm_tn("mla_o_dw", att, dy2, "row")
    datt = _mm_nt("mla_o_dx", dy2, w_o_f, "row", o_dtypes=(BF,))
    dqn, dqr, dkn, dvv, dkr = _attn_bwd(qn, qr, kn, kr, vv, datt)
    (dqr_pre,), _ = _rw_bwd("mla_q_rope_bwd", _f_rope, [qr_pre, cos_t, sin_t], [], [dqr], [BF, None, None])
    g_qn = _mm_tn("mla_uq_nope_dw", c_q, dqn, "col")
    g_qr = _mm_tn("mla_uq_rope_dw", c_q, dqr_pre, "col")
    dc_q = _mm_nt("mla_uq_nope_dx", dqn, w_qn_f, "col")
    dc_q = _mm_nt("mla_uq_rope_dx", dqr_pre, w_qr_f, "col", extras=(dc_q,), epilogue=_add_epilogue)
    g_uk = _mm_tn("mla_uk_dw", c_kv, dkn, "col")
    g_uv = _mm_tn("mla_uv_dw", c_kv, dvv, "col")
    dc_kv = _mm_nt("mla_uk_dx", dkn, w_uk_f, "col")
    dc_kv = _mm_nt("mla_uv_dx", dvv, w_uv_f, "col", extras=(dc_kv,), epilogue=_add_epilogue)
    (dpre_kv, dpre_q), (dg_ckv, dg_cq) = _rw_bwd(
        "mla_latent_bwd", f_lat, [pre_kv, pre_q, cos_t, sin_t], [vec(w["g_ckv"]), vec(w["g_cq"][0])],
        [dc_kv, dkr, dc_q], [BF, BF, None, None])
    gvec["g_ckv"], gvec["g_cq"] = dg_ckv, dg_cq
    g_dkvkr = _mm_tn("mla_dkv_dw", kvn, dpre_kv, "row")
    g_dq = _mm_tn("mla_dq_dw", hn2, dpre_q, "row")
    dkvn = _mm_nt("mla_dkv_dx", dpre_kv, w_dkvkr, "row")
    dhn2 = _mm_nt("mla_dq_dx", dpre_q, w_dq_f, "row")
    (dh2,), (dg_kv, dkvsh, dkvsc, dgp, dsh, dsc) = _rw_bwd(
        "mla_pre_bwd", _f_pre2, [h2], [vec(w["g_kv"]), kv_shift, kv_scale, vec(gpm[1]), mix[1][0], mix[1][1]],
        [dkvn, dhn2], [F32], add_rows={0: dh3})
    gvec["mix1"] = (dsh, dsc, dgate)
    gvec["kv"] = (dkvsh, dkvsc)
    gvec["g_kv"], gvec["g_pre_mix1"] = dg_kv, dgp
    gw["small"] = jnp.concatenate([chunked(g_dkvkr), chunked(g_dq), g_uk, g_uv, g_qn, g_qr], axis=2)
    reduce_end(rs1, dh2)
    rs2, token = reduce_begin("r2", ["small", "w_o"])

    dh1 = mlp_bwd(0, h1, hn1, a1, s1, y1, dh2, token=token)
    reduce_end(rs2, dh1)
    rs3, token = reduce_begin("r3", ["up0", "down0"])

    (dy0,), (dbout, dgate, dgq) = _rw_bwd("conv_post_bwd", _f_post_bias, [y0],
                                          [bout_full, after_token(mix[0][2], token), vec(gqm[0])], [dh1], [BF])
    gvec["g_post_mix0"] = dgq
    gw["conv_w_out"] = _mm_tn("conv_out_dw", n0, dy0, "row")
    dn0 = _mm_nt("conv_out_dx", dy0, w_out_f, "row")
    (duc0,), (dlng, dlnb) = _rw_bwd("conv_ln_bwd", _f_ln_silu, [uc0], [lng_full, lnb_full], [dn0], [F32])
    du0, ddw = _conv_bwd(u0, duc0, dw_full)
    (dz0,), (dbin,) = _rw_bwd("conv_glu_bwd", _f_glu, [z0], [bin_full], [du0], [BF])
    gw["conv_w_in"] = _mm_tn("conv_in_dw", hn0, dz0, "col")
    dhn0 = _mm_nt("conv_in_dx", dz0, w_in_f, "col")
    (dx,), (dgp, dsh, dsc) = _rw_bwd("conv_pre_bwd", _f_pre, [h0], [vec(gpm[0]), mix[0][0], mix[0][1]], [dhn0], [F32],
                                     add_rows={0: dh1})
    gvec["mix0"] = (dsh, dsc, dgate)
    gvec["g_pre_mix0"] = dgp
    reduce_end(rs3, dx)
    rs4, token = reduce_begin("r4", ["conv_w_in", "conv_w_out"])

    vec_list = [*gvec["mix0"], *gvec["mix1"], *gvec["mlp0"], *gvec["mlp1"], *gvec["kv"],
                gvec["g_pre_mix0"], gvec["g_pre_mix1"], gvec["g_post_mix0"], gvec["g_post_mix1"],
                gvec["g_pre_mlp0"], gvec["g_pre_mlp1"], gvec["g_post_mlp0"], gvec["g_post_mlp1"],
                gvec["g_kv"], gvec["g_ckv"], gvec["g_cq"], dbin, dlng, dlnb, dbout, ddw.reshape(1, -1)]
    vo = [0]
    for v_ in vec_list:
        vo.append(vo[-1] + v_.shape[1])
    vec_pad = (-vo[-1]) % (8 * LANES)
    n_vec = vo[-1] + vec_pad
    flat = jnp.concatenate(vec_list + [jnp.zeros((1, vec_pad), F32)], axis=1).reshape(8, n_vec // 8)
    all_vecs = _all_gather_small("gather_vector_grads", after_token(flat, token)).reshape(N_DEV, 8, n_vec // 8)
    summed = _sum_devices(all_vecs).reshape(1, n_vec)
    per_dev = all_vecs.reshape(N_DEV, n_vec)
    seg = lambda a, i: a[:, vo[i]:vo[i + 1]]

    out = {"loss": loss, "x": dx.reshape(x.shape)}
    dm_mix = [jnp.concatenate([seg(per_dev, 3 * l + i) for i in range(3)], axis=1) for l in range(2)]
    dm_mlp = [jnp.concatenate([seg(per_dev, 6 + 3 * l + i) for i in range(3)], axis=1) for l in range(2)]
    dm_kv = jnp.concatenate([seg(per_dev, 12), seg(per_dev, 13)], axis=1)
    ada_dw = lambda name, dm, width: _mm_tn(name, sc_all, chip_cols(dm, width), "row")
    out["w_ada_mix"] = jnp.stack([ada_dw("ada_mix_dw_%d" % l, dm_mix[l], n_ada) for l in range(2)])
    out["w_ada_mlp"] = jnp.stack([ada_dw("ada_mlp_dw_%d" % l, dm_mlp[l], n_ada) for l in range(2)])
    out["w_ada_kv"] = ada_dw("ada_kv_dw", dm_kv, n_kvada)
    sum_seg = lambda i: seg(summed, i)
    out["b_ada_mix"] = jnp.concatenate([jnp.concatenate([sum_seg(3 * l + i) for i in range(3)], axis=1) for l in range(2)], axis=0)
    out["b_ada_mlp"] = jnp.concatenate([jnp.concatenate([sum_seg(6 + 3 * l + i) for i in range(3)], axis=1) for l in range(2)], axis=0)
    out["b_ada_kv"] = jnp.concatenate([sum_seg(12), sum_seg(13)], axis=1).reshape(-1)
    out["g_pre_mix"] = jnp.concatenate([sum_seg(14), sum_seg(15)], axis=0)
    out["g_post_mix"] = jnp.concatenate([sum_seg(16), sum_seg(17)], axis=0)
    out["g_pre_mlp"] = jnp.concatenate([sum_seg(18), sum_seg(19)], axis=0)
    out["g_post_mlp"] = jnp.concatenate([sum_seg(20), sum_seg(21)], axis=0)
    out["g_kv"] = sum_seg(22).reshape(-1)
    out["g_ckv"] = sum_seg(23).reshape(-1)
    out["g_cq"] = sum_seg(24)
    out["conv_b_in"] = chip_cols(sum_seg(25), 2 * dl)
    out["conv_ln_g"] = chip_cols(sum_seg(26), dl)
    out["conv_ln_b"] = chip_cols(sum_seg(27), dl)
    out["conv_b_out"] = chip_cols(sum_seg(28), dl)
    ddw_sum = chip_cols(sum_seg(29).reshape(CONV_WIDTH + 1, d), dl)
    out["conv_dw"] = ddw_sum[:CONV_WIDTH].reshape(1, CONV_WIDTH, dl)
    out["conv_dw_b"] = ddw_sum[CONV_WIDTH:]

    reduce_end(rs4, summed)
    out["conv_w_in"] = red["conv_w_in"][None]
    out["conv_w_out"] = red["conv_w_out"][None]
    out["w_o"] = red["w_o"][None]
    out["mlp_w_up"] = jnp.stack([red["up0"], red["up1"]])
    out["mlp_w_down"] = jnp.stack([red["down0"], red["down1"]])
    rs = red["small"]
    piece = lambda i: rs[:, so[i]:so[i + 1]]
    out["w_dkv"] = piece(0)
    out["w_kr"] = piece(1)[:, :QK_ROPE]
    out["w_dq"] = piece(2)[None]
    out["w_uk"], out["w_uv"] = piece(3), piece(4)
    out["w_uq"] = jnp.concatenate([piece(5).reshape(r_q, heads_l, QK_NOPE), piece(6).reshape(r_q, heads_l, QK_ROPE)],
                                  axis=2).reshape(1, r_q, -1)
    return out
```

```python
import functools

import jax
import jax.numpy as jnp
from jax import lax
from jax.experimental import pallas as pl
from jax.experimental.pallas import tpu as pltpu

F32 = jnp.float32
BF = jnp.bfloat16
MXU_DTYPE = BF

EPS = 1e-6
NEG = -1e30
ROPE_THETA = 10000.0
QK_NOPE = 128
QK_ROPE = 64
V_HEAD = 128
CONV_WIDTH = 31
ADAM_LR, ADAM_B1, ADAM_B2, ADAM_EPS, ADAM_WD, ADAM_STEP = 0.001, 0.9, 0.999, 1e-08, 0.01, 10

N_CHIPS = 4
N_DEV = 8
LANES = 128
VMEM_LIMIT = 56 * 2 ** 20
MESH = pl.DeviceIdType.MESH
ANY = pl.BlockSpec(memory_space=pl.ANY)


def _params(sem=None):
    return pltpu.CompilerParams(dimension_semantics=sem, vmem_limit_bytes=VMEM_LIMIT)


def _tile(n, pref, unit=LANES):
    if n <= pref:
        return n
    t = (pref // unit) * unit
    while t > unit and n % t:
        t -= unit
    assert n % t == 0, (n, pref)
    return t


def _dg(a, b, ca, cb):
    return lax.dot_general(a.astype(MXU_DTYPE), b.astype(MXU_DTYPE), (((ca,), (cb,)), ((), ())),
                           preferred_element_type=F32)


@jax.custom_vjp
def _dot_nn(a, b):
    return _dg(a, b, 1, 0)


def _dot_nn_fwd(a, b):
    return _dg(a, b, 1, 0), (a, b)


def _dot_nn_bwd(res, g):
    a, b = res
    return _dg(g, b, 1, 1).astype(a.dtype), _dg(a, g, 0, 0).astype(b.dtype)


_dot_nn.defvjp(_dot_nn_fwd, _dot_nn_bwd)


@jax.custom_vjp
def _dot_nt(a, b):
    return _dg(a, b, 1, 1)


def _dot_nt_fwd(a, b):
    return _dg(a, b, 1, 1), (a, b)


def _dot_nt_bwd(res, g):
    a, b = res
    return _dg(g, b, 1, 0).astype(a.dtype), _dg(g, a, 0, 0).astype(b.dtype)


_dot_nt.defvjp(_dot_nt_fwd, _dot_nt_bwd)


def _matmul(name, a, b, *, ca, cb, grid, a_blk, a_map, b_blk, b_map, o_shape, o_dtypes, o_blk, o_map,
            extras=(), epilogue=None):
    nk = grid[-1]
    n_ex, n_out = len(extras), len(o_dtypes)

    def body(*refs):
        a_ref, b_ref = refs[0], refs[1]
        ex_refs = refs[2:2 + n_ex]
        out_refs = refs[2 + n_ex:2 + n_ex + n_out]
        kk = pl.program_id(len(grid) - 1)

        def finish(acc):
            outs = (acc,) if epilogue is None else epilogue(acc, *[r[...] for r in ex_refs])
            for r, o in zip(out_refs, outs):
                r[...] = o.astype(r.dtype)

        part = _dg(a_ref[...], b_ref[...], ca, cb)
        if nk == 1:
            finish(part)
        else:
            acc_ref = refs[-1]

            @pl.when(kk == 0)
            def _():
                acc_ref[...] = part

            @pl.when(kk > 0)
            def _():
                acc_ref[...] += part

            @pl.when(kk == nk - 1)
            def _():
                finish(acc_ref[...])

    o_spec = pl.BlockSpec(o_blk, o_map)
    acc_shape = tuple(d for d in o_blk if d is not None)
    res = pl.pallas_call(
        body, grid=grid,
        in_specs=[pl.BlockSpec(a_blk, a_map), pl.BlockSpec(b_blk, b_map)] + [o_spec] * n_ex,
        out_specs=[o_spec] * n_out,
        out_shape=[jax.ShapeDtypeStruct(o_shape, dt) for dt in o_dtypes],
        scratch_shapes=[] if nk == 1 else [pltpu.VMEM(acc_shape, F32)],
        compiler_params=_params(("parallel",) * (len(grid) - 1) + ("arbitrary",)),
        name=name,
    )(a, b, *extras)
    return res[0] if n_out == 1 else res


TM, TN, TK = 1024, 512, 1024


def _mm_nn(name, a, w, layout, o_dtypes=(F32,), extras=(), epilogue=None, tiles=(TM, TN, TK)):
    m, k = a.shape
    tm, tk = _tile(m, tiles[0], 8), _tile(k, tiles[2])
    if layout == "row":
        n = w.shape[1]
        tn = _tile(n, tiles[1])
        grid = (m // tm, n // tn, k // tk)
        b_blk, b_map = (tk, tn), (lambda i, j, kk: (kk, j))
    else:
        nl = w.shape[2]
        n = nl * w.shape[0]
        tn = _tile(nl, tiles[1])
        nb = nl // tn
        grid = (m // tm, n // tn, k // tk)
        b_blk, b_map = (None, tk, tn), (lambda i, j, kk: (j // nb, kk, j % nb))
    return _matmul(name, a, w, ca=1, cb=0, grid=grid, a_blk=(tm, tk), a_map=lambda i, j, kk: (i, kk),
                   b_blk=b_blk, b_map=b_map, o_shape=(m, n), o_dtypes=o_dtypes, o_blk=(tm, tn),
                   o_map=lambda i, j, kk: (i, j), extras=extras, epilogue=epilogue)


def _mm_nt(name, g, w, layout, o_dtypes=(F32,), extras=(), epilogue=None, tiles=(TM, TN, TK)):
    m, n = g.shape
    tm, tn = _tile(m, tiles[0], 8), None
    if layout == "row":
        k = w.shape[0]
        tn = _tile(n, tiles[2])
        tk = _tile(k, tiles[1])
        grid = (m // tm, k // tk, n // tn)
        b_blk, b_map = (tk, tn), (lambda i, j, kk: (j, kk))
    else:
        k, nl = w.shape[1], w.shape[2]
        tn = _tile(nl, tiles[2])
        nb = nl // tn
        tk = _tile(k, tiles[1])
        grid = (m // tm, k // tk, n // tn)
        b_blk, b_map = (None, tk, tn), (lambda i, j, kk: (kk // nb, j, kk % nb))
    return _matmul(name, g, w, ca=1, cb=1, grid=grid, a_blk=(tm, tn), a_map=lambda i, j, kk: (i, kk),
                   b_blk=b_blk, b_map=b_map, o_shape=(m, k), o_dtypes=o_dtypes, o_blk=(tm, tk),
                   o_map=lambda i, j, kk: (i, j), extras=extras, epilogue=epilogue)


def _mm_tn(name, x, g, layout, o_dtype=F32, tiles=(TM, TN, TK)):
    t, k = x.shape
    n = g.shape[1]
    tt = _tile(t, tiles[2], 8)
    tk = _tile(k, tiles[0])
    if layout == "row":
        tn = _tile(n, tiles[1])
        o_shape, o_blk, o_map = (k, n), (tk, tn), (lambda i, j, kk: (i, j))
    else:
        nl = n // N_CHIPS
        tn = _tile(nl, tiles[1])
        nb = nl // tn
        o_shape, o_blk, o_map = (N_CHIPS, k, nl), (None, tk, tn), (lambda i, j, kk: (j // nb, i, j % nb))
    grid = (k // tk, n // tn, t // tt)
    return _matmul(name, x, g, ca=0, cb=0, grid=grid, a_blk=(tt, tk), a_map=lambda i, j, kk: (kk, i),
                   b_blk=(tt, tn), b_map=lambda i, j, kk: (kk, j), o_shape=o_shape, o_dtypes=(o_dtype,),
                   o_blk=o_blk, o_map=o_map)


TR = 256


def _rw_fwd(name, fn, rows, vecs, o_dtypes, tr=TR):
    t = rows[0].shape[0]
    tr = min(tr, t)
    n_r, n_v = len(rows), len(vecs)
    o_sds = jax.eval_shape(fn, *[jax.ShapeDtypeStruct((tr, r.shape[1]), r.dtype) for r in rows],
                           *[jax.ShapeDtypeStruct(v.shape, v.dtype) for v in vecs])

    def body(*refs):
        outs = fn(*[r[...] for r in refs[:n_r + n_v]])
        for r, o in zip(refs[n_r + n_v:], outs):
            r[...] = o.astype(r.dtype)

    res = pl.pallas_call(
        body, grid=(t // tr,),
        in_specs=[pl.BlockSpec((tr, r.shape[1]), lambda i: (i, 0)) for r in rows]
        + [pl.BlockSpec(v.shape, lambda i: (0, 0)) for v in vecs],
        out_specs=[pl.BlockSpec((tr, o.shape[1]), lambda i: (i, 0)) for o in o_sds],
        out_shape=[jax.ShapeDtypeStruct((t, o.shape[1]), dt) for o, dt in zip(o_sds, o_dtypes)],
        compiler_params=_params(("parallel",)), name=name,
    )(*rows, *vecs)
    return res


def _rw_bwd(name, fn, rows, vecs, cots, row_grad_dtypes, add_rows=None, tr=TR):
    t = rows[0].shape[0]
    tr = min(tr, t)
    add_rows = add_rows or {}
    n_r, n_v, n_c = len(rows), len(vecs), len(cots)
    diff = [i for i, dt in enumerate(row_grad_dtypes) if dt is not None]
    adds = [add_rows[i] for i in diff if i in add_rows]
    n_a = len(adds)

    def body(*refs):
        rvals = [r[...] for r in refs[:n_r]]
        vvals = [r[...] for r in refs[n_r:n_r + n_v]]
        c_refs = refs[n_r + n_v:n_r + n_v + n_c]
        a_refs = list(refs[n_r + n_v + n_c:n_r + n_v + n_c + n_a])
        o_refs = refs[n_r + n_v + n_c + n_a:]

        def f(*d):
            full = list(rvals)
            for i, x in zip(diff, d[:len(diff)]):
                full[i] = x
            return fn(*full, *d[len(diff):])

        outs, vjp = jax.vjp(f, *[rvals[i] for i in diff], *vvals)
        grads = vjp(tuple(c[...].astype(o.dtype) for c, o in zip(c_refs, outs)))
        for n, i in enumerate(diff):
            gr = grads[n].astype(F32)
            if i in add_rows:
                gr = gr + a_refs.pop(0)[...].astype(F32)
            o_refs[n][...] = gr.astype(o_refs[n].dtype)
        first = pl.program_id(0) == 0
        for r, gv in zip(o_refs[len(diff):], grads[len(diff):]):
            @pl.when(first)
            def _(r=r, gv=gv):
                r[...] = gv

            @pl.when(jnp.logical_not(first))
            def _(r=r, gv=gv):
                r[...] += gv

    row_spec = lambda a: pl.BlockSpec((tr, a.shape[1]), lambda i: (i, 0))
    vec_spec = lambda a: pl.BlockSpec(a.shape, lambda i: (0, 0))
    res = pl.pallas_call(
        body, grid=(t // tr,),
        in_specs=[row_spec(r) for r in rows] + [vec_spec(v) for v in vecs] + [row_spec(c) for c in cots]
        + [row_spec(a) for a in adds],
        out_specs=[row_spec(rows[i]) for i in diff] + [vec_spec(v) for v in vecs],
        out_shape=[jax.ShapeDtypeStruct(rows[i].shape, row_grad_dtypes[i]) for i in diff]
        + [jax.ShapeDtypeStruct(v.shape, F32) for v in vecs],
        compiler_params=_params(("arbitrary",)), name=name,
    )(*rows, *vecs, *cots, *adds)
    return res[:len(diff)], res[len(diff):]


def _rms(x, g):
    return x * lax.rsqrt(jnp.mean(x * x, axis=-1, keepdims=True) + EPS) * g


def _f_pre(h, g, shift, scale):
    return (_rms(h, g) * (1.0 + scale) + shift,)


def _f_pre2(h, g1, sh1, sc1, g2, sh2, sc2):
    return _f_pre(h, g1, sh1, sc1) + _f_pre(h, g2, sh2, sc2)


def _f_post(y, gate, g):
    return (gate * _rms(y, g),)


def _f_post_bias(y, bias, gate, g):
    return (gate * _rms(y + bias, g),)


def _f_glu(z, bias):
    z = z + bias
    half = z.shape[1] // 2
    return (z[:, :half] * jax.nn.sigmoid(z[:, half:]),)


def _f_ln_silu(u, g, b):
    mu = jnp.mean(u, axis=-1, keepdims=True)
    var = jnp.mean(jnp.square(u - mu), axis=-1, keepdims=True)
    y = (u - mu) * lax.rsqrt(var + EPS) * g + b
    return (y * jax.nn.sigmoid(y),)


def _rope_raw(x, cos, sin):
    n = x.shape[1]
    reps = n // LANES
    if reps > 1:
        cos = jnp.concatenate([cos] * reps, axis=1)
        sin = jnp.concatenate([sin] * reps, axis=1)
    lane = lax.broadcasted_iota(jnp.int32, x.shape, 1)
    half = QK_ROPE // 2
    partner = jnp.where((lane % QK_ROPE) < half, pltpu.roll(x, n - half, 1), pltpu.roll(x, half, 1))
    return x * cos + partner * sin


@jax.custom_vjp
def _rope(x, cos, sin):
    return _rope_raw(x, cos, sin)


def _rope_fwd(x, cos, sin):
    return _rope_raw(x, cos, sin), (cos, sin)


def _rope_bwd(res, g):
    cos, sin = res
    return _rope_raw(g, cos, -sin), jnp.zeros_like(cos), jnp.zeros_like(sin)


_rope.defvjp(_rope_fwd, _rope_bwd)


def _make_f_lat(r_kv):
    def f(pre_kv, pre_q, cos, sin, g_ckv, g_cq):
        c_kv = _rms(pre_kv[:, :r_kv], g_ckv)
        kr = _rope(pre_kv[:, r_kv:], cos, sin)
        return c_kv, kr, _rms(pre_q, g_cq)
    return f


def _f_rope(x, cos, sin):
    return (_rope(x, cos, sin),)


CONV_TC = 128
CONV_TT = 256
PADR = 32


def _conv_fwd(u, dw, dw_b):
    t, d = u.shape
    tc, tt = min(CONV_TC, d), min(CONV_TT, t)
    off = PADR - (CONV_WIDTH - 1)

    def body(u_ref, w_ref, b_ref, o_ref, pad_ref):
        pad_ref[pl.ds(0, PADR), :] = jnp.zeros((PADR, tc), F32)
        pad_ref[pl.ds(PADR, t), :] = u_ref[...]
        for t0 in range(0, t, tt):
            acc = jnp.zeros((tt, tc), F32) + b_ref[...]
            for j in range(CONV_WIDTH):
                acc = acc + pad_ref[pl.ds(t0 + off + j, tt), :] * w_ref[pl.ds(j, 1), :]
            o_ref[pl.ds(t0, tt), :] = acc

    return pl.pallas_call(
        body, grid=(d // tc,),
        in_specs=[pl.BlockSpec((t, tc), lambda j: (0, j)), pl.BlockSpec((CONV_WIDTH, tc), lambda j: (0, j)),
                  pl.BlockSpec((1, tc), lambda j: (0, j))],
        out_specs=pl.BlockSpec((t, tc), lambda j: (0, j)),
        out_shape=jax.ShapeDtypeStruct((t, d), F32),
        scratch_shapes=[pltpu.VMEM((t + PADR, tc), F32)],
        compiler_params=_params(("parallel",)), name="conv_fwd",
    )(u, dw, dw_b)


def _conv_bwd(u, duc, dw):
    t, d = u.shape
    tc, tt = min(CONV_TC, d), min(CONV_TT, t)
    off = PADR - (CONV_WIDTH - 1)

    def body(u_ref, g_ref, w_ref, du_ref, dwt_ref, padu_ref, padg_ref):
        padu_ref[pl.ds(0, PADR), :] = jnp.zeros((PADR, tc), F32)
        padu_ref[pl.ds(PADR, t), :] = u_ref[...]
        padg_ref[pl.ds(t, PADR), :] = jnp.zeros((PADR, tc), F32)
        padg_ref[pl.ds(0, t), :] = g_ref[...]
        for t0 in range(0, t, tt):
            acc = jnp.zeros((tt, tc), F32)
            for j in range(CONV_WIDTH):
                acc = acc + padg_ref[pl.ds(t0 + (CONV_WIDTH - 1) - j, tt), :] * w_ref[pl.ds(j, 1), :]
            du_ref[pl.ds(t0, tt), :] = acc
        for j in range(CONV_WIDTH):
            acc = jnp.zeros((tt, tc), F32)
            for t0 in range(0, t, tt):
                acc = acc + g_ref[pl.ds(t0, tt), :] * padu_ref[pl.ds(t0 + off + j, tt), :]
            dwt_ref[pl.ds(j, 1), :] = jnp.sum(acc, axis=0, keepdims=True)
        acc = jnp.zeros((tt, tc), F32)
        for t0 in range(0, t, tt):
            acc = acc + g_ref[pl.ds(t0, tt), :]
        dwt_ref[pl.ds(CONV_WIDTH, 1), :] = jnp.sum(acc, axis=0, keepdims=True)

    col = lambda r: pl.BlockSpec((r, tc), lambda j: (0, j))
    return pl.pallas_call(
        body, grid=(d // tc,),
        in_specs=[col(t), col(t), col(CONV_WIDTH)],
        out_specs=[col(t), col(CONV_WIDTH + 1)],
        out_shape=[jax.ShapeDtypeStruct((t, d), F32), jax.ShapeDtypeStruct((CONV_WIDTH + 1, d), F32)],
        scratch_shapes=[pltpu.VMEM((t + PADR, tc), F32), pltpu.VMEM((t + PADR, tc), F32)],
        compiler_params=_params(("parallel",)), name="conv_bwd",
    )(u, duc, dw)


@jax.custom_vjp
def _swap_halves(x):
    return pltpu.roll(x, LANES // 2, 1)


_swap_halves.defvjp(lambda x: (pltpu.roll(x, LANES // 2, 1), None), lambda _, g: (pltpu.roll(g, LANES // 2, 1),))


def _attn_block(qn, qr, kn, kr, v, q0):
    scale = (QK_NOPE + QK_ROPE) ** -0.5
    lane = lax.broadcasted_iota(jnp.int32, kr.shape, 1)
    kr_a = kr * (lane < QK_ROPE).astype(kr.dtype)
    kr_b = _swap_halves(kr_a)
    outs = []
    for hh, kr_h in ((0, kr_a), (1, kr_b)):
        sl = slice(hh * QK_NOPE, (hh + 1) * QK_NOPE)
        s = (_dot_nt(qn[:, sl], kn[:, sl]) + _dot_nt(qr, kr_h)) * scale
        row = lax.broadcasted_iota(jnp.int32, s.shape, 0) + q0
        col = lax.broadcasted_iota(jnp.int32, s.shape, 1)
        s = jnp.where(col <= row, s, NEG)
        e = jnp.exp(s - jnp.max(s, axis=-1, keepdims=True))
        p = e / jnp.sum(e, axis=-1, keepdims=True)
        outs.append(_dot_nn(p, v[:, sl]))
    return jnp.concatenate(outs, axis=1)


def _attn_fwd(qn, qr, kn, kr, v, tq=512):
    t, w = qn.shape
    pairs = w // (2 * QK_NOPE)
    tq = min(tq, t)
    pw = 2 * QK_NOPE

    def body(qn_ref, qr_ref, kn_ref, kr_ref, v_ref, o_ref):
        for q0 in range(0, t, tq):
            l = q0 + tq
            o_ref[pl.ds(q0, tq), :] = _attn_block(
                qn_ref[pl.ds(q0, tq), :], qr_ref[pl.ds(q0, tq), :], kn_ref[pl.ds(0, l), :], kr_ref[pl.ds(0, l), :],
                v_ref[pl.ds(0, l), :], q0).astype(o_ref.dtype)

    pair = lambda wd: pl.BlockSpec((t, wd), lambda p: (0, p))
    return pl.pallas_call(
        body, grid=(pairs,),
        in_specs=[pair(pw), pair(LANES), pair(pw), pl.BlockSpec((t, LANES), lambda p: (0, 0)), pair(pw)],
        out_specs=pair(pw), out_shape=jax.ShapeDtypeStruct((t, w), BF),
        compiler_params=_params(("parallel",)), name="attn_fwd",
    )(qn, qr, kn, kr, v)


def _attn_bwd(qn, qr, kn, kr, v, do, tq=256):
    t, w = qn.shape
    pairs = w // (2 * QK_NOPE)
    tq = min(tq, t)
    pw = 2 * QK_NOPE

    def body(qn_ref, qr_ref, kn_ref, kr_ref, v_ref, do_ref, dqn_ref, dqr_ref, dkn_ref, dv_ref, dkr_ref,
             akn_ref, av_ref, akr_ref):
        akn_ref[...] = jnp.zeros_like(akn_ref)
        av_ref[...] = jnp.zeros_like(av_ref)
        akr_ref[...] = jnp.zeros_like(akr_ref)
        for q0 in range(0, t, tq):
            l = q0 + tq
            rows, keys = pl.ds(q0, tq), pl.ds(0, l)
            _, vjp = jax.vjp(functools.partial(_attn_block, q0=q0), qn_ref[rows, :], qr_ref[rows, :],
                             kn_ref[keys, :], kr_ref[keys, :], v_ref[keys, :])
            dqn, dqr, dkn, dkr, dv = vjp(do_ref[rows, :].astype(F32))
            dqn_ref[rows, :] = dqn.astype(dqn_ref.dtype)
            dqr_ref[rows, :] = dqr.astype(dqr_ref.dtype)
            akn_ref[keys, :] += dkn.astype(F32)
            av_ref[keys, :] += dv.astype(F32)
            akr_ref[keys, :] += dkr.astype(F32)
        dkn_ref[...] = akn_ref[...].astype(dkn_ref.dtype)
        dv_ref[...] = av_ref[...].astype(dv_ref.dtype)
        first = pl.program_id(0) == 0

        @pl.when(first)
        def _():
            dkr_ref[...] = akr_ref[...]

        @pl.when(jnp.logical_not(first))
        def _():
            dkr_ref[...] += akr_ref[...]

    pair = lambda wd: pl.BlockSpec((t, wd), lambda p: (0, p))
    shared = pl.BlockSpec((t, LANES), lambda p: (0, 0))
    sds = jax.ShapeDtypeStruct
    return pl.pallas_call(
        body, grid=(pairs,),
        in_specs=[pair(pw), pair(LANES), pair(pw), shared, pair(pw), pair(pw)],
        out_specs=[pair(pw), pair(LANES), pair(pw), pair(pw), shared],
        out_shape=[sds((t, w), BF), sds((t, pairs * LANES), F32), sds((t, w), BF), sds((t, w), BF), sds((t, LANES), F32)],
        scratch_shapes=[pltpu.VMEM((t, pw), F32), pltpu.VMEM((t, pw), F32), pltpu.VMEM((t, LANES), F32)],
        compiler_params=_params(("arbitrary",)), name="attn_bwd",
    )(qn, qr, kn, kr, v, do)


def _ew(name, fn, ins, o_dtypes, max_bytes=2 ** 21):
    r, c = ins[0].shape
    tr = r
    if r * c * 4 > max_bytes:
        tr = max(16, (max_bytes // (c * 4)) // 16 * 16)
        while r % tr:
            tr -= 16
    n_in = len(ins)

    def body(*refs):
        outs = fn(*[x[...] for x in refs[:n_in]])
        for o_ref, o in zip(refs[n_in:], outs):
            o_ref[...] = o.astype(o_ref.dtype)

    spec = pl.BlockSpec((tr, c), lambda i: (i, 0))
    return pl.pallas_call(
        body, grid=(r // tr,), in_specs=[spec] * n_in, out_specs=[spec] * len(o_dtypes),
        out_shape=[jax.ShapeDtypeStruct((r, c), dt) for dt in o_dtypes],
        compiler_params=_params(("parallel",)), name=name,
    )(*ins)


def _adamw_math(w, g, m, v):
    m = ADAM_B1 * m + (1.0 - ADAM_B1) * g
    v = ADAM_B2 * v + (1.0 - ADAM_B2) * jnp.square(g)
    m_hat = m / (1.0 - ADAM_B1 ** ADAM_STEP)
    v_hat = v / (1.0 - ADAM_B2 ** ADAM_STEP)
    delta = -ADAM_LR * (m_hat / (jnp.sqrt(v_hat) + ADAM_EPS) + ADAM_WD * w)
    return delta, m, v


def _adamw(name, w, g, m, v):
    shape = w.shape
    to2 = lambda a: a.reshape(-1, shape[-1]) if a.ndim > 1 else a.reshape(1, -1)
    d, nm, nv = _ew("adamw_" + name, _adamw_math, [to2(w), to2(g), to2(m), to2(v)], (F32, F32, F32))
    return d.reshape(shape), nm.reshape(shape), nv.reshape(shape)


def _rope_tables(positions, t):
    half = QK_ROPE // 2
    inv = 1.0 / (ROPE_THETA ** (jnp.arange(0, QK_ROPE, 2, dtype=F32) / QK_ROPE))
    inv_l = jnp.tile(inv, LANES // half).reshape(1, LANES)
    sign = jnp.tile(jnp.concatenate([-jnp.ones((half,), F32), jnp.ones((half,), F32)]), LANES // QK_ROPE).reshape(1, LANES)

    def body(p_ref, inv_ref, sg_ref, c_ref, s_ref):
        ang = p_ref[...].astype(F32) * inv_ref[...]
        c_ref[...] = jnp.cos(ang)
        s_ref[...] = jnp.sin(ang) * sg_ref[...]

    return pl.pallas_call(body, out_shape=[jax.ShapeDtypeStruct((t, LANES), F32)] * 2, name="rope_tables")(
        positions.reshape(t, 1), inv_l, sign)


def _loss_grad(h, target):
    t, d = h.shape
    tr = min(TR, t)

    def body(h_ref, y_ref, dh_ref, l_ref):
        err = h_ref[...] - y_ref[...]
        dh_ref[...] = err * (1.0 / d)
        part = 0.5 * jnp.sum(jnp.mean(jnp.square(err), axis=-1, keepdims=True), axis=0, keepdims=True)
        first = pl.program_id(0) == 0

        @pl.when(first)
        def _():
            l_ref[...] = part

        @pl.when(jnp.logical_not(first))
        def _():
            l_ref[...] += part

    row = pl.BlockSpec((tr, d), lambda i: (i, 0))
    return pl.pallas_call(
        body, grid=(t // tr,), in_specs=[row, row],
        out_specs=[row, pl.BlockSpec((1, 1), lambda i: (0, 0))],
        out_shape=[jax.ShapeDtypeStruct((t, d), F32), jax.ShapeDtypeStruct((1, 1), F32)],
        compiler_params=_params(("arbitrary",)), name="loss_grad",
    )(h, target)


def _sum_devices(g):
    def fn(*parts):
        acc = parts[0]
        for p in parts[1:]:
            acc = acc + p
        return (acc,)
    return _ew("sum_devices", fn, [g[i] for i in range(g.shape[0])], (F32,))[0]


def _place():
    x, y, c = lax.axis_index("x"), lax.axis_index("y"), lax.axis_index("c")
    return x, y, c, [(1 - x, y), (x, 1 - y), (1 - x, 1 - y)]


def _all_gather_small(name, v):
    r, n = v.shape

    def body(x_ref, out_ref, send_sems, recv_sems, local_sem):
        x, y, c, chips = _place()
        me, sibling = (x, y, c), (x, y, 1 - c)

        def rows(px, py, pc):
            return out_ref.at[4 * px + 2 * py + pc]

        def copy(k, block, to, src=None):
            return pltpu.make_async_remote_copy(
                src_ref=rows(*block) if src is None else src, dst_ref=rows(*block), send_sem=send_sems.at[k],
                recv_sem=recv_sems.at[k], device_id=to, device_id_type=MESH)

        mine = pltpu.make_async_copy(x_ref, rows(*me), local_sem)
        mine.start()
        first = [copy(0, me, sibling, src=x_ref)]
        first += [copy(1 + j, me, (*chip, c), src=x_ref) for j, chip in enumerate(chips)]
        for cp in first:
            cp.start()
        passed = [copy(4 + j, (*chip, c), sibling) for j, chip in enumerate(chips)]
        for j, chip in enumerate(chips):
            copy(1 + j, (*chip, c), me).wait_recv()
            passed[j].start()
        copy(0, sibling, me).wait_recv()
        for j, chip in enumerate(chips):
            copy(4 + j, (*chip, 1 - c), me).wait_recv()
        for cp in first + passed:
            cp.wait_send()
        mine.wait()

    return pl.pallas_call(
        body, out_shape=jax.ShapeDtypeStruct((N_DEV, r, n), v.dtype),
        in_specs=[pl.BlockSpec(memory_space=pltpu.VMEM)], out_specs=pl.BlockSpec(memory_space=pltpu.VMEM),
        scratch_shapes=[pltpu.SemaphoreType.DMA((7,)), pltpu.SemaphoreType.DMA((7,)), pltpu.SemaphoreType.DMA],
        compiler_params=pltpu.CompilerParams(vmem_limit_bytes=VMEM_LIMIT), name=name,
    )(v)


def _cast_into_slot(name, w, idx):
    r, n = w.shape
    tr = r
    while tr * n * 4 > 2 ** 21 and tr % 32 == 0:
        tr //= 2

    def body(idx_ref, w_ref, o_ref):
        o_ref[...] = w_ref[...].astype(o_ref.dtype)

    return pl.pallas_call(
        body,
        grid_spec=pltpu.PrefetchScalarGridSpec(
            num_scalar_prefetch=1, grid=(r // tr,),
            in_specs=[pl.BlockSpec((tr, n), lambda i, idx_ref: (i, 0))],
            out_specs=pl.BlockSpec((None, tr, n), lambda i, idx_ref: (idx_ref[1], i, 0))),
        out_shape=jax.ShapeDtypeStruct((N_CHIPS, r, n), BF),
        compiler_params=_params(("parallel",)), name="cast_" + name,
    )(idx, w)


def _gather_weights(bufs):
    n_w = len(bufs)

    def body(*refs):
        outs = refs[n_w:2 * n_w]
        send_sems, recv_sems = refs[2 * n_w:]
        x, y, c, chips = _place()

        def copy(i, s, chip, which, to):
            h = outs[i].shape[1] // 2
            blk = outs[i].at[2 * chip[0] + chip[1], pl.ds(which * h, h)]
            return pltpu.make_async_remote_copy(
                src_ref=blk, dst_ref=blk, send_sem=send_sems.at[6 * i + s], recv_sem=recv_sems.at[6 * i + s],
                device_id=to, device_id_type=MESH)

        sends = []
        for i in range(n_w):
            for j, chip in enumerate(chips):
                cp = copy(i, j, (x, y), c, (*chip, c))
                cp.start()
                sends.append(cp)
        for i in range(n_w):
            for j, chip in enumerate(chips):
                copy(i, j, chip, c, (x, y, c)).wait_recv()
                cp = copy(i, 3 + j, chip, c, (x, y, 1 - c))
                cp.start()
                sends.append(cp)
        for i in range(n_w):
            for j, chip in enumerate(chips):
                copy(i, 3 + j, chip, 1 - c, (x, y, c)).wait_recv()
        for cp in sends:
            cp.wait_send()

    return pl.pallas_call(
        body, in_specs=[ANY] * n_w, out_specs=[ANY] * n_w,
        out_shape=[jax.ShapeDtypeStruct(b.shape, b.dtype) for b in bufs],
        input_output_aliases={i: i for i in range(n_w)},
        scratch_shapes=[pltpu.SemaphoreType.DMA((6 * n_w,)), pltpu.SemaphoreType.DMA((6 * n_w,))],
        name="gather_weights",
    )(*bufs)


def _swap_other_half(name, grads, after=None):
    n_w = len(grads)
    order = [] if after is None else [after]
    n_in = n_w + len(order)

    def body(*refs):
        ins, outs = refs[:n_w], refs[n_in:n_in + n_w]
        send_sems, recv_sems = refs[n_in + n_w:]
        x, y, c, _ = _place()
        cps = []
        for i in range(n_w):
            h = ins[i].shape[1] // 2
            cp = pltpu.make_async_remote_copy(
                src_ref=ins[i].at[:, pl.ds((1 - c) * h, h), :], dst_ref=outs[i], send_sem=send_sems.at[i],
                recv_sem=recv_sems.at[i], device_id=(x, y, 1 - c), device_id_type=MESH)
            cp.start()
            cps.append(cp)
        for cp in cps:
            cp.wait()

    return pl.pallas_call(
        body, in_specs=[ANY] * n_in, out_specs=[ANY] * n_w,
        out_shape=[jax.ShapeDtypeStruct((g.shape[0], g.shape[1] // 2, g.shape[2]), g.dtype) for g in grads],
        scratch_shapes=[pltpu.SemaphoreType.DMA((n_w,)), pltpu.SemaphoreType.DMA((n_w,))],
        name=name,
    )(*grads, *order)


def _add_my_half(name, g, s, idx, o_dtype):
    nc, r, n = g.shape
    h = r // 2
    tr = h
    while tr * n * 4 > 2 ** 21 and tr % 32 == 0:
        tr //= 2
    nb = h // tr

    def body(idx_ref, g_ref, s_ref, o_ref, own_ref):
        val = (g_ref[...].astype(F32) + s_ref[...].astype(F32)).astype(o_ref.dtype)
        o_ref[...] = val

        @pl.when(pl.program_id(1) == idx_ref[1])
        def _():
            own_ref[...] = val

    return pl.pallas_call(
        body,
        grid_spec=pltpu.PrefetchScalarGridSpec(
            num_scalar_prefetch=1, grid=(nb, nc),
            in_specs=[pl.BlockSpec((None, tr, n), lambda i, j, idx_ref: (j, idx_ref[0] * nb + i, 0)),
                      pl.BlockSpec((None, tr, n), lambda i, j, idx_ref: (j, i, 0))],
            out_specs=[pl.BlockSpec((None, tr, n), lambda i, j, idx_ref: (j, i, 0)),
                       pl.BlockSpec((None, tr, n), lambda i, j, idx_ref: (idx_ref[1], i, 0))]),
        out_shape=[jax.ShapeDtypeStruct((nc, h, n), o_dtype)] * 2,
        compiler_params=_params(("parallel", "arbitrary")), name=name,
    )(idx, g, s)


def _scatter_partials(parts, bufs):
    n_w = len(parts)

    def body(*refs):
        ins, outs = refs[:n_w], refs[2 * n_w:3 * n_w]
        send_sems, recv_sems = refs[3 * n_w:]
        x, y, c, chips = _place()
        k = 2 * x + y
        sends = []
        for i in range(n_w):
            for j, chip in enumerate(chips):
                cp = pltpu.make_async_remote_copy(
                    src_ref=ins[i].at[2 * chip[0] + chip[1]], dst_ref=outs[i].at[k], send_sem=send_sems.at[3 * i + j],
                    recv_sem=recv_sems.at[3 * i + j], device_id=(*chip, c), device_id_type=MESH)
                cp.start()
                sends.append(cp)
        for i in range(n_w):
            for j, chip in enumerate(chips):
                blk = outs[i].at[2 * chip[0] + chip[1]]
                pltpu.make_async_remote_copy(
                    src_ref=blk, dst_ref=blk, send_sem=send_sems.at[3 * i + j], recv_sem=recv_sems.at[3 * i + j],
                    device_id=(*chip, c), device_id_type=MESH).wait_recv()
        for cp in sends:
            cp.wait_send()

    return pl.pallas_call(
        body, in_specs=[ANY] * (2 * n_w), out_specs=[ANY] * n_w,
        out_shape=[jax.ShapeDtypeStruct(b.shape, b.dtype) for b in bufs],
        input_output_aliases={n_w + i: i for i in range(n_w)},
        scratch_shapes=[pltpu.SemaphoreType.DMA((3 * n_w,)), pltpu.SemaphoreType.DMA((3 * n_w,))],
        name="scatter_partials",
    )(*parts, *bufs)


def _sum_chips(name, q, idx):
    nc, h, n = q.shape
    tr = h
    while tr * n * 4 > 2 ** 20 and tr % 32 == 0:
        tr //= 2
    nb = h // tr

    def body(idx_ref, q_ref, o_ref):
        acc = q_ref[0].astype(F32)
        for j in range(1, nc):
            acc = acc + q_ref[j].astype(F32)
        o_ref[...] = acc

    return pl.pallas_call(
        body,
        grid_spec=pltpu.PrefetchScalarGridSpec(
            num_scalar_prefetch=1, grid=(nb,),
            in_specs=[pl.BlockSpec((nc, tr, n), lambda i, idx_ref: (0, i, 0))],
            out_specs=pl.BlockSpec((tr, n), lambda i, idx_ref: (idx_ref[0] * nb + i, 0))),
        out_shape=jax.ShapeDtypeStruct((2 * h, n), F32),
        compiler_params=_params(("parallel",)), name=name,
    )(idx, q)


def _join_halves(name, bufs):
    n_w = len(bufs)

    def body(*refs):
        outs = refs[n_w:2 * n_w]
        send_sems, recv_sems = refs[2 * n_w:]
        x, y, c, _ = _place()

        def copy(i, which):
            h = outs[i].shape[0] // 2
            rows = outs[i].at[pl.ds(which * h, h)]
            return pltpu.make_async_remote_copy(
                src_ref=rows, dst_ref=rows, send_sem=send_sems.at[i], recv_sem=recv_sems.at[i],
                device_id=(x, y, 1 - c), device_id_type=MESH)

        cps = [copy(i, c) for i in range(n_w)]
        for cp in cps:
            cp.start()
        for i, cp in enumerate(cps):
            cp.wait_send()
            copy(i, 1 - c).wait_recv()

    return pl.pallas_call(
        body, in_specs=[ANY] * n_w, out_specs=[ANY] * n_w,
        out_shape=[jax.ShapeDtypeStruct(b.shape, b.dtype) for b in bufs],
        input_output_aliases={i: i for i in range(n_w)},
        scratch_shapes=[pltpu.SemaphoreType.DMA((n_w,)), pltpu.SemaphoreType.DMA((n_w,))],
        name=name,
    )(*bufs)


HBM_SPEC = pl.BlockSpec(memory_space=pltpu.HBM)
SEM_SPEC = pl.BlockSpec(memory_space=pltpu.SEMAPHORE)
VMEM_SPEC = pl.BlockSpec(memory_space=pltpu.VMEM)
SIDE_EFFECT = pltpu.SideEffectType.DATAFLOW_SIDE_EFFECTING


def _in_hbm(arrays):
    return [pltpu.with_memory_space_constraint(a, pltpu.HBM) for a in arrays]


def _ici_gather_copy(ref, i, j, chip, send_sems, recv_sems):
    x, y, c, _ = _place()
    h = ref.shape[1] // 2
    mine = ref.at[2 * x + y, pl.ds(c * h, h)]
    return pltpu.make_async_remote_copy(
        src_ref=mine, dst_ref=mine, send_sem=send_sems.at[3 * i + j], recv_sem=recv_sems.at[3 * i + j],
        device_id=(*chip, c), device_id_type=MESH)


def _gather_start(name, bufs, after=None):
    n_w = len(bufs)
    order = [] if after is None else [after]

    def body(*refs):
        ins, token = refs[:n_w], refs[-1]
        send_sems, recv_sems = refs[n_w + len(order)], refs[n_w + len(order) + 1]
        chips = _place()[3]
        for i in range(n_w):
            for j, chip in enumerate(chips):
                _ici_gather_copy(ins[i], i, j, chip, send_sems, recv_sems).start()
        token[...] = jnp.zeros_like(token)

    res = pl.pallas_call(
        body, name=name, in_specs=[HBM_SPEC] * n_w + [ANY] * len(order),
        out_shape=(pltpu.SemaphoreType.DMA((3 * n_w,)), pltpu.SemaphoreType.DMA((3 * n_w,)),
                   *[pltpu.HBM(b.shape, b.dtype) for b in bufs], jax.ShapeDtypeStruct((8, LANES), F32)),
        out_specs=(SEM_SPEC, SEM_SPEC, *[HBM_SPEC] * n_w, VMEM_SPEC),
        input_output_aliases={i: 2 + i for i in range(n_w)},
        compiler_params=pltpu.CompilerParams(has_side_effects=SIDE_EFFECT),
    )(*_in_hbm(bufs), *order)
    return res[0], res[1], list(res[2:2 + n_w]), res[-1]


def _gather_wait(name, send_sems, recv_sems, bufs, after):
    n_w = len(bufs)

    def body(*refs):
        ins, s_sems, r_sems = refs[:n_w], refs[n_w], refs[n_w + 1]
        chips = _place()[3]
        for i in range(n_w):
            for j, chip in enumerate(chips):
                cp = _ici_gather_copy(ins[i], i, j, chip, s_sems, r_sems)
                cp.wait_send()
                cp.wait_recv()

    return pl.pallas_call(
        body, name=name, in_specs=[HBM_SPEC] * n_w + [SEM_SPEC, SEM_SPEC, ANY],
        out_shape=[pltpu.HBM(b.shape, b.dtype) for b in bufs], out_specs=[HBM_SPEC] * n_w,
        input_output_aliases={i: i for i in range(n_w)},
        compiler_params=pltpu.CompilerParams(has_side_effects=SIDE_EFFECT),
    )(*bufs, send_sems, recv_sems, after)


def _forward_to_sibling(name, bufs):
    n_w = len(bufs)

    def body(*refs):
        outs = refs[n_w:2 * n_w]
        send_sems, recv_sems = refs[2 * n_w:]
        x, y, c, chips = _place()

        def copy(i, j, chip, which):
            h = outs[i].shape[1] // 2
            blk = outs[i].at[2 * chip[0] + chip[1], pl.ds(which * h, h)]
            return pltpu.make_async_remote_copy(
                src_ref=blk, dst_ref=blk, send_sem=send_sems.at[3 * i + j], recv_sem=recv_sems.at[3 * i + j],
                device_id=(x, y, 1 - c), device_id_type=MESH)

        sends = [copy(i, j, chip, c) for i in range(n_w) for j, chip in enumerate(chips)]
        for cp in sends:
            cp.start()
        for i in range(n_w):
            for j, chip in enumerate(chips):
                copy(i, j, chip, 1 - c).wait_recv()
        for cp in sends:
            cp.wait_send()

    return pl.pallas_call(
        body, in_specs=[ANY] * n_w, out_specs=[ANY] * n_w,
        out_shape=[jax.ShapeDtypeStruct(b.shape, b.dtype) for b in bufs],
        input_output_aliases={i: i for i in range(n_w)},
        scratch_shapes=[pltpu.SemaphoreType.DMA((3 * n_w,)), pltpu.SemaphoreType.DMA((3 * n_w,))],
        name=name,
    )(*bufs)


def _ici_scatter_copy(part, buf, i, j, chip, send_sems, recv_sems):
    x, y, c, _ = _place()
    return pltpu.make_async_remote_copy(
        src_ref=part.at[2 * chip[0] + chip[1]], dst_ref=buf.at[2 * x + y], send_sem=send_sems.at[3 * i + j],
        recv_sem=recv_sems.at[3 * i + j], device_id=(*chip, c), device_id_type=MESH)


def _ici_scatter_landing(buf, i, j, chip, send_sems, recv_sems):
    x, y, c, _ = _place()
    blk = buf.at[2 * chip[0] + chip[1]]
    return pltpu.make_async_remote_copy(
        src_ref=blk, dst_ref=blk, send_sem=send_sems.at[3 * i + j], recv_sem=recv_sems.at[3 * i + j],
        device_id=(*chip, c), device_id_type=MESH)


def _scatter_start(name, parts, bufs):
    n_w = len(parts)

    def body(*refs):
        ps, bs = refs[:n_w], refs[n_w:2 * n_w]
        send_sems, recv_sems, token = refs[2 * n_w], refs[2 * n_w + 1], refs[-1]
        chips = _place()[3]
        for i in range(n_w):
            for j, chip in enumerate(chips):
                _ici_scatter_copy(ps[i], bs[i], i, j, chip, send_sems, recv_sems).start()
        token[...] = jnp.zeros_like(token)

    both = list(parts) + list(bufs)
    res = pl.pallas_call(
        body, name=name, in_specs=[HBM_SPEC] * (2 * n_w),
        out_shape=(pltpu.SemaphoreType.DMA((3 * n_w,)), pltpu.SemaphoreType.DMA((3 * n_w,)),
                   *[pltpu.HBM(b.shape, b.dtype) for b in both], jax.ShapeDtypeStruct((8, LANES), F32)),
        out_specs=(SEM_SPEC, SEM_SPEC, *[HBM_SPEC] * (2 * n_w), VMEM_SPEC),
        input_output_aliases={i: 2 + i for i in range(2 * n_w)},
        compiler_params=pltpu.CompilerParams(has_side_effects=SIDE_EFFECT),
    )(*_in_hbm(both))
    return res[0], res[1], list(res[2:2 + n_w]), list(res[2 + n_w:2 + 2 * n_w]), res[-1]


def _scatter_wait(name, send_sems, recv_sems, parts, bufs, after):
    n_w = len(parts)

    def body(*refs):
        ps, bs = refs[:n_w], refs[n_w:2 * n_w]
        s_sems, r_sems = refs[2 * n_w], refs[2 * n_w + 1]
        chips = _place()[3]
        for i in range(n_w):
            for j, chip in enumerate(chips):
                _ici_scatter_copy(ps[i], bs[i], i, j, chip, s_sems, r_sems).wait_send()
                _ici_scatter_landing(bs[i], i, j, chip, s_sems, r_sems).wait_recv()

    both = list(parts) + list(bufs)
    res = pl.pallas_call(
        body, name=name, in_specs=[HBM_SPEC] * (2 * n_w) + [SEM_SPEC, SEM_SPEC, ANY],
        out_shape=[pltpu.HBM(b.shape, b.dtype) for b in both], out_specs=[HBM_SPEC] * (2 * n_w),
        input_output_aliases={i: i for i in range(2 * n_w)},
        compiler_params=pltpu.CompilerParams(has_side_effects=SIDE_EFFECT),
    )(*both, send_sems, recv_sems, after)
    return list(res[n_w:])


def _reduce_scatter_begin(tag, grads, idx, after=None):
    from_sibling = _swap_other_half("swap_other_half_" + tag, grads, after)
    pairs = [_add_my_half("add_my_half_%s_%d" % (tag, i), g, s, idx, BF)
             for i, (g, s) in enumerate(zip(grads, from_sibling))]
    s_sems, r_sems, parts, bufs, token = _scatter_start("scatter_start_" + tag, [p[0] for p in pairs],
                                                        [p[1] for p in pairs])
    return (tag, s_sems, r_sems, parts, bufs, idx), token


def _reduce_scatter_end(state, after):
    tag, s_sems, r_sems, parts, bufs, idx = state
    received = _scatter_wait("scatter_wait_" + tag, s_sems, r_sems, parts, bufs, after)
    halves = [_sum_chips("sum_chips_%s_%d" % (tag, i), q, idx) for i, q in enumerate(received)]
    return _join_halves("join_halves_" + tag, halves)


def _relu2_epilogue(acc):
    r = jnp.maximum(acc, 0.0)
    return (r * r,)


def _relu2_bwd_epilogue(acc, s):
    return (acc * (2.0 * jnp.sqrt(s.astype(F32))),)


WIDE = (2048, 512, 2048)
DEEP = (1024, 2048, 512)


def _add_epilogue(acc, other):
    return (acc + other,)


def kernel(x, c, positions, w_ada_mix, b_ada_mix, w_ada_mlp, b_ada_mlp, g_pre_mix, g_post_mix, g_pre_mlp, g_post_mlp, conv_w_in, conv_b_in, conv_dw, conv_dw_b, conv_ln_g, conv_ln_b, conv_w_out, conv_b_out, w_ada_kv, b_ada_kv, g_kv, w_dkv, g_ckv, w_kr, w_uk, w_uv, w_dq, g_cq, w_uq, w_o, mlp_w_up, mlp_w_down, loss_target, m_w_ada_mix, m_b_ada_mix, m_w_ada_mlp, m_b_ada_mlp, m_g_pre_mix, m_g_post_mix, m_g_pre_mlp, m_g_post_mlp, m_conv_w_in, m_conv_b_in, m_conv_dw, m_conv_dw_b, m_conv_ln_g, m_conv_ln_b, m_conv_w_out, m_conv_b_out, m_w_ada_kv, m_b_ada_kv, m_g_kv, m_w_dkv, m_g_ckv, m_w_kr, m_w_uk, m_w_uv, m_w_dq, m_g_cq, m_w_uq, m_w_o, m_mlp_w_up, m_mlp_w_down, v_w_ada_mix, v_b_ada_mix, v_w_ada_mlp, v_b_ada_mlp, v_g_pre_mix, v_g_post_mix, v_g_pre_mlp, v_g_post_mlp, v_conv_w_in, v_conv_b_in, v_conv_dw, v_conv_dw_b, v_conv_ln_g, v_conv_ln_b, v_conv_w_out, v_conv_b_out, v_w_ada_kv, v_b_ada_kv, v_g_kv, v_w_dkv, v_g_ckv, v_w_kr, v_w_uk, v_w_uv, v_w_dq, v_g_cq, v_w_uq, v_w_o, v_mlp_w_up, v_mlp_w_down):
    weights = dict(w_ada_mix=w_ada_mix, b_ada_mix=b_ada_mix, w_ada_mlp=w_ada_mlp, b_ada_mlp=b_ada_mlp, g_pre_mix=g_pre_mix, g_post_mix=g_post_mix, g_pre_mlp=g_pre_mlp, g_post_mlp=g_post_mlp, conv_w_in=conv_w_in, conv_b_in=conv_b_in, conv_dw=conv_dw, conv_dw_b=conv_dw_b, conv_ln_g=conv_ln_g, conv_ln_b=conv_ln_b, conv_w_out=conv_w_out, conv_b_out=conv_b_out, w_ada_kv=w_ada_kv, b_ada_kv=b_ada_kv, g_kv=g_kv, w_dkv=w_dkv, g_ckv=g_ckv, w_kr=w_kr, w_uk=w_uk, w_uv=w_uv, w_dq=w_dq, g_cq=g_cq, w_uq=w_uq, w_o=w_o, mlp_w_up=mlp_w_up, mlp_w_down=mlp_w_down)
    m_in = dict(w_ada_mix=m_w_ada_mix, b_ada_mix=m_b_ada_mix, w_ada_mlp=m_w_ada_mlp, b_ada_mlp=m_b_ada_mlp, g_pre_mix=m_g_pre_mix, g_post_mix=m_g_post_mix, g_pre_mlp=m_g_pre_mlp, g_post_mlp=m_g_post_mlp, conv_w_in=m_conv_w_in, conv_b_in=m_conv_b_in, conv_dw=m_conv_dw, conv_dw_b=m_conv_dw_b, conv_ln_g=m_conv_ln_g, conv_ln_b=m_conv_ln_b, conv_w_out=m_conv_w_out, conv_b_out=m_conv_b_out, w_ada_kv=m_w_ada_kv, b_ada_kv=m_b_ada_kv, g_kv=m_g_kv, w_dkv=m_w_dkv, g_ckv=m_g_ckv, w_kr=m_w_kr, w_uk=m_w_uk, w_uv=m_w_uv, w_dq=m_w_dq, g_cq=m_g_cq, w_uq=m_w_uq, w_o=m_w_o, mlp_w_up=m_mlp_w_up, mlp_w_down=m_mlp_w_down)
    v_in = dict(w_ada_mix=v_w_ada_mix, b_ada_mix=v_b_ada_mix, w_ada_mlp=v_w_ada_mlp, b_ada_mlp=v_b_ada_mlp, g_pre_mix=v_g_pre_mix, g_post_mix=v_g_post_mix, g_pre_mlp=v_g_pre_mlp, g_post_mlp=v_g_post_mlp, conv_w_in=v_conv_w_in, conv_b_in=v_conv_b_in, conv_dw=v_conv_dw, conv_dw_b=v_conv_dw_b, conv_ln_g=v_conv_ln_g, conv_ln_b=v_conv_ln_b, conv_w_out=v_conv_w_out, conv_b_out=v_conv_b_out, w_ada_kv=v_w_ada_kv, b_ada_kv=v_b_ada_kv, g_kv=v_g_kv, w_dkv=v_w_dkv, g_ckv=v_g_ckv, w_kr=v_w_kr, w_uk=v_w_uk, w_uv=v_w_uv, w_dq=v_w_dq, g_cq=v_g_cq, w_uq=v_w_uq, w_o=v_w_o, mlp_w_up=v_mlp_w_up, mlp_w_down=v_mlp_w_down)
    grads, last_group = _step_grads(x, c, positions, loss_target, weights)
    loss, grad_x = grads.pop("loss"), grads.pop("x")
    names = list(weights)
    upd = {n: _adamw(n, weights[n], grads[n], m_in[n], v_in[n]) for n in names if n in grads}
    grads.update(last_group(upd["mlp_w_down"][0]))
    upd.update({n: _adamw(n, weights[n], grads[n], m_in[n], v_in[n]) for n in names if n not in upd})
    return (loss, grad_x, *[grads[n] for n in names], *[upd[n][0] for n in names], *[upd[n][1] for n in names],
            *[upd[n][2] for n in names])


def _step_grads(x, c, positions, loss_target, w):
    xi, yi, ci = lax.axis_index("x"), lax.axis_index("y"), lax.axis_index("c")
    chip = 2 * xi + yi
    dev = 2 * chip + ci
    place_idx = jnp.stack([ci, chip]).astype(jnp.int32)
    t, d = x.shape[1], x.shape[2]
    dl = d // N_CHIPS
    r_kv, r_q = w["w_dkv"].shape[1], w["w_dq"].shape[2]
    n_ada, n_kvada = w["w_ada_mix"].shape[2], w["w_ada_kv"].shape[1]
    heads_l = w["w_uq"].shape[2] // (QK_NOPE + QK_ROPE)
    assert dl == r_kv == r_q and dl % LANES == 0 and heads_l * N_CHIPS % 2 == 0

    def chip_cols(a, width):
        return lax.dynamic_slice_in_dim(a, chip * width, width, axis=a.ndim - 1)

    uq = w["w_uq"][0].reshape(r_q, heads_l, QK_NOPE + QK_ROPE)
    uq_nope, uq_rope = uq[:, :, :QK_NOPE].reshape(r_q, -1), uq[:, :, QK_NOPE:].reshape(r_q, -1)
    kr_pad = jnp.pad(w["w_kr"], ((0, 0), (0, LANES - QK_ROPE)))
    small = jnp.concatenate([w["w_dkv"], kr_pad, w["w_dq"][0], w["w_uk"], w["w_uv"], uq_nope, uq_rope], axis=1)
    widths = [r_kv, LANES, r_q, w["w_uk"].shape[1], w["w_uv"].shape[1], uq_nope.shape[1], uq_rope.shape[1]]
    so = [0]
    for wd in widths:
        so.append(so[-1] + wd)
    big = dict(conv_w_in=w["conv_w_in"][0], conv_w_out=w["conv_w_out"][0], w_o=w["w_o"][0],
               up0=w["mlp_w_up"][0], up1=w["mlp_w_up"][1], down0=w["mlp_w_down"][0], down1=w["mlp_w_down"][1],
               small=small)
    groups = dict(g0=["conv_w_in", "conv_w_out"], g1=["up0", "down0"], g2=["small", "w_o"], g3=["up1", "down1"])
    gathered, pending, started, token = {}, {}, jnp.zeros((1, 1), F32), None
    for gname in ("g0", "g1", "g2", "g3"):
        slots = [_cast_into_slot(n, big[n], place_idx) for n in groups[gname]]
        s_sems, r_sems, bufs, token = _gather_start("gather_start_" + gname, slots, after=token)
        pending[gname] = (s_sems, r_sems, bufs)
        started = started + token[0:1, 0:1]

    def finish_gather(gname, after):
        s_sems, r_sems, bufs = pending[gname]
        bufs = _gather_wait("gather_wait_" + gname, s_sems, r_sems, bufs, after)
        gathered.update(zip(groups[gname], _forward_to_sibling("forward_" + gname, bufs)))

    rowl = lambda a: a.reshape(-1, a.shape[2])
    w_up_f = lambda l: gathered["up%d" % l]
    w_down_f = lambda l: rowl(gathered["down%d" % l])

    pack_rows = [w["conv_dw"][0], w["conv_dw_b"], w["conv_ln_g"], w["conv_ln_b"], w["conv_b_out"],
                 w["conv_b_in"].reshape(2, dl), c.reshape(N_CHIPS, dl) + started]
    pack_rows = [jnp.pad(p, ((0, (-p.shape[0]) % 8), (0, 0))) for p in pack_rows]
    po = [0]
    for p in pack_rows:
        po.append(po[-1] + p.shape[0])
    packs = _all_gather_small("gather_params", jnp.concatenate(pack_rows, axis=0))
    by_chip = packs[0::2]

    def full_width(r0, nr):
        return jnp.transpose(by_chip[:, r0:r0 + nr, :], (1, 0, 2)).reshape(nr, d)

    dw_full, dwb_full = full_width(po[0], CONV_WIDTH), full_width(po[1], 1)
    lng_full, lnb_full, bout_full = full_width(po[2], 1), full_width(po[3], 1), full_width(po[4], 1)
    bin_full = by_chip[:, po[5]:po[5] + 2, :].reshape(1, 2 * d)
    c_all = packs[:, po[6]:po[6] + N_CHIPS, :].reshape(N_DEV, d)

    sc_all = _ew("silu_c", lambda a: (a * jax.nn.sigmoid(a),), [c_all], (F32,))[0]
    ada_w = [w["w_ada_mix"][0], w["w_ada_mix"][1], w["w_ada_mlp"][0], w["w_ada_mlp"][1], w["w_ada_kv"]]
    ada_b = [chip_cols(w["b_ada_mix"][0:1], n_ada), chip_cols(w["b_ada_mix"][1:2], n_ada),
             chip_cols(w["b_ada_mlp"][0:1], n_ada), chip_cols(w["b_ada_mlp"][1:2], n_ada),
             chip_cols(w["b_ada_kv"].reshape(1, -1), n_kvada)]
    mods = [_mm_nn("ada_fwd_%d" % i, sc_all, wi, "row", extras=(jnp.broadcast_to(bi, (N_DEV, bi.shape[1])),),
                   epilogue=_add_epilogue) for i, (wi, bi) in enumerate(zip(ada_w, ada_b))]
    mods_all = _all_gather_small("gather_mods", jnp.concatenate(mods, axis=1))
    mine = lax.dynamic_index_in_dim(mods_all[0::2], dev, axis=1, keepdims=False)
    offs = [0]
    for m_ in mods:
        offs.append(offs[-1] + m_.shape[1])
    mod_vec = [mine[:, offs[i]:offs[i + 1]].reshape(1, -1) for i in range(5)]
    split3 = lambda v: (v[:, :d], v[:, d:2 * d], v[:, 2 * d:])
    mix = [split3(mod_vec[0]), split3(mod_vec[1])]
    mlp = [split3(mod_vec[2]), split3(mod_vec[3])]
    kv_shift, kv_scale = mod_vec[4][:, :d], mod_vec[4][:, d:]

    cos_t, sin_t = _rope_tables(positions, t)
    vec = lambda a: a.reshape(1, -1)
    gpm, gqm = w["g_pre_mix"], w["g_post_mix"]
    gpl, gql = w["g_pre_mlp"], w["g_post_mlp"]
    h0 = x[0]
    after_token = lambda v, token: v + token[0:1, 0:1]

    def mlp_fwd(l, h):
        sh, sc, _ = mlp[l]
        (hn,) = _rw_fwd("mlp_pre_%d" % l, _f_pre, [h], [vec(gpl[l]), sh, sc], (BF,))
        s = _mm_nn("mlp_up_%d" % l, hn, w_up_f(l), "col", o_dtypes=(BF,), epilogue=_relu2_epilogue, tiles=WIDE)
        y = _mm_nn("mlp_down_%d" % l, s, w_down_f(l), "row", tiles=DEEP)
        return hn, s, y

    def post_fwd(name, h, y, gate, g, bias=None):
        if bias is None:
            return _rw_fwd(name, lambda h_, y_, gt, g_: (h_ + _f_post(y_, gt, g_)[0],), [h, y], [gate, g], (F32,))[0]
        return _rw_fwd(name, lambda h_, y_, b_, gt, g_: (h_ + _f_post_bias(y_, b_, gt, g_)[0],), [h, y],
                       [bias, gate, g], (F32,))[0]

    (hn0,) = _rw_fwd("conv_pre", _f_pre, [h0], [vec(gpm[0]), mix[0][0] + started, mix[0][1]], (BF,))
    finish_gather("g0", hn0)
    w_in_f, w_out_f = gathered["conv_w_in"], rowl(gathered["conv_w_out"])
    z0 = _mm_nn("conv_in", hn0, w_in_f, "col")
    (u0,) = _rw_fwd("conv_glu", _f_glu, [z0], [bin_full], (F32,))
    uc0 = _conv_fwd(u0, dw_full, dwb_full)
    (n0,) = _rw_fwd("conv_ln", _f_ln_silu, [uc0], [lng_full, lnb_full], (BF,))
    y0 = _mm_nn("conv_out", n0, w_out_f, "row")
    h1 = post_fwd("conv_post", h0, y0, mix[0][2], vec(gqm[0]), bias=bout_full)
    finish_gather("g1", h1)
    hn1, s1, y1 = mlp_fwd(0, h1)
    h2 = post_fwd("mlp_post_0", h1, y1, mlp[0][2], vec(gql[0]))

    finish_gather("g2", h2)
    gs = gathered["small"]
    w_dkvkr = rowl(gs[:, :, so[0]:so[2]])
    w_dq_f = rowl(gs[:, :, so[2]:so[3]])
    w_uk_f, w_uv_f = gs[:, :, so[3]:so[4]], gs[:, :, so[4]:so[5]]
    w_qn_f, w_qr_f = gs[:, :, so[5]:so[6]], gs[:, :, so[6]:so[7]]
    w_o_f = rowl(gathered["w_o"])
    kvn, hn2 = _rw_fwd("mla_pre", _f_pre2, [h2], [vec(w["g_kv"]), kv_shift, kv_scale, vec(gpm[1]), mix[1][0], mix[1][1]],
                       (BF, BF))
    pre_kv = _mm_nn("mla_dkv", kvn, w_dkvkr, "row")
    pre_q = _mm_nn("mla_dq", hn2, w_dq_f, "row")
    f_lat = _make_f_lat(r_kv)
    c_kv, kr, c_q = _rw_fwd("mla_latent", f_lat, [pre_kv, pre_q, cos_t, sin_t], [vec(w["g_ckv"]), vec(w["g_cq"][0])],
                            (BF, BF, BF))
    kn = _mm_nn("mla_uk", c_kv, w_uk_f, "col", o_dtypes=(BF,))
    vv = _mm_nn("mla_uv", c_kv, w_uv_f, "col", o_dtypes=(BF,))
    qn = _mm_nn("mla_uq_nope", c_q, w_qn_f, "col", o_dtypes=(BF,))
    qr_pre = _mm_nn("mla_uq_rope", c_q, w_qr_f, "col")
    (qr,) = _rw_fwd("mla_q_rope", _f_rope, [qr_pre, cos_t, sin_t], [], (BF,))
    att = _attn_fwd(qn, qr, kn, kr, vv)
    y2 = _mm_nn("mla_o", att, w_o_f, "row")
    h3 = post_fwd("mla_post", h2, y2, mix[1][2], vec(gqm[1]))
    finish_gather("g3", h3)
    hn3, s3, y3 = mlp_fwd(1, h3)
    h4 = post_fwd("mlp_post_1", h3, y3, mlp[1][2], vec(gql[1]))

    dh4, loss_part = _loss_grad(h4, loss_target[0])
    loss = lax.psum(loss_part[0, 0], ("x", "y", "c"))

    gw = {}
    gvec = {}

    dw_mm = functools.partial(_mm_tn, o_dtype=BF)

    def mlp_bwd(l, h_in, hn, s, y, dh, token=None):
        sh, sc, gate = mlp[l]
        if token is not None:
            gate = after_token(gate, token)
        (dy,), (dgate, dgq) = _rw_bwd("mlp_post_bwd_%d" % l, _f_post, [y], [gate, vec(gql[l])], [dh], [BF])
        gw["down%d" % l] = dw_mm("mlp_down_dw_%d" % l, s, dy, "row", tiles=(512, 2048, 2048))
        da = _mm_nt("mlp_down_dx_%d" % l, dy, w_down_f(l), "row", o_dtypes=(BF,), extras=(s,),
                    epilogue=_relu2_bwd_epilogue, tiles=WIDE)
        gw["up%d" % l] = dw_mm("mlp_up_dw_%d" % l, hn, da, "col", tiles=WIDE)
        dhn = _mm_nt("mlp_up_dx_%d" % l, da, w_up_f(l), "col", tiles=DEEP)
        (dh_in,), (dgp, dsh, dsc) = _rw_bwd("mlp_pre_bwd_%d" % l, _f_pre, [h_in], [vec(gpl[l]), sh, sc], [dhn], [F32],
                                            add_rows={0: dh})
        gvec["mlp%d" % l] = (dsh, dsc, dgate)
        gvec["g_pre_mlp%d" % l], gvec["g_post_mlp%d" % l] = dgp, dgq
        return dh_in

    chunked = lambda a: a.reshape(N_CHIPS, a.shape[0] // N_CHIPS, a.shape[1])
    to_chunks = lambda a: a if a.ndim == 3 else chunked(a)
    red = {}

    def reduce_begin(tag, names, after=None):
        state, token = _reduce_scatter_begin(tag, [to_chunks(gw[n]) for n in names], place_idx, after)
        return (names, state), token

    def reduce_end(handle, after):
        names, state = handle
        red.update(zip(names, _reduce_scatter_end(state, after)))

    dh3 = mlp_bwd(1, h3, hn3, s3, y3, dh4)
    rs1, token = reduce_begin("r1", ["up1", "down1"])

    (dy2,), (dgate, dgq) = _rw_bwd("mla_post_bwd", _f_post, [y2], [after_token(mix[1][2], token), vec(gqm[1])],
                                   [dh3], [BF])
    gvec["g_post_mix1"] = dgq
    gw["w_o"] = dw_mm("mla_o_dw", att, dy2, "row")
    datt = _mm_nt("mla_o_dx", dy2, w_o_f, "row", o_dtypes=(BF,))
    dqn, dqr, dkn, dvv, dkr = _attn_bwd(qn, qr, kn, kr, vv, datt)
    (dqr_pre,), _ = _rw_bwd("mla_q_rope_bwd", _f_rope, [qr_pre, cos_t, sin_t], [], [dqr], [BF, None, None])
    g_qn = dw_mm("mla_uq_nope_dw", c_q, dqn, "col")
    g_qr = dw_mm("mla_uq_rope_dw", c_q, dqr_pre, "col")
    dc_q = _mm_nt("mla_uq_nope_dx", dqn, w_qn_f, "col")
    dc_q = _mm_nt("mla_uq_rope_dx", dqr_pre, w_qr_f, "col", extras=(dc_q,), epilogue=_add_epilogue)
    g_uk = dw_mm("mla_uk_dw", c_kv, dkn, "col")
    g_uv = dw_mm("mla_uv_dw", c_kv, dvv, "col")
    dc_kv = _mm_nt("mla_uk_dx", dkn, w_uk_f, "col")
    dc_kv = _mm_nt("mla_uv_dx", dvv, w_uv_f, "col", extras=(dc_kv,), epilogue=_add_epilogue)
    (dpre_kv, dpre_q), (dg_ckv, dg_cq) = _rw_bwd(
        "mla_latent_bwd", f_lat, [pre_kv, pre_q, cos_t, sin_t], [vec(w["g_ckv"]), vec(w["g_cq"][0])],
        [dc_kv, dkr, dc_q], [BF, BF, None, None])
    gvec["g_ckv"], gvec["g_cq"] = dg_ckv, dg_cq
    g_dkvkr = dw_mm("mla_dkv_dw", kvn, dpre_kv, "row")
    g_dq = dw_mm("mla_dq_dw", hn2, dpre_q, "row")
    dkvn = _mm_nt("mla_dkv_dx", dpre_kv, w_dkvkr, "row")
    dhn2 = _mm_nt("mla_dq_dx", dpre_q, w_dq_f, "row")
    (dh2,), (dg_kv, dkvsh, dkvsc, dgp, dsh, dsc) = _rw_bwd(
        "mla_pre_bwd", _f_pre2, [h2], [vec(w["g_kv"]), kv_shift, kv_scale, vec(gpm[1]), mix[1][0], mix[1][1]],
        [dkvn, dhn2], [F32], add_rows={0: dh3})
    gvec["mix1"] = (dsh, dsc, dgate)
    gvec["kv"] = (dkvsh, dkvsc)
    gvec["g_kv"], gvec["g_pre_mix1"] = dg_kv, dgp
    gw["small"] = jnp.concatenate([chunked(g_dkvkr), chunked(g_dq), g_uk, g_uv, g_qn, g_qr], axis=2)
    reduce_end(rs1, dh2)
    rs2, token = reduce_begin("r2", ["small", "w_o"])

    dh1 = mlp_bwd(0, h1, hn1, s1, y1, dh2, token=token)
    reduce_end(rs2, dh1)
    rs3, token = reduce_begin("r3", ["up0", "down0"])

    (dy0,), (dbout, dgate, dgq) = _rw_bwd("conv_post_bwd", _f_post_bias, [y0],
                                          [bout_full, after_token(mix[0][2], token), vec(gqm[0])], [dh1], [BF])
    gvec["g_post_mix0"] = dgq
    gw["conv_w_out"] = dw_mm("conv_out_dw", n0, dy0, "row")
    dn0 = _mm_nt("conv_out_dx", dy0, w_out_f, "row")
    (duc0,), (dlng, dlnb) = _rw_bwd("conv_ln_bwd", _f_ln_silu, [uc0], [lng_full, lnb_full], [dn0], [F32])
    du0, ddw = _conv_bwd(u0, duc0, dw_full)
    (dz0,), (dbin,) = _rw_bwd("conv_glu_bwd", _f_glu, [z0], [bin_full], [du0], [BF])
    gw["conv_w_in"] = dw_mm("conv_in_dw", hn0, dz0, "col")
    dhn0 = _mm_nt("conv_in_dx", dz0, w_in_f, "col")
    (dx,), (dgp, dsh, dsc) = _rw_bwd("conv_pre_bwd", _f_pre, [h0], [vec(gpm[0]), mix[0][0], mix[0][1]], [dhn0], [F32],
                                     add_rows={0: dh1})
    gvec["mix0"] = (dsh, dsc, dgate)
    gvec["g_pre_mix0"] = dgp
    reduce_end(rs3, dx)

    vec_list = [*gvec["mix0"], *gvec["mix1"], *gvec["mlp0"], *gvec["mlp1"], *gvec["kv"],
                gvec["g_pre_mix0"], gvec["g_pre_mix1"], gvec["g_post_mix0"], gvec["g_post_mix1"],
                gvec["g_pre_mlp0"], gvec["g_pre_mlp1"], gvec["g_post_mlp0"], gvec["g_post_mlp1"],
                gvec["g_kv"], gvec["g_ckv"], gvec["g_cq"], dbin, dlng, dlnb, dbout, ddw.reshape(1, -1)]
    vo = [0]
    for v_ in vec_list:
        vo.append(vo[-1] + v_.shape[1])
    vec_pad = (-vo[-1]) % (8 * LANES)
    n_vec = vo[-1] + vec_pad
    flat = jnp.concatenate(vec_list + [jnp.zeros((1, vec_pad), F32)], axis=1).reshape(8, n_vec // 8)
    all_vecs = _all_gather_small("gather_vector_grads", flat).reshape(N_DEV, 8, n_vec // 8)
    rs4, token = reduce_begin("r4", ["conv_w_in", "conv_w_out"], after=all_vecs)
    all_vecs = after_token(all_vecs, token)
    summed = _sum_devices(all_vecs).reshape(1, n_vec)
    per_dev = all_vecs.reshape(N_DEV, n_vec)
    seg = lambda a, i: a[:, vo[i]:vo[i + 1]]

    out = {"loss": loss, "x": dx.reshape(x.shape)}
    dm_mix = [jnp.concatenate([seg(per_dev, 3 * l + i) for i in range(3)], axis=1) for l in range(2)]
    dm_mlp = [jnp.concatenate([seg(per_dev, 6 + 3 * l + i) for i in range(3)], axis=1) for l in range(2)]
    dm_kv = jnp.concatenate([seg(per_dev, 12), seg(per_dev, 13)], axis=1)
    ada_dw = lambda name, dm, width: _mm_tn(name, sc_all, chip_cols(dm, width), "row")
    out["w_ada_mix"] = jnp.stack([ada_dw("ada_mix_dw_%d" % l, dm_mix[l], n_ada) for l in range(2)])
    out["w_ada_mlp"] = jnp.stack([ada_dw("ada_mlp_dw_%d" % l, dm_mlp[l], n_ada) for l in range(2)])
    out["w_ada_kv"] = ada_dw("ada_kv_dw", dm_kv, n_kvada)
    sum_seg = lambda i: seg(summed, i)
    out["b_ada_mix"] = jnp.concatenate([jnp.concatenate([sum_seg(3 * l + i) for i in range(3)], axis=1) for l in range(2)], axis=0)
    out["b_ada_mlp"] = jnp.concatenate([jnp.concatenate([sum_seg(6 + 3 * l + i) for i in range(3)], axis=1) for l in range(2)], axis=0)
    out["b_ada_kv"] = jnp.concatenate([sum_seg(12), sum_seg(13)], axis=1).reshape(-1)
    out["g_pre_mix"] = jnp.concatenate([sum_seg(14), sum_seg(15)], axis=0)
    out["g_post_mix"] = jnp.concatenate([sum_seg(16), sum_seg(17)], axis=0)
    out["g_pre_mlp"] = jnp.concatenate([sum_seg(18), sum_seg(19)], axis=0)
    out["g_post_mlp"] = jnp.concatenate([sum_seg(20), sum_seg(21)], axis=0)
    out["g_kv"] = sum_seg(22).reshape(-1)
    out["g_ckv"] = sum_seg(23).reshape(-1)
    out["g_cq"] = sum_seg(24)
    out["conv_b_in"] = chip_cols(sum_seg(25), 2 * dl)
    out["conv_ln_g"] = chip_cols(sum_seg(26), dl)
    out["conv_ln_b"] = chip_cols(sum_seg(27), dl)
    out["conv_b_out"] = chip_cols(sum_seg(28), dl)
    ddw_sum = chip_cols(sum_seg(29).reshape(CONV_WIDTH + 1, d), dl)
    out["conv_dw"] = ddw_sum[:CONV_WIDTH].reshape(1, CONV_WIDTH, dl)
    out["conv_dw_b"] = ddw_sum[CONV_WIDTH:]

    def last_group(after):
        reduce_end(rs4, after)
        return {"conv_w_in": red["conv_w_in"][None], "conv_w_out": red["conv_w_out"][None]}

    out["w_o"] = red["w_o"][None]
    out["mlp_w_up"] = jnp.stack([red["up0"], red["up1"]])
    out["mlp_w_down"] = jnp.stack([red["down0"], red["down1"]])
    rs = red["small"]
    piece = lambda i: rs[:, so[i]:so[i + 1]]
    out["w_dkv"] = piece(0)
    out["w_kr"] = piece(1)[:, :QK_ROPE]
    out["w_dq"] = piece(2)[None]
    out["w_uk"], out["w_uv"] = piece(3), piece(4)
    out["w_uq"] = jnp.concatenate([piece(5).reshape(r_q, heads_l, QK_NOPE), piece(6).reshape(r_q, heads_l, QK_ROPE)],
                                  axis=2).reshape(1, r_q, -1)
    return out, last_group
```

```python
import functools

import jax
import jax.numpy as jnp
from jax import lax
from jax.experimental import pallas as pl
from jax.experimental.pallas import tpu as pltpu

F32 = jnp.float32
BF = jnp.bfloat16
MXU_DTYPE = BF

EPS = 1e-6
NEG = -1e30
ROPE_THETA = 10000.0
QK_NOPE = 128
QK_ROPE = 64
V_HEAD = 128
CONV_WIDTH = 31
ADAM_LR, ADAM_B1, ADAM_B2, ADAM_EPS, ADAM_WD, ADAM_STEP = 0.001, 0.9, 0.999, 1e-08, 0.01, 10

N_CHIPS = 4
N_DEV = 8
LANES = 128
VMEM_LIMIT = 56 * 2 ** 20
MESH = pl.DeviceIdType.MESH
ANY = pl.BlockSpec(memory_space=pl.ANY)


def _params(sem=None):
    return pltpu.CompilerParams(dimension_semantics=sem, vmem_limit_bytes=VMEM_LIMIT)


def _tile(n, pref, unit=LANES):
    if n <= pref:
        return n
    t = (pref // unit) * unit
    while t > unit and n % t:
        t -= unit
    assert n % t == 0, (n, pref)
    return t


def _dg(a, b, ca, cb):
    return lax.dot_general(a.astype(MXU_DTYPE), b.astype(MXU_DTYPE), (((ca,), (cb,)), ((), ())),
                           preferred_element_type=F32)


@jax.custom_vjp
def _dot_nn(a, b):
    return _dg(a, b, 1, 0)


def _dot_nn_fwd(a, b):
    return _dg(a, b, 1, 0), (a, b)


def _dot_nn_bwd(res, g):
    a, b = res
    return _dg(g, b, 1, 1).astype(a.dtype), _dg(a, g, 0, 0).astype(b.dtype)


_dot_nn.defvjp(_dot_nn_fwd, _dot_nn_bwd)


@jax.custom_vjp
def _dot_nt(a, b):
    return _dg(a, b, 1, 1)


def _dot_nt_fwd(a, b):
    return _dg(a, b, 1, 1), (a, b)


def _dot_nt_bwd(res, g):
    a, b = res
    return _dg(g, b, 1, 0).astype(a.dtype), _dg(g, a, 0, 0).astype(b.dtype)


_dot_nt.defvjp(_dot_nt_fwd, _dot_nt_bwd)


def _matmul(name, a, b, *, ca, cb, grid, a_blk, a_map, b_blk, b_map, o_shape, o_dtypes, o_blk, o_map,
            extras=(), epilogue=None):
    nk = grid[-1]
    n_ex, n_out = len(extras), len(o_dtypes)

    def body(*refs):
        a_ref, b_ref = refs[0], refs[1]
        ex_refs = refs[2:2 + n_ex]
        out_refs = refs[2 + n_ex:2 + n_ex + n_out]
        kk = pl.program_id(len(grid) - 1)

        def finish(acc):
            outs = (acc,) if epilogue is None else epilogue(acc, *[r[...] for r in ex_refs])
            for r, o in zip(out_refs, outs):
                r[...] = o.astype(r.dtype)

        part = _dg(a_ref[...], b_ref[...], ca, cb)
        if nk == 1:
            finish(part)
        else:
            acc_ref = refs[-1]

            @pl.when(kk == 0)
            def _():
                acc_ref[...] = part

            @pl.when(kk > 0)
            def _():
                acc_ref[...] += part

            @pl.when(kk == nk - 1)
            def _():
                finish(acc_ref[...])

    o_spec = pl.BlockSpec(o_blk, o_map)
    acc_shape = tuple(d for d in o_blk if d is not None)
    res = pl.pallas_call(
        body, grid=grid,
        in_specs=[pl.BlockSpec(a_blk, a_map), pl.BlockSpec(b_blk, b_map)] + [o_spec] * n_ex,
        out_specs=[o_spec] * n_out,
        out_shape=[jax.ShapeDtypeStruct(o_shape, dt) for dt in o_dtypes],
        scratch_shapes=[] if nk == 1 else [pltpu.VMEM(acc_shape, F32)],
        compiler_params=_params(("parallel",) * (len(grid) - 1) + ("arbitrary",)),
        name=name,
    )(a, b, *extras)
    return res[0] if n_out == 1 else res


TM, TN, TK = 1024, 512, 1024


def _mm_nn(name, a, w, layout, o_dtypes=(F32,), extras=(), epilogue=None, tiles=(TM, TN, TK), w_layer=None):
    m, k = a.shape
    tm, tk = _tile(m, tiles[0], 8), _tile(k, tiles[2])
    if layout == "row":
        n = w.shape[-1]
        tn = _tile(n, tiles[1])
        grid = (m // tm, n // tn, k // tk)
        b_blk, b_map = (tk, tn), (lambda i, j, kk: (kk, j))
        if w_layer is not None:
            b_blk, b_map = (None, tk, tn), (lambda i, j, kk: (w_layer, kk, j))
    else:
        nl = w.shape[2]
        n = nl * w.shape[0]
        tn = _tile(nl, tiles[1])
        nb = nl // tn
        grid = (m // tm, n // tn, k // tk)
        b_blk, b_map = (None, tk, tn), (lambda i, j, kk: (j // nb, kk, j % nb))
    return _matmul(name, a, w, ca=1, cb=0, grid=grid, a_blk=(tm, tk), a_map=lambda i, j, kk: (i, kk),
                   b_blk=b_blk, b_map=b_map, o_shape=(m, n), o_dtypes=o_dtypes, o_blk=(tm, tn),
                   o_map=lambda i, j, kk: (i, j), extras=extras, epilogue=epilogue)


def _mm_nt(name, g, w, layout, o_dtypes=(F32,), extras=(), epilogue=None, tiles=(TM, TN, TK)):
    m, n = g.shape
    tm, tn = _tile(m, tiles[0], 8), None
    if layout == "row":
        k = w.shape[0]
        tn = _tile(n, tiles[2])
        tk = _tile(k, tiles[1])
        grid = (m // tm, k // tk, n // tn)
        b_blk, b_map = (tk, tn), (lambda i, j, kk: (j, kk))
    else:
        k, nl = w.shape[1], w.shape[2]
        tn = _tile(nl, tiles[2])
        nb = nl // tn
        tk = _tile(k, tiles[1])
        grid = (m // tm, k // tk, n // tn)
        b_blk, b_map = (None, tk, tn), (lambda i, j, kk: (kk // nb, j, kk % nb))
    return _matmul(name, g, w, ca=1, cb=1, grid=grid, a_blk=(tm, tn), a_map=lambda i, j, kk: (i, kk),
                   b_blk=b_blk, b_map=b_map, o_shape=(m, k), o_dtypes=o_dtypes, o_blk=(tm, tk),
                   o_map=lambda i, j, kk: (i, j), extras=extras, epilogue=epilogue)


def _mm_tn(name, x, g, layout, o_dtype=F32, tiles=(TM, TN, TK)):
    t, k = x.shape
    n = g.shape[1]
    tt = _tile(t, tiles[2], 8)
    tk = _tile(k, tiles[0])
    if layout == "row":
        tn = _tile(n, tiles[1])
        o_shape, o_blk, o_map = (k, n), (tk, tn), (lambda i, j, kk: (i, j))
    else:
        nl = n // N_CHIPS
        tn = _tile(nl, tiles[1])
        nb = nl // tn
        o_shape, o_blk, o_map = (N_CHIPS, k, nl), (None, tk, tn), (lambda i, j, kk: (j // nb, i, j % nb))
    grid = (k // tk, n // tn, t // tt)
    return _matmul(name, x, g, ca=0, cb=0, grid=grid, a_blk=(tt, tk), a_map=lambda i, j, kk: (kk, i),
                   b_blk=(tt, tn), b_map=lambda i, j, kk: (kk, j), o_shape=o_shape, o_dtypes=(o_dtype,),
                   o_blk=o_blk, o_map=o_map)


TR = 256


def _rw_fwd(name, fn, rows, vecs, o_dtypes, tr=TR):
    t = rows[0].shape[0]
    tr = min(tr, t)
    n_r, n_v = len(rows), len(vecs)
    o_sds = jax.eval_shape(fn, *[jax.ShapeDtypeStruct((tr, r.shape[1]), r.dtype) for r in rows],
                           *[jax.ShapeDtypeStruct(v.shape, v.dtype) for v in vecs])

    def body(*refs):
        outs = fn(*[r[...] for r in refs[:n_r + n_v]])
        for r, o in zip(refs[n_r + n_v:], outs):
            r[...] = o.astype(r.dtype)

    res = pl.pallas_call(
        body, grid=(t // tr,),
        in_specs=[pl.BlockSpec((tr, r.shape[1]), lambda i: (i, 0)) for r in rows]
        + [pl.BlockSpec(v.shape, lambda i: (0, 0)) for v in vecs],
        out_specs=[pl.BlockSpec((tr, o.shape[1]), lambda i: (i, 0)) for o in o_sds],
        out_shape=[jax.ShapeDtypeStruct((t, o.shape[1]), dt) for o, dt in zip(o_sds, o_dtypes)],
        compiler_params=_params(("parallel",)), name=name,
    )(*rows, *vecs)
    return res


def _rw_bwd(name, fn, rows, vecs, cots, row_grad_dtypes, add_rows=None, tr=TR):
    t = rows[0].shape[0]
    tr = min(tr, t)
    add_rows = add_rows or {}
    n_r, n_v, n_c = len(rows), len(vecs), len(cots)
    diff = [i for i, dt in enumerate(row_grad_dtypes) if dt is not None]
    adds = [add_rows[i] for i in diff if i in add_rows]
    n_a = len(adds)

    def body(*refs):
        rvals = [r[...] for r in refs[:n_r]]
        vvals = [r[...] for r in refs[n_r:n_r + n_v]]
        c_refs = refs[n_r + n_v:n_r + n_v + n_c]
        a_refs = list(refs[n_r + n_v + n_c:n_r + n_v + n_c + n_a])
        o_refs = refs[n_r + n_v + n_c + n_a:]

        def f(*d):
            full = list(rvals)
            for i, x in zip(diff, d[:len(diff)]):
                full[i] = x
            return fn(*full, *d[len(diff):])

        outs, vjp = jax.vjp(f, *[rvals[i] for i in diff], *vvals)
        grads = vjp(tuple(c[...].astype(o.dtype) for c, o in zip(c_refs, outs)))
        for n, i in enumerate(diff):
            gr = grads[n].astype(F32)
            if i in add_rows:
                gr = gr + a_refs.pop(0)[...].astype(F32)
            o_refs[n][...] = gr.astype(o_refs[n].dtype)
        first = pl.program_id(0) == 0
        for r, gv in zip(o_refs[len(diff):], grads[len(diff):]):
            @pl.when(first)
            def _(r=r, gv=gv):
                r[...] = gv

            @pl.when(jnp.logical_not(first))
            def _(r=r, gv=gv):
                r[...] += gv

    row_spec = lambda a: pl.BlockSpec((tr, a.shape[1]), lambda i: (i, 0))
    vec_spec = lambda a: pl.BlockSpec(a.shape, lambda i: (0, 0))
    res = pl.pallas_call(
        body, grid=(t // tr,),
        in_specs=[row_spec(r) for r in rows] + [vec_spec(v) for v in vecs] + [row_spec(c) for c in cots]
        + [row_spec(a) for a in adds],
        out_specs=[row_spec(rows[i]) for i in diff] + [vec_spec(v) for v in vecs],
        out_shape=[jax.ShapeDtypeStruct(rows[i].shape, row_grad_dtypes[i]) for i in diff]
        + [jax.ShapeDtypeStruct(v.shape, F32) for v in vecs],
        compiler_params=_params(("arbitrary",)), name=name,
    )(*rows, *vecs, *cots, *adds)
    return res[:len(diff)], res[len(diff):]


def _rms(x, g):
    return x * lax.rsqrt(jnp.mean(x * x, axis=-1, keepdims=True) + EPS) * g


def _f_pre(h, g, shift, scale):
    return (_rms(h, g) * (1.0 + scale) + shift,)


def _f_pre2(h, g1, sh1, sc1, g2, sh2, sc2):
    return _f_pre(h, g1, sh1, sc1) + _f_pre(h, g2, sh2, sc2)


def _f_post(y, gate, g):
    return (gate * _rms(y, g),)


def _f_post_bias(y, bias, gate, g):
    return (gate * _rms(y + bias, g),)


def _f_glu(z, bias):
    z = z + bias
    half = z.shape[1] // 2
    return (z[:, :half] * jax.nn.sigmoid(z[:, half:]),)


def _f_ln_silu(u, g, b):
    mu = jnp.mean(u, axis=-1, keepdims=True)
    var = jnp.mean(jnp.square(u - mu), axis=-1, keepdims=True)
    y = (u - mu) * lax.rsqrt(var + EPS) * g + b
    return (y * jax.nn.sigmoid(y),)


def _rope_raw(x, cos, sin):
    n = x.shape[1]
    reps = n // LANES
    if reps > 1:
        cos = jnp.concatenate([cos] * reps, axis=1)
        sin = jnp.concatenate([sin] * reps, axis=1)
    lane = lax.broadcasted_iota(jnp.int32, x.shape, 1)
    half = QK_ROPE // 2
    partner = jnp.where((lane % QK_ROPE) < half, pltpu.roll(x, n - half, 1), pltpu.roll(x, half, 1))
    return x * cos + partner * sin


@jax.custom_vjp
def _rope(x, cos, sin):
    return _rope_raw(x, cos, sin)


def _rope_fwd(x, cos, sin):
    return _rope_raw(x, cos, sin), (cos, sin)


def _rope_bwd(res, g):
    cos, sin = res
    return _rope_raw(g, cos, -sin), jnp.zeros_like(cos), jnp.zeros_like(sin)


_rope.defvjp(_rope_fwd, _rope_bwd)


def _make_f_lat(r_kv):
    def f(pre_kv, pre_q, cos, sin, g_ckv, g_cq):
        c_kv = _rms(pre_kv[:, :r_kv], g_ckv)
        kr = _rope(pre_kv[:, r_kv:], cos, sin)
        return c_kv, kr, _rms(pre_q, g_cq)
    return f


def _f_rope(x, cos, sin):
    return (_rope(x, cos, sin),)


CONV_TC = 128
CONV_TT = 256
PADR = 32


def _conv_fwd(u, dw, dw_b):
    t, d = u.shape
    tc, tt = min(CONV_TC, d), min(CONV_TT, t)
    off = PADR - (CONV_WIDTH - 1)

    def body(u_ref, w_ref, b_ref, o_ref, pad_ref):
        pad_ref[pl.ds(0, PADR), :] = jnp.zeros((PADR, tc), F32)
        pad_ref[pl.ds(PADR, t), :] = u_ref[...]
        for t0 in range(0, t, tt):
            acc = jnp.zeros((tt, tc), F32) + b_ref[...]
            for j in range(CONV_WIDTH):
                acc = acc + pad_ref[pl.ds(t0 + off + j, tt), :] * w_ref[pl.ds(j, 1), :]
            o_ref[pl.ds(t0, tt), :] = acc

    return pl.pallas_call(
        body, grid=(d // tc,),
        in_specs=[pl.BlockSpec((t, tc), lambda j: (0, j)), pl.BlockSpec((CONV_WIDTH, tc), lambda j: (0, j)),
                  pl.BlockSpec((1, tc), lambda j: (0, j))],
        out_specs=pl.BlockSpec((t, tc), lambda j: (0, j)),
        out_shape=jax.ShapeDtypeStruct((t, d), F32),
        scratch_shapes=[pltpu.VMEM((t + PADR, tc), F32)],
        compiler_params=_params(("parallel",)), name="conv_fwd",
    )(u, dw, dw_b)


def _conv_bwd(u, duc, dw):
    t, d = u.shape
    tc, tt = min(CONV_TC, d), min(CONV_TT, t)
    off = PADR - (CONV_WIDTH - 1)

    def body(u_ref, g_ref, w_ref, du_ref, dwt_ref, padu_ref, padg_ref):
        padu_ref[pl.ds(0, PADR), :] = jnp.zeros((PADR, tc), F32)
        padu_ref[pl.ds(PADR, t), :] = u_ref[...]
        padg_ref[pl.ds(t, PADR), :] = jnp.zeros((PADR, tc), F32)
        padg_ref[pl.ds(0, t), :] = g_ref[...]
        for t0 in range(0, t, tt):
            acc = jnp.zeros((tt, tc), F32)
            for j in range(CONV_WIDTH):
                acc = acc + padg_ref[pl.ds(t0 + (CONV_WIDTH - 1) - j, tt), :] * w_ref[pl.ds(j, 1), :]
            du_ref[pl.ds(t0, tt), :] = acc
        for j in range(CONV_WIDTH):
            acc = jnp.zeros((tt, tc), F32)
            for t0 in range(0, t, tt):
                acc = acc + g_ref[pl.ds(t0, tt), :] * padu_ref[pl.ds(t0 + off + j, tt), :]
            dwt_ref[pl.ds(j, 1), :] = jnp.sum(acc, axis=0, keepdims=True)
        acc = jnp.zeros((tt, tc), F32)
        for t0 in range(0, t, tt):
            acc = acc + g_ref[pl.ds(t0, tt), :]
        dwt_ref[pl.ds(CONV_WIDTH, 1), :] = jnp.sum(acc, axis=0, keepdims=True)

    col = lambda r: pl.BlockSpec((r, tc), lambda j: (0, j))
    return pl.pallas_call(
        body, grid=(d // tc,),
        in_specs=[col(t), col(t), col(CONV_WIDTH)],
        out_specs=[col(t), col(CONV_WIDTH + 1)],
        out_shape=[jax.ShapeDtypeStruct((t, d), F32), jax.ShapeDtypeStruct((CONV_WIDTH + 1, d), F32)],
        scratch_shapes=[pltpu.VMEM((t + PADR, tc), F32), pltpu.VMEM((t + PADR, tc), F32)],
        compiler_params=_params(("parallel",)), name="conv_bwd",
    )(u, duc, dw)


@jax.custom_vjp
def _swap_halves(x):
    return pltpu.roll(x, LANES // 2, 1)


_swap_halves.defvjp(lambda x: (pltpu.roll(x, LANES // 2, 1), None), lambda _, g: (pltpu.roll(g, LANES // 2, 1),))


def _attn_block(qn, qr, kn, kr, v, q0):
    scale = (QK_NOPE + QK_ROPE) ** -0.5
    lane = lax.broadcasted_iota(jnp.int32, kr.shape, 1)
    kr_a = kr * (lane < QK_ROPE).astype(kr.dtype)
    kr_b = _swap_halves(kr_a)
    outs = []
    for hh, kr_h in ((0, kr_a), (1, kr_b)):
        sl = slice(hh * QK_NOPE, (hh + 1) * QK_NOPE)
        s = (_dot_nt(qn[:, sl], kn[:, sl]) + _dot_nt(qr, kr_h)) * scale
        row = lax.broadcasted_iota(jnp.int32, s.shape, 0) + q0
        col = lax.broadcasted_iota(jnp.int32, s.shape, 1)
        s = jnp.where(col <= row, s, NEG)
        e = jnp.exp(s - jnp.max(s, axis=-1, keepdims=True))
        p = e / jnp.sum(e, axis=-1, keepdims=True)
        outs.append(_dot_nn(p, v[:, sl]))
    return jnp.concatenate(outs, axis=1)


def _attn_fwd(qn, qr, kn, kr, v, tq=512):
    t, w = qn.shape
    pairs = w // (2 * QK_NOPE)
    tq = min(tq, t)
    pw = 2 * QK_NOPE

    def body(qn_ref, qr_ref, kn_ref, kr_ref, v_ref, o_ref):
        for q0 in range(0, t, tq):
            l = q0 + tq
            o_ref[pl.ds(q0, tq), :] = _attn_block(
                qn_ref[pl.ds(q0, tq), :], qr_ref[pl.ds(q0, tq), :], kn_ref[pl.ds(0, l), :], kr_ref[pl.ds(0, l), :],
                v_ref[pl.ds(0, l), :], q0).astype(o_ref.dtype)

    pair = lambda wd: pl.BlockSpec((t, wd), lambda p: (0, p))
    return pl.pallas_call(
        body, grid=(pairs,),
        in_specs=[pair(pw), pair(LANES), pair(pw), pl.BlockSpec((t, LANES), lambda p: (0, 0)), pair(pw)],
        out_specs=pair(pw), out_shape=jax.ShapeDtypeStruct((t, w), BF),
        compiler_params=_params(("parallel",)), name="attn_fwd",
    )(qn, qr, kn, kr, v)


def _attn_bwd(qn, qr, kn, kr, v, do, tq=256):
    t, w = qn.shape
    pairs = w // (2 * QK_NOPE)
    tq = min(tq, t)
    pw = 2 * QK_NOPE

    def body(qn_ref, qr_ref, kn_ref, kr_ref, v_ref, do_ref, dqn_ref, dqr_ref, dkn_ref, dv_ref, dkr_ref,
             akn_ref, av_ref, akr_ref):
        akn_ref[...] = jnp.zeros_like(akn_ref)
        av_ref[...] = jnp.zeros_like(av_ref)
        akr_ref[...] = jnp.zeros_like(akr_ref)
        for q0 in range(0, t, tq):
            l = q0 + tq
            rows, keys = pl.ds(q0, tq), pl.ds(0, l)
            _, vjp = jax.vjp(functools.partial(_attn_block, q0=q0), qn_ref[rows, :], qr_ref[rows, :],
                             kn_ref[keys, :], kr_ref[keys, :], v_ref[keys, :])
            dqn, dqr, dkn, dkr, dv = vjp(do_ref[rows, :].astype(F32))
            dqn_ref[rows, :] = dqn.astype(dqn_ref.dtype)
            dqr_ref[rows, :] = dqr.astype(dqr_ref.dtype)
            akn_ref[keys, :] += dkn.astype(F32)
            av_ref[keys, :] += dv.astype(F32)
            akr_ref[keys, :] += dkr.astype(F32)
        dkn_ref[...] = akn_ref[...].astype(dkn_ref.dtype)
        dv_ref[...] = av_ref[...].astype(dv_ref.dtype)
        first = pl.program_id(0) == 0

        @pl.when(first)
        def _():
            dkr_ref[...] = akr_ref[...]

        @pl.when(jnp.logical_not(first))
        def _():
            dkr_ref[...] += akr_ref[...]

    pair = lambda wd: pl.BlockSpec((t, wd), lambda p: (0, p))
    shared = pl.BlockSpec((t, LANES), lambda p: (0, 0))
    sds = jax.ShapeDtypeStruct
    return pl.pallas_call(
        body, grid=(pairs,),
        in_specs=[pair(pw), pair(LANES), pair(pw), shared, pair(pw), pair(pw)],
        out_specs=[pair(pw), pair(LANES), pair(pw), pair(pw), shared],
        out_shape=[sds((t, w), BF), sds((t, pairs * LANES), F32), sds((t, w), BF), sds((t, w), BF), sds((t, LANES), F32)],
        scratch_shapes=[pltpu.VMEM((t, pw), F32), pltpu.VMEM((t, pw), F32), pltpu.VMEM((t, LANES), F32)],
        compiler_params=_params(("arbitrary",)), name="attn_bwd",
    )(qn, qr, kn, kr, v, do)


def _ew(name, fn, ins, o_dtypes, max_bytes=2 ** 21):
    r, c = ins[0].shape
    tr = r
    if r * c * 4 > max_bytes:
        tr = max(16, (max_bytes // (c * 4)) // 16 * 16)
        while r % tr:
            tr -= 16
    n_in = len(ins)

    def body(*refs):
        outs = fn(*[x[...] for x in refs[:n_in]])
        for o_ref, o in zip(refs[n_in:], outs):
            o_ref[...] = o.astype(o_ref.dtype)

    spec = pl.BlockSpec((tr, c), lambda i: (i, 0))
    return pl.pallas_call(
        body, grid=(r // tr,), in_specs=[spec] * n_in, out_specs=[spec] * len(o_dtypes),
        out_shape=[jax.ShapeDtypeStruct((r, c), dt) for dt in o_dtypes],
        compiler_params=_params(("parallel",)), name=name,
    )(*ins)


def _adamw_math(w, g, m, v):
    m = ADAM_B1 * m + (1.0 - ADAM_B1) * g
    v = ADAM_B2 * v + (1.0 - ADAM_B2) * jnp.square(g)
    m_hat = m / (1.0 - ADAM_B1 ** ADAM_STEP)
    v_hat = v / (1.0 - ADAM_B2 ** ADAM_STEP)
    delta = -ADAM_LR * (m_hat / (jnp.sqrt(v_hat) + ADAM_EPS) + ADAM_WD * w)
    return delta, m, v


def _adamw(name, w, g, m, v):
    shape = w.shape
    to2 = lambda a: a.reshape(-1, shape[-1]) if a.ndim > 1 else a.reshape(1, -1)
    d, nm, nv = _ew("adamw_" + name, _adamw_math, [to2(w), to2(g), to2(m), to2(v)], (F32, F32, F32))
    return d.reshape(shape), nm.reshape(shape), nv.reshape(shape)


def _rope_tables(positions, t):
    half = QK_ROPE // 2
    inv = 1.0 / (ROPE_THETA ** (jnp.arange(0, QK_ROPE, 2, dtype=F32) / QK_ROPE))
    inv_l = jnp.tile(inv, LANES // half).reshape(1, LANES)
    sign = jnp.tile(jnp.concatenate([-jnp.ones((half,), F32), jnp.ones((half,), F32)]), LANES // QK_ROPE).reshape(1, LANES)

    def body(p_ref, inv_ref, sg_ref, c_ref, s_ref):
        ang = p_ref[...].astype(F32) * inv_ref[...]
        c_ref[...] = jnp.cos(ang)
        s_ref[...] = jnp.sin(ang) * sg_ref[...]

    return pl.pallas_call(body, out_shape=[jax.ShapeDtypeStruct((t, LANES), F32)] * 2, name="rope_tables")(
        positions.reshape(t, 1), inv_l, sign)


def _loss_grad(h, target):
    t, d = h.shape
    tr = min(TR, t)

    def body(h_ref, y_ref, dh_ref, l_ref):
        err = h_ref[...] - y_ref[...]
        dh_ref[...] = err * (1.0 / d)
        part = 0.5 * jnp.sum(jnp.mean(jnp.square(err), axis=-1, keepdims=True), axis=0, keepdims=True)
        first = pl.program_id(0) == 0

        @pl.when(first)
        def _():
            l_ref[...] = part

        @pl.when(jnp.logical_not(first))
        def _():
            l_ref[...] += part

    row = pl.BlockSpec((tr, d), lambda i: (i, 0))
    return pl.pallas_call(
        body, grid=(t // tr,), in_specs=[row, row],
        out_specs=[row, pl.BlockSpec((1, 1), lambda i: (0, 0))],
        out_shape=[jax.ShapeDtypeStruct((t, d), F32), jax.ShapeDtypeStruct((1, 1), F32)],
        compiler_params=_params(("arbitrary",)), name="loss_grad",
    )(h, target)


def _sum_devices(g):
    def fn(*parts):
        acc = parts[0]
        for p in parts[1:]:
            acc = acc + p
        return (acc,)
    return _ew("sum_devices", fn, [g[i] for i in range(g.shape[0])], (F32,))[0]


def _place():
    x, y, c = lax.axis_index("x"), lax.axis_index("y"), lax.axis_index("c")
    return x, y, c, [(1 - x, y), (x, 1 - y), (1 - x, 1 - y)]


def _all_gather_small(name, v):
    r, n = v.shape

    def body(x_ref, out_ref, send_sems, recv_sems, local_sem):
        x, y, c, chips = _place()
        me, sibling = (x, y, c), (x, y, 1 - c)

        def rows(px, py, pc):
            return out_ref.at[4 * px + 2 * py + pc]

        def copy(k, block, to, src=None):
            return pltpu.make_async_remote_copy(
                src_ref=rows(*block) if src is None else src, dst_ref=rows(*block), send_sem=send_sems.at[k],
                recv_sem=recv_sems.at[k], device_id=to, device_id_type=MESH)

        mine = pltpu.make_async_copy(x_ref, rows(*me), local_sem)
        mine.start()
        first = [copy(0, me, sibling, src=x_ref)]
        first += [copy(1 + j, me, (*chip, c), src=x_ref) for j, chip in enumerate(chips)]
        for cp in first:
            cp.start()
        passed = [copy(4 + j, (*chip, c), sibling) for j, chip in enumerate(chips)]
        for j, chip in enumerate(chips):
            copy(1 + j, (*chip, c), me).wait_recv()
            passed[j].start()
        copy(0, sibling, me).wait_recv()
        for j, chip in enumerate(chips):
            copy(4 + j, (*chip, 1 - c), me).wait_recv()
        for cp in first + passed:
            cp.wait_send()
        mine.wait()

    return pl.pallas_call(
        body, out_shape=jax.ShapeDtypeStruct((N_DEV, r, n), v.dtype),
        in_specs=[pl.BlockSpec(memory_space=pltpu.VMEM)], out_specs=pl.BlockSpec(memory_space=pltpu.VMEM),
        scratch_shapes=[pltpu.SemaphoreType.DMA((7,)), pltpu.SemaphoreType.DMA((7,)), pltpu.SemaphoreType.DMA],
        compiler_params=pltpu.CompilerParams(vmem_limit_bytes=VMEM_LIMIT), name=name,
    )(v)


def _cast_into_slot(name, w, layer, idx):
    _, r, n = w.shape
    tr = r
    while tr * n * 4 > 2 ** 21 and tr % 32 == 0:
        tr //= 2

    def body(idx_ref, w_ref, o_ref):
        o_ref[...] = w_ref[...].astype(o_ref.dtype)

    return pl.pallas_call(
        body,
        grid_spec=pltpu.PrefetchScalarGridSpec(
            num_scalar_prefetch=1, grid=(r // tr,),
            in_specs=[pl.BlockSpec((None, tr, n), lambda i, idx_ref: (layer, i, 0))],
            out_specs=pl.BlockSpec((None, tr, n), lambda i, idx_ref: (idx_ref[1], i, 0))),
        out_shape=jax.ShapeDtypeStruct((N_CHIPS, r, n), BF),
        compiler_params=_params(("parallel",)), name="cast_" + name,
    )(idx, w)


def _gather_weights(bufs):
    n_w = len(bufs)

    def body(*refs):
        outs = refs[n_w:2 * n_w]
        send_sems, recv_sems = refs[2 * n_w:]
        x, y, c, chips = _place()

        def copy(i, s, chip, which, to):
            h = outs[i].shape[1] // 2
            blk = outs[i].at[2 * chip[0] + chip[1], pl.ds(which * h, h)]
            return pltpu.make_async_remote_copy(
                src_ref=blk, dst_ref=blk, send_sem=send_sems.at[6 * i + s], recv_sem=recv_sems.at[6 * i + s],
                device_id=to, device_id_type=MESH)

        sends = []
        for i in range(n_w):
            for j, chip in enumerate(chips):
                cp = copy(i, j, (x, y), c, (*chip, c))
                cp.start()
                sends.append(cp)
        for i in range(n_w):
            for j, chip in enumerate(chips):
                copy(i, j, chip, c, (x, y, c)).wait_recv()
                cp = copy(i, 3 + j, chip, c, (x, y, 1 - c))
                cp.start()
                sends.append(cp)
        for i in range(n_w):
            for j, chip in enumerate(chips):
                copy(i, 3 + j, chip, 1 - c, (x, y, c)).wait_recv()
        for cp in sends:
            cp.wait_send()

    return pl.pallas_call(
        body, in_specs=[ANY] * n_w, out_specs=[ANY] * n_w,
        out_shape=[jax.ShapeDtypeStruct(b.shape, b.dtype) for b in bufs],
        input_output_aliases={i: i for i in range(n_w)},
        scratch_shapes=[pltpu.SemaphoreType.DMA((6 * n_w,)), pltpu.SemaphoreType.DMA((6 * n_w,))],
        name="gather_weights",
    )(*bufs)


def _swap_other_half(name, grads, after=None):
    n_w = len(grads)
    order = [] if after is None else [after]
    n_in = n_w + len(order)

    def body(*refs):
        ins, outs = refs[:n_w], refs[n_in:n_in + n_w]
        send_sems, recv_sems = refs[n_in + n_w:]
        x, y, c, _ = _place()
        cps = []
        for i in range(n_w):
            h = ins[i].shape[1] // 2
            cp = pltpu.make_async_remote_copy(
                src_ref=ins[i].at[:, pl.ds((1 - c) * h, h), :], dst_ref=outs[i], send_sem=send_sems.at[i],
                recv_sem=recv_sems.at[i], device_id=(x, y, 1 - c), device_id_type=MESH)
            cp.start()
            cps.append(cp)
        for cp in cps:
            cp.wait()

    return pl.pallas_call(
        body, in_specs=[ANY] * n_in, out_specs=[ANY] * n_w,
        out_shape=[jax.ShapeDtypeStruct((g.shape[0], g.shape[1] // 2, g.shape[2]), g.dtype) for g in grads],
        scratch_shapes=[pltpu.SemaphoreType.DMA((n_w,)), pltpu.SemaphoreType.DMA((n_w,))],
        name=name,
    )(*grads, *order)


def _add_my_half(name, g, s, idx, o_dtype):
    nc, r, n = g.shape
    h = r // 2
    tr = h
    while tr * n * 4 > 2 ** 21 and tr % 32 == 0:
        tr //= 2
    nb = h // tr

    def body(idx_ref, g_ref, s_ref, o_ref, own_ref):
        val = (g_ref[...].astype(F32) + s_ref[...].astype(F32)).astype(o_ref.dtype)
        o_ref[...] = val

        @pl.when(pl.program_id(1) == idx_ref[1])
        def _():
            own_ref[...] = val

    return pl.pallas_call(
        body,
        grid_spec=pltpu.PrefetchScalarGridSpec(
            num_scalar_prefetch=1, grid=(nb, nc),
            in_specs=[pl.BlockSpec((None, tr, n), lambda i, j, idx_ref: (j, idx_ref[0] * nb + i, 0)),
                      pl.BlockSpec((None, tr, n), lambda i, j, idx_ref: (j, i, 0))],
            out_specs=[pl.BlockSpec((None, tr, n), lambda i, j, idx_ref: (j, i, 0)),
                       pl.BlockSpec((None, tr, n), lambda i, j, idx_ref: (idx_ref[1], i, 0))]),
        out_shape=[jax.ShapeDtypeStruct((nc, h, n), o_dtype)] * 2,
        compiler_params=_params(("parallel", "arbitrary")), name=name,
    )(idx, g, s)


def _scatter_partials(parts, bufs):
    n_w = len(parts)

    def body(*refs):
        ins, outs = refs[:n_w], refs[2 * n_w:3 * n_w]
        send_sems, recv_sems = refs[3 * n_w:]
        x, y, c, chips = _place()
        k = 2 * x + y
        sends = []
        for i in range(n_w):
            for j, chip in enumerate(chips):
                cp = pltpu.make_async_remote_copy(
                    src_ref=ins[i].at[2 * chip[0] + chip[1]], dst_ref=outs[i].at[k], send_sem=send_sems.at[3 * i + j],
                    recv_sem=recv_sems.at[3 * i + j], device_id=(*chip, c), device_id_type=MESH)
                cp.start()
                sends.append(cp)
        for i in range(n_w):
            for j, chip in enumerate(chips):
                blk = outs[i].at[2 * chip[0] + chip[1]]
                pltpu.make_async_remote_copy(
                    src_ref=blk, dst_ref=blk, send_sem=send_sems.at[3 * i + j], recv_sem=recv_sems.at[3 * i + j],
                    device_id=(*chip, c), device_id_type=MESH).wait_recv()
        for cp in sends:
            cp.wait_send()

    return pl.pallas_call(
        body, in_specs=[ANY] * (2 * n_w), out_specs=[ANY] * n_w,
        out_shape=[jax.ShapeDtypeStruct(b.shape, b.dtype) for b in bufs],
        input_output_aliases={n_w + i: i for i in range(n_w)},
        scratch_shapes=[pltpu.SemaphoreType.DMA((3 * n_w,)), pltpu.SemaphoreType.DMA((3 * n_w,))],
        name="scatter_partials",
    )(*parts, *bufs)


def _sum_chips(name, q, idx):
    nc, h, n = q.shape
    tr = h
    while tr * n * 4 > 2 ** 20 and tr % 32 == 0:
        tr //= 2
    nb = h // tr

    def body(idx_ref, q_ref, o_ref):
        acc = q_ref[0].astype(F32)
        for j in range(1, nc):
            acc = acc + q_ref[j].astype(F32)
        o_ref[...] = acc

    return pl.pallas_call(
        body,
        grid_spec=pltpu.PrefetchScalarGridSpec(
            num_scalar_prefetch=1, grid=(nb,),
            in_specs=[pl.BlockSpec((nc, tr, n), lambda i, idx_ref: (0, i, 0))],
            out_specs=pl.BlockSpec((tr, n), lambda i, idx_ref: (idx_ref[0] * nb + i, 0))),
        out_shape=jax.ShapeDtypeStruct((2 * h, n), F32),
        compiler_params=_params(("parallel",)), name=name,
    )(idx, q)


def _join_halves(name, bufs):
    n_w = len(bufs)

    def body(*refs):
        outs = refs[n_w:2 * n_w]
        send_sems, recv_sems = refs[2 * n_w:]
        x, y, c, _ = _place()

        def copy(i, which):
            h = outs[i].shape[0] // 2
            rows = outs[i].at[pl.ds(which * h, h)]
            return pltpu.make_async_remote_copy(
                src_ref=rows, dst_ref=rows, send_sem=send_sems.at[i], recv_sem=recv_sems.at[i],
                device_id=(x, y, 1 - c), device_id_type=MESH)

        cps = [copy(i, c) for i in range(n_w)]
        for cp in cps:
            cp.start()
        for i, cp in enumerate(cps):
            cp.wait_send()
            copy(i, 1 - c).wait_recv()

    return pl.pallas_call(
        body, in_specs=[ANY] * n_w, out_specs=[ANY] * n_w,
        out_shape=[jax.ShapeDtypeStruct(b.shape, b.dtype) for b in bufs],
        input_output_aliases={i: i for i in range(n_w)},
        scratch_shapes=[pltpu.SemaphoreType.DMA((n_w,)), pltpu.SemaphoreType.DMA((n_w,))],
        name=name,
    )(*bufs)


HBM_SPEC = pl.BlockSpec(memory_space=pltpu.HBM)
SEM_SPEC = pl.BlockSpec(memory_space=pltpu.SEMAPHORE)
VMEM_SPEC = pl.BlockSpec(memory_space=pltpu.VMEM)
SIDE_EFFECT = pltpu.SideEffectType.DATAFLOW_SIDE_EFFECTING


def _in_hbm(arrays):
    return [pltpu.with_memory_space_constraint(a, pltpu.HBM) for a in arrays]


def _ici_gather_copy(ref, i, j, chip, send_sems, recv_sems):
    x, y, c, _ = _place()
    h = ref.shape[1] // 2
    mine = ref.at[2 * x + y, pl.ds(c * h, h)]
    return pltpu.make_async_remote_copy(
        src_ref=mine, dst_ref=mine, send_sem=send_sems.at[3 * i + j], recv_sem=recv_sems.at[3 * i + j],
        device_id=(*chip, c), device_id_type=MESH)


def _gather_start(name, bufs, after=None):
    n_w = len(bufs)
    order = [] if after is None else [after]

    def body(*refs):
        ins, token = refs[:n_w], refs[-1]
        send_sems, recv_sems = refs[n_w + len(order)], refs[n_w + len(order) + 1]
        chips = _place()[3]
        for i in range(n_w):
            for j, chip in enumerate(chips):
                _ici_gather_copy(ins[i], i, j, chip, send_sems, recv_sems).start()
        token[...] = jnp.zeros_like(token)

    res = pl.pallas_call(
        body, name=name, in_specs=[HBM_SPEC] * n_w + [ANY] * len(order),
        out_shape=(pltpu.SemaphoreType.DMA((3 * n_w,)), pltpu.SemaphoreType.DMA((3 * n_w,)),
                   *[pltpu.HBM(b.shape, b.dtype) for b in bufs], jax.ShapeDtypeStruct((8, LANES), F32)),
        out_specs=(SEM_SPEC, SEM_SPEC, *[HBM_SPEC] * n_w, VMEM_SPEC),
        input_output_aliases={i: 2 + i for i in range(n_w)},
        compiler_params=pltpu.CompilerParams(has_side_effects=SIDE_EFFECT),
    )(*_in_hbm(bufs), *order)
    return res[0], res[1], list(res[2:2 + n_w]), res[-1]


def _gather_wait(name, send_sems, recv_sems, bufs, after):
    n_w = len(bufs)

    def body(*refs):
        ins, s_sems, r_sems = refs[:n_w], refs[n_w], refs[n_w + 1]
        chips = _place()[3]
        for i in range(n_w):
            for j, chip in enumerate(chips):
                cp = _ici_gather_copy(ins[i], i, j, chip, s_sems, r_sems)
                cp.wait_send()
                cp.wait_recv()

    return pl.pallas_call(
        body, name=name, in_specs=[HBM_SPEC] * n_w + [SEM_SPEC, SEM_SPEC, ANY],
        out_shape=[pltpu.HBM(b.shape, b.dtype) for b in bufs], out_specs=[HBM_SPEC] * n_w,
        input_output_aliases={i: i for i in range(n_w)},
        compiler_params=pltpu.CompilerParams(has_side_effects=SIDE_EFFECT),
    )(*bufs, send_sems, recv_sems, after)


def _forward_to_sibling(name, bufs):
    n_w = len(bufs)

    def body(*refs):
        outs = refs[n_w:2 * n_w]
        send_sems, recv_sems = refs[2 * n_w:]
        x, y, c, chips = _place()

        def copy(i, j, chip, which):
            h = outs[i].shape[1] // 2
            blk = outs[i].at[2 * chip[0] + chip[1], pl.ds(which * h, h)]
            return pltpu.make_async_remote_copy(
                src_ref=blk, dst_ref=blk, send_sem=send_sems.at[3 * i + j], recv_sem=recv_sems.at[3 * i + j],
                device_id=(x, y, 1 - c), device_id_type=MESH)

        sends = [copy(i, j, chip, c) for i in range(n_w) for j, chip in enumerate(chips)]
        for cp in sends:
            cp.start()
        for i in range(n_w):
            for j, chip in enumerate(chips):
                copy(i, j, chip, 1 - c).wait_recv()
        for cp in sends:
            cp.wait_send()

    return pl.pallas_call(
        body, in_specs=[ANY] * n_w, out_specs=[ANY] * n_w,
        out_shape=[jax.ShapeDtypeStruct(b.shape, b.dtype) for b in bufs],
        input_output_aliases={i: i for i in range(n_w)},
        scratch_shapes=[pltpu.SemaphoreType.DMA((3 * n_w,)), pltpu.SemaphoreType.DMA((3 * n_w,))],
        name=name,
    )(*bufs)


def _ici_scatter_copy(part, buf, i, j, chip, send_sems, recv_sems):
    x, y, c, _ = _place()
    return pltpu.make_async_remote_copy(
        src_ref=part.at[2 * chip[0] + chip[1]], dst_ref=buf.at[2 * x + y], send_sem=send_sems.at[3 * i + j],
        recv_sem=recv_sems.at[3 * i + j], device_id=(*chip, c), device_id_type=MESH)


def _ici_scatter_landing(buf, i, j, chip, send_sems, recv_sems):
    x, y, c, _ = _place()
    blk = buf.at[2 * chip[0] + chip[1]]
    return pltpu.make_async_remote_copy(
        src_ref=blk, dst_ref=blk, send_sem=send_sems.at[3 * i + j], recv_sem=recv_sems.at[3 * i + j],
        device_id=(*chip, c), device_id_type=MESH)


def _scatter_start(name, parts, bufs):
    n_w = len(parts)

    def body(*refs):
        ps, bs = refs[:n_w], refs[n_w:2 * n_w]
        send_sems, recv_sems, token = refs[2 * n_w], refs[2 * n_w + 1], refs[-1]
        chips = _place()[3]
        for i in range(n_w):
            for j, chip in enumerate(chips):
                _ici_scatter_copy(ps[i], bs[i], i, j, chip, send_sems, recv_sems).start()
        token[...] = jnp.zeros_like(token)

    both = list(parts) + list(bufs)
    res = pl.pallas_call(
        body, name=name, in_specs=[HBM_SPEC] * (2 * n_w),
        out_shape=(pltpu.SemaphoreType.DMA((3 * n_w,)), pltpu.SemaphoreType.DMA((3 * n_w,)),
                   *[pltpu.HBM(b.shape, b.dtype) for b in both], jax.ShapeDtypeStruct((8, LANES), F32)),
        out_specs=(SEM_SPEC, SEM_SPEC, *[HBM_SPEC] * (2 * n_w), VMEM_SPEC),
        input_output_aliases={i: 2 + i for i in range(2 * n_w)},
        compiler_params=pltpu.CompilerParams(has_side_effects=SIDE_EFFECT),
    )(*_in_hbm(both))
    return res[0], res[1], list(res[2:2 + n_w]), list(res[2 + n_w:2 + 2 * n_w]), res[-1]


def _scatter_wait(name, send_sems, recv_sems, parts, bufs, after):
    n_w = len(parts)
    after = list(after) if isinstance(after, (list, tuple)) else [after]

    def body(*refs):
        ps, bs = refs[:n_w], refs[n_w:2 * n_w]
        s_sems, r_sems = refs[2 * n_w], refs[2 * n_w + 1]
        chips = _place()[3]
        for i in range(n_w):
            for j, chip in enumerate(chips):
                _ici_scatter_copy(ps[i], bs[i], i, j, chip, s_sems, r_sems).wait_send()
                _ici_scatter_landing(bs[i], i, j, chip, s_sems, r_sems).wait_recv()

    both = list(parts) + list(bufs)
    res = pl.pallas_call(
        body, name=name, in_specs=[HBM_SPEC] * (2 * n_w) + [SEM_SPEC, SEM_SPEC] + [ANY] * len(after),
        out_shape=[pltpu.HBM(b.shape, b.dtype) for b in both], out_specs=[HBM_SPEC] * (2 * n_w),
        input_output_aliases={i: i for i in range(2 * n_w)},
        compiler_params=pltpu.CompilerParams(has_side_effects=SIDE_EFFECT),
    )(*both, send_sems, recv_sems, *after)
    return list(res[n_w:])


def _reduce_scatter_begin(tag, grads, idx, after=None):
    from_sibling = _swap_other_half("swap_other_half_" + tag, grads, after)
    pairs = [_add_my_half("add_my_half_%s_%d" % (tag, i), g, s, idx, BF)
             for i, (g, s) in enumerate(zip(grads, from_sibling))]
    s_sems, r_sems, parts, bufs, token = _scatter_start("scatter_start_" + tag, [p[0] for p in pairs],
                                                        [p[1] for p in pairs])
    return (tag, s_sems, r_sems, parts, bufs, idx), token


def _reduce_scatter_end(state, after):
    tag, s_sems, r_sems, parts, bufs, idx = state
    received = _scatter_wait("scatter_wait_" + tag, s_sems, r_sems, parts, bufs, after)
    halves = [_sum_chips("sum_chips_%s_%d" % (tag, i), q, idx) for i, q in enumerate(received)]
    return _join_halves("join_halves_" + tag, halves)


def _relu2_epilogue(acc):
    r = jnp.maximum(acc, 0.0)
    return (r * r,)


def _relu2_bwd_epilogue(acc, s):
    return (acc * (2.0 * jnp.sqrt(s.astype(F32))),)


WIDE = (2048, 512, 2048)
DEEP = (1024, 2048, 512)


def _add_epilogue(acc, other):
    return (acc + other,)


def kernel(x, c, positions, w_ada_mix, b_ada_mix, w_ada_mlp, b_ada_mlp, g_pre_mix, g_post_mix, g_pre_mlp, g_post_mlp, conv_w_in, conv_b_in, conv_dw, conv_dw_b, conv_ln_g, conv_ln_b, conv_w_out, conv_b_out, w_ada_kv, b_ada_kv, g_kv, w_dkv, g_ckv, w_kr, w_uk, w_uv, w_dq, g_cq, w_uq, w_o, mlp_w_up, mlp_w_down, loss_target, m_w_ada_mix, m_b_ada_mix, m_w_ada_mlp, m_b_ada_mlp, m_g_pre_mix, m_g_post_mix, m_g_pre_mlp, m_g_post_mlp, m_conv_w_in, m_conv_b_in, m_conv_dw, m_conv_dw_b, m_conv_ln_g, m_conv_ln_b, m_conv_w_out, m_conv_b_out, m_w_ada_kv, m_b_ada_kv, m_g_kv, m_w_dkv, m_g_ckv, m_w_kr, m_w_uk, m_w_uv, m_w_dq, m_g_cq, m_w_uq, m_w_o, m_mlp_w_up, m_mlp_w_down, v_w_ada_mix, v_b_ada_mix, v_w_ada_mlp, v_b_ada_mlp, v_g_pre_mix, v_g_post_mix, v_g_pre_mlp, v_g_post_mlp, v_conv_w_in, v_conv_b_in, v_conv_dw, v_conv_dw_b, v_conv_ln_g, v_conv_ln_b, v_conv_w_out, v_conv_b_out, v_w_ada_kv, v_b_ada_kv, v_g_kv, v_w_dkv, v_g_ckv, v_w_kr, v_w_uk, v_w_uv, v_w_dq, v_g_cq, v_w_uq, v_w_o, v_mlp_w_up, v_mlp_w_down):
    weights = dict(w_ada_mix=w_ada_mix, b_ada_mix=b_ada_mix, w_ada_mlp=w_ada_mlp, b_ada_mlp=b_ada_mlp, g_pre_mix=g_pre_mix, g_post_mix=g_post_mix, g_pre_mlp=g_pre_mlp, g_post_mlp=g_post_mlp, conv_w_in=conv_w_in, conv_b_in=conv_b_in, conv_dw=conv_dw, conv_dw_b=conv_dw_b, conv_ln_g=conv_ln_g, conv_ln_b=conv_ln_b, conv_w_out=conv_w_out, conv_b_out=conv_b_out, w_ada_kv=w_ada_kv, b_ada_kv=b_ada_kv, g_kv=g_kv, w_dkv=w_dkv, g_ckv=g_ckv, w_kr=w_kr, w_uk=w_uk, w_uv=w_uv, w_dq=w_dq, g_cq=g_cq, w_uq=w_uq, w_o=w_o, mlp_w_up=mlp_w_up, mlp_w_down=mlp_w_down)
    m_in = dict(w_ada_mix=m_w_ada_mix, b_ada_mix=m_b_ada_mix, w_ada_mlp=m_w_ada_mlp, b_ada_mlp=m_b_ada_mlp, g_pre_mix=m_g_pre_mix, g_post_mix=m_g_post_mix, g_pre_mlp=m_g_pre_mlp, g_post_mlp=m_g_post_mlp, conv_w_in=m_conv_w_in, conv_b_in=m_conv_b_in, conv_dw=m_conv_dw, conv_dw_b=m_conv_dw_b, conv_ln_g=m_conv_ln_g, conv_ln_b=m_conv_ln_b, conv_w_out=m_conv_w_out, conv_b_out=m_conv_b_out, w_ada_kv=m_w_ada_kv, b_ada_kv=m_b_ada_kv, g_kv=m_g_kv, w_dkv=m_w_dkv, g_ckv=m_g_ckv, w_kr=m_w_kr, w_uk=m_w_uk, w_uv=m_w_uv, w_dq=m_w_dq, g_cq=m_g_cq, w_uq=m_w_uq, w_o=m_w_o, mlp_w_up=m_mlp_w_up, mlp_w_down=m_mlp_w_down)
    v_in = dict(w_ada_mix=v_w_ada_mix, b_ada_mix=v_b_ada_mix, w_ada_mlp=v_w_ada_mlp, b_ada_mlp=v_b_ada_mlp, g_pre_mix=v_g_pre_mix, g_post_mix=v_g_post_mix, g_pre_mlp=v_g_pre_mlp, g_post_mlp=v_g_post_mlp, conv_w_in=v_conv_w_in, conv_b_in=v_conv_b_in, conv_dw=v_conv_dw, conv_dw_b=v_conv_dw_b, conv_ln_g=v_conv_ln_g, conv_ln_b=v_conv_ln_b, conv_w_out=v_conv_w_out, conv_b_out=v_conv_b_out, w_ada_kv=v_w_ada_kv, b_ada_kv=v_b_ada_kv, g_kv=v_g_kv, w_dkv=v_w_dkv, g_ckv=v_g_ckv, w_kr=v_w_kr, w_uk=v_w_uk, w_uv=v_w_uv, w_dq=v_w_dq, g_cq=v_g_cq, w_uq=v_w_uq, w_o=v_w_o, mlp_w_up=v_mlp_w_up, mlp_w_down=v_mlp_w_down)
    grads, last_group = _step_grads(x, c, positions, loss_target, weights)
    loss, grad_x = grads.pop("loss"), grads.pop("x")
    names = list(weights)
    upd = {n: _adamw(n, weights[n], grads[n], m_in[n], v_in[n]) for n in names if n in grads}
    grads.update(last_group([upd[n][0] for n in ("w_ada_mix", "w_ada_mlp", "w_ada_kv", "mlp_w_down")]))
    upd.update({n: _adamw(n, weights[n], grads[n], m_in[n], v_in[n]) for n in names if n not in upd})
    return (loss, grad_x, *[grads[n] for n in names], *[upd[n][0] for n in names], *[upd[n][1] for n in names],
            *[upd[n][2] for n in names])


def _step_grads(x, c, positions, loss_target, w):
    xi, yi, ci = lax.axis_index("x"), lax.axis_index("y"), lax.axis_index("c")
    chip = 2 * xi + yi
    dev = 2 * chip + ci
    place_idx = jnp.stack([ci, chip]).astype(jnp.int32)
    t, d = x.shape[1], x.shape[2]
    dl = d // N_CHIPS
    r_kv, r_q = w["w_dkv"].shape[1], w["w_dq"].shape[2]
    n_ada, n_kvada = w["w_ada_mix"].shape[2], w["w_ada_kv"].shape[1]
    heads_l = w["w_uq"].shape[2] // (QK_NOPE + QK_ROPE)
    assert dl == r_kv == r_q and dl % LANES == 0 and heads_l * N_CHIPS % 2 == 0

    def chip_cols(a, width):
        return lax.dynamic_slice_in_dim(a, chip * width, width, axis=a.ndim - 1)

    uq = w["w_uq"][0].reshape(r_q, heads_l, QK_NOPE + QK_ROPE)
    uq_nope, uq_rope = uq[:, :, :QK_NOPE].reshape(r_q, -1), uq[:, :, QK_NOPE:].reshape(r_q, -1)
    kr_pad = jnp.pad(w["w_kr"], ((0, 0), (0, LANES - QK_ROPE)))
    small = jnp.concatenate([w["w_dkv"], kr_pad, w["w_dq"][0], w["w_uk"], w["w_uv"], uq_nope, uq_rope], axis=1)
    widths = [r_kv, LANES, r_q, w["w_uk"].shape[1], w["w_uv"].shape[1], uq_nope.shape[1], uq_rope.shape[1]]
    so = [0]
    for wd in widths:
        so.append(so[-1] + wd)
    big = dict(conv_w_in=(w["conv_w_in"], 0), conv_w_out=(w["conv_w_out"], 0), w_o=(w["w_o"], 0),
               up0=(w["mlp_w_up"], 0), up1=(w["mlp_w_up"], 1), down0=(w["mlp_w_down"], 0),
               down1=(w["mlp_w_down"], 1), small=(small[None], 0))
    groups = dict(g0=["conv_w_in", "conv_w_out"], g1=["up0", "down0"], g2=["small", "w_o"], g3=["up1", "down1"])
    gathered, pending = {}, {}

    def start_gathers(after):
        started, token = jnp.zeros((1, 1), F32), after
        for gname in ("g0", "g1", "g2", "g3"):
            slots = [_cast_into_slot(n, *big[n], place_idx) for n in groups[gname]]
            s_sems, r_sems, bufs, token = _gather_start("gather_start_" + gname, slots, after=token)
            pending[gname] = (s_sems, r_sems, bufs)
            started = started + token[0:1, 0:1]
        return started

    def finish_gather(gname, after):
        s_sems, r_sems, bufs = pending[gname]
        bufs = _gather_wait("gather_wait_" + gname, s_sems, r_sems, bufs, after)
        gathered.update(zip(groups[gname], _forward_to_sibling("forward_" + gname, bufs)))

    rowl = lambda a: a.reshape(-1, a.shape[2])
    w_up_f = lambda l: gathered["up%d" % l]
    w_down_f = lambda l: rowl(gathered["down%d" % l])

    pack_rows = [w["conv_dw"][0], w["conv_dw_b"], w["conv_ln_g"], w["conv_ln_b"], w["conv_b_out"],
                 w["conv_b_in"].reshape(2, dl), c.reshape(N_CHIPS, dl)]
    pack_rows = [jnp.pad(p, ((0, (-p.shape[0]) % 8), (0, 0))) for p in pack_rows]
    po = [0]
    for p in pack_rows:
        po.append(po[-1] + p.shape[0])
    packs = _all_gather_small("gather_params", jnp.concatenate(pack_rows, axis=0))
    by_chip = packs[0::2]

    def full_width(r0, nr):
        return jnp.transpose(by_chip[:, r0:r0 + nr, :], (1, 0, 2)).reshape(nr, d)

    dw_full, dwb_full = full_width(po[0], CONV_WIDTH), full_width(po[1], 1)
    lng_full, lnb_full, bout_full = full_width(po[2], 1), full_width(po[3], 1), full_width(po[4], 1)
    bin_full = by_chip[:, po[5]:po[5] + 2, :].reshape(1, 2 * d)
    c_all = packs[:, po[6]:po[6] + N_CHIPS, :].reshape(N_DEV, d)

    sc_all = _ew("silu_c", lambda a: (a * jax.nn.sigmoid(a),), [c_all], (F32,))[0]
    ada_w = [(w["w_ada_mix"], 0), (w["w_ada_mix"], 1), (w["w_ada_mlp"], 0), (w["w_ada_mlp"], 1),
             (w["w_ada_kv"][None], 0)]
    ada_b = [chip_cols(w["b_ada_mix"][0:1], n_ada), chip_cols(w["b_ada_mix"][1:2], n_ada),
             chip_cols(w["b_ada_mlp"][0:1], n_ada), chip_cols(w["b_ada_mlp"][1:2], n_ada),
             chip_cols(w["b_ada_kv"].reshape(1, -1), n_kvada)]
    mods = [_mm_nn("ada_fwd_%d" % i, sc_all, wi, "row", extras=(jnp.broadcast_to(bi, (N_DEV, bi.shape[1])),),
                   epilogue=_add_epilogue, w_layer=li) for i, ((wi, li), bi) in enumerate(zip(ada_w, ada_b))]
    mods_all = _all_gather_small("gather_mods", jnp.concatenate(mods, axis=1))
    started = start_gathers(mods_all)
    mine = lax.dynamic_index_in_dim(mods_all[0::2], dev, axis=1, keepdims=False)
    offs = [0]
    for m_ in mods:
        offs.append(offs[-1] + m_.shape[1])
    mod_vec = [mine[:, offs[i]:offs[i + 1]].reshape(1, -1) for i in range(5)]
    split3 = lambda v: (v[:, :d], v[:, d:2 * d], v[:, 2 * d:])
    mix = [split3(mod_vec[0]), split3(mod_vec[1])]
    mlp = [split3(mod_vec[2]), split3(mod_vec[3])]
    kv_shift, kv_scale = mod_vec[4][:, :d], mod_vec[4][:, d:]

    cos_t, sin_t = _rope_tables(positions, t)
    vec = lambda a: a.reshape(1, -1)
    gpm, gqm = w["g_pre_mix"], w["g_post_mix"]
    gpl, gql = w["g_pre_mlp"], w["g_post_mlp"]
    h0 = x[0]
    after_token = lambda v, token: v + token[0:1, 0:1]

    def mlp_fwd(l, h):
        sh, sc, _ = mlp[l]
        (hn,) = _rw_fwd("mlp_pre_%d" % l, _f_pre, [h], [vec(gpl[l]), sh, sc], (BF,))
        s = _mm_nn("mlp_up_%d" % l, hn, w_up_f(l), "col", o_dtypes=(BF,), epilogue=_relu2_epilogue, tiles=WIDE)
        y = _mm_nn("mlp_down_%d" % l, s, w_down_f(l), "row", tiles=DEEP)
        return hn, s, y

    def post_fwd(name, h, y, gate, g, bias=None):
        if bias is None:
            return _rw_fwd(name, lambda h_, y_, gt, g_: (h_ + _f_post(y_, gt, g_)[0],), [h, y], [gate, g], (F32,))[0]
        return _rw_fwd(name, lambda h_, y_, b_, gt, g_: (h_ + _f_post_bias(y_, b_, gt, g_)[0],), [h, y],
                       [bias, gate, g], (F32,))[0]

    (hn0,) = _rw_fwd("conv_pre", _f_pre, [h0], [vec(gpm[0]), mix[0][0] + started, mix[0][1]], (BF,))
    finish_gather("g0", hn0)
    w_in_f, w_out_f = gathered["conv_w_in"], rowl(gathered["conv_w_out"])
    z0 = _mm_nn("conv_in", hn0, w_in_f, "col")
    (u0,) = _rw_fwd("conv_glu", _f_glu, [z0], [bin_full], (F32,))
    uc0 = _conv_fwd(u0, dw_full, dwb_full)
    (n0,) = _rw_fwd("conv_ln", _f_ln_silu, [uc0], [lng_full, lnb_full], (BF,))
    y0 = _mm_nn("conv_out", n0, w_out_f, "row")
    h1 = post_fwd("conv_post", h0, y0, mix[0][2], vec(gqm[0]), bias=bout_full)
    finish_gather("g1", h1)
    hn1, s1, y1 = mlp_fwd(0, h1)
    h2 = post_fwd("mlp_post_0", h1, y1, mlp[0][2], vec(gql[0]))

    finish_gather("g2", h2)
    gs = gathered["small"]
    w_dkvkr = rowl(gs[:, :, so[0]:so[2]])
    w_dq_f = rowl(gs[:, :, so[2]:so[3]])
    w_uk_f, w_uv_f = gs[:, :, so[3]:so[4]], gs[:, :, so[4]:so[5]]
    w_qn_f, w_qr_f = gs[:, :, so[5]:so[6]], gs[:, :, so[6]:so[7]]
    w_o_f = rowl(gathered["w_o"])
    kvn, hn2 = _rw_fwd("mla_pre", _f_pre2, [h2], [vec(w["g_kv"]), kv_shift, kv_scale, vec(gpm[1]), mix[1][0], mix[1][1]],
                       (BF, BF))
    pre_kv = _mm_nn("mla_dkv", kvn, w_dkvkr, "row")
    pre_q = _mm_nn("mla_dq", hn2, w_dq_f, "row")
    f_lat = _make_f_lat(r_kv)
    c_kv, kr, c_q = _rw_fwd("mla_latent", f_lat, [pre_kv, pre_q, cos_t, sin_t], [vec(w["g_ckv"]), vec(w["g_cq"][0])],
                            (BF, BF, BF))
    kn = _mm_nn("mla_uk", c_kv, w_uk_f, "col", o_dtypes=(BF,))
    vv = _mm_nn("mla_uv", c_kv, w_uv_f, "col", o_dtypes=(BF,))
    qn = _mm_nn("mla_uq_nope", c_q, w_qn_f, "col", o_dtypes=(BF,))
    qr_pre = _mm_nn("mla_uq_rope", c_q, w_qr_f, "col")
    (qr,) = _rw_fwd("mla_q_rope", _f_rope, [qr_pre, cos_t, sin_t], [], (BF,))
    att = _attn_fwd(qn, qr, kn, kr, vv)
    y2 = _mm_nn("mla_o", att, w_o_f, "row")
    h3 = post_fwd("mla_post", h2, y2, mix[1][2], vec(gqm[1]))
    finish_gather("g3", h3)
    hn3, s3, y3 = mlp_fwd(1, h3)
    h4 = post_fwd("mlp_post_1", h3, y3, mlp[1][2], vec(gql[1]))

    dh4, loss_part = _loss_grad(h4, loss_target[0])
    loss = lax.psum(loss_part[0, 0], ("x", "y", "c"))

    gw = {}
    gvec = {}

    dw_mm = functools.partial(_mm_tn, o_dtype=BF)

    def mlp_bwd(l, h_in, hn, s, y, dh, token=None):
        sh, sc, gate = mlp[l]
        if token is not None:
            gate = after_token(gate, token)
        (dy,), (dgate, dgq) = _rw_bwd("mlp_post_bwd_%d" % l, _f_post, [y], [gate, vec(gql[l])], [dh], [BF])
        gw["down%d" % l] = dw_mm("mlp_down_dw_%d" % l, s, dy, "row", tiles=(512, 2048, 2048))
        da = _mm_nt("mlp_down_dx_%d" % l, dy, w_down_f(l), "row", o_dtypes=(BF,), extras=(s,),
                    epilogue=_relu2_bwd_epilogue, tiles=WIDE)
        gw["up%d" % l] = dw_mm("mlp_up_dw_%d" % l, hn, da, "col", tiles=WIDE)
        dhn = _mm_nt("mlp_up_dx_%d" % l, da, w_up_f(l), "col", tiles=DEEP)
        (dh_in,), (dgp, dsh, dsc) = _rw_bwd("mlp_pre_bwd_%d" % l, _f_pre, [h_in], [vec(gpl[l]), sh, sc], [dhn], [F32],
                                            add_rows={0: dh})
        gvec["mlp%d" % l] = (dsh, dsc, dgate)
        gvec["g_pre_mlp%d" % l], gvec["g_post_mlp%d" % l] = dgp, dgq
        return dh_in

    chunked = lambda a: a.reshape(N_CHIPS, a.shape[0] // N_CHIPS, a.shape[1])
    to_chunks = lambda a: a if a.ndim == 3 else chunked(a)
    red = {}

    def reduce_begin(tag, names, after=None):
        state, token = _reduce_scatter_begin(tag, [to_chunks(gw[n]) for n in names], place_idx, after)
        return (names, state), token

    def reduce_end(handle, after):
        names, state = handle
        red.update(zip(names, _reduce_scatter_end(state, after)))

    dh3 = mlp_bwd(1, h3, hn3, s3, y3, dh4)
    rs1, token = reduce_begin("r1", ["up1", "down1"])

    (dy2,), (dgate, dgq) = _rw_bwd("mla_post_bwd", _f_post, [y2], [after_token(mix[1][2], token), vec(gqm[1])],
                                   [dh3], [BF])
    gvec["g_post_mix1"] = dgq
    gw["w_o"] = dw_mm("mla_o_dw", att, dy2, "row")
    datt = _mm_nt("mla_o_dx", dy2, w_o_f, "row", o_dtypes=(BF,))
    dqn, dqr, dkn, dvv, dkr = _attn_bwd(qn, qr, kn, kr, vv, datt)
    (dqr_pre,), _ = _rw_bwd("mla_q_rope_bwd", _f_rope, [qr_pre, cos_t, sin_t], [], [dqr], [BF, None, None])
    g_qn = dw_mm("mla_uq_nope_dw", c_q, dqn, "col")
    g_qr = dw_mm("mla_uq_rope_dw", c_q, dqr_pre, "col")
    dc_q = _mm_nt("mla_uq_nope_dx", dqn, w_qn_f, "col")
    dc_q = _mm_nt("mla_uq_rope_dx", dqr_pre, w_qr_f, "col", extras=(dc_q,), epilogue=_add_epilogue)
    g_uk = dw_mm("mla_uk_dw", c_kv, dkn, "col")
    g_uv = dw_mm("mla_uv_dw", c_kv, dvv, "col")
    dc_kv = _mm_nt("mla_uk_dx", dkn, w_uk_f, "col")
    dc_kv = _mm_nt("mla_uv_dx", dvv, w_uv_f, "col", extras=(dc_kv,), epilogue=_add_epilogue)
    (dpre_kv, dpre_q), (dg_ckv, dg_cq) = _rw_bwd(
        "mla_latent_bwd", f_lat, [pre_kv, pre_q, cos_t, sin_t], [vec(w["g_ckv"]), vec(w["g_cq"][0])],
        [dc_kv, dkr, dc_q], [BF, BF, None, None])
    gvec["g_ckv"], gvec["g_cq"] = dg_ckv, dg_cq
    g_dkvkr = dw_mm("mla_dkv_dw", kvn, dpre_kv, "row")
    g_dq = dw_mm("mla_dq_dw", hn2, dpre_q, "row")
    dkvn = _mm_nt("mla_dkv_dx", dpre_kv, w_dkvkr, "row")
    dhn2 = _mm_nt("mla_dq_dx", dpre_q, w_dq_f, "row")
    (dh2,), (dg_kv, dkvsh, dkvsc, dgp, dsh, dsc) = _rw_bwd(
        "mla_pre_bwd", _f_pre2, [h2], [vec(w["g_kv"]), kv_shift, kv_scale, vec(gpm[1]), mix[1][0], mix[1][1]],
        [dkvn, dhn2], [F32], add_rows={0: dh3})
    gvec["mix1"] = (dsh, dsc, dgate)
    gvec["kv"] = (dkvsh, dkvsc)
    gvec["g_kv"], gvec["g_pre_mix1"] = dg_kv, dgp
    gw["small"] = jnp.concatenate([chunked(g_dkvkr), chunked(g_dq), g_uk, g_uv, g_qn, g_qr], axis=2)
    reduce_end(rs1, dh2)
    rs2, token = reduce_begin("r2", ["small", "w_o"])

    dh1 = mlp_bwd(0, h1, hn1, s1, y1, dh2, token=token)
    reduce_end(rs2, dh1)
    rs3, token = reduce_begin("r3", ["up0", "down0"])

    (dy0,), (dbout, dgate, dgq) = _rw_bwd("conv_post_bwd", _f_post_bias, [y0],
                                          [bout_full, after_token(mix[0][2], token), vec(gqm[0])], [dh1], [BF])
    gvec["g_post_mix0"] = dgq
    gw["conv_w_out"] = dw_mm("conv_out_dw", n0, dy0, "row")
    dn0 = _mm_nt("conv_out_dx", dy0, w_out_f, "row")
    (duc0,), (dlng, dlnb) = _rw_bwd("conv_ln_bwd", _f_ln_silu, [uc0], [lng_full, lnb_full], [dn0], [F32])
    du0, ddw = _conv_bwd(u0, duc0, dw_full)
    (dz0,), (dbin,) = _rw_bwd("conv_glu_bwd", _f_glu, [z0], [bin_full], [du0], [BF])
    gw["conv_w_in"] = dw_mm("conv_in_dw", hn0, dz0, "col")
    dhn0 = _mm_nt("conv_in_dx", dz0, w_in_f, "col")
    (dx,), (dgp, dsh, dsc) = _rw_bwd("conv_pre_bwd", _f_pre, [h0], [vec(gpm[0]), mix[0][0], mix[0][1]], [dhn0], [F32],
                                     add_rows={0: dh1})
    gvec["mix0"] = (dsh, dsc, dgate)
    gvec["g_pre_mix0"] = dgp
    reduce_end(rs3, dx)

    vec_list = [*gvec["mix0"], *gvec["mix1"], *gvec["mlp0"], *gvec["mlp1"], *gvec["kv"],
                gvec["g_pre_mix0"], gvec["g_pre_mix1"], gvec["g_post_mix0"], gvec["g_post_mix1"],
                gvec["g_pre_mlp0"], gvec["g_pre_mlp1"], gvec["g_post_mlp0"], gvec["g_post_mlp1"],
                gvec["g_kv"], gvec["g_ckv"], gvec["g_cq"], dbin, dlng, dlnb, dbout, ddw.reshape(1, -1)]
    vo = [0]
    for v_ in vec_list:
        vo.append(vo[-1] + v_.shape[1])
    vec_pad = (-vo[-1]) % (8 * LANES)
    n_vec = vo[-1] + vec_pad
    flat = jnp.concatenate(vec_list + [jnp.zeros((1, vec_pad), F32)], axis=1).reshape(8, n_vec // 8)
    all_vecs = _all_gather_small("gather_vector_grads", flat).reshape(N_DEV, 8, n_vec // 8)
    rs4, token = reduce_begin("r4", ["conv_w_in", "conv_w_out"], after=all_vecs)
    all_vecs = after_token(all_vecs, token)
    summed = _sum_devices(all_vecs).reshape(1, n_vec)
    per_dev = all_vecs.reshape(N_DEV, n_vec)
    seg = lambda a, i: a[:, vo[i]:vo[i + 1]]

    out = {"loss": loss, "x": dx.reshape(x.shape)}
    dm_mix = [jnp.concatenate([seg(per_dev, 3 * l + i) for i in range(3)], axis=1) for l in range(2)]
    dm_mlp = [jnp.concatenate([seg(per_dev, 6 + 3 * l + i) for i in range(3)], axis=1) for l in range(2)]
    dm_kv = jnp.concatenate([seg(per_dev, 12), seg(per_dev, 13)], axis=1)
    ada_dw = lambda name, dm, width: _mm_tn(name, sc_all, chip_cols(dm, width), "row")
    out["w_ada_mix"] = jnp.stack([ada_dw("ada_mix_dw_%d" % l, dm_mix[l], n_ada) for l in range(2)])
    out["w_ada_mlp"] = jnp.stack([ada_dw("ada_mlp_dw_%d" % l, dm_mlp[l], n_ada) for l in range(2)])
    out["w_ada_kv"] = ada_dw("ada_kv_dw", dm_kv, n_kvada)
    sum_seg = lambda i: seg(summed, i)
    out["b_ada_mix"] = jnp.concatenate([jnp.concatenate([sum_seg(3 * l + i) for i in range(3)], axis=1) for l in range(2)], axis=0)
    out["b_ada_mlp"] = jnp.concatenate([jnp.concatenate([sum_seg(6 + 3 * l + i) for i in range(3)], axis=1) for l in range(2)], axis=0)
    out["b_ada_kv"] = jnp.concatenate([sum_seg(12), sum_seg(13)], axis=1).reshape(-1)
    out["g_pre_mix"] = jnp.concatenate([sum_seg(14), sum_seg(15)], axis=0)
    out["g_post_mix"] = jnp.concatenate([sum_seg(16), sum_seg(17)], axis=0)
    out["g_pre_mlp"] = jnp.concatenate([sum_seg(18), sum_seg(19)], axis=0)
    out["g_post_mlp"] = jnp.concatenate([sum_seg(20), sum_seg(21)], axis=0)
    out["g_kv"] = sum_seg(22).reshape(-1)
    out["g_ckv"] = sum_seg(23).reshape(-1)
    out["g_cq"] = sum_seg(24)
    out["conv_b_in"] = chip_cols(sum_seg(25), 2 * dl)
    out["conv_ln_g"] = chip_cols(sum_seg(26), dl)
    out["conv_ln_b"] = chip_cols(sum_seg(27), dl)
    out["conv_b_out"] = chip_cols(sum_seg(28), dl)
    ddw_sum = chip_cols(sum_seg(29).reshape(CONV_WIDTH + 1, d), dl)
    out["conv_dw"] = ddw_sum[:CONV_WIDTH].reshape(1, CONV_WIDTH, dl)
    out["conv_dw_b"] = ddw_sum[CONV_WIDTH:]

    def last_group(after):
        reduce_end(rs4, after)
        return {"conv_w_in": red["conv_w_in"][None], "conv_w_out": red["conv_w_out"][None]}

    out["w_o"] = red["w_o"][None]
    out["mlp_w_up"] = jnp.stack([red["up0"], red["up1"]])
    out["mlp_w_down"] = jnp.stack([red["down0"], red["down1"]])
    rs = red["small"]
    piece = lambda i: rs[:, so[i]:so[i + 1]]
    out["w_dkv"] = piece(0)
    out["w_kr"] = piece(1)[:, :QK_ROPE]
    out["w_dq"] = piece(2)[None]
    out["w_uk"], out["w_uv"] = piece(3), piece(4)
    out["w_uq"] = jnp.concatenate([piece(5).reshape(r_q, heads_l, QK_NOPE), piece(6).reshape(r_q, heads_l, QK_ROPE)],
                                  axis=2).reshape(1, r_q, -1)
    return out, last_group
```

```python
import functools

import jax
import jax.numpy as jnp
from jax import lax
from jax.experimental import pallas as pl
from jax.experimental.pallas import tpu as pltpu

F32 = jnp.float32
BF = jnp.bfloat16
MXU_DTYPE = BF

EPS = 1e-6
NEG = -1e30
ROPE_THETA = 10000.0
QK_NOPE = 128
QK_ROPE = 64
V_HEAD = 128
CONV_WIDTH = 31
ADAM_LR, ADAM_B1, ADAM_B2, ADAM_EPS, ADAM_WD, ADAM_STEP = 0.001, 0.9, 0.999, 1e-08, 0.01, 10

N_CHIPS = 4
N_DEV = 8
LANES = 128
VMEM_LIMIT = 56 * 2 ** 20
MESH = pl.DeviceIdType.MESH
ANY = pl.BlockSpec(memory_space=pl.ANY)


def _params(sem=None):
    return pltpu.CompilerParams(dimension_semantics=sem, vmem_limit_bytes=VMEM_LIMIT)


def _tile(n, pref, unit=LANES):
    if n <= pref:
        return n
    t = (pref // unit) * unit
    while t > unit and n % t:
        t -= unit
    assert n % t == 0, (n, pref)
    return t


def _dg(a, b, ca, cb):
    return lax.dot_general(a.astype(MXU_DTYPE), b.astype(MXU_DTYPE), (((ca,), (cb,)), ((), ())),
                           preferred_element_type=F32)


@jax.custom_vjp
def _dot_nn(a, b):
    return _dg(a, b, 1, 0)


def _dot_nn_fwd(a, b):
    return _dg(a, b, 1, 0), (a, b)


def _dot_nn_bwd(res, g):
    a, b = res
    return _dg(g, b, 1, 1).astype(a.dtype), _dg(a, g, 0, 0).astype(b.dtype)


_dot_nn.defvjp(_dot_nn_fwd, _dot_nn_bwd)


@jax.custom_vjp
def _dot_nt(a, b):
    return _dg(a, b, 1, 1)


def _dot_nt_fwd(a, b):
    return _dg(a, b, 1, 1), (a, b)


def _dot_nt_bwd(res, g):
    a, b = res
    return _dg(g, b, 1, 0).astype(a.dtype), _dg(g, a, 0, 0).astype(b.dtype)


_dot_nt.defvjp(_dot_nt_fwd, _dot_nt_bwd)


def _matmul(name, a, b, *, ca, cb, grid, a_blk, a_map, b_blk, b_map, o_shape, o_dtypes, o_blk, o_map,
            extras=(), epilogue=None):
    nk = grid[-1]
    n_ex, n_out = len(extras), len(o_dtypes)

    def body(*refs):
        a_ref, b_ref = refs[0], refs[1]
        ex_refs = refs[2:2 + n_ex]
        out_refs = refs[2 + n_ex:2 + n_ex + n_out]
        kk = pl.program_id(len(grid) - 1)

        def finish(acc):
            outs = (acc,) if epilogue is None else epilogue(acc, *[r[...] for r in ex_refs])
            for r, o in zip(out_refs, outs):
                r[...] = o.astype(r.dtype)

        part = _dg(a_ref[...], b_ref[...], ca, cb)
        if nk == 1:
            finish(part)
        else:
            acc_ref = refs[-1]

            @pl.when(kk == 0)
            def _():
                acc_ref[...] = part

            @pl.when(kk > 0)
            def _():
                acc_ref[...] += part

            @pl.when(kk == nk - 1)
            def _():
                finish(acc_ref[...])

    o_spec = pl.BlockSpec(o_blk, o_map)
    acc_shape = tuple(d for d in o_blk if d is not None)
    res = pl.pallas_call(
        body, grid=grid,
        in_specs=[pl.BlockSpec(a_blk, a_map), pl.BlockSpec(b_blk, b_map)] + [o_spec] * n_ex,
        out_specs=[o_spec] * n_out,
        out_shape=[jax.ShapeDtypeStruct(o_shape, dt) for dt in o_dtypes],
        scratch_shapes=[] if nk == 1 else [pltpu.VMEM(acc_shape, F32)],
        compiler_params=_params(("parallel",) * (len(grid) - 1) + ("arbitrary",)),
        name=name,
    )(a, b, *extras)
    return res[0] if n_out == 1 else res


TM, TN, TK = 1024, 512, 1024


def _mm_nn(name, a, w, layout, o_dtypes=(F32,), extras=(), epilogue=None, tiles=(TM, TN, TK), w_layer=None):
    m, k = a.shape
    tm, tk = _tile(m, tiles[0], 8), _tile(k, tiles[2])
    if layout == "row":
        n = w.shape[-1]
        tn = _tile(n, tiles[1])
        grid = (m // tm, n // tn, k // tk)
        b_blk, b_map = (tk, tn), (lambda i, j, kk: (kk, j))
        if w_layer is not None:
            b_blk, b_map = (None, tk, tn), (lambda i, j, kk: (w_layer, kk, j))
    else:
        nl = w.shape[2]
        n = nl * w.shape[0]
        tn = _tile(nl, tiles[1])
        nb = nl // tn
        grid = (m // tm, n // tn, k // tk)
        b_blk, b_map = (None, tk, tn), (lambda i, j, kk: (j // nb, kk, j % nb))
    return _matmul(name, a, w, ca=1, cb=0, grid=grid, a_blk=(tm, tk), a_map=lambda i, j, kk: (i, kk),
                   b_blk=b_blk, b_map=b_map, o_shape=(m, n), o_dtypes=o_dtypes, o_blk=(tm, tn),
                   o_map=lambda i, j, kk: (i, j), extras=extras, epilogue=epilogue)


def _mm_nt(name, g, w, layout, o_dtypes=(F32,), extras=(), epilogue=None, tiles=(TM, TN, TK)):
    m, n = g.shape
    tm, tn = _tile(m, tiles[0], 8), None
    if layout == "row":
        k = w.shape[0]
        tn = _tile(n, tiles[2])
        tk = _tile(k, tiles[1])
        grid = (m // tm, k // tk, n // tn)
        b_blk, b_map = (tk, tn), (lambda i, j, kk: (j, kk))
    else:
        k, nl = w.shape[1], w.shape[2]
        tn = _tile(nl, tiles[2])
        nb = nl // tn
        tk = _tile(k, tiles[1])
        grid = (m // tm, k // tk, n // tn)
        b_blk, b_map = (None, tk, tn), (lambda i, j, kk: (kk // nb, j, kk % nb))
    return _matmul(name, g, w, ca=1, cb=1, grid=grid, a_blk=(tm, tn), a_map=lambda i, j, kk: (i, kk),
                   b_blk=b_blk, b_map=b_map, o_shape=(m, k), o_dtypes=o_dtypes, o_blk=(tm, tk),
                   o_map=lambda i, j, kk: (i, j), extras=extras, epilogue=epilogue)


def _mm_tn(name, x, g, layout, o_dtype=F32, tiles=(TM, TN, TK)):
    t, k = x.shape
    n = g.shape[1]
    tt = _tile(t, tiles[2], 8)
    tk = _tile(k, tiles[0])
    if layout == "row":
        tn = _tile(n, tiles[1])
        o_shape, o_blk, o_map = (k, n), (tk, tn), (lambda i, j, kk: (i, j))
    else:
        nl = n // N_CHIPS
        tn = _tile(nl, tiles[1])
        nb = nl // tn
        o_shape, o_blk, o_map = (N_CHIPS, k, nl), (None, tk, tn), (lambda i, j, kk: (j // nb, i, j % nb))
    grid = (k // tk, n // tn, t // tt)
    return _matmul(name, x, g, ca=0, cb=0, grid=grid, a_blk=(tt, tk), a_map=lambda i, j, kk: (kk, i),
                   b_blk=(tt, tn), b_map=lambda i, j, kk: (kk, j), o_shape=o_shape, o_dtypes=(o_dtype,),
                   o_blk=o_blk, o_map=o_map)


TR = 256


def _rw_fwd(name, fn, rows, vecs, o_dtypes, tr=TR):
    t = rows[0].shape[0]
    tr = min(tr, t)
    n_r, n_v = len(rows), len(vecs)
    o_sds = jax.eval_shape(fn, *[jax.ShapeDtypeStruct((tr, r.shape[1]), r.dtype) for r in rows],
                           *[jax.ShapeDtypeStruct(v.shape, v.dtype) for v in vecs])

    def body(*refs):
        outs = fn(*[r[...] for r in refs[:n_r + n_v]])
        for r, o in zip(refs[n_r + n_v:], outs):
            r[...] = o.astype(r.dtype)

    res = pl.pallas_call(
        body, grid=(t // tr,),
        in_specs=[pl.BlockSpec((tr, r.shape[1]), lambda i: (i, 0)) for r in rows]
        + [pl.BlockSpec(v.shape, lambda i: (0, 0)) for v in vecs],
        out_specs=[pl.BlockSpec((tr, o.shape[1]), lambda i: (i, 0)) for o in o_sds],
        out_shape=[jax.ShapeDtypeStruct((t, o.shape[1]), dt) for o, dt in zip(o_sds, o_dtypes)],
        compiler_params=_params(("parallel",)), name=name,
    )(*rows, *vecs)
    return res


def _rw_bwd(name, fn, rows, vecs, cots, row_grad_dtypes, add_rows=None, tr=TR):
    t = rows[0].shape[0]
    tr = min(tr, t)
    add_rows = add_rows or {}
    n_r, n_v, n_c = len(rows), len(vecs), len(cots)
    diff = [i for i, dt in enumerate(row_grad_dtypes) if dt is not None]
    adds = [add_rows[i] for i in diff if i in add_rows]
    n_a = len(adds)

    def body(*refs):
        rvals = [r[...] for r in refs[:n_r]]
        vvals = [r[...] for r in refs[n_r:n_r + n_v]]
        c_refs = refs[n_r + n_v:n_r + n_v + n_c]
        a_refs = list(refs[n_r + n_v + n_c:n_r + n_v + n_c + n_a])
        o_refs = refs[n_r + n_v + n_c + n_a:]

        def f(*d):
            full = list(rvals)
            for i, x in zip(diff, d[:len(diff)]):
                full[i] = x
            return fn(*full, *d[len(diff):])

        outs, vjp = jax.vjp(f, *[rvals[i] for i in diff], *vvals)
        grads = vjp(tuple(c[...].astype(o.dtype) for c, o in zip(c_refs, outs)))
        for n, i in enumerate(diff):
            gr = grads[n].astype(F32)
            if i in add_rows:
                gr = gr + a_refs.pop(0)[...].astype(F32)
            o_refs[n][...] = gr.astype(o_refs[n].dtype)
        first = pl.program_id(0) == 0
        for r, gv in zip(o_refs[len(diff):], grads[len(diff):]):
            @pl.when(first)
            def _(r=r, gv=gv):
                r[...] = gv

            @pl.when(jnp.logical_not(first))
            def _(r=r, gv=gv):
                r[...] += gv

    row_spec = lambda a: pl.BlockSpec((tr, a.shape[1]), lambda i: (i, 0))
    vec_spec = lambda a: pl.BlockSpec(a.shape, lambda i: (0, 0))
    res = pl.pallas_call(
        body, grid=(t // tr,),
        in_specs=[row_spec(r) for r in rows] + [vec_spec(v) for v in vecs] + [row_spec(c) for c in cots]
        + [row_spec(a) for a in adds],
        out_specs=[row_spec(rows[i]) for i in diff] + [vec_spec(v) for v in vecs],
        out_shape=[jax.ShapeDtypeStruct(rows[i].shape, row_grad_dtypes[i]) for i in diff]
        + [jax.ShapeDtypeStruct(v.shape, F32) for v in vecs],
        compiler_params=_params(("arbitrary",)), name=name,
    )(*rows, *vecs, *cots, *adds)
    return res[:len(diff)], res[len(diff):]


def _rms(x, g):
    return x * lax.rsqrt(jnp.mean(x * x, axis=-1, keepdims=True) + EPS) * g


def _f_pre(h, g, shift, scale):
    return (_rms(h, g) * (1.0 + scale) + shift,)


def _f_pre2(h, g1, sh1, sc1, g2, sh2, sc2):
    return _f_pre(h, g1, sh1, sc1) + _f_pre(h, g2, sh2, sc2)


def _f_post(y, gate, g):
    return (gate * _rms(y, g),)


def _f_post_bias(y, bias, gate, g):
    return (gate * _rms(y + bias, g),)


def _f_glu(z, bias):
    z = z + bias
    half = z.shape[1] // 2
    return (z[:, :half] * jax.nn.sigmoid(z[:, half:]),)


def _f_ln_silu(u, g, b):
    mu = jnp.mean(u, axis=-1, keepdims=True)
    var = jnp.mean(jnp.square(u - mu), axis=-1, keepdims=True)
    y = (u - mu) * lax.rsqrt(var + EPS) * g + b
    return (y * jax.nn.sigmoid(y),)


def _rope_raw(x, cos, sin):
    n = x.shape[1]
    reps = n // LANES
    if reps > 1:
        cos = jnp.concatenate([cos] * reps, axis=1)
        sin = jnp.concatenate([sin] * reps, axis=1)
    lane = lax.broadcasted_iota(jnp.int32, x.shape, 1)
    half = QK_ROPE // 2
    partner = jnp.where((lane % QK_ROPE) < half, pltpu.roll(x, n - half, 1), pltpu.roll(x, half, 1))
    return x * cos + partner * sin


@jax.custom_vjp
def _rope(x, cos, sin):
    return _rope_raw(x, cos, sin)


def _rope_fwd(x, cos, sin):
    return _rope_raw(x, cos, sin), (cos, sin)


def _rope_bwd(res, g):
    cos, sin = res
    return _rope_raw(g, cos, -sin), jnp.zeros_like(cos), jnp.zeros_like(sin)


_rope.defvjp(_rope_fwd, _rope_bwd)


def _make_f_lat(r_kv):
    def f(pre_kv, pre_q, cos, sin, g_ckv, g_cq):
        c_kv = _rms(pre_kv[:, :r_kv], g_ckv)
        kr = _rope(pre_kv[:, r_kv:], cos, sin)
        return c_kv, kr, _rms(pre_q, g_cq)
    return f


def _f_rope(x, cos, sin):
    return (_rope(x, cos, sin),)


CONV_TC = 128
CONV_TT = 256
PADR = 32


def _conv_fwd(u, dw, dw_b):
    t, d = u.shape
    tc, tt = min(CONV_TC, d), min(CONV_TT, t)
    off = PADR - (CONV_WIDTH - 1)

    def body(u_ref, w_ref, b_ref, o_ref, pad_ref):
        pad_ref[pl.ds(0, PADR), :] = jnp.zeros((PADR, tc), F32)
        pad_ref[pl.ds(PADR, t), :] = u_ref[...]
        for t0 in range(0, t, tt):
            acc = jnp.zeros((tt, tc), F32) + b_ref[...]
            for j in range(CONV_WIDTH):
                acc = acc + pad_ref[pl.ds(t0 + off + j, tt), :] * w_ref[pl.ds(j, 1), :]
            o_ref[pl.ds(t0, tt), :] = acc

    return pl.pallas_call(
        body, grid=(d // tc,),
        in_specs=[pl.BlockSpec((t, tc), lambda j: (0, j)), pl.BlockSpec((CONV_WIDTH, tc), lambda j: (0, j)),
                  pl.BlockSpec((1, tc), lambda j: (0, j))],
        out_specs=pl.BlockSpec((t, tc), lambda j: (0, j)),
        out_shape=jax.ShapeDtypeStruct((t, d), F32),
        scratch_shapes=[pltpu.VMEM((t + PADR, tc), F32)],
        compiler_params=_params(("parallel",)), name="conv_fwd",
    )(u, dw, dw_b)


def _conv_bwd(u, duc, dw):
    t, d = u.shape
    tc, tt = min(CONV_TC, d), min(CONV_TT, t)
    off = PADR - (CONV_WIDTH - 1)

    def body(u_ref, g_ref, w_ref, du_ref, dwt_ref, padu_ref, padg_ref):
        padu_ref[pl.ds(0, PADR), :] = jnp.zeros((PADR, tc), F32)
        padu_ref[pl.ds(PADR, t), :] = u_ref[...]
        padg_ref[pl.ds(t, PADR), :] = jnp.zeros((PADR, tc), F32)
        padg_ref[pl.ds(0, t), :] = g_ref[...]
        for t0 in range(0, t, tt):
            acc = jnp.zeros((tt, tc), F32)
            for j in range(CONV_WIDTH):
                acc = acc + padg_ref[pl.ds(t0 + (CONV_WIDTH - 1) - j, tt), :] * w_ref[pl.ds(j, 1), :]
            du_ref[pl.ds(t0, tt), :] = acc
        for j in range(CONV_WIDTH):
            acc = jnp.zeros((tt, tc), F32)
            for t0 in range(0, t, tt):
                acc = acc + g_ref[pl.ds(t0, tt), :] * padu_ref[pl.ds(t0 + off + j, tt), :]
            dwt_ref[pl.ds(j, 1), :] = jnp.sum(acc, axis=0, keepdims=True)
        acc = jnp.zeros((tt, tc), F32)
        for t0 in range(0, t, tt):
            acc = acc + g_ref[pl.ds(t0, tt), :]
        dwt_ref[pl.ds(CONV_WIDTH, 1), :] = jnp.sum(acc, axis=0, keepdims=True)

    col = lambda r: pl.BlockSpec((r, tc), lambda j: (0, j))
    return pl.pallas_call(
        body, grid=(d // tc,),
        in_specs=[col(t), col(t), col(CONV_WIDTH)],
        out_specs=[col(t), col(CONV_WIDTH + 1)],
        out_shape=[jax.ShapeDtypeStruct((t, d), F32), jax.ShapeDtypeStruct((CONV_WIDTH + 1, d), F32)],
        scratch_shapes=[pltpu.VMEM((t + PADR, tc), F32), pltpu.VMEM((t + PADR, tc), F32)],
        compiler_params=_params(("parallel",)), name="conv_bwd",
    )(u, duc, dw)


@jax.custom_vjp
def _swap_halves(x):
    return pltpu.roll(x, LANES // 2, 1)


_swap_halves.defvjp(lambda x: (pltpu.roll(x, LANES // 2, 1), None), lambda _, g: (pltpu.roll(g, LANES // 2, 1),))


def _attn_block(qn, qr, kn, kr, v, q0):
    scale = (QK_NOPE + QK_ROPE) ** -0.5
    lane = lax.broadcasted_iota(jnp.int32, kr.shape, 1)
    kr_a = kr * (lane < QK_ROPE).astype(kr.dtype)
    kr_b = _swap_halves(kr_a)
    outs = []
    for hh, kr_h in ((0, kr_a), (1, kr_b)):
        sl = slice(hh * QK_NOPE, (hh + 1) * QK_NOPE)
        s = (_dot_nt(qn[:, sl], kn[:, sl]) + _dot_nt(qr, kr_h)) * scale
        row = lax.broadcasted_iota(jnp.int32, s.shape, 0) + q0
        col = lax.broadcasted_iota(jnp.int32, s.shape, 1)
        s = jnp.where(col <= row, s, NEG)
        e = jnp.exp(s - jnp.max(s, axis=-1, keepdims=True))
        p = e / jnp.sum(e, axis=-1, keepdims=True)
        outs.append(_dot_nn(p, v[:, sl]))
    return jnp.concatenate(outs, axis=1)


def _attn_fwd(qn, qr, kn, kr, v, tq=512):
    t, w = qn.shape
    pairs = w // (2 * QK_NOPE)
    tq = min(tq, t)
    pw = 2 * QK_NOPE

    def body(qn_ref, qr_ref, kn_ref, kr_ref, v_ref, o_ref):
        for q0 in range(0, t, tq):
            l = q0 + tq
            o_ref[pl.ds(q0, tq), :] = _attn_block(
                qn_ref[pl.ds(q0, tq), :], qr_ref[pl.ds(q0, tq), :], kn_ref[pl.ds(0, l), :], kr_ref[pl.ds(0, l), :],
                v_ref[pl.ds(0, l), :], q0).astype(o_ref.dtype)

    pair = lambda wd: pl.BlockSpec((t, wd), lambda p: (0, p))
    return pl.pallas_call(
        body, grid=(pairs,),
        in_specs=[pair(pw), pair(LANES), pair(pw), pl.BlockSpec((t, LANES), lambda p: (0, 0)), pair(pw)],
        out_specs=pair(pw), out_shape=jax.ShapeDtypeStruct((t, w), BF),
        compiler_params=_params(("parallel",)), name="attn_fwd",
    )(qn, qr, kn, kr, v)


def _attn_bwd(qn, qr, kn, kr, v, do, tq=512):
    t, w = qn.shape
    pairs = w // (2 * QK_NOPE)
    tq = min(tq, t)
    pw = 2 * QK_NOPE

    def body(qn_ref, qr_ref, kn_ref, kr_ref, v_ref, do_ref, dqn_ref, dqr_ref, dkn_ref, dv_ref, dkr_ref,
             akn_ref, av_ref, akr_ref):
        akn_ref[...] = jnp.zeros_like(akn_ref)
        av_ref[...] = jnp.zeros_like(av_ref)
        akr_ref[...] = jnp.zeros_like(akr_ref)
        for q0 in range(0, t, tq):
            l = q0 + tq
            rows, keys = pl.ds(q0, tq), pl.ds(0, l)
            _, vjp = jax.vjp(functools.partial(_attn_block, q0=q0), qn_ref[rows, :], qr_ref[rows, :],
                             kn_ref[keys, :], kr_ref[keys, :], v_ref[keys, :])
            dqn, dqr, dkn, dkr, dv = vjp(do_ref[rows, :].astype(F32))
            dqn_ref[rows, :] = dqn.astype(dqn_ref.dtype)
            dqr_ref[rows, :] = dqr.astype(dqr_ref.dtype)
            akn_ref[keys, :] += dkn.astype(F32)
            av_ref[keys, :] += dv.astype(F32)
            akr_ref[keys, :] += dkr.astype(F32)
        dkn_ref[...] = akn_ref[...].astype(dkn_ref.dtype)
        dv_ref[...] = av_ref[...].astype(dv_ref.dtype)
        first = pl.program_id(0) == 0

        @pl.when(first)
        def _():
            dkr_ref[...] = akr_ref[...]

        @pl.when(jnp.logical_not(first))
        def _():
            dkr_ref[...] += akr_ref[...]

    pair = lambda wd: pl.BlockSpec((t, wd), lambda p: (0, p))
    shared = pl.BlockSpec((t, LANES), lambda p: (0, 0))
    sds = jax.ShapeDtypeStruct
    return pl.pallas_call(
        body, grid=(pairs,),
        in_specs=[pair(pw), pair(LANES), pair(pw), shared, pair(pw), pair(pw)],
        out_specs=[pair(pw), pair(LANES), pair(pw), pair(pw), shared],
        out_shape=[sds((t, w), BF), sds((t, pairs * LANES), F32), sds((t, w), BF), sds((t, w), BF), sds((t, LANES), F32)],
        scratch_shapes=[pltpu.VMEM((t, pw), F32), pltpu.VMEM((t, pw), F32), pltpu.VMEM((t, LANES), F32)],
        compiler_params=_params(("arbitrary",)), name="attn_bwd",
    )(qn, qr, kn, kr, v, do)


def _ew(name, fn, ins, o_dtypes, max_bytes=2 ** 21):
    r, c = ins[0].shape
    tr = r
    if r * c * 4 > max_bytes:
        tr = max(16, (max_bytes // (c * 4)) // 16 * 16)
        while r % tr:
            tr -= 16
    n_in = len(ins)

    def body(*refs):
        outs = fn(*[x[...] for x in refs[:n_in]])
        for o_ref, o in zip(refs[n_in:], outs):
            o_ref[...] = o.astype(o_ref.dtype)

    spec = pl.BlockSpec((tr, c), lambda i: (i, 0))
    return pl.pallas_call(
        body, grid=(r // tr,), in_specs=[spec] * n_in, out_specs=[spec] * len(o_dtypes),
        out_shape=[jax.ShapeDtypeStruct((r, c), dt) for dt in o_dtypes],
        compiler_params=_params(("parallel",)), name=name,
    )(*ins)


def _adamw_math(w, g, m, v):
    m = ADAM_B1 * m + (1.0 - ADAM_B1) * g
    v = ADAM_B2 * v + (1.0 - ADAM_B2) * jnp.square(g)
    m_hat = m / (1.0 - ADAM_B1 ** ADAM_STEP)
    v_hat = v / (1.0 - ADAM_B2 ** ADAM_STEP)
    delta = -ADAM_LR * (m_hat / (jnp.sqrt(v_hat) + ADAM_EPS) + ADAM_WD * w)
    return delta, m, v


def _adamw(name, w, g, m, v):
    shape = w.shape
    to2 = lambda a: a.reshape(-1, shape[-1]) if a.ndim > 1 else a.reshape(1, -1)
    d, nm, nv = _ew("adamw_" + name, _adamw_math, [to2(w), to2(g), to2(m), to2(v)], (F32, F32, F32))
    return d.reshape(shape), nm.reshape(shape), nv.reshape(shape)


def _rope_tables(positions, t):
    half = QK_ROPE // 2
    inv = 1.0 / (ROPE_THETA ** (jnp.arange(0, QK_ROPE, 2, dtype=F32) / QK_ROPE))
    inv_l = jnp.tile(inv, LANES // half).reshape(1, LANES)
    sign = jnp.tile(jnp.concatenate([-jnp.ones((half,), F32), jnp.ones((half,), F32)]), LANES // QK_ROPE).reshape(1, LANES)

    def body(p_ref, inv_ref, sg_ref, c_ref, s_ref):
        ang = p_ref[...].astype(F32) * inv_ref[...]
        c_ref[...] = jnp.cos(ang)
        s_ref[...] = jnp.sin(ang) * sg_ref[...]

    return pl.pallas_call(body, out_shape=[jax.ShapeDtypeStruct((t, LANES), F32)] * 2, name="rope_tables")(
        positions.reshape(t, 1), inv_l, sign)


def _loss_grad(h, target):
    t, d = h.shape
    tr = min(TR, t)

    def body(h_ref, y_ref, dh_ref, l_ref):
        err = h_ref[...] - y_ref[...]
        dh_ref[...] = err * (1.0 / d)
        part = 0.5 * jnp.sum(jnp.mean(jnp.square(err), axis=-1, keepdims=True), axis=0, keepdims=True)
        first = pl.program_id(0) == 0

        @pl.when(first)
        def _():
            l_ref[...] = part

        @pl.when(jnp.logical_not(first))
        def _():
            l_ref[...] += part

    row = pl.BlockSpec((tr, d), lambda i: (i, 0))
    return pl.pallas_call(
        body, grid=(t // tr,), in_specs=[row, row],
        out_specs=[row, pl.BlockSpec((1, 1), lambda i: (0, 0))],
        out_shape=[jax.ShapeDtypeStruct((t, d), F32), jax.ShapeDtypeStruct((1, 1), F32)],
        compiler_params=_params(("arbitrary",)), name="loss_grad",
    )(h, target)


def _sum_devices(g):
    def fn(*parts):
        acc = parts[0]
        for p in parts[1:]:
            acc = acc + p
        return (acc,)
    return _ew("sum_devices", fn, [g[i] for i in range(g.shape[0])], (F32,))[0]


def _place():
    x, y, c = lax.axis_index("x"), lax.axis_index("y"), lax.axis_index("c")
    return x, y, c, [(1 - x, y), (x, 1 - y), (1 - x, 1 - y)]


def _all_gather_small(name, v):
    r, n = v.shape

    def body(x_ref, out_ref, send_sems, recv_sems, local_sem):
        x, y, c, chips = _place()
        me, sibling = (x, y, c), (x, y, 1 - c)

        def rows(px, py, pc):
            return out_ref.at[4 * px + 2 * py + pc]

        def copy(k, block, to, src=None):
            return pltpu.make_async_remote_copy(
                src_ref=rows(*block) if src is None else src, dst_ref=rows(*block), send_sem=send_sems.at[k],
                recv_sem=recv_sems.at[k], device_id=to, device_id_type=MESH)

        mine = pltpu.make_async_copy(x_ref, rows(*me), local_sem)
        mine.start()
        first = [copy(0, me, sibling, src=x_ref)]
        first += [copy(1 + j, me, (*chip, c), src=x_ref) for j, chip in enumerate(chips)]
        for cp in first:
            cp.start()
        passed = [copy(4 + j, (*chip, c), sibling) for j, chip in enumerate(chips)]
        for j, chip in enumerate(chips):
            copy(1 + j, (*chip, c), me).wait_recv()
            passed[j].start()
        copy(0, sibling, me).wait_recv()
        for j, chip in enumerate(chips):
            copy(4 + j, (*chip, 1 - c), me).wait_recv()
        for cp in first + passed:
            cp.wait_send()
        mine.wait()

    return pl.pallas_call(
        body, out_shape=jax.ShapeDtypeStruct((N_DEV, r, n), v.dtype),
        in_specs=[pl.BlockSpec(memory_space=pltpu.VMEM)], out_specs=pl.BlockSpec(memory_space=pltpu.VMEM),
        scratch_shapes=[pltpu.SemaphoreType.DMA((7,)), pltpu.SemaphoreType.DMA((7,)), pltpu.SemaphoreType.DMA],
        compiler_params=pltpu.CompilerParams(vmem_limit_bytes=VMEM_LIMIT), name=name,
    )(v)


def _cast_into_slot(name, w, layer, idx, after=None):
    _, r, n = w.shape
    tr = r
    while tr * n * 4 > 2 ** 21 and tr % 32 == 0:
        tr //= 2
    order = [] if after is None else [after]

    def body(idx_ref, w_ref, *rest):
        o_ref = rest[-1]
        o_ref[...] = w_ref[...].astype(o_ref.dtype)

    return pl.pallas_call(
        body,
        grid_spec=pltpu.PrefetchScalarGridSpec(
            num_scalar_prefetch=1, grid=(r // tr,),
            in_specs=[pl.BlockSpec((None, tr, n), lambda i, idx_ref: (layer, i, 0))] + [ANY] * len(order),
            out_specs=pl.BlockSpec((None, tr, n), lambda i, idx_ref: (idx_ref[1], i, 0))),
        out_shape=jax.ShapeDtypeStruct((N_CHIPS, r, n), BF),
        compiler_params=_params(("parallel",)), name="cast_" + name,
    )(idx, w, *order)


def _gather_weights(bufs):
    n_w = len(bufs)

    def body(*refs):
        outs = refs[n_w:2 * n_w]
        send_sems, recv_sems = refs[2 * n_w:]
        x, y, c, chips = _place()

        def copy(i, s, chip, which, to):
            h = outs[i].shape[1] // 2
            blk = outs[i].at[2 * chip[0] + chip[1], pl.ds(which * h, h)]
            return pltpu.make_async_remote_copy(
                src_ref=blk, dst_ref=blk, send_sem=send_sems.at[6 * i + s], recv_sem=recv_sems.at[6 * i + s],
                device_id=to, device_id_type=MESH)

        sends = []
        for i in range(n_w):
            for j, chip in enumerate(chips):
                cp = copy(i, j, (x, y), c, (*chip, c))
                cp.start()
                sends.append(cp)
        for i in range(n_w):
            for j, chip in enumerate(chips):
                copy(i, j, chip, c, (x, y, c)).wait_recv()
                cp = copy(i, 3 + j, chip, c, (x, y, 1 - c))
                cp.start()
                sends.append(cp)
        for i in range(n_w):
            for j, chip in enumerate(chips):
                copy(i, 3 + j, chip, 1 - c, (x, y, c)).wait_recv()
        for cp in sends:
            cp.wait_send()

    return pl.pallas_call(
        body, in_specs=[ANY] * n_w, out_specs=[ANY] * n_w,
        out_shape=[jax.ShapeDtypeStruct(b.shape, b.dtype) for b in bufs],
        input_output_aliases={i: i for i in range(n_w)},
        scratch_shapes=[pltpu.SemaphoreType.DMA((6 * n_w,)), pltpu.SemaphoreType.DMA((6 * n_w,))],
        name="gather_weights",
    )(*bufs)


def _swap_other_half(name, grads, after=None):
    n_w = len(grads)
    order = [] if after is None else [after]
    n_in = n_w + len(order)

    def body(*refs):
        ins, outs = refs[:n_w], refs[n_in:n_in + n_w]
        send_sems, recv_sems = refs[n_in + n_w:]
        x, y, c, _ = _place()
        cps = []
        for i in range(n_w):
            h = ins[i].shape[1] // 2
            cp = pltpu.make_async_remote_copy(
                src_ref=ins[i].at[:, pl.ds((1 - c) * h, h), :], dst_ref=outs[i], send_sem=send_sems.at[i],
                recv_sem=recv_sems.at[i], device_id=(x, y, 1 - c), device_id_type=MESH)
            cp.start()
            cps.append(cp)
        for cp in cps:
            cp.wait()

    return pl.pallas_call(
        body, in_specs=[ANY] * n_in, out_specs=[ANY] * n_w,
        out_shape=[jax.ShapeDtypeStruct((g.shape[0], g.shape[1] // 2, g.shape[2]), g.dtype) for g in grads],
        scratch_shapes=[pltpu.SemaphoreType.DMA((n_w,)), pltpu.SemaphoreType.DMA((n_w,))],
        name=name,
    )(*grads, *order)


def _add_my_half(name, g, s, idx, o_dtype):
    nc, r, n = g.shape
    h = r // 2
    tr = h
    while tr * n * 4 > 2 ** 21 and tr % 32 == 0:
        tr //= 2
    nb = h // tr

    def body(idx_ref, g_ref, s_ref, o_ref, own_ref):
        val = (g_ref[...].astype(F32) + s_ref[...].astype(F32)).astype(o_ref.dtype)
        o_ref[...] = val

        @pl.when(pl.program_id(1) == idx_ref[1])
        def _():
            own_ref[...] = val

    return pl.pallas_call(
        body,
        grid_spec=pltpu.PrefetchScalarGridSpec(
            num_scalar_prefetch=1, grid=(nb, nc),
            in_specs=[pl.BlockSpec((None, tr, n), lambda i, j, idx_ref: (j, idx_ref[0] * nb + i, 0)),
                      pl.BlockSpec((None, tr, n), lambda i, j, idx_ref: (j, i, 0))],
            out_specs=[pl.BlockSpec((None, tr, n), lambda i, j, idx_ref: (j, i, 0)),
                       pl.BlockSpec((None, tr, n), lambda i, j, idx_ref: (idx_ref[1], i, 0))]),
        out_shape=[jax.ShapeDtypeStruct((nc, h, n), o_dtype)] * 2,
        compiler_params=_params(("parallel", "arbitrary")), name=name,
    )(idx, g, s)


def _scatter_partials(parts, bufs):
    n_w = len(parts)

    def body(*refs):
        ins, outs = refs[:n_w], refs[2 * n_w:3 * n_w]
        send_sems, recv_sems = refs[3 * n_w:]
        x, y, c, chips = _place()
        k = 2 * x + y
        sends = []
        for i in range(n_w):
            for j, chip in enumerate(chips):
                cp = pltpu.make_async_remote_copy(
                    src_ref=ins[i].at[2 * chip[0] + chip[1]], dst_ref=outs[i].at[k], send_sem=send_sems.at[3 * i + j],
                    recv_sem=recv_sems.at[3 * i + j], device_id=(*chip, c), device_id_type=MESH)
                cp.start()
                sends.append(cp)
        for i in range(n_w):
            for j, chip in enumerate(chips):
                blk = outs[i].at[2 * chip[0] + chip[1]]
                pltpu.make_async_remote_copy(
                    src_ref=blk, dst_ref=blk, send_sem=send_sems.at[3 * i + j], recv_sem=recv_sems.at[3 * i + j],
                    device_id=(*chip, c), device_id_type=MESH).wait_recv()
        for cp in sends:
            cp.wait_send()

    return pl.pallas_call(
        body, in_specs=[ANY] * (2 * n_w), out_specs=[ANY] * n_w,
        out_shape=[jax.ShapeDtypeStruct(b.shape, b.dtype) for b in bufs],
        input_output_aliases={n_w + i: i for i in range(n_w)},
        scratch_shapes=[pltpu.SemaphoreType.DMA((3 * n_w,)), pltpu.SemaphoreType.DMA((3 * n_w,))],
        name="scatter_partials",
    )(*parts, *bufs)


def _sum_chips(name, q, idx):
    nc, h, n = q.shape
    tr = h
    while tr * n * 4 > 2 ** 20 and tr % 32 == 0:
        tr //= 2
    nb = h // tr

    def body(idx_ref, q_ref, o_ref):
        acc = q_ref[0].astype(F32)
        for j in range(1, nc):
            acc = acc + q_ref[j].astype(F32)
        o_ref[...] = acc

    return pl.pallas_call(
        body,
        grid_spec=pltpu.PrefetchScalarGridSpec(
            num_scalar_prefetch=1, grid=(nb,),
            in_specs=[pl.BlockSpec((nc, tr, n), lambda i, idx_ref: (0, i, 0))],
            out_specs=pl.BlockSpec((tr, n), lambda i, idx_ref: (idx_ref[0] * nb + i, 0))),
        out_shape=jax.ShapeDtypeStruct((2 * h, n), F32),
        compiler_params=_params(("parallel",)), name=name,
    )(idx, q)


def _join_halves(name, bufs):
    n_w = len(bufs)

    def body(*refs):
        outs = refs[n_w:2 * n_w]
        send_sems, recv_sems = refs[2 * n_w:]
        x, y, c, _ = _place()

        def copy(i, which):
            h = outs[i].shape[0] // 2
            rows = outs[i].at[pl.ds(which * h, h)]
            return pltpu.make_async_remote_copy(
                src_ref=rows, dst_ref=rows, send_sem=send_sems.at[i], recv_sem=recv_sems.at[i],
                device_id=(x, y, 1 - c), device_id_type=MESH)

        cps = [copy(i, c) for i in range(n_w)]
        for cp in cps:
            cp.start()
        for i, cp in enumerate(cps):
            cp.wait_send()
            copy(i, 1 - c).wait_recv()

    return pl.pallas_call(
        body, in_specs=[ANY] * n_w, out_specs=[ANY] * n_w,
        out_shape=[jax.ShapeDtypeStruct(b.shape, b.dtype) for b in bufs],
        input_output_aliases={i: i for i in range(n_w)},
        scratch_shapes=[pltpu.SemaphoreType.DMA((n_w,)), pltpu.SemaphoreType.DMA((n_w,))],
        name=name,
    )(*bufs)


HBM_SPEC = pl.BlockSpec(memory_space=pltpu.HBM)
SEM_SPEC = pl.BlockSpec(memory_space=pltpu.SEMAPHORE)
VMEM_SPEC = pl.BlockSpec(memory_space=pltpu.VMEM)
SIDE_EFFECT = pltpu.SideEffectType.DATAFLOW_SIDE_EFFECTING


def _in_hbm(arrays):
    return [pltpu.with_memory_space_constraint(a, pltpu.HBM) for a in arrays]


def _ici_gather_copy(ref, i, j, chip, send_sems, recv_sems):
    x, y, c, _ = _place()
    h = ref.shape[1] // 2
    mine = ref.at[2 * x + y, pl.ds(c * h, h)]
    return pltpu.make_async_remote_copy(
        src_ref=mine, dst_ref=mine, send_sem=send_sems.at[3 * i + j], recv_sem=recv_sems.at[3 * i + j],
        device_id=(*chip, c), device_id_type=MESH)


def _gather_start(name, bufs, after=None):
    n_w = len(bufs)
    order = [] if after is None else [after]

    def body(*refs):
        ins, token = refs[:n_w], refs[-1]
        send_sems, recv_sems = refs[n_w + len(order)], refs[n_w + len(order) + 1]
        chips = _place()[3]
        for i in range(n_w):
            for j, chip in enumerate(chips):
                _ici_gather_copy(ins[i], i, j, chip, send_sems, recv_sems).start()
        token[...] = jnp.zeros_like(token)

    res = pl.pallas_call(
        body, name=name, in_specs=[HBM_SPEC] * n_w + [ANY] * len(order),
        out_shape=(pltpu.SemaphoreType.DMA((3 * n_w,)), pltpu.SemaphoreType.DMA((3 * n_w,)),
                   *[pltpu.HBM(b.shape, b.dtype) for b in bufs], jax.ShapeDtypeStruct((8, LANES), F32)),
        out_specs=(SEM_SPEC, SEM_SPEC, *[HBM_SPEC] * n_w, VMEM_SPEC),
        input_output_aliases={i: 2 + i for i in range(n_w)},
        compiler_params=pltpu.CompilerParams(has_side_effects=SIDE_EFFECT),
    )(*_in_hbm(bufs), *order)
    return res[0], res[1], list(res[2:2 + n_w]), res[-1]


def _gather_wait(name, send_sems, recv_sems, bufs, after):
    n_w = len(bufs)

    def body(*refs):
        ins, s_sems, r_sems = refs[:n_w], refs[n_w], refs[n_w + 1]
        chips = _place()[3]
        for i in range(n_w):
            for j, chip in enumerate(chips):
                cp = _ici_gather_copy(ins[i], i, j, chip, s_sems, r_sems)
                cp.wait_send()
                cp.wait_recv()

    return pl.pallas_call(
        body, name=name, in_specs=[HBM_SPEC] * n_w + [SEM_SPEC, SEM_SPEC, ANY],
        out_shape=[pltpu.HBM(b.shape, b.dtype) for b in bufs], out_specs=[HBM_SPEC] * n_w,
        input_output_aliases={i: i for i in range(n_w)},
        compiler_params=pltpu.CompilerParams(has_side_effects=SIDE_EFFECT),
    )(*bufs, send_sems, recv_sems, after)


def _forward_to_sibling(name, bufs):
    n_w = len(bufs)

    def body(*refs):
        outs = refs[n_w:2 * n_w]
        send_sems, recv_sems = refs[2 * n_w:]
        x, y, c, chips = _place()

        def copy(i, j, chip, which):
            h = outs[i].shape[1] // 2
            blk = outs[i].at[2 * chip[0] + chip[1], pl.ds(which * h, h)]
            return pltpu.make_async_remote_copy(
                src_ref=blk, dst_ref=blk, send_sem=send_sems.at[3 * i + j], recv_sem=recv_sems.at[3 * i + j],
                device_id=(x, y, 1 - c), device_id_type=MESH)

        sends = [copy(i, j, chip, c) for i in range(n_w) for j, chip in enumerate(chips)]
        for cp in sends:
            cp.start()
        for i in range(n_w):
            for j, chip in enumerate(chips):
                copy(i, j, chip, 1 - c).wait_recv()
        for cp in sends:
            cp.wait_send()

    return pl.pallas_call(
        body, in_specs=[ANY] * n_w, out_specs=[ANY] * n_w,
        out_shape=[jax.ShapeDtypeStruct(b.shape, b.dtype) for b in bufs],
        input_output_aliases={i: i for i in range(n_w)},
        scratch_shapes=[pltpu.SemaphoreType.DMA((3 * n_w,)), pltpu.SemaphoreType.DMA((3 * n_w,))],
        name=name,
    )(*bufs)


def _ici_scatter_copy(part, buf, i, j, chip, send_sems, recv_sems):
    x, y, c, _ = _place()
    return pltpu.make_async_remote_copy(
        src_ref=part.at[2 * chip[0] + chip[1]], dst_ref=buf.at[2 * x + y], send_sem=send_sems.at[3 * i + j],
        recv_sem=recv_sems.at[3 * i + j], device_id=(*chip, c), device_id_type=MESH)


def _ici_scatter_landing(buf, i, j, chip, send_sems, recv_sems):
    x, y, c, _ = _place()
    blk = buf.at[2 * chip[0] + chip[1]]
    return pltpu.make_async_remote_copy(
        src_ref=blk, dst_ref=blk, send_sem=send_sems.at[3 * i + j], recv_sem=recv_sems.at[3 * i + j],
        device_id=(*chip, c), device_id_type=MESH)


def _scatter_start(name, parts, bufs):
    n_w = len(parts)

    def body(*refs):
        ps, bs = refs[:n_w], refs[n_w:2 * n_w]
        send_sems, recv_sems, token = refs[2 * n_w], refs[2 * n_w + 1], refs[-1]
        chips = _place()[3]
        for i in range(n_w):
            for j, chip in enumerate(chips):
                _ici_scatter_copy(ps[i], bs[i], i, j, chip, send_sems, recv_sems).start()
        token[...] = jnp.zeros_like(token)

    both = list(parts) + list(bufs)
    res = pl.pallas_call(
        body, name=name, in_specs=[HBM_SPEC] * (2 * n_w),
        out_shape=(pltpu.SemaphoreType.DMA((3 * n_w,)), pltpu.SemaphoreType.DMA((3 * n_w,)),
                   *[pltpu.HBM(b.shape, b.dtype) for b in both], jax.ShapeDtypeStruct((8, LANES), F32)),
        out_specs=(SEM_SPEC, SEM_SPEC, *[HBM_SPEC] * (2 * n_w), VMEM_SPEC),
        input_output_aliases={i: 2 + i for i in range(2 * n_w)},
        compiler_params=pltpu.CompilerParams(has_side_effects=SIDE_EFFECT),
    )(*_in_hbm(both))
    return res[0], res[1], list(res[2:2 + n_w]), list(res[2 + n_w:2 + 2 * n_w]), res[-1]


def _scatter_wait(name, send_sems, recv_sems, parts, bufs, after):
    n_w = len(parts)
    after = list(after) if isinstance(after, (list, tuple)) else [after]

    def body(*refs):
        ps, bs = refs[:n_w], refs[n_w:2 * n_w]
        s_sems, r_sems = refs[2 * n_w], refs[2 * n_w + 1]
        chips = _place()[3]
        for i in range(n_w):
            for j, chip in enumerate(chips):
                _ici_scatter_copy(ps[i], bs[i], i, j, chip, s_sems, r_sems).wait_send()
                _ici_scatter_landing(bs[i], i, j, chip, s_sems, r_sems).wait_recv()

    both = list(parts) + list(bufs)
    res = pl.pallas_call(
        body, name=name, in_specs=[HBM_SPEC] * (2 * n_w) + [SEM_SPEC, SEM_SPEC] + [ANY] * len(after),
        out_shape=[pltpu.HBM(b.shape, b.dtype) for b in both], out_specs=[HBM_SPEC] * (2 * n_w),
        input_output_aliases={i: i for i in range(2 * n_w)},
        compiler_params=pltpu.CompilerParams(has_side_effects=SIDE_EFFECT),
    )(*both, send_sems, recv_sems, *after)
    return list(res[n_w:])


def _swap_copy(g, land, i, send_sems, recv_sems):
    x, y, c, _ = _place()
    h = g.shape[1] // 2
    return pltpu.make_async_remote_copy(
        src_ref=g.at[:, pl.ds((1 - c) * h, h), :], dst_ref=land, send_sem=send_sems.at[i], recv_sem=recv_sems.at[i],
        device_id=(x, y, 1 - c), device_id_type=MESH)


def _swap_start(name, grads):
    n_w = len(grads)
    lands = [lax.empty((g.shape[0], g.shape[1] // 2, g.shape[2]), g.dtype) for g in grads]

    def body(*refs):
        gs, ls = refs[:n_w], refs[n_w:2 * n_w]
        send_sems, recv_sems, token = refs[2 * n_w], refs[2 * n_w + 1], refs[-1]
        for i in range(n_w):
            _swap_copy(gs[i], ls[i], i, send_sems, recv_sems).start()
        token[...] = jnp.zeros_like(token)

    both = list(grads) + lands
    res = pl.pallas_call(
        body, name=name, in_specs=[HBM_SPEC] * (2 * n_w),
        out_shape=(pltpu.SemaphoreType.DMA((n_w,)), pltpu.SemaphoreType.DMA((n_w,)),
                   *[pltpu.HBM(b.shape, b.dtype) for b in both], jax.ShapeDtypeStruct((8, LANES), F32)),
        out_specs=(SEM_SPEC, SEM_SPEC, *[HBM_SPEC] * (2 * n_w), VMEM_SPEC),
        input_output_aliases={i: 2 + i for i in range(2 * n_w)},
        compiler_params=pltpu.CompilerParams(has_side_effects=SIDE_EFFECT),
    )(*_in_hbm(both))
    return (res[0], res[1], list(res[2:2 + n_w]), list(res[2 + n_w:2 + 2 * n_w])), res[-1]


def _swap_wait(name, swapped, after):
    send_sems, recv_sems, grads, lands = swapped
    n_w = len(grads)

    def body(*refs):
        gs, ls = refs[:n_w], refs[n_w:2 * n_w]
        s_sems, r_sems = refs[2 * n_w], refs[2 * n_w + 1]
        for i in range(n_w):
            cp = _swap_copy(gs[i], ls[i], i, s_sems, r_sems)
            cp.wait_send()
            cp.wait_recv()

    both = list(grads) + list(lands)
    res = pl.pallas_call(
        body, name=name, in_specs=[HBM_SPEC] * (2 * n_w) + [SEM_SPEC, SEM_SPEC, ANY],
        out_shape=[pltpu.HBM(b.shape, b.dtype) for b in both], out_specs=[HBM_SPEC] * (2 * n_w),
        input_output_aliases={i: i for i in range(2 * n_w)},
        compiler_params=pltpu.CompilerParams(has_side_effects=SIDE_EFFECT),
    )(*both, send_sems, recv_sems, after)
    return list(res[:n_w]), list(res[n_w:])


def _reduce_scatter_begin(tag, grads, idx, after=None, swapped=None):
    if swapped is None:
        from_sibling = _swap_other_half("swap_other_half_" + tag, grads, after)
    else:
        grads, from_sibling = _swap_wait("swap_wait_" + tag, swapped, after)
    pairs = [_add_my_half("add_my_half_%s_%d" % (tag, i), g, s, idx, BF)
             for i, (g, s) in enumerate(zip(grads, from_sibling))]
    s_sems, r_sems, parts, bufs, token = _scatter_start("scatter_start_" + tag, [p[0] for p in pairs],
                                                        [p[1] for p in pairs])
    return (tag, s_sems, r_sems, parts, bufs, idx), token


def _reduce_scatter_end(state, after):
    tag, s_sems, r_sems, parts, bufs, idx = state
    received = _scatter_wait("scatter_wait_" + tag, s_sems, r_sems, parts, bufs, after)
    halves = [_sum_chips("sum_chips_%s_%d" % (tag, i), q, idx) for i, q in enumerate(received)]
    return _join_halves("join_halves_" + tag, halves)


def _relu2_epilogue(acc):
    r = jnp.maximum(acc, 0.0)
    return (r * r,)


def _relu2_bwd_epilogue(acc, s):
    return (acc * (2.0 * jnp.sqrt(s.astype(F32))),)


WIDE = (2048, 512, 2048)
DEEP = (2048, 512, 2048)


def _add_epilogue(acc, other):
    return (acc + other,)


def kernel(x, c, positions, w_ada_mix, b_ada_mix, w_ada_mlp, b_ada_mlp, g_pre_mix, g_post_mix, g_pre_mlp, g_post_mlp, conv_w_in, conv_b_in, conv_dw, conv_dw_b, conv_ln_g, conv_ln_b, conv_w_out, conv_b_out, w_ada_kv, b_ada_kv, g_kv, w_dkv, g_ckv, w_kr, w_uk, w_uv, w_dq, g_cq, w_uq, w_o, mlp_w_up, mlp_w_down, loss_target, m_w_ada_mix, m_b_ada_mix, m_w_ada_mlp, m_b_ada_mlp, m_g_pre_mix, m_g_post_mix, m_g_pre_mlp, m_g_post_mlp, m_conv_w_in, m_conv_b_in, m_conv_dw, m_conv_dw_b, m_conv_ln_g, m_conv_ln_b, m_conv_w_out, m_conv_b_out, m_w_ada_kv, m_b_ada_kv, m_g_kv, m_w_dkv, m_g_ckv, m_w_kr, m_w_uk, m_w_uv, m_w_dq, m_g_cq, m_w_uq, m_w_o, m_mlp_w_up, m_mlp_w_down, v_w_ada_mix, v_b_ada_mix, v_w_ada_mlp, v_b_ada_mlp, v_g_pre_mix, v_g_post_mix, v_g_pre_mlp, v_g_post_mlp, v_conv_w_in, v_conv_b_in, v_conv_dw, v_conv_dw_b, v_conv_ln_g, v_conv_ln_b, v_conv_w_out, v_conv_b_out, v_w_ada_kv, v_b_ada_kv, v_g_kv, v_w_dkv, v_g_ckv, v_w_kr, v_w_uk, v_w_uv, v_w_dq, v_g_cq, v_w_uq, v_w_o, v_mlp_w_up, v_mlp_w_down):
    weights = dict(w_ada_mix=w_ada_mix, b_ada_mix=b_ada_mix, w_ada_mlp=w_ada_mlp, b_ada_mlp=b_ada_mlp, g_pre_mix=g_pre_mix, g_post_mix=g_post_mix, g_pre_mlp=g_pre_mlp, g_post_mlp=g_post_mlp, conv_w_in=conv_w_in, conv_b_in=conv_b_in, conv_dw=conv_dw, conv_dw_b=conv_dw_b, conv_ln_g=conv_ln_g, conv_ln_b=conv_ln_b, conv_w_out=conv_w_out, conv_b_out=conv_b_out, w_ada_kv=w_ada_kv, b_ada_kv=b_ada_kv, g_kv=g_kv, w_dkv=w_dkv, g_ckv=g_ckv, w_kr=w_kr, w_uk=w_uk, w_uv=w_uv, w_dq=w_dq, g_cq=g_cq, w_uq=w_uq, w_o=w_o, mlp_w_up=mlp_w_up, mlp_w_down=mlp_w_down)
    m_in = dict(w_ada_mix=m_w_ada_mix, b_ada_mix=m_b_ada_mix, w_ada_mlp=m_w_ada_mlp, b_ada_mlp=m_b_ada_mlp, g_pre_mix=m_g_pre_mix, g_post_mix=m_g_post_mix, g_pre_mlp=m_g_pre_mlp, g_post_mlp=m_g_post_mlp, conv_w_in=m_conv_w_in, conv_b_in=m_conv_b_in, conv_dw=m_conv_dw, conv_dw_b=m_conv_dw_b, conv_ln_g=m_conv_ln_g, conv_ln_b=m_conv_ln_b, conv_w_out=m_conv_w_out, conv_b_out=m_conv_b_out, w_ada_kv=m_w_ada_kv, b_ada_kv=m_b_ada_kv, g_kv=m_g_kv, w_dkv=m_w_dkv, g_ckv=m_g_ckv, w_kr=m_w_kr, w_uk=m_w_uk, w_uv=m_w_uv, w_dq=m_w_dq, g_cq=m_g_cq, w_uq=m_w_uq, w_o=m_w_o, mlp_w_up=m_mlp_w_up, mlp_w_down=m_mlp_w_down)
    v_in = dict(w_ada_mix=v_w_ada_mix, b_ada_mix=v_b_ada_mix, w_ada_mlp=v_w_ada_mlp, b_ada_mlp=v_b_ada_mlp, g_pre_mix=v_g_pre_mix, g_post_mix=v_g_post_mix, g_pre_mlp=v_g_pre_mlp, g_post_mlp=v_g_post_mlp, conv_w_in=v_conv_w_in, conv_b_in=v_conv_b_in, conv_dw=v_conv_dw, conv_dw_b=v_conv_dw_b, conv_ln_g=v_conv_ln_g, conv_ln_b=v_conv_ln_b, conv_w_out=v_conv_w_out, conv_b_out=v_conv_b_out, w_ada_kv=v_w_ada_kv, b_ada_kv=v_b_ada_kv, g_kv=v_g_kv, w_dkv=v_w_dkv, g_ckv=v_g_ckv, w_kr=v_w_kr, w_uk=v_w_uk, w_uv=v_w_uv, w_dq=v_w_dq, g_cq=v_g_cq, w_uq=v_w_uq, w_o=v_w_o, mlp_w_up=v_mlp_w_up, mlp_w_down=v_mlp_w_down)
    grads, last_group = _step_grads(x, c, positions, loss_target, weights)
    loss, grad_x = grads.pop("loss"), grads.pop("x")
    names = list(weights)
    upd = {n: _adamw(n, weights[n], grads[n], m_in[n], v_in[n]) for n in names if n in grads}
    grads.update(last_group([upd[n][0] for n in ("w_ada_mix", "w_ada_mlp", "w_ada_kv", "mlp_w_down")]))
    upd.update({n: _adamw(n, weights[n], grads[n], m_in[n], v_in[n]) for n in names if n not in upd})
    return (loss, grad_x, *[grads[n] for n in names], *[upd[n][0] for n in names], *[upd[n][1] for n in names],
            *[upd[n][2] for n in names])


def _step_grads(x, c, positions, loss_target, w):
    xi, yi, ci = lax.axis_index("x"), lax.axis_index("y"), lax.axis_index("c")
    chip = 2 * xi + yi
    dev = 2 * chip + ci
    place_idx = jnp.stack([ci, chip]).astype(jnp.int32)
    t, d = x.shape[1], x.shape[2]
    dl = d // N_CHIPS
    r_kv, r_q = w["w_dkv"].shape[1], w["w_dq"].shape[2]
    n_ada, n_kvada = w["w_ada_mix"].shape[2], w["w_ada_kv"].shape[1]
    heads_l = w["w_uq"].shape[2] // (QK_NOPE + QK_ROPE)
    assert dl == r_kv == r_q and dl % LANES == 0 and heads_l * N_CHIPS % 2 == 0

    def chip_cols(a, width):
        return lax.dynamic_slice_in_dim(a, chip * width, width, axis=a.ndim - 1)

    uq = w["w_uq"][0].reshape(r_q, heads_l, QK_NOPE + QK_ROPE)
    uq_nope, uq_rope = uq[:, :, :QK_NOPE].reshape(r_q, -1), uq[:, :, QK_NOPE:].reshape(r_q, -1)
    kr_pad = jnp.pad(w["w_kr"], ((0, 0), (0, LANES - QK_ROPE)))
    small = jnp.concatenate([w["w_dkv"], kr_pad, w["w_dq"][0], w["w_uk"], w["w_uv"], uq_nope, uq_rope], axis=1)
    widths = [r_kv, LANES, r_q, w["w_uk"].shape[1], w["w_uv"].shape[1], uq_nope.shape[1], uq_rope.shape[1]]
    so = [0]
    for wd in widths:
        so.append(so[-1] + wd)
    big = dict(conv_w_in=(w["conv_w_in"], 0), conv_w_out=(w["conv_w_out"], 0), w_o=(w["w_o"], 0),
               up0=(w["mlp_w_up"], 0), up1=(w["mlp_w_up"], 1), down0=(w["mlp_w_down"], 0),
               down1=(w["mlp_w_down"], 1), small=(small[None], 0))
    groups = dict(g0=["conv_w_in", "conv_w_out"], g1=["up0", "down0"], g2=["small", "w_o"], g3=["up1", "down1"])
    gathered, pending = {}, {}

    def start_gathers(gnames, after):
        token = after
        for gname in gnames:
            slots = [_cast_into_slot(n, *big[n], place_idx, after=token) for n in groups[gname]]
            s_sems, r_sems, bufs, token = _gather_start("gather_start_" + gname, slots, after=token)
            pending[gname] = (s_sems, r_sems, bufs)
        return token

    def finish_gather(gname, after):
        s_sems, r_sems, bufs = pending[gname]
        bufs = _gather_wait("gather_wait_" + gname, s_sems, r_sems, bufs, after)
        gathered.update(zip(groups[gname], _forward_to_sibling("forward_" + gname, bufs)))

    rowl = lambda a: a.reshape(-1, a.shape[2])
    w_up_f = lambda l: gathered["up%d" % l]
    w_down_f = lambda l: rowl(gathered["down%d" % l])

    pack_rows = [w["conv_dw"][0], w["conv_dw_b"], w["conv_ln_g"], w["conv_ln_b"], w["conv_b_out"],
                 w["conv_b_in"].reshape(2, dl), c.reshape(N_CHIPS, dl)]
    pack_rows = [jnp.pad(p, ((0, (-p.shape[0]) % 8), (0, 0))) for p in pack_rows]
    po = [0]
    for p in pack_rows:
        po.append(po[-1] + p.shape[0])
    packs = _all_gather_small("gather_params", jnp.concatenate(pack_rows, axis=0))
    by_chip = packs[0::2]
    start_gathers(("g0",), packs)

    def full_width(r0, nr):
        return jnp.transpose(by_chip[:, r0:r0 + nr, :], (1, 0, 2)).reshape(nr, d)

    dw_full, dwb_full = full_width(po[0], CONV_WIDTH), full_width(po[1], 1)
    lng_full, lnb_full, bout_full = full_width(po[2], 1), full_width(po[3], 1), full_width(po[4], 1)
    bin_full = by_chip[:, po[5]:po[5] + 2, :].reshape(1, 2 * d)
    c_all = packs[:, po[6]:po[6] + N_CHIPS, :].reshape(N_DEV, d)

    sc_all = _ew("silu_c", lambda a: (a * jax.nn.sigmoid(a),), [c_all], (F32,))[0]
    ada_w = [(w["w_ada_mix"], 0), (w["w_ada_mix"], 1), (w["w_ada_mlp"], 0), (w["w_ada_mlp"], 1),
             (w["w_ada_kv"][None], 0)]
    ada_b = [chip_cols(w["b_ada_mix"][0:1], n_ada), chip_cols(w["b_ada_mix"][1:2], n_ada),
             chip_cols(w["b_ada_mlp"][0:1], n_ada), chip_cols(w["b_ada_mlp"][1:2], n_ada),
             chip_cols(w["b_ada_kv"].reshape(1, -1), n_kvada)]
    mods = [_mm_nn("ada_fwd_%d" % i, sc_all, wi, "row", extras=(jnp.broadcast_to(bi, (N_DEV, bi.shape[1])),),
                   epilogue=_add_epilogue, w_layer=li) for i, ((wi, li), bi) in enumerate(zip(ada_w, ada_b))]
    mods_all = _all_gather_small("gather_mods", jnp.concatenate(mods, axis=1))
    started = start_gathers(("g1", "g2", "g3"), mods_all)[0:1, 0:1]
    mine = lax.dynamic_index_in_dim(mods_all[0::2], dev, axis=1, keepdims=False)
    offs = [0]
    for m_ in mods:
        offs.append(offs[-1] + m_.shape[1])
    mod_vec = [mine[:, offs[i]:offs[i + 1]].reshape(1, -1) for i in range(5)]
    split3 = lambda v: (v[:, :d], v[:, d:2 * d], v[:, 2 * d:])
    mix = [split3(mod_vec[0]), split3(mod_vec[1])]
    mlp = [split3(mod_vec[2]), split3(mod_vec[3])]
    kv_shift, kv_scale = mod_vec[4][:, :d], mod_vec[4][:, d:]

    cos_t, sin_t = _rope_tables(positions, t)
    vec = lambda a: a.reshape(1, -1)
    gpm, gqm = w["g_pre_mix"], w["g_post_mix"]
    gpl, gql = w["g_pre_mlp"], w["g_post_mlp"]
    h0 = x[0]
    after_token = lambda v, token: v + token[0:1, 0:1]

    def mlp_fwd(l, h):
        sh, sc, _ = mlp[l]
        (hn,) = _rw_fwd("mlp_pre_%d" % l, _f_pre, [h], [vec(gpl[l]), sh, sc], (BF,))
        s = _mm_nn("mlp_up_%d" % l, hn, w_up_f(l), "col", o_dtypes=(BF,), epilogue=_relu2_epilogue, tiles=WIDE)
        y = _mm_nn("mlp_down_%d" % l, s, w_down_f(l), "row", tiles=DEEP)
        return hn, s, y

    def post_fwd(name, h, y, gate, g, bias=None):
        if bias is None:
            return _rw_fwd(name, lambda h_, y_, gt, g_: (h_ + _f_post(y_, gt, g_)[0],), [h, y], [gate, g], (F32,))[0]
        return _rw_fwd(name, lambda h_, y_, b_, gt, g_: (h_ + _f_post_bias(y_, b_, gt, g_)[0],), [h, y],
                       [bias, gate, g], (F32,))[0]

    (hn0,) = _rw_fwd("conv_pre", _f_pre, [h0], [vec(gpm[0]), mix[0][0] + started, mix[0][1]], (BF,))
    finish_gather("g0", hn0)
    w_in_f, w_out_f = gathered["conv_w_in"], rowl(gathered["conv_w_out"])
    z0 = _mm_nn("conv_in", hn0, w_in_f, "col")
    (u0,) = _rw_fwd("conv_glu", _f_glu, [z0], [bin_full], (F32,))
    uc0 = _conv_fwd(u0, dw_full, dwb_full)
    (n0,) = _rw_fwd("conv_ln", _f_ln_silu, [uc0], [lng_full, lnb_full], (BF,))
    y0 = _mm_nn("conv_out", n0, w_out_f, "row")
    h1 = post_fwd("conv_post", h0, y0, mix[0][2], vec(gqm[0]), bias=bout_full)
    finish_gather("g1", h1)
    hn1, s1, y1 = mlp_fwd(0, h1)
    h2 = post_fwd("mlp_post_0", h1, y1, mlp[0][2], vec(gql[0]))

    finish_gather("g2", h2)
    gs = gathered["small"]
    w_dkvkr = rowl(gs[:, :, so[0]:so[2]])
    w_dq_f = rowl(gs[:, :, so[2]:so[3]])
    w_uk_f, w_uv_f = gs[:, :, so[3]:so[4]], gs[:, :, so[4]:so[5]]
    w_qn_f, w_qr_f = gs[:, :, so[5]:so[6]], gs[:, :, so[6]:so[7]]
    w_o_f = rowl(gathered["w_o"])
    kvn, hn2 = _rw_fwd("mla_pre", _f_pre2, [h2], [vec(w["g_kv"]), kv_shift, kv_scale, vec(gpm[1]), mix[1][0], mix[1][1]],
                       (BF, BF))
    pre_kv = _mm_nn("mla_dkv", kvn, w_dkvkr, "row")
    pre_q = _mm_nn("mla_dq", hn2, w_dq_f, "row")
    f_lat = _make_f_lat(r_kv)
    c_kv, kr, c_q = _rw_fwd("mla_latent", f_lat, [pre_kv, pre_q, cos_t, sin_t], [vec(w["g_ckv"]), vec(w["g_cq"][0])],
                            (BF, BF, BF))
    kn = _mm_nn("mla_uk", c_kv, w_uk_f, "col", o_dtypes=(BF,))
    vv = _mm_nn("mla_uv", c_kv, w_uv_f, "col", o_dtypes=(BF,))
    qn = _mm_nn("mla_uq_nope", c_q, w_qn_f, "col", o_dtypes=(BF,))
    qr_pre = _mm_nn("mla_uq_rope", c_q, w_qr_f, "col")
    (qr,) = _rw_fwd("mla_q_rope", _f_rope, [qr_pre, cos_t, sin_t], [], (BF,))
    att = _attn_fwd(qn, qr, kn, kr, vv)
    y2 = _mm_nn("mla_o", att, w_o_f, "row")
    h3 = post_fwd("mla_post", h2, y2, mix[1][2], vec(gqm[1]))
    finish_gather("g3", h3)
    hn3, s3, y3 = mlp_fwd(1, h3)
    h4 = post_fwd("mlp_post_1", h3, y3, mlp[1][2], vec(gql[1]))

    dh4, loss_part = _loss_grad(h4, loss_target[0])
    loss = lax.psum(loss_part[0, 0], ("x", "y", "c"))

    gw = {}
    gvec = {}

    dw_mm = functools.partial(_mm_tn, o_dtype=BF)

    def mlp_bwd(l, h_in, hn, s, y, dh, token=None):
        sh, sc, gate = mlp[l]
        if token is not None:
            gate = after_token(gate, token)
        (dy,), (dgate, dgq) = _rw_bwd("mlp_post_bwd_%d" % l, _f_post, [y], [gate, vec(gql[l])], [dh], [BF])
        gw["down%d" % l] = dw_mm("mlp_down_dw_%d" % l, s, dy, "row", tiles=(512, 2048, 2048))
        da = _mm_nt("mlp_down_dx_%d" % l, dy, w_down_f(l), "row", o_dtypes=(BF,), extras=(s,),
                    epilogue=_relu2_bwd_epilogue, tiles=WIDE)
        gw["up%d" % l] = dw_mm("mlp_up_dw_%d" % l, hn, da, "col", tiles=WIDE)
        dhn = _mm_nt("mlp_up_dx_%d" % l, da, w_up_f(l), "col", tiles=DEEP)
        (dh_in,), (dgp, dsh, dsc) = _rw_bwd("mlp_pre_bwd_%d" % l, _f_pre, [h_in], [vec(gpl[l]), sh, sc], [dhn], [F32],
                                            add_rows={0: dh})
        gvec["mlp%d" % l] = (dsh, dsc, dgate)
        gvec["g_pre_mlp%d" % l], gvec["g_post_mlp%d" % l] = dgp, dgq
        return dh_in

    chunked = lambda a: a.reshape(N_CHIPS, a.shape[0] // N_CHIPS, a.shape[1])
    to_chunks = lambda a: a if a.ndim == 3 else chunked(a)
    red = {}

    def swap_begin(tag, names):
        return _swap_start("swap_start_" + tag, [to_chunks(gw[n]) for n in names])

    def reduce_begin(tag, names, after=None, swapped=None):
        state, token = _reduce_scatter_begin(tag, [to_chunks(gw[n]) for n in names], place_idx, after, swapped)
        return (names, state), token

    def reduce_end(handle, after):
        names, state = handle
        red.update(zip(names, _reduce_scatter_end(state, after)))

    dh3 = mlp_bwd(1, h3, hn3, s3, y3, dh4)
    sw1, token = swap_begin("r1", ["up1", "down1"])

    (dy2,), (dgate, dgq) = _rw_bwd("mla_post_bwd", _f_post, [y2], [after_token(mix[1][2], token), vec(gqm[1])],
                                   [dh3], [BF])
    gvec["g_post_mix1"] = dgq
    gw["w_o"] = dw_mm("mla_o_dw", att, dy2, "row")
    datt = _mm_nt("mla_o_dx", dy2, w_o_f, "row", o_dtypes=(BF,))
    rs1, token = reduce_begin("r1", ["up1", "down1"], after=datt, swapped=sw1)
    dqn, dqr, dkn, dvv, dkr = _attn_bwd(qn, qr, kn, kr + token[0:1, 0:1].astype(BF), vv, datt)
    (dqr_pre,), _ = _rw_bwd("mla_q_rope_bwd", _f_rope, [qr_pre, cos_t, sin_t], [], [dqr], [BF, None, None])
    g_qn = dw_mm("mla_uq_nope_dw", c_q, dqn, "col")
    g_qr = dw_mm("mla_uq_rope_dw", c_q, dqr_pre, "col")
    dc_q = _mm_nt("mla_uq_nope_dx", dqn, w_qn_f, "col")
    dc_q = _mm_nt("mla_uq_rope_dx", dqr_pre, w_qr_f, "col", extras=(dc_q,), epilogue=_add_epilogue)
    g_uk = dw_mm("mla_uk_dw", c_kv, dkn, "col")
    g_uv = dw_mm("mla_uv_dw", c_kv, dvv, "col")
    dc_kv = _mm_nt("mla_uk_dx", dkn, w_uk_f, "col")
    dc_kv = _mm_nt("mla_uv_dx", dvv, w_uv_f, "col", extras=(dc_kv,), epilogue=_add_epilogue)
    (dpre_kv, dpre_q), (dg_ckv, dg_cq) = _rw_bwd(
        "mla_latent_bwd", f_lat, [pre_kv, pre_q, cos_t, sin_t], [vec(w["g_ckv"]), vec(w["g_cq"][0])],
        [dc_kv, dkr, dc_q], [BF, BF, None, None])
    gvec["g_ckv"], gvec["g_cq"] = dg_ckv, dg_cq
    g_dkvkr = dw_mm("mla_dkv_dw", kvn, dpre_kv, "row")
    g_dq = dw_mm("mla_dq_dw", hn2, dpre_q, "row")
    dkvn = _mm_nt("mla_dkv_dx", dpre_kv, w_dkvkr, "row")
    dhn2 = _mm_nt("mla_dq_dx", dpre_q, w_dq_f, "row")
    (dh2,), (dg_kv, dkvsh, dkvsc, dgp, dsh, dsc) = _rw_bwd(
        "mla_pre_bwd", _f_pre2, [h2], [vec(w["g_kv"]), kv_shift, kv_scale, vec(gpm[1]), mix[1][0], mix[1][1]],
        [dkvn, dhn2], [F32], add_rows={0: dh3})
    gvec["mix1"] = (dsh, dsc, dgate)
    gvec["kv"] = (dkvsh, dkvsc)
    gvec["g_kv"], gvec["g_pre_mix1"] = dg_kv, dgp
    gw["small"] = jnp.concatenate([chunked(g_dkvkr), chunked(g_dq), g_uk, g_uv, g_qn, g_qr], axis=2)
    reduce_end(rs1, dh2)
    rs2, token = reduce_begin("r2", ["small", "w_o"])

    dh1 = mlp_bwd(0, h1, hn1, s1, y1, dh2, token=token)
    reduce_end(rs2, dh1)
    sw3, token = swap_begin("r3", ["up0", "down0"])

    (dy0,), (dbout, dgate, dgq) = _rw_bwd("conv_post_bwd", _f_post_bias, [y0],
                                          [bout_full, after_token(mix[0][2], token), vec(gqm[0])], [dh1], [BF])
    gvec["g_post_mix0"] = dgq
    gw["conv_w_out"] = dw_mm("conv_out_dw", n0, dy0, "row")
    dn0 = _mm_nt("conv_out_dx", dy0, w_out_f, "row")
    rs3, token = reduce_begin("r3", ["up0", "down0"], after=dn0, swapped=sw3)
    (duc0,), (dlng, dlnb) = _rw_bwd("conv_ln_bwd", _f_ln_silu, [uc0], [after_token(lng_full, token), lnb_full],
                                    [dn0], [F32])
    du0, ddw = _conv_bwd(u0, duc0, dw_full)
    (dz0,), (dbin,) = _rw_bwd("conv_glu_bwd", _f_glu, [z0], [bin_full], [du0], [BF])
    gw["conv_w_in"] = dw_mm("conv_in_dw", hn0, dz0, "col")
    dhn0 = _mm_nt("conv_in_dx", dz0, w_in_f, "col")
    (dx,), (dgp, dsh, dsc) = _rw_bwd("conv_pre_bwd", _f_pre, [h0], [vec(gpm[0]), mix[0][0], mix[0][1]], [dhn0], [F32],
                                     add_rows={0: dh1})
    gvec["mix0"] = (dsh, dsc, dgate)
    gvec["g_pre_mix0"] = dgp
    reduce_end(rs3, dx)

    vec_list = [*gvec["mix0"], *gvec["mix1"], *gvec["mlp0"], *gvec["mlp1"], *gvec["kv"],
                gvec["g_pre_mix0"], gvec["g_pre_mix1"], gvec["g_post_mix0"], gvec["g_post_mix1"],
                gvec["g_pre_mlp0"], gvec["g_pre_mlp1"], gvec["g_post_mlp0"], gvec["g_post_mlp1"],
                gvec["g_kv"], gvec["g_ckv"], gvec["g_cq"], dbin, dlng, dlnb, dbout, ddw.reshape(1, -1)]
    vo = [0]
    for v_ in vec_list:
        vo.append(vo[-1] + v_.shape[1])
    vec_pad = (-vo[-1]) % (8 * LANES)
    n_vec = vo[-1] + vec_pad
    flat = jnp.concatenate(vec_list + [jnp.zeros((1, vec_pad), F32)], axis=1).reshape(8, n_vec // 8)
    all_vecs = _all_gather_small("gather_vector_grads", flat).reshape(N_DEV, 8, n_vec // 8)
    rs4, token = reduce_begin("r4", ["conv_w_in", "conv_w_out"], after=all_vecs)
    all_vecs = after_token(all_vecs, token)
    summed = _sum_devices(all_vecs).reshape(1, n_vec)
    per_dev = all_vecs.reshape(N_DEV, n_vec)
    seg = lambda a, i: a[:, vo[i]:vo[i + 1]]

    out = {"loss": loss, "x": dx.reshape(x.shape)}
    dm_mix = [jnp.concatenate([seg(per_dev, 3 * l + i) for i in range(3)], axis=1) for l in range(2)]
    dm_mlp = [jnp.concatenate([seg(per_dev, 6 + 3 * l + i) for i in range(3)], axis=1) for l in range(2)]
    dm_kv = jnp.concatenate([seg(per_dev, 12), seg(per_dev, 13)], axis=1)
    ada_dw = lambda name, dm, width: _mm_tn(name, sc_all, chip_cols(dm, width), "row")
    out["w_ada_mix"] = jnp.stack([ada_dw("ada_mix_dw_%d" % l, dm_mix[l], n_ada) for l in range(2)])
    out["w_ada_mlp"] = jnp.stack([ada_dw("ada_mlp_dw_%d" % l, dm_mlp[l], n_ada) for l in range(2)])
    out["w_ada_kv"] = ada_dw("ada_kv_dw", dm_kv, n_kvada)
    sum_seg = lambda i: seg(summed, i)
    out["b_ada_mix"] = jnp.concatenate([jnp.concatenate([sum_seg(3 * l + i) for i in range(3)], axis=1) for l in range(2)], axis=0)
    out["b_ada_mlp"] = jnp.concatenate([jnp.concatenate([sum_seg(6 + 3 * l + i) for i in range(3)], axis=1) for l in range(2)], axis=0)
    out["b_ada_kv"] = jnp.concatenate([sum_seg(12), sum_seg(13)], axis=1).reshape(-1)
    out["g_pre_mix"] = jnp.concatenate([sum_seg(14), sum_seg(15)], axis=0)
    out["g_post_mix"] = jnp.concatenate([sum_seg(16), sum_seg(17)], axis=0)
    out["g_pre_mlp"] = jnp.concatenate([sum_seg(18), sum_seg(19)], axis=0)
    out["g_post_mlp"] = jnp.concatenate([sum_seg(20), sum_seg(21)], axis=0)
    out["g_kv"] = sum_seg(22).reshape(-1)
    out["g_ckv"] = sum_seg(23).reshape(-1)
    out["g_cq"] = sum_seg(24)
    out["conv_b_in"] = chip_cols(sum_seg(25), 2 * dl)
    out["conv_ln_g"] = chip_cols(sum_seg(26), dl)
    out["conv_ln_b"] = chip_cols(sum_seg(27), dl)
    out["conv_b_out"] = chip_cols(sum_seg(28), dl)
    ddw_sum = chip_cols(sum_seg(29).reshape(CONV_WIDTH + 1, d), dl)
    out["conv_dw"] = ddw_sum[:CONV_WIDTH].reshape(1, CONV_WIDTH, dl)
    out["conv_dw_b"] = ddw_sum[CONV_WIDTH:]

    def last_group(after):
        reduce_end(rs4, after)
        return {"conv_w_in": red["conv_w_in"][None], "conv_w_out": red["conv_w_out"][None]}

    out["w_o"] = red["w_o"][None]
    out["mlp_w_up"] = jnp.stack([red["up0"], red["up1"]])
    out["mlp_w_down"] = jnp.stack([red["down0"], red["down1"]])
    rs = red["small"]
    piece = lambda i: rs[:, so[i]:so[i + 1]]
    out["w_dkv"] = piece(0)
    out["w_kr"] = piece(1)[:, :QK_ROPE]
    out["w_dq"] = piece(2)[None]
    out["w_uk"], out["w_uv"] = piece(3), piece(4)
    out["w_uq"] = jnp.concatenate([piece(5).reshape(r_q, heads_l, QK_NOPE), piece(6).reshape(r_q, heads_l, QK_ROPE)],
                                  axis=2).reshape(1, r_q, -1)
    return out, last_group
```

```python
import functools

import jax
import jax.numpy as jnp
from jax import lax
from jax.experimental import pallas as pl
from jax.experimental.pallas import tpu as pltpu

F32 = jnp.float32
BF = jnp.bfloat16
MXU_DTYPE = BF

EPS = 1e-6
NEG = -1e30
ROPE_THETA = 10000.0
QK_NOPE = 128
QK_ROPE = 64
V_HEAD = 128
CONV_WIDTH = 31
ADAM_LR, ADAM_B1, ADAM_B2, ADAM_EPS, ADAM_WD, ADAM_STEP = 0.001, 0.9, 0.999, 1e-08, 0.01, 10

N_CHIPS = 4
N_DEV = 8
LANES = 128
VMEM_LIMIT = 56 * 2 ** 20
MESH = pl.DeviceIdType.MESH
ANY = pl.BlockSpec(memory_space=pl.ANY)


def _params(sem=None):
    return pltpu.CompilerParams(dimension_semantics=sem, vmem_limit_bytes=VMEM_LIMIT)


def _tile(n, pref, unit=LANES):
    if n <= pref:
        return n
    t = (pref // unit) * unit
    while t > unit and n % t:
        t -= unit
    assert n % t == 0, (n, pref)
    return t


def _dg(a, b, ca, cb):
    return lax.dot_general(a.astype(MXU_DTYPE), b.astype(MXU_DTYPE), (((ca,), (cb,)), ((), ())),
                           preferred_element_type=F32)


@jax.custom_vjp
def _dot_nn(a, b):
    return _dg(a, b, 1, 0)


def _dot_nn_fwd(a, b):
    return _dg(a, b, 1, 0), (a, b)


def _dot_nn_bwd(res, g):
    a, b = res
    return _dg(g, b, 1, 1).astype(a.dtype), _dg(a, g, 0, 0).astype(b.dtype)


_dot_nn.defvjp(_dot_nn_fwd, _dot_nn_bwd)


@jax.custom_vjp
def _dot_nt(a, b):
    return _dg(a, b, 1, 1)


def _dot_nt_fwd(a, b):
    return _dg(a, b, 1, 1), (a, b)


def _dot_nt_bwd(res, g):
    a, b = res
    return _dg(g, b, 1, 0).astype(a.dtype), _dg(g, a, 0, 0).astype(b.dtype)


_dot_nt.defvjp(_dot_nt_fwd, _dot_nt_bwd)


def _matmul(name, a, b, *, ca, cb, grid, a_blk, a_map, b_blk, b_map, o_shape, o_dtypes, o_blk, o_map,
            extras=(), epilogue=None):
    nk = grid[-1]
    n_ex, n_out = len(extras), len(o_dtypes)

    def body(*refs):
        a_ref, b_ref = refs[0], refs[1]
        ex_refs = refs[2:2 + n_ex]
        out_refs = refs[2 + n_ex:2 + n_ex + n_out]
        kk = pl.program_id(len(grid) - 1)

        def finish(acc):
            outs = (acc,) if epilogue is None else epilogue(acc, *[r[...] for r in ex_refs])
            for r, o in zip(out_refs, outs):
                r[...] = o.astype(r.dtype)

        part = _dg(a_ref[...], b_ref[...], ca, cb)
        if nk == 1:
            finish(part)
        else:
            acc_ref = refs[-1]

            @pl.when(kk == 0)
            def _():
                acc_ref[...] = part

            @pl.when(kk > 0)
            def _():
                acc_ref[...] += part

            @pl.when(kk == nk - 1)
            def _():
                finish(acc_ref[...])

    o_spec = pl.BlockSpec(o_blk, o_map)
    acc_shape = tuple(d for d in o_blk if d is not None)
    res = pl.pallas_call(
        body, grid=grid,
        in_specs=[pl.BlockSpec(a_blk, a_map), pl.BlockSpec(b_blk, b_map)] + [o_spec] * n_ex,
        out_specs=[o_spec] * n_out,
        out_shape=[jax.ShapeDtypeStruct(o_shape, dt) for dt in o_dtypes],
        scratch_shapes=[] if nk == 1 else [pltpu.VMEM(acc_shape, F32)],
        compiler_params=_params(("parallel",) * (len(grid) - 1) + ("arbitrary",)),
        name=name,
    )(a, b, *extras)
    return res[0] if n_out == 1 else res


TM, TN, TK = 2048, 512, 2048


def _mm_nn(name, a, w, layout, o_dtypes=(F32,), extras=(), epilogue=None, tiles=(TM, TN, TK), w_layer=None):
    m, k = a.shape
    tm, tk = _tile(m, tiles[0], 8), _tile(k, tiles[2])
    if layout == "row":
        n = w.shape[-1]
        tn = _tile(n, tiles[1])
        grid = (m // tm, n // tn, k // tk)
        b_blk, b_map = (tk, tn), (lambda i, j, kk: (kk, j))
        if w_layer is not None:
            b_blk, b_map = (None, tk, tn), (lambda i, j, kk: (w_layer, kk, j))
    else:
        nl = w.shape[2]
        n = nl * w.shape[0]
        tn = _tile(nl, tiles[1])
        nb = nl // tn
        grid = (m // tm, n // tn, k // tk)
        b_blk, b_map = (None, tk, tn), (lambda i, j, kk: (j // nb, kk, j % nb))
    return _matmul(name, a, w, ca=1, cb=0, grid=grid, a_blk=(tm, tk), a_map=lambda i, j, kk: (i, kk),
                   b_blk=b_blk, b_map=b_map, o_shape=(m, n), o_dtypes=o_dtypes, o_blk=(tm, tn),
                   o_map=lambda i, j, kk: (i, j), extras=extras, epilogue=epilogue)


def _mm_nt(name, g, w, layout, o_dtypes=(F32,), extras=(), epilogue=None, tiles=(TM, TN, TK)):
    m, n = g.shape
    tm, tn = _tile(m, tiles[0], 8), None
    if layout == "row":
        k = w.shape[0]
        tn = _tile(n, tiles[2])
        tk = _tile(k, tiles[1])
        grid = (m // tm, k // tk, n // tn)
        b_blk, b_map = (tk, tn), (lambda i, j, kk: (j, kk))
    else:
        k, nl = w.shape[1], w.shape[2]
        tn = _tile(nl, tiles[2])
        nb = nl // tn
        tk = _tile(k, tiles[1])
        grid = (m // tm, k // tk, n // tn)
        b_blk, b_map = (None, tk, tn), (lambda i, j, kk: (kk // nb, j, kk % nb))
    return _matmul(name, g, w, ca=1, cb=1, grid=grid, a_blk=(tm, tn), a_map=lambda i, j, kk: (i, kk),
                   b_blk=b_blk, b_map=b_map, o_shape=(m, k), o_dtypes=o_dtypes, o_blk=(tm, tk),
                   o_map=lambda i, j, kk: (i, j), extras=extras, epilogue=epilogue)


def _mm_tn(name, x, g, layout, o_dtype=F32, tiles=(TM, TN, TK)):
    t, k = x.shape
    n = g.shape[1]
    tt = _tile(t, tiles[2], 8)
    tk = _tile(k, tiles[0])
    if layout == "row":
        tn = _tile(n, tiles[1])
        o_shape, o_blk, o_map = (k, n), (tk, tn), (lambda i, j, kk: (i, j))
    else:
        nl = n // N_CHIPS
        tn = _tile(nl, tiles[1])
        nb = nl // tn
        o_shape, o_blk, o_map = (N_CHIPS, k, nl), (None, tk, tn), (lambda i, j, kk: (j // nb, i, j % nb))
    grid = (k // tk, n // tn, t // tt)
    return _matmul(name, x, g, ca=0, cb=0, grid=grid, a_blk=(tt, tk), a_map=lambda i, j, kk: (kk, i),
                   b_blk=(tt, tn), b_map=lambda i, j, kk: (kk, j), o_shape=o_shape, o_dtypes=(o_dtype,),
                   o_blk=o_blk, o_map=o_map)


TR = 256


def _rw_fwd(name, fn, rows, vecs, o_dtypes, tr=TR):
    t = rows[0].shape[0]
    tr = min(tr, t)
    n_r, n_v = len(rows), len(vecs)
    o_sds = jax.eval_shape(fn, *[jax.ShapeDtypeStruct((tr, r.shape[1]), r.dtype) for r in rows],
                           *[jax.ShapeDtypeStruct(v.shape, v.dtype) for v in vecs])

    def body(*refs):
        outs = fn(*[r[...] for r in refs[:n_r + n_v]])
        for r, o in zip(refs[n_r + n_v:], outs):
            r[...] = o.astype(r.dtype)

    res = pl.pallas_call(
        body, grid=(t // tr,),
        in_specs=[pl.BlockSpec((tr, r.shape[1]), lambda i: (i, 0)) for r in rows]
        + [pl.BlockSpec(v.shape, lambda i: (0, 0)) for v in vecs],
        out_specs=[pl.BlockSpec((tr, o.shape[1]), lambda i: (i, 0)) for o in o_sds],
        out_shape=[jax.ShapeDtypeStruct((t, o.shape[1]), dt) for o, dt in zip(o_sds, o_dtypes)],
        compiler_params=_params(("parallel",)), name=name,
    )(*rows, *vecs)
    return res


def _rw_bwd(name, fn, rows, vecs, cots, row_grad_dtypes, add_rows=None, tr=TR):
    t = rows[0].shape[0]
    tr = min(tr, t)
    add_rows = add_rows or {}
    n_r, n_v, n_c = len(rows), len(vecs), len(cots)
    diff = [i for i, dt in enumerate(row_grad_dtypes) if dt is not None]
    adds = [add_rows[i] for i in diff if i in add_rows]
    n_a = len(adds)

    def body(*refs):
        rvals = [r[...] for r in refs[:n_r]]
        vvals = [r[...] for r in refs[n_r:n_r + n_v]]
        c_refs = refs[n_r + n_v:n_r + n_v + n_c]
        a_refs = list(refs[n_r + n_v + n_c:n_r + n_v + n_c + n_a])
        o_refs = refs[n_r + n_v + n_c + n_a:]

        def f(*d):
            full = list(rvals)
            for i, x in zip(diff, d[:len(diff)]):
                full[i] = x
            return fn(*full, *d[len(diff):])

        outs, vjp = jax.vjp(f, *[rvals[i] for i in diff], *vvals)
        grads = vjp(tuple(c[...].astype(o.dtype) for c, o in zip(c_refs, outs)))
        for n, i in enumerate(diff):
            gr = grads[n].astype(F32)
            if i in add_rows:
                gr = gr + a_refs.pop(0)[...].astype(F32)
            o_refs[n][...] = gr.astype(o_refs[n].dtype)
        first = pl.program_id(0) == 0
        for r, gv in zip(o_refs[len(diff):], grads[len(diff):]):
            @pl.when(first)
            def _(r=r, gv=gv):
                r[...] = gv

            @pl.when(jnp.logical_not(first))
            def _(r=r, gv=gv):
                r[...] += gv

    row_spec = lambda a: pl.BlockSpec((tr, a.shape[1]), lambda i: (i, 0))
    vec_spec = lambda a: pl.BlockSpec(a.shape, lambda i: (0, 0))
    res = pl.pallas_call(
        body, grid=(t // tr,),
        in_specs=[row_spec(r) for r in rows] + [vec_spec(v) for v in vecs] + [row_spec(c) for c in cots]
        + [row_spec(a) for a in adds],
        out_specs=[row_spec(rows[i]) for i in diff] + [vec_spec(v) for v in vecs],
        out_shape=[jax.ShapeDtypeStruct(rows[i].shape, row_grad_dtypes[i]) for i in diff]
        + [jax.ShapeDtypeStruct(v.shape, F32) for v in vecs],
        compiler_params=_params(("arbitrary",)), name=name,
    )(*rows, *vecs, *cots, *adds)
    return res[:len(diff)], res[len(diff):]


def _rms(x, g):
    return x * lax.rsqrt(jnp.mean(x * x, axis=-1, keepdims=True) + EPS) * g


def _f_pre(h, g, shift, scale):
    return (_rms(h, g) * (1.0 + scale) + shift,)


def _f_pre2(h, g1, sh1, sc1, g2, sh2, sc2):
    return _f_pre(h, g1, sh1, sc1) + _f_pre(h, g2, sh2, sc2)


def _f_post(y, gate, g):
    return (gate * _rms(y, g),)


def _f_post_bias(y, bias, gate, g):
    return (gate * _rms(y + bias, g),)


def _f_glu(z, bias):
    z = z + bias
    half = z.shape[1] // 2
    return (z[:, :half] * jax.nn.sigmoid(z[:, half:]),)


def _f_ln_silu(u, g, b):
    mu = jnp.mean(u, axis=-1, keepdims=True)
    var = jnp.mean(jnp.square(u - mu), axis=-1, keepdims=True)
    y = (u - mu) * lax.rsqrt(var + EPS) * g + b
    return (y * jax.nn.sigmoid(y),)


def _rope_raw(x, cos, sin):
    n = x.shape[1]
    reps = n // LANES
    if reps > 1:
        cos = jnp.concatenate([cos] * reps, axis=1)
        sin = jnp.concatenate([sin] * reps, axis=1)
    lane = lax.broadcasted_iota(jnp.int32, x.shape, 1)
    half = QK_ROPE // 2
    partner = jnp.where((lane % QK_ROPE) < half, pltpu.roll(x, n - half, 1), pltpu.roll(x, half, 1))
    return x * cos + partner * sin


@jax.custom_vjp
def _rope(x, cos, sin):
    return _rope_raw(x, cos, sin)


def _rope_fwd(x, cos, sin):
    return _rope_raw(x, cos, sin), (cos, sin)


def _rope_bwd(res, g):
    cos, sin = res
    return _rope_raw(g, cos, -sin), jnp.zeros_like(cos), jnp.zeros_like(sin)


_rope.defvjp(_rope_fwd, _rope_bwd)


def _make_f_lat(r_kv):
    def f(pre_kv, pre_q, cos, sin, g_ckv, g_cq):
        c_kv = _rms(pre_kv[:, :r_kv], g_ckv)
        kr = _rope(pre_kv[:, r_kv:], cos, sin)
        return c_kv, kr, _rms(pre_q, g_cq)
    return f


def _f_rope(x, cos, sin):
    return (_rope(x, cos, sin),)


CONV_TC = 128
CONV_TT = 256
PADR = 32


def _conv_fwd(u, dw, dw_b):
    t, d = u.shape
    tc, tt = min(CONV_TC, d), min(CONV_TT, t)
    off = PADR - (CONV_WIDTH - 1)

    def body(u_ref, w_ref, b_ref, o_ref, pad_ref):
        pad_ref[pl.ds(0, PADR), :] = jnp.zeros((PADR, tc), F32)
        pad_ref[pl.ds(PADR, t), :] = u_ref[...]
        for t0 in range(0, t, tt):
            acc = jnp.zeros((tt, tc), F32) + b_ref[...]
            for j in range(CONV_WIDTH):
                acc = acc + pad_ref[pl.ds(t0 + off + j, tt), :] * w_ref[pl.ds(j, 1), :]
            o_ref[pl.ds(t0, tt), :] = acc

    return pl.pallas_call(
        body, grid=(d // tc,),
        in_specs=[pl.BlockSpec((t, tc), lambda j: (0, j)), pl.BlockSpec((CONV_WIDTH, tc), lambda j: (0, j)),
                  pl.BlockSpec((1, tc), lambda j: (0, j))],
        out_specs=pl.BlockSpec((t, tc), lambda j: (0, j)),
        out_shape=jax.ShapeDtypeStruct((t, d), F32),
        scratch_shapes=[pltpu.VMEM((t + PADR, tc), F32)],
        compiler_params=_params(("parallel",)), name="conv_fwd",
    )(u, dw, dw_b)


def _conv_bwd(u, duc, dw):
    t, d = u.shape
    tc, tt = min(CONV_TC, d), min(CONV_TT, t)
    off = PADR - (CONV_WIDTH - 1)

    def body(u_ref, g_ref, w_ref, du_ref, dwt_ref, padu_ref, padg_ref):
        padu_ref[pl.ds(0, PADR), :] = jnp.zeros((PADR, tc), F32)
        padu_ref[pl.ds(PADR, t), :] = u_ref[...]
        padg_ref[pl.ds(t, PADR), :] = jnp.zeros((PADR, tc), F32)
        padg_ref[pl.ds(0, t), :] = g_ref[...]
        for t0 in range(0, t, tt):
            acc = jnp.zeros((tt, tc), F32)
            for j in range(CONV_WIDTH):
                acc = acc + padg_ref[pl.ds(t0 + (CONV_WIDTH - 1) - j, tt), :] * w_ref[pl.ds(j, 1), :]
            du_ref[pl.ds(t0, tt), :] = acc
        for j in range(CONV_WIDTH):
            acc = jnp.zeros((tt, tc), F32)
            for t0 in range(0, t, tt):
                acc = acc + g_ref[pl.ds(t0, tt), :] * padu_ref[pl.ds(t0 + off + j, tt), :]
            dwt_ref[pl.ds(j, 1), :] = jnp.sum(acc, axis=0, keepdims=True)
        acc = jnp.zeros((tt, tc), F32)
        for t0 in range(0, t, tt):
            acc = acc + g_ref[pl.ds(t0, tt), :]
        dwt_ref[pl.ds(CONV_WIDTH, 1), :] = jnp.sum(acc, axis=0, keepdims=True)

    col = lambda r: pl.BlockSpec((r, tc), lambda j: (0, j))
    return pl.pallas_call(
        body, grid=(d // tc,),
        in_specs=[col(t), col(t), col(CONV_WIDTH)],
        out_specs=[col(t), col(CONV_WIDTH + 1)],
        out_shape=[jax.ShapeDtypeStruct((t, d), F32), jax.ShapeDtypeStruct((CONV_WIDTH + 1, d), F32)],
        scratch_shapes=[pltpu.VMEM((t + PADR, tc), F32), pltpu.VMEM((t + PADR, tc), F32)],
        compiler_params=_params(("parallel",)), name="conv_bwd",
    )(u, duc, dw)


@jax.custom_vjp
def _swap_halves(x):
    return pltpu.roll(x, LANES // 2, 1)


_swap_halves.defvjp(lambda x: (pltpu.roll(x, LANES // 2, 1), None), lambda _, g: (pltpu.roll(g, LANES // 2, 1),))


def _attn_block(qn, qr, kn, kr, v, q0):
    scale = (QK_NOPE + QK_ROPE) ** -0.5
    lane = lax.broadcasted_iota(jnp.int32, kr.shape, 1)
    kr_a = kr * (lane < QK_ROPE).astype(kr.dtype)
    kr_b = _swap_halves(kr_a)
    outs = []
    for hh, kr_h in ((0, kr_a), (1, kr_b)):
        sl = slice(hh * QK_NOPE, (hh + 1) * QK_NOPE)
        s = (_dot_nt(qn[:, sl], kn[:, sl]) + _dot_nt(qr, kr_h)) * scale
        row = lax.broadcasted_iota(jnp.int32, s.shape, 0) + q0
        col = lax.broadcasted_iota(jnp.int32, s.shape, 1)
        s = jnp.where(col <= row, s, NEG)
        e = jnp.exp(s - jnp.max(s, axis=-1, keepdims=True))
        p = e / jnp.sum(e, axis=-1, keepdims=True)
        outs.append(_dot_nn(p, v[:, sl]))
    return jnp.concatenate(outs, axis=1)


def _attn_fwd(qn, qr, kn, kr, v, tq=512):
    t, w = qn.shape
    pairs = w // (2 * QK_NOPE)
    tq = min(tq, t)
    pw = 2 * QK_NOPE

    def body(qn_ref, qr_ref, kn_ref, kr_ref, v_ref, o_ref):
        for q0 in range(0, t, tq):
            l = q0 + tq
            o_ref[pl.ds(q0, tq), :] = _attn_block(
                qn_ref[pl.ds(q0, tq), :], qr_ref[pl.ds(q0, tq), :], kn_ref[pl.ds(0, l), :], kr_ref[pl.ds(0, l), :],
                v_ref[pl.ds(0, l), :], q0).astype(o_ref.dtype)

    pair = lambda wd: pl.BlockSpec((t, wd), lambda p: (0, p))
    return pl.pallas_call(
        body, grid=(pairs,),
        in_specs=[pair(pw), pair(LANES), pair(pw), pl.BlockSpec((t, LANES), lambda p: (0, 0)), pair(pw)],
        out_specs=pair(pw), out_shape=jax.ShapeDtypeStruct((t, w), BF),
        compiler_params=_params(("parallel",)), name="attn_fwd",
    )(qn, qr, kn, kr, v)


def _attn_bwd(qn, qr, kn, kr, v, do, tq=256):
    t, w = qn.shape
    pairs = w // (2 * QK_NOPE)
    tq = min(tq, t)
    pw = 2 * QK_NOPE

    def body(qn_ref, qr_ref, kn_ref, kr_ref, v_ref, do_ref, dqn_ref, dqr_ref, dkn_ref, dv_ref, dkr_ref,
             akn_ref, av_ref, akr_ref):
        akn_ref[...] = jnp.zeros_like(akn_ref)
        av_ref[...] = jnp.zeros_like(av_ref)
        akr_ref[...] = jnp.zeros_like(akr_ref)
        for q0 in range(0, t, tq):
            l = q0 + tq
            rows, keys = pl.ds(q0, tq), pl.ds(0, l)
            _, vjp = jax.vjp(functools.partial(_attn_block, q0=q0), qn_ref[rows, :], qr_ref[rows, :],
                             kn_ref[keys, :], kr_ref[keys, :], v_ref[keys, :])
            dqn, dqr, dkn, dkr, dv = vjp(do_ref[rows, :].astype(F32))
            dqn_ref[rows, :] = dqn.astype(dqn_ref.dtype)
            dqr_ref[rows, :] = dqr.astype(dqr_ref.dtype)
            akn_ref[keys, :] += dkn.astype(F32)
            av_ref[keys, :] += dv.astype(F32)
            akr_ref[keys, :] += dkr.astype(F32)
        dkn_ref[...] = akn_ref[...].astype(dkn_ref.dtype)
        dv_ref[...] = av_ref[...].astype(dv_ref.dtype)
        first = pl.program_id(0) == 0

        @pl.when(first)
        def _():
            dkr_ref[...] = akr_ref[...]

        @pl.when(jnp.logical_not(first))
        def _():
            dkr_ref[...] += akr_ref[...]

    pair = lambda wd: pl.BlockSpec((t, wd), lambda p: (0, p))
    shared = pl.BlockSpec((t, LANES), lambda p: (0, 0))
    sds = jax.ShapeDtypeStruct
    return pl.pallas_call(
        body, grid=(pairs,),
        in_specs=[pair(pw), pair(LANES), pair(pw), shared, pair(pw), pair(pw)],
        out_specs=[pair(pw), pair(LANES), pair(pw), pair(pw), shared],
        out_shape=[sds((t, w), BF), sds((t, pairs * LANES), F32), sds((t, w), BF), sds((t, w), BF), sds((t, LANES), F32)],
        scratch_shapes=[pltpu.VMEM((t, pw), F32), pltpu.VMEM((t, pw), F32), pltpu.VMEM((t, LANES), F32)],
        compiler_params=_params(("arbitrary",)), name="attn_bwd",
    )(qn, qr, kn, kr, v, do)


def _ew(name, fn, ins, o_dtypes, max_bytes=2 ** 21):
    r, c = ins[0].shape
    tr = r
    if r * c * 4 > max_bytes:
        tr = max(16, (max_bytes // (c * 4)) // 16 * 16)
        while r % tr:
            tr -= 16
    n_in = len(ins)

    def body(*refs):
        outs = fn(*[x[...] for x in refs[:n_in]])
        for o_ref, o in zip(refs[n_in:], outs):
            o_ref[...] = o.astype(o_ref.dtype)

    spec = pl.BlockSpec((tr, c), lambda i: (i, 0))
    return pl.pallas_call(
        body, grid=(r // tr,), in_specs=[spec] * n_in, out_specs=[spec] * len(o_dtypes),
        out_shape=[jax.ShapeDtypeStruct((r, c), dt) for dt in o_dtypes],
        compiler_params=_params(("parallel",)), name=name,
    )(*ins)


def _adamw_math(w, g, m, v):
    m = ADAM_B1 * m + (1.0 - ADAM_B1) * g
    v = ADAM_B2 * v + (1.0 - ADAM_B2) * jnp.square(g)
    m_hat = m / (1.0 - ADAM_B1 ** ADAM_STEP)
    v_hat = v / (1.0 - ADAM_B2 ** ADAM_STEP)
    delta = -ADAM_LR * (m_hat / (jnp.sqrt(v_hat) + ADAM_EPS) + ADAM_WD * w)
    return delta, m, v


def _adamw(name, w, g, m, v):
    shape = w.shape
    to2 = lambda a: a.reshape(-1, shape[-1]) if a.ndim > 1 else a.reshape(1, -1)
    d, nm, nv = _ew("adamw_" + name, _adamw_math, [to2(w), to2(g), to2(m), to2(v)], (F32, F32, F32))
    return d.reshape(shape), nm.reshape(shape), nv.reshape(shape)


def _rope_tables(positions, t):
    half = QK_ROPE // 2
    inv = 1.0 / (ROPE_THETA ** (jnp.arange(0, QK_ROPE, 2, dtype=F32) / QK_ROPE))
    inv_l = jnp.tile(inv, LANES // half).reshape(1, LANES)
    sign = jnp.tile(jnp.concatenate([-jnp.ones((half,), F32), jnp.ones((half,), F32)]), LANES // QK_ROPE).reshape(1, LANES)

    def body(p_ref, inv_ref, sg_ref, c_ref, s_ref):
        ang = p_ref[...].astype(F32) * inv_ref[...]
        c_ref[...] = jnp.cos(ang)
        s_ref[...] = jnp.sin(ang) * sg_ref[...]

    return pl.pallas_call(body, out_shape=[jax.ShapeDtypeStruct((t, LANES), F32)] * 2, name="rope_tables")(
        positions.reshape(t, 1), inv_l, sign)


def _loss_grad(h, target):
    t, d = h.shape
    tr = min(TR, t)

    def body(h_ref, y_ref, dh_ref, l_ref):
        err = h_ref[...] - y_ref[...]
        dh_ref[...] = err * (1.0 / d)
        part = 0.5 * jnp.sum(jnp.mean(jnp.square(err), axis=-1, keepdims=True), axis=0, keepdims=True)
        first = pl.program_id(0) == 0

        @pl.when(first)
        def _():
            l_ref[...] = part

        @pl.when(jnp.logical_not(first))
        def _():
            l_ref[...] += part

    row = pl.BlockSpec((tr, d), lambda i: (i, 0))
    return pl.pallas_call(
        body, grid=(t // tr,), in_specs=[row, row],
        out_specs=[row, pl.BlockSpec((1, 1), lambda i: (0, 0))],
        out_shape=[jax.ShapeDtypeStruct((t, d), F32), jax.ShapeDtypeStruct((1, 1), F32)],
        compiler_params=_params(("arbitrary",)), name="loss_grad",
    )(h, target)


def _sum_devices(g):
    def fn(*parts):
        acc = parts[0]
        for p in parts[1:]:
            acc = acc + p
        return (acc,)
    return _ew("sum_devices", fn, [g[i] for i in range(g.shape[0])], (F32,))[0]


def _place():
    x, y, c = lax.axis_index("x"), lax.axis_index("y"), lax.axis_index("c")
    return x, y, c, [(1 - x, y), (x, 1 - y), (1 - x, 1 - y)]


def _all_gather_small(name, v):
    r, n = v.shape

    def body(x_ref, out_ref, send_sems, recv_sems, local_sem):
        x, y, c, chips = _place()
        me, sibling = (x, y, c), (x, y, 1 - c)

        def rows(px, py, pc):
            return out_ref.at[4 * px + 2 * py + pc]

        def copy(k, block, to, src=None):
            return pltpu.make_async_remote_copy(
                src_ref=rows(*block) if src is None else src, dst_ref=rows(*block), send_sem=send_sems.at[k],
                recv_sem=recv_sems.at[k], device_id=to, device_id_type=MESH)

        mine = pltpu.make_async_copy(x_ref, rows(*me), local_sem)
        mine.start()
        first = [copy(0, me, sibling, src=x_ref)]
        first += [copy(1 + j, me, (*chip, c), src=x_ref) for j, chip in enumerate(chips)]
        for cp in first:
            cp.start()
        passed = [copy(4 + j, (*chip, c), sibling) for j, chip in enumerate(chips)]
        for j, chip in enumerate(chips):
            copy(1 + j, (*chip, c), me).wait_recv()
            passed[j].start()
        copy(0, sibling, me).wait_recv()
        for j, chip in enumerate(chips):
            copy(4 + j, (*chip, 1 - c), me).wait_recv()
        for cp in first + passed:
            cp.wait_send()
        mine.wait()

    return pl.pallas_call(
        body, out_shape=jax.ShapeDtypeStruct((N_DEV, r, n), v.dtype),
        in_specs=[pl.BlockSpec(memory_space=pltpu.VMEM)], out_specs=pl.BlockSpec(memory_space=pltpu.VMEM),
        scratch_shapes=[pltpu.SemaphoreType.DMA((7,)), pltpu.SemaphoreType.DMA((7,)), pltpu.SemaphoreType.DMA],
        compiler_params=pltpu.CompilerParams(vmem_limit_bytes=VMEM_LIMIT), name=name,
    )(v)


def _cast_into_slot(name, w, layer, idx, after=None):
    _, r, n = w.shape
    tr = r
    while tr * n * 4 > 2 ** 21 and tr % 32 == 0:
        tr //= 2
    order = [] if after is None else [after]

    def body(idx_ref, w_ref, *rest):
        o_ref = rest[-1]
        o_ref[...] = w_ref[...].astype(o_ref.dtype)

    return pl.pallas_call(
        body,
        grid_spec=pltpu.PrefetchScalarGridSpec(
            num_scalar_prefetch=1, grid=(r // tr,),
            in_specs=[pl.BlockSpec((None, tr, n), lambda i, idx_ref: (layer, i, 0))] + [ANY] * len(order),
            out_specs=pl.BlockSpec((None, tr, n), lambda i, idx_ref: (idx_ref[1], i, 0))),
        out_shape=jax.ShapeDtypeStruct((N_CHIPS, r, n), BF),
        compiler_params=_params(("parallel",)), name="cast_" + name,
    )(idx, w, *order)


def _gather_weights(bufs):
    n_w = len(bufs)

    def body(*refs):
        outs = refs[n_w:2 * n_w]
        send_sems, recv_sems = refs[2 * n_w:]
        x, y, c, chips = _place()

        def copy(i, s, chip, which, to):
            h = outs[i].shape[1] // 2
            blk = outs[i].at[2 * chip[0] + chip[1], pl.ds(which * h, h)]
            return pltpu.make_async_remote_copy(
                src_ref=blk, dst_ref=blk, send_sem=send_sems.at[6 * i + s], recv_sem=recv_sems.at[6 * i + s],
                device_id=to, device_id_type=MESH)

        sends = []
        for i in range(n_w):
            for j, chip in enumerate(chips):
                cp = copy(i, j, (x, y), c, (*chip, c))
                cp.start()
                sends.append(cp)
        for i in range(n_w):
            for j, chip in enumerate(chips):
                copy(i, j, chip, c, (x, y, c)).wait_recv()
                cp = copy(i, 3 + j, chip, c, (x, y, 1 - c))
                cp.start()
                sends.append(cp)
        for i in range(n_w):
            for j, chip in enumerate(chips):
                copy(i, 3 + j, chip, 1 - c, (x, y, c)).wait_recv()
        for cp in sends:
            cp.wait_send()

    return pl.pallas_call(
        body, in_specs=[ANY] * n_w, out_specs=[ANY] * n_w,
        out_shape=[jax.ShapeDtypeStruct(b.shape, b.dtype) for b in bufs],
        input_output_aliases={i: i for i in range(n_w)},
        scratch_shapes=[pltpu.SemaphoreType.DMA((6 * n_w,)), pltpu.SemaphoreType.DMA((6 * n_w,))],
        name="gather_weights",
    )(*bufs)


def _swap_other_half(name, grads, after=None):
    n_w = len(grads)
    order = [] if after is None else [after]
    n_in = n_w + len(order)

    def body(*refs):
        ins, outs = refs[:n_w], refs[n_in:n_in + n_w]
        send_sems, recv_sems = refs[n_in + n_w:]
        x, y, c, _ = _place()
        cps = []
        for i in range(n_w):
            h = ins[i].shape[1] // 2
            cp = pltpu.make_async_remote_copy(
                src_ref=ins[i].at[:, pl.ds((1 - c) * h, h), :], dst_ref=outs[i], send_sem=send_sems.at[i],
                recv_sem=recv_sems.at[i], device_id=(x, y, 1 - c), device_id_type=MESH)
            cp.start()
            cps.append(cp)
        for cp in cps:
            cp.wait()

    return pl.pallas_call(
        body, in_specs=[ANY] * n_in, out_specs=[ANY] * n_w,
        out_shape=[jax.ShapeDtypeStruct((g.shape[0], g.shape[1] // 2, g.shape[2]), g.dtype) for g in grads],
        scratch_shapes=[pltpu.SemaphoreType.DMA((n_w,)), pltpu.SemaphoreType.DMA((n_w,))],
        name=name,
    )(*grads, *order)


def _add_my_half(name, g, s, idx, o_dtype):
    nc, r, n = g.shape
    h = r // 2
    tr = h
    while tr * n * 4 > 2 ** 21 and tr % 32 == 0:
        tr //= 2
    nb = h // tr

    def body(idx_ref, g_ref, s_ref, o_ref, own_ref):
        val = (g_ref[...].astype(F32) + s_ref[...].astype(F32)).astype(o_ref.dtype)
        o_ref[...] = val

        @pl.when(pl.program_id(1) == idx_ref[1])
        def _():
            own_ref[...] = val

    return pl.pallas_call(
        body,
        grid_spec=pltpu.PrefetchScalarGridSpec(
            num_scalar_prefetch=1, grid=(nb, nc),
            in_specs=[pl.BlockSpec((None, tr, n), lambda i, j, idx_ref: (j, idx_ref[0] * nb + i, 0)),
                      pl.BlockSpec((None, tr, n), lambda i, j, idx_ref: (j, i, 0))],
            out_specs=[pl.BlockSpec((None, tr, n), lambda i, j, idx_ref: (j, i, 0)),
                       pl.BlockSpec((None, tr, n), lambda i, j, idx_ref: (idx_ref[1], i, 0))]),
        out_shape=[jax.ShapeDtypeStruct((nc, h, n), o_dtype)] * 2,
        compiler_params=_params(("parallel", "arbitrary")), name=name,
    )(idx, g, s)


def _scatter_partials(parts, bufs):
    n_w = len(parts)

    def body(*refs):
        ins, outs = refs[:n_w], refs[2 * n_w:3 * n_w]
        send_sems, recv_sems = refs[3 * n_w:]
        x, y, c, chips = _place()
        k = 2 * x + y
        sends = []
        for i in range(n_w):
            for j, chip in enumerate(chips):
                cp = pltpu.make_async_remote_copy(
                    src_ref=ins[i].at[2 * chip[0] + chip[1]], dst_ref=outs[i].at[k], send_sem=send_sems.at[3 * i + j],
                    recv_sem=recv_sems.at[3 * i + j], device_id=(*chip, c), device_id_type=MESH)
                cp.start()
                sends.append(cp)
        for i in range(n_w):
            for j, chip in enumerate(chips):
                blk = outs[i].at[2 * chip[0] + chip[1]]
                pltpu.make_async_remote_copy(
                    src_ref=blk, dst_ref=blk, send_sem=send_sems.at[3 * i + j], recv_sem=recv_sems.at[3 * i + j],
                    device_id=(*chip, c), device_id_type=MESH).wait_recv()
        for cp in sends:
            cp.wait_send()

    return pl.pallas_call(
        body, in_specs=[ANY] * (2 * n_w), out_specs=[ANY] * n_w,
        out_shape=[jax.ShapeDtypeStruct(b.shape, b.dtype) for b in bufs],
        input_output_aliases={n_w + i: i for i in range(n_w)},
        scratch_shapes=[pltpu.SemaphoreType.DMA((3 * n_w,)), pltpu.SemaphoreType.DMA((3 * n_w,))],
        name="scatter_partials",
    )(*parts, *bufs)


def _sum_chips(name, q, idx):
    nc, h, n = q.shape
    tr = h
    while tr * n * 4 > 2 ** 20 and tr % 32 == 0:
        tr //= 2
    nb = h // tr

    def body(idx_ref, q_ref, o_ref):
        acc = q_ref[0].astype(F32)
        for j in range(1, nc):
            acc = acc + q_ref[j].astype(F32)
        o_ref[...] = acc

    return pl.pallas_call(
        body,
        grid_spec=pltpu.PrefetchScalarGridSpec(
            num_scalar_prefetch=1, grid=(nb,),
            in_specs=[pl.BlockSpec((nc, tr, n), lambda i, idx_ref: (0, i, 0))],
            out_specs=pl.BlockSpec((tr, n), lambda i, idx_ref: (idx_ref[0] * nb + i, 0))),
        out_shape=jax.ShapeDtypeStruct((2 * h, n), F32),
        compiler_params=_params(("parallel",)), name=name,
    )(idx, q)


def _join_halves(name, bufs):
    n_w = len(bufs)

    def body(*refs):
        outs = refs[n_w:2 * n_w]
        send_sems, recv_sems = refs[2 * n_w:]
        x, y, c, _ = _place()

        def copy(i, which):
            h = outs[i].shape[0] // 2
            rows = outs[i].at[pl.ds(which * h, h)]
            return pltpu.make_async_remote_copy(
                src_ref=rows, dst_ref=rows, send_sem=send_sems.at[i], recv_sem=recv_sems.at[i],
                device_id=(x, y, 1 - c), device_id_type=MESH)

        cps = [copy(i, c) for i in range(n_w)]
        for cp in cps:
            cp.start()
        for i, cp in enumerate(cps):
            cp.wait_send()
            copy(i, 1 - c).wait_recv()

    return pl.pallas_call(
        body, in_specs=[ANY] * n_w, out_specs=[ANY] * n_w,
        out_shape=[jax.ShapeDtypeStruct(b.shape, b.dtype) for b in bufs],
        input_output_aliases={i: i for i in range(n_w)},
        scratch_shapes=[pltpu.SemaphoreType.DMA((n_w,)), pltpu.SemaphoreType.DMA((n_w,))],
        name=name,
    )(*bufs)


HBM_SPEC = pl.BlockSpec(memory_space=pltpu.HBM)
SEM_SPEC = pl.BlockSpec(memory_space=pltpu.SEMAPHORE)
VMEM_SPEC = pl.BlockSpec(memory_space=pltpu.VMEM)
SIDE_EFFECT = pltpu.SideEffectType.DATAFLOW_SIDE_EFFECTING


def _in_hbm(arrays):
    return [pltpu.with_memory_space_constraint(a, pltpu.HBM) for a in arrays]


def _ici_gather_copy(ref, i, j, chip, send_sems, recv_sems):
    x, y, c, _ = _place()
    h = ref.shape[1] // 2
    mine = ref.at[2 * x + y, pl.ds(c * h, h)]
    return pltpu.make_async_remote_copy(
        src_ref=mine, dst_ref=mine, send_sem=send_sems.at[3 * i + j], recv_sem=recv_sems.at[3 * i + j],
        device_id=(*chip, c), device_id_type=MESH)


def _gather_start(name, bufs, after=None):
    n_w = len(bufs)
    order = [] if after is None else [after]

    def body(*refs):
        ins, token = refs[:n_w], refs[-1]
        send_sems, recv_sems = refs[n_w + len(order)], refs[n_w + len(order) + 1]
        chips = _place()[3]
        for i in range(n_w):
            for j, chip in enumerate(chips):
                _ici_gather_copy(ins[i], i, j, chip, send_sems, recv_sems).start()
        token[...] = jnp.zeros_like(token)

    res = pl.pallas_call(
        body, name=name, in_specs=[HBM_SPEC] * n_w + [ANY] * len(order),
        out_shape=(pltpu.SemaphoreType.DMA((3 * n_w,)), pltpu.SemaphoreType.DMA((3 * n_w,)),
                   *[pltpu.HBM(b.shape, b.dtype) for b in bufs], jax.ShapeDtypeStruct((8, LANES), F32)),
        out_specs=(SEM_SPEC, SEM_SPEC, *[HBM_SPEC] * n_w, VMEM_SPEC),
        input_output_aliases={i: 2 + i for i in range(n_w)},
        compiler_params=pltpu.CompilerParams(has_side_effects=SIDE_EFFECT),
    )(*_in_hbm(bufs), *order)
    return res[0], res[1], list(res[2:2 + n_w]), res[-1]


def _gather_wait(name, send_sems, recv_sems, bufs, after):
    n_w = len(bufs)

    def body(*refs):
        ins, s_sems, r_sems = refs[:n_w], refs[n_w], refs[n_w + 1]
        chips = _place()[3]
        for i in range(n_w):
            for j, chip in enumerate(chips):
                cp = _ici_gather_copy(ins[i], i, j, chip, s_sems, r_sems)
                cp.wait_send()
                cp.wait_recv()

    return pl.pallas_call(
        body, name=name, in_specs=[HBM_SPEC] * n_w + [SEM_SPEC, SEM_SPEC, ANY],
        out_shape=[pltpu.HBM(b.shape, b.dtype) for b in bufs], out_specs=[HBM_SPEC] * n_w,
        input_output_aliases={i: i for i in range(n_w)},
        compiler_params=pltpu.CompilerParams(has_side_effects=SIDE_EFFECT),
    )(*bufs, send_sems, recv_sems, after)


def _forward_to_sibling(name, bufs):
    n_w = len(bufs)

    def body(*refs):
        outs = refs[n_w:2 * n_w]
        send_sems, recv_sems = refs[2 * n_w:]
        x, y, c, chips = _place()

        def copy(i, j, chip, which):
            h = outs[i].shape[1] // 2
            blk = outs[i].at[2 * chip[0] + chip[1], pl.ds(which * h, h)]
            return pltpu.make_async_remote_copy(
                src_ref=blk, dst_ref=blk, send_sem=send_sems.at[3 * i + j], recv_sem=recv_sems.at[3 * i + j],
                device_id=(x, y, 1 - c), device_id_type=MESH)

        sends = [copy(i, j, chip, c) for i in range(n_w) for j, chip in enumerate(chips)]
        for cp in sends:
            cp.start()
        for i in range(n_w):
            for j, chip in enumerate(chips):
                copy(i, j, chip, 1 - c).wait_recv()
        for cp in sends:
            cp.wait_send()

    return pl.pallas_call(
        body, in_specs=[ANY] * n_w, out_specs=[ANY] * n_w,
        out_shape=[jax.ShapeDtypeStruct(b.shape, b.dtype) for b in bufs],
        input_output_aliases={i: i for i in range(n_w)},
        scratch_shapes=[pltpu.SemaphoreType.DMA((3 * n_w,)), pltpu.SemaphoreType.DMA((3 * n_w,))],
        name=name,
    )(*bufs)


def _ici_scatter_copy(part, buf, i, j, chip, send_sems, recv_sems):
    x, y, c, _ = _place()
    return pltpu.make_async_remote_copy(
        src_ref=part.at[2 * chip[0] + chip[1]], dst_ref=buf.at[2 * x + y], send_sem=send_sems.at[3 * i + j],
        recv_sem=recv_sems.at[3 * i + j], device_id=(*chip, c), device_id_type=MESH)


def _ici_scatter_landing(buf, i, j, chip, send_sems, recv_sems):
    x, y, c, _ = _place()
    blk = buf.at[2 * chip[0] + chip[1]]
    return pltpu.make_async_remote_copy(
        src_ref=blk, dst_ref=blk, send_sem=send_sems.at[3 * i + j], recv_sem=recv_sems.at[3 * i + j],
        device_id=(*chip, c), device_id_type=MESH)


def _scatter_start(name, parts, bufs):
    n_w = len(parts)

    def body(*refs):
        ps, bs = refs[:n_w], refs[n_w:2 * n_w]
        send_sems, recv_sems, token = refs[2 * n_w], refs[2 * n_w + 1], refs[-1]
        chips = _place()[3]
        for i in range(n_w):
            for j, chip in enumerate(chips):
                _ici_scatter_copy(ps[i], bs[i], i, j, chip, send_sems, recv_sems).start()
        token[...] = jnp.zeros_like(token)

    both = list(parts) + list(bufs)
    res = pl.pallas_call(
        body, name=name, in_specs=[HBM_SPEC] * (2 * n_w),
        out_shape=(pltpu.SemaphoreType.DMA((3 * n_w,)), pltpu.SemaphoreType.DMA((3 * n_w,)),
                   *[pltpu.HBM(b.shape, b.dtype) for b in both], jax.ShapeDtypeStruct((8, LANES), F32)),
        out_specs=(SEM_SPEC, SEM_SPEC, *[HBM_SPEC] * (2 * n_w), VMEM_SPEC),
        input_output_aliases={i: 2 + i for i in range(2 * n_w)},
        compiler_params=pltpu.CompilerParams(has_side_effects=SIDE_EFFECT),
    )(*_in_hbm(both))
    return res[0], res[1], list(res[2:2 + n_w]), list(res[2 + n_w:2 + 2 * n_w]), res[-1]


def _scatter_wait(name, send_sems, recv_sems, parts, bufs, after):
    n_w = len(parts)
    after = list(after) if isinstance(after, (list, tuple)) else [after]

    def body(*refs):
        ps, bs = refs[:n_w], refs[n_w:2 * n_w]
        s_sems, r_sems = refs[2 * n_w], refs[2 * n_w + 1]
        chips = _place()[3]
        for i in range(n_w):
            for j, chip in enumerate(chips):
                _ici_scatter_copy(ps[i], bs[i], i, j, chip, s_sems, r_sems).wait_send()
                _ici_scatter_landing(bs[i], i, j, chip, s_sems, r_sems).wait_recv()

    both = list(parts) + list(bufs)
    res = pl.pallas_call(
        body, name=name, in_specs=[HBM_SPEC] * (2 * n_w) + [SEM_SPEC, SEM_SPEC] + [ANY] * len(after),
        out_shape=[pltpu.HBM(b.shape, b.dtype) for b in both], out_specs=[HBM_SPEC] * (2 * n_w),
        input_output_aliases={i: i for i in range(2 * n_w)},
        compiler_params=pltpu.CompilerParams(has_side_effects=SIDE_EFFECT),
    )(*both, send_sems, recv_sems, *after)
    return list(res[n_w:])


def _swap_copy(g, land, i, send_sems, recv_sems):
    x, y, c, _ = _place()
    h = g.shape[1] // 2
    return pltpu.make_async_remote_copy(
        src_ref=g.at[:, pl.ds((1 - c) * h, h), :], dst_ref=land, send_sem=send_sems.at[i], recv_sem=recv_sems.at[i],
        device_id=(x, y, 1 - c), device_id_type=MESH)


def _swap_start(name, grads):
    n_w = len(grads)
    lands = [lax.empty((g.shape[0], g.shape[1] // 2, g.shape[2]), g.dtype) for g in grads]

    def body(*refs):
        gs, ls = refs[:n_w], refs[n_w:2 * n_w]
        send_sems, recv_sems, token = refs[2 * n_w], refs[2 * n_w + 1], refs[-1]
        for i in range(n_w):
            _swap_copy(gs[i], ls[i], i, send_sems, recv_sems).start()
        token[...] = jnp.zeros_like(token)

    both = list(grads) + lands
    res = pl.pallas_call(
        body, name=name, in_specs=[HBM_SPEC] * (2 * n_w),
        out_shape=(pltpu.SemaphoreType.DMA((n_w,)), pltpu.SemaphoreType.DMA((n_w,)),
                   *[pltpu.HBM(b.shape, b.dtype) for b in both], jax.ShapeDtypeStruct((8, LANES), F32)),
        out_specs=(SEM_SPEC, SEM_SPEC, *[HBM_SPEC] * (2 * n_w), VMEM_SPEC),
        input_output_aliases={i: 2 + i for i in range(2 * n_w)},
        compiler_params=pltpu.CompilerParams(has_side_effects=SIDE_EFFECT),
    )(*_in_hbm(both))
    return (res[0], res[1], list(res[2:2 + n_w]), list(res[2 + n_w:2 + 2 * n_w])), res[-1]


def _swap_wait(name, swapped, after):
    send_sems, recv_sems, grads, lands = swapped
    n_w = len(grads)

    def body(*refs):
        gs, ls = refs[:n_w], refs[n_w:2 * n_w]
        s_sems, r_sems = refs[2 * n_w], refs[2 * n_w + 1]
        for i in range(n_w):
            cp = _swap_copy(gs[i], ls[i], i, s_sems, r_sems)
            cp.wait_send()
            cp.wait_recv()

    both = list(grads) + list(lands)
    res = pl.pallas_call(
        body, name=name, in_specs=[HBM_SPEC] * (2 * n_w) + [SEM_SPEC, SEM_SPEC, ANY],
        out_shape=[pltpu.HBM(b.shape, b.dtype) for b in both], out_specs=[HBM_SPEC] * (2 * n_w),
        input_output_aliases={i: i for i in range(2 * n_w)},
        compiler_params=pltpu.CompilerParams(has_side_effects=SIDE_EFFECT),
    )(*both, send_sems, recv_sems, after)
    return list(res[:n_w]), list(res[n_w:])


def _reduce_scatter_begin(tag, grads, idx, after=None, swapped=None):
    if swapped is None:
        from_sibling = _swap_other_half("swap_other_half_" + tag, grads, after)
    else:
        grads, from_sibling = _swap_wait("swap_wait_" + tag, swapped, after)
    pairs = [_add_my_half("add_my_half_%s_%d" % (tag, i), g, s, idx, BF)
             for i, (g, s) in enumerate(zip(grads, from_sibling))]
    s_sems, r_sems, parts, bufs, token = _scatter_start("scatter_start_" + tag, [p[0] for p in pairs],
                                                        [p[1] for p in pairs])
    return (tag, s_sems, r_sems, parts, bufs, idx), token


def _reduce_scatter_end(state, after):
    tag, s_sems, r_sems, parts, bufs, idx = state
    received = _scatter_wait("scatter_wait_" + tag, s_sems, r_sems, parts, bufs, after)
    halves = [_sum_chips("sum_chips_%s_%d" % (tag, i), q, idx) for i, q in enumerate(received)]
    return _join_halves("join_halves_" + tag, halves)


def _relu2_epilogue(acc):
    r = jnp.maximum(acc, 0.0)
    return (r * r,)


def _relu2_bwd_epilogue(acc, s):
    return (acc * (2.0 * jnp.sqrt(s.astype(F32))),)


WIDE = (2048, 512, 2048)
DEEP = (2048, 512, 2048)


def _add_epilogue(acc, other):
    return (acc + other,)


def kernel(x, c, positions, w_ada_mix, b_ada_mix, w_ada_mlp, b_ada_mlp, g_pre_mix, g_post_mix, g_pre_mlp, g_post_mlp, conv_w_in, conv_b_in, conv_dw, conv_dw_b, conv_ln_g, conv_ln_b, conv_w_out, conv_b_out, w_ada_kv, b_ada_kv, g_kv, w_dkv, g_ckv, w_kr, w_uk, w_uv, w_dq, g_cq, w_uq, w_o, mlp_w_up, mlp_w_down, loss_target, m_w_ada_mix, m_b_ada_mix, m_w_ada_mlp, m_b_ada_mlp, m_g_pre_mix, m_g_post_mix, m_g_pre_mlp, m_g_post_mlp, m_conv_w_in, m_conv_b_in, m_conv_dw, m_conv_dw_b, m_conv_ln_g, m_conv_ln_b, m_conv_w_out, m_conv_b_out, m_w_ada_kv, m_b_ada_kv, m_g_kv, m_w_dkv, m_g_ckv, m_w_kr, m_w_uk, m_w_uv, m_w_dq, m_g_cq, m_w_uq, m_w_o, m_mlp_w_up, m_mlp_w_down, v_w_ada_mix, v_b_ada_mix, v_w_ada_mlp, v_b_ada_mlp, v_g_pre_mix, v_g_post_mix, v_g_pre_mlp, v_g_post_mlp, v_conv_w_in, v_conv_b_in, v_conv_dw, v_conv_dw_b, v_conv_ln_g, v_conv_ln_b, v_conv_w_out, v_conv_b_out, v_w_ada_kv, v_b_ada_kv, v_g_kv, v_w_dkv, v_g_ckv, v_w_kr, v_w_uk, v_w_uv, v_w_dq, v_g_cq, v_w_uq, v_w_o, v_mlp_w_up, v_mlp_w_down):
    weights = dict(w_ada_mix=w_ada_mix, b_ada_mix=b_ada_mix, w_ada_mlp=w_ada_mlp, b_ada_mlp=b_ada_mlp, g_pre_mix=g_pre_mix, g_post_mix=g_post_mix, g_pre_mlp=g_pre_mlp, g_post_mlp=g_post_mlp, conv_w_in=conv_w_in, conv_b_in=conv_b_in, conv_dw=conv_dw, conv_dw_b=conv_dw_b, conv_ln_g=conv_ln_g, conv_ln_b=conv_ln_b, conv_w_out=conv_w_out, conv_b_out=conv_b_out, w_ada_kv=w_ada_kv, b_ada_kv=b_ada_kv, g_kv=g_kv, w_dkv=w_dkv, g_ckv=g_ckv, w_kr=w_kr, w_uk=w_uk, w_uv=w_uv, w_dq=w_dq, g_cq=g_cq, w_uq=w_uq, w_o=w_o, mlp_w_up=mlp_w_up, mlp_w_down=mlp_w_down)
    m_in = dict(w_ada_mix=m_w_ada_mix, b_ada_mix=m_b_ada_mix, w_ada_mlp=m_w_ada_mlp, b_ada_mlp=m_b_ada_mlp, g_pre_mix=m_g_pre_mix, g_post_mix=m_g_post_mix, g_pre_mlp=m_g_pre_mlp, g_post_mlp=m_g_post_mlp, conv_w_in=m_conv_w_in, conv_b_in=m_conv_b_in, conv_dw=m_conv_dw, conv_dw_b=m_conv_dw_b, conv_ln_g=m_conv_ln_g, conv_ln_b=m_conv_ln_b, conv_w_out=m_conv_w_out, conv_b_out=m_conv_b_out, w_ada_kv=m_w_ada_kv, b_ada_kv=m_b_ada_kv, g_kv=m_g_kv, w_dkv=m_w_dkv, g_ckv=m_g_ckv, w_kr=m_w_kr, w_uk=m_w_uk, w_uv=m_w_uv, w_dq=m_w_dq, g_cq=m_g_cq, w_uq=m_w_uq, w_o=m_w_o, mlp_w_up=m_mlp_w_up, mlp_w_down=m_mlp_w_down)
    v_in = dict(w_ada_mix=v_w_ada_mix, b_ada_mix=v_b_ada_mix, w_ada_mlp=v_w_ada_mlp, b_ada_mlp=v_b_ada_mlp, g_pre_mix=v_g_pre_mix, g_post_mix=v_g_post_mix, g_pre_mlp=v_g_pre_mlp, g_post_mlp=v_g_post_mlp, conv_w_in=v_conv_w_in, conv_b_in=v_conv_b_in, conv_dw=v_conv_dw, conv_dw_b=v_conv_dw_b, conv_ln_g=v_conv_ln_g, conv_ln_b=v_conv_ln_b, conv_w_out=v_conv_w_out, conv_b_out=v_conv_b_out, w_ada_kv=v_w_ada_kv, b_ada_kv=v_b_ada_kv, g_kv=v_g_kv, w_dkv=v_w_dkv, g_ckv=v_g_ckv, w_kr=v_w_kr, w_uk=v_w_uk, w_uv=v_w_uv, w_dq=v_w_dq, g_cq=v_g_cq, w_uq=v_w_uq, w_o=v_w_o, mlp_w_up=v_mlp_w_up, mlp_w_down=v_mlp_w_down)
    grads, last_group = _step_grads(x, c, positions, loss_target, weights)
    loss, grad_x = grads.pop("loss"), grads.pop("x")
    names = list(weights)
    upd = {n: _adamw(n, weights[n], grads[n], m_in[n], v_in[n]) for n in names if n in grads}
    grads.update(last_group([upd[n][0] for n in ("w_ada_mix", "w_ada_mlp", "w_ada_kv", "mlp_w_down")]))
    upd.update({n: _adamw(n, weights[n], grads[n], m_in[n], v_in[n]) for n in names if n not in upd})
    return (loss, grad_x, *[grads[n] for n in names], *[upd[n][0] for n in names], *[upd[n][1] for n in names],
            *[upd[n][2] for n in names])


def _step_grads(x, c, positions, loss_target, w):
    xi, yi, ci = lax.axis_index("x"), lax.axis_index("y"), lax.axis_index("c")
    chip = 2 * xi + yi
    dev = 2 * chip + ci
    place_idx = jnp.stack([ci, chip]).astype(jnp.int32)
    t, d = x.shape[1], x.shape[2]
    dl = d // N_CHIPS
    r_kv, r_q = w["w_dkv"].shape[1], w["w_dq"].shape[2]
    n_ada, n_kvada = w["w_ada_mix"].shape[2], w["w_ada_kv"].shape[1]
    heads_l = w["w_uq"].shape[2] // (QK_NOPE + QK_ROPE)
    assert dl == r_kv == r_q and dl % LANES == 0 and heads_l * N_CHIPS % 2 == 0

    def chip_cols(a, width):
        return lax.dynamic_slice_in_dim(a, chip * width, width, axis=a.ndim - 1)

    uq = w["w_uq"][0].reshape(r_q, heads_l, QK_NOPE + QK_ROPE)
    uq_nope, uq_rope = uq[:, :, :QK_NOPE].reshape(r_q, -1), uq[:, :, QK_NOPE:].reshape(r_q, -1)
    kr_pad = jnp.pad(w["w_kr"], ((0, 0), (0, LANES - QK_ROPE)))
    small = jnp.concatenate([w["w_dkv"], kr_pad, w["w_dq"][0], w["w_uk"], w["w_uv"], uq_nope, uq_rope], axis=1)
    widths = [r_kv, LANES, r_q, w["w_uk"].shape[1], w["w_uv"].shape[1], uq_nope.shape[1], uq_rope.shape[1]]
    so = [0]
    for wd in widths:
        so.append(so[-1] + wd)
    big = dict(conv_w_in=(w["conv_w_in"], 0), conv_w_out=(w["conv_w_out"], 0), w_o=(w["w_o"], 0),
               up0=(w["mlp_w_up"], 0), up1=(w["mlp_w_up"], 1), down0=(w["mlp_w_down"], 0),
               down1=(w["mlp_w_down"], 1), small=(small[None], 0))
    groups = dict(conv_in=["conv_w_in"], conv_out=["conv_w_out"], up0=["up0"], down0=["down0"],
                  mla=["small", "w_o"], up1=["up1"], down1=["down1"])
    gathered, pending = {}, {}

    def start_gathers(gnames, after):
        token = after
        for gname in gnames:
            slots = [_cast_into_slot(n, *big[n], place_idx, after=token) for n in groups[gname]]
            s_sems, r_sems, bufs, token = _gather_start("gather_start_" + gname, slots, after=token)
            pending[gname] = (s_sems, r_sems, bufs)
        return token

    def finish_gather(gname, after):
        s_sems, r_sems, bufs = pending[gname]
        bufs = _gather_wait("gather_wait_" + gname, s_sems, r_sems, bufs, after)
        gathered.update(zip(groups[gname], _forward_to_sibling("forward_" + gname, bufs)))

    rowl = lambda a: a.reshape(-1, a.shape[2])
    w_up_f = lambda l: gathered["up%d" % l]
    w_down_f = lambda l: rowl(gathered["down%d" % l])

    pack_rows = [w["conv_dw"][0], w["conv_dw_b"], w["conv_ln_g"], w["conv_ln_b"], w["conv_b_out"],
                 w["conv_b_in"].reshape(2, dl), c.reshape(N_CHIPS, dl)]
    pack_rows = [jnp.pad(p, ((0, (-p.shape[0]) % 8), (0, 0))) for p in pack_rows]
    po = [0]
    for p in pack_rows:
        po.append(po[-1] + p.shape[0])
    packs = _all_gather_small("gather_params", jnp.concatenate(pack_rows, axis=0))
    by_chip = packs[0::2]
    first_start = start_gathers(("conv_in",), packs)[0:1, 0:1]

    def full_width(r0, nr):
        return jnp.transpose(by_chip[:, r0:r0 + nr, :], (1, 0, 2)).reshape(nr, d)

    dw_full, dwb_full = full_width(po[0], CONV_WIDTH), full_width(po[1], 1)
    lng_full, lnb_full, bout_full = full_width(po[2], 1), full_width(po[3], 1), full_width(po[4], 1)
    bin_full = by_chip[:, po[5]:po[5] + 2, :].reshape(1, 2 * d)
    c_all = packs[:, po[6]:po[6] + N_CHIPS, :].reshape(N_DEV, d) + first_start

    sc_all = _ew("silu_c", lambda a: (a * jax.nn.sigmoid(a),), [c_all], (F32,))[0]
    ada_w = [(w["w_ada_mix"], 0), (w["w_ada_mix"], 1), (w["w_ada_mlp"], 0), (w["w_ada_mlp"], 1),
             (w["w_ada_kv"][None], 0)]
    ada_b = [chip_cols(w["b_ada_mix"][0:1], n_ada), chip_cols(w["b_ada_mix"][1:2], n_ada),
             chip_cols(w["b_ada_mlp"][0:1], n_ada), chip_cols(w["b_ada_mlp"][1:2], n_ada),
             chip_cols(w["b_ada_kv"].reshape(1, -1), n_kvada)]
    mods = [_mm_nn("ada_fwd_%d" % i, sc_all, wi, "row", extras=(jnp.broadcast_to(bi, (N_DEV, bi.shape[1])),),
                   epilogue=_add_epilogue, w_layer=li) for i, ((wi, li), bi) in enumerate(zip(ada_w, ada_b))]
    mods_all = _all_gather_small("gather_mods", jnp.concatenate(mods, axis=1))
    started = start_gathers(("conv_out", "up0", "down0", "mla", "up1", "down1"),
                            mods_all[0, 0:1, 0:1] + first_start)[0:1, 0:1]
    mine = lax.dynamic_index_in_dim(mods_all[0::2], dev, axis=1, keepdims=False)
    offs = [0]
    for m_ in mods:
        offs.append(offs[-1] + m_.shape[1])
    mod_vec = [mine[:, offs[i]:offs[i + 1]].reshape(1, -1) for i in range(5)]
    split3 = lambda v: (v[:, :d], v[:, d:2 * d], v[:, 2 * d:])
    mix = [split3(mod_vec[0]), split3(mod_vec[1])]
    mlp = [split3(mod_vec[2]), split3(mod_vec[3])]
    kv_shift, kv_scale = mod_vec[4][:, :d], mod_vec[4][:, d:]

    cos_t, sin_t = _rope_tables(positions, t)
    vec = lambda a: a.reshape(1, -1)
    gpm, gqm = w["g_pre_mix"], w["g_post_mix"]
    gpl, gql = w["g_pre_mlp"], w["g_post_mlp"]
    h0 = x[0]
    after_token = lambda v, token: v + token[0:1, 0:1]

    def mlp_fwd(l, h):
        sh, sc, _ = mlp[l]
        (hn,) = _rw_fwd("mlp_pre_%d" % l, _f_pre, [h], [vec(gpl[l]), sh, sc], (BF,))
        finish_gather("up%d" % l, hn)
        s = _mm_nn("mlp_up_%d" % l, hn, w_up_f(l), "col", o_dtypes=(BF,), epilogue=_relu2_epilogue, tiles=WIDE)
        finish_gather("down%d" % l, s)
        y = _mm_nn("mlp_down_%d" % l, s, w_down_f(l), "row", tiles=DEEP)
        return hn, s, y

    def post_fwd(name, h, y, gate, g, bias=None):
        if bias is None:
            return _rw_fwd(name, lambda h_, y_, gt, g_: (h_ + _f_post(y_, gt, g_)[0],), [h, y], [gate, g], (F32,))[0]
        return _rw_fwd(name, lambda h_, y_, b_, gt, g_: (h_ + _f_post_bias(y_, b_, gt, g_)[0],), [h, y],
                       [bias, gate, g], (F32,))[0]

    (hn0,) = _rw_fwd("conv_pre", _f_pre, [h0], [vec(gpm[0]), mix[0][0] + started, mix[0][1]], (BF,))
    finish_gather("conv_in", hn0)
    w_in_f = gathered["conv_w_in"]
    z0 = _mm_nn("conv_in", hn0, w_in_f, "col")
    (u0,) = _rw_fwd("conv_glu", _f_glu, [z0], [bin_full], (F32,))
    uc0 = _conv_fwd(u0, dw_full, dwb_full)
    (n0,) = _rw_fwd("conv_ln", _f_ln_silu, [uc0], [lng_full, lnb_full], (BF,))
    finish_gather("conv_out", n0)
    w_out_f = rowl(gathered["conv_w_out"])
    y0 = _mm_nn("conv_out", n0, w_out_f, "row")
    h1 = post_fwd("conv_post", h0, y0, mix[0][2], vec(gqm[0]), bias=bout_full)
    hn1, s1, y1 = mlp_fwd(0, h1)
    h2 = post_fwd("mlp_post_0", h1, y1, mlp[0][2], vec(gql[0]))

    finish_gather("mla", h2)
    gs = gathered["small"]
    w_dkvkr = rowl(gs[:, :, so[0]:so[2]])
    w_dq_f = rowl(gs[:, :, so[2]:so[3]])
    w_uk_f, w_uv_f = gs[:, :, so[3]:so[4]], gs[:, :, so[4]:so[5]]
    w_qn_f, w_qr_f = gs[:, :, so[5]:so[6]], gs[:, :, so[6]:so[7]]
    w_o_f = rowl(gathered["w_o"])
    kvn, hn2 = _rw_fwd("mla_pre", _f_pre2, [h2], [vec(w["g_kv"]), kv_shift, kv_scale, vec(gpm[1]), mix[1][0], mix[1][1]],
                       (BF, BF))
    pre_kv = _mm_nn("mla_dkv", kvn, w_dkvkr, "row")
    pre_q = _mm_nn("mla_dq", hn2, w_dq_f, "row")
    f_lat = _make_f_lat(r_kv)
    c_kv, kr, c_q = _rw_fwd("mla_latent", f_lat, [pre_kv, pre_q, cos_t, sin_t], [vec(w["g_ckv"]), vec(w["g_cq"][0])],
                            (BF, BF, BF))
    kn = _mm_nn("mla_uk", c_kv, w_uk_f, "col", o_dtypes=(BF,))
    vv = _mm_nn("mla_uv", c_kv, w_uv_f, "col", o_dtypes=(BF,))
    qn = _mm_nn("mla_uq_nope", c_q, w_qn_f, "col", o_dtypes=(BF,))
    qr_pre = _mm_nn("mla_uq_rope", c_q, w_qr_f, "col")
    (qr,) = _rw_fwd("mla_q_rope", _f_rope, [qr_pre, cos_t, sin_t], [], (BF,))
    att = _attn_fwd(qn, qr, kn, kr, vv)
    y2 = _mm_nn("mla_o", att, w_o_f, "row")
    h3 = post_fwd("mla_post", h2, y2, mix[1][2], vec(gqm[1]))
    hn3, s3, y3 = mlp_fwd(1, h3)
    h4 = post_fwd("mlp_post_1", h3, y3, mlp[1][2], vec(gql[1]))

    dh4, loss_part = _loss_grad(h4, loss_target[0])
    loss = lax.psum(loss_part[0, 0], ("x", "y", "c"))

    gw = {}
    gvec = {}

    dw_mm = functools.partial(_mm_tn, o_dtype=BF)

    def mlp_bwd(l, h_in, hn, s, y, dh, token=None):
        sh, sc, gate = mlp[l]
        if token is not None:
            gate = after_token(gate, token)
        (dy,), (dgate, dgq) = _rw_bwd("mlp_post_bwd_%d" % l, _f_post, [y], [gate, vec(gql[l])], [dh], [BF])
        gw["down%d" % l] = dw_mm("mlp_down_dw_%d" % l, s, dy, "row", tiles=(512, 2048, 2048))
        da = _mm_nt("mlp_down_dx_%d" % l, dy, w_down_f(l), "row", o_dtypes=(BF,), extras=(s,),
                    epilogue=_relu2_bwd_epilogue, tiles=WIDE)
        gw["up%d" % l] = dw_mm("mlp_up_dw_%d" % l, hn, da, "col", tiles=WIDE)
        dhn = _mm_nt("mlp_up_dx_%d" % l, da, w_up_f(l), "col", tiles=DEEP)
        (dh_in,), (dgp, dsh, dsc) = _rw_bwd("mlp_pre_bwd_%d" % l, _f_pre, [h_in], [vec(gpl[l]), sh, sc], [dhn], [F32],
                                            add_rows={0: dh})
        gvec["mlp%d" % l] = (dsh, dsc, dgate)
        gvec["g_pre_mlp%d" % l], gvec["g_post_mlp%d" % l] = dgp, dgq
        return dh_in

    chunked = lambda a: a.reshape(N_CHIPS, a.shape[0] // N_CHIPS, a.shape[1])
    to_chunks = lambda a: a if a.ndim == 3 else chunked(a)
    red = {}

    def swap_begin(tag, names):
        return _swap_start("swap_start_" + tag, [to_chunks(gw[n]) for n in names])

    def reduce_begin(tag, names, after=None, swapped=None):
        state, token = _reduce_scatter_begin(tag, [to_chunks(gw[n]) for n in names], place_idx, after, swapped)
        return (names, state), token

    def reduce_end(handle, after):
        names, state = handle
        red.update(zip(names, _reduce_scatter_end(state, after)))

    dh3 = mlp_bwd(1, h3, hn3, s3, y3, dh4)
    sw1, token = swap_begin("r1", ["up1", "down1"])

    (dy2,), (dgate, dgq) = _rw_bwd("mla_post_bwd", _f_post, [y2], [after_token(mix[1][2], token), vec(gqm[1])],
                                   [dh3], [BF])
    gvec["g_post_mix1"] = dgq
    gw["w_o"] = dw_mm("mla_o_dw", att, dy2, "row")
    datt = _mm_nt("mla_o_dx", dy2, w_o_f, "row", o_dtypes=(BF,))
    rs1, token = reduce_begin("r1", ["up1", "down1"], after=datt, swapped=sw1)
    dqn, dqr, dkn, dvv, dkr = _attn_bwd(qn, qr, kn, kr + token[0:1, 0:1].astype(BF), vv, datt)
    (dqr_pre,), _ = _rw_bwd("mla_q_rope_bwd", _f_rope, [qr_pre, cos_t, sin_t], [], [dqr], [BF, None, None])
    g_qn = dw_mm("mla_uq_nope_dw", c_q, dqn, "col")
    g_qr = dw_mm("mla_uq_rope_dw", c_q, dqr_pre, "col")
    dc_q = _mm_nt("mla_uq_nope_dx", dqn, w_qn_f, "col")
    dc_q = _mm_nt("mla_uq_rope_dx", dqr_pre, w_qr_f, "col", extras=(dc_q,), epilogue=_add_epilogue)
    g_uk = dw_mm("mla_uk_dw", c_kv, dkn, "col")
    g_uv = dw_mm("mla_uv_dw", c_kv, dvv, "col")
    dc_kv = _mm_nt("mla_uk_dx", dkn, w_uk_f, "col")
    dc_kv = _mm_nt("mla_uv_dx", dvv, w_uv_f, "col", extras=(dc_kv,), epilogue=_add_epilogue)
    (dpre_kv, dpre_q), (dg_ckv, dg_cq) = _rw_bwd(
        "mla_latent_bwd", f_lat, [pre_kv, pre_q, cos_t, sin_t], [vec(w["g_ckv"]), vec(w["g_cq"][0])],
        [dc_kv, dkr, dc_q], [BF, BF, None, None])
    gvec["g_ckv"], gvec["g_cq"] = dg_ckv, dg_cq
    g_dkvkr = dw_mm("mla_dkv_dw", kvn, dpre_kv, "row")
    g_dq = dw_mm("mla_dq_dw", hn2, dpre_q, "row")
    dkvn = _mm_nt("mla_dkv_dx", dpre_kv, w_dkvkr, "row")
    dhn2 = _mm_nt("mla_dq_dx", dpre_q, w_dq_f, "row")
    (dh2,), (dg_kv, dkvsh, dkvsc, dgp, dsh, dsc) = _rw_bwd(
        "mla_pre_bwd", _f_pre2, [h2], [vec(w["g_kv"]), kv_shift, kv_scale, vec(gpm[1]), mix[1][0], mix[1][1]],
        [dkvn, dhn2], [F32], add_rows={0: dh3})
    gvec["mix1"] = (dsh, dsc, dgate)
    gvec["kv"] = (dkvsh, dkvsc)
    gvec["g_kv"], gvec["g_pre_mix1"] = dg_kv, dgp
    gw["small"] = jnp.concatenate([chunked(g_dkvkr), chunked(g_dq), g_uk, g_uv, g_qn, g_qr], axis=2)
    reduce_end(rs1, dh2)
    rs2, token = reduce_begin("r2", ["small", "w_o"])

    dh1 = mlp_bwd(0, h1, hn1, s1, y1, dh2, token=token)
    reduce_end(rs2, dh1)
    rs3, token = reduce_begin("r3", ["up0", "down0"])

    (dy0,), (dbout, dgate, dgq) = _rw_bwd("conv_post_bwd", _f_post_bias, [y0],
                                          [bout_full, after_token(mix[0][2], token), vec(gqm[0])], [dh1], [BF])
    gvec["g_post_mix0"] = dgq
    gw["conv_w_out"] = dw_mm("conv_out_dw", n0, dy0, "row")
    dn0 = _mm_nt("conv_out_dx", dy0, w_out_f, "row")
    (duc0,), (dlng, dlnb) = _rw_bwd("conv_ln_bwd", _f_ln_silu, [uc0], [lng_full, lnb_full], [dn0], [F32])
    du0, ddw = _conv_bwd(u0, duc0, dw_full)
    (dz0,), (dbin,) = _rw_bwd("conv_glu_bwd", _f_glu, [z0], [bin_full], [du0], [BF])
    gw["conv_w_in"] = dw_mm("conv_in_dw", hn0, dz0, "col")
    dhn0 = _mm_nt("conv_in_dx", dz0, w_in_f, "col")
    (dx,), (dgp, dsh, dsc) = _rw_bwd("conv_pre_bwd", _f_pre, [h0], [vec(gpm[0]), mix[0][0], mix[0][1]], [dhn0], [F32],
                                     add_rows={0: dh1})
    gvec["mix0"] = (dsh, dsc, dgate)
    gvec["g_pre_mix0"] = dgp
    reduce_end(rs3, dx)

    vec_list = [*gvec["mix0"], *gvec["mix1"], *gvec["mlp0"], *gvec["mlp1"], *gvec["kv"],
                gvec["g_pre_mix0"], gvec["g_pre_mix1"], gvec["g_post_mix0"], gvec["g_post_mix1"],
                gvec["g_pre_mlp0"], gvec["g_pre_mlp1"], gvec["g_post_mlp0"], gvec["g_post_mlp1"],
                gvec["g_kv"], gvec["g_ckv"], gvec["g_cq"], dbin, dlng, dlnb, dbout, ddw.reshape(1, -1)]
    vo = [0]
    for v_ in vec_list:
        vo.append(vo[-1] + v_.shape[1])
    vec_pad = (-vo[-1]) % (8 * LANES)
    n_vec = vo[-1] + vec_pad
    flat = jnp.concatenate(vec_list + [jnp.zeros((1, vec_pad), F32)], axis=1).reshape(8, n_vec // 8)
    all_vecs = _all_gather_small("gather_vector_grads", flat).reshape(N_DEV, 8, n_vec // 8)
    rs4, token = reduce_begin("r4", ["conv_w_in", "conv_w_out"], after=all_vecs)
    all_vecs = after_token(all_vecs, token)
    summed = _sum_devices(all_vecs).reshape(1, n_vec)
    per_dev = all_vecs.reshape(N_DEV, n_vec)
    seg = lambda a, i: a[:, vo[i]:vo[i + 1]]

    out = {"loss": loss, "x": dx.reshape(x.shape)}
    dm_mix = [jnp.concatenate([seg(per_dev, 3 * l + i) for i in range(3)], axis=1) for l in range(2)]
    dm_mlp = [jnp.concatenate([seg(per_dev, 6 + 3 * l + i) for i in range(3)], axis=1) for l in range(2)]
    dm_kv = jnp.concatenate([seg(per_dev, 12), seg(per_dev, 13)], axis=1)
    ada_dw = lambda name, dm, width: _mm_tn(name, sc_all, chip_cols(dm, width), "row")
    out["w_ada_mix"] = jnp.stack([ada_dw("ada_mix_dw_%d" % l, dm_mix[l], n_ada) for l in range(2)])
    out["w_ada_mlp"] = jnp.stack([ada_dw("ada_mlp_dw_%d" % l, dm_mlp[l], n_ada) for l in range(2)])
    out["w_ada_kv"] = ada_dw("ada_kv_dw", dm_kv, n_kvada)
    sum_seg = lambda i: seg(summed, i)
    out["b_ada_mix"] = jnp.concatenate([jnp.concatenate([sum_seg(3 * l + i) for i in range(3)], axis=1) for l in range(2)], axis=0)
    out["b_ada_mlp"] = jnp.concatenate([jnp.concatenate([sum_seg(6 + 3 * l + i) for i in range(3)], axis=1) for l in range(2)], axis=0)
    out["b_ada_kv"] = jnp.concatenate([sum_seg(12), sum_seg(13)], axis=1).reshape(-1)
    out["g_pre_mix"] = jnp.concatenate([sum_seg(14), sum_seg(15)], axis=0)
    out["g_post_mix"] = jnp.concatenate([sum_seg(16), sum_seg(17)], axis=0)
    out["g_pre_mlp"] = jnp.concatenate([sum_seg(18), sum_seg(19)], axis=0)
    out["g_post_mlp"] = jnp.concatenate([sum_seg(20), sum_seg(21)], axis=0)
    out["g_kv"] = sum_seg(22).reshape(-1)
    out["g_ckv"] = sum_seg(23).reshape(-1)
    out["g_cq"] = sum_seg(24)
    out["conv_b_in"] = chip_cols(sum_seg(25), 2 * dl)
    out["conv_ln_g"] = chip_cols(sum_seg(26), dl)
    out["conv_ln_b"] = chip_cols(sum_seg(27), dl)
    out["conv_b_out"] = chip_cols(sum_seg(28), dl)
    ddw_sum = chip_cols(sum_seg(29).reshape(CONV_WIDTH + 1, d), dl)
    out["conv_dw"] = ddw_sum[:CONV_WIDTH].reshape(1, CONV_WIDTH, dl)
    out["conv_dw_b"] = ddw_sum[CONV_WIDTH:]

    def last_group(after):
        reduce_end(rs4, after)
        return {"conv_w_in": red["conv_w_in"][None], "conv_w_out": red["conv_w_out"][None]}

    out["w_o"] = red["w_o"][None]
    out["mlp_w_up"] = jnp.stack([red["up0"], red["up1"]])
    out["mlp_w_down"] = jnp.stack([red["down0"], red["down1"]])
    rs = red["small"]
    piece = lambda i: rs[:, so[i]:so[i + 1]]
    out["w_dkv"] = piece(0)
    out["w_kr"] = piece(1)[:, :QK_ROPE]
    out["w_dq"] = piece(2)[None]
    out["w_uk"], out["w_uv"] = piece(3), piece(4)
    out["w_uq"] = jnp.concatenate([piece(5).reshape(r_q, heads_l, QK_NOPE), piece(6).reshape(r_q, heads_l, QK_ROPE)],
                                  axis=2).reshape(1, r_q, -1)
    return out, last_group
```

```python
import functools

import jax
import jax.numpy as jnp
from jax import lax
from jax.experimental import pallas as pl
from jax.experimental.pallas import tpu as pltpu

F32 = jnp.float32
BF = jnp.bfloat16
MXU_DTYPE = BF

EPS = 1e-6
NEG = -1e30
ROPE_THETA = 10000.0
QK_NOPE = 128
QK_ROPE = 64
V_HEAD = 128
CONV_WIDTH = 31
ADAM_LR, ADAM_B1, ADAM_B2, ADAM_EPS, ADAM_WD, ADAM_STEP = 0.001, 0.9, 0.999, 1e-08, 0.01, 10

N_CHIPS = 4
N_DEV = 8
LANES = 128
VMEM_LIMIT = 56 * 2 ** 20
MESH = pl.DeviceIdType.MESH
ANY = pl.BlockSpec(memory_space=pl.ANY)


def _params(sem=None):
    return pltpu.CompilerParams(dimension_semantics=sem, vmem_limit_bytes=VMEM_LIMIT)


def _tile(n, pref, unit=LANES):
    if n <= pref:
        return n
    t = (pref // unit) * unit
    while t > unit and n % t:
        t -= unit
    assert n % t == 0, (n, pref)
    return t


def _dg(a, b, ca, cb):
    return lax.dot_general(a.astype(MXU_DTYPE), b.astype(MXU_DTYPE), (((ca,), (cb,)), ((), ())),
                           preferred_element_type=F32)


@jax.custom_vjp
def _dot_nn(a, b):
    return _dg(a, b, 1, 0)


def _dot_nn_fwd(a, b):
    return _dg(a, b, 1, 0), (a, b)


def _dot_nn_bwd(res, g):
    a, b = res
    return _dg(g, b, 1, 1).astype(a.dtype), _dg(a, g, 0, 0).astype(b.dtype)


_dot_nn.defvjp(_dot_nn_fwd, _dot_nn_bwd)


@jax.custom_vjp
def _dot_nt(a, b):
    return _dg(a, b, 1, 1)


def _dot_nt_fwd(a, b):
    return _dg(a, b, 1, 1), (a, b)


def _dot_nt_bwd(res, g):
    a, b = res
    return _dg(g, b, 1, 0).astype(a.dtype), _dg(g, a, 0, 0).astype(b.dtype)


_dot_nt.defvjp(_dot_nt_fwd, _dot_nt_bwd)


def _matmul(name, a, b, *, ca, cb, grid, a_blk, a_map, b_blk, b_map, o_shape, o_dtypes, o_blk, o_map,
            extras=(), epilogue=None):
    nk = grid[-1]
    n_ex, n_out = len(extras), len(o_dtypes)

    def body(*refs):
        a_ref, b_ref = refs[0], refs[1]
        ex_refs = refs[2:2 + n_ex]
        out_refs = refs[2 + n_ex:2 + n_ex + n_out]
        kk = pl.program_id(len(grid) - 1)

        def finish(acc):
            outs = (acc,) if epilogue is None else epilogue(acc, *[r[...] for r in ex_refs])
            for r, o in zip(out_refs, outs):
                r[...] = o.astype(r.dtype)

        part = _dg(a_ref[...], b_ref[...], ca, cb)
        if nk == 1:
            finish(part)
        else:
            acc_ref = refs[-1]

            @pl.when(kk == 0)
            def _():
                acc_ref[...] = part

            @pl.when(kk > 0)
            def _():
                acc_ref[...] += part

            @pl.when(kk == nk - 1)
            def _():
                finish(acc_ref[...])

    o_spec = pl.BlockSpec(o_blk, o_map)
    acc_shape = tuple(d for d in o_blk if d is not None)
    res = pl.pallas_call(
        body, grid=grid,
        in_specs=[pl.BlockSpec(a_blk, a_map), pl.BlockSpec(b_blk, b_map)] + [o_spec] * n_ex,
        out_specs=[o_spec] * n_out,
        out_shape=[jax.ShapeDtypeStruct(o_shape, dt) for dt in o_dtypes],
        scratch_shapes=[] if nk == 1 else [pltpu.VMEM(acc_shape, F32)],
        compiler_params=_params(("parallel",) * (len(grid) - 1) + ("arbitrary",)),
        name=name,
    )(a, b, *extras)
    return res[0] if n_out == 1 else res


TM, TN, TK = 2048, 512, 2048


def _mm_nn(name, a, w, layout, o_dtypes=(F32,), extras=(), epilogue=None, tiles=(TM, TN, TK), w_layer=None):
    m, k = a.shape
    tm, tk = _tile(m, tiles[0], 8), _tile(k, tiles[2])
    if layout == "row":
        n = w.shape[-1]
        tn = _tile(n, tiles[1])
        grid = (m // tm, n // tn, k // tk)
        b_blk, b_map = (tk, tn), (lambda i, j, kk: (kk, j))
        if w_layer is not None:
            b_blk, b_map = (None, tk, tn), (lambda i, j, kk: (w_layer, kk, j))
    else:
        nl = w.shape[2]
        n = nl * w.shape[0]
        tn = _tile(nl, tiles[1])
        nb = nl // tn
        grid = (m // tm, n // tn, k // tk)
        b_blk, b_map = (None, tk, tn), (lambda i, j, kk: (j // nb, kk, j % nb))
    return _matmul(name, a, w, ca=1, cb=0, grid=grid, a_blk=(tm, tk), a_map=lambda i, j, kk: (i, kk),
                   b_blk=b_blk, b_map=b_map, o_shape=(m, n), o_dtypes=o_dtypes, o_blk=(tm, tn),
                   o_map=lambda i, j, kk: (i, j), extras=extras, epilogue=epilogue)


def _mm_nt(name, g, w, layout, o_dtypes=(F32,), extras=(), epilogue=None, tiles=(TM, TN, TK)):
    m, n = g.shape
    tm, tn = _tile(m, tiles[0], 8), None
    if layout == "row":
        k = w.shape[0]
        tn = _tile(n, tiles[2])
        tk = _tile(k, tiles[1])
        grid = (m // tm, k // tk, n // tn)
        b_blk, b_map = (tk, tn), (lambda i, j, kk: (j, kk))
    else:
        k, nl = w.shape[1], w.shape[2]
        tn = _tile(nl, tiles[2])
        nb = nl // tn
        tk = _tile(k, tiles[1])
        grid = (m // tm, k // tk, n // tn)
        b_blk, b_map = (None, tk, tn), (lambda i, j, kk: (kk // nb, j, kk % nb))
    return _matmul(name, g, w, ca=1, cb=1, grid=grid, a_blk=(tm, tn), a_map=lambda i, j, kk: (i, kk),
                   b_blk=b_blk, b_map=b_map, o_shape=(m, k), o_dtypes=o_dtypes, o_blk=(tm, tk),
                   o_map=lambda i, j, kk: (i, j), extras=extras, epilogue=epilogue)


def _mm_tn(name, x, g, layout, o_dtype=F32, tiles=(TM, TN, TK)):
    t, k = x.shape
    n = g.shape[1]
    tt = _tile(t, tiles[2], 8)
    tk = _tile(k, tiles[0])
    if layout == "row":
        tn = _tile(n, tiles[1])
        o_shape, o_blk, o_map = (k, n), (tk, tn), (lambda i, j, kk: (i, j))
    else:
        nl = n // N_CHIPS
        tn = _tile(nl, tiles[1])
        nb = nl // tn
        o_shape, o_blk, o_map = (N_CHIPS, k, nl), (None, tk, tn), (lambda i, j, kk: (j // nb, i, j % nb))
    grid = (k // tk, n // tn, t // tt)
    return _matmul(name, x, g, ca=0, cb=0, grid=grid, a_blk=(tt, tk), a_map=lambda i, j, kk: (kk, i),
                   b_blk=(tt, tn), b_map=lambda i, j, kk: (kk, j), o_shape=o_shape, o_dtypes=(o_dtype,),
                   o_blk=o_blk, o_map=o_map)


TR = 256


def _rw_fwd(name, fn, rows, vecs, o_dtypes, tr=TR):
    t = rows[0].shape[0]
    tr = min(tr, t)
    n_r, n_v = len(rows), len(vecs)
    o_sds = jax.eval_shape(fn, *[jax.ShapeDtypeStruct((tr, r.shape[1]), r.dtype) for r in rows],
                           *[jax.ShapeDtypeStruct(v.shape, v.dtype) for v in vecs])

    def body(*refs):
        outs = fn(*[r[...] for r in refs[:n_r + n_v]])
        for r, o in zip(refs[n_r + n_v:], outs):
            r[...] = o.astype(r.dtype)

    res = pl.pallas_call(
        body, grid=(t // tr,),
        in_specs=[pl.BlockSpec((tr, r.shape[1]), lambda i: (i, 0)) for r in rows]
        + [pl.BlockSpec(v.shape, lambda i: (0, 0)) for v in vecs],
        out_specs=[pl.BlockSpec((tr, o.shape[1]), lambda i: (i, 0)) for o in o_sds],
        out_shape=[jax.ShapeDtypeStruct((t, o.shape[1]), dt) for o, dt in zip(o_sds, o_dtypes)],
        compiler_params=_params(("parallel",)), name=name,
    )(*rows, *vecs)
    return res


def _rw_bwd(name, fn, rows, vecs, cots, row_grad_dtypes, add_rows=None, tr=TR):
    t = rows[0].shape[0]
    tr = min(tr, t)
    add_rows = add_rows or {}
    n_r, n_v, n_c = len(rows), len(vecs), len(cots)
    diff = [i for i, dt in enumerate(row_grad_dtypes) if dt is not None]
    adds = [add_rows[i] for i in diff if i in add_rows]
    n_a = len(adds)

    def body(*refs):
        rvals = [r[...] for r in refs[:n_r]]
        vvals = [r[...] for r in refs[n_r:n_r + n_v]]
        c_refs = refs[n_r + n_v:n_r + n_v + n_c]
        a_refs = list(refs[n_r + n_v + n_c:n_r + n_v + n_c + n_a])
        o_refs = refs[n_r + n_v + n_c + n_a:]

        def f(*d):
            full = list(rvals)
            for i, x in zip(diff, d[:len(diff)]):
                full[i] = x
            return fn(*full, *d[len(diff):])

        outs, vjp = jax.vjp(f, *[rvals[i] for i in diff], *vvals)
        grads = vjp(tuple(c[...].astype(o.dtype) for c, o in zip(c_refs, outs)))
        for n, i in enumerate(diff):
            gr = grads[n].astype(F32)
            if i in add_rows:
                gr = gr + a_refs.pop(0)[...].astype(F32)
            o_refs[n][...] = gr.astype(o_refs[n].dtype)
        first = pl.program_id(0) == 0
        for r, gv in zip(o_refs[len(diff):], grads[len(diff):]):
            @pl.when(first)
            def _(r=r, gv=gv):
                r[...] = gv

            @pl.when(jnp.logical_not(first))
            def _(r=r, gv=gv):
                r[...] += gv

    row_spec = lambda a: pl.BlockSpec((tr, a.shape[1]), lambda i: (i, 0))
    vec_spec = lambda a: pl.BlockSpec(a.shape, lambda i: (0, 0))
    res = pl.pallas_call(
        body, grid=(t // tr,),
        in_specs=[row_spec(r) for r in rows] + [vec_spec(v) for v in vecs] + [row_spec(c) for c in cots]
        + [row_spec(a) for a in adds],
        out_specs=[row_spec(rows[i]) for i in diff] + [vec_spec(v) for v in vecs],
        out_shape=[jax.ShapeDtypeStruct(rows[i].shape, row_grad_dtypes[i]) for i in diff]
        + [jax.ShapeDtypeStruct(v.shape, F32) for v in vecs],
        compiler_params=_params(("arbitrary",)), name=name,
    )(*rows, *vecs, *cots, *adds)
    return res[:len(diff)], res[len(diff):]


def _rms(x, g):
    return x * lax.rsqrt(jnp.mean(x * x, axis=-1, keepdims=True) + EPS) * g


def _f_pre(h, g, shift, scale):
    return (_rms(h, g) * (1.0 + scale) + shift,)


def _f_pre2(h, g1, sh1, sc1, g2, sh2, sc2):
    return _f_pre(h, g1, sh1, sc1) + _f_pre(h, g2, sh2, sc2)


def _f_post(y, gate, g):
    return (gate * _rms(y, g),)


def _f_post_bias(y, bias, gate, g):
    return (gate * _rms(y + bias, g),)


def _f_glu(z, bias):
    z = z + bias
    half = z.shape[1] // 2
    return (z[:, :half] * jax.nn.sigmoid(z[:, half:]),)


def _f_ln_silu(u, g, b):
    mu = jnp.mean(u, axis=-1, keepdims=True)
    var = jnp.mean(jnp.square(u - mu), axis=-1, keepdims=True)
    y = (u - mu) * lax.rsqrt(var + EPS) * g + b
    return (y * jax.nn.sigmoid(y),)


def _rope_raw(x, cos, sin):
    n = x.shape[1]
    reps = n // LANES
    if reps > 1:
        cos = jnp.concatenate([cos] * reps, axis=1)
        sin = jnp.concatenate([sin] * reps, axis=1)
    lane = lax.broadcasted_iota(jnp.int32, x.shape, 1)
    half = QK_ROPE // 2
    partner = jnp.where((lane % QK_ROPE) < half, pltpu.roll(x, n - half, 1), pltpu.roll(x, half, 1))
    return x * cos + partner * sin


@jax.custom_vjp
def _rope(x, cos, sin):
    return _rope_raw(x, cos, sin)


def _rope_fwd(x, cos, sin):
    return _rope_raw(x, cos, sin), (cos, sin)


def _rope_bwd(res, g):
    cos, sin = res
    return _rope_raw(g, cos, -sin), jnp.zeros_like(cos), jnp.zeros_like(sin)


_rope.defvjp(_rope_fwd, _rope_bwd)


def _make_f_lat(r_kv):
    def f(pre_kv, pre_q, cos, sin, g_ckv, g_cq):
        c_kv = _rms(pre_kv[:, :r_kv], g_ckv)
        kr = _rope(pre_kv[:, r_kv:], cos, sin)
        return c_kv, kr, _rms(pre_q, g_cq)
    return f


def _f_rope(x, cos, sin):
    return (_rope(x, cos, sin),)


CONV_TC = 128
CONV_TT = 256
PADR = 32


def _conv_fwd(u, dw, dw_b):
    t, d = u.shape
    tc, tt = min(CONV_TC, d), min(CONV_TT, t)
    off = PADR - (CONV_WIDTH - 1)

    def body(u_ref, w_ref, b_ref, o_ref, pad_ref):
        pad_ref[pl.ds(0, PADR), :] = jnp.zeros((PADR, tc), F32)
        pad_ref[pl.ds(PADR, t), :] = u_ref[...]
        for t0 in range(0, t, tt):
            acc = jnp.zeros((tt, tc), F32) + b_ref[...]
            for j in range(CONV_WIDTH):
                acc = acc + pad_ref[pl.ds(t0 + off + j, tt), :] * w_ref[pl.ds(j, 1), :]
            o_ref[pl.ds(t0, tt), :] = acc

    return pl.pallas_call(
        body, grid=(d // tc,),
        in_specs=[pl.BlockSpec((t, tc), lambda j: (0, j)), pl.BlockSpec((CONV_WIDTH, tc), lambda j: (0, j)),
                  pl.BlockSpec((1, tc), lambda j: (0, j))],
        out_specs=pl.BlockSpec((t, tc), lambda j: (0, j)),
        out_shape=jax.ShapeDtypeStruct((t, d), F32),
        scratch_shapes=[pltpu.VMEM((t + PADR, tc), F32)],
        compiler_params=_params(("parallel",)), name="conv_fwd",
    )(u, dw, dw_b)


def _conv_bwd(u, duc, dw):
    t, d = u.shape
    tc, tt = min(CONV_TC, d), min(CONV_TT, t)
    off = PADR - (CONV_WIDTH - 1)

    def body(u_ref, g_ref, w_ref, du_ref, dwt_ref, padu_ref, padg_ref):
        padu_ref[pl.ds(0, PADR), :] = jnp.zeros((PADR, tc), F32)
        padu_ref[pl.ds(PADR, t), :] = u_ref[...]
        padg_ref[pl.ds(t, PADR), :] = jnp.zeros((PADR, tc), F32)
        padg_ref[pl.ds(0, t), :] = g_ref[...]
        for t0 in range(0, t, tt):
            acc = jnp.zeros((tt, tc), F32)
            for j in range(CONV_WIDTH):
                acc = acc + padg_ref[pl.ds(t0 + (CONV_WIDTH - 1) - j, tt), :] * w_ref[pl.ds(j, 1), :]
            du_ref[pl.ds(t0, tt), :] = acc
        for j in range(CONV_WIDTH):
            acc = jnp.zeros((tt, tc), F32)
            for t0 in range(0, t, tt):
                acc = acc + g_ref[pl.ds(t0, tt), :] * padu_ref[pl.ds(t0 + off + j, tt), :]
            dwt_ref[pl.ds(j, 1), :] = jnp.sum(acc, axis=0, keepdims=True)
        acc = jnp.zeros((tt, tc), F32)
        for t0 in range(0, t, tt):
            acc = acc + g_ref[pl.ds(t0, tt), :]
        dwt_ref[pl.ds(CONV_WIDTH, 1), :] = jnp.sum(acc, axis=0, keepdims=True)

    col = lambda r: pl.BlockSpec((r, tc), lambda j: (0, j))
    return pl.pallas_call(
        body, grid=(d // tc,),
        in_specs=[col(t), col(t), col(CONV_WIDTH)],
        out_specs=[col(t), col(CONV_WIDTH + 1)],
        out_shape=[jax.ShapeDtypeStruct((t, d), F32), jax.ShapeDtypeStruct((CONV_WIDTH + 1, d), F32)],
        scratch_shapes=[pltpu.VMEM((t + PADR, tc), F32), pltpu.VMEM((t + PADR, tc), F32)],
        compiler_params=_params(("parallel",)), name="conv_bwd",
    )(u, duc, dw)


@jax.custom_vjp
def _swap_halves(x):
    return pltpu.roll(x, LANES // 2, 1)


_swap_halves.defvjp(lambda x: (pltpu.roll(x, LANES // 2, 1), None), lambda _, g: (pltpu.roll(g, LANES // 2, 1),))


def _attn_block(qn, qr, kn, kr, v, q0):
    scale = (QK_NOPE + QK_ROPE) ** -0.5
    lane = lax.broadcasted_iota(jnp.int32, kr.shape, 1)
    kr_a = kr * (lane < QK_ROPE).astype(kr.dtype)
    kr_b = _swap_halves(kr_a)
    outs = []
    for hh, kr_h in ((0, kr_a), (1, kr_b)):
        sl = slice(hh * QK_NOPE, (hh + 1) * QK_NOPE)
        s = _dot_nt(jnp.concatenate([qn[:, sl], qr], axis=1), jnp.concatenate([kn[:, sl], kr_h], axis=1)) * scale
        row = lax.broadcasted_iota(jnp.int32, s.shape, 0) + q0
        col = lax.broadcasted_iota(jnp.int32, s.shape, 1)
        s = jnp.where(col <= row, s, NEG)
        e = jnp.exp(s - jnp.max(s, axis=-1, keepdims=True))
        p = e / jnp.sum(e, axis=-1, keepdims=True)
        outs.append(_dot_nn(p, v[:, sl]))
    return jnp.concatenate(outs, axis=1)


def _attn_fwd(qn, qr, kn, kr, v, tq=512):
    t, w = qn.shape
    pairs = w // (2 * QK_NOPE)
    tq = min(tq, t)
    pw = 2 * QK_NOPE

    def body(qn_ref, qr_ref, kn_ref, kr_ref, v_ref, o_ref):
        for q0 in range(0, t, tq):
            l = q0 + tq
            o_ref[pl.ds(q0, tq), :] = _attn_block(
                qn_ref[pl.ds(q0, tq), :], qr_ref[pl.ds(q0, tq), :], kn_ref[pl.ds(0, l), :], kr_ref[pl.ds(0, l), :],
                v_ref[pl.ds(0, l), :], q0).astype(o_ref.dtype)

    pair = lambda wd: pl.BlockSpec((t, wd), lambda p: (0, p))
    return pl.pallas_call(
        body, grid=(pairs,),
        in_specs=[pair(pw), pair(LANES), pair(pw), pl.BlockSpec((t, LANES), lambda p: (0, 0)), pair(pw)],
        out_specs=pair(pw), out_shape=jax.ShapeDtypeStruct((t, w), BF),
        compiler_params=_params(("parallel",)), name="attn_fwd",
    )(qn, qr, kn, kr, v)


def _attn_bwd(qn, qr, kn, kr, v, do, tq=256):
    t, w = qn.shape
    pairs = w // (2 * QK_NOPE)
    tq = min(tq, t)
    pw = 2 * QK_NOPE

    def body(qn_ref, qr_ref, kn_ref, kr_ref, v_ref, do_ref, dqn_ref, dqr_ref, dkn_ref, dv_ref, dkr_ref,
             akn_ref, av_ref, akr_ref):
        akn_ref[...] = jnp.zeros_like(akn_ref)
        av_ref[...] = jnp.zeros_like(av_ref)
        akr_ref[...] = jnp.zeros_like(akr_ref)
        for q0 in range(0, t, tq):
            l = q0 + tq
            rows, keys = pl.ds(q0, tq), pl.ds(0, l)
            _, vjp = jax.vjp(functools.partial(_attn_block, q0=q0), qn_ref[rows, :], qr_ref[rows, :],
                             kn_ref[keys, :], kr_ref[keys, :], v_ref[keys, :])
            dqn, dqr, dkn, dkr, dv = vjp(do_ref[rows, :].astype(F32))
            dqn_ref[rows, :] = dqn.astype(dqn_ref.dtype)
            dqr_ref[rows, :] = dqr.astype(dqr_ref.dtype)
            akn_ref[keys, :] += dkn.astype(F32)
            av_ref[keys, :] += dv.astype(F32)
            akr_ref[keys, :] += dkr.astype(F32)
        dkn_ref[...] = akn_ref[...].astype(dkn_ref.dtype)
        dv_ref[...] = av_ref[...].astype(dv_ref.dtype)
        first = pl.program_id(0) == 0

        @pl.when(first)
        def _():
            dkr_ref[...] = akr_ref[...]

        @pl.when(jnp.logical_not(first))
        def _():
            dkr_ref[...] += akr_ref[...]

    pair = lambda wd: pl.BlockSpec((t, wd), lambda p: (0, p))
    shared = pl.BlockSpec((t, LANES), lambda p: (0, 0))
    sds = jax.ShapeDtypeStruct
    return pl.pallas_call(
        body, grid=(pairs,),
        in_specs=[pair(pw), pair(LANES), pair(pw), shared, pair(pw), pair(pw)],
        out_specs=[pair(pw), pair(LANES), pair(pw), pair(pw), shared],
        out_shape=[sds((t, w), BF), sds((t, pairs * LANES), F32), sds((t, w), BF), sds((t, w), BF), sds((t, LANES), F32)],
        scratch_shapes=[pltpu.VMEM((t, pw), F32), pltpu.VMEM((t, pw), F32), pltpu.VMEM((t, LANES), F32)],
        compiler_params=_params(("arbitrary",)), name="attn_bwd",
    )(qn, qr, kn, kr, v, do)


def _ew(name, fn, ins, o_dtypes, max_bytes=2 ** 21):
    r, c = ins[0].shape
    tr = r
    if r * c * 4 > max_bytes:
        tr = max(16, (max_bytes // (c * 4)) // 16 * 16)
        while r % tr:
            tr -= 16
    n_in = len(ins)

    def body(*refs):
        outs = fn(*[x[...] for x in refs[:n_in]])
        for o_ref, o in zip(refs[n_in:], outs):
            o_ref[...] = o.astype(o_ref.dtype)

    spec = pl.BlockSpec((tr, c), lambda i: (i, 0))
    return pl.pallas_call(
        body, grid=(r // tr,), in_specs=[spec] * n_in, out_specs=[spec] * len(o_dtypes),
        out_shape=[jax.ShapeDtypeStruct((r, c), dt) for dt in o_dtypes],
        compiler_params=_params(("parallel",)), name=name,
    )(*ins)


def _adamw_math(w, g, m, v):
    m = ADAM_B1 * m + (1.0 - ADAM_B1) * g
    v = ADAM_B2 * v + (1.0 - ADAM_B2) * jnp.square(g)
    m_hat = m / (1.0 - ADAM_B1 ** ADAM_STEP)
    v_hat = v / (1.0 - ADAM_B2 ** ADAM_STEP)
    delta = -ADAM_LR * (m_hat / (jnp.sqrt(v_hat) + ADAM_EPS) + ADAM_WD * w)
    return delta, m, v


def _adamw(name, w, g, m, v):
    shape = w.shape
    to2 = lambda a: a.reshape(-1, shape[-1]) if a.ndim > 1 else a.reshape(1, -1)
    d, nm, nv = _ew("adamw_" + name, _adamw_math, [to2(w), to2(g), to2(m), to2(v)], (F32, F32, F32))
    return d.reshape(shape), nm.reshape(shape), nv.reshape(shape)


def _adamw_layers(name, w, g_layers, m, v):
    n_l, r, n = w.shape
    tr = r
    while tr * n * 4 > 2 ** 21 and tr % 16 == 0:
        tr //= 2
    outs = []
    for l, g in enumerate(g_layers):
        def body(*refs):
            w_ref, g_ref, m_ref, v_ref = refs[:4]
            og, od, om, ov = refs[-4:]
            gval = g_ref[...].astype(F32)
            d, nm, nv = _adamw_math(w_ref[...], gval, m_ref[...], v_ref[...])
            og[...] = gval
            od[...] = d
            om[...] = nm
            ov[...] = nv

        lay = pl.BlockSpec((None, tr, n), lambda i, l=l: (l, i, 0))
        outs = pl.pallas_call(
            body, grid=(r // tr,),
            in_specs=[lay, pl.BlockSpec((tr, n), lambda i: (i, 0)), lay, lay] + [ANY] * len(outs),
            out_specs=[lay] * 4, out_shape=[jax.ShapeDtypeStruct(w.shape, F32)] * 4,
            input_output_aliases={4 + k: k for k in range(len(outs))},
            compiler_params=_params(("parallel",)), name="adamw_%s_%d" % (name, l),
        )(w, g, m, v, *outs)
    return outs


def _rope_tables(positions, t):
    half = QK_ROPE // 2
    inv = 1.0 / (ROPE_THETA ** (jnp.arange(0, QK_ROPE, 2, dtype=F32) / QK_ROPE))
    inv_l = jnp.tile(inv, LANES // half).reshape(1, LANES)
    sign = jnp.tile(jnp.concatenate([-jnp.ones((half,), F32), jnp.ones((half,), F32)]), LANES // QK_ROPE).reshape(1, LANES)

    def body(p_ref, inv_ref, sg_ref, c_ref, s_ref):
        ang = p_ref[...].astype(F32) * inv_ref[...]
        c_ref[...] = jnp.cos(ang)
        s_ref[...] = jnp.sin(ang) * sg_ref[...]

    return pl.pallas_call(body, out_shape=[jax.ShapeDtypeStruct((t, LANES), F32)] * 2, name="rope_tables")(
        positions.reshape(t, 1), inv_l, sign)


def _loss_grad(h, target):
    t, d = h.shape
    tr = min(TR, t)

    def body(h_ref, y_ref, dh_ref, l_ref):
        err = h_ref[...] - y_ref[...]
        dh_ref[...] = err * (1.0 / d)
        part = 0.5 * jnp.sum(jnp.mean(jnp.square(err), axis=-1, keepdims=True), axis=0, keepdims=True)
        first = pl.program_id(0) == 0

        @pl.when(first)
        def _():
            l_ref[...] = part

        @pl.when(jnp.logical_not(first))
        def _():
            l_ref[...] += part

    row = pl.BlockSpec((tr, d), lambda i: (i, 0))
    return pl.pallas_call(
        body, grid=(t // tr,), in_specs=[row, row],
        out_specs=[row, pl.BlockSpec((1, 1), lambda i: (0, 0))],
        out_shape=[jax.ShapeDtypeStruct((t, d), F32), jax.ShapeDtypeStruct((1, 1), F32)],
        compiler_params=_params(("arbitrary",)), name="loss_grad",
    )(h, target)


def _sum_devices(g):
    def fn(*parts):
        acc = parts[0]
        for p in parts[1:]:
            acc = acc + p
        return (acc,)
    return _ew("sum_devices", fn, [g[i] for i in range(g.shape[0])], (F32,))[0]


def _place():
    x, y, c = lax.axis_index("x"), lax.axis_index("y"), lax.axis_index("c")
    return x, y, c, [(1 - x, y), (x, 1 - y), (1 - x, 1 - y)]


def _all_gather_small(name, v):
    r, n = v.shape

    def body(x_ref, out_ref, send_sems, recv_sems, local_sem):
        x, y, c, chips = _place()
        me, sibling = (x, y, c), (x, y, 1 - c)

        def rows(px, py, pc):
            return out_ref.at[4 * px + 2 * py + pc]

        def copy(k, block, to, src=None):
            return pltpu.make_async_remote_copy(
                src_ref=rows(*block) if src is None else src, dst_ref=rows(*block), send_sem=send_sems.at[k],
                recv_sem=recv_sems.at[k], device_id=to, device_id_type=MESH)

        mine = pltpu.make_async_copy(x_ref, rows(*me), local_sem)
        mine.start()
        first = [copy(0, me, sibling, src=x_ref)]
        first += [copy(1 + j, me, (*chip, c), src=x_ref) for j, chip in enumerate(chips)]
        for cp in first:
            cp.start()
        passed = [copy(4 + j, (*chip, c), sibling) for j, chip in enumerate(chips)]
        for j, chip in enumerate(chips):
            copy(1 + j, (*chip, c), me).wait_recv()
            passed[j].start()
        copy(0, sibling, me).wait_recv()
        for j, chip in enumerate(chips):
            copy(4 + j, (*chip, 1 - c), me).wait_recv()
        for cp in first + passed:
            cp.wait_send()
        mine.wait()

    return pl.pallas_call(
        body, out_shape=jax.ShapeDtypeStruct((N_DEV, r, n), v.dtype),
        in_specs=[pl.BlockSpec(memory_space=pltpu.VMEM)], out_specs=pl.BlockSpec(memory_space=pltpu.VMEM),
        scratch_shapes=[pltpu.SemaphoreType.DMA((7,)), pltpu.SemaphoreType.DMA((7,)), pltpu.SemaphoreType.DMA],
        compiler_params=pltpu.CompilerParams(vmem_limit_bytes=VMEM_LIMIT), name=name,
    )(v)


def _cast_into_slot(name, w, layer, idx, after=None):
    _, r, n = w.shape
    tr = r
    while tr * n * 4 > 2 ** 21 and tr % 32 == 0:
        tr //= 2
    order = [] if after is None else [after]

    def body(idx_ref, w_ref, *rest):
        o_ref = rest[-1]
        o_ref[...] = w_ref[...].astype(o_ref.dtype)

    return pl.pallas_call(
        body,
        grid_spec=pltpu.PrefetchScalarGridSpec(
            num_scalar_prefetch=1, grid=(r // tr,),
            in_specs=[pl.BlockSpec((None, tr, n), lambda i, idx_ref: (layer, i, 0))] + [ANY] * len(order),
            out_specs=pl.BlockSpec((None, tr, n), lambda i, idx_ref: (idx_ref[1], i, 0))),
        out_shape=jax.ShapeDtypeStruct((N_CHIPS, r, n), BF),
        compiler_params=_params(("parallel",)), name="cast_" + name,
    )(idx, w, *order)


def _gather_weights(bufs):
    n_w = len(bufs)

    def body(*refs):
        outs = refs[n_w:2 * n_w]
        send_sems, recv_sems = refs[2 * n_w:]
        x, y, c, chips = _place()

        def copy(i, s, chip, which, to):
            h = outs[i].shape[1] // 2
            blk = outs[i].at[2 * chip[0] + chip[1], pl.ds(which * h, h)]
            return pltpu.make_async_remote_copy(
                src_ref=blk, dst_ref=blk, send_sem=send_sems.at[6 * i + s], recv_sem=recv_sems.at[6 * i + s],
                device_id=to, device_id_type=MESH)

        sends = []
        for i in range(n_w):
            for j, chip in enumerate(chips):
                cp = copy(i, j, (x, y), c, (*chip, c))
                cp.start()
                sends.append(cp)
        for i in range(n_w):
            for j, chip in enumerate(chips):
                copy(i, j, chip, c, (x, y, c)).wait_recv()
                cp = copy(i, 3 + j, chip, c, (x, y, 1 - c))
                cp.start()
                sends.append(cp)
        for i in range(n_w):
            for j, chip in enumerate(chips):
                copy(i, 3 + j, chip, 1 - c, (x, y, c)).wait_recv()
        for cp in sends:
            cp.wait_send()

    return pl.pallas_call(
        body, in_specs=[ANY] * n_w, out_specs=[ANY] * n_w,
        out_shape=[jax.ShapeDtypeStruct(b.shape, b.dtype) for b in bufs],
        input_output_aliases={i: i for i in range(n_w)},
        scratch_shapes=[pltpu.SemaphoreType.DMA((6 * n_w,)), pltpu.SemaphoreType.DMA((6 * n_w,))],
        name="gather_weights",
    )(*bufs)


def _swap_other_half(name, grads, after=None):
    n_w = len(grads)
    order = [] if after is None else [after]
    n_in = n_w + len(order)

    def body(*refs):
        ins, outs = refs[:n_w], refs[n_in:n_in + n_w]
        send_sems, recv_sems = refs[n_in + n_w:]
        x, y, c, _ = _place()
        cps = []
        for i in range(n_w):
            h = ins[i].shape[1] // 2
            cp = pltpu.make_async_remote_copy(
                src_ref=ins[i].at[:, pl.ds((1 - c) * h, h), :], dst_ref=outs[i], send_sem=send_sems.at[i],
                recv_sem=recv_sems.at[i], device_id=(x, y, 1 - c), device_id_type=MESH)
            cp.start()
            cps.append(cp)
        for cp in cps:
            cp.wait()

    return pl.pallas_call(
        body, in_specs=[ANY] * n_in, out_specs=[ANY] * n_w,
        out_shape=[jax.ShapeDtypeStruct((g.shape[0], g.shape[1] // 2, g.shape[2]), g.dtype) for g in grads],
        scratch_shapes=[pltpu.SemaphoreType.DMA((n_w,)), pltpu.SemaphoreType.DMA((n_w,))],
        name=name,
    )(*grads, *order)


def _add_my_half(name, g, s, idx, o_dtype):
    nc, r, n = g.shape
    h = r // 2
    tr = h
    while tr * n * 4 > 2 ** 21 and tr % 32 == 0:
        tr //= 2
    nb = h // tr

    def body(idx_ref, g_ref, s_ref, o_ref, own_ref):
        val = (g_ref[...].astype(F32) + s_ref[...].astype(F32)).astype(o_ref.dtype)
        o_ref[...] = val

        @pl.when(pl.program_id(1) == idx_ref[1])
        def _():
            own_ref[...] = val

    return pl.pallas_call(
        body,
        grid_spec=pltpu.PrefetchScalarGridSpec(
            num_scalar_prefetch=1, grid=(nb, nc),
            in_specs=[pl.BlockSpec((None, tr, n), lambda i, j, idx_ref: (j, idx_ref[0] * nb + i, 0)),
                      pl.BlockSpec((None, tr, n), lambda i, j, idx_ref: (j, i, 0))],
            out_specs=[pl.BlockSpec((None, tr, n), lambda i, j, idx_ref: (j, i, 0)),
                       pl.BlockSpec((None, tr, n), lambda i, j, idx_ref: (idx_ref[1], i, 0))]),
        out_shape=[jax.ShapeDtypeStruct((nc, h, n), o_dtype)] * 2,
        compiler_params=_params(("parallel", "arbitrary")), name=name,
    )(idx, g, s)


def _scatter_partials(parts, bufs):
    n_w = len(parts)

    def body(*refs):
        ins, outs = refs[:n_w], refs[2 * n_w:3 * n_w]
        send_sems, recv_sems = refs[3 * n_w:]
        x, y, c, chips = _place()
        k = 2 * x + y
        sends = []
        for i in range(n_w):
            for j, chip in enumerate(chips):
                cp = pltpu.make_async_remote_copy(
                    src_ref=ins[i].at[2 * chip[0] + chip[1]], dst_ref=outs[i].at[k], send_sem=send_sems.at[3 * i + j],
                    recv_sem=recv_sems.at[3 * i + j], device_id=(*chip, c), device_id_type=MESH)
                cp.start()
                sends.append(cp)
        for i in range(n_w):
            for j, chip in enumerate(chips):
                blk = outs[i].at[2 * chip[0] + chip[1]]
                pltpu.make_async_remote_copy(
                    src_ref=blk, dst_ref=blk, send_sem=send_sems.at[3 * i + j], recv_sem=recv_sems.at[3 * i + j],
                    device_id=(*chip, c), device_id_type=MESH).wait_recv()
        for cp in sends:
            cp.wait_send()

    return pl.pallas_call(
        body, in_specs=[ANY] * (2 * n_w), out_specs=[ANY] * n_w,
        out_shape=[jax.ShapeDtypeStruct(b.shape, b.dtype) for b in bufs],
        input_output_aliases={n_w + i: i for i in range(n_w)},
        scratch_shapes=[pltpu.SemaphoreType.DMA((3 * n_w,)), pltpu.SemaphoreType.DMA((3 * n_w,))],
        name="scatter_partials",
    )(*parts, *bufs)


def _sum_chips(name, q, idx):
    nc, h, n = q.shape
    tr = h
    while tr * n * 4 > 2 ** 20 and tr % 32 == 0:
        tr //= 2
    nb = h // tr

    def body(idx_ref, q_ref, o_ref):
        acc = q_ref[0].astype(F32)
        for j in range(1, nc):
            acc = acc + q_ref[j].astype(F32)
        o_ref[...] = acc

    return pl.pallas_call(
        body,
        grid_spec=pltpu.PrefetchScalarGridSpec(
            num_scalar_prefetch=1, grid=(nb,),
            in_specs=[pl.BlockSpec((nc, tr, n), lambda i, idx_ref: (0, i, 0))],
            out_specs=pl.BlockSpec((tr, n), lambda i, idx_ref: (idx_ref[0] * nb + i, 0))),
        out_shape=jax.ShapeDtypeStruct((2 * h, n), F32),
        compiler_params=_params(("parallel",)), name=name,
    )(idx, q)


def _join_halves(name, bufs):
    n_w = len(bufs)

    def body(*refs):
        outs = refs[n_w:2 * n_w]
        send_sems, recv_sems = refs[2 * n_w:]
        x, y, c, _ = _place()

        def copy(i, which):
            h = outs[i].shape[0] // 2
            rows = outs[i].at[pl.ds(which * h, h)]
            return pltpu.make_async_remote_copy(
                src_ref=rows, dst_ref=rows, send_sem=send_sems.at[i], recv_sem=recv_sems.at[i],
                device_id=(x, y, 1 - c), device_id_type=MESH)

        cps = [copy(i, c) for i in range(n_w)]
        for cp in cps:
            cp.start()
        for i, cp in enumerate(cps):
            cp.wait_send()
            copy(i, 1 - c).wait_recv()

    return pl.pallas_call(
        body, in_specs=[ANY] * n_w, out_specs=[ANY] * n_w,
        out_shape=[jax.ShapeDtypeStruct(b.shape, b.dtype) for b in bufs],
        input_output_aliases={i: i for i in range(n_w)},
        scratch_shapes=[pltpu.SemaphoreType.DMA((n_w,)), pltpu.SemaphoreType.DMA((n_w,))],
        name=name,
    )(*bufs)


HBM_SPEC = pl.BlockSpec(memory_space=pltpu.HBM)
SEM_SPEC = pl.BlockSpec(memory_space=pltpu.SEMAPHORE)
VMEM_SPEC = pl.BlockSpec(memory_space=pltpu.VMEM)
SIDE_EFFECT = pltpu.SideEffectType.DATAFLOW_SIDE_EFFECTING


def _in_hbm(arrays):
    return [pltpu.with_memory_space_constraint(a, pltpu.HBM) for a in arrays]


def _ici_gather_copy(ref, i, j, chip, send_sems, recv_sems):
    x, y, c, _ = _place()
    h = ref.shape[1] // 2
    mine = ref.at[2 * x + y, pl.ds(c * h, h)]
    return pltpu.make_async_remote_copy(
        src_ref=mine, dst_ref=mine, send_sem=send_sems.at[3 * i + j], recv_sem=recv_sems.at[3 * i + j],
        device_id=(*chip, c), device_id_type=MESH)


def _gather_start(name, bufs, after=None):
    n_w = len(bufs)
    order = [] if after is None else [after]

    def body(*refs):
        ins, token = refs[:n_w], refs[-1]
        send_sems, recv_sems = refs[n_w + len(order)], refs[n_w + len(order) + 1]
        chips = _place()[3]
        for i in range(n_w):
            for j, chip in enumerate(chips):
                _ici_gather_copy(ins[i], i, j, chip, send_sems, recv_sems).start()
        token[...] = jnp.zeros_like(token)

    res = pl.pallas_call(
        body, name=name, in_specs=[HBM_SPEC] * n_w + [ANY] * len(order),
        out_shape=(pltpu.SemaphoreType.DMA((3 * n_w,)), pltpu.SemaphoreType.DMA((3 * n_w,)),
                   *[pltpu.HBM(b.shape, b.dtype) for b in bufs], jax.ShapeDtypeStruct((8, LANES), F32)),
        out_specs=(SEM_SPEC, SEM_SPEC, *[HBM_SPEC] * n_w, VMEM_SPEC),
        input_output_aliases={i: 2 + i for i in range(n_w)},
        compiler_params=pltpu.CompilerParams(has_side_effects=SIDE_EFFECT),
    )(*_in_hbm(bufs), *order)
    return res[0], res[1], list(res[2:2 + n_w]), res[-1]


def _gather_wait(name, send_sems, recv_sems, bufs, after):
    n_w = len(bufs)

    def body(*refs):
        ins, s_sems, r_sems = refs[:n_w], refs[n_w], refs[n_w + 1]
        chips = _place()[3]
        for i in range(n_w):
            for j, chip in enumerate(chips):
                cp = _ici_gather_copy(ins[i], i, j, chip, s_sems, r_sems)
                cp.wait_send()
                cp.wait_recv()

    return pl.pallas_call(
        body, name=name, in_specs=[HBM_SPEC] * n_w + [SEM_SPEC, SEM_SPEC, ANY],
        out_shape=[pltpu.HBM(b.shape, b.dtype) for b in bufs], out_specs=[HBM_SPEC] * n_w,
        input_output_aliases={i: i for i in range(n_w)},
        compiler_params=pltpu.CompilerParams(has_side_effects=SIDE_EFFECT),
    )(*bufs, send_sems, recv_sems, after)


def _forward_to_sibling(name, bufs):
    n_w = len(bufs)

    def body(*refs):
        outs = refs[n_w:2 * n_w]
        send_sems, recv_sems = refs[2 * n_w:]
        x, y, c, chips = _place()

        def copy(i, j, chip, which):
            h = outs[i].shape[1] // 2
            blk = outs[i].at[2 * chip[0] + chip[1], pl.ds(which * h, h)]
            return pltpu.make_async_remote_copy(
                src_ref=blk, dst_ref=blk, send_sem=send_sems.at[3 * i + j], recv_sem=recv_sems.at[3 * i + j],
                device_id=(x, y, 1 - c), device_id_type=MESH)

        sends = [copy(i, j, chip, c) for i in range(n_w) for j, chip in enumerate(chips)]
        for cp in sends:
            cp.start()
        for i in range(n_w):
            for j, chip in enumerate(chips):
                copy(i, j, chip, 1 - c).wait_recv()
        for cp in sends:
            cp.wait_send()

    return pl.pallas_call(
        body, in_specs=[ANY] * n_w, out_specs=[ANY] * n_w,
        out_shape=[jax.ShapeDtypeStruct(b.shape, b.dtype) for b in bufs],
        input_output_aliases={i: i for i in range(n_w)},
        scratch_shapes=[pltpu.SemaphoreType.DMA((3 * n_w,)), pltpu.SemaphoreType.DMA((3 * n_w,))],
        name=name,
    )(*bufs)


def _ici_scatter_copy(part, buf, i, j, chip, send_sems, recv_sems):
    x, y, c, _ = _place()
    return pltpu.make_async_remote_copy(
        src_ref=part.at[2 * chip[0] + chip[1]], dst_ref=buf.at[2 * x + y], send_sem=send_sems.at[3 * i + j],
        recv_sem=recv_sems.at[3 * i + j], device_id=(*chip, c), device_id_type=MESH)


def _ici_scatter_landing(buf, i, j, chip, send_sems, recv_sems):
    x, y, c, _ = _place()
    blk = buf.at[2 * chip[0] + chip[1]]
    return pltpu.make_async_remote_copy(
        src_ref=blk, dst_ref=blk, send_sem=send_sems.at[3 * i + j], recv_sem=recv_sems.at[3 * i + j],
        device_id=(*chip, c), device_id_type=MESH)


def _scatter_start(name, parts, bufs):
    n_w = len(parts)

    def body(*refs):
        ps, bs = refs[:n_w], refs[n_w:2 * n_w]
        send_sems, recv_sems, token = refs[2 * n_w], refs[2 * n_w + 1], refs[-1]
        chips = _place()[3]
        for i in range(n_w):
            for j, chip in enumerate(chips):
                _ici_scatter_copy(ps[i], bs[i], i, j, chip, send_sems, recv_sems).start()
        token[...] = jnp.zeros_like(token)

    both = list(parts) + list(bufs)
    res = pl.pallas_call(
        body, name=name, in_specs=[HBM_SPEC] * (2 * n_w),
        out_shape=(pltpu.SemaphoreType.DMA((3 * n_w,)), pltpu.SemaphoreType.DMA((3 * n_w,)),
                   *[pltpu.HBM(b.shape, b.dtype) for b in both], jax.ShapeDtypeStruct((8, LANES), F32)),
        out_specs=(SEM_SPEC, SEM_SPEC, *[HBM_SPEC] * (2 * n_w), VMEM_SPEC),
        input_output_aliases={i: 2 + i for i in range(2 * n_w)},
        compiler_params=pltpu.CompilerParams(has_side_effects=SIDE_EFFECT),
    )(*_in_hbm(both))
    return res[0], res[1], list(res[2:2 + n_w]), list(res[2 + n_w:2 + 2 * n_w]), res[-1]


def _scatter_wait(name, send_sems, recv_sems, parts, bufs, after):
    n_w = len(parts)
    after = list(after) if isinstance(after, (list, tuple)) else [after]

    def body(*refs):
        ps, bs = refs[:n_w], refs[n_w:2 * n_w]
        s_sems, r_sems = refs[2 * n_w], refs[2 * n_w + 1]
        chips = _place()[3]
        for i in range(n_w):
            for j, chip in enumerate(chips):
                _ici_scatter_copy(ps[i], bs[i], i, j, chip, s_sems, r_sems).wait_send()
                _ici_scatter_landing(bs[i], i, j, chip, s_sems, r_sems).wait_recv()

    both = list(parts) + list(bufs)
    res = pl.pallas_call(
        body, name=name, in_specs=[HBM_SPEC] * (2 * n_w) + [SEM_SPEC, SEM_SPEC] + [ANY] * len(after),
        out_shape=[pltpu.HBM(b.shape, b.dtype) for b in both], out_specs=[HBM_SPEC] * (2 * n_w),
        input_output_aliases={i: i for i in range(2 * n_w)},
        compiler_params=pltpu.CompilerParams(has_side_effects=SIDE_EFFECT),
    )(*both, send_sems, recv_sems, *after)
    return list(res[n_w:])


def _swap_copy(g, land, i, send_sems, recv_sems):
    x, y, c, _ = _place()
    h = g.shape[1] // 2
    return pltpu.make_async_remote_copy(
        src_ref=g.at[:, pl.ds((1 - c) * h, h), :], dst_ref=land, send_sem=send_sems.at[i], recv_sem=recv_sems.at[i],
        device_id=(x, y, 1 - c), device_id_type=MESH)


def _swap_start(name, grads):
    n_w = len(grads)
    lands = [lax.empty((g.shape[0], g.shape[1] // 2, g.shape[2]), g.dtype) for g in grads]

    def body(*refs):
        gs, ls = refs[:n_w], refs[n_w:2 * n_w]
        send_sems, recv_sems, token = refs[2 * n_w], refs[2 * n_w + 1], refs[-1]
        for i in range(n_w):
            _swap_copy(gs[i], ls[i], i, send_sems, recv_sems).start()
        token[...] = jnp.zeros_like(token)

    both = list(grads) + lands
    res = pl.pallas_call(
        body, name=name, in_specs=[HBM_SPEC] * (2 * n_w),
        out_shape=(pltpu.SemaphoreType.DMA((n_w,)), pltpu.SemaphoreType.DMA((n_w,)),
                   *[pltpu.HBM(b.shape, b.dtype) for b in both], jax.ShapeDtypeStruct((8, LANES), F32)),
        out_specs=(SEM_SPEC, SEM_SPEC, *[HBM_SPEC] * (2 * n_w), VMEM_SPEC),
        input_output_aliases={i: 2 + i for i in range(2 * n_w)},
        compiler_params=pltpu.CompilerParams(has_side_effects=SIDE_EFFECT),
    )(*_in_hbm(both))
    return (res[0], res[1], list(res[2:2 + n_w]), list(res[2 + n_w:2 + 2 * n_w])), res[-1]


def _swap_wait(name, swapped, after):
    send_sems, recv_sems, grads, lands = swapped
    n_w = len(grads)

    def body(*refs):
        gs, ls = refs[:n_w], refs[n_w:2 * n_w]
        s_sems, r_sems = refs[2 * n_w], refs[2 * n_w + 1]
        for i in range(n_w):
            cp = _swap_copy(gs[i], ls[i], i, s_sems, r_sems)
            cp.wait_send()
            cp.wait_recv()

    both = list(grads) + list(lands)
    res = pl.pallas_call(
        body, name=name, in_specs=[HBM_SPEC] * (2 * n_w) + [SEM_SPEC, SEM_SPEC, ANY],
        out_shape=[pltpu.HBM(b.shape, b.dtype) for b in both], out_specs=[HBM_SPEC] * (2 * n_w),
        input_output_aliases={i: i for i in range(2 * n_w)},
        compiler_params=pltpu.CompilerParams(has_side_effects=SIDE_EFFECT),
    )(*both, send_sems, recv_sems, after)
    return list(res[:n_w]), list(res[n_w:])


def _reduce_scatter_begin(tag, grads, idx, after=None, swapped=None):
    if swapped is None:
        from_sibling = _swap_other_half("swap_other_half_" + tag, grads, after)
    else:
        grads, from_sibling = _swap_wait("swap_wait_" + tag, swapped, after)
    pairs = [_add_my_half("add_my_half_%s_%d" % (tag, i), g, s, idx, BF)
             for i, (g, s) in enumerate(zip(grads, from_sibling))]
    s_sems, r_sems, parts, bufs, token = _scatter_start("scatter_start_" + tag, [p[0] for p in pairs],
                                                        [p[1] for p in pairs])
    return (tag, s_sems, r_sems, parts, bufs, idx), token


def _reduce_scatter_end(state, after):
    tag, s_sems, r_sems, parts, bufs, idx = state
    received = _scatter_wait("scatter_wait_" + tag, s_sems, r_sems, parts, bufs, after)
    halves = [_sum_chips("sum_chips_%s_%d" % (tag, i), q, idx) for i, q in enumerate(received)]
    return _join_halves("join_halves_" + tag, halves)


def _relu2_epilogue(acc):
    r = jnp.maximum(acc, 0.0)
    return (r * r,)


def _relu2_bwd_epilogue(acc, s):
    return (acc * (2.0 * jnp.sqrt(s.astype(F32))),)


WIDE = (2048, 512, 2048)
DEEP = (2048, 512, 2048)


def _add_epilogue(acc, other):
    return (acc + other,)


def kernel(x, c, positions, w_ada_mix, b_ada_mix, w_ada_mlp, b_ada_mlp, g_pre_mix, g_post_mix, g_pre_mlp, g_post_mlp, conv_w_in, conv_b_in, conv_dw, conv_dw_b, conv_ln_g, conv_ln_b, conv_w_out, conv_b_out, w_ada_kv, b_ada_kv, g_kv, w_dkv, g_ckv, w_kr, w_uk, w_uv, w_dq, g_cq, w_uq, w_o, mlp_w_up, mlp_w_down, loss_target, m_w_ada_mix, m_b_ada_mix, m_w_ada_mlp, m_b_ada_mlp, m_g_pre_mix, m_g_post_mix, m_g_pre_mlp, m_g_post_mlp, m_conv_w_in, m_conv_b_in, m_conv_dw, m_conv_dw_b, m_conv_ln_g, m_conv_ln_b, m_conv_w_out, m_conv_b_out, m_w_ada_kv, m_b_ada_kv, m_g_kv, m_w_dkv, m_g_ckv, m_w_kr, m_w_uk, m_w_uv, m_w_dq, m_g_cq, m_w_uq, m_w_o, m_mlp_w_up, m_mlp_w_down, v_w_ada_mix, v_b_ada_mix, v_w_ada_mlp, v_b_ada_mlp, v_g_pre_mix, v_g_post_mix, v_g_pre_mlp, v_g_post_mlp, v_conv_w_in, v_conv_b_in, v_conv_dw, v_conv_dw_b, v_conv_ln_g, v_conv_ln_b, v_conv_w_out, v_conv_b_out, v_w_ada_kv, v_b_ada_kv, v_g_kv, v_w_dkv, v_g_ckv, v_w_kr, v_w_uk, v_w_uv, v_w_dq, v_g_cq, v_w_uq, v_w_o, v_mlp_w_up, v_mlp_w_down):
    weights = dict(w_ada_mix=w_ada_mix, b_ada_mix=b_ada_mix, w_ada_mlp=w_ada_mlp, b_ada_mlp=b_ada_mlp, g_pre_mix=g_pre_mix, g_post_mix=g_post_mix, g_pre_mlp=g_pre_mlp, g_post_mlp=g_post_mlp, conv_w_in=conv_w_in, conv_b_in=conv_b_in, conv_dw=conv_dw, conv_dw_b=conv_dw_b, conv_ln_g=conv_ln_g, conv_ln_b=conv_ln_b, conv_w_out=conv_w_out, conv_b_out=conv_b_out, w_ada_kv=w_ada_kv, b_ada_kv=b_ada_kv, g_kv=g_kv, w_dkv=w_dkv, g_ckv=g_ckv, w_kr=w_kr, w_uk=w_uk, w_uv=w_uv, w_dq=w_dq, g_cq=g_cq, w_uq=w_uq, w_o=w_o, mlp_w_up=mlp_w_up, mlp_w_down=mlp_w_down)
    m_in = dict(w_ada_mix=m_w_ada_mix, b_ada_mix=m_b_ada_mix, w_ada_mlp=m_w_ada_mlp, b_ada_mlp=m_b_ada_mlp, g_pre_mix=m_g_pre_mix, g_post_mix=m_g_post_mix, g_pre_mlp=m_g_pre_mlp, g_post_mlp=m_g_post_mlp, conv_w_in=m_conv_w_in, conv_b_in=m_conv_b_in, conv_dw=m_conv_dw, conv_dw_b=m_conv_dw_b, conv_ln_g=m_conv_ln_g, conv_ln_b=m_conv_ln_b, conv_w_out=m_conv_w_out, conv_b_out=m_conv_b_out, w_ada_kv=m_w_ada_kv, b_ada_kv=m_b_ada_kv, g_kv=m_g_kv, w_dkv=m_w_dkv, g_ckv=m_g_ckv, w_kr=m_w_kr, w_uk=m_w_uk, w_uv=m_w_uv, w_dq=m_w_dq, g_cq=m_g_cq, w_uq=m_w_uq, w_o=m_w_o, mlp_w_up=m_mlp_w_up, mlp_w_down=m_mlp_w_down)
    v_in = dict(w_ada_mix=v_w_ada_mix, b_ada_mix=v_b_ada_mix, w_ada_mlp=v_w_ada_mlp, b_ada_mlp=v_b_ada_mlp, g_pre_mix=v_g_pre_mix, g_post_mix=v_g_post_mix, g_pre_mlp=v_g_pre_mlp, g_post_mlp=v_g_post_mlp, conv_w_in=v_conv_w_in, conv_b_in=v_conv_b_in, conv_dw=v_conv_dw, conv_dw_b=v_conv_dw_b, conv_ln_g=v_conv_ln_g, conv_ln_b=v_conv_ln_b, conv_w_out=v_conv_w_out, conv_b_out=v_conv_b_out, w_ada_kv=v_w_ada_kv, b_ada_kv=v_b_ada_kv, g_kv=v_g_kv, w_dkv=v_w_dkv, g_ckv=v_g_ckv, w_kr=v_w_kr, w_uk=v_w_uk, w_uv=v_w_uv, w_dq=v_w_dq, g_cq=v_g_cq, w_uq=v_w_uq, w_o=v_w_o, mlp_w_up=v_mlp_w_up, mlp_w_down=v_mlp_w_down)
    grads, last_group = _step_grads(x, c, positions, loss_target, weights)
    loss, grad_x = grads.pop("loss"), grads.pop("x")
    names = list(weights)
    upd = {}
    for n in names:
        if isinstance(grads.get(n), list):
            grads[n], *upd[n] = _adamw_layers(n, weights[n], grads[n], m_in[n], v_in[n])
        elif n in grads:
            upd[n] = _adamw(n, weights[n], grads[n], m_in[n], v_in[n])
    grads.update(last_group([upd[n][0] for n in ("w_ada_mix", "w_ada_mlp", "w_ada_kv", "mlp_w_down")]))
    upd.update({n: _adamw(n, weights[n], grads[n], m_in[n], v_in[n]) for n in names if n not in upd})
    return (loss, grad_x, *[grads[n] for n in names], *[upd[n][0] for n in names], *[upd[n][1] for n in names],
            *[upd[n][2] for n in names])


def _step_grads(x, c, positions, loss_target, w):
    xi, yi, ci = lax.axis_index("x"), lax.axis_index("y"), lax.axis_index("c")
    chip = 2 * xi + yi
    dev = 2 * chip + ci
    place_idx = jnp.stack([ci, chip]).astype(jnp.int32)
    t, d = x.shape[1], x.shape[2]
    dl = d // N_CHIPS
    r_kv, r_q = w["w_dkv"].shape[1], w["w_dq"].shape[2]
    n_ada, n_kvada = w["w_ada_mix"].shape[2], w["w_ada_kv"].shape[1]
    heads_l = w["w_uq"].shape[2] // (QK_NOPE + QK_ROPE)
    assert dl == r_kv == r_q and dl % LANES == 0 and heads_l * N_CHIPS % 2 == 0

    def chip_cols(a, width):
        return lax.dynamic_slice_in_dim(a, chip * width, width, axis=a.ndim - 1)

    uq = w["w_uq"][0].reshape(r_q, heads_l, QK_NOPE + QK_ROPE)
    uq_nope, uq_rope = uq[:, :, :QK_NOPE].reshape(r_q, -1), uq[:, :, QK_NOPE:].reshape(r_q, -1)
    kr_pad = jnp.pad(w["w_kr"], ((0, 0), (0, LANES - QK_ROPE)))
    small = jnp.concatenate([w["w_dkv"], kr_pad, w["w_dq"][0], w["w_uk"], w["w_uv"], uq_nope, uq_rope], axis=1)
    widths = [r_kv, LANES, r_q, w["w_uk"].shape[1], w["w_uv"].shape[1], uq_nope.shape[1], uq_rope.shape[1]]
    so = [0]
    for wd in widths:
        so.append(so[-1] + wd)
    big = dict(conv_w_in=(w["conv_w_in"], 0), conv_w_out=(w["conv_w_out"], 0), w_o=(w["w_o"], 0),
               up0=(w["mlp_w_up"], 0), up1=(w["mlp_w_up"], 1), down0=(w["mlp_w_down"], 0),
               down1=(w["mlp_w_down"], 1), small=(small[None], 0))
    groups = dict(conv_in=["conv_w_in"], conv_out=["conv_w_out"], up0=["up0"], down0=["down0"],
                  mla=["small", "w_o"], up1=["up1"], down1=["down1"])
    gathered, pending = {}, {}

    def start_gathers(gnames, after):
        token = after
        for gname in gnames:
            slots = [_cast_into_slot(n, *big[n], place_idx, after=token) for n in groups[gname]]
            s_sems, r_sems, bufs, token = _gather_start("gather_start_" + gname, slots, after=token)
            pending[gname] = (s_sems, r_sems, bufs)
        return token

    def finish_gather(gname, after):
        s_sems, r_sems, bufs = pending[gname]
        bufs = _gather_wait("gather_wait_" + gname, s_sems, r_sems, bufs, after)
        gathered.update(zip(groups[gname], _forward_to_sibling("forward_" + gname, bufs)))

    rowl = lambda a: a.reshape(-1, a.shape[2])
    w_up_f = lambda l: gathered["up%d" % l]
    w_down_f = lambda l: rowl(gathered["down%d" % l])

    pack_rows = [w["conv_dw"][0], w["conv_dw_b"], w["conv_ln_g"], w["conv_ln_b"], w["conv_b_out"],
                 w["conv_b_in"].reshape(2, dl), c.reshape(N_CHIPS, dl)]
    pack_rows = [jnp.pad(p, ((0, (-p.shape[0]) % 8), (0, 0))) for p in pack_rows]
    po = [0]
    for p in pack_rows:
        po.append(po[-1] + p.shape[0])
    packs = _all_gather_small("gather_params", jnp.concatenate(pack_rows, axis=0))
    by_chip = packs[0::2]

    def full_width(r0, nr):
        return jnp.transpose(by_chip[:, r0:r0 + nr, :], (1, 0, 2)).reshape(nr, d)

    dw_full, dwb_full = full_width(po[0], CONV_WIDTH), full_width(po[1], 1)
    lng_full, lnb_full, bout_full = full_width(po[2], 1), full_width(po[3], 1), full_width(po[4], 1)
    bin_full = by_chip[:, po[5]:po[5] + 2, :].reshape(1, 2 * d)
    c_all = packs[:, po[6]:po[6] + N_CHIPS, :].reshape(N_DEV, d)

    sc_all = _ew("silu_c", lambda a: (a * jax.nn.sigmoid(a),), [c_all], (F32,))[0]
    ada_w = [(w["w_ada_mix"], 0), (w["w_ada_mix"], 1), (w["w_ada_mlp"], 0), (w["w_ada_mlp"], 1),
             (w["w_ada_kv"][None], 0)]
    ada_b = [chip_cols(w["b_ada_mix"][0:1], n_ada), chip_cols(w["b_ada_mix"][1:2], n_ada),
             chip_cols(w["b_ada_mlp"][0:1], n_ada), chip_cols(w["b_ada_mlp"][1:2], n_ada),
             chip_cols(w["b_ada_kv"].reshape(1, -1), n_kvada)]
    mods = [_mm_nn("ada_fwd_%d" % i, sc_all, wi, "row", extras=(jnp.broadcast_to(bi, (N_DEV, bi.shape[1])),),
                   epilogue=_add_epilogue, w_layer=li) for i, ((wi, li), bi) in enumerate(zip(ada_w, ada_b))]
    mods_all = _all_gather_small("gather_mods", jnp.concatenate(mods, axis=1))
    started = start_gathers(("conv_in", "conv_out", "up0", "down0", "mla", "up1", "down1"), mods_all)[0:1, 0:1]
    mine = lax.dynamic_index_in_dim(mods_all[0::2], dev, axis=1, keepdims=False)
    offs = [0]
    for m_ in mods:
        offs.append(offs[-1] + m_.shape[1])
    mod_vec = [mine[:, offs[i]:offs[i + 1]].reshape(1, -1) for i in range(5)]
    split3 = lambda v: (v[:, :d], v[:, d:2 * d], v[:, 2 * d:])
    mix = [split3(mod_vec[0]), split3(mod_vec[1])]
    mlp = [split3(mod_vec[2]), split3(mod_vec[3])]
    kv_shift, kv_scale = mod_vec[4][:, :d], mod_vec[4][:, d:]

    cos_t, sin_t = _rope_tables(positions, t)
    vec = lambda a: a.reshape(1, -1)
    gpm, gqm = w["g_pre_mix"], w["g_post_mix"]
    gpl, gql = w["g_pre_mlp"], w["g_post_mlp"]
    h0 = x[0]
    after_token = lambda v, token: v + token[0:1, 0:1]

    def mlp_fwd(l, h):
        sh, sc, _ = mlp[l]
        (hn,) = _rw_fwd("mlp_pre_%d" % l, _f_pre, [h], [vec(gpl[l]), sh, sc], (BF,))
        finish_gather("up%d" % l, hn)
        s = _mm_nn("mlp_up_%d" % l, hn, w_up_f(l), "col", o_dtypes=(BF,), epilogue=_relu2_epilogue, tiles=WIDE)
        finish_gather("down%d" % l, s)
        y = _mm_nn("mlp_down_%d" % l, s, w_down_f(l), "row", tiles=DEEP)
        return hn, s, y

    def post_fwd(name, h, y, gate, g, bias=None):
        if bias is None:
            return _rw_fwd(name, lambda h_, y_, gt, g_: (h_ + _f_post(y_, gt, g_)[0],), [h, y], [gate, g], (F32,))[0]
        return _rw_fwd(name, lambda h_, y_, b_, gt, g_: (h_ + _f_post_bias(y_, b_, gt, g_)[0],), [h, y],
                       [bias, gate, g], (F32,))[0]

    (hn0,) = _rw_fwd("conv_pre", _f_pre, [h0], [vec(gpm[0]), mix[0][0] + started, mix[0][1]], (BF,))
    finish_gather("conv_in", hn0)
    w_in_f = gathered["conv_w_in"]
    z0 = _mm_nn("conv_in", hn0, w_in_f, "col")
    (u0,) = _rw_fwd("conv_glu", _f_glu, [z0], [bin_full], (F32,))
    uc0 = _conv_fwd(u0, dw_full, dwb_full)
    (n0,) = _rw_fwd("conv_ln", _f_ln_silu, [uc0], [lng_full, lnb_full], (BF,))
    finish_gather("conv_out", n0)
    w_out_f = rowl(gathered["conv_w_out"])
    y0 = _mm_nn("conv_out", n0, w_out_f, "row")
    h1 = post_fwd("conv_post", h0, y0, mix[0][2], vec(gqm[0]), bias=bout_full)
    hn1, s1, y1 = mlp_fwd(0, h1)
    h2 = post_fwd("mlp_post_0", h1, y1, mlp[0][2], vec(gql[0]))

    finish_gather("mla", h2)
    gs = gathered["small"]
    w_dkvkr = rowl(gs[:, :, so[0]:so[2]])
    w_dq_f = rowl(gs[:, :, so[2]:so[3]])
    w_uk_f, w_uv_f = gs[:, :, so[3]:so[4]], gs[:, :, so[4]:so[5]]
    w_qn_f, w_qr_f = gs[:, :, so[5]:so[6]], gs[:, :, so[6]:so[7]]
    w_o_f = rowl(gathered["w_o"])
    kvn, hn2 = _rw_fwd("mla_pre", _f_pre2, [h2], [vec(w["g_kv"]), kv_shift, kv_scale, vec(gpm[1]), mix[1][0], mix[1][1]],
                       (BF, BF))
    pre_kv = _mm_nn("mla_dkv", kvn, w_dkvkr, "row")
    pre_q = _mm_nn("mla_dq", hn2, w_dq_f, "row")
    f_lat = _make_f_lat(r_kv)
    c_kv, kr, c_q = _rw_fwd("mla_latent", f_lat, [pre_kv, pre_q, cos_t, sin_t], [vec(w["g_ckv"]), vec(w["g_cq"][0])],
                            (BF, BF, BF))
    kn = _mm_nn("mla_uk", c_kv, w_uk_f, "col", o_dtypes=(BF,))
    vv = _mm_nn("mla_uv", c_kv, w_uv_f, "col", o_dtypes=(BF,))
    qn = _mm_nn("mla_uq_nope", c_q, w_qn_f, "col", o_dtypes=(BF,))
    qr_pre = _mm_nn("mla_uq_rope", c_q, w_qr_f, "col")
    (qr,) = _rw_fwd("mla_q_rope", _f_rope, [qr_pre, cos_t, sin_t], [], (BF,))
    att = _attn_fwd(qn, qr, kn, kr, vv)
    y2 = _mm_nn("mla_o", att, w_o_f, "row")
    h3 = post_fwd("mla_post", h2, y2, mix[1][2], vec(gqm[1]))
    hn3, s3, y3 = mlp_fwd(1, h3)
    h4 = post_fwd("mlp_post_1", h3, y3, mlp[1][2], vec(gql[1]))

    dh4, loss_part = _loss_grad(h4, loss_target[0])
    loss = lax.psum(loss_part[0, 0], ("x", "y", "c"))

    gw = {}
    gvec = {}

    dw_mm = functools.partial(_mm_tn, o_dtype=BF)

    def mlp_bwd(l, h_in, hn, s, y, dh, token=None):
        sh, sc, gate = mlp[l]
        if token is not None:
            gate = after_token(gate, token)
        (dy,), (dgate, dgq) = _rw_bwd("mlp_post_bwd_%d" % l, _f_post, [y], [gate, vec(gql[l])], [dh], [BF])
        gw["down%d" % l] = dw_mm("mlp_down_dw_%d" % l, s, dy, "row", tiles=(512, 2048, 2048))
        da = _mm_nt("mlp_down_dx_%d" % l, dy, w_down_f(l), "row", o_dtypes=(BF,), extras=(s,),
                    epilogue=_relu2_bwd_epilogue, tiles=WIDE)
        gw["up%d" % l] = dw_mm("mlp_up_dw_%d" % l, hn, da, "col", tiles=WIDE)
        dhn = _mm_nt("mlp_up_dx_%d" % l, da, w_up_f(l), "col", tiles=DEEP)
        (dh_in,), (dgp, dsh, dsc) = _rw_bwd("mlp_pre_bwd_%d" % l, _f_pre, [h_in], [vec(gpl[l]), sh, sc], [dhn], [F32],
                                            add_rows={0: dh})
        gvec["mlp%d" % l] = (dsh, dsc, dgate)
        gvec["g_pre_mlp%d" % l], gvec["g_post_mlp%d" % l] = dgp, dgq
        return dh_in

    chunked = lambda a: a.reshape(N_CHIPS, a.shape[0] // N_CHIPS, a.shape[1])
    to_chunks = lambda a: a if a.ndim == 3 else chunked(a)
    red = {}

    def swap_begin(tag, names):
        return _swap_start("swap_start_" + tag, [to_chunks(gw[n]) for n in names])

    def reduce_begin(tag, names, after=None, swapped=None):
        state, token = _reduce_scatter_begin(tag, [to_chunks(gw[n]) for n in names], place_idx, after, swapped)
        return (names, state), token

    def reduce_end(handle, after):
        names, state = handle
        red.update(zip(names, _reduce_scatter_end(state, after)))

    dh3 = mlp_bwd(1, h3, hn3, s3, y3, dh4)
    sw1, token = swap_begin("r1", ["up1", "down1"])

    (dy2,), (dgate, dgq) = _rw_bwd("mla_post_bwd", _f_post, [y2], [after_token(mix[1][2], token), vec(gqm[1])],
                                   [dh3], [BF])
    gvec["g_post_mix1"] = dgq
    gw["w_o"] = dw_mm("mla_o_dw", att, dy2, "row")
    datt = _mm_nt("mla_o_dx", dy2, w_o_f, "row", o_dtypes=(BF,))
    rs1, token = reduce_begin("r1", ["up1", "down1"], after=datt, swapped=sw1)
    dqn, dqr, dkn, dvv, dkr = _attn_bwd(qn, qr, kn, kr + token[0:1, 0:1].astype(BF), vv, datt)
    (dqr_pre,), _ = _rw_bwd("mla_q_rope_bwd", _f_rope, [qr_pre, cos_t, sin_t], [], [dqr], [BF, None, None])
    g_qn = dw_mm("mla_uq_nope_dw", c_q, dqn, "col")
    g_qr = dw_mm("mla_uq_rope_dw", c_q, dqr_pre, "col")
    dc_q = _mm_nt("mla_uq_nope_dx", dqn, w_qn_f, "col")
    dc_q = _mm_nt("mla_uq_rope_dx", dqr_pre, w_qr_f, "col", extras=(dc_q,), epilogue=_add_epilogue)
    g_uk = dw_mm("mla_uk_dw", c_kv, dkn, "col")
    g_uv = dw_mm("mla_uv_dw", c_kv, dvv, "col")
    dc_kv = _mm_nt("mla_uk_dx", dkn, w_uk_f, "col")
    dc_kv = _mm_nt("mla_uv_dx", dvv, w_uv_f, "col", extras=(dc_kv,), epilogue=_add_epilogue)
    (dpre_kv, dpre_q), (dg_ckv, dg_cq) = _rw_bwd(
        "mla_latent_bwd", f_lat, [pre_kv, pre_q, cos_t, sin_t], [vec(w["g_ckv"]), vec(w["g_cq"][0])],
        [dc_kv, dkr, dc_q], [BF, BF, None, None])
    gvec["g_ckv"], gvec["g_cq"] = dg_ckv, dg_cq
    g_dkvkr = dw_mm("mla_dkv_dw", kvn, dpre_kv, "row")
    g_dq = dw_mm("mla_dq_dw", hn2, dpre_q, "row")
    dkvn = _mm_nt("mla_dkv_dx", dpre_kv, w_dkvkr, "row")
    dhn2 = _mm_nt("mla_dq_dx", dpre_q, w_dq_f, "row")
    (dh2,), (dg_kv, dkvsh, dkvsc, dgp, dsh, dsc) = _rw_bwd(
        "mla_pre_bwd", _f_pre2, [h2], [vec(w["g_kv"]), kv_shift, kv_scale, vec(gpm[1]), mix[1][0], mix[1][1]],
        [dkvn, dhn2], [F32], add_rows={0: dh3})
    gvec["mix1"] = (dsh, dsc, dgate)
    gvec["kv"] = (dkvsh, dkvsc)
    gvec["g_kv"], gvec["g_pre_mix1"] = dg_kv, dgp
    gw["small"] = jnp.concatenate([chunked(g_dkvkr), chunked(g_dq), g_uk, g_uv, g_qn, g_qr], axis=2)
    reduce_end(rs1, dh2)
    rs2, token = reduce_begin("r2", ["small", "w_o"])

    dh1 = mlp_bwd(0, h1, hn1, s1, y1, dh2, token=token)
    reduce_end(rs2, dh1)
    rs3, token = reduce_begin("r3", ["up0", "down0"])

    (dy0,), (dbout, dgate, dgq) = _rw_bwd("conv_post_bwd", _f_post_bias, [y0],
                                          [bout_full, after_token(mix[0][2], token), vec(gqm[0])], [dh1], [BF])
    gvec["g_post_mix0"] = dgq
    gw["conv_w_out"] = dw_mm("conv_out_dw", n0, dy0, "row")
    dn0 = _mm_nt("conv_out_dx", dy0, w_out_f, "row")
    (duc0,), (dlng, dlnb) = _rw_bwd("conv_ln_bwd", _f_ln_silu, [uc0], [lng_full, lnb_full], [dn0], [F32])
    du0, ddw = _conv_bwd(u0, duc0, dw_full)
    (dz0,), (dbin,) = _rw_bwd("conv_glu_bwd", _f_glu, [z0], [bin_full], [du0], [BF])
    gw["conv_w_in"] = dw_mm("conv_in_dw", hn0, dz0, "col")
    dhn0 = _mm_nt("conv_in_dx", dz0, w_in_f, "col")
    (dx,), (dgp, dsh, dsc) = _rw_bwd("conv_pre_bwd", _f_pre, [h0], [vec(gpm[0]), mix[0][0], mix[0][1]], [dhn0], [F32],
                                     add_rows={0: dh1})
    gvec["mix0"] = (dsh, dsc, dgate)
    gvec["g_pre_mix0"] = dgp
    reduce_end(rs3, dx)

    vec_list = [*gvec["mix0"], *gvec["mix1"], *gvec["mlp0"], *gvec["mlp1"], *gvec["kv"],
                gvec["g_pre_mix0"], gvec["g_pre_mix1"], gvec["g_post_mix0"], gvec["g_post_mix1"],
                gvec["g_pre_mlp0"], gvec["g_pre_mlp1"], gvec["g_post_mlp0"], gvec["g_post_mlp1"],
                gvec["g_kv"], gvec["g_ckv"], gvec["g_cq"], dbin, dlng, dlnb, dbout, ddw.reshape(1, -1)]
    vo = [0]
    for v_ in vec_list:
        vo.append(vo[-1] + v_.shape[1])
    vec_pad = (-vo[-1]) % (8 * LANES)
    n_vec = vo[-1] + vec_pad
    flat = jnp.concatenate(vec_list + [jnp.zeros((1, vec_pad), F32)], axis=1).reshape(8, n_vec // 8)
    all_vecs = _all_gather_small("gather_vector_grads", flat).reshape(N_DEV, 8, n_vec // 8)
    rs4, token = reduce_begin("r4", ["conv_w_in", "conv_w_out"], after=all_vecs)
    all_vecs = after_token(all_vecs, token)
    summed = _sum_devices(all_vecs).reshape(1, n_vec)
    per_dev = all_vecs.reshape(N_DEV, n_vec)
    seg = lambda a, i: a[:, vo[i]:vo[i + 1]]

    out = {"loss": loss, "x": dx.reshape(x.shape)}
    dm_mix = [jnp.concatenate([seg(per_dev, 3 * l + i) for i in range(3)], axis=1) for l in range(2)]
    dm_mlp = [jnp.concatenate([seg(per_dev, 6 + 3 * l + i) for i in range(3)], axis=1) for l in range(2)]
    dm_kv = jnp.concatenate([seg(per_dev, 12), seg(per_dev, 13)], axis=1)
    ada_dw = lambda name, dm, width: _mm_tn(name, sc_all, chip_cols(dm, width), "row")
    out["w_ada_mix"] = [ada_dw("ada_mix_dw_%d" % l, dm_mix[l], n_ada) for l in range(2)]
    out["w_ada_mlp"] = [ada_dw("ada_mlp_dw_%d" % l, dm_mlp[l], n_ada) for l in range(2)]
    out["w_ada_kv"] = ada_dw("ada_kv_dw", dm_kv, n_kvada)
    sum_seg = lambda i: seg(summed, i)
    out["b_ada_mix"] = jnp.concatenate([jnp.concatenate([sum_seg(3 * l + i) for i in range(3)], axis=1) for l in range(2)], axis=0)
    out["b_ada_mlp"] = jnp.concatenate([jnp.concatenate([sum_seg(6 + 3 * l + i) for i in range(3)], axis=1) for l in range(2)], axis=0)
    out["b_ada_kv"] = jnp.concatenate([sum_seg(12), sum_seg(13)], axis=1).reshape(-1)
    out["g_pre_mix"] = jnp.concatenate([sum_seg(14), sum_seg(15)], axis=0)
    out["g_post_mix"] = jnp.concatenate([sum_seg(16), sum_seg(17)], axis=0)
    out["g_pre_mlp"] = jnp.concatenate([sum_seg(18), sum_seg(19)], axis=0)
    out["g_post_mlp"] = jnp.concatenate([sum_seg(20), sum_seg(21)], axis=0)
    out["g_kv"] = sum_seg(22).reshape(-1)
    out["g_ckv"] = sum_seg(23).reshape(-1)
    out["g_cq"] = sum_seg(24)
    out["conv_b_in"] = chip_cols(sum_seg(25), 2 * dl)
    out["conv_ln_g"] = chip_cols(sum_seg(26), dl)
    out["conv_ln_b"] = chip_cols(sum_seg(27), dl)
    out["conv_b_out"] = chip_cols(sum_seg(28), dl)
    ddw_sum = chip_cols(sum_seg(29).reshape(CONV_WIDTH + 1, d), dl)
    out["conv_dw"] = ddw_sum[:CONV_WIDTH].reshape(1, CONV_WIDTH, dl)
    out["conv_dw_b"] = ddw_sum[CONV_WIDTH:]

    def last_group(after):
        reduce_end(rs4, after)
        return {"conv_w_in": red["conv_w_in"][None], "conv_w_out": red["conv_w_out"][None]}

    out["w_o"] = red["w_o"][None]
    out["mlp_w_up"] = [red["up0"], red["up1"]]
    out["mlp_w_down"] = [red["down0"], red["down1"]]
    rs = red["small"]
    piece = lambda i: rs[:, so[i]:so[i + 1]]
    out["w_dkv"] = piece(0)
    out["w_kr"] = piece(1)[:, :QK_ROPE]
    out["w_dq"] = piece(2)[None]
    out["w_uk"], out["w_uv"] = piece(3), piece(4)
    out["w_uq"] = jnp.concatenate([piece(5).reshape(r_q, heads_l, QK_NOPE), piece(6).reshape(r_q, heads_l, QK_ROPE)],
                                  axis=2).reshape(1, r_q, -1)
    return out, last_group
```

```python
import functools

import jax
import jax.numpy as jnp
from jax import lax
from jax.experimental import pallas as pl
from jax.experimental.pallas import tpu as pltpu

F32 = jnp.float32
BF = jnp.bfloat16
MXU_DTYPE = BF

EPS = 1e-6
NEG = -1e30
ROPE_THETA = 10000.0
QK_NOPE = 128
QK_ROPE = 64
V_HEAD = 128
CONV_WIDTH = 31
ADAM_LR, ADAM_B1, ADAM_B2, ADAM_EPS, ADAM_WD, ADAM_STEP = 0.001, 0.9, 0.999, 1e-08, 0.01, 10

N_CHIPS = 4
N_DEV = 8
LANES = 128
VMEM_LIMIT = 56 * 2 ** 20
MESH = pl.DeviceIdType.MESH
ANY = pl.BlockSpec(memory_space=pl.ANY)


def _params(sem=None):
    return pltpu.CompilerParams(dimension_semantics=sem, vmem_limit_bytes=VMEM_LIMIT)


def _tile(n, pref, unit=LANES):
    if n <= pref:
        return n
    t = (pref // unit) * unit
    while t > unit and n % t:
        t -= unit
    assert n % t == 0, (n, pref)
    return t


def _dg(a, b, ca, cb):
    return lax.dot_general(a.astype(MXU_DTYPE), b.astype(MXU_DTYPE), (((ca,), (cb,)), ((), ())),
                           preferred_element_type=F32)


@jax.custom_vjp
def _dot_nn(a, b):
    return _dg(a, b, 1, 0)


def _dot_nn_fwd(a, b):
    return _dg(a, b, 1, 0), (a, b)


def _dot_nn_bwd(res, g):
    a, b = res
    return _dg(g, b, 1, 1).astype(a.dtype), _dg(a, g, 0, 0).astype(b.dtype)


_dot_nn.defvjp(_dot_nn_fwd, _dot_nn_bwd)


@jax.custom_vjp
def _dot_nt(a, b):
    return _dg(a, b, 1, 1)


def _dot_nt_fwd(a, b):
    return _dg(a, b, 1, 1), (a, b)


def _dot_nt_bwd(res, g):
    a, b = res
    return _dg(g, b, 1, 0).astype(a.dtype), _dg(g, a, 0, 0).astype(b.dtype)


_dot_nt.defvjp(_dot_nt_fwd, _dot_nt_bwd)


def _matmul(name, a, b, *, ca, cb, grid, a_blk, a_map, b_blk, b_map, o_shape, o_dtypes, o_blk, o_map,
            extras=(), epilogue=None):
    nk = grid[-1]
    n_ex, n_out = len(extras), len(o_dtypes)

    def body(*refs):
        a_ref, b_ref = refs[0], refs[1]
        ex_refs = refs[2:2 + n_ex]
        out_refs = refs[2 + n_ex:2 + n_ex + n_out]
        kk = pl.program_id(len(grid) - 1)

        def finish(acc):
            outs = (acc,) if epilogue is None else epilogue(acc, *[r[...] for r in ex_refs])
            for r, o in zip(out_refs, outs):
                r[...] = o.astype(r.dtype)

        part = _dg(a_ref[...], b_ref[...], ca, cb)
        if nk == 1:
            finish(part)
        else:
            acc_ref = refs[-1]

            @pl.when(kk == 0)
            def _():
                acc_ref[...] = part

            @pl.when(kk > 0)
            def _():
                acc_ref[...] += part

            @pl.when(kk == nk - 1)
            def _():
                finish(acc_ref[...])

    o_spec = pl.BlockSpec(o_blk, o_map)
    acc_shape = tuple(d for d in o_blk if d is not None)
    res = pl.pallas_call(
        body, grid=grid,
        in_specs=[pl.BlockSpec(a_blk, a_map), pl.BlockSpec(b_blk, b_map)] + [o_spec] * n_ex,
        out_specs=[o_spec] * n_out,
        out_shape=[jax.ShapeDtypeStruct(o_shape, dt) for dt in o_dtypes],
        scratch_shapes=[] if nk == 1 else [pltpu.VMEM(acc_shape, F32)],
        compiler_params=_params(("parallel",) * (len(grid) - 1) + ("arbitrary",)),
        name=name,
    )(a, b, *extras)
    return res[0] if n_out == 1 else res


TM, TN, TK = 2048, 512, 2048


def _mm_nn(name, a, w, layout, o_dtypes=(F32,), extras=(), epilogue=None, tiles=(TM, TN, TK), w_layer=None):
    m, k = a.shape
    tm, tk = _tile(m, tiles[0], 8), _tile(k, tiles[2])
    if layout == "row":
        n = w.shape[-1]
        tn = _tile(n, tiles[1])
        grid = (m // tm, n // tn, k // tk)
        b_blk, b_map = (tk, tn), (lambda i, j, kk: (kk, j))
        if w_layer is not None:
            b_blk, b_map = (None, tk, tn), (lambda i, j, kk: (w_layer, kk, j))
    else:
        nl = w.shape[2]
        n = nl * w.shape[0]
        tn = _tile(nl, tiles[1])
        nb = nl // tn
        grid = (m // tm, n // tn, k // tk)
        b_blk, b_map = (None, tk, tn), (lambda i, j, kk: (j // nb, kk, j % nb))
    return _matmul(name, a, w, ca=1, cb=0, grid=grid, a_blk=(tm, tk), a_map=lambda i, j, kk: (i, kk),
                   b_blk=b_blk, b_map=b_map, o_shape=(m, n), o_dtypes=o_dtypes, o_blk=(tm, tn),
                   o_map=lambda i, j, kk: (i, j), extras=extras, epilogue=epilogue)


def _mm_nt(name, g, w, layout, o_dtypes=(F32,), extras=(), epilogue=None, tiles=(TM, TN, TK)):
    m, n = g.shape
    tm, tn = _tile(m, tiles[0], 8), None
    if layout == "row":
        k = w.shape[0]
        tn = _tile(n, tiles[2])
        tk = _tile(k, tiles[1])
        grid = (m // tm, k // tk, n // tn)
        b_blk, b_map = (tk, tn), (lambda i, j, kk: (j, kk))
    else:
        k, nl = w.shape[1], w.shape[2]
        tn = _tile(nl, tiles[2])
        nb = nl // tn
        tk = _tile(k, tiles[1])
        grid = (m // tm, k // tk, n // tn)
        b_blk, b_map = (None, tk, tn), (lambda i, j, kk: (kk // nb, j, kk % nb))
    return _matmul(name, g, w, ca=1, cb=1, grid=grid, a_blk=(tm, tn), a_map=lambda i, j, kk: (i, kk),
                   b_blk=b_blk, b_map=b_map, o_shape=(m, k), o_dtypes=o_dtypes, o_blk=(tm, tk),
                   o_map=lambda i, j, kk: (i, j), extras=extras, epilogue=epilogue)


def _mm_tn(name, x, g, layout, o_dtype=F32, tiles=(TM, TN, TK)):
    t, k = x.shape
    n = g.shape[1]
    tt = _tile(t, tiles[2], 8)
    tk = _tile(k, tiles[0])
    if layout == "row":
        tn = _tile(n, tiles[1])
        o_shape, o_blk, o_map = (k, n), (tk, tn), (lambda i, j, kk: (i, j))
    else:
        nl = n // N_CHIPS
        tn = _tile(nl, tiles[1])
        nb = nl // tn
        o_shape, o_blk, o_map = (N_CHIPS, k, nl), (None, tk, tn), (lambda i, j, kk: (j // nb, i, j % nb))
    grid = (k // tk, n // tn, t // tt)
    return _matmul(name, x, g, ca=0, cb=0, grid=grid, a_blk=(tt, tk), a_map=lambda i, j, kk: (kk, i),
                   b_blk=(tt, tn), b_map=lambda i, j, kk: (kk, j), o_shape=o_shape, o_dtypes=(o_dtype,),
                   o_blk=o_blk, o_map=o_map)


TR = 256


def _rw_fwd(name, fn, rows, vecs, o_dtypes, tr=TR):
    t = rows[0].shape[0]
    tr = min(tr, t)
    n_r, n_v = len(rows), len(vecs)
    o_sds = jax.eval_shape(fn, *[jax.ShapeDtypeStruct((tr, r.shape[1]), r.dtype) for r in rows],
                           *[jax.ShapeDtypeStruct(v.shape, v.dtype) for v in vecs])

    def body(*refs):
        outs = fn(*[r[...] for r in refs[:n_r + n_v]])
        for r, o in zip(refs[n_r + n_v:], outs):
            r[...] = o.astype(r.dtype)

    res = pl.pallas_call(
        body, grid=(t // tr,),
        in_specs=[pl.BlockSpec((tr, r.shape[1]), lambda i: (i, 0)) for r in rows]
        + [pl.BlockSpec(v.shape, lambda i: (0, 0)) for v in vecs],
        out_specs=[pl.BlockSpec((tr, o.shape[1]), lambda i: (i, 0)) for o in o_sds],
        out_shape=[jax.ShapeDtypeStruct((t, o.shape[1]), dt) for o, dt in zip(o_sds, o_dtypes)],
        compiler_params=_params(("parallel",)), name=name,
    )(*rows, *vecs)
    return res


def _rw_bwd(name, fn, rows, vecs, cots, row_grad_dtypes, add_rows=None, tr=TR):
    t = rows[0].shape[0]
    tr = min(tr, t)
    add_rows = add_rows or {}
    n_r, n_v, n_c = len(rows), len(vecs), len(cots)
    diff = [i for i, dt in enumerate(row_grad_dtypes) if dt is not None]
    adds = [add_rows[i] for i in diff if i in add_rows]
    n_a = len(adds)

    def body(*refs):
        rvals = [r[...] for r in refs[:n_r]]
        vvals = [r[...] for r in refs[n_r:n_r + n_v]]
        c_refs = refs[n_r + n_v:n_r + n_v + n_c]
        a_refs = list(refs[n_r + n_v + n_c:n_r + n_v + n_c + n_a])
        o_refs = refs[n_r + n_v + n_c + n_a:]

        def f(*d):
            full = list(rvals)
            for i, x in zip(diff, d[:len(diff)]):
                full[i] = x
            return fn(*full, *d[len(diff):])

        outs, vjp = jax.vjp(f, *[rvals[i] for i in diff], *vvals)
        grads = vjp(tuple(c[...].astype(o.dtype) for c, o in zip(c_refs, outs)))
        for n, i in enumerate(diff):
            gr = grads[n].astype(F32)
            if i in add_rows:
                gr = gr + a_refs.pop(0)[...].astype(F32)
            o_refs[n][...] = gr.astype(o_refs[n].dtype)
        first = pl.program_id(0) == 0
        for r, gv in zip(o_refs[len(diff):], grads[len(diff):]):
            @pl.when(first)
            def _(r=r, gv=gv):
                r[...] = gv

            @pl.when(jnp.logical_not(first))
            def _(r=r, gv=gv):
                r[...] += gv

    row_spec = lambda a: pl.BlockSpec((tr, a.shape[1]), lambda i: (i, 0))
    vec_spec = lambda a: pl.BlockSpec(a.shape, lambda i: (0, 0))
    res = pl.pallas_call(
        body, grid=(t // tr,),
        in_specs=[row_spec(r) for r in rows] + [vec_spec(v) for v in vecs] + [row_spec(c) for c in cots]
        + [row_spec(a) for a in adds],
        out_specs=[row_spec(rows[i]) for i in diff] + [vec_spec(v) for v in vecs],
        out_shape=[jax.ShapeDtypeStruct(rows[i].shape, row_grad_dtypes[i]) for i in diff]
        + [jax.ShapeDtypeStruct(v.shape, F32) for v in vecs],
        compiler_params=_params(("arbitrary",)), name=name,
    )(*rows, *vecs, *cots, *adds)
    return res[:len(diff)], res[len(diff):]


def _rms(x, g):
    return x * lax.rsqrt(jnp.mean(x * x, axis=-1, keepdims=True) + EPS) * g


def _f_pre(h, g, shift, scale):
    return (_rms(h, g) * (1.0 + scale) + shift,)


def _f_pre2(h, g1, sh1, sc1, g2, sh2, sc2):
    return _f_pre(h, g1, sh1, sc1) + _f_pre(h, g2, sh2, sc2)


def _f_post(y, gate, g):
    return (gate * _rms(y, g),)


def _f_post_bias(y, bias, gate, g):
    return (gate * _rms(y + bias, g),)


def _f_glu(z, bias):
    z = z + bias
    half = z.shape[1] // 2
    return (z[:, :half] * jax.nn.sigmoid(z[:, half:]),)


def _f_ln_silu(u, g, b):
    mu = jnp.mean(u, axis=-1, keepdims=True)
    var = jnp.mean(jnp.square(u - mu), axis=-1, keepdims=True)
    y = (u - mu) * lax.rsqrt(var + EPS) * g + b
    return (y * jax.nn.sigmoid(y),)


def _rope_raw(x, cos, sin):
    n = x.shape[1]
    reps = n // LANES
    if reps > 1:
        cos = jnp.concatenate([cos] * reps, axis=1)
        sin = jnp.concatenate([sin] * reps, axis=1)
    lane = lax.broadcasted_iota(jnp.int32, x.shape, 1)
    half = QK_ROPE // 2
    partner = jnp.where((lane % QK_ROPE) < half, pltpu.roll(x, n - half, 1), pltpu.roll(x, half, 1))
    return x * cos + partner * sin


@jax.custom_vjp
def _rope(x, cos, sin):
    return _rope_raw(x, cos, sin)


def _rope_fwd(x, cos, sin):
    return _rope_raw(x, cos, sin), (cos, sin)


def _rope_bwd(res, g):
    cos, sin = res
    return _rope_raw(g, cos, -sin), jnp.zeros_like(cos), jnp.zeros_like(sin)


_rope.defvjp(_rope_fwd, _rope_bwd)


def _make_f_lat(r_kv):
    def f(pre_kv, pre_q, cos, sin, g_ckv, g_cq):
        c_kv = _rms(pre_kv[:, :r_kv], g_ckv)
        kr = _rope(pre_kv[:, r_kv:], cos, sin)
        return c_kv, kr, _rms(pre_q, g_cq)
    return f


def _f_rope(x, cos, sin):
    return (_rope(x, cos, sin),)


CONV_TC = 128
CONV_TT = 256
PADR = 32


def _conv_fwd(u, dw, dw_b):
    t, d = u.shape
    tc, tt = min(CONV_TC, d), min(CONV_TT, t)
    off = PADR - (CONV_WIDTH - 1)

    def body(u_ref, w_ref, b_ref, o_ref, pad_ref):
        pad_ref[pl.ds(0, PADR), :] = jnp.zeros((PADR, tc), F32)
        pad_ref[pl.ds(PADR, t), :] = u_ref[...]
        for t0 in range(0, t, tt):
            acc = jnp.zeros((tt, tc), F32) + b_ref[...]
            for j in range(CONV_WIDTH):
                acc = acc + pad_ref[pl.ds(t0 + off + j, tt), :] * w_ref[pl.ds(j, 1), :]
            o_ref[pl.ds(t0, tt), :] = acc

    return pl.pallas_call(
        body, grid=(d // tc,),
        in_specs=[pl.BlockSpec((t, tc), lambda j: (0, j)), pl.BlockSpec((CONV_WIDTH, tc), lambda j: (0, j)),
                  pl.BlockSpec((1, tc), lambda j: (0, j))],
        out_specs=pl.BlockSpec((t, tc), lambda j: (0, j)),
        out_shape=jax.ShapeDtypeStruct((t, d), F32),
        scratch_shapes=[pltpu.VMEM((t + PADR, tc), F32)],
        compiler_params=_params(("parallel",)), name="conv_fwd",
    )(u, dw, dw_b)


def _conv_bwd(u, duc, dw):
    t, d = u.shape
    tc, tt = min(CONV_TC, d), min(CONV_TT, t)
    off = PADR - (CONV_WIDTH - 1)

    def body(u_ref, g_ref, w_ref, du_ref, dwt_ref, padu_ref, padg_ref):
        padu_ref[pl.ds(0, PADR), :] = jnp.zeros((PADR, tc), F32)
        padu_ref[pl.ds(PADR, t), :] = u_ref[...]
        padg_ref[pl.ds(t, PADR), :] = jnp.zeros((PADR, tc), F32)
        padg_ref[pl.ds(0, t), :] = g_ref[...]
        for t0 in range(0, t, tt):
            acc = jnp.zeros((tt, tc), F32)
            for j in range(CONV_WIDTH):
                acc = acc + padg_ref[pl.ds(t0 + (CONV_WIDTH - 1) - j, tt), :] * w_ref[pl.ds(j, 1), :]
            du_ref[pl.ds(t0, tt), :] = acc
        for j in range(CONV_WIDTH):
            acc = jnp.zeros((tt, tc), F32)
            for t0 in range(0, t, tt):
                acc = acc + g_ref[pl.ds(t0, tt), :] * padu_ref[pl.ds(t0 + off + j, tt), :]
            dwt_ref[pl.ds(j, 1), :] = jnp.sum(acc, axis=0, keepdims=True)
        acc = jnp.zeros((tt, tc), F32)
        for t0 in range(0, t, tt):
            acc = acc + g_ref[pl.ds(t0, tt), :]
        dwt_ref[pl.ds(CONV_WIDTH, 1), :] = jnp.sum(acc, axis=0, keepdims=True)

    col = lambda r: pl.BlockSpec((r, tc), lambda j: (0, j))
    return pl.pallas_call(
        body, grid=(d // tc,),
        in_specs=[col(t), col(t), col(CONV_WIDTH)],
        out_specs=[col(t), col(CONV_WIDTH + 1)],
        out_shape=[jax.ShapeDtypeStruct((t, d), F32), jax.ShapeDtypeStruct((CONV_WIDTH + 1, d), F32)],
        scratch_shapes=[pltpu.VMEM((t + PADR, tc), F32), pltpu.VMEM((t + PADR, tc), F32)],
        compiler_params=_params(("parallel",)), name="conv_bwd",
    )(u, duc, dw)


@jax.custom_vjp
def _swap_halves(x):
    return pltpu.roll(x, LANES // 2, 1)


_swap_halves.defvjp(lambda x: (pltpu.roll(x, LANES // 2, 1), None), lambda _, g: (pltpu.roll(g, LANES // 2, 1),))


def _attn_block(qn, qr, kn, kr, v, q0):
    scale = (QK_NOPE + QK_ROPE) ** -0.5
    lane = lax.broadcasted_iota(jnp.int32, kr.shape, 1)
    kr_a = kr * (lane < QK_ROPE).astype(kr.dtype)
    kr_b = _swap_halves(kr_a)
    outs = []
    for hh, kr_h in ((0, kr_a), (1, kr_b)):
        sl = slice(hh * QK_NOPE, (hh + 1) * QK_NOPE)
        s = _dot_nt(jnp.concatenate([qn[:, sl], qr], axis=1), jnp.concatenate([kn[:, sl], kr_h], axis=1)) * scale
        row = lax.broadcasted_iota(jnp.int32, s.shape, 0) + q0
        col = lax.broadcasted_iota(jnp.int32, s.shape, 1)
        s = jnp.where(col <= row, s, NEG)
        e = jnp.exp(s - jnp.max(s, axis=-1, keepdims=True))
        p = e / jnp.sum(e, axis=-1, keepdims=True)
        outs.append(_dot_nn(p, v[:, sl]))
    return jnp.concatenate(outs, axis=1)


def _attn_fwd(qn, qr, kn, kr, v, tq=512):
    t, w = qn.shape
    pairs = w // (2 * QK_NOPE)
    tq = min(tq, t)
    pw = 2 * QK_NOPE

    def body(qn_ref, qr_ref, kn_ref, kr_ref, v_ref, o_ref):
        for q0 in range(0, t, tq):
            l = q0 + tq
            o_ref[pl.ds(q0, tq), :] = _attn_block(
                qn_ref[pl.ds(q0, tq), :], qr_ref[pl.ds(q0, tq), :], kn_ref[pl.ds(0, l), :], kr_ref[pl.ds(0, l), :],
                v_ref[pl.ds(0, l), :], q0).astype(o_ref.dtype)

    pair = lambda wd: pl.BlockSpec((t, wd), lambda p: (0, p))
    return pl.pallas_call(
        body, grid=(pairs,),
        in_specs=[pair(pw), pair(LANES), pair(pw), pl.BlockSpec((t, LANES), lambda p: (0, 0)), pair(pw)],
        out_specs=pair(pw), out_shape=jax.ShapeDtypeStruct((t, w), BF),
        compiler_params=_params(("parallel",)), name="attn_fwd",
    )(qn, qr, kn, kr, v)


def _attn_bwd(qn, qr, kn, kr, v, do, tq=256):
    t, w = qn.shape
    pairs = w // (2 * QK_NOPE)
    tq = min(tq, t)
    pw = 2 * QK_NOPE

    def body(qn_ref, qr_ref, kn_ref, kr_ref, v_ref, do_ref, dqn_ref, dqr_ref, dkn_ref, dv_ref, dkr_ref,
             akn_ref, av_ref, akr_ref):
        akn_ref[...] = jnp.zeros_like(akn_ref)
        av_ref[...] = jnp.zeros_like(av_ref)
        akr_ref[...] = jnp.zeros_like(akr_ref)
        for q0 in range(0, t, tq):
            l = q0 + tq
            rows, keys = pl.ds(q0, tq), pl.ds(0, l)
            _, vjp = jax.vjp(functools.partial(_attn_block, q0=q0), qn_ref[rows, :], qr_ref[rows, :],
                             kn_ref[keys, :], kr_ref[keys, :], v_ref[keys, :])
            dqn, dqr, dkn, dkr, dv = vjp(do_ref[rows, :].astype(F32))
            dqn_ref[rows, :] = dqn.astype(dqn_ref.dtype)
            dqr_ref[rows, :] = dqr.astype(dqr_ref.dtype)
            akn_ref[keys, :] += dkn.astype(F32)
            av_ref[keys, :] += dv.astype(F32)
            akr_ref[keys, :] += dkr.astype(F32)
        dkn_ref[...] = akn_ref[...].astype(dkn_ref.dtype)
        dv_ref[...] = av_ref[...].astype(dv_ref.dtype)
        first = pl.program_id(0) == 0

        @pl.when(first)
        def _():
            dkr_ref[...] = akr_ref[...]

        @pl.when(jnp.logical_not(first))
        def _():
            dkr_ref[...] += akr_ref[...]

    pair = lambda wd: pl.BlockSpec((t, wd), lambda p: (0, p))
    shared = pl.BlockSpec((t, LANES), lambda p: (0, 0))
    sds = jax.ShapeDtypeStruct
    return pl.pallas_call(
        body, grid=(pairs,),
        in_specs=[pair(pw), pair(LANES), pair(pw), shared, pair(pw), pair(pw)],
        out_specs=[pair(pw), pair(LANES), pair(pw), pair(pw), shared],
        out_shape=[sds((t, w), BF), sds((t, pairs * LANES), F32), sds((t, w), BF), sds((t, w), BF), sds((t, LANES), F32)],
        scratch_shapes=[pltpu.VMEM((t, pw), F32), pltpu.VMEM((t, pw), F32), pltpu.VMEM((t, LANES), F32)],
        compiler_params=_params(("arbitrary",)), name="attn_bwd",
    )(qn, qr, kn, kr, v, do)


def _ew(name, fn, ins, o_dtypes, max_bytes=2 ** 21):
    r, c = ins[0].shape
    tr = r
    if r * c * 4 > max_bytes:
        tr = max(16, (max_bytes // (c * 4)) // 16 * 16)
        while r % tr:
            tr -= 16
    n_in = len(ins)

    def body(*refs):
        outs = fn(*[x[...] for x in refs[:n_in]])
        for o_ref, o in zip(refs[n_in:], outs):
            o_ref[...] = o.astype(o_ref.dtype)

    spec = pl.BlockSpec((tr, c), lambda i: (i, 0))
    return pl.pallas_call(
        body, grid=(r // tr,), in_specs=[spec] * n_in, out_specs=[spec] * len(o_dtypes),
        out_shape=[jax.ShapeDtypeStruct((r, c), dt) for dt in o_dtypes],
        compiler_params=_params(("parallel",)), name=name,
    )(*ins)


def _adamw_math(w, g, m, v):
    m = ADAM_B1 * m + (1.0 - ADAM_B1) * g
    v = ADAM_B2 * v + (1.0 - ADAM_B2) * jnp.square(g)
    m_hat = m / (1.0 - ADAM_B1 ** ADAM_STEP)
    v_hat = v / (1.0 - ADAM_B2 ** ADAM_STEP)
    delta = -ADAM_LR * (m_hat / (jnp.sqrt(v_hat) + ADAM_EPS) + ADAM_WD * w)
    return delta, m, v


def _adamw(name, w, g, m, v):
    shape = w.shape
    to2 = lambda a: a.reshape(-1, shape[-1]) if a.ndim > 1 else a.reshape(1, -1)
    d, nm, nv = _ew("adamw_" + name, _adamw_math, [to2(w), to2(g), to2(m), to2(v)], (F32, F32, F32))
    return d.reshape(shape), nm.reshape(shape), nv.reshape(shape)


def _adamw_layer(name, w, g, m, v, layer, prev=()):
    _, r, n = w.shape
    tr = r
    while tr * n * 4 > 2 ** 21 and tr % 16 == 0:
        tr //= 2

    def body(*refs):
        w_ref, g_ref, m_ref, v_ref = refs[:4]
        og, od, om, ov = refs[-4:]
        gval = g_ref[...].astype(F32)
        d, nm, nv = _adamw_math(w_ref[...], gval, m_ref[...], v_ref[...])
        og[...] = gval
        od[...] = d
        om[...] = nm
        ov[...] = nv

    lay = pl.BlockSpec((None, tr, n), lambda i: (layer, i, 0))
    return pl.pallas_call(
        body, grid=(r // tr,),
        in_specs=[lay, pl.BlockSpec((tr, n), lambda i: (i, 0)), lay, lay] + [ANY] * len(prev),
        out_specs=[lay] * 4, out_shape=[jax.ShapeDtypeStruct(w.shape, F32)] * 4,
        input_output_aliases={4 + k: k for k in range(len(prev))},
        compiler_params=_params(("parallel",)), name="adamw_%s_%d" % (name, layer),
    )(w, g, m, v, *prev)


def _rope_tables(positions, t):
    half = QK_ROPE // 2
    inv = 1.0 / (ROPE_THETA ** (jnp.arange(0, QK_ROPE, 2, dtype=F32) / QK_ROPE))
    inv_l = jnp.tile(inv, LANES // half).reshape(1, LANES)
    sign = jnp.tile(jnp.concatenate([-jnp.ones((half,), F32), jnp.ones((half,), F32)]), LANES // QK_ROPE).reshape(1, LANES)

    def body(p_ref, inv_ref, sg_ref, c_ref, s_ref):
        ang = p_ref[...].astype(F32) * inv_ref[...]
        c_ref[...] = jnp.cos(ang)
        s_ref[...] = jnp.sin(ang) * sg_ref[...]

    return pl.pallas_call(body, out_shape=[jax.ShapeDtypeStruct((t, LANES), F32)] * 2, name="rope_tables")(
        positions.reshape(t, 1), inv_l, sign)


def _loss_grad(h, target):
    t, d = h.shape
    tr = min(TR, t)

    def body(h_ref, y_ref, dh_ref, l_ref):
        err = h_ref[...] - y_ref[...]
        dh_ref[...] = err * (1.0 / d)
        part = 0.5 * jnp.sum(jnp.mean(jnp.square(err), axis=-1, keepdims=True), axis=0, keepdims=True)
        first = pl.program_id(0) == 0

        @pl.when(first)
        def _():
            l_ref[...] = part

        @pl.when(jnp.logical_not(first))
        def _():
            l_ref[...] += part

    row = pl.BlockSpec((tr, d), lambda i: (i, 0))
    return pl.pallas_call(
        body, grid=(t // tr,), in_specs=[row, row],
        out_specs=[row, pl.BlockSpec((1, 1), lambda i: (0, 0))],
        out_shape=[jax.ShapeDtypeStruct((t, d), F32), jax.ShapeDtypeStruct((1, 1), F32)],
        compiler_params=_params(("arbitrary",)), name="loss_grad",
    )(h, target)


def _sum_devices(g):
    def fn(*parts):
        acc = parts[0]
        for p in parts[1:]:
            acc = acc + p
        return (acc,)
    return _ew("sum_devices", fn, [g[i] for i in range(g.shape[0])], (F32,))[0]


def _place():
    x, y, c = lax.axis_index("x"), lax.axis_index("y"), lax.axis_index("c")
    return x, y, c, [(1 - x, y), (x, 1 - y), (1 - x, 1 - y)]


def _all_gather_small(name, v):
    r, n = v.shape

    def body(x_ref, out_ref, send_sems, recv_sems, local_sem):
        x, y, c, chips = _place()
        me, sibling = (x, y, c), (x, y, 1 - c)

        def rows(px, py, pc):
            return out_ref.at[4 * px + 2 * py + pc]

        def copy(k, block, to, src=None):
            return pltpu.make_async_remote_copy(
                src_ref=rows(*block) if src is None else src, dst_ref=rows(*block), send_sem=send_sems.at[k],
                recv_sem=recv_sems.at[k], device_id=to, device_id_type=MESH)

        mine = pltpu.make_async_copy(x_ref, rows(*me), local_sem)
        mine.start()
        first = [copy(0, me, sibling, src=x_ref)]
        first += [copy(1 + j, me, (*chip, c), src=x_ref) for j, chip in enumerate(chips)]
        for cp in first:
            cp.start()
        passed = [copy(4 + j, (*chip, c), sibling) for j, chip in enumerate(chips)]
        for j, chip in enumerate(chips):
            copy(1 + j, (*chip, c), me).wait_recv()
            passed[j].start()
        copy(0, sibling, me).wait_recv()
        for j, chip in enumerate(chips):
            copy(4 + j, (*chip, 1 - c), me).wait_recv()
        for cp in first + passed:
            cp.wait_send()
        mine.wait()

    return pl.pallas_call(
        body, out_shape=jax.ShapeDtypeStruct((N_DEV, r, n), v.dtype),
        in_specs=[pl.BlockSpec(memory_space=pltpu.VMEM)], out_specs=pl.BlockSpec(memory_space=pltpu.VMEM),
        scratch_shapes=[pltpu.SemaphoreType.DMA((7,)), pltpu.SemaphoreType.DMA((7,)), pltpu.SemaphoreType.DMA],
        compiler_params=pltpu.CompilerParams(vmem_limit_bytes=VMEM_LIMIT), name=name,
    )(v)


def _cast_into_slot(name, w, layer, idx, after=None):
    _, r, n = w.shape
    tr = r
    while tr * n * 4 > 2 ** 21 and tr % 32 == 0:
        tr //= 2
    order = [] if after is None else [after]

    def body(idx_ref, w_ref, *rest):
        o_ref = rest[-1]
        o_ref[...] = w_ref[...].astype(o_ref.dtype)

    return pl.pallas_call(
        body,
        grid_spec=pltpu.PrefetchScalarGridSpec(
            num_scalar_prefetch=1, grid=(r // tr,),
            in_specs=[pl.BlockSpec((None, tr, n), lambda i, idx_ref: (layer, i, 0))] + [ANY] * len(order),
            out_specs=pl.BlockSpec((None, tr, n), lambda i, idx_ref: (idx_ref[1], i, 0))),
        out_shape=jax.ShapeDtypeStruct((N_CHIPS, r, n), BF),
        compiler_params=_params(("parallel",)), name="cast_" + name,
    )(idx, w, *order)


def _gather_weights(bufs):
    n_w = len(bufs)

    def body(*refs):
        outs = refs[n_w:2 * n_w]
        send_sems, recv_sems = refs[2 * n_w:]
        x, y, c, chips = _place()

        def copy(i, s, chip, which, to):
            h = outs[i].shape[1] // 2
            blk = outs[i].at[2 * chip[0] + chip[1], pl.ds(which * h, h)]
            return pltpu.make_async_remote_copy(
                src_ref=blk, dst_ref=blk, send_sem=send_sems.at[6 * i + s], recv_sem=recv_sems.at[6 * i + s],
                device_id=to, device_id_type=MESH)

        sends = []
        for i in range(n_w):
            for j, chip in enumerate(chips):
                cp = copy(i, j, (x, y), c, (*chip, c))
                cp.start()
                sends.append(cp)
        for i in range(n_w):
            for j, chip in enumerate(chips):
                copy(i, j, chip, c, (x, y, c)).wait_recv()
                cp = copy(i, 3 + j, chip, c, (x, y, 1 - c))
                cp.start()
                sends.append(cp)
        for i in range(n_w):
            for j, chip in enumerate(chips):
                copy(i, 3 + j, chip, 1 - c, (x, y, c)).wait_recv()
        for cp in sends:
            cp.wait_send()

    return pl.pallas_call(
        body, in_specs=[ANY] * n_w, out_specs=[ANY] * n_w,
        out_shape=[jax.ShapeDtypeStruct(b.shape, b.dtype) for b in bufs],
        input_output_aliases={i: i for i in range(n_w)},
        scratch_shapes=[pltpu.SemaphoreType.DMA((6 * n_w,)), pltpu.SemaphoreType.DMA((6 * n_w,))],
        name="gather_weights",
    )(*bufs)


def _swap_other_half(name, grads, after=None):
    n_w = len(grads)
    order = [] if after is None else [after]
    n_in = n_w + len(order)

    def body(*refs):
        ins, outs = refs[:n_w], refs[n_in:n_in + n_w]
        send_sems, recv_sems = refs[n_in + n_w:]
        x, y, c, _ = _place()
        cps = []
        for i in range(n_w):
            h = ins[i].shape[1] // 2
            cp = pltpu.make_async_remote_copy(
                src_ref=ins[i].at[:, pl.ds((1 - c) * h, h), :], dst_ref=outs[i], send_sem=send_sems.at[i],
                recv_sem=recv_sems.at[i], device_id=(x, y, 1 - c), device_id_type=MESH)
            cp.start()
            cps.append(cp)
        for cp in cps:
            cp.wait()

    return pl.pallas_call(
        body, in_specs=[ANY] * n_in, out_specs=[ANY] * n_w,
        out_shape=[jax.ShapeDtypeStruct((g.shape[0], g.shape[1] // 2, g.shape[2]), g.dtype) for g in grads],
        scratch_shapes=[pltpu.SemaphoreType.DMA((n_w,)), pltpu.SemaphoreType.DMA((n_w,))],
        name=name,
    )(*grads, *order)


def _add_my_half(name, g, s, idx, o_dtype):
    nc, r, n = g.shape
    h = r // 2
    tr = h
    while tr * n * 4 > 2 ** 21 and tr % 32 == 0:
        tr //= 2
    nb = h // tr

    def body(idx_ref, g_ref, s_ref, o_ref, own_ref):
        val = (g_ref[...].astype(F32) + s_ref[...].astype(F32)).astype(o_ref.dtype)
        o_ref[...] = val

        @pl.when(pl.program_id(1) == idx_ref[1])
        def _():
            own_ref[...] = val

    return pl.pallas_call(
        body,
        grid_spec=pltpu.PrefetchScalarGridSpec(
            num_scalar_prefetch=1, grid=(nb, nc),
            in_specs=[pl.BlockSpec((None, tr, n), lambda i, j, idx_ref: (j, idx_ref[0] * nb + i, 0)),
                      pl.BlockSpec((None, tr, n), lambda i, j, idx_ref: (j, i, 0))],
            out_specs=[pl.BlockSpec((None, tr, n), lambda i, j, idx_ref: (j, i, 0)),
                       pl.BlockSpec((None, tr, n), lambda i, j, idx_ref: (idx_ref[1], i, 0))]),
        out_shape=[jax.ShapeDtypeStruct((nc, h, n), o_dtype)] * 2,
        compiler_params=_params(("parallel", "arbitrary")), name=name,
    )(idx, g, s)


def _scatter_partials(parts, bufs):
    n_w = len(parts)

    def body(*refs):
        ins, outs = refs[:n_w], refs[2 * n_w:3 * n_w]
        send_sems, recv_sems = refs[3 * n_w:]
        x, y, c, chips = _place()
        k = 2 * x + y
        sends = []
        for i in range(n_w):
            for j, chip in enumerate(chips):
                cp = pltpu.make_async_remote_copy(
                    src_ref=ins[i].at[2 * chip[0] + chip[1]], dst_ref=outs[i].at[k], send_sem=send_sems.at[3 * i + j],
                    recv_sem=recv_sems.at[3 * i + j], device_id=(*chip, c), device_id_type=MESH)
                cp.start()
                sends.append(cp)
        for i in range(n_w):
            for j, chip in enumerate(chips):
                blk = outs[i].at[2 * chip[0] + chip[1]]
                pltpu.make_async_remote_copy(
                    src_ref=blk, dst_ref=blk, send_sem=send_sems.at[3 * i + j], recv_sem=recv_sems.at[3 * i + j],
                    device_id=(*chip, c), device_id_type=MESH).wait_recv()
        for cp in sends:
            cp.wait_send()

    return pl.pallas_call(
        body, in_specs=[ANY] * (2 * n_w), out_specs=[ANY] * n_w,
        out_shape=[jax.ShapeDtypeStruct(b.shape, b.dtype) for b in bufs],
        input_output_aliases={n_w + i: i for i in range(n_w)},
        scratch_shapes=[pltpu.SemaphoreType.DMA((3 * n_w,)), pltpu.SemaphoreType.DMA((3 * n_w,))],
        name="scatter_partials",
    )(*parts, *bufs)


def _sum_chips(name, q, idx):
    nc, h, n = q.shape
    tr = h
    while tr * n * 4 > 2 ** 20 and tr % 32 == 0:
        tr //= 2
    nb = h // tr

    def body(idx_ref, q_ref, o_ref):
        acc = q_ref[0].astype(F32)
        for j in range(1, nc):
            acc = acc + q_ref[j].astype(F32)
        o_ref[...] = acc

    return pl.pallas_call(
        body,
        grid_spec=pltpu.PrefetchScalarGridSpec(
            num_scalar_prefetch=1, grid=(nb,),
            in_specs=[pl.BlockSpec((nc, tr, n), lambda i, idx_ref: (0, i, 0))],
            out_specs=pl.BlockSpec((tr, n), lambda i, idx_ref: (idx_ref[0] * nb + i, 0))),
        out_shape=jax.ShapeDtypeStruct((2 * h, n), F32),
        compiler_params=_params(("parallel",)), name=name,
    )(idx, q)


def _join_halves(name, bufs):
    n_w = len(bufs)

    def body(*refs):
        outs = refs[n_w:2 * n_w]
        send_sems, recv_sems = refs[2 * n_w:]
        x, y, c, _ = _place()

        def copy(i, which):
            h = outs[i].shape[0] // 2
            rows = outs[i].at[pl.ds(which * h, h)]
            return pltpu.make_async_remote_copy(
                src_ref=rows, dst_ref=rows, send_sem=send_sems.at[i], recv_sem=recv_sems.at[i],
                device_id=(x, y, 1 - c), device_id_type=MESH)

        cps = [copy(i, c) for i in range(n_w)]
        for cp in cps:
            cp.start()
        for i, cp in enumerate(cps):
            cp.wait_send()
            copy(i, 1 - c).wait_recv()

    return pl.pallas_call(
        body, in_specs=[ANY] * n_w, out_specs=[ANY] * n_w,
        out_shape=[jax.ShapeDtypeStruct(b.shape, b.dtype) for b in bufs],
        input_output_aliases={i: i for i in range(n_w)},
        scratch_shapes=[pltpu.SemaphoreType.DMA((n_w,)), pltpu.SemaphoreType.DMA((n_w,))],
        name=name,
    )(*bufs)


HBM_SPEC = pl.BlockSpec(memory_space=pltpu.HBM)
SEM_SPEC = pl.BlockSpec(memory_space=pltpu.SEMAPHORE)
VMEM_SPEC = pl.BlockSpec(memory_space=pltpu.VMEM)
SIDE_EFFECT = pltpu.SideEffectType.DATAFLOW_SIDE_EFFECTING


def _in_hbm(arrays):
    return [pltpu.with_memory_space_constraint(a, pltpu.HBM) for a in arrays]


def _ici_gather_copy(ref, i, j, chip, send_sems, recv_sems):
    x, y, c, _ = _place()
    h = ref.shape[1] // 2
    mine = ref.at[2 * x + y, pl.ds(c * h, h)]
    return pltpu.make_async_remote_copy(
        src_ref=mine, dst_ref=mine, send_sem=send_sems.at[3 * i + j], recv_sem=recv_sems.at[3 * i + j],
        device_id=(*chip, c), device_id_type=MESH)


def _gather_start(name, bufs, after=None):
    n_w = len(bufs)
    order = [] if after is None else [after]

    def body(*refs):
        ins, token = refs[:n_w], refs[-1]
        send_sems, recv_sems = refs[n_w + len(order)], refs[n_w + len(order) + 1]
        chips = _place()[3]
        for i in range(n_w):
            for j, chip in enumerate(chips):
                _ici_gather_copy(ins[i], i, j, chip, send_sems, recv_sems).start()
        token[...] = jnp.zeros_like(token)

    res = pl.pallas_call(
        body, name=name, in_specs=[HBM_SPEC] * n_w + [ANY] * len(order),
        out_shape=(pltpu.SemaphoreType.DMA((3 * n_w,)), pltpu.SemaphoreType.DMA((3 * n_w,)),
                   *[pltpu.HBM(b.shape, b.dtype) for b in bufs], jax.ShapeDtypeStruct((8, LANES), F32)),
        out_specs=(SEM_SPEC, SEM_SPEC, *[HBM_SPEC] * n_w, VMEM_SPEC),
        input_output_aliases={i: 2 + i for i in range(n_w)},
        compiler_params=pltpu.CompilerParams(has_side_effects=SIDE_EFFECT),
    )(*_in_hbm(bufs), *order)
    return res[0], res[1], list(res[2:2 + n_w]), res[-1]


def _gather_wait(name, send_sems, recv_sems, bufs, after):
    n_w = len(bufs)

    def body(*refs):
        ins, s_sems, r_sems = refs[:n_w], refs[n_w], refs[n_w + 1]
        chips = _place()[3]
        for i in range(n_w):
            for j, chip in enumerate(chips):
                cp = _ici_gather_copy(ins[i], i, j, chip, s_sems, r_sems)
                cp.wait_send()
                cp.wait_recv()

    return pl.pallas_call(
        body, name=name, in_specs=[HBM_SPEC] * n_w + [SEM_SPEC, SEM_SPEC, ANY],
        out_shape=[pltpu.HBM(b.shape, b.dtype) for b in bufs], out_specs=[HBM_SPEC] * n_w,
        input_output_aliases={i: i for i in range(n_w)},
        compiler_params=pltpu.CompilerParams(has_side_effects=SIDE_EFFECT),
    )(*bufs, send_sems, recv_sems, after)


def _forward_to_sibling(name, bufs):
    n_w = len(bufs)

    def body(*refs):
        outs = refs[n_w:2 * n_w]
        send_sems, recv_sems = refs[2 * n_w:]
        x, y, c, chips = _place()

        def copy(i, j, chip, which):
            h = outs[i].shape[1] // 2
            blk = outs[i].at[2 * chip[0] + chip[1], pl.ds(which * h, h)]
            return pltpu.make_async_remote_copy(
                src_ref=blk, dst_ref=blk, send_sem=send_sems.at[3 * i + j], recv_sem=recv_sems.at[3 * i + j],
                device_id=(x, y, 1 - c), device_id_type=MESH)

        sends = [copy(i, j, chip, c) for i in range(n_w) for j, chip in enumerate(chips)]
        for cp in sends:
            cp.start()
        for i in range(n_w):
            for j, chip in enumerate(chips):
                copy(i, j, chip, 1 - c).wait_recv()
        for cp in sends:
            cp.wait_send()

    return pl.pallas_call(
        body, in_specs=[ANY] * n_w, out_specs=[ANY] * n_w,
        out_shape=[jax.ShapeDtypeStruct(b.shape, b.dtype) for b in bufs],
        input_output_aliases={i: i for i in range(n_w)},
        scratch_shapes=[pltpu.SemaphoreType.DMA((3 * n_w,)), pltpu.SemaphoreType.DMA((3 * n_w,))],
        name=name,
    )(*bufs)


def _ici_scatter_copy(part, buf, i, j, chip, send_sems, recv_sems):
    x, y, c, _ = _place()
    return pltpu.make_async_remote_copy(
        src_ref=part.at[2 * chip[0] + chip[1]], dst_ref=buf.at[2 * x + y], send_sem=send_sems.at[3 * i + j],
        recv_sem=recv_sems.at[3 * i + j], device_id=(*chip, c), device_id_type=MESH)


def _ici_scatter_landing(buf, i, j, chip, send_sems, recv_sems):
    x, y, c, _ = _place()
    blk = buf.at[2 * chip[0] + chip[1]]
    return pltpu.make_async_remote_copy(
        src_ref=blk, dst_ref=blk, send_sem=send_sems.at[3 * i + j], recv_sem=recv_sems.at[3 * i + j],
        device_id=(*chip, c), device_id_type=MESH)


def _scatter_start(name, parts, bufs):
    n_w = len(parts)

    def body(*refs):
        ps, bs = refs[:n_w], refs[n_w:2 * n_w]
        send_sems, recv_sems, token = refs[2 * n_w], refs[2 * n_w + 1], refs[-1]
        chips = _place()[3]
        for i in range(n_w):
            for j, chip in enumerate(chips):
                _ici_scatter_copy(ps[i], bs[i], i, j, chip, send_sems, recv_sems).start()
        token[...] = jnp.zeros_like(token)

    both = list(parts) + list(bufs)
    res = pl.pallas_call(
        body, name=name, in_specs=[HBM_SPEC] * (2 * n_w),
        out_shape=(pltpu.SemaphoreType.DMA((3 * n_w,)), pltpu.SemaphoreType.DMA((3 * n_w,)),
                   *[pltpu.HBM(b.shape, b.dtype) for b in both], jax.ShapeDtypeStruct((8, LANES), F32)),
        out_specs=(SEM_SPEC, SEM_SPEC, *[HBM_SPEC] * (2 * n_w), VMEM_SPEC),
        input_output_aliases={i: 2 + i for i in range(2 * n_w)},
        compiler_params=pltpu.CompilerParams(has_side_effects=SIDE_EFFECT),
    )(*_in_hbm(both))
    return res[0], res[1], list(res[2:2 + n_w]), list(res[2 + n_w:2 + 2 * n_w]), res[-1]


def _scatter_wait(name, send_sems, recv_sems, parts, bufs, after):
    n_w = len(parts)
    after = list(after) if isinstance(after, (list, tuple)) else [after]

    def body(*refs):
        ps, bs = refs[:n_w], refs[n_w:2 * n_w]
        s_sems, r_sems = refs[2 * n_w], refs[2 * n_w + 1]
        chips = _place()[3]
        for i in range(n_w):
            for j, chip in enumerate(chips):
                _ici_scatter_copy(ps[i], bs[i], i, j, chip, s_sems, r_sems).wait_send()
                _ici_scatter_landing(bs[i], i, j, chip, s_sems, r_sems).wait_recv()

    both = list(parts) + list(bufs)
    res = pl.pallas_call(
        body, name=name, in_specs=[HBM_SPEC] * (2 * n_w) + [SEM_SPEC, SEM_SPEC] + [ANY] * len(after),
        out_shape=[pltpu.HBM(b.shape, b.dtype) for b in both], out_specs=[HBM_SPEC] * (2 * n_w),
        input_output_aliases={i: i for i in range(2 * n_w)},
        compiler_params=pltpu.CompilerParams(has_side_effects=SIDE_EFFECT),
    )(*both, send_sems, recv_sems, *after)
    return list(res[n_w:])


def _swap_copy(g, land, i, send_sems, recv_sems):
    x, y, c, _ = _place()
    h = g.shape[1] // 2
    return pltpu.make_async_remote_copy(
        src_ref=g.at[:, pl.ds((1 - c) * h, h), :], dst_ref=land, send_sem=send_sems.at[i], recv_sem=recv_sems.at[i],
        device_id=(x, y, 1 - c), device_id_type=MESH)


def _swap_start(name, grads):
    n_w = len(grads)
    lands = [lax.empty((g.shape[0], g.shape[1] // 2, g.shape[2]), g.dtype) for g in grads]

    def body(*refs):
        gs, ls = refs[:n_w], refs[n_w:2 * n_w]
        send_sems, recv_sems, token = refs[2 * n_w], refs[2 * n_w + 1], refs[-1]
        for i in range(n_w):
            _swap_copy(gs[i], ls[i], i, send_sems, recv_sems).start()
        token[...] = jnp.zeros_like(token)

    both = list(grads) + lands
    res = pl.pallas_call(
        body, name=name, in_specs=[HBM_SPEC] * (2 * n_w),
        out_shape=(pltpu.SemaphoreType.DMA((n_w,)), pltpu.SemaphoreType.DMA((n_w,)),
                   *[pltpu.HBM(b.shape, b.dtype) for b in both], jax.ShapeDtypeStruct((8, LANES), F32)),
        out_specs=(SEM_SPEC, SEM_SPEC, *[HBM_SPEC] * (2 * n_w), VMEM_SPEC),
        input_output_aliases={i: 2 + i for i in range(2 * n_w)},
        compiler_params=pltpu.CompilerParams(has_side_effects=SIDE_EFFECT),
    )(*_in_hbm(both))
    return (res[0], res[1], list(res[2:2 + n_w]), list(res[2 + n_w:2 + 2 * n_w])), res[-1]


def _swap_wait(name, swapped, after):
    send_sems, recv_sems, grads, lands = swapped
    n_w = len(grads)

    def body(*refs):
        gs, ls = refs[:n_w], refs[n_w:2 * n_w]
        s_sems, r_sems = refs[2 * n_w], refs[2 * n_w + 1]
        for i in range(n_w):
            cp = _swap_copy(gs[i], ls[i], i, s_sems, r_sems)
            cp.wait_send()
            cp.wait_recv()

    both = list(grads) + list(lands)
    res = pl.pallas_call(
        body, name=name, in_specs=[HBM_SPEC] * (2 * n_w) + [SEM_SPEC, SEM_SPEC, ANY],
        out_shape=[pltpu.HBM(b.shape, b.dtype) for b in both], out_specs=[HBM_SPEC] * (2 * n_w),
        input_output_aliases={i: i for i in range(2 * n_w)},
        compiler_params=pltpu.CompilerParams(has_side_effects=SIDE_EFFECT),
    )(*both, send_sems, recv_sems, after)
    return list(res[:n_w]), list(res[n_w:])


def _reduce_scatter_begin(tag, grads, idx, after=None, swapped=None):
    if swapped is None:
        from_sibling = _swap_other_half("swap_other_half_" + tag, grads, after)
    else:
        grads, from_sibling = _swap_wait("swap_wait_" + tag, swapped, after)
    pairs = [_add_my_half("add_my_half_%s_%d" % (tag, i), g, s, idx, BF)
             for i, (g, s) in enumerate(zip(grads, from_sibling))]
    s_sems, r_sems, parts, bufs, token = _scatter_start("scatter_start_" + tag, [p[0] for p in pairs],
                                                        [p[1] for p in pairs])
    return (tag, s_sems, r_sems, parts, bufs, idx), token


def _reduce_scatter_end(state, after):
    tag, s_sems, r_sems, parts, bufs, idx = state
    received = _scatter_wait("scatter_wait_" + tag, s_sems, r_sems, parts, bufs, after)
    halves = [_sum_chips("sum_chips_%s_%d" % (tag, i), q, idx) for i, q in enumerate(received)]
    return _join_halves("join_halves_" + tag, halves)


def _relu2_epilogue(acc):
    r = jnp.maximum(acc, 0.0)
    return (r * r,)


def _relu2_bwd_epilogue(acc, s):
    return (acc * (2.0 * jnp.sqrt(s.astype(F32))),)


WIDE = (2048, 512, 2048)
DEEP = (2048, 512, 2048)


def _add_epilogue(acc, other):
    return (acc + other,)


def kernel(x, c, positions, w_ada_mix, b_ada_mix, w_ada_mlp, b_ada_mlp, g_pre_mix, g_post_mix, g_pre_mlp, g_post_mlp, conv_w_in, conv_b_in, conv_dw, conv_dw_b, conv_ln_g, conv_ln_b, conv_w_out, conv_b_out, w_ada_kv, b_ada_kv, g_kv, w_dkv, g_ckv, w_kr, w_uk, w_uv, w_dq, g_cq, w_uq, w_o, mlp_w_up, mlp_w_down, loss_target, m_w_ada_mix, m_b_ada_mix, m_w_ada_mlp, m_b_ada_mlp, m_g_pre_mix, m_g_post_mix, m_g_pre_mlp, m_g_post_mlp, m_conv_w_in, m_conv_b_in, m_conv_dw, m_conv_dw_b, m_conv_ln_g, m_conv_ln_b, m_conv_w_out, m_conv_b_out, m_w_ada_kv, m_b_ada_kv, m_g_kv, m_w_dkv, m_g_ckv, m_w_kr, m_w_uk, m_w_uv, m_w_dq, m_g_cq, m_w_uq, m_w_o, m_mlp_w_up, m_mlp_w_down, v_w_ada_mix, v_b_ada_mix, v_w_ada_mlp, v_b_ada_mlp, v_g_pre_mix, v_g_post_mix, v_g_pre_mlp, v_g_post_mlp, v_conv_w_in, v_conv_b_in, v_conv_dw, v_conv_dw_b, v_conv_ln_g, v_conv_ln_b, v_conv_w_out, v_conv_b_out, v_w_ada_kv, v_b_ada_kv, v_g_kv, v_w_dkv, v_g_ckv, v_w_kr, v_w_uk, v_w_uv, v_w_dq, v_g_cq, v_w_uq, v_w_o, v_mlp_w_up, v_mlp_w_down):
    weights = dict(w_ada_mix=w_ada_mix, b_ada_mix=b_ada_mix, w_ada_mlp=w_ada_mlp, b_ada_mlp=b_ada_mlp, g_pre_mix=g_pre_mix, g_post_mix=g_post_mix, g_pre_mlp=g_pre_mlp, g_post_mlp=g_post_mlp, conv_w_in=conv_w_in, conv_b_in=conv_b_in, conv_dw=conv_dw, conv_dw_b=conv_dw_b, conv_ln_g=conv_ln_g, conv_ln_b=conv_ln_b, conv_w_out=conv_w_out, conv_b_out=conv_b_out, w_ada_kv=w_ada_kv, b_ada_kv=b_ada_kv, g_kv=g_kv, w_dkv=w_dkv, g_ckv=g_ckv, w_kr=w_kr, w_uk=w_uk, w_uv=w_uv, w_dq=w_dq, g_cq=g_cq, w_uq=w_uq, w_o=w_o, mlp_w_up=mlp_w_up, mlp_w_down=mlp_w_down)
    m_in = dict(w_ada_mix=m_w_ada_mix, b_ada_mix=m_b_ada_mix, w_ada_mlp=m_w_ada_mlp, b_ada_mlp=m_b_ada_mlp, g_pre_mix=m_g_pre_mix, g_post_mix=m_g_post_mix, g_pre_mlp=m_g_pre_mlp, g_post_mlp=m_g_post_mlp, conv_w_in=m_conv_w_in, conv_b_in=m_conv_b_in, conv_dw=m_conv_dw, conv_dw_b=m_conv_dw_b, conv_ln_g=m_conv_ln_g, conv_ln_b=m_conv_ln_b, conv_w_out=m_conv_w_out, conv_b_out=m_conv_b_out, w_ada_kv=m_w_ada_kv, b_ada_kv=m_b_ada_kv, g_kv=m_g_kv, w_dkv=m_w_dkv, g_ckv=m_g_ckv, w_kr=m_w_kr, w_uk=m_w_uk, w_uv=m_w_uv, w_dq=m_w_dq, g_cq=m_g_cq, w_uq=m_w_uq, w_o=m_w_o, mlp_w_up=m_mlp_w_up, mlp_w_down=m_mlp_w_down)
    v_in = dict(w_ada_mix=v_w_ada_mix, b_ada_mix=v_b_ada_mix, w_ada_mlp=v_w_ada_mlp, b_ada_mlp=v_b_ada_mlp, g_pre_mix=v_g_pre_mix, g_post_mix=v_g_post_mix, g_pre_mlp=v_g_pre_mlp, g_post_mlp=v_g_post_mlp, conv_w_in=v_conv_w_in, conv_b_in=v_conv_b_in, conv_dw=v_conv_dw, conv_dw_b=v_conv_dw_b, conv_ln_g=v_conv_ln_g, conv_ln_b=v_conv_ln_b, conv_w_out=v_conv_w_out, conv_b_out=v_conv_b_out, w_ada_kv=v_w_ada_kv, b_ada_kv=v_b_ada_kv, g_kv=v_g_kv, w_dkv=v_w_dkv, g_ckv=v_g_ckv, w_kr=v_w_kr, w_uk=v_w_uk, w_uv=v_w_uv, w_dq=v_w_dq, g_cq=v_g_cq, w_uq=v_w_uq, w_o=v_w_o, mlp_w_up=v_mlp_w_up, mlp_w_down=v_mlp_w_down)
    layer1 = {}

    def early_updates(red):
        for n, key in (("mlp_w_up", "up1"), ("mlp_w_down", "down1")):
            layer1[n] = _adamw_layer(n, weights[n], red[key], m_in[n], v_in[n], 1)
        return [layer1[n][1] for n in layer1]

    grads, last_group = _step_grads(x, c, positions, loss_target, weights, early_updates)
    loss, grad_x = grads.pop("loss"), grads.pop("x")
    names = list(weights)
    upd = {}
    for n in names:
        if isinstance(grads.get(n), list):
            res = layer1.get(n) or _adamw_layer(n, weights[n], grads[n][1], m_in[n], v_in[n], 1)
            grads[n], *upd[n] = _adamw_layer(n, weights[n], grads[n][0], m_in[n], v_in[n], 0, prev=res)
        elif n in grads:
            upd[n] = _adamw(n, weights[n], grads[n], m_in[n], v_in[n])
    grads.update(last_group([upd[n][0] for n in ("w_ada_mix", "w_ada_mlp", "w_ada_kv", "mlp_w_down")]))
    upd.update({n: _adamw(n, weights[n], grads[n], m_in[n], v_in[n]) for n in names if n not in upd})
    return (loss, grad_x, *[grads[n] for n in names], *[upd[n][0] for n in names], *[upd[n][1] for n in names],
            *[upd[n][2] for n in names])


def _step_grads(x, c, positions, loss_target, w, early_updates):
    xi, yi, ci = lax.axis_index("x"), lax.axis_index("y"), lax.axis_index("c")
    chip = 2 * xi + yi
    dev = 2 * chip + ci
    place_idx = jnp.stack([ci, chip]).astype(jnp.int32)
    t, d = x.shape[1], x.shape[2]
    dl = d // N_CHIPS
    r_kv, r_q = w["w_dkv"].shape[1], w["w_dq"].shape[2]
    n_ada, n_kvada = w["w_ada_mix"].shape[2], w["w_ada_kv"].shape[1]
    heads_l = w["w_uq"].shape[2] // (QK_NOPE + QK_ROPE)
    assert dl == r_kv == r_q and dl % LANES == 0 and heads_l * N_CHIPS % 2 == 0

    def chip_cols(a, width):
        return lax.dynamic_slice_in_dim(a, chip * width, width, axis=a.ndim - 1)

    uq = w["w_uq"][0].reshape(r_q, heads_l, QK_NOPE + QK_ROPE)
    uq_nope, uq_rope = uq[:, :, :QK_NOPE].reshape(r_q, -1), uq[:, :, QK_NOPE:].reshape(r_q, -1)
    kr_pad = jnp.pad(w["w_kr"], ((0, 0), (0, LANES - QK_ROPE)))
    small = jnp.concatenate([w["w_dkv"], kr_pad, w["w_dq"][0], w["w_uk"], w["w_uv"], uq_nope, uq_rope], axis=1)
    widths = [r_kv, LANES, r_q, w["w_uk"].shape[1], w["w_uv"].shape[1], uq_nope.shape[1], uq_rope.shape[1]]
    so = [0]
    for wd in widths:
        so.append(so[-1] + wd)
    big = dict(conv_w_in=(w["conv_w_in"], 0), conv_w_out=(w["conv_w_out"], 0), w_o=(w["w_o"], 0),
               up0=(w["mlp_w_up"], 0), up1=(w["mlp_w_up"], 1), down0=(w["mlp_w_down"], 0),
               down1=(w["mlp_w_down"], 1), small=(small[None], 0))
    groups = dict(conv_in=["conv_w_in"], conv_out=["conv_w_out"], up0=["up0"], down0=["down0"],
                  mla=["small", "w_o"], up1=["up1"], down1=["down1"])
    gathered, pending = {}, {}

    def start_gathers(gnames, after):
        token = after
        for gname in gnames:
            slots = [_cast_into_slot(n, *big[n], place_idx, after=token) for n in groups[gname]]
            s_sems, r_sems, bufs, token = _gather_start("gather_start_" + gname, slots, after=token)
            pending[gname] = (s_sems, r_sems, bufs)
        return token

    def finish_gather(gname, after):
        s_sems, r_sems, bufs = pending[gname]
        bufs = _gather_wait("gather_wait_" + gname, s_sems, r_sems, bufs, after)
        gathered.update(zip(groups[gname], _forward_to_sibling("forward_" + gname, bufs)))

    rowl = lambda a: a.reshape(-1, a.shape[2])
    w_up_f = lambda l: gathered["up%d" % l]
    w_down_f = lambda l: rowl(gathered["down%d" % l])

    pack_rows = [w["conv_dw"][0], w["conv_dw_b"], w["conv_ln_g"], w["conv_ln_b"], w["conv_b_out"],
                 w["conv_b_in"].reshape(2, dl), c.reshape(N_CHIPS, dl)]
    pack_rows = [jnp.pad(p, ((0, (-p.shape[0]) % 8), (0, 0))) for p in pack_rows]
    po = [0]
    for p in pack_rows:
        po.append(po[-1] + p.shape[0])
    packs = _all_gather_small("gather_params", jnp.concatenate(pack_rows, axis=0))
    by_chip = packs[0::2]

    def full_width(r0, nr):
        return jnp.transpose(by_chip[:, r0:r0 + nr, :], (1, 0, 2)).reshape(nr, d)

    dw_full, dwb_full = full_width(po[0], CONV_WIDTH), full_width(po[1], 1)
    lng_full, lnb_full, bout_full = full_width(po[2], 1), full_width(po[3], 1), full_width(po[4], 1)
    bin_full = by_chip[:, po[5]:po[5] + 2, :].reshape(1, 2 * d)
    first_start = start_gathers(("conv_in",), packs)[0:1, 0:1]
    c_all = packs[:, po[6]:po[6] + N_CHIPS, :].reshape(N_DEV, d) + first_start

    sc_all = _ew("silu_c", lambda a: (a * jax.nn.sigmoid(a),), [c_all], (F32,))[0]
    ada_w = [(w["w_ada_mix"], 0), (w["w_ada_mix"], 1), (w["w_ada_mlp"], 0), (w["w_ada_mlp"], 1),
             (w["w_ada_kv"][None], 0)]
    ada_b = [chip_cols(w["b_ada_mix"][0:1], n_ada), chip_cols(w["b_ada_mix"][1:2], n_ada),
             chip_cols(w["b_ada_mlp"][0:1], n_ada), chip_cols(w["b_ada_mlp"][1:2], n_ada),
             chip_cols(w["b_ada_kv"].reshape(1, -1), n_kvada)]
    mods = [_mm_nn("ada_fwd_%d" % i, sc_all, wi, "row", extras=(jnp.broadcast_to(bi, (N_DEV, bi.shape[1])),),
                   epilogue=_add_epilogue, w_layer=li) for i, ((wi, li), bi) in enumerate(zip(ada_w, ada_b))]
    mods_all = _all_gather_small("gather_mods", jnp.concatenate(mods, axis=1))
    started = start_gathers(("conv_out", "up0", "down0", "mla", "up1", "down1"),
                            mods_all[0, 0:1, 0:1] + first_start)[0:1, 0:1]
    mine = lax.dynamic_index_in_dim(mods_all[0::2], dev, axis=1, keepdims=False)
    offs = [0]
    for m_ in mods:
        offs.append(offs[-1] + m_.shape[1])
    mod_vec = [mine[:, offs[i]:offs[i + 1]].reshape(1, -1) for i in range(5)]
    split3 = lambda v: (v[:, :d], v[:, d:2 * d], v[:, 2 * d:])
    mix = [split3(mod_vec[0]), split3(mod_vec[1])]
    mlp = [split3(mod_vec[2]), split3(mod_vec[3])]
    kv_shift, kv_scale = mod_vec[4][:, :d], mod_vec[4][:, d:]

    cos_t, sin_t = _rope_tables(positions, t)
    vec = lambda a: a.reshape(1, -1)
    gpm, gqm = w["g_pre_mix"], w["g_post_mix"]
    gpl, gql = w["g_pre_mlp"], w["g_post_mlp"]
    h0 = x[0]
    after_token = lambda v, token: v + token[0:1, 0:1]

    def mlp_fwd(l, h):
        sh, sc, _ = mlp[l]
        (hn,) = _rw_fwd("mlp_pre_%d" % l, _f_pre, [h], [vec(gpl[l]), sh, sc], (BF,))
        finish_gather("up%d" % l, hn)
        s = _mm_nn("mlp_up_%d" % l, hn, w_up_f(l), "col", o_dtypes=(BF,), epilogue=_relu2_epilogue, tiles=WIDE)
        finish_gather("down%d" % l, s)
        y = _mm_nn("mlp_down_%d" % l, s, w_down_f(l), "row", tiles=DEEP)
        return hn, s, y

    def post_fwd(name, h, y, gate, g, bias=None):
        if bias is None:
            return _rw_fwd(name, lambda h_, y_, gt, g_: (h_ + _f_post(y_, gt, g_)[0],), [h, y], [gate, g], (F32,))[0]
        return _rw_fwd(name, lambda h_, y_, b_, gt, g_: (h_ + _f_post_bias(y_, b_, gt, g_)[0],), [h, y],
                       [bias, gate, g], (F32,))[0]

    (hn0,) = _rw_fwd("conv_pre", _f_pre, [h0], [vec(gpm[0]), mix[0][0] + started, mix[0][1]], (BF,))
    finish_gather("conv_in", hn0)
    w_in_f = gathered["conv_w_in"]
    z0 = _mm_nn("conv_in", hn0, w_in_f, "col")
    (u0,) = _rw_fwd("conv_glu", _f_glu, [z0], [bin_full], (F32,))
    uc0 = _conv_fwd(u0, dw_full, dwb_full)
    (n0,) = _rw_fwd("conv_ln", _f_ln_silu, [uc0], [lng_full, lnb_full], (BF,))
    finish_gather("conv_out", n0)
    w_out_f = rowl(gathered["conv_w_out"])
    y0 = _mm_nn("conv_out", n0, w_out_f, "row")
    h1 = post_fwd("conv_post", h0, y0, mix[0][2], vec(gqm[0]), bias=bout_full)
    hn1, s1, y1 = mlp_fwd(0, h1)
    h2 = post_fwd("mlp_post_0", h1, y1, mlp[0][2], vec(gql[0]))

    finish_gather("mla", h2)
    gs = gathered["small"]
    w_dkvkr = rowl(gs[:, :, so[0]:so[2]])
    w_dq_f = rowl(gs[:, :, so[2]:so[3]])
    w_uk_f, w_uv_f = gs[:, :, so[3]:so[4]], gs[:, :, so[4]:so[5]]
    w_qn_f, w_qr_f = gs[:, :, so[5]:so[6]], gs[:, :, so[6]:so[7]]
    w_o_f = rowl(gathered["w_o"])
    kvn, hn2 = _rw_fwd("mla_pre", _f_pre2, [h2], [vec(w["g_kv"]), kv_shift, kv_scale, vec(gpm[1]), mix[1][0], mix[1][1]],
                       (BF, BF))
    pre_kv = _mm_nn("mla_dkv", kvn, w_dkvkr, "row")
    pre_q = _mm_nn("mla_dq", hn2, w_dq_f, "row")
    f_lat = _make_f_lat(r_kv)
    c_kv, kr, c_q = _rw_fwd("mla_latent", f_lat, [pre_kv, pre_q, cos_t, sin_t], [vec(w["g_ckv"]), vec(w["g_cq"][0])],
                            (BF, BF, BF))
    kn = _mm_nn("mla_uk", c_kv, w_uk_f, "col", o_dtypes=(BF,))
    vv = _mm_nn("mla_uv", c_kv, w_uv_f, "col", o_dtypes=(BF,))
    qn = _mm_nn("mla_uq_nope", c_q, w_qn_f, "col", o_dtypes=(BF,))
    qr_pre = _mm_nn("mla_uq_rope", c_q, w_qr_f, "col")
    (qr,) = _rw_fwd("mla_q_rope", _f_rope, [qr_pre, cos_t, sin_t], [], (BF,))
    att = _attn_fwd(qn, qr, kn, kr, vv)
    y2 = _mm_nn("mla_o", att, w_o_f, "row")
    h3 = post_fwd("mla_post", h2, y2, mix[1][2], vec(gqm[1]))
    hn3, s3, y3 = mlp_fwd(1, h3)
    h4 = post_fwd("mlp_post_1", h3, y3, mlp[1][2], vec(gql[1]))

    dh4, loss_part = _loss_grad(h4, loss_target[0])
    loss = lax.psum(loss_part[0, 0], ("x", "y", "c"))

    gw = {}
    gvec = {}

    dw_mm = functools.partial(_mm_tn, o_dtype=BF)

    def mlp_bwd(l, h_in, hn, s, y, dh, token=None):
        sh, sc, gate = mlp[l]
        if token is not None:
            gate = after_token(gate, token)
        (dy,), (dgate, dgq) = _rw_bwd("mlp_post_bwd_%d" % l, _f_post, [y], [gate, vec(gql[l])], [dh], [BF])
        gw["down%d" % l] = dw_mm("mlp_down_dw_%d" % l, s, dy, "row", tiles=(512, 2048, 2048))
        da = _mm_nt("mlp_down_dx_%d" % l, dy, w_down_f(l), "row", o_dtypes=(BF,), extras=(s,),
                    epilogue=_relu2_bwd_epilogue, tiles=WIDE)
        gw["up%d" % l] = dw_mm("mlp_up_dw_%d" % l, hn, da, "col", tiles=WIDE)
        dhn = _mm_nt("mlp_up_dx_%d" % l, da, w_up_f(l), "col", tiles=DEEP)
        (dh_in,), (dgp, dsh, dsc) = _rw_bwd("mlp_pre_bwd_%d" % l, _f_pre, [h_in], [vec(gpl[l]), sh, sc], [dhn], [F32],
                                            add_rows={0: dh})
        gvec["mlp%d" % l] = (dsh, dsc, dgate)
        gvec["g_pre_mlp%d" % l], gvec["g_post_mlp%d" % l] = dgp, dgq
        return dh_in

    chunked = lambda a: a.reshape(N_CHIPS, a.shape[0] // N_CHIPS, a.shape[1])
    to_chunks = lambda a: a if a.ndim == 3 else chunked(a)
    red = {}

    def swap_begin(tag, names):
        return _swap_start("swap_start_" + tag, [to_chunks(gw[n]) for n in names])

    def reduce_begin(tag, names, after=None, swapped=None):
        state, token = _reduce_scatter_begin(tag, [to_chunks(gw[n]) for n in names], place_idx, after, swapped)
        return (names, state), token

    def reduce_end(handle, after):
        names, state = handle
        red.update(zip(names, _reduce_scatter_end(state, after)))

    dh3 = mlp_bwd(1, h3, hn3, s3, y3, dh4)
    sw1, token = swap_begin("r1", ["up1", "down1"])

    (dy2,), (dgate, dgq) = _rw_bwd("mla_post_bwd", _f_post, [y2], [after_token(mix[1][2], token), vec(gqm[1])],
                                   [dh3], [BF])
    gvec["g_post_mix1"] = dgq
    gw["w_o"] = dw_mm("mla_o_dw", att, dy2, "row")
    datt = _mm_nt("mla_o_dx", dy2, w_o_f, "row", o_dtypes=(BF,))
    rs1, token = reduce_begin("r1", ["up1", "down1"], after=datt, swapped=sw1)
    dqn, dqr, dkn, dvv, dkr = _attn_bwd(qn, qr, kn, kr + token[0:1, 0:1].astype(BF), vv, datt)
    (dqr_pre,), _ = _rw_bwd("mla_q_rope_bwd", _f_rope, [qr_pre, cos_t, sin_t], [], [dqr], [BF, None, None])
    g_qn = dw_mm("mla_uq_nope_dw", c_q, dqn, "col")
    g_qr = dw_mm("mla_uq_rope_dw", c_q, dqr_pre, "col")
    dc_q = _mm_nt("mla_uq_nope_dx", dqn, w_qn_f, "col")
    dc_q = _mm_nt("mla_uq_rope_dx", dqr_pre, w_qr_f, "col", extras=(dc_q,), epilogue=_add_epilogue)
    g_uk = dw_mm("mla_uk_dw", c_kv, dkn, "col")
    g_uv = dw_mm("mla_uv_dw", c_kv, dvv, "col")
    dc_kv = _mm_nt("mla_uk_dx", dkn, w_uk_f, "col")
    dc_kv = _mm_nt("mla_uv_dx", dvv, w_uv_f, "col", extras=(dc_kv,), epilogue=_add_epilogue)
    (dpre_kv, dpre_q), (dg_ckv, dg_cq) = _rw_bwd(
        "mla_latent_bwd", f_lat, [pre_kv, pre_q, cos_t, sin_t], [vec(w["g_ckv"]), vec(w["g_cq"][0])],
        [dc_kv, dkr, dc_q], [BF, BF, None, None])
    gvec["g_ckv"], gvec["g_cq"] = dg_ckv, dg_cq
    g_dkvkr = dw_mm("mla_dkv_dw", kvn, dpre_kv, "row")
    g_dq = dw_mm("mla_dq_dw", hn2, dpre_q, "row")
    dkvn = _mm_nt("mla_dkv_dx", dpre_kv, w_dkvkr, "row")
    dhn2 = _mm_nt("mla_dq_dx", dpre_q, w_dq_f, "row")
    (dh2,), (dg_kv, dkvsh, dkvsc, dgp, dsh, dsc) = _rw_bwd(
        "mla_pre_bwd", _f_pre2, [h2], [vec(w["g_kv"]), kv_shift, kv_scale, vec(gpm[1]), mix[1][0], mix[1][1]],
        [dkvn, dhn2], [F32], add_rows={0: dh3})
    gvec["mix1"] = (dsh, dsc, dgate)
    gvec["kv"] = (dkvsh, dkvsc)
    gvec["g_kv"], gvec["g_pre_mix1"] = dg_kv, dgp
    gw["small"] = jnp.concatenate([chunked(g_dkvkr), chunked(g_dq), g_uk, g_uv, g_qn, g_qr], axis=2)
    reduce_end(rs1, dh2)
    rs2, token = reduce_begin("r2", ["small", "w_o"])

    dh1 = mlp_bwd(0, h1, hn1, s1, y1, dh2, token=token)
    reduce_end(rs2, dh1)
    rs3, token = reduce_begin("r3", ["up0", "down0"])

    (dy0,), (dbout, dgate, dgq) = _rw_bwd("conv_post_bwd", _f_post_bias, [y0],
                                          [bout_full, after_token(mix[0][2], token), vec(gqm[0])], [dh1], [BF])
    gvec["g_post_mix0"] = dgq
    gw["conv_w_out"] = dw_mm("conv_out_dw", n0, dy0, "row")
    dn0 = _mm_nt("conv_out_dx", dy0, w_out_f, "row")
    (duc0,), (dlng, dlnb) = _rw_bwd("conv_ln_bwd", _f_ln_silu, [uc0], [lng_full, lnb_full], [dn0], [F32])
    du0, ddw = _conv_bwd(u0, duc0, dw_full)
    (dz0,), (dbin,) = _rw_bwd("conv_glu_bwd", _f_glu, [z0], [bin_full], [du0], [BF])
    gw["conv_w_in"] = dw_mm("conv_in_dw", hn0, dz0, "col")
    dhn0 = _mm_nt("conv_in_dx", dz0, w_in_f, "col")
    (dx,), (dgp, dsh, dsc) = _rw_bwd("conv_pre_bwd", _f_pre, [h0], [vec(gpm[0]), mix[0][0], mix[0][1]], [dhn0], [F32],
                                     add_rows={0: dh1})
    gvec["mix0"] = (dsh, dsc, dgate)
    gvec["g_pre_mix0"] = dgp
    reduce_end(rs3, [dx] + early_updates(red))

    vec_list = [*gvec["mix0"], *gvec["mix1"], *gvec["mlp0"], *gvec["mlp1"], *gvec["kv"],
                gvec["g_pre_mix0"], gvec["g_pre_mix1"], gvec["g_post_mix0"], gvec["g_post_mix1"],
                gvec["g_pre_mlp0"], gvec["g_pre_mlp1"], gvec["g_post_mlp0"], gvec["g_post_mlp1"],
                gvec["g_kv"], gvec["g_ckv"], gvec["g_cq"], dbin, dlng, dlnb, dbout, ddw.reshape(1, -1)]
    vo = [0]
    for v_ in vec_list:
        vo.append(vo[-1] + v_.shape[1])
    vec_pad = (-vo[-1]) % (8 * LANES)
    n_vec = vo[-1] + vec_pad
    flat = jnp.concatenate(vec_list + [jnp.zeros((1, vec_pad), F32)], axis=1).reshape(8, n_vec // 8)
    all_vecs = _all_gather_small("gather_vector_grads", flat).reshape(N_DEV, 8, n_vec // 8)
    rs4, token = reduce_begin("r4", ["conv_w_in", "conv_w_out"], after=all_vecs)
    all_vecs = after_token(all_vecs, token)
    summed = _sum_devices(all_vecs).reshape(1, n_vec)
    per_dev = all_vecs.reshape(N_DEV, n_vec)
    seg = lambda a, i: a[:, vo[i]:vo[i + 1]]

    out = {"loss": loss, "x": dx.reshape(x.shape)}
    dm_mix = [jnp.concatenate([seg(per_dev, 3 * l + i) for i in range(3)], axis=1) for l in range(2)]
    dm_mlp = [jnp.concatenate([seg(per_dev, 6 + 3 * l + i) for i in range(3)], axis=1) for l in range(2)]
    dm_kv = jnp.concatenate([seg(per_dev, 12), seg(per_dev, 13)], axis=1)
    ada_dw = lambda name, dm, width: _mm_tn(name, sc_all, chip_cols(dm, width), "row")
    out["w_ada_mix"] = [ada_dw("ada_mix_dw_%d" % l, dm_mix[l], n_ada) for l in range(2)]
    out["w_ada_mlp"] = [ada_dw("ada_mlp_dw_%d" % l, dm_mlp[l], n_ada) for l in range(2)]
    out["w_ada_kv"] = ada_dw("ada_kv_dw", dm_kv, n_kvada)
    sum_seg = lambda i: seg(summed, i)
    out["b_ada_mix"] = jnp.concatenate([jnp.concatenate([sum_seg(3 * l + i) for i in range(3)], axis=1) for l in range(2)], axis=0)
    out["b_ada_mlp"] = jnp.concatenate([jnp.concatenate([sum_seg(6 + 3 * l + i) for i in range(3)], axis=1) for l in range(2)], axis=0)
    out["b_ada_kv"] = jnp.concatenate([sum_seg(12), sum_seg(13)], axis=1).reshape(-1)
    out["g_pre_mix"] = jnp.concatenate([sum_seg(14), sum_seg(15)], axis=0)
    out["g_post_mix"] = jnp.concatenate([sum_seg(16), sum_seg(17)], axis=0)
    out["g_pre_mlp"] = jnp.concatenate([sum_seg(18), sum_seg(19)], axis=0)
    out["g_post_mlp"] = jnp.concatenate([sum_seg(20), sum_seg(21)], axis=0)
    out["g_kv"] = sum_seg(22).reshape(-1)
    out["g_ckv"] = sum_seg(23).reshape(-1)
    out["g_cq"] = sum_seg(24)
    out["conv_b_in"] = chip_cols(sum_seg(25), 2 * dl)
    out["conv_ln_g"] = chip_cols(sum_seg(26), dl)
    out["conv_ln_b"] = chip_cols(sum_seg(27), dl)
    out["conv_b_out"] = chip_cols(sum_seg(28), dl)
    ddw_sum = chip_cols(sum_seg(29).reshape(CONV_WIDTH + 1, d), dl)
    out["conv_dw"] = ddw_sum[:CONV_WIDTH].reshape(1, CONV_WIDTH, dl)
    out["conv_dw_b"] = ddw_sum[CONV_WIDTH:]

    def last_group(after):
        reduce_end(rs4, after)
        return {"conv_w_in": red["conv_w_in"][None], "conv_w_out": red["conv_w_out"][None]}

    out["w_o"] = red["w_o"][None]
    out["mlp_w_up"] = [red["up0"], red["up1"]]
    out["mlp_w_down"] = [red["down0"], red["down1"]]
    rs = red["small"]
    piece = lambda i: rs[:, so[i]:so[i + 1]]
    out["w_dkv"] = piece(0)
    out["w_kr"] = piece(1)[:, :QK_ROPE]
    out["w_dq"] = piece(2)[None]
    out["w_uk"], out["w_uv"] = piece(3), piece(4)
    out["w_uq"] = jnp.concatenate([piece(5).reshape(r_q, heads_l, QK_NOPE), piece(6).reshape(r_q, heads_l, QK_ROPE)],
                                  axis=2).reshape(1, r_q, -1)
    return out, last_group
```

```python
import functools

import jax
import jax.numpy as jnp
from jax import lax
from jax.experimental import pallas as pl
from jax.experimental.pallas import tpu as pltpu

F32 = jnp.float32
BF = jnp.bfloat16
MXU_DTYPE = BF

EPS = 1e-6
NEG = -1e30
ROPE_THETA = 10000.0
QK_NOPE = 128
QK_ROPE = 64
V_HEAD = 128
CONV_WIDTH = 31
ADAM_LR, ADAM_B1, ADAM_B2, ADAM_EPS, ADAM_WD, ADAM_STEP = 0.001, 0.9, 0.999, 1e-08, 0.01, 10

N_CHIPS = 4
N_DEV = 8
LANES = 128
VMEM_LIMIT = 56 * 2 ** 20
MESH = pl.DeviceIdType.MESH
ANY = pl.BlockSpec(memory_space=pl.ANY)


def _params(sem=None):
    return pltpu.CompilerParams(dimension_semantics=sem, vmem_limit_bytes=VMEM_LIMIT)


def _tile(n, pref, unit=LANES):
    if n <= pref:
        return n
    t = (pref // unit) * unit
    while t > unit and n % t:
        t -= unit
    assert n % t == 0, (n, pref)
    return t


def _dg(a, b, ca, cb):
    return lax.dot_general(a.astype(MXU_DTYPE), b.astype(MXU_DTYPE), (((ca,), (cb,)), ((), ())),
                           preferred_element_type=F32)


@jax.custom_vjp
def _dot_nn(a, b):
    return _dg(a, b, 1, 0)


def _dot_nn_fwd(a, b):
    return _dg(a, b, 1, 0), (a, b)


def _dot_nn_bwd(res, g):
    a, b = res
    return _dg(g, b, 1, 1).astype(a.dtype), _dg(a, g, 0, 0).astype(b.dtype)


_dot_nn.defvjp(_dot_nn_fwd, _dot_nn_bwd)


@jax.custom_vjp
def _dot_nt(a, b):
    return _dg(a, b, 1, 1)


def _dot_nt_fwd(a, b):
    return _dg(a, b, 1, 1), (a, b)


def _dot_nt_bwd(res, g):
    a, b = res
    return _dg(g, b, 1, 0).astype(a.dtype), _dg(g, a, 0, 0).astype(b.dtype)


_dot_nt.defvjp(_dot_nt_fwd, _dot_nt_bwd)


def _matmul(name, a, b, *, ca, cb, grid, a_blk, a_map, b_blk, b_map, o_shape, o_dtypes, o_blk, o_map,
            extras=(), epilogue=None):
    nk = grid[-1]
    n_ex, n_out = len(extras), len(o_dtypes)

    def body(*refs):
        a_ref, b_ref = refs[0], refs[1]
        ex_refs = refs[2:2 + n_ex]
        out_refs = refs[2 + n_ex:2 + n_ex + n_out]
        kk = pl.program_id(len(grid) - 1)

        def finish(acc):
            outs = (acc,) if epilogue is None else epilogue(acc, *[r[...] for r in ex_refs])
            for r, o in zip(out_refs, outs):
                r[...] = o.astype(r.dtype)

        part = _dg(a_ref[...], b_ref[...], ca, cb)
        if nk == 1:
            finish(part)
        else:
            acc_ref = refs[-1]

            @pl.when(kk == 0)
            def _():
                acc_ref[...] = part

            @pl.when(kk > 0)
            def _():
                acc_ref[...] += part

            @pl.when(kk == nk - 1)
            def _():
                finish(acc_ref[...])

    o_spec = pl.BlockSpec(o_blk, o_map)
    acc_shape = tuple(d for d in o_blk if d is not None)
    res = pl.pallas_call(
        body, grid=grid,
        in_specs=[pl.BlockSpec(a_blk, a_map), pl.BlockSpec(b_blk, b_map)] + [o_spec] * n_ex,
        out_specs=[o_spec] * n_out,
        out_shape=[jax.ShapeDtypeStruct(o_shape, dt) for dt in o_dtypes],
        scratch_shapes=[] if nk == 1 else [pltpu.VMEM(acc_shape, F32)],
        compiler_params=_params(("parallel",) * (len(grid) - 1) + ("arbitrary",)),
        name=name,
    )(a, b, *extras)
    return res[0] if n_out == 1 else res


TM, TN, TK = 2048, 512, 2048


def _mm_nn(name, a, w, layout, o_dtypes=(F32,), extras=(), epilogue=None, tiles=(TM, TN, TK), w_layer=None):
    m, k = a.shape
    tm, tk = _tile(m, tiles[0], 8), _tile(k, tiles[2])
    if layout == "row":
        n = w.shape[-1]
        tn = _tile(n, tiles[1])
        grid = (m // tm, n // tn, k // tk)
        b_blk, b_map = (tk, tn), (lambda i, j, kk: (kk, j))
        if w_layer is not None:
            b_blk, b_map = (None, tk, tn), (lambda i, j, kk: (w_layer, kk, j))
    else:
        nl = w.shape[2]
        n = nl * w.shape[0]
        tn = _tile(nl, tiles[1])
        nb = nl // tn
        grid = (m // tm, n // tn, k // tk)
        b_blk, b_map = (None, tk, tn), (lambda i, j, kk: (j // nb, kk, j % nb))
    return _matmul(name, a, w, ca=1, cb=0, grid=grid, a_blk=(tm, tk), a_map=lambda i, j, kk: (i, kk),
                   b_blk=b_blk, b_map=b_map, o_shape=(m, n), o_dtypes=o_dtypes, o_blk=(tm, tn),
                   o_map=lambda i, j, kk: (i, j), extras=extras, epilogue=epilogue)


def _mm_nt(name, g, w, layout, o_dtypes=(F32,), extras=(), epilogue=None, tiles=(TM, TN, TK)):
    m, n = g.shape
    tm, tn = _tile(m, tiles[0], 8), None
    if layout == "row":
        k = w.shape[0]
        tn = _tile(n, tiles[2])
        tk = _tile(k, tiles[1])
        grid = (m // tm, k // tk, n // tn)
        b_blk, b_map = (tk, tn), (lambda i, j, kk: (j, kk))
    else:
        k, nl = w.shape[1], w.shape[2]
        tn = _tile(nl, tiles[2])
        nb = nl // tn
        tk = _tile(k, tiles[1])
        grid = (m // tm, k // tk, n // tn)
        b_blk, b_map = (None, tk, tn), (lambda i, j, kk: (kk // nb, j, kk % nb))
    return _matmul(name, g, w, ca=1, cb=1, grid=grid, a_blk=(tm, tn), a_map=lambda i, j, kk: (i, kk),
                   b_blk=b_blk, b_map=b_map, o_shape=(m, k), o_dtypes=o_dtypes, o_blk=(tm, tk),
                   o_map=lambda i, j, kk: (i, j), extras=extras, epilogue=epilogue)


def _mm_tn(name, x, g, layout, o_dtype=F32, tiles=(TM, TN, TK)):
    t, k = x.shape
    n = g.shape[1]
    tt = _tile(t, tiles[2], 8)
    tk = _tile(k, tiles[0])
    if layout == "row":
        tn = _tile(n, tiles[1])
        o_shape, o_blk, o_map = (k, n), (tk, tn), (lambda i, j, kk: (i, j))
    else:
        nl = n // N_CHIPS
        tn = _tile(nl, tiles[1])
        nb = nl // tn
        o_shape, o_blk, o_map = (N_CHIPS, k, nl), (None, tk, tn), (lambda i, j, kk: (j // nb, i, j % nb))
    grid = (k // tk, n // tn, t // tt)
    return _matmul(name, x, g, ca=0, cb=0, grid=grid, a_blk=(tt, tk), a_map=lambda i, j, kk: (kk, i),
                   b_blk=(tt, tn), b_map=lambda i, j, kk: (kk, j), o_shape=o_shape, o_dtypes=(o_dtype,),
                   o_blk=o_blk, o_map=o_map)


TR = 256


def _rw_fwd(name, fn, rows, vecs, o_dtypes, tr=TR):
    t = rows[0].shape[0]
    tr = min(tr, t)
    n_r, n_v = len(rows), len(vecs)
    o_sds = jax.eval_shape(fn, *[jax.ShapeDtypeStruct((tr, r.shape[1]), r.dtype) for r in rows],
                           *[jax.ShapeDtypeStruct(v.shape, v.dtype) for v in vecs])

    def body(*refs):
        outs = fn(*[r[...] for r in refs[:n_r + n_v]])
        for r, o in zip(refs[n_r + n_v:], outs):
            r[...] = o.astype(r.dtype)

    res = pl.pallas_call(
        body, grid=(t // tr,),
        in_specs=[pl.BlockSpec((tr, r.shape[1]), lambda i: (i, 0)) for r in rows]
        + [pl.BlockSpec(v.shape, lambda i: (0, 0)) for v in vecs],
        out_specs=[pl.BlockSpec((tr, o.shape[1]), lambda i: (i, 0)) for o in o_sds],
        out_shape=[jax.ShapeDtypeStruct((t, o.shape[1]), dt) for o, dt in zip(o_sds, o_dtypes)],
        compiler_params=_params(("parallel",)), name=name,
    )(*rows, *vecs)
    return res


def _rw_bwd(name, fn, rows, vecs, cots, row_grad_dtypes, add_rows=None, tr=TR):
    t = rows[0].shape[0]
    tr = min(tr, t)
    add_rows = add_rows or {}
    n_r, n_v, n_c = len(rows), len(vecs), len(cots)
    diff = [i for i, dt in enumerate(row_grad_dtypes) if dt is not None]
    adds = [add_rows[i] for i in diff if i in add_rows]
    n_a = len(adds)

    def body(*refs):
        rvals = [r[...] for r in refs[:n_r]]
        vvals = [r[...] for r in refs[n_r:n_r + n_v]]
        c_refs = refs[n_r + n_v:n_r + n_v + n_c]
        a_refs = list(refs[n_r + n_v + n_c:n_r + n_v + n_c + n_a])
        o_refs = refs[n_r + n_v + n_c + n_a:]

        def f(*d):
            full = list(rvals)
            for i, x in zip(diff, d[:len(diff)]):
                full[i] = x
            return fn(*full, *d[len(diff):])

        outs, vjp = jax.vjp(f, *[rvals[i] for i in diff], *vvals)
        grads = vjp(tuple(c[...].astype(o.dtype) for c, o in zip(c_refs, outs)))
        for n, i in enumerate(diff):
            gr = grads[n].astype(F32)
            if i in add_rows:
                gr = gr + a_refs.pop(0)[...].astype(F32)
            o_refs[n][...] = gr.astype(o_refs[n].dtype)
        first = pl.program_id(0) == 0
        for r, gv in zip(o_refs[len(diff):], grads[len(diff):]):
            @pl.when(first)
            def _(r=r, gv=gv):
                r[...] = gv

            @pl.when(jnp.logical_not(first))
            def _(r=r, gv=gv):
                r[...] += gv

    row_spec = lambda a: pl.BlockSpec((tr, a.shape[1]), lambda i: (i, 0))
    vec_spec = lambda a: pl.BlockSpec(a.shape, lambda i: (0, 0))
    res = pl.pallas_call(
        body, grid=(t // tr,),
        in_specs=[row_spec(r) for r in rows] + [vec_spec(v) for v in vecs] + [row_spec(c) for c in cots]
        + [row_spec(a) for a in adds],
        out_specs=[row_spec(rows[i]) for i in diff] + [vec_spec(v) for v in vecs],
        out_shape=[jax.ShapeDtypeStruct(rows[i].shape, row_grad_dtypes[i]) for i in diff]
        + [jax.ShapeDtypeStruct(v.shape, F32) for v in vecs],
        compiler_params=_params(("arbitrary",)), name=name,
    )(*rows, *vecs, *cots, *adds)
    return res[:len(diff)], res[len(diff):]


def _rms(x, g):
    return x * lax.rsqrt(jnp.mean(x * x, axis=-1, keepdims=True) + EPS) * g


def _f_pre(h, g, shift, scale):
    return (_rms(h, g) * (1.0 + scale) + shift,)


def _f_pre2(h, g1, sh1, sc1, g2, sh2, sc2):
    return _f_pre(h, g1, sh1, sc1) + _f_pre(h, g2, sh2, sc2)


def _f_post(y, gate, g):
    return (gate * _rms(y, g),)


def _f_post_bias(y, bias, gate, g):
    return (gate * _rms(y + bias, g),)


def _f_glu(z, bias):
    z = z + bias
    half = z.shape[1] // 2
    return (z[:, :half] * jax.nn.sigmoid(z[:, half:]),)


def _f_ln_silu(u, g, b):
    mu = jnp.mean(u, axis=-1, keepdims=True)
    var = jnp.mean(jnp.square(u - mu), axis=-1, keepdims=True)
    y = (u - mu) * lax.rsqrt(var + EPS) * g + b
    return (y * jax.nn.sigmoid(y),)


def _rope_raw(x, cos, sin):
    n = x.shape[1]
    reps = n // LANES
    if reps > 1:
        cos = jnp.concatenate([cos] * reps, axis=1)
        sin = jnp.concatenate([sin] * reps, axis=1)
    lane = lax.broadcasted_iota(jnp.int32, x.shape, 1)
    half = QK_ROPE // 2
    partner = jnp.where((lane % QK_ROPE) < half, pltpu.roll(x, n - half, 1), pltpu.roll(x, half, 1))
    return x * cos + partner * sin


@jax.custom_vjp
def _rope(x, cos, sin):
    return _rope_raw(x, cos, sin)


def _rope_fwd(x, cos, sin):
    return _rope_raw(x, cos, sin), (cos, sin)


def _rope_bwd(res, g):
    cos, sin = res
    return _rope_raw(g, cos, -sin), jnp.zeros_like(cos), jnp.zeros_like(sin)


_rope.defvjp(_rope_fwd, _rope_bwd)


def _make_f_lat(r_kv):
    def f(pre_kv, pre_q, cos, sin, g_ckv, g_cq):
        c_kv = _rms(pre_kv[:, :r_kv], g_ckv)
        kr = _rope(pre_kv[:, r_kv:], cos, sin)
        return c_kv, kr, _rms(pre_q, g_cq)
    return f


def _f_rope(x, cos, sin):
    return (_rope(x, cos, sin),)


CONV_TC = 128
CONV_TT = 256
PADR = 32


def _conv_fwd(u, dw, dw_b):
    t, d = u.shape
    tc, tt = min(CONV_TC, d), min(CONV_TT, t)
    off = PADR - (CONV_WIDTH - 1)

    def body(u_ref, w_ref, b_ref, o_ref, pad_ref):
        pad_ref[pl.ds(0, PADR), :] = jnp.zeros((PADR, tc), F32)
        pad_ref[pl.ds(PADR, t), :] = u_ref[...]
        for t0 in range(0, t, tt):
            acc = jnp.zeros((tt, tc), F32) + b_ref[...]
            for j in range(CONV_WIDTH):
                acc = acc + pad_ref[pl.ds(t0 + off + j, tt), :] * w_ref[pl.ds(j, 1), :]
            o_ref[pl.ds(t0, tt), :] = acc

    return pl.pallas_call(
        body, grid=(d // tc,),
        in_specs=[pl.BlockSpec((t, tc), lambda j: (0, j)), pl.BlockSpec((CONV_WIDTH, tc), lambda j: (0, j)),
                  pl.BlockSpec((1, tc), lambda j: (0, j))],
        out_specs=pl.BlockSpec((t, tc), lambda j: (0, j)),
        out_shape=jax.ShapeDtypeStruct((t, d), F32),
        scratch_shapes=[pltpu.VMEM((t + PADR, tc), F32)],
        compiler_params=_params(("parallel",)), name="conv_fwd",
    )(u, dw, dw_b)


def _conv_bwd(u, duc, dw):
    t, d = u.shape
    tc, tt = min(CONV_TC, d), min(CONV_TT, t)
    off = PADR - (CONV_WIDTH - 1)

    def body(u_ref, g_ref, w_ref, du_ref, dwt_ref, padu_ref, padg_ref):
        padu_ref[pl.ds(0, PADR), :] = jnp.zeros((PADR, tc), F32)
        padu_ref[pl.ds(PADR, t), :] = u_ref[...]
        padg_ref[pl.ds(t, PADR), :] = jnp.zeros((PADR, tc), F32)
        padg_ref[pl.ds(0, t), :] = g_ref[...]
        for t0 in range(0, t, tt):
            acc = jnp.zeros((tt, tc), F32)
            for j in range(CONV_WIDTH):
                acc = acc + padg_ref[pl.ds(t0 + (CONV_WIDTH - 1) - j, tt), :] * w_ref[pl.ds(j, 1), :]
            du_ref[pl.ds(t0, tt), :] = acc
        for j in range(CONV_WIDTH):
            acc = jnp.zeros((tt, tc), F32)
            for t0 in range(0, t, tt):
                acc = acc + g_ref[pl.ds(t0, tt), :] * padu_ref[pl.ds(t0 + off + j, tt), :]
            dwt_ref[pl.ds(j, 1), :] = jnp.sum(acc, axis=0, keepdims=True)
        acc = jnp.zeros((tt, tc), F32)
        for t0 in range(0, t, tt):
            acc = acc + g_ref[pl.ds(t0, tt), :]
        dwt_ref[pl.ds(CONV_WIDTH, 1), :] = jnp.sum(acc, axis=0, keepdims=True)

    col = lambda r: pl.BlockSpec((r, tc), lambda j: (0, j))
    return pl.pallas_call(
        body, grid=(d // tc,),
        in_specs=[col(t), col(t), col(CONV_WIDTH)],
        out_specs=[col(t), col(CONV_WIDTH + 1)],
        out_shape=[jax.ShapeDtypeStruct((t, d), F32), jax.ShapeDtypeStruct((CONV_WIDTH + 1, d), F32)],
        scratch_shapes=[pltpu.VMEM((t + PADR, tc), F32), pltpu.VMEM((t + PADR, tc), F32)],
        compiler_params=_params(("parallel",)), name="conv_bwd",
    )(u, duc, dw)


@jax.custom_vjp
def _swap_halves(x):
    return pltpu.roll(x, LANES // 2, 1)


_swap_halves.defvjp(lambda x: (pltpu.roll(x, LANES // 2, 1), None), lambda _, g: (pltpu.roll(g, LANES // 2, 1),))


def _attn_block(qn, qr, kn, kr, v, q0):
    scale = (QK_NOPE + QK_ROPE) ** -0.5
    lane = lax.broadcasted_iota(jnp.int32, kr.shape, 1)
    kr_a = kr * (lane < QK_ROPE).astype(kr.dtype)
    kr_b = _swap_halves(kr_a)
    outs = []
    for hh, kr_h in ((0, kr_a), (1, kr_b)):
        sl = slice(hh * QK_NOPE, (hh + 1) * QK_NOPE)
        s = _dot_nt(jnp.concatenate([qn[:, sl], qr], axis=1), jnp.concatenate([kn[:, sl], kr_h], axis=1)) * scale
        row = lax.broadcasted_iota(jnp.int32, s.shape, 0) + q0
        col = lax.broadcasted_iota(jnp.int32, s.shape, 1)
        s = jnp.where(col <= row, s, NEG)
        e = jnp.exp(s - jnp.max(s, axis=-1, keepdims=True))
        p = e / jnp.sum(e, axis=-1, keepdims=True)
        outs.append(_dot_nn(p, v[:, sl]))
    return jnp.concatenate(outs, axis=1)


def _attn_fwd(qn, qr, kn, kr, v, tq=512):
    t, w = qn.shape
    pairs = w // (2 * QK_NOPE)
    tq = min(tq, t)
    pw = 2 * QK_NOPE

    def body(qn_ref, qr_ref, kn_ref, kr_ref, v_ref, o_ref):
        for q0 in range(0, t, tq):
            l = q0 + tq
            o_ref[pl.ds(q0, tq), :] = _attn_block(
                qn_ref[pl.ds(q0, tq), :], qr_ref[pl.ds(q0, tq), :], kn_ref[pl.ds(0, l), :], kr_ref[pl.ds(0, l), :],
                v_ref[pl.ds(0, l), :], q0).astype(o_ref.dtype)

    pair = lambda wd: pl.BlockSpec((t, wd), lambda p: (0, p))
    return pl.pallas_call(
        body, grid=(pairs,),
        in_specs=[pair(pw), pair(LANES), pair(pw), pl.BlockSpec((t, LANES), lambda p: (0, 0)), pair(pw)],
        out_specs=pair(pw), out_shape=jax.ShapeDtypeStruct((t, w), BF),
        compiler_params=_params(("parallel",)), name="attn_fwd",
    )(qn, qr, kn, kr, v)


def _attn_bwd(qn, qr, kn, kr, v, do, tq=256):
    t, w = qn.shape
    pairs = w // (2 * QK_NOPE)
    tq = min(tq, t)
    pw = 2 * QK_NOPE

    def body(qn_ref, qr_ref, kn_ref, kr_ref, v_ref, do_ref, dqn_ref, dqr_ref, dkn_ref, dv_ref, dkr_ref,
             akn_ref, av_ref, akr_ref):
        akn_ref[...] = jnp.zeros_like(akn_ref)
        av_ref[...] = jnp.zeros_like(av_ref)
        akr_ref[...] = jnp.zeros_like(akr_ref)
        for q0 in range(0, t, tq):
            l = q0 + tq
            rows, keys = pl.ds(q0, tq), pl.ds(0, l)
            _, vjp = jax.vjp(functools.partial(_attn_block, q0=q0), qn_ref[rows, :], qr_ref[rows, :],
                             kn_ref[keys, :], kr_ref[keys, :], v_ref[keys, :])
            dqn, dqr, dkn, dkr, dv = vjp(do_ref[rows, :].astype(F32))
            dqn_ref[rows, :] = dqn.astype(dqn_ref.dtype)
            dqr_ref[rows, :] = dqr.astype(dqr_ref.dtype)
            akn_ref[keys, :] += dkn.astype(F32)
            av_ref[keys, :] += dv.astype(F32)
            akr_ref[keys, :] += dkr.astype(F32)
        dkn_ref[...] = akn_ref[...].astype(dkn_ref.dtype)
        dv_ref[...] = av_ref[...].astype(dv_ref.dtype)
        first = pl.program_id(0) == 0

        @pl.when(first)
        def _():
            dkr_ref[...] = akr_ref[...]

        @pl.when(jnp.logical_not(first))
        def _():
            dkr_ref[...] += akr_ref[...]

    pair = lambda wd: pl.BlockSpec((t, wd), lambda p: (0, p))
    shared = pl.BlockSpec((t, LANES), lambda p: (0, 0))
    sds = jax.ShapeDtypeStruct
    return pl.pallas_call(
        body, grid=(pairs,),
        in_specs=[pair(pw), pair(LANES), pair(pw), shared, pair(pw), pair(pw)],
        out_specs=[pair(pw), pair(LANES), pair(pw), pair(pw), shared],
        out_shape=[sds((t, w), BF), sds((t, pairs * LANES), F32), sds((t, w), BF), sds((t, w), BF), sds((t, LANES), F32)],
        scratch_shapes=[pltpu.VMEM((t, pw), F32), pltpu.VMEM((t, pw), F32), pltpu.VMEM((t, LANES), F32)],
        compiler_params=_params(("arbitrary",)), name="attn_bwd",
    )(qn, qr, kn, kr, v, do)


def _ew(name, fn, ins, o_dtypes, max_bytes=2 ** 21):
    r, c = ins[0].shape
    tr = r
    if r * c * 4 > max_bytes:
        tr = max(16, (max_bytes // (c * 4)) // 16 * 16)
        while r % tr:
            tr -= 16
    n_in = len(ins)

    def body(*refs):
        outs = fn(*[x[...] for x in refs[:n_in]])
        for o_ref, o in zip(refs[n_in:], outs):
            o_ref[...] = o.astype(o_ref.dtype)

    spec = pl.BlockSpec((tr, c), lambda i: (i, 0))
    return pl.pallas_call(
        body, grid=(r // tr,), in_specs=[spec] * n_in, out_specs=[spec] * len(o_dtypes),
        out_shape=[jax.ShapeDtypeStruct((r, c), dt) for dt in o_dtypes],
        compiler_params=_params(("parallel",)), name=name,
    )(*ins)


def _adamw_math(w, g, m, v):
    m = ADAM_B1 * m + (1.0 - ADAM_B1) * g
    v = ADAM_B2 * v + (1.0 - ADAM_B2) * jnp.square(g)
    m_hat = m / (1.0 - ADAM_B1 ** ADAM_STEP)
    v_hat = v / (1.0 - ADAM_B2 ** ADAM_STEP)
    delta = -ADAM_LR * (m_hat / (jnp.sqrt(v_hat) + ADAM_EPS) + ADAM_WD * w)
    return delta, m, v


def _adamw(name, w, g, m, v):
    shape = w.shape
    to2 = lambda a: a.reshape(-1, shape[-1]) if a.ndim > 1 else a.reshape(1, -1)
    d, nm, nv = _ew("adamw_" + name, _adamw_math, [to2(w), to2(g), to2(m), to2(v)], (F32, F32, F32))
    return d.reshape(shape), nm.reshape(shape), nv.reshape(shape)


def _adamw_layer(name, w, g, m, v, layer, prev=()):
    _, r, n = w.shape
    tr = r
    while tr * n * 4 > 2 ** 21 and tr % 16 == 0:
        tr //= 2

    def body(*refs):
        w_ref, g_ref, m_ref, v_ref = refs[:4]
        og, od, om, ov = refs[-4:]
        gval = g_ref[...].astype(F32)
        d, nm, nv = _adamw_math(w_ref[...], gval, m_ref[...], v_ref[...])
        og[...] = gval
        od[...] = d
        om[...] = nm
        ov[...] = nv

    lay = pl.BlockSpec((None, tr, n), lambda i: (layer, i, 0))
    return pl.pallas_call(
        body, grid=(r // tr,),
        in_specs=[lay, pl.BlockSpec((tr, n), lambda i: (i, 0)), lay, lay] + [ANY] * len(prev),
        out_specs=[lay] * 4, out_shape=[jax.ShapeDtypeStruct(w.shape, F32)] * 4,
        input_output_aliases={4 + k: k for k in range(len(prev))},
        compiler_params=_params(("parallel",)), name="adamw_%s_%d" % (name, layer),
    )(w, g, m, v, *prev)


def _rope_tables(positions, t):
    half = QK_ROPE // 2
    inv = 1.0 / (ROPE_THETA ** (jnp.arange(0, QK_ROPE, 2, dtype=F32) / QK_ROPE))
    inv_l = jnp.tile(inv, LANES // half).reshape(1, LANES)
    sign = jnp.tile(jnp.concatenate([-jnp.ones((half,), F32), jnp.ones((half,), F32)]), LANES // QK_ROPE).reshape(1, LANES)

    def body(p_ref, inv_ref, sg_ref, c_ref, s_ref):
        ang = p_ref[...].astype(F32) * inv_ref[...]
        c_ref[...] = jnp.cos(ang)
        s_ref[...] = jnp.sin(ang) * sg_ref[...]

    return pl.pallas_call(body, out_shape=[jax.ShapeDtypeStruct((t, LANES), F32)] * 2, name="rope_tables")(
        positions.reshape(t, 1), inv_l, sign)


def _loss_grad(h, target):
    t, d = h.shape
    tr = min(TR, t)

    def body(h_ref, y_ref, dh_ref, l_ref):
        err = h_ref[...] - y_ref[...]
        dh_ref[...] = err * (1.0 / d)
        part = 0.5 * jnp.sum(jnp.mean(jnp.square(err), axis=-1, keepdims=True), axis=0, keepdims=True)
        first = pl.program_id(0) == 0

        @pl.when(first)
        def _():
            l_ref[...] = part

        @pl.when(jnp.logical_not(first))
        def _():
            l_ref[...] += part

    row = pl.BlockSpec((tr, d), lambda i: (i, 0))
    return pl.pallas_call(
        body, grid=(t // tr,), in_specs=[row, row],
        out_specs=[row, pl.BlockSpec((1, 1), lambda i: (0, 0))],
        out_shape=[jax.ShapeDtypeStruct((t, d), F32), jax.ShapeDtypeStruct((1, 1), F32)],
        compiler_params=_params(("arbitrary",)), name="loss_grad",
    )(h, target)


def _sum_devices(g):
    def fn(*parts):
        acc = parts[0]
        for p in parts[1:]:
            acc = acc + p
        return (acc,)
    return _ew("sum_devices", fn, [g[i] for i in range(g.shape[0])], (F32,))[0]


def _place():
    x, y, c = lax.axis_index("x"), lax.axis_index("y"), lax.axis_index("c")
    return x, y, c, [(1 - x, y), (x, 1 - y), (1 - x, 1 - y)]


def _all_gather_small(name, v):
    r, n = v.shape

    def body(x_ref, out_ref, send_sems, recv_sems, local_sem):
        x, y, c, chips = _place()
        me, sibling = (x, y, c), (x, y, 1 - c)

        def rows(px, py, pc):
            return out_ref.at[4 * px + 2 * py + pc]

        def copy(k, block, to, src=None):
            return pltpu.make_async_remote_copy(
                src_ref=rows(*block) if src is None else src, dst_ref=rows(*block), send_sem=send_sems.at[k],
                recv_sem=recv_sems.at[k], device_id=to, device_id_type=MESH)

        mine = pltpu.make_async_copy(x_ref, rows(*me), local_sem)
        mine.start()
        first = [copy(0, me, sibling, src=x_ref)]
        first += [copy(1 + j, me, (*chip, c), src=x_ref) for j, chip in enumerate(chips)]
        for cp in first:
            cp.start()
        passed = [copy(4 + j, (*chip, c), sibling) for j, chip in enumerate(chips)]
        for j, chip in enumerate(chips):
            copy(1 + j, (*chip, c), me).wait_recv()
            passed[j].start()
        copy(0, sibling, me).wait_recv()
        for j, chip in enumerate(chips):
            copy(4 + j, (*chip, 1 - c), me).wait_recv()
        for cp in first + passed:
            cp.wait_send()
        mine.wait()

    return pl.pallas_call(
        body, out_shape=jax.ShapeDtypeStruct((N_DEV, r, n), v.dtype),
        in_specs=[pl.BlockSpec(memory_space=pltpu.VMEM)], out_specs=pl.BlockSpec(memory_space=pltpu.VMEM),
        scratch_shapes=[pltpu.SemaphoreType.DMA((7,)), pltpu.SemaphoreType.DMA((7,)), pltpu.SemaphoreType.DMA],
        compiler_params=pltpu.CompilerParams(vmem_limit_bytes=VMEM_LIMIT), name=name,
    )(v)


def _cast_into_slot(name, w, layer, idx, after=None):
    _, r, n = w.shape
    tr = r
    while tr * n * 4 > 2 ** 21 and tr % 32 == 0:
        tr //= 2
    order = [] if after is None else [after]

    def body(idx_ref, w_ref, *rest):
        o_ref = rest[-1]
        o_ref[...] = w_ref[...].astype(o_ref.dtype)

    return pl.pallas_call(
        body,
        grid_spec=pltpu.PrefetchScalarGridSpec(
            num_scalar_prefetch=1, grid=(r // tr,),
            in_specs=[pl.BlockSpec((None, tr, n), lambda i, idx_ref: (layer, i, 0))] + [ANY] * len(order),
            out_specs=pl.BlockSpec((None, tr, n), lambda i, idx_ref: (idx_ref[1], i, 0))),
        out_shape=jax.ShapeDtypeStruct((N_CHIPS, r, n), BF),
        compiler_params=_params(("parallel",)), name="cast_" + name,
    )(idx, w, *order)


def _swap_other_half(name, grads, after=None):
    n_w = len(grads)
    order = [] if after is None else [after]
    n_in = n_w + len(order)

    def body(*refs):
        ins, outs = refs[:n_w], refs[n_in:n_in + n_w]
        send_sems, recv_sems = refs[n_in + n_w:]
        x, y, c, _ = _place()
        cps = []
        for i in range(n_w):
            h = ins[i].shape[1] // 2
            cp = pltpu.make_async_remote_copy(
                src_ref=ins[i].at[:, pl.ds((1 - c) * h, h), :], dst_ref=outs[i], send_sem=send_sems.at[i],
                recv_sem=recv_sems.at[i], device_id=(x, y, 1 - c), device_id_type=MESH)
            cp.start()
            cps.append(cp)
        for cp in cps:
            cp.wait()

    return pl.pallas_call(
        body, in_specs=[ANY] * n_in, out_specs=[ANY] * n_w,
        out_shape=[jax.ShapeDtypeStruct((g.shape[0], g.shape[1] // 2, g.shape[2]), g.dtype) for g in grads],
        scratch_shapes=[pltpu.SemaphoreType.DMA((n_w,)), pltpu.SemaphoreType.DMA((n_w,))],
        name=name,
    )(*grads, *order)


def _add_my_half(name, g, s, idx, o_dtype):
    nc, r, n = g.shape
    h = r // 2
    tr = h
    while tr * n * 4 > 2 ** 21 and tr % 32 == 0:
        tr //= 2
    nb = h // tr

    def body(idx_ref, g_ref, s_ref, o_ref, own_ref):
        val = (g_ref[...].astype(F32) + s_ref[...].astype(F32)).astype(o_ref.dtype)
        o_ref[...] = val

        @pl.when(pl.program_id(1) == idx_ref[1])
        def _():
            own_ref[...] = val

    return pl.pallas_call(
        body,
        grid_spec=pltpu.PrefetchScalarGridSpec(
            num_scalar_prefetch=1, grid=(nb, nc),
            in_specs=[pl.BlockSpec((None, tr, n), lambda i, j, idx_ref: (j, idx_ref[0] * nb + i, 0)),
                      pl.BlockSpec((None, tr, n), lambda i, j, idx_ref: (j, i, 0))],
            out_specs=[pl.BlockSpec((None, tr, n), lambda i, j, idx_ref: (j, i, 0)),
                       pl.BlockSpec((None, tr, n), lambda i, j, idx_ref: (idx_ref[1], i, 0))]),
        out_shape=[jax.ShapeDtypeStruct((nc, h, n), o_dtype)] * 2,
        compiler_params=_params(("parallel", "arbitrary")), name=name,
    )(idx, g, s)


def _sum_chips(name, q, idx):
    nc, h, n = q.shape
    tr = h
    while tr * n * 4 > 2 ** 20 and tr % 32 == 0:
        tr //= 2
    nb = h // tr

    def body(idx_ref, q_ref, o_ref):
        acc = q_ref[0].astype(F32)
        for j in range(1, nc):
            acc = acc + q_ref[j].astype(F32)
        o_ref[...] = acc

    return pl.pallas_call(
        body,
        grid_spec=pltpu.PrefetchScalarGridSpec(
            num_scalar_prefetch=1, grid=(nb,),
            in_specs=[pl.BlockSpec((nc, tr, n), lambda i, idx_ref: (0, i, 0))],
            out_specs=pl.BlockSpec((tr, n), lambda i, idx_ref: (idx_ref[0] * nb + i, 0))),
        out_shape=jax.ShapeDtypeStruct((2 * h, n), F32),
        compiler_params=_params(("parallel",)), name=name,
    )(idx, q)


def _join_halves(name, bufs):
    n_w = len(bufs)

    def body(*refs):
        outs = refs[n_w:2 * n_w]
        send_sems, recv_sems = refs[2 * n_w:]
        x, y, c, _ = _place()

        def copy(i, which):
            h = outs[i].shape[0] // 2
            rows = outs[i].at[pl.ds(which * h, h)]
            return pltpu.make_async_remote_copy(
                src_ref=rows, dst_ref=rows, send_sem=send_sems.at[i], recv_sem=recv_sems.at[i],
                device_id=(x, y, 1 - c), device_id_type=MESH)

        cps = [copy(i, c) for i in range(n_w)]
        for cp in cps:
            cp.start()
        for i, cp in enumerate(cps):
            cp.wait_send()
            copy(i, 1 - c).wait_recv()

    return pl.pallas_call(
        body, in_specs=[ANY] * n_w, out_specs=[ANY] * n_w,
        out_shape=[jax.ShapeDtypeStruct(b.shape, b.dtype) for b in bufs],
        input_output_aliases={i: i for i in range(n_w)},
        scratch_shapes=[pltpu.SemaphoreType.DMA((n_w,)), pltpu.SemaphoreType.DMA((n_w,))],
        name=name,
    )(*bufs)


HBM_SPEC = pl.BlockSpec(memory_space=pltpu.HBM)
SEM_SPEC = pl.BlockSpec(memory_space=pltpu.SEMAPHORE)
VMEM_SPEC = pl.BlockSpec(memory_space=pltpu.VMEM)
SIDE_EFFECT = pltpu.SideEffectType.DATAFLOW_SIDE_EFFECTING


def _in_hbm(arrays):
    return [pltpu.with_memory_space_constraint(a, pltpu.HBM) for a in arrays]


def _ici_gather_copy(ref, i, j, chip, send_sems, recv_sems):
    x, y, c, _ = _place()
    h = ref.shape[1] // 2
    mine = ref.at[2 * x + y, pl.ds(c * h, h)]
    return pltpu.make_async_remote_copy(
        src_ref=mine, dst_ref=mine, send_sem=send_sems.at[3 * i + j], recv_sem=recv_sems.at[3 * i + j],
        device_id=(*chip, c), device_id_type=MESH)


def _gather_start(name, bufs, after=None):
    n_w = len(bufs)
    order = [] if after is None else [after]

    def body(*refs):
        ins, token = refs[:n_w], refs[-1]
        send_sems, recv_sems = refs[n_w + len(order)], refs[n_w + len(order) + 1]
        chips = _place()[3]
        for i in range(n_w):
            for j, chip in enumerate(chips):
                _ici_gather_copy(ins[i], i, j, chip, send_sems, recv_sems).start()
        token[...] = jnp.zeros_like(token)

    res = pl.pallas_call(
        body, name=name, in_specs=[HBM_SPEC] * n_w + [ANY] * len(order),
        out_shape=(pltpu.SemaphoreType.DMA((3 * n_w,)), pltpu.SemaphoreType.DMA((3 * n_w,)),
                   *[pltpu.HBM(b.shape, b.dtype) for b in bufs], jax.ShapeDtypeStruct((8, LANES), F32)),
        out_specs=(SEM_SPEC, SEM_SPEC, *[HBM_SPEC] * n_w, VMEM_SPEC),
        input_output_aliases={i: 2 + i for i in range(n_w)},
        compiler_params=pltpu.CompilerParams(has_side_effects=SIDE_EFFECT),
    )(*_in_hbm(bufs), *order)
    return res[0], res[1], list(res[2:2 + n_w]), res[-1]


def _gather_wait(name, send_sems, recv_sems, bufs, after):
    n_w = len(bufs)

    def body(*refs):
        ins, s_sems, r_sems = refs[:n_w], refs[n_w], refs[n_w + 1]
        chips = _place()[3]
        for i in range(n_w):
            for j, chip in enumerate(chips):
                cp = _ici_gather_copy(ins[i], i, j, chip, s_sems, r_sems)
                cp.wait_send()
                cp.wait_recv()

    return pl.pallas_call(
        body, name=name, in_specs=[HBM_SPEC] * n_w + [SEM_SPEC, SEM_SPEC, ANY],
        out_shape=[pltpu.HBM(b.shape, b.dtype) for b in bufs], out_specs=[HBM_SPEC] * n_w,
        input_output_aliases={i: i for i in range(n_w)},
        compiler_params=pltpu.CompilerParams(has_side_effects=SIDE_EFFECT),
    )(*bufs, send_sems, recv_sems, after)


def _forward_to_sibling(name, bufs):
    n_w = len(bufs)

    def body(*refs):
        outs = refs[n_w:2 * n_w]
        send_sems, recv_sems = refs[2 * n_w:]
        x, y, c, chips = _place()

        def copy(i, j, chip, which):
            h = outs[i].shape[1] // 2
            blk = outs[i].at[2 * chip[0] + chip[1], pl.ds(which * h, h)]
            return pltpu.make_async_remote_copy(
                src_ref=blk, dst_ref=blk, send_sem=send_sems.at[3 * i + j], recv_sem=recv_sems.at[3 * i + j],
                device_id=(x, y, 1 - c), device_id_type=MESH)

        sends = [copy(i, j, chip, c) for i in range(n_w) for j, chip in enumerate(chips)]
        for cp in sends:
            cp.start()
        for i in range(n_w):
            for j, chip in enumerate(chips):
                copy(i, j, chip, 1 - c).wait_recv()
        for cp in sends:
            cp.wait_send()

    return pl.pallas_call(
        body, in_specs=[ANY] * n_w, out_specs=[ANY] * n_w,
        out_shape=[jax.ShapeDtypeStruct(b.shape, b.dtype) for b in bufs],
        input_output_aliases={i: i for i in range(n_w)},
        scratch_shapes=[pltpu.SemaphoreType.DMA((3 * n_w,)), pltpu.SemaphoreType.DMA((3 * n_w,))],
        name=name,
    )(*bufs)


def _ici_scatter_copy(part, buf, i, j, chip, send_sems, recv_sems):
    x, y, c, _ = _place()
    return pltpu.make_async_remote_copy(
        src_ref=part.at[2 * chip[0] + chip[1]], dst_ref=buf.at[2 * x + y], send_sem=send_sems.at[3 * i + j],
        recv_sem=recv_sems.at[3 * i + j], device_id=(*chip, c), device_id_type=MESH)


def _ici_scatter_landing(buf, i, j, chip, send_sems, recv_sems):
    x, y, c, _ = _place()
    blk = buf.at[2 * chip[0] + chip[1]]
    return pltpu.make_async_remote_copy(
        src_ref=blk, dst_ref=blk, send_sem=send_sems.at[3 * i + j], recv_sem=recv_sems.at[3 * i + j],
        device_id=(*chip, c), device_id_type=MESH)


def _scatter_start(name, parts, bufs):
    n_w = len(parts)

    def body(*refs):
        ps, bs = refs[:n_w], refs[n_w:2 * n_w]
        send_sems, recv_sems, token = refs[2 * n_w], refs[2 * n_w + 1], refs[-1]
        chips = _place()[3]
        for i in range(n_w):
            for j, chip in enumerate(chips):
                _ici_scatter_copy(ps[i], bs[i], i, j, chip, send_sems, recv_sems).start()
        token[...] = jnp.zeros_like(token)

    both = list(parts) + list(bufs)
    res = pl.pallas_call(
        body, name=name, in_specs=[HBM_SPEC] * (2 * n_w),
        out_shape=(pltpu.SemaphoreType.DMA((3 * n_w,)), pltpu.SemaphoreType.DMA((3 * n_w,)),
                   *[pltpu.HBM(b.shape, b.dtype) for b in both], jax.ShapeDtypeStruct((8, LANES), F32)),
        out_specs=(SEM_SPEC, SEM_SPEC, *[HBM_SPEC] * (2 * n_w), VMEM_SPEC),
        input_output_aliases={i: 2 + i for i in range(2 * n_w)},
        compiler_params=pltpu.CompilerParams(has_side_effects=SIDE_EFFECT),
    )(*_in_hbm(both))
    return res[0], res[1], list(res[2:2 + n_w]), list(res[2 + n_w:2 + 2 * n_w]), res[-1]


def _scatter_wait(name, send_sems, recv_sems, parts, bufs, after):
    n_w = len(parts)
    after = list(after) if isinstance(after, (list, tuple)) else [after]

    def body(*refs):
        ps, bs = refs[:n_w], refs[n_w:2 * n_w]
        s_sems, r_sems = refs[2 * n_w], refs[2 * n_w + 1]
        chips = _place()[3]
        for i in range(n_w):
            for j, chip in enumerate(chips):
                _ici_scatter_copy(ps[i], bs[i], i, j, chip, s_sems, r_sems).wait_send()
                _ici_scatter_landing(bs[i], i, j, chip, s_sems, r_sems).wait_recv()

    both = list(parts) + list(bufs)
    res = pl.pallas_call(
        body, name=name, in_specs=[HBM_SPEC] * (2 * n_w) + [SEM_SPEC, SEM_SPEC] + [ANY] * len(after),
        out_shape=[pltpu.HBM(b.shape, b.dtype) for b in both], out_specs=[HBM_SPEC] * (2 * n_w),
        input_output_aliases={i: i for i in range(2 * n_w)},
        compiler_params=pltpu.CompilerParams(has_side_effects=SIDE_EFFECT),
    )(*both, send_sems, recv_sems, *after)
    return list(res[n_w:])


def _swap_copy(g, land, i, send_sems, recv_sems):
    x, y, c, _ = _place()
    h = g.shape[1] // 2
    return pltpu.make_async_remote_copy(
        src_ref=g.at[:, pl.ds((1 - c) * h, h), :], dst_ref=land, send_sem=send_sems.at[i], recv_sem=recv_sems.at[i],
        device_id=(x, y, 1 - c), device_id_type=MESH)


def _swap_start(name, grads):
    n_w = len(grads)
    lands = [lax.empty((g.shape[0], g.shape[1] // 2, g.shape[2]), g.dtype) for g in grads]

    def body(*refs):
        gs, ls = refs[:n_w], refs[n_w:2 * n_w]
        send_sems, recv_sems, token = refs[2 * n_w], refs[2 * n_w + 1], refs[-1]
        for i in range(n_w):
            _swap_copy(gs[i], ls[i], i, send_sems, recv_sems).start()
        token[...] = jnp.zeros_like(token)

    both = list(grads) + lands
    res = pl.pallas_call(
        body, name=name, in_specs=[HBM_SPEC] * (2 * n_w),
        out_shape=(pltpu.SemaphoreType.DMA((n_w,)), pltpu.SemaphoreType.DMA((n_w,)),
                   *[pltpu.HBM(b.shape, b.dtype) for b in both], jax.ShapeDtypeStruct((8, LANES), F32)),
        out_specs=(SEM_SPEC, SEM_SPEC, *[HBM_SPEC] * (2 * n_w), VMEM_SPEC),
        input_output_aliases={i: 2 + i for i in range(2 * n_w)},
        compiler_params=pltpu.CompilerParams(has_side_effects=SIDE_EFFECT),
    )(*_in_hbm(both))
    return (res[0], res[1], list(res[2:2 + n_w]), list(res[2 + n_w:2 + 2 * n_w])), res[-1]


def _swap_wait(name, swapped, after):
    send_sems, recv_sems, grads, lands = swapped
    n_w = len(grads)

    def body(*refs):
        gs, ls = refs[:n_w], refs[n_w:2 * n_w]
        s_sems, r_sems = refs[2 * n_w], refs[2 * n_w + 1]
        for i in range(n_w):
            cp = _swap_copy(gs[i], ls[i], i, s_sems, r_sems)
            cp.wait_send()
            cp.wait_recv()

    both = list(grads) + list(lands)
    res = pl.pallas_call(
        body, name=name, in_specs=[HBM_SPEC] * (2 * n_w) + [SEM_SPEC, SEM_SPEC, ANY],
        out_shape=[pltpu.HBM(b.shape, b.dtype) for b in both], out_specs=[HBM_SPEC] * (2 * n_w),
        input_output_aliases={i: i for i in range(2 * n_w)},
        compiler_params=pltpu.CompilerParams(has_side_effects=SIDE_EFFECT),
    )(*both, send_sems, recv_sems, after)
    return list(res[:n_w]), list(res[n_w:])


def _reduce_scatter_begin(tag, grads, idx, after=None, swapped=None):
    if swapped is None:
        from_sibling = _swap_other_half("swap_other_half_" + tag, grads, after)
    else:
        grads, from_sibling = _swap_wait("swap_wait_" + tag, swapped, after)
    pairs = [_add_my_half("add_my_half_%s_%d" % (tag, i), g, s, idx, BF)
             for i, (g, s) in enumerate(zip(grads, from_sibling))]
    s_sems, r_sems, parts, bufs, token = _scatter_start("scatter_start_" + tag, [p[0] for p in pairs],
                                                        [p[1] for p in pairs])
    return (tag, s_sems, r_sems, parts, bufs, idx), token


def _reduce_scatter_end(state, after):
    tag, s_sems, r_sems, parts, bufs, idx = state
    received = _scatter_wait("scatter_wait_" + tag, s_sems, r_sems, parts, bufs, after)
    halves = [_sum_chips("sum_chips_%s_%d" % (tag, i), q, idx) for i, q in enumerate(received)]
    return _join_halves("join_halves_" + tag, halves)


def _relu2_epilogue(acc):
    r = jnp.maximum(acc, 0.0)
    return (r * r,)


def _relu2_bwd_epilogue(acc, s):
    return (acc * (2.0 * jnp.sqrt(s.astype(F32))),)


WIDE = (2048, 512, 2048)
DEEP = (2048, 512, 2048)


def _add_epilogue(acc, other):
    return (acc + other,)


def kernel(x, c, positions, w_ada_mix, b_ada_mix, w_ada_mlp, b_ada_mlp, g_pre_mix, g_post_mix, g_pre_mlp, g_post_mlp, conv_w_in, conv_b_in, conv_dw, conv_dw_b, conv_ln_g, conv_ln_b, conv_w_out, conv_b_out, w_ada_kv, b_ada_kv, g_kv, w_dkv, g_ckv, w_kr, w_uk, w_uv, w_dq, g_cq, w_uq, w_o, mlp_w_up, mlp_w_down, loss_target, m_w_ada_mix, m_b_ada_mix, m_w_ada_mlp, m_b_ada_mlp, m_g_pre_mix, m_g_post_mix, m_g_pre_mlp, m_g_post_mlp, m_conv_w_in, m_conv_b_in, m_conv_dw, m_conv_dw_b, m_conv_ln_g, m_conv_ln_b, m_conv_w_out, m_conv_b_out, m_w_ada_kv, m_b_ada_kv, m_g_kv, m_w_dkv, m_g_ckv, m_w_kr, m_w_uk, m_w_uv, m_w_dq, m_g_cq, m_w_uq, m_w_o, m_mlp_w_up, m_mlp_w_down, v_w_ada_mix, v_b_ada_mix, v_w_ada_mlp, v_b_ada_mlp, v_g_pre_mix, v_g_post_mix, v_g_pre_mlp, v_g_post_mlp, v_conv_w_in, v_conv_b_in, v_conv_dw, v_conv_dw_b, v_conv_ln_g, v_conv_ln_b, v_conv_w_out, v_conv_b_out, v_w_ada_kv, v_b_ada_kv, v_g_kv, v_w_dkv, v_g_ckv, v_w_kr, v_w_uk, v_w_uv, v_w_dq, v_g_cq, v_w_uq, v_w_o, v_mlp_w_up, v_mlp_w_down):
    weights = dict(w_ada_mix=w_ada_mix, b_ada_mix=b_ada_mix, w_ada_mlp=w_ada_mlp, b_ada_mlp=b_ada_mlp, g_pre_mix=g_pre_mix, g_post_mix=g_post_mix, g_pre_mlp=g_pre_mlp, g_post_mlp=g_post_mlp, conv_w_in=conv_w_in, conv_b_in=conv_b_in, conv_dw=conv_dw, conv_dw_b=conv_dw_b, conv_ln_g=conv_ln_g, conv_ln_b=conv_ln_b, conv_w_out=conv_w_out, conv_b_out=conv_b_out, w_ada_kv=w_ada_kv, b_ada_kv=b_ada_kv, g_kv=g_kv, w_dkv=w_dkv, g_ckv=g_ckv, w_kr=w_kr, w_uk=w_uk, w_uv=w_uv, w_dq=w_dq, g_cq=g_cq, w_uq=w_uq, w_o=w_o, mlp_w_up=mlp_w_up, mlp_w_down=mlp_w_down)
    m_in = dict(w_ada_mix=m_w_ada_mix, b_ada_mix=m_b_ada_mix, w_ada_mlp=m_w_ada_mlp, b_ada_mlp=m_b_ada_mlp, g_pre_mix=m_g_pre_mix, g_post_mix=m_g_post_mix, g_pre_mlp=m_g_pre_mlp, g_post_mlp=m_g_post_mlp, conv_w_in=m_conv_w_in, conv_b_in=m_conv_b_in, conv_dw=m_conv_dw, conv_dw_b=m_conv_dw_b, conv_ln_g=m_conv_ln_g, conv_ln_b=m_conv_ln_b, conv_w_out=m_conv_w_out, conv_b_out=m_conv_b_out, w_ada_kv=m_w_ada_kv, b_ada_kv=m_b_ada_kv, g_kv=m_g_kv, w_dkv=m_w_dkv, g_ckv=m_g_ckv, w_kr=m_w_kr, w_uk=m_w_uk, w_uv=m_w_uv, w_dq=m_w_dq, g_cq=m_g_cq, w_uq=m_w_uq, w_o=m_w_o, mlp_w_up=m_mlp_w_up, mlp_w_down=m_mlp_w_down)
    v_in = dict(w_ada_mix=v_w_ada_mix, b_ada_mix=v_b_ada_mix, w_ada_mlp=v_w_ada_mlp, b_ada_mlp=v_b_ada_mlp, g_pre_mix=v_g_pre_mix, g_post_mix=v_g_post_mix, g_pre_mlp=v_g_pre_mlp, g_post_mlp=v_g_post_mlp, conv_w_in=v_conv_w_in, conv_b_in=v_conv_b_in, conv_dw=v_conv_dw, conv_dw_b=v_conv_dw_b, conv_ln_g=v_conv_ln_g, conv_ln_b=v_conv_ln_b, conv_w_out=v_conv_w_out, conv_b_out=v_conv_b_out, w_ada_kv=v_w_ada_kv, b_ada_kv=v_b_ada_kv, g_kv=v_g_kv, w_dkv=v_w_dkv, g_ckv=v_g_ckv, w_kr=v_w_kr, w_uk=v_w_uk, w_uv=v_w_uv, w_dq=v_w_dq, g_cq=v_g_cq, w_uq=v_w_uq, w_o=v_w_o, mlp_w_up=v_mlp_w_up, mlp_w_down=v_mlp_w_down)
    layer1 = {}

    def early_updates(red):
        for n, key in (("mlp_w_up", "up1"), ("mlp_w_down", "down1")):
            layer1[n] = _adamw_layer(n, weights[n], red[key], m_in[n], v_in[n], 1)
        return [layer1[n][1] for n in layer1]

    grads, last_group = _step_grads(x, c, positions, loss_target, weights, early_updates)
    loss, grad_x = grads.pop("loss"), grads.pop("x")
    names = list(weights)
    upd = {}
    for n in names:
        if isinstance(grads.get(n), list):
            res = layer1.get(n) or _adamw_layer(n, weights[n], grads[n][1], m_in[n], v_in[n], 1)
            grads[n], *upd[n] = _adamw_layer(n, weights[n], grads[n][0], m_in[n], v_in[n], 0, prev=res)
        elif n in grads:
            upd[n] = _adamw(n, weights[n], grads[n], m_in[n], v_in[n])
    grads.update(last_group([upd[n][0] for n in ("w_ada_mix", "w_ada_mlp", "w_ada_kv", "mlp_w_down")]))
    upd.update({n: _adamw(n, weights[n], grads[n], m_in[n], v_in[n]) for n in names if n not in upd})
    return (loss, grad_x, *[grads[n] for n in names], *[upd[n][0] for n in names], *[upd[n][1] for n in names],
            *[upd[n][2] for n in names])


def _step_grads(x, c, positions, loss_target, w, early_updates):
    xi, yi, ci = lax.axis_index("x"), lax.axis_index("y"), lax.axis_index("c")
    chip = 2 * xi + yi
    dev = 2 * chip + ci
    place_idx = jnp.stack([ci, chip]).astype(jnp.int32)
    t, d = x.shape[1], x.shape[2]
    dl = d // N_CHIPS
    r_kv, r_q = w["w_dkv"].shape[1], w["w_dq"].shape[2]
    n_ada, n_kvada = w["w_ada_mix"].shape[2], w["w_ada_kv"].shape[1]
    heads_l = w["w_uq"].shape[2] // (QK_NOPE + QK_ROPE)
    assert dl == r_kv == r_q and dl % LANES == 0 and heads_l * N_CHIPS % 2 == 0

    def chip_cols(a, width):
        return lax.dynamic_slice_in_dim(a, chip * width, width, axis=a.ndim - 1)

    uq = w["w_uq"][0].reshape(r_q, heads_l, QK_NOPE + QK_ROPE)
    uq_nope, uq_rope = uq[:, :, :QK_NOPE].reshape(r_q, -1), uq[:, :, QK_NOPE:].reshape(r_q, -1)
    kr_pad = jnp.pad(w["w_kr"], ((0, 0), (0, LANES - QK_ROPE)))
    small = jnp.concatenate([w["w_dkv"], kr_pad, w["w_dq"][0], w["w_uk"], w["w_uv"], uq_nope, uq_rope], axis=1)
    widths = [r_kv, LANES, r_q, w["w_uk"].shape[1], w["w_uv"].shape[1], uq_nope.shape[1], uq_rope.shape[1]]
    so = [0]
    for wd in widths:
        so.append(so[-1] + wd)
    big = dict(conv_w_in=(w["conv_w_in"], 0), conv_w_out=(w["conv_w_out"], 0), w_o=(w["w_o"], 0),
               up0=(w["mlp_w_up"], 0), up1=(w["mlp_w_up"], 1), down0=(w["mlp_w_down"], 0),
               down1=(w["mlp_w_down"], 1), small=(small[None], 0))
    groups = dict(conv_in=["conv_w_in"], conv_out=["conv_w_out"], up0=["up0"], down0=["down0"],
                  mla=["small", "w_o"], up1=["up1"], down1=["down1"])
    gathered, pending = {}, {}

    def start_gathers(gnames, after):
        token = after
        for gname in gnames:
            slots = [_cast_into_slot(n, *big[n], place_idx, after=token) for n in groups[gname]]
            s_sems, r_sems, bufs, token = _gather_start("gather_start_" + gname, slots, after=token)
            pending[gname] = (s_sems, r_sems, bufs)
        return token

    def finish_gather(gname, after):
        s_sems, r_sems, bufs = pending[gname]
        bufs = _gather_wait("gather_wait_" + gname, s_sems, r_sems, bufs, after)
        gathered.update(zip(groups[gname], _forward_to_sibling("forward_" + gname, bufs)))

    rowl = lambda a: a.reshape(-1, a.shape[2])
    w_up_f = lambda l: gathered["up%d" % l]
    w_down_f = lambda l: rowl(gathered["down%d" % l])

    pack_rows = [w["conv_dw"][0], w["conv_dw_b"], w["conv_ln_g"], w["conv_ln_b"], w["conv_b_out"],
                 w["conv_b_in"].reshape(2, dl), c.reshape(N_CHIPS, dl)]
    pack_rows = [jnp.pad(p, ((0, (-p.shape[0]) % 8), (0, 0))) for p in pack_rows]
    po = [0]
    for p in pack_rows:
        po.append(po[-1] + p.shape[0])
    packs = _all_gather_small("gather_params", jnp.concatenate(pack_rows, axis=0))
    by_chip = packs[0::2]

    def full_width(r0, nr):
        return jnp.transpose(by_chip[:, r0:r0 + nr, :], (1, 0, 2)).reshape(nr, d)

    dw_full, dwb_full = full_width(po[0], CONV_WIDTH), full_width(po[1], 1)
    lng_full, lnb_full, bout_full = full_width(po[2], 1), full_width(po[3], 1), full_width(po[4], 1)
    bin_full = by_chip[:, po[5]:po[5] + 2, :].reshape(1, 2 * d)
    first_start = start_gathers(("conv_in",), packs)[0:1, 0:1]
    c_all = packs[:, po[6]:po[6] + N_CHIPS, :].reshape(N_DEV, d) + first_start

    sc_all = _ew("silu_c", lambda a: (a * jax.nn.sigmoid(a),), [c_all], (F32,))[0]
    ada_w = [(w["w_ada_mix"], 0), (w["w_ada_mix"], 1), (w["w_ada_mlp"], 0), (w["w_ada_mlp"], 1),
             (w["w_ada_kv"][None], 0)]
    ada_b = [chip_cols(w["b_ada_mix"][0:1], n_ada), chip_cols(w["b_ada_mix"][1:2], n_ada),
             chip_cols(w["b_ada_mlp"][0:1], n_ada), chip_cols(w["b_ada_mlp"][1:2], n_ada),
             chip_cols(w["b_ada_kv"].reshape(1, -1), n_kvada)]
    mods = [_mm_nn("ada_fwd_%d" % i, sc_all, wi, "row", extras=(jnp.broadcast_to(bi, (N_DEV, bi.shape[1])),),
                   epilogue=_add_epilogue, w_layer=li) for i, ((wi, li), bi) in enumerate(zip(ada_w, ada_b))]
    mods_all = _all_gather_small("gather_mods", jnp.concatenate(mods, axis=1))
    started = start_gathers(("conv_out", "up0", "down0", "mla", "up1", "down1"),
                            mods_all[0, 0:1, 0:1] + first_start)[0:1, 0:1]
    mine = lax.dynamic_index_in_dim(mods_all[0::2], dev, axis=1, keepdims=False)
    offs = [0]
    for m_ in mods:
        offs.append(offs[-1] + m_.shape[1])
    mod_vec = [mine[:, offs[i]:offs[i + 1]].reshape(1, -1) for i in range(5)]
    split3 = lambda v: (v[:, :d], v[:, d:2 * d], v[:, 2 * d:])
    mix = [split3(mod_vec[0]), split3(mod_vec[1])]
    mlp = [split3(mod_vec[2]), split3(mod_vec[3])]
    kv_shift, kv_scale = mod_vec[4][:, :d], mod_vec[4][:, d:]

    cos_t, sin_t = _rope_tables(positions, t)
    vec = lambda a: a.reshape(1, -1)
    gpm, gqm = w["g_pre_mix"], w["g_post_mix"]
    gpl, gql = w["g_pre_mlp"], w["g_post_mlp"]
    h0 = x[0]
    after_token = lambda v, token: v + token[0:1, 0:1]

    def mlp_fwd(l, h):
        sh, sc, _ = mlp[l]
        (hn,) = _rw_fwd("mlp_pre_%d" % l, _f_pre, [h], [vec(gpl[l]), sh, sc], (BF,))
        finish_gather("up%d" % l, hn)
        s = _mm_nn("mlp_up_%d" % l, hn, w_up_f(l), "col", o_dtypes=(BF,), epilogue=_relu2_epilogue, tiles=WIDE)
        finish_gather("down%d" % l, s)
        y = _mm_nn("mlp_down_%d" % l, s, w_down_f(l), "row", tiles=DEEP)
        return hn, s, y

    def post_fwd(name, h, y, gate, g, bias=None):
        if bias is None:
            return _rw_fwd(name, lambda h_, y_, gt, g_: (h_ + _f_post(y_, gt, g_)[0],), [h, y], [gate, g], (F32,))[0]
        return _rw_fwd(name, lambda h_, y_, b_, gt, g_: (h_ + _f_post_bias(y_, b_, gt, g_)[0],), [h, y],
                       [bias, gate, g], (F32,))[0]

    (hn0,) = _rw_fwd("conv_pre", _f_pre, [h0], [vec(gpm[0]), mix[0][0] + started, mix[0][1]], (BF,))
    finish_gather("conv_in", hn0)
    w_in_f = gathered["conv_w_in"]
    z0 = _mm_nn("conv_in", hn0, w_in_f, "col")
    (u0,) = _rw_fwd("conv_glu", _f_glu, [z0], [bin_full], (F32,))
    uc0 = _conv_fwd(u0, dw_full, dwb_full)
    (n0,) = _rw_fwd("conv_ln", _f_ln_silu, [uc0], [lng_full, lnb_full], (BF,))
    finish_gather("conv_out", n0)
    w_out_f = rowl(gathered["conv_w_out"])
    y0 = _mm_nn("conv_out", n0, w_out_f, "row")
    h1 = post_fwd("conv_post", h0, y0, mix[0][2], vec(gqm[0]), bias=bout_full)
    hn1, s1, y1 = mlp_fwd(0, h1)
    h2 = post_fwd("mlp_post_0", h1, y1, mlp[0][2], vec(gql[0]))

    finish_gather("mla", h2)
    gs = gathered["small"]
    w_dkvkr = rowl(gs[:, :, so[0]:so[2]])
    w_dq_f = rowl(gs[:, :, so[2]:so[3]])
    w_uk_f, w_uv_f = gs[:, :, so[3]:so[4]], gs[:, :, so[4]:so[5]]
    w_qn_f, w_qr_f = gs[:, :, so[5]:so[6]], gs[:, :, so[6]:so[7]]
    w_o_f = rowl(gathered["w_o"])
    kvn, hn2 = _rw_fwd("mla_pre", _f_pre2, [h2], [vec(w["g_kv"]), kv_shift, kv_scale, vec(gpm[1]), mix[1][0], mix[1][1]],
                       (BF, BF))
    pre_kv = _mm_nn("mla_dkv", kvn, w_dkvkr, "row")
    pre_q = _mm_nn("mla_dq", hn2, w_dq_f, "row")
    f_lat = _make_f_lat(r_kv)
    c_kv, kr, c_q = _rw_fwd("mla_latent", f_lat, [pre_kv, pre_q, cos_t, sin_t], [vec(w["g_ckv"]), vec(w["g_cq"][0])],
                            (BF, BF, BF))
    kn = _mm_nn("mla_uk", c_kv, w_uk_f, "col", o_dtypes=(BF,))
    vv = _mm_nn("mla_uv", c_kv, w_uv_f, "col", o_dtypes=(BF,))
    qn = _mm_nn("mla_uq_nope", c_q, w_qn_f, "col", o_dtypes=(BF,))
    qr_pre = _mm_nn("mla_uq_rope", c_q, w_qr_f, "col")
    (qr,) = _rw_fwd("mla_q_rope", _f_rope, [qr_pre, cos_t, sin_t], [], (BF,))
    att = _attn_fwd(qn, qr, kn, kr, vv)
    y2 = _mm_nn("mla_o", att, w_o_f, "row")
    h3 = post_fwd("mla_post", h2, y2, mix[1][2], vec(gqm[1]))
    hn3, s3, y3 = mlp_fwd(1, h3)
    h4 = post_fwd("mlp_post_1", h3, y3, mlp[1][2], vec(gql[1]))

    dh4, loss_part = _loss_grad(h4, loss_target[0])
    loss = lax.psum(loss_part[0, 0], ("x", "y", "c"))

    gw = {}
    gvec = {}

    dw_mm = functools.partial(_mm_tn, o_dtype=BF)

    def mlp_bwd(l, h_in, hn, s, y, dh, token=None):
        sh, sc, gate = mlp[l]
        if token is not None:
            gate = after_token(gate, token)
        (dy,), (dgate, dgq) = _rw_bwd("mlp_post_bwd_%d" % l, _f_post, [y], [gate, vec(gql[l])], [dh], [BF])
        gw["down%d" % l] = dw_mm("mlp_down_dw_%d" % l, s, dy, "row", tiles=(512, 2048, 2048))
        da = _mm_nt("mlp_down_dx_%d" % l, dy, w_down_f(l), "row", o_dtypes=(BF,), extras=(s,),
                    epilogue=_relu2_bwd_epilogue, tiles=WIDE)
        gw["up%d" % l] = dw_mm("mlp_up_dw_%d" % l, hn, da, "col", tiles=WIDE)
        dhn = _mm_nt("mlp_up_dx_%d" % l, da, w_up_f(l), "col", tiles=DEEP)
        (dh_in,), (dgp, dsh, dsc) = _rw_bwd("mlp_pre_bwd_%d" % l, _f_pre, [h_in], [vec(gpl[l]), sh, sc], [dhn], [F32],
                                            add_rows={0: dh})
        gvec["mlp%d" % l] = (dsh, dsc, dgate)
        gvec["g_pre_mlp%d" % l], gvec["g_post_mlp%d" % l] = dgp, dgq
        return dh_in

    chunked = lambda a: a.reshape(N_CHIPS, a.shape[0] // N_CHIPS, a.shape[1])
    to_chunks = lambda a: a if a.ndim == 3 else chunked(a)
    red = {}

    def swap_begin(tag, names):
        return _swap_start("swap_start_" + tag, [to_chunks(gw[n]) for n in names])

    def reduce_begin(tag, names, after=None, swapped=None):
        state, token = _reduce_scatter_begin(tag, [to_chunks(gw[n]) for n in names], place_idx, after, swapped)
        return (names, state), token

    def reduce_end(handle, after):
        names, state = handle
        red.update(zip(names, _reduce_scatter_end(state, after)))

    dh3 = mlp_bwd(1, h3, hn3, s3, y3, dh4)
    sw1, token = swap_begin("r1", ["up1", "down1"])

    (dy2,), (dgate, dgq) = _rw_bwd("mla_post_bwd", _f_post, [y2], [after_token(mix[1][2], token), vec(gqm[1])],
                                   [dh3], [BF])
    gvec["g_post_mix1"] = dgq
    gw["w_o"] = dw_mm("mla_o_dw", att, dy2, "row")
    datt = _mm_nt("mla_o_dx", dy2, w_o_f, "row", o_dtypes=(BF,))
    rs1, token = reduce_begin("r1", ["up1", "down1"], after=datt, swapped=sw1)
    dqn, dqr, dkn, dvv, dkr = _attn_bwd(qn, qr, kn, kr + token[0:1, 0:1].astype(BF), vv, datt)
    (dqr_pre,), _ = _rw_bwd("mla_q_rope_bwd", _f_rope, [qr_pre, cos_t, sin_t], [], [dqr], [BF, None, None])
    g_qn = dw_mm("mla_uq_nope_dw", c_q, dqn, "col")
    g_qr = dw_mm("mla_uq_rope_dw", c_q, dqr_pre, "col")
    dc_q = _mm_nt("mla_uq_nope_dx", dqn, w_qn_f, "col")
    dc_q = _mm_nt("mla_uq_rope_dx", dqr_pre, w_qr_f, "col", extras=(dc_q,), epilogue=_add_epilogue)
    g_uk = dw_mm("mla_uk_dw", c_kv, dkn, "col")
    g_uv = dw_mm("mla_uv_dw", c_kv, dvv, "col")
    dc_kv = _mm_nt("mla_uk_dx", dkn, w_uk_f, "col")
    dc_kv = _mm_nt("mla_uv_dx", dvv, w_uv_f, "col", extras=(dc_kv,), epilogue=_add_epilogue)
    (dpre_kv, dpre_q), (dg_ckv, dg_cq) = _rw_bwd(
        "mla_latent_bwd", f_lat, [pre_kv, pre_q, cos_t, sin_t], [vec(w["g_ckv"]), vec(w["g_cq"][0])],
        [dc_kv, dkr, dc_q], [BF, BF, None, None])
    gvec["g_ckv"], gvec["g_cq"] = dg_ckv, dg_cq
    g_dkvkr = dw_mm("mla_dkv_dw", kvn, dpre_kv, "row")
    g_dq = dw_mm("mla_dq_dw", hn2, dpre_q, "row")
    dkvn = _mm_nt("mla_dkv_dx", dpre_kv, w_dkvkr, "row")
    dhn2 = _mm_nt("mla_dq_dx", dpre_q, w_dq_f, "row")
    (dh2,), (dg_kv, dkvsh, dkvsc, dgp, dsh, dsc) = _rw_bwd(
        "mla_pre_bwd", _f_pre2, [h2], [vec(w["g_kv"]), kv_shift, kv_scale, vec(gpm[1]), mix[1][0], mix[1][1]],
        [dkvn, dhn2], [F32], add_rows={0: dh3})
    gvec["mix1"] = (dsh, dsc, dgate)
    gvec["kv"] = (dkvsh, dkvsc)
    gvec["g_kv"], gvec["g_pre_mix1"] = dg_kv, dgp
    gw["small"] = jnp.concatenate([chunked(g_dkvkr), chunked(g_dq), g_uk, g_uv, g_qn, g_qr], axis=2)
    reduce_end(rs1, dh2)
    rs2, token = reduce_begin("r2", ["small", "w_o"])

    dh1 = mlp_bwd(0, h1, hn1, s1, y1, dh2, token=token)
    reduce_end(rs2, dh1)
    rs3, token = reduce_begin("r3", ["up0", "down0"])

    (dy0,), (dbout, dgate, dgq) = _rw_bwd("conv_post_bwd", _f_post_bias, [y0],
                                          [bout_full, after_token(mix[0][2], token), vec(gqm[0])], [dh1], [BF])
    gvec["g_post_mix0"] = dgq
    gw["conv_w_out"] = dw_mm("conv_out_dw", n0, dy0, "row")
    dn0 = _mm_nt("conv_out_dx", dy0, w_out_f, "row")
    (duc0,), (dlng, dlnb) = _rw_bwd("conv_ln_bwd", _f_ln_silu, [uc0], [lng_full, lnb_full], [dn0], [F32])
    du0, ddw = _conv_bwd(u0, duc0, dw_full)
    (dz0,), (dbin,) = _rw_bwd("conv_glu_bwd", _f_glu, [z0], [bin_full], [du0], [BF])
    gw["conv_w_in"] = dw_mm("conv_in_dw", hn0, dz0, "col")
    dhn0 = _mm_nt("conv_in_dx", dz0, w_in_f, "col")
    (dx,), (dgp, dsh, dsc) = _rw_bwd("conv_pre_bwd", _f_pre, [h0], [vec(gpm[0]), mix[0][0], mix[0][1]], [dhn0], [F32],
                                     add_rows={0: dh1})
    gvec["mix0"] = (dsh, dsc, dgate)
    gvec["g_pre_mix0"] = dgp
    reduce_end(rs3, [dx, gw["conv_w_in"], gw["conv_w_out"]] + early_updates(red))

    vec_list = [*gvec["mix0"], *gvec["mix1"], *gvec["mlp0"], *gvec["mlp1"], *gvec["kv"],
                gvec["g_pre_mix0"], gvec["g_pre_mix1"], gvec["g_post_mix0"], gvec["g_post_mix1"],
                gvec["g_pre_mlp0"], gvec["g_pre_mlp1"], gvec["g_post_mlp0"], gvec["g_post_mlp1"],
                gvec["g_kv"], gvec["g_ckv"], gvec["g_cq"], dbin, dlng, dlnb, dbout, ddw.reshape(1, -1)]
    vo = [0]
    for v_ in vec_list:
        vo.append(vo[-1] + v_.shape[1])
    vec_pad = (-vo[-1]) % (8 * LANES)
    n_vec = vo[-1] + vec_pad
    flat = jnp.concatenate(vec_list + [jnp.zeros((1, vec_pad), F32)], axis=1).reshape(8, n_vec // 8)
    all_vecs = _all_gather_small("gather_vector_grads", flat).reshape(N_DEV, 8, n_vec // 8)
    rs4, token = reduce_begin("r4", ["conv_w_in", "conv_w_out"], after=all_vecs)
    all_vecs = after_token(all_vecs, token)
    summed = _sum_devices(all_vecs).reshape(1, n_vec)
    per_dev = all_vecs.reshape(N_DEV, n_vec)
    seg = lambda a, i: a[:, vo[i]:vo[i + 1]]

    out = {"loss": loss, "x": dx.reshape(x.shape)}
    dm_mix = [jnp.concatenate([seg(per_dev, 3 * l + i) for i in range(3)], axis=1) for l in range(2)]
    dm_mlp = [jnp.concatenate([seg(per_dev, 6 + 3 * l + i) for i in range(3)], axis=1) for l in range(2)]
    dm_kv = jnp.concatenate([seg(per_dev, 12), seg(per_dev, 13)], axis=1)
    ada_dw = lambda name, dm, width: _mm_tn(name, sc_all, chip_cols(dm, width), "row")
    out["w_ada_mix"] = [ada_dw("ada_mix_dw_%d" % l, dm_mix[l], n_ada) for l in range(2)]
    out["w_ada_mlp"] = [ada_dw("ada_mlp_dw_%d" % l, dm_mlp[l], n_ada) for l in range(2)]
    out["w_ada_kv"] = ada_dw("ada_kv_dw", dm_kv, n_kvada)
    sum_seg = lambda i: seg(summed, i)
    out["b_ada_mix"] = jnp.concatenate([jnp.concatenate([sum_seg(3 * l + i) for i in range(3)], axis=1) for l in range(2)], axis=0)
    out["b_ada_mlp"] = jnp.concatenate([jnp.concatenate([sum_seg(6 + 3 * l + i) for i in range(3)], axis=1) for l in range(2)], axis=0)
    out["b_ada_kv"] = jnp.concatenate([sum_seg(12), sum_seg(13)], axis=1).reshape(-1)
    out["g_pre_mix"] = jnp.concatenate([sum_seg(14), sum_seg(15)], axis=0)
    out["g_post_mix"] = jnp.concatenate([sum_seg(16), sum_seg(17)], axis=0)
    out["g_pre_mlp"] = jnp.concatenate([sum_seg(18), sum_seg(19)], axis=0)
    out["g_post_mlp"] = jnp.concatenate([sum_seg(20), sum_seg(21)], axis=0)
    out["g_kv"] = sum_seg(22).reshape(-1)
    out["g_ckv"] = sum_seg(23).reshape(-1)
    out["g_cq"] = sum_seg(24)
    out["conv_b_in"] = chip_cols(sum_seg(25), 2 * dl)
    out["conv_ln_g"] = chip_cols(sum_seg(26), dl)
    out["conv_ln_b"] = chip_cols(sum_seg(27), dl)
    out["conv_b_out"] = chip_cols(sum_seg(28), dl)
    ddw_sum = chip_cols(sum_seg(29).reshape(CONV_WIDTH + 1, d), dl)
    out["conv_dw"] = ddw_sum[:CONV_WIDTH].reshape(1, CONV_WIDTH, dl)
    out["conv_dw_b"] = ddw_sum[CONV_WIDTH:]

    def last_group(after):
        reduce_end(rs4, after)
        return {"conv_w_in": red["conv_w_in"][None], "conv_w_out": red["conv_w_out"][None]}

    out["w_o"] = red["w_o"][None]
    out["mlp_w_up"] = [red["up0"], red["up1"]]
    out["mlp_w_down"] = [red["down0"], red["down1"]]
    rs = red["small"]
    piece = lambda i: rs[:, so[i]:so[i + 1]]
    out["w_dkv"] = piece(0)
    out["w_kr"] = piece(1)[:, :QK_ROPE]
    out["w_dq"] = piece(2)[None]
    out["w_uk"], out["w_uv"] = piece(3), piece(4)
    out["w_uq"] = jnp.concatenate([piece(5).reshape(r_q, heads_l, QK_NOPE), piece(6).reshape(r_q, heads_l, QK_ROPE)],
                                  axis=2).reshape(1, r_q, -1)
    return out, last_group
```

```python
import functools

import jax
import jax.numpy as jnp
from jax import lax
from jax.experimental import pallas as pl
from jax.experimental.pallas import tpu as pltpu

F32 = jnp.float32
BF = jnp.bfloat16
MXU_DTYPE = BF

EPS = 1e-6
NEG = -1e30
ROPE_THETA = 10000.0
QK_NOPE = 128
QK_ROPE = 64
V_HEAD = 128
CONV_WIDTH = 31
ADAM_LR, ADAM_B1, ADAM_B2, ADAM_EPS, ADAM_WD, ADAM_STEP = 0.001, 0.9, 0.999, 1e-08, 0.01, 10

N_CHIPS = 4
N_DEV = 8
LANES = 128
VMEM_LIMIT = 56 * 2 ** 20
MESH = pl.DeviceIdType.MESH
ANY = pl.BlockSpec(memory_space=pl.ANY)


def _params(sem=None):
    return pltpu.CompilerParams(dimension_semantics=sem, vmem_limit_bytes=VMEM_LIMIT)


def _tile(n, pref, unit=LANES):
    if n <= pref:
        return n
    t = (pref // unit) * unit
    while t > unit and n % t:
        t -= unit
    assert n % t == 0, (n, pref)
    return t


def _dg(a, b, ca, cb):
    return lax.dot_general(a.astype(MXU_DTYPE), b.astype(MXU_DTYPE), (((ca,), (cb,)), ((), ())),
                           preferred_element_type=F32)


@jax.custom_vjp
def _dot_nn(a, b):
    return _dg(a, b, 1, 0)


def _dot_nn_fwd(a, b):
    return _dg(a, b, 1, 0), (a, b)


def _dot_nn_bwd(res, g):
    a, b = res
    return _dg(g, b, 1, 1).astype(a.dtype), _dg(a, g, 0, 0).astype(b.dtype)


_dot_nn.defvjp(_dot_nn_fwd, _dot_nn_bwd)


@jax.custom_vjp
def _dot_nt(a, b):
    return _dg(a, b, 1, 1)


def _dot_nt_fwd(a, b):
    return _dg(a, b, 1, 1), (a, b)


def _dot_nt_bwd(res, g):
    a, b = res
    return _dg(g, b, 1, 0).astype(a.dtype), _dg(g, a, 0, 0).astype(b.dtype)


_dot_nt.defvjp(_dot_nt_fwd, _dot_nt_bwd)


def _matmul(name, a, b, *, ca, cb, grid, a_blk, a_map, b_blk, b_map, o_shape, o_dtypes, o_blk, o_map,
            extras=(), epilogue=None):
    nk = grid[-1]
    n_ex, n_out = len(extras), len(o_dtypes)

    def body(*refs):
        a_ref, b_ref = refs[0], refs[1]
        ex_refs = refs[2:2 + n_ex]
        out_refs = refs[2 + n_ex:2 + n_ex + n_out]
        kk = pl.program_id(len(grid) - 1)

        def finish(acc):
            outs = (acc,) if epilogue is None else epilogue(acc, *[r[...] for r in ex_refs])
            for r, o in zip(out_refs, outs):
                r[...] = o.astype(r.dtype)

        part = _dg(a_ref[...], b_ref[...], ca, cb)
        if nk == 1:
            finish(part)
        else:
            acc_ref = refs[-1]

            @pl.when(kk == 0)
            def _():
                acc_ref[...] = part

            @pl.when(kk > 0)
            def _():
                acc_ref[...] += part

            @pl.when(kk == nk - 1)
            def _():
                finish(acc_ref[...])

    o_spec = pl.BlockSpec(o_blk, o_map)
    acc_shape = tuple(d for d in o_blk if d is not None)
    res = pl.pallas_call(
        body, grid=grid,
        in_specs=[pl.BlockSpec(a_blk, a_map), pl.BlockSpec(b_blk, b_map)] + [o_spec] * n_ex,
        out_specs=[o_spec] * n_out,
        out_shape=[jax.ShapeDtypeStruct(o_shape, dt) for dt in o_dtypes],
        scratch_shapes=[] if nk == 1 else [pltpu.VMEM(acc_shape, F32)],
        compiler_params=_params(("parallel",) * (len(grid) - 1) + ("arbitrary",)),
        name=name,
    )(a, b, *extras)
    return res[0] if n_out == 1 else res


TM, TN, TK = 2048, 512, 2048


def _mm_nn(name, a, w, layout, o_dtypes=(F32,), extras=(), epilogue=None, tiles=(TM, TN, TK), w_layer=None):
    m, k = a.shape
    tm, tk = _tile(m, tiles[0], 8), _tile(k, tiles[2])
    if layout == "row":
        n = w.shape[-1]
        tn = _tile(n, tiles[1])
        grid = (m // tm, n // tn, k // tk)
        b_blk, b_map = (tk, tn), (lambda i, j, kk: (kk, j))
        if w_layer is not None:
            b_blk, b_map = (None, tk, tn), (lambda i, j, kk: (w_layer, kk, j))
    else:
        nl = w.shape[2]
        n = nl * w.shape[0]
        tn = _tile(nl, tiles[1])
        nb = nl // tn
        grid = (m // tm, n // tn, k // tk)
        b_blk, b_map = (None, tk, tn), (lambda i, j, kk: (j // nb, kk, j % nb))
    return _matmul(name, a, w, ca=1, cb=0, grid=grid, a_blk=(tm, tk), a_map=lambda i, j, kk: (i, kk),
                   b_blk=b_blk, b_map=b_map, o_shape=(m, n), o_dtypes=o_dtypes, o_blk=(tm, tn),
                   o_map=lambda i, j, kk: (i, j), extras=extras, epilogue=epilogue)


def _mm_nt(name, g, w, layout, o_dtypes=(F32,), extras=(), epilogue=None, tiles=(TM, TN, TK)):
    m, n = g.shape
    tm, tn = _tile(m, tiles[0], 8), None
    if layout == "row":
        k = w.shape[0]
        tn = _tile(n, tiles[2])
        tk = _tile(k, tiles[1])
        grid = (m // tm, k // tk, n // tn)
        b_blk, b_map = (tk, tn), (lambda i, j, kk: (j, kk))
    else:
        k, nl = w.shape[1], w.shape[2]
        tn = _tile(nl, tiles[2])
        nb = nl // tn
        tk = _tile(k, tiles[1])
        grid = (m // tm, k // tk, n // tn)
        b_blk, b_map = (None, tk, tn), (lambda i, j, kk: (kk // nb, j, kk % nb))
    return _matmul(name, g, w, ca=1, cb=1, grid=grid, a_blk=(tm, tn), a_map=lambda i, j, kk: (i, kk),
                   b_blk=b_blk, b_map=b_map, o_shape=(m, k), o_dtypes=o_dtypes, o_blk=(tm, tk),
                   o_map=lambda i, j, kk: (i, j), extras=extras, epilogue=epilogue)


def _mm_tn(name, x, g, layout, o_dtype=F32, tiles=(TM, TN, TK)):
    t, k = x.shape
    n = g.shape[1]
    tt = _tile(t, tiles[2], 8)
    tk = _tile(k, tiles[0])
    if layout == "row":
        tn = _tile(n, tiles[1])
        o_shape, o_blk, o_map = (k, n), (tk, tn), (lambda i, j, kk: (i, j))
    else:
        nl = n // N_CHIPS
        tn = _tile(nl, tiles[1])
        nb = nl // tn
        o_shape, o_blk, o_map = (N_CHIPS, k, nl), (None, tk, tn), (lambda i, j, kk: (j // nb, i, j % nb))
    grid = (k // tk, n // tn, t // tt)
    return _matmul(name, x, g, ca=0, cb=0, grid=grid, a_blk=(tt, tk), a_map=lambda i, j, kk: (kk, i),
                   b_blk=(tt, tn), b_map=lambda i, j, kk: (kk, j), o_shape=o_shape, o_dtypes=(o_dtype,),
                   o_blk=o_blk, o_map=o_map)


TR = 256


def _rw_fwd(name, fn, rows, vecs, o_dtypes, tr=TR):
    t = rows[0].shape[0]
    tr = min(tr, t)
    n_r, n_v = len(rows), len(vecs)
    o_sds = jax.eval_shape(fn, *[jax.ShapeDtypeStruct((tr, r.shape[1]), r.dtype) for r in rows],
                           *[jax.ShapeDtypeStruct(v.shape, v.dtype) for v in vecs])

    def body(*refs):
        outs = fn(*[r[...] for r in refs[:n_r + n_v]])
        for r, o in zip(refs[n_r + n_v:], outs):
            r[...] = o.astype(r.dtype)

    res = pl.pallas_call(
        body, grid=(t // tr,),
        in_specs=[pl.BlockSpec((tr, r.shape[1]), lambda i: (i, 0)) for r in rows]
        + [pl.BlockSpec(v.shape, lambda i: (0, 0)) for v in vecs],
        out_specs=[pl.BlockSpec((tr, o.shape[1]), lambda i: (i, 0)) for o in o_sds],
        out_shape=[jax.ShapeDtypeStruct((t, o.shape[1]), dt) for o, dt in zip(o_sds, o_dtypes)],
        compiler_params=_params(("parallel",)), name=name,
    )(*rows, *vecs)
    return res


def _rw_bwd(name, fn, rows, vecs, cots, row_grad_dtypes, add_rows=None, tr=TR):
    t = rows[0].shape[0]
    tr = min(tr, t)
    add_rows = add_rows or {}
    n_r, n_v, n_c = len(rows), len(vecs), len(cots)
    diff = [i for i, dt in enumerate(row_grad_dtypes) if dt is not None]
    adds = [add_rows[i] for i in diff if i in add_rows]
    n_a = len(adds)

    def body(*refs):
        rvals = [r[...] for r in refs[:n_r]]
        vvals = [r[...] for r in refs[n_r:n_r + n_v]]
        c_refs = refs[n_r + n_v:n_r + n_v + n_c]
        a_refs = list(refs[n_r + n_v + n_c:n_r + n_v + n_c + n_a])
        o_refs = refs[n_r + n_v + n_c + n_a:]

        def f(*d):
            full = list(rvals)
            for i, x in zip(diff, d[:len(diff)]):
                full[i] = x
            return fn(*full, *d[len(diff):])

        outs, vjp = jax.vjp(f, *[rvals[i] for i in diff], *vvals)
        grads = vjp(tuple(c[...].astype(o.dtype) for c, o in zip(c_refs, outs)))
        for n, i in enumerate(diff):
            gr = grads[n].astype(F32)
            if i in add_rows:
                gr = gr + a_refs.pop(0)[...].astype(F32)
            o_refs[n][...] = gr.astype(o_refs[n].dtype)
        first = pl.program_id(0) == 0
        for r, gv in zip(o_refs[len(diff):], grads[len(diff):]):
            @pl.when(first)
            def _(r=r, gv=gv):
                r[...] = gv

            @pl.when(jnp.logical_not(first))
            def _(r=r, gv=gv):
                r[...] += gv

    row_spec = lambda a: pl.BlockSpec((tr, a.shape[1]), lambda i: (i, 0))
    vec_spec = lambda a: pl.BlockSpec(a.shape, lambda i: (0, 0))
    res = pl.pallas_call(
        body, grid=(t // tr,),
        in_specs=[row_spec(r) for r in rows] + [vec_spec(v) for v in vecs] + [row_spec(c) for c in cots]
        + [row_spec(a) for a in adds],
        out_specs=[row_spec(rows[i]) for i in diff] + [vec_spec(v) for v in vecs],
        out_shape=[jax.ShapeDtypeStruct(rows[i].shape, row_grad_dtypes[i]) for i in diff]
        + [jax.ShapeDtypeStruct(v.shape, F32) for v in vecs],
        compiler_params=_params(("arbitrary",)), name=name,
    )(*rows, *vecs, *cots, *adds)
    return res[:len(diff)], res[len(diff):]


def _rms(x, g):
    return x * lax.rsqrt(jnp.mean(x * x, axis=-1, keepdims=True) + EPS) * g


def _f_pre(h, g, shift, scale):
    return (_rms(h, g) * (1.0 + scale) + shift,)


def _f_pre2(h, g1, sh1, sc1, g2, sh2, sc2):
    return _f_pre(h, g1, sh1, sc1) + _f_pre(h, g2, sh2, sc2)


def _f_post(y, gate, g):
    return (gate * _rms(y, g),)


def _f_post_bias(y, bias, gate, g):
    return (gate * _rms(y + bias, g),)


def _f_glu(z, bias):
    z = z + bias
    half = z.shape[1] // 2
    return (z[:, :half] * jax.nn.sigmoid(z[:, half:]),)


def _f_ln_silu(u, g, b):
    mu = jnp.mean(u, axis=-1, keepdims=True)
    var = jnp.mean(jnp.square(u - mu), axis=-1, keepdims=True)
    y = (u - mu) * lax.rsqrt(var + EPS) * g + b
    return (y * jax.nn.sigmoid(y),)


def _rope_raw(x, cos, sin):
    n = x.shape[1]
    reps = n // LANES
    if reps > 1:
        cos = jnp.concatenate([cos] * reps, axis=1)
        sin = jnp.concatenate([sin] * reps, axis=1)
    lane = lax.broadcasted_iota(jnp.int32, x.shape, 1)
    half = QK_ROPE // 2
    partner = jnp.where((lane % QK_ROPE) < half, pltpu.roll(x, n - half, 1), pltpu.roll(x, half, 1))
    return x * cos + partner * sin


@jax.custom_vjp
def _rope(x, cos, sin):
    return _rope_raw(x, cos, sin)


def _rope_fwd(x, cos, sin):
    return _rope_raw(x, cos, sin), (cos, sin)


def _rope_bwd(res, g):
    cos, sin = res
    return _rope_raw(g, cos, -sin), jnp.zeros_like(cos), jnp.zeros_like(sin)


_rope.defvjp(_rope_fwd, _rope_bwd)


def _make_f_lat(r_kv):
    def f(pre_kv, pre_q, cos, sin, g_ckv, g_cq):
        c_kv = _rms(pre_kv[:, :r_kv], g_ckv)
        kr = _rope(pre_kv[:, r_kv:], cos, sin)
        return c_kv, kr, _rms(pre_q, g_cq)
    return f


def _f_rope(x, cos, sin):
    return (_rope(x, cos, sin),)


CONV_TC = 128
CONV_TT = 256
PADR = 32


def _conv_fwd(u, dw, dw_b):
    t, d = u.shape
    tc, tt = min(CONV_TC, d), min(CONV_TT, t)
    off = PADR - (CONV_WIDTH - 1)

    def body(u_ref, w_ref, b_ref, o_ref, pad_ref):
        pad_ref[pl.ds(0, PADR), :] = jnp.zeros((PADR, tc), F32)
        pad_ref[pl.ds(PADR, t), :] = u_ref[...]
        for t0 in range(0, t, tt):
            acc = jnp.zeros((tt, tc), F32) + b_ref[...]
            for j in range(CONV_WIDTH):
                acc = acc + pad_ref[pl.ds(t0 + off + j, tt), :] * w_ref[pl.ds(j, 1), :]
            o_ref[pl.ds(t0, tt), :] = acc

    return pl.pallas_call(
        body, grid=(d // tc,),
        in_specs=[pl.BlockSpec((t, tc), lambda j: (0, j)), pl.BlockSpec((CONV_WIDTH, tc), lambda j: (0, j)),
                  pl.BlockSpec((1, tc), lambda j: (0, j))],
        out_specs=pl.BlockSpec((t, tc), lambda j: (0, j)),
        out_shape=jax.ShapeDtypeStruct((t, d), F32),
        scratch_shapes=[pltpu.VMEM((t + PADR, tc), F32)],
        compiler_params=_params(("parallel",)), name="conv_fwd",
    )(u, dw, dw_b)


def _conv_bwd(u, duc, dw):
    t, d = u.shape
    tc, tt = min(CONV_TC, d), min(CONV_TT, t)
    off = PADR - (CONV_WIDTH - 1)

    def body(u_ref, g_ref, w_ref, du_ref, dwt_ref, padu_ref, padg_ref):
        padu_ref[pl.ds(0, PADR), :] = jnp.zeros((PADR, tc), F32)
        padu_ref[pl.ds(PADR, t), :] = u_ref[...]
        padg_ref[pl.ds(t, PADR), :] = jnp.zeros((PADR, tc), F32)
        padg_ref[pl.ds(0, t), :] = g_ref[...]
        for t0 in range(0, t, tt):
            acc = jnp.zeros((tt, tc), F32)
            for j in range(CONV_WIDTH):
                acc = acc + padg_ref[pl.ds(t0 + (CONV_WIDTH - 1) - j, tt), :] * w_ref[pl.ds(j, 1), :]
            du_ref[pl.ds(t0, tt), :] = acc
        for j in range(CONV_WIDTH):
            acc = jnp.zeros((tt, tc), F32)
            for t0 in range(0, t, tt):
                acc = acc + g_ref[pl.ds(t0, tt), :] * padu_ref[pl.ds(t0 + off + j, tt), :]
            dwt_ref[pl.ds(j, 1), :] = jnp.sum(acc, axis=0, keepdims=True)
        acc = jnp.zeros((tt, tc), F32)
        for t0 in range(0, t, tt):
            acc = acc + g_ref[pl.ds(t0, tt), :]
        dwt_ref[pl.ds(CONV_WIDTH, 1), :] = jnp.sum(acc, axis=0, keepdims=True)

    col = lambda r: pl.BlockSpec((r, tc), lambda j: (0, j))
    return pl.pallas_call(
        body, grid=(d // tc,),
        in_specs=[col(t), col(t), col(CONV_WIDTH)],
        out_specs=[col(t), col(CONV_WIDTH + 1)],
        out_shape=[jax.ShapeDtypeStruct((t, d), F32), jax.ShapeDtypeStruct((CONV_WIDTH + 1, d), F32)],
        scratch_shapes=[pltpu.VMEM((t + PADR, tc), F32), pltpu.VMEM((t + PADR, tc), F32)],
        compiler_params=_params(("parallel",)), name="conv_bwd",
    )(u, duc, dw)


@jax.custom_vjp
def _swap_halves(x):
    return pltpu.roll(x, LANES // 2, 1)


_swap_halves.defvjp(lambda x: (pltpu.roll(x, LANES // 2, 1), None), lambda _, g: (pltpu.roll(g, LANES // 2, 1),))


def _attn_block(qn, qr, kn, kr, v, q0):
    scale = (QK_NOPE + QK_ROPE) ** -0.5
    lane = lax.broadcasted_iota(jnp.int32, kr.shape, 1)
    kr_a = kr * (lane < QK_ROPE).astype(kr.dtype)
    kr_b = _swap_halves(kr_a)
    outs = []
    for hh, kr_h in ((0, kr_a), (1, kr_b)):
        sl = slice(hh * QK_NOPE, (hh + 1) * QK_NOPE)
        s = _dot_nt(jnp.concatenate([qn[:, sl], qr], axis=1), jnp.concatenate([kn[:, sl], kr_h], axis=1)) * scale
        row = lax.broadcasted_iota(jnp.int32, s.shape, 0) + q0
        col = lax.broadcasted_iota(jnp.int32, s.shape, 1)
        s = jnp.where(col <= row, s, NEG)
        e = jnp.exp(s - jnp.max(s, axis=-1, keepdims=True))
        p = e / jnp.sum(e, axis=-1, keepdims=True)
        outs.append(_dot_nn(p, v[:, sl]))
    return jnp.concatenate(outs, axis=1)


def _attn_fwd(qn, qr, kn, kr, v, tq=512):
    t, w = qn.shape
    pairs = w // (2 * QK_NOPE)
    tq = min(tq, t)
    pw = 2 * QK_NOPE

    def body(qn_ref, qr_ref, kn_ref, kr_ref, v_ref, o_ref):
        for q0 in range(0, t, tq):
            l = q0 + tq
            o_ref[pl.ds(q0, tq), :] = _attn_block(
                qn_ref[pl.ds(q0, tq), :], qr_ref[pl.ds(q0, tq), :], kn_ref[pl.ds(0, l), :], kr_ref[pl.ds(0, l), :],
                v_ref[pl.ds(0, l), :], q0).astype(o_ref.dtype)

    pair = lambda wd: pl.BlockSpec((t, wd), lambda p: (0, p))
    return pl.pallas_call(
        body, grid=(pairs,),
        in_specs=[pair(pw), pair(LANES), pair(pw), pl.BlockSpec((t, LANES), lambda p: (0, 0)), pair(pw)],
        out_specs=pair(pw), out_shape=jax.ShapeDtypeStruct((t, w), BF),
        compiler_params=_params(("parallel",)), name="attn_fwd",
    )(qn, qr, kn, kr, v)


def _attn_bwd(qn, qr, kn, kr, v, do, tq=256):
    t, w = qn.shape
    pairs = w // (2 * QK_NOPE)
    tq = min(tq, t)
    pw = 2 * QK_NOPE

    def body(qn_ref, qr_ref, kn_ref, kr_ref, v_ref, do_ref, dqn_ref, dqr_ref, dkn_ref, dv_ref, dkr_ref,
             akn_ref, av_ref, akr_ref):
        akn_ref[...] = jnp.zeros_like(akn_ref)
        av_ref[...] = jnp.zeros_like(av_ref)
        akr_ref[...] = jnp.zeros_like(akr_ref)
        for q0 in range(0, t, tq):
            l = q0 + tq
            rows, keys = pl.ds(q0, tq), pl.ds(0, l)
            _, vjp = jax.vjp(functools.partial(_attn_block, q0=q0), qn_ref[rows, :], qr_ref[rows, :],
                             kn_ref[keys, :], kr_ref[keys, :], v_ref[keys, :])
            dqn, dqr, dkn, dkr, dv = vjp(do_ref[rows, :].astype(F32))
            dqn_ref[rows, :] = dqn.astype(dqn_ref.dtype)
            dqr_ref[rows, :] = dqr.astype(dqr_ref.dtype)
            akn_ref[keys, :] += dkn.astype(F32)
            av_ref[keys, :] += dv.astype(F32)
            akr_ref[keys, :] += dkr.astype(F32)
        dkn_ref[...] = akn_ref[...].astype(dkn_ref.dtype)
        dv_ref[...] = av_ref[...].astype(dv_ref.dtype)
        first = pl.program_id(0) == 0

        @pl.when(first)
        def _():
            dkr_ref[...] = akr_ref[...]

        @pl.when(jnp.logical_not(first))
        def _():
            dkr_ref[...] += akr_ref[...]

    pair = lambda wd: pl.BlockSpec((t, wd), lambda p: (0, p))
    shared = pl.BlockSpec((t, LANES), lambda p: (0, 0))
    sds = jax.ShapeDtypeStruct
    return pl.pallas_call(
        body, grid=(pairs,),
        in_specs=[pair(pw), pair(LANES), pair(pw), shared, pair(pw), pair(pw)],
        out_specs=[pair(pw), pair(LANES), pair(pw), pair(pw), shared],
        out_shape=[sds((t, w), BF), sds((t, pairs * LANES), F32), sds((t, w), BF), sds((t, w), BF), sds((t, LANES), F32)],
        scratch_shapes=[pltpu.VMEM((t, pw), F32), pltpu.VMEM((t, pw), F32), pltpu.VMEM((t, LANES), F32)],
        compiler_params=_params(("arbitrary",)), name="attn_bwd",
    )(qn, qr, kn, kr, v, do)


def _ew(name, fn, ins, o_dtypes, max_bytes=2 ** 21):
    r, c = ins[0].shape
    tr = r
    if r * c * 4 > max_bytes:
        tr = max(16, (max_bytes // (c * 4)) // 16 * 16)
        while r % tr:
            tr -= 16
    n_in = len(ins)

    def body(*refs):
        outs = fn(*[x[...] for x in refs[:n_in]])
        for o_ref, o in zip(refs[n_in:], outs):
            o_ref[...] = o.astype(o_ref.dtype)

    spec = pl.BlockSpec((tr, c), lambda i: (i, 0))
    return pl.pallas_call(
        body, grid=(r // tr,), in_specs=[spec] * n_in, out_specs=[spec] * len(o_dtypes),
        out_shape=[jax.ShapeDtypeStruct((r, c), dt) for dt in o_dtypes],
        compiler_params=_params(("parallel",)), name=name,
    )(*ins)


def _adamw_math(w, g, m, v):
    m = ADAM_B1 * m + (1.0 - ADAM_B1) * g
    v = ADAM_B2 * v + (1.0 - ADAM_B2) * jnp.square(g)
    m_hat = m / (1.0 - ADAM_B1 ** ADAM_STEP)
    v_hat = v / (1.0 - ADAM_B2 ** ADAM_STEP)
    delta = -ADAM_LR * (m_hat / (jnp.sqrt(v_hat) + ADAM_EPS) + ADAM_WD * w)
    return delta, m, v


def _adamw(name, w, g, m, v):
    shape = w.shape
    to2 = lambda a: a.reshape(-1, shape[-1]) if a.ndim > 1 else a.reshape(1, -1)
    d, nm, nv = _ew("adamw_" + name, _adamw_math, [to2(w), to2(g), to2(m), to2(v)], (F32, F32, F32))
    return d.reshape(shape), nm.reshape(shape), nv.reshape(shape)


def _adamw_layer(name, w, g, m, v, layer, prev=()):
    _, r, n = w.shape
    tr = r
    while tr * n * 4 > 2 ** 21 and tr % 16 == 0:
        tr //= 2

    def body(*refs):
        w_ref, g_ref, m_ref, v_ref = refs[:4]
        og, od, om, ov = refs[-4:]
        gval = g_ref[...].astype(F32)
        d, nm, nv = _adamw_math(w_ref[...], gval, m_ref[...], v_ref[...])
        og[...] = gval
        od[...] = d
        om[...] = nm
        ov[...] = nv

    lay = pl.BlockSpec((None, tr, n), lambda i: (layer, i, 0))
    return pl.pallas_call(
        body, grid=(r // tr,),
        in_specs=[lay, pl.BlockSpec((tr, n), lambda i: (i, 0)), lay, lay] + [ANY] * len(prev),
        out_specs=[lay] * 4, out_shape=[jax.ShapeDtypeStruct(w.shape, F32)] * 4,
        input_output_aliases={4 + k: k for k in range(len(prev))},
        compiler_params=_params(("parallel",)), name="adamw_%s_%d" % (name, layer),
    )(w, g, m, v, *prev)


def _rope_tables(positions, t):
    half = QK_ROPE // 2
    inv = 1.0 / (ROPE_THETA ** (jnp.arange(0, QK_ROPE, 2, dtype=F32) / QK_ROPE))
    inv_l = jnp.tile(inv, LANES // half).reshape(1, LANES)
    sign = jnp.tile(jnp.concatenate([-jnp.ones((half,), F32), jnp.ones((half,), F32)]), LANES // QK_ROPE).reshape(1, LANES)

    def body(p_ref, inv_ref, sg_ref, c_ref, s_ref):
        ang = p_ref[...].astype(F32) * inv_ref[...]
        c_ref[...] = jnp.cos(ang)
        s_ref[...] = jnp.sin(ang) * sg_ref[...]

    return pl.pallas_call(body, out_shape=[jax.ShapeDtypeStruct((t, LANES), F32)] * 2, name="rope_tables")(
        positions.reshape(t, 1), inv_l, sign)


def _loss_grad(h, target):
    t, d = h.shape
    tr = min(TR, t)

    def body(h_ref, y_ref, dh_ref, l_ref):
        err = h_ref[...] - y_ref[...]
        dh_ref[...] = err * (1.0 / d)
        part = 0.5 * jnp.sum(jnp.mean(jnp.square(err), axis=-1, keepdims=True), axis=0, keepdims=True)
        first = pl.program_id(0) == 0

        @pl.when(first)
        def _():
            l_ref[...] = part

        @pl.when(jnp.logical_not(first))
        def _():
            l_ref[...] += part

    row = pl.BlockSpec((tr, d), lambda i: (i, 0))
    return pl.pallas_call(
        body, grid=(t // tr,), in_specs=[row, row],
        out_specs=[row, pl.BlockSpec((1, 1), lambda i: (0, 0))],
        out_shape=[jax.ShapeDtypeStruct((t, d), F32), jax.ShapeDtypeStruct((1, 1), F32)],
        compiler_params=_params(("arbitrary",)), name="loss_grad",
    )(h, target)


def _sum_devices(g):
    def fn(*parts):
        acc = parts[0]
        for p in parts[1:]:
            acc = acc + p
        return (acc,)
    return _ew("sum_devices", fn, [g[i] for i in range(g.shape[0])], (F32,))[0]


def _place():
    x, y, c = lax.axis_index("x"), lax.axis_index("y"), lax.axis_index("c")
    return x, y, c, [(1 - x, y), (x, 1 - y), (1 - x, 1 - y)]


def _all_gather_small(name, v):
    r, n = v.shape

    def body(x_ref, out_ref, send_sems, recv_sems, local_sem):
        x, y, c, chips = _place()
        me, sibling = (x, y, c), (x, y, 1 - c)

        def rows(px, py, pc):
            return out_ref.at[4 * px + 2 * py + pc]

        def copy(k, block, to, src=None):
            return pltpu.make_async_remote_copy(
                src_ref=rows(*block) if src is None else src, dst_ref=rows(*block), send_sem=send_sems.at[k],
                recv_sem=recv_sems.at[k], device_id=to, device_id_type=MESH)

        mine = pltpu.make_async_copy(x_ref, rows(*me), local_sem)
        mine.start()
        first = [copy(0, me, sibling, src=x_ref)]
        first += [copy(1 + j, me, (*chip, c), src=x_ref) for j, chip in enumerate(chips)]
        for cp in first:
            cp.start()
        passed = [copy(4 + j, (*chip, c), sibling) for j, chip in enumerate(chips)]
        for j, chip in enumerate(chips):
            copy(1 + j, (*chip, c), me).wait_recv()
            passed[j].start()
        copy(0, sibling, me).wait_recv()
        for j, chip in enumerate(chips):
            copy(4 + j, (*chip, 1 - c), me).wait_recv()
        for cp in first + passed:
            cp.wait_send()
        mine.wait()

    return pl.pallas_call(
        body, out_shape=jax.ShapeDtypeStruct((N_DEV, r, n), v.dtype),
        in_specs=[pl.BlockSpec(memory_space=pltpu.VMEM)], out_specs=pl.BlockSpec(memory_space=pltpu.VMEM),
        scratch_shapes=[pltpu.SemaphoreType.DMA((7,)), pltpu.SemaphoreType.DMA((7,)), pltpu.SemaphoreType.DMA],
        compiler_params=pltpu.CompilerParams(vmem_limit_bytes=VMEM_LIMIT), name=name,
    )(v)


def _cast_into_slot(name, w, layer, idx, after=None):
    _, r, n = w.shape
    tr = r
    while tr * n * 4 > 2 ** 21 and tr % 32 == 0:
        tr //= 2
    order = [] if after is None else [after]

    def body(idx_ref, w_ref, *rest):
        o_ref = rest[-1]
        o_ref[...] = w_ref[...].astype(o_ref.dtype)

    return pl.pallas_call(
        body,
        grid_spec=pltpu.PrefetchScalarGridSpec(
            num_scalar_prefetch=1, grid=(r // tr,),
            in_specs=[pl.BlockSpec((None, tr, n), lambda i, idx_ref: (layer, i, 0))] + [ANY] * len(order),
            out_specs=pl.BlockSpec((None, tr, n), lambda i, idx_ref: (idx_ref[1], i, 0))),
        out_shape=jax.ShapeDtypeStruct((N_CHIPS, r, n), BF),
        compiler_params=_params(("parallel",)), name="cast_" + name,
    )(idx, w, *order)


def _swap_other_half(name, grads, after=None):
    n_w = len(grads)
    order = [] if after is None else [after]
    n_in = n_w + len(order)

    def body(*refs):
        ins, outs = refs[:n_w], refs[n_in:n_in + n_w]
        send_sems, recv_sems = refs[n_in + n_w:]
        x, y, c, _ = _place()
        cps = []
        for i in range(n_w):
            h = ins[i].shape[1] // 2
            cp = pltpu.make_async_remote_copy(
                src_ref=ins[i].at[:, pl.ds((1 - c) * h, h), :], dst_ref=outs[i], send_sem=send_sems.at[i],
                recv_sem=recv_sems.at[i], device_id=(x, y, 1 - c), device_id_type=MESH)
            cp.start()
            cps.append(cp)
        for cp in cps:
            cp.wait()

    return pl.pallas_call(
        body, in_specs=[ANY] * n_in, out_specs=[ANY] * n_w,
        out_shape=[jax.ShapeDtypeStruct((g.shape[0], g.shape[1] // 2, g.shape[2]), g.dtype) for g in grads],
        scratch_shapes=[pltpu.SemaphoreType.DMA((n_w,)), pltpu.SemaphoreType.DMA((n_w,))],
        name=name,
    )(*grads, *order)


def _add_my_half(name, g, s, idx, o_dtype):
    nc, r, n = g.shape
    h = r // 2
    tr = h
    while tr * n * 4 > 2 ** 21 and tr % 32 == 0:
        tr //= 2
    nb = h // tr

    def body(idx_ref, g_ref, s_ref, o_ref, own_ref):
        val = (g_ref[...].astype(F32) + s_ref[...].astype(F32)).astype(o_ref.dtype)
        o_ref[...] = val

        @pl.when(pl.program_id(1) == idx_ref[1])
        def _():
            own_ref[...] = val

    return pl.pallas_call(
        body,
        grid_spec=pltpu.PrefetchScalarGridSpec(
            num_scalar_prefetch=1, grid=(nb, nc),
            in_specs=[pl.BlockSpec((None, tr, n), lambda i, j, idx_ref: (j, idx_ref[0] * nb + i, 0)),
                      pl.BlockSpec((None, tr, n), lambda i, j, idx_ref: (j, i, 0))],
            out_specs=[pl.BlockSpec((None, tr, n), lambda i, j, idx_ref: (j, i, 0)),
                       pl.BlockSpec((None, tr, n), lambda i, j, idx_ref: (idx_ref[1], i, 0))]),
        out_shape=[jax.ShapeDtypeStruct((nc, h, n), o_dtype)] * 2,
        compiler_params=_params(("parallel", "arbitrary")), name=name,
    )(idx, g, s)


def _sum_chips(name, q, idx):
    nc, h, n = q.shape
    tr = h
    while tr * n * 4 > 2 ** 20 and tr % 32 == 0:
        tr //= 2
    nb = h // tr

    def body(idx_ref, q_ref, o_ref):
        acc = q_ref[0].astype(F32)
        for j in range(1, nc):
            acc = acc + q_ref[j].astype(F32)
        o_ref[...] = acc

    return pl.pallas_call(
        body,
        grid_spec=pltpu.PrefetchScalarGridSpec(
            num_scalar_prefetch=1, grid=(nb,),
            in_specs=[pl.BlockSpec((nc, tr, n), lambda i, idx_ref: (0, i, 0))],
            out_specs=pl.BlockSpec((tr, n), lambda i, idx_ref: (idx_ref[0] * nb + i, 0))),
        out_shape=jax.ShapeDtypeStruct((2 * h, n), F32),
        compiler_params=_params(("parallel",)), name=name,
    )(idx, q)


def _join_halves(name, bufs):
    n_w = len(bufs)

    def body(*refs):
        outs = refs[n_w:2 * n_w]
        send_sems, recv_sems = refs[2 * n_w:]
        x, y, c, _ = _place()

        def copy(i, which):
            h = outs[i].shape[0] // 2
            rows = outs[i].at[pl.ds(which * h, h)]
            return pltpu.make_async_remote_copy(
                src_ref=rows, dst_ref=rows, send_sem=send_sems.at[i], recv_sem=recv_sems.at[i],
                device_id=(x, y, 1 - c), device_id_type=MESH)

        cps = [copy(i, c) for i in range(n_w)]
        for cp in cps:
            cp.start()
        for i, cp in enumerate(cps):
            cp.wait_send()
            copy(i, 1 - c).wait_recv()

    return pl.pallas_call(
        body, in_specs=[ANY] * n_w, out_specs=[ANY] * n_w,
        out_shape=[jax.ShapeDtypeStruct(b.shape, b.dtype) for b in bufs],
        input_output_aliases={i: i for i in range(n_w)},
        scratch_shapes=[pltpu.SemaphoreType.DMA((n_w,)), pltpu.SemaphoreType.DMA((n_w,))],
        name=name,
    )(*bufs)


HBM_SPEC = pl.BlockSpec(memory_space=pltpu.HBM)
SEM_SPEC = pl.BlockSpec(memory_space=pltpu.SEMAPHORE)
VMEM_SPEC = pl.BlockSpec(memory_space=pltpu.VMEM)
SIDE_EFFECT = pltpu.SideEffectType.DATAFLOW_SIDE_EFFECTING


def _in_hbm(arrays):
    return [pltpu.with_memory_space_constraint(a, pltpu.HBM) for a in arrays]


def _ici_gather_copy(ref, i, j, chip, send_sems, recv_sems):
    x, y, c, _ = _place()
    h = ref.shape[1] // 2
    mine = ref.at[2 * x + y, pl.ds(c * h, h)]
    return pltpu.make_async_remote_copy(
        src_ref=mine, dst_ref=mine, send_sem=send_sems.at[3 * i + j], recv_sem=recv_sems.at[3 * i + j],
        device_id=(*chip, c), device_id_type=MESH)


def _gather_start(name, bufs, after=None):
    n_w = len(bufs)
    order = [] if after is None else [after]

    def body(*refs):
        ins, token = refs[:n_w], refs[-1]
        send_sems, recv_sems = refs[n_w + len(order)], refs[n_w + len(order) + 1]
        chips = _place()[3]
        for i in range(n_w):
            for j, chip in enumerate(chips):
                _ici_gather_copy(ins[i], i, j, chip, send_sems, recv_sems).start()
        token[...] = jnp.zeros_like(token)

    res = pl.pallas_call(
        body, name=name, in_specs=[HBM_SPEC] * n_w + [ANY] * len(order),
        out_shape=(pltpu.SemaphoreType.DMA((3 * n_w,)), pltpu.SemaphoreType.DMA((3 * n_w,)),
                   *[pltpu.HBM(b.shape, b.dtype) for b in bufs], jax.ShapeDtypeStruct((8, LANES), F32)),
        out_specs=(SEM_SPEC, SEM_SPEC, *[HBM_SPEC] * n_w, VMEM_SPEC),
        input_output_aliases={i: 2 + i for i in range(n_w)},
        compiler_params=pltpu.CompilerParams(has_side_effects=SIDE_EFFECT),
    )(*_in_hbm(bufs), *order)
    return res[0], res[1], list(res[2:2 + n_w]), res[-1]


def _gather_wait(name, send_sems, recv_sems, bufs, after):
    n_w = len(bufs)

    def body(*refs):
        ins, s_sems, r_sems = refs[:n_w], refs[n_w], refs[n_w + 1]
        chips = _place()[3]
        for i in range(n_w):
            for j, chip in enumerate(chips):
                cp = _ici_gather_copy(ins[i], i, j, chip, s_sems, r_sems)
                cp.wait_send()
                cp.wait_recv()

    return pl.pallas_call(
        body, name=name, in_specs=[HBM_SPEC] * n_w + [SEM_SPEC, SEM_SPEC, ANY],
        out_shape=[pltpu.HBM(b.shape, b.dtype) for b in bufs], out_specs=[HBM_SPEC] * n_w,
        input_output_aliases={i: i for i in range(n_w)},
        compiler_params=pltpu.CompilerParams(has_side_effects=SIDE_EFFECT),
    )(*bufs, send_sems, recv_sems, after)


def _forward_to_sibling(name, bufs):
    n_w = len(bufs)

    def body(*refs):
        outs = refs[n_w:2 * n_w]
        send_sems, recv_sems = refs[2 * n_w:]
        x, y, c, chips = _place()

        def copy(i, j, chip, which):
            h = outs[i].shape[1] // 2
            blk = outs[i].at[2 * chip[0] + chip[1], pl.ds(which * h, h)]
            return pltpu.make_async_remote_copy(
                src_ref=blk, dst_ref=blk, send_sem=send_sems.at[3 * i + j], recv_sem=recv_sems.at[3 * i + j],
                device_id=(x, y, 1 - c), device_id_type=MESH)

        sends = [copy(i, j, chip, c) for i in range(n_w) for j, chip in enumerate(chips)]
        for cp in sends:
            cp.start()
        for i in range(n_w):
            for j, chip in enumerate(chips):
                copy(i, j, chip, 1 - c).wait_recv()
        for cp in sends:
            cp.wait_send()

    return pl.pallas_call(
        body, in_specs=[ANY] * n_w, out_specs=[ANY] * n_w,
        out_shape=[jax.ShapeDtypeStruct(b.shape, b.dtype) for b in bufs],
        input_output_aliases={i: i for i in range(n_w)},
        scratch_shapes=[pltpu.SemaphoreType.DMA((3 * n_w,)), pltpu.SemaphoreType.DMA((3 * n_w,))],
        name=name,
    )(*bufs)


def _forward_copy(buf, i, j, chip, which, send_sems, recv_sems):
    x, y, c, _ = _place()
    h = buf.shape[1] // 2
    blk = buf.at[2 * chip[0] + chip[1], pl.ds(which * h, h)]
    return pltpu.make_async_remote_copy(
        src_ref=blk, dst_ref=blk, send_sem=send_sems.at[3 * i + j], recv_sem=recv_sems.at[3 * i + j],
        device_id=(x, y, 1 - c), device_id_type=MESH)


def _forward_start(name, bufs):
    n_w = len(bufs)

    def body(*refs):
        ins, send_sems, recv_sems, token = refs[:n_w], refs[n_w], refs[n_w + 1], refs[-1]
        x, y, c, chips = _place()
        for i in range(n_w):
            for j, chip in enumerate(chips):
                _forward_copy(ins[i], i, j, chip, c, send_sems, recv_sems).start()
        token[...] = jnp.zeros_like(token)

    res = pl.pallas_call(
        body, name=name, in_specs=[HBM_SPEC] * n_w,
        out_shape=(pltpu.SemaphoreType.DMA((3 * n_w,)), pltpu.SemaphoreType.DMA((3 * n_w,)),
                   *[pltpu.HBM(b.shape, b.dtype) for b in bufs], jax.ShapeDtypeStruct((8, LANES), F32)),
        out_specs=(SEM_SPEC, SEM_SPEC, *[HBM_SPEC] * n_w, VMEM_SPEC),
        input_output_aliases={i: 2 + i for i in range(n_w)},
        compiler_params=pltpu.CompilerParams(has_side_effects=SIDE_EFFECT),
    )(*_in_hbm(bufs))
    return (res[0], res[1], list(res[2:2 + n_w])), res[-1]


def _forward_wait(name, state, after):
    send_sems, recv_sems, bufs = state
    n_w = len(bufs)

    def body(*refs):
        ins, s_sems, r_sems = refs[:n_w], refs[n_w], refs[n_w + 1]
        x, y, c, chips = _place()
        for i in range(n_w):
            for j, chip in enumerate(chips):
                _forward_copy(ins[i], i, j, chip, c, s_sems, r_sems).wait_send()
                _forward_copy(ins[i], i, j, chip, 1 - c, s_sems, r_sems).wait_recv()

    return pl.pallas_call(
        body, name=name, in_specs=[HBM_SPEC] * n_w + [SEM_SPEC, SEM_SPEC, ANY],
        out_shape=[pltpu.HBM(b.shape, b.dtype) for b in bufs], out_specs=[HBM_SPEC] * n_w,
        input_output_aliases={i: i for i in range(n_w)},
        compiler_params=pltpu.CompilerParams(has_side_effects=SIDE_EFFECT),
    )(*bufs, send_sems, recv_sems, after)


def _ici_scatter_copy(part, buf, i, j, chip, send_sems, recv_sems):
    x, y, c, _ = _place()
    return pltpu.make_async_remote_copy(
        src_ref=part.at[2 * chip[0] + chip[1]], dst_ref=buf.at[2 * x + y], send_sem=send_sems.at[3 * i + j],
        recv_sem=recv_sems.at[3 * i + j], device_id=(*chip, c), device_id_type=MESH)


def _ici_scatter_landing(buf, i, j, chip, send_sems, recv_sems):
    x, y, c, _ = _place()
    blk = buf.at[2 * chip[0] + chip[1]]
    return pltpu.make_async_remote_copy(
        src_ref=blk, dst_ref=blk, send_sem=send_sems.at[3 * i + j], recv_sem=recv_sems.at[3 * i + j],
        device_id=(*chip, c), device_id_type=MESH)


def _scatter_start(name, parts, bufs):
    n_w = len(parts)

    def body(*refs):
        ps, bs = refs[:n_w], refs[n_w:2 * n_w]
        send_sems, recv_sems, token = refs[2 * n_w], refs[2 * n_w + 1], refs[-1]
        chips = _place()[3]
        for i in range(n_w):
            for j, chip in enumerate(chips):
                _ici_scatter_copy(ps[i], bs[i], i, j, chip, send_sems, recv_sems).start()
        token[...] = jnp.zeros_like(token)

    both = list(parts) + list(bufs)
    res = pl.pallas_call(
        body, name=name, in_specs=[HBM_SPEC] * (2 * n_w),
        out_shape=(pltpu.SemaphoreType.DMA((3 * n_w,)), pltpu.SemaphoreType.DMA((3 * n_w,)),
                   *[pltpu.HBM(b.shape, b.dtype) for b in both], jax.ShapeDtypeStruct((8, LANES), F32)),
        out_specs=(SEM_SPEC, SEM_SPEC, *[HBM_SPEC] * (2 * n_w), VMEM_SPEC),
        input_output_aliases={i: 2 + i for i in range(2 * n_w)},
        compiler_params=pltpu.CompilerParams(has_side_effects=SIDE_EFFECT),
    )(*_in_hbm(both))
    return res[0], res[1], list(res[2:2 + n_w]), list(res[2 + n_w:2 + 2 * n_w]), res[-1]


def _scatter_wait(name, send_sems, recv_sems, parts, bufs, after):
    n_w = len(parts)
    after = list(after) if isinstance(after, (list, tuple)) else [after]

    def body(*refs):
        ps, bs = refs[:n_w], refs[n_w:2 * n_w]
        s_sems, r_sems = refs[2 * n_w], refs[2 * n_w + 1]
        chips = _place()[3]
        for i in range(n_w):
            for j, chip in enumerate(chips):
                _ici_scatter_copy(ps[i], bs[i], i, j, chip, s_sems, r_sems).wait_send()
                _ici_scatter_landing(bs[i], i, j, chip, s_sems, r_sems).wait_recv()

    both = list(parts) + list(bufs)
    res = pl.pallas_call(
        body, name=name, in_specs=[HBM_SPEC] * (2 * n_w) + [SEM_SPEC, SEM_SPEC] + [ANY] * len(after),
        out_shape=[pltpu.HBM(b.shape, b.dtype) for b in both], out_specs=[HBM_SPEC] * (2 * n_w),
        input_output_aliases={i: i for i in range(2 * n_w)},
        compiler_params=pltpu.CompilerParams(has_side_effects=SIDE_EFFECT),
    )(*both, send_sems, recv_sems, *after)
    return list(res[n_w:])


def _swap_copy(g, land, i, send_sems, recv_sems):
    x, y, c, _ = _place()
    h = g.shape[1] // 2
    return pltpu.make_async_remote_copy(
        src_ref=g.at[:, pl.ds((1 - c) * h, h), :], dst_ref=land, send_sem=send_sems.at[i], recv_sem=recv_sems.at[i],
        device_id=(x, y, 1 - c), device_id_type=MESH)


def _swap_start(name, grads):
    n_w = len(grads)
    lands = [lax.empty((g.shape[0], g.shape[1] // 2, g.shape[2]), g.dtype) for g in grads]

    def body(*refs):
        gs, ls = refs[:n_w], refs[n_w:2 * n_w]
        send_sems, recv_sems, token = refs[2 * n_w], refs[2 * n_w + 1], refs[-1]
        for i in range(n_w):
            _swap_copy(gs[i], ls[i], i, send_sems, recv_sems).start()
        token[...] = jnp.zeros_like(token)

    both = list(grads) + lands
    res = pl.pallas_call(
        body, name=name, in_specs=[HBM_SPEC] * (2 * n_w),
        out_shape=(pltpu.SemaphoreType.DMA((n_w,)), pltpu.SemaphoreType.DMA((n_w,)),
                   *[pltpu.HBM(b.shape, b.dtype) for b in both], jax.ShapeDtypeStruct((8, LANES), F32)),
        out_specs=(SEM_SPEC, SEM_SPEC, *[HBM_SPEC] * (2 * n_w), VMEM_SPEC),
        input_output_aliases={i: 2 + i for i in range(2 * n_w)},
        compiler_params=pltpu.CompilerParams(has_side_effects=SIDE_EFFECT),
    )(*_in_hbm(both))
    return (res[0], res[1], list(res[2:2 + n_w]), list(res[2 + n_w:2 + 2 * n_w])), res[-1]


def _swap_wait(name, swapped, after):
    send_sems, recv_sems, grads, lands = swapped
    n_w = len(grads)

    def body(*refs):
        gs, ls = refs[:n_w], refs[n_w:2 * n_w]
        s_sems, r_sems = refs[2 * n_w], refs[2 * n_w + 1]
        for i in range(n_w):
            cp = _swap_copy(gs[i], ls[i], i, s_sems, r_sems)
            cp.wait_send()
            cp.wait_recv()

    both = list(grads) + list(lands)
    res = pl.pallas_call(
        body, name=name, in_specs=[HBM_SPEC] * (2 * n_w) + [SEM_SPEC, SEM_SPEC, ANY],
        out_shape=[pltpu.HBM(b.shape, b.dtype) for b in both], out_specs=[HBM_SPEC] * (2 * n_w),
        input_output_aliases={i: i for i in range(2 * n_w)},
        compiler_params=pltpu.CompilerParams(has_side_effects=SIDE_EFFECT),
    )(*both, send_sems, recv_sems, after)
    return list(res[:n_w]), list(res[n_w:])


def _reduce_scatter_begin(tag, grads, idx, after=None, swapped=None):
    if swapped is None:
        from_sibling = _swap_other_half("swap_other_half_" + tag, grads, after)
    else:
        grads, from_sibling = _swap_wait("swap_wait_" + tag, swapped, after)
    pairs = [_add_my_half("add_my_half_%s_%d" % (tag, i), g, s, idx, BF)
             for i, (g, s) in enumerate(zip(grads, from_sibling))]
    s_sems, r_sems, parts, bufs, token = _scatter_start("scatter_start_" + tag, [p[0] for p in pairs],
                                                        [p[1] for p in pairs])
    return (tag, s_sems, r_sems, parts, bufs, idx), token


def _reduce_scatter_end(state, after):
    tag, s_sems, r_sems, parts, bufs, idx = state
    received = _scatter_wait("scatter_wait_" + tag, s_sems, r_sems, parts, bufs, after)
    halves = [_sum_chips("sum_chips_%s_%d" % (tag, i), q, idx) for i, q in enumerate(received)]
    return _join_halves("join_halves_" + tag, halves)


def _relu2_epilogue(acc):
    r = jnp.maximum(acc, 0.0)
    return (r * r,)


def _relu2_bwd_epilogue(acc, s):
    return (acc * (2.0 * jnp.sqrt(s.astype(F32))),)


WIDE = (2048, 512, 2048)
DEEP = (2048, 512, 2048)


def _add_epilogue(acc, other):
    return (acc + other,)


def kernel(x, c, positions, w_ada_mix, b_ada_mix, w_ada_mlp, b_ada_mlp, g_pre_mix, g_post_mix, g_pre_mlp, g_post_mlp, conv_w_in, conv_b_in, conv_dw, conv_dw_b, conv_ln_g, conv_ln_b, conv_w_out, conv_b_out, w_ada_kv, b_ada_kv, g_kv, w_dkv, g_ckv, w_kr, w_uk, w_uv, w_dq, g_cq, w_uq, w_o, mlp_w_up, mlp_w_down, loss_target, m_w_ada_mix, m_b_ada_mix, m_w_ada_mlp, m_b_ada_mlp, m_g_pre_mix, m_g_post_mix, m_g_pre_mlp, m_g_post_mlp, m_conv_w_in, m_conv_b_in, m_conv_dw, m_conv_dw_b, m_conv_ln_g, m_conv_ln_b, m_conv_w_out, m_conv_b_out, m_w_ada_kv, m_b_ada_kv, m_g_kv, m_w_dkv, m_g_ckv, m_w_kr, m_w_uk, m_w_uv, m_w_dq, m_g_cq, m_w_uq, m_w_o, m_mlp_w_up, m_mlp_w_down, v_w_ada_mix, v_b_ada_mix, v_w_ada_mlp, v_b_ada_mlp, v_g_pre_mix, v_g_post_mix, v_g_pre_mlp, v_g_post_mlp, v_conv_w_in, v_conv_b_in, v_conv_dw, v_conv_dw_b, v_conv_ln_g, v_conv_ln_b, v_conv_w_out, v_conv_b_out, v_w_ada_kv, v_b_ada_kv, v_g_kv, v_w_dkv, v_g_ckv, v_w_kr, v_w_uk, v_w_uv, v_w_dq, v_g_cq, v_w_uq, v_w_o, v_mlp_w_up, v_mlp_w_down):
    weights = dict(w_ada_mix=w_ada_mix, b_ada_mix=b_ada_mix, w_ada_mlp=w_ada_mlp, b_ada_mlp=b_ada_mlp, g_pre_mix=g_pre_mix, g_post_mix=g_post_mix, g_pre_mlp=g_pre_mlp, g_post_mlp=g_post_mlp, conv_w_in=conv_w_in, conv_b_in=conv_b_in, conv_dw=conv_dw, conv_dw_b=conv_dw_b, conv_ln_g=conv_ln_g, conv_ln_b=conv_ln_b, conv_w_out=conv_w_out, conv_b_out=conv_b_out, w_ada_kv=w_ada_kv, b_ada_kv=b_ada_kv, g_kv=g_kv, w_dkv=w_dkv, g_ckv=g_ckv, w_kr=w_kr, w_uk=w_uk, w_uv=w_uv, w_dq=w_dq, g_cq=g_cq, w_uq=w_uq, w_o=w_o, mlp_w_up=mlp_w_up, mlp_w_down=mlp_w_down)
    m_in = dict(w_ada_mix=m_w_ada_mix, b_ada_mix=m_b_ada_mix, w_ada_mlp=m_w_ada_mlp, b_ada_mlp=m_b_ada_mlp, g_pre_mix=m_g_pre_mix, g_post_mix=m_g_post_mix, g_pre_mlp=m_g_pre_mlp, g_post_mlp=m_g_post_mlp, conv_w_in=m_conv_w_in, conv_b_in=m_conv_b_in, conv_dw=m_conv_dw, conv_dw_b=m_conv_dw_b, conv_ln_g=m_conv_ln_g, conv_ln_b=m_conv_ln_b, conv_w_out=m_conv_w_out, conv_b_out=m_conv_b_out, w_ada_kv=m_w_ada_kv, b_ada_kv=m_b_ada_kv, g_kv=m_g_kv, w_dkv=m_w_dkv, g_ckv=m_g_ckv, w_kr=m_w_kr, w_uk=m_w_uk, w_uv=m_w_uv, w_dq=m_w_dq, g_cq=m_g_cq, w_uq=m_w_uq, w_o=m_w_o, mlp_w_up=m_mlp_w_up, mlp_w_down=m_mlp_w_down)
    v_in = dict(w_ada_mix=v_w_ada_mix, b_ada_mix=v_b_ada_mix, w_ada_mlp=v_w_ada_mlp, b_ada_mlp=v_b_ada_mlp, g_pre_mix=v_g_pre_mix, g_post_mix=v_g_post_mix, g_pre_mlp=v_g_pre_mlp, g_post_mlp=v_g_post_mlp, conv_w_in=v_conv_w_in, conv_b_in=v_conv_b_in, conv_dw=v_conv_dw, conv_dw_b=v_conv_dw_b, conv_ln_g=v_conv_ln_g, conv_ln_b=v_conv_ln_b, conv_w_out=v_conv_w_out, conv_b_out=v_conv_b_out, w_ada_kv=v_w_ada_kv, b_ada_kv=v_b_ada_kv, g_kv=v_g_kv, w_dkv=v_w_dkv, g_ckv=v_g_ckv, w_kr=v_w_kr, w_uk=v_w_uk, w_uv=v_w_uv, w_dq=v_w_dq, g_cq=v_g_cq, w_uq=v_w_uq, w_o=v_w_o, mlp_w_up=v_mlp_w_up, mlp_w_down=v_mlp_w_down)
    layer1 = {}

    def early_updates(red):
        for n, key in (("mlp_w_up", "up1"), ("mlp_w_down", "down1")):
            layer1[n] = _adamw_layer(n, weights[n], red[key], m_in[n], v_in[n], 1)
        return [layer1[n][1] for n in layer1]

    grads, last_group = _step_grads(x, c, positions, loss_target, weights, early_updates)
    loss, grad_x = grads.pop("loss"), grads.pop("x")
    names = list(weights)
    upd = {}
    for n in names:
        if isinstance(grads.get(n), list):
            res = layer1.get(n) or _adamw_layer(n, weights[n], grads[n][1], m_in[n], v_in[n], 1)
            grads[n], *upd[n] = _adamw_layer(n, weights[n], grads[n][0], m_in[n], v_in[n], 0, prev=res)
        elif n in grads:
            upd[n] = _adamw(n, weights[n], grads[n], m_in[n], v_in[n])
    grads.update(last_group([upd[n][0] for n in ("w_ada_mix", "w_ada_mlp", "w_ada_kv", "mlp_w_down")]))
    upd.update({n: _adamw(n, weights[n], grads[n], m_in[n], v_in[n]) for n in names if n not in upd})
    return (loss, grad_x, *[grads[n] for n in names], *[upd[n][0] for n in names], *[upd[n][1] for n in names],
            *[upd[n][2] for n in names])


def _step_grads(x, c, positions, loss_target, w, early_updates):
    xi, yi, ci = lax.axis_index("x"), lax.axis_index("y"), lax.axis_index("c")
    chip = 2 * xi + yi
    dev = 2 * chip + ci
    place_idx = jnp.stack([ci, chip]).astype(jnp.int32)
    t, d = x.shape[1], x.shape[2]
    dl = d // N_CHIPS
    r_kv, r_q = w["w_dkv"].shape[1], w["w_dq"].shape[2]
    n_ada, n_kvada = w["w_ada_mix"].shape[2], w["w_ada_kv"].shape[1]
    heads_l = w["w_uq"].shape[2] // (QK_NOPE + QK_ROPE)
    assert dl == r_kv == r_q and dl % LANES == 0 and heads_l * N_CHIPS % 2 == 0

    def chip_cols(a, width):
        return lax.dynamic_slice_in_dim(a, chip * width, width, axis=a.ndim - 1)

    uq = w["w_uq"][0].reshape(r_q, heads_l, QK_NOPE + QK_ROPE)
    uq_nope, uq_rope = uq[:, :, :QK_NOPE].reshape(r_q, -1), uq[:, :, QK_NOPE:].reshape(r_q, -1)
    kr_pad = jnp.pad(w["w_kr"], ((0, 0), (0, LANES - QK_ROPE)))
    small = jnp.concatenate([w["w_dkv"], kr_pad, w["w_dq"][0], w["w_uk"], w["w_uv"], uq_nope, uq_rope], axis=1)
    widths = [r_kv, LANES, r_q, w["w_uk"].shape[1], w["w_uv"].shape[1], uq_nope.shape[1], uq_rope.shape[1]]
    so = [0]
    for wd in widths:
        so.append(so[-1] + wd)
    big = dict(conv_w_in=(w["conv_w_in"], 0), conv_w_out=(w["conv_w_out"], 0), w_o=(w["w_o"], 0),
               up0=(w["mlp_w_up"], 0), up1=(w["mlp_w_up"], 1), down0=(w["mlp_w_down"], 0),
               down1=(w["mlp_w_down"], 1), small=(small[None], 0))
    groups = dict(conv_in=["conv_w_in"], conv_out=["conv_w_out"], up0=["up0"], down0=["down0"],
                  mla=["small", "w_o"], up1=["up1"], down1=["down1"])
    gathered, pending = {}, {}

    def start_gathers(gnames, after):
        token = after
        for gname in gnames:
            slots = [_cast_into_slot(n, *big[n], place_idx, after=token) for n in groups[gname]]
            s_sems, r_sems, bufs, token = _gather_start("gather_start_" + gname, slots, after=token)
            pending[gname] = (s_sems, r_sems, bufs)
        return token

    forwarding = {}

    def early_forward(gname, after):
        s_sems, r_sems, bufs = pending[gname]
        bufs = _gather_wait("gather_wait_" + gname, s_sems, r_sems, bufs, after)
        forwarding[gname], token = _forward_start("forward_start_" + gname, bufs)
        return token

    def finish_gather(gname, after):
        if gname in forwarding:
            bufs = _forward_wait("forward_wait_" + gname, forwarding[gname], after)
        else:
            s_sems, r_sems, bufs = pending[gname]
            bufs = _gather_wait("gather_wait_" + gname, s_sems, r_sems, bufs, after)
            bufs = _forward_to_sibling("forward_" + gname, bufs)
        gathered.update(zip(groups[gname], bufs))

    rowl = lambda a: a.reshape(-1, a.shape[2])
    w_up_f = lambda l: gathered["up%d" % l]
    w_down_f = lambda l: rowl(gathered["down%d" % l])

    pack_rows = [w["conv_dw"][0], w["conv_dw_b"], w["conv_ln_g"], w["conv_ln_b"], w["conv_b_out"],
                 w["conv_b_in"].reshape(2, dl), c.reshape(N_CHIPS, dl)]
    pack_rows = [jnp.pad(p, ((0, (-p.shape[0]) % 8), (0, 0))) for p in pack_rows]
    po = [0]
    for p in pack_rows:
        po.append(po[-1] + p.shape[0])
    packs = _all_gather_small("gather_params", jnp.concatenate(pack_rows, axis=0))
    by_chip = packs[0::2]

    def full_width(r0, nr):
        return jnp.transpose(by_chip[:, r0:r0 + nr, :], (1, 0, 2)).reshape(nr, d)

    dw_full, dwb_full = full_width(po[0], CONV_WIDTH), full_width(po[1], 1)
    lng_full, lnb_full, bout_full = full_width(po[2], 1), full_width(po[3], 1), full_width(po[4], 1)
    bin_full = by_chip[:, po[5]:po[5] + 2, :].reshape(1, 2 * d)
    first_start = start_gathers(("conv_in",), packs)[0:1, 0:1]
    c_all = packs[:, po[6]:po[6] + N_CHIPS, :].reshape(N_DEV, d) + first_start

    sc_all = _ew("silu_c", lambda a: (a * jax.nn.sigmoid(a),), [c_all], (F32,))[0]
    ada_w = [(w["w_ada_mix"], 0), (w["w_ada_mix"], 1), (w["w_ada_mlp"], 0), (w["w_ada_mlp"], 1),
             (w["w_ada_kv"][None], 0)]
    ada_b = [chip_cols(w["b_ada_mix"][0:1], n_ada), chip_cols(w["b_ada_mix"][1:2], n_ada),
             chip_cols(w["b_ada_mlp"][0:1], n_ada), chip_cols(w["b_ada_mlp"][1:2], n_ada),
             chip_cols(w["b_ada_kv"].reshape(1, -1), n_kvada)]
    mods = [_mm_nn("ada_fwd_%d" % i, sc_all, wi, "row", extras=(jnp.broadcast_to(bi, (N_DEV, bi.shape[1])),),
                   epilogue=_add_epilogue, w_layer=li) for i, ((wi, li), bi) in enumerate(zip(ada_w, ada_b))]
    mods_all = _all_gather_small("gather_mods", jnp.concatenate(mods, axis=1))
    started = start_gathers(("conv_out", "up0", "down0", "mla", "up1", "down1"),
                            mods_all[0, 0:1, 0:1] + first_start)[0:1, 0:1]
    mine = lax.dynamic_index_in_dim(mods_all[0::2], dev, axis=1, keepdims=False)
    offs = [0]
    for m_ in mods:
        offs.append(offs[-1] + m_.shape[1])
    mod_vec = [mine[:, offs[i]:offs[i + 1]].reshape(1, -1) for i in range(5)]
    split3 = lambda v: (v[:, :d], v[:, d:2 * d], v[:, 2 * d:])
    mix = [split3(mod_vec[0]), split3(mod_vec[1])]
    mlp = [split3(mod_vec[2]), split3(mod_vec[3])]
    kv_shift, kv_scale = mod_vec[4][:, :d], mod_vec[4][:, d:]

    cos_t, sin_t = _rope_tables(positions, t)
    vec = lambda a: a.reshape(1, -1)
    gpm, gqm = w["g_pre_mix"], w["g_post_mix"]
    gpl, gql = w["g_pre_mlp"], w["g_post_mlp"]
    h0 = x[0]
    after_token = lambda v, token: v + token[0:1, 0:1]

    def mlp_fwd(l, h):
        sh, sc, _ = mlp[l]
        (hn,) = _rw_fwd("mlp_pre_%d" % l, _f_pre, [h], [vec(gpl[l]), sh, sc], (BF,))
        finish_gather("up%d" % l, hn)
        s = _mm_nn("mlp_up_%d" % l, hn, w_up_f(l), "col", o_dtypes=(BF,), epilogue=_relu2_epilogue, tiles=WIDE)
        finish_gather("down%d" % l, s)
        y = _mm_nn("mlp_down_%d" % l, s, w_down_f(l), "row", tiles=DEEP)
        return hn, s, y

    def post_fwd(name, h, y, gate, g, bias=None):
        if bias is None:
            return _rw_fwd(name, lambda h_, y_, gt, g_: (h_ + _f_post(y_, gt, g_)[0],), [h, y], [gate, g], (F32,))[0]
        return _rw_fwd(name, lambda h_, y_, b_, gt, g_: (h_ + _f_post_bias(y_, b_, gt, g_)[0],), [h, y],
                       [bias, gate, g], (F32,))[0]

    (hn0,) = _rw_fwd("conv_pre", _f_pre, [h0], [vec(gpm[0]), mix[0][0] + started, mix[0][1]], (BF,))
    finish_gather("conv_in", hn0)
    w_in_f = gathered["conv_w_in"]
    z0 = _mm_nn("conv_in", hn0, w_in_f, "col")
    token = early_forward("conv_out", z0)
    (u0,) = _rw_fwd("conv_glu", _f_glu, [z0], [after_token(bin_full, token)], (F32,))
    uc0 = _conv_fwd(u0, dw_full, dwb_full)
    (n0,) = _rw_fwd("conv_ln", _f_ln_silu, [uc0], [lng_full, lnb_full], (BF,))
    finish_gather("conv_out", n0)
    w_out_f = rowl(gathered["conv_w_out"])
    y0 = _mm_nn("conv_out", n0, w_out_f, "row")
    token = early_forward("up0", y0)
    h1 = post_fwd("conv_post", h0, y0, after_token(mix[0][2], token), vec(gqm[0]), bias=bout_full)
    hn1, s1, y1 = mlp_fwd(0, h1)
    h2 = post_fwd("mlp_post_0", h1, y1, mlp[0][2], vec(gql[0]))

    finish_gather("mla", h2)
    gs = gathered["small"]
    w_dkvkr = rowl(gs[:, :, so[0]:so[2]])
    w_dq_f = rowl(gs[:, :, so[2]:so[3]])
    w_uk_f, w_uv_f = gs[:, :, so[3]:so[4]], gs[:, :, so[4]:so[5]]
    w_qn_f, w_qr_f = gs[:, :, so[5]:so[6]], gs[:, :, so[6]:so[7]]
    w_o_f = rowl(gathered["w_o"])
    kvn, hn2 = _rw_fwd("mla_pre", _f_pre2, [h2], [vec(w["g_kv"]), kv_shift, kv_scale, vec(gpm[1]), mix[1][0], mix[1][1]],
                       (BF, BF))
    pre_kv = _mm_nn("mla_dkv", kvn, w_dkvkr, "row")
    pre_q = _mm_nn("mla_dq", hn2, w_dq_f, "row")
    f_lat = _make_f_lat(r_kv)
    c_kv, kr, c_q = _rw_fwd("mla_latent", f_lat, [pre_kv, pre_q, cos_t, sin_t], [vec(w["g_ckv"]), vec(w["g_cq"][0])],
                            (BF, BF, BF))
    kn = _mm_nn("mla_uk", c_kv, w_uk_f, "col", o_dtypes=(BF,))
    vv = _mm_nn("mla_uv", c_kv, w_uv_f, "col", o_dtypes=(BF,))
    qn = _mm_nn("mla_uq_nope", c_q, w_qn_f, "col", o_dtypes=(BF,))
    qr_pre = _mm_nn("mla_uq_rope", c_q, w_qr_f, "col")
    (qr,) = _rw_fwd("mla_q_rope", _f_rope, [qr_pre, cos_t, sin_t], [], (BF,))
    att = _attn_fwd(qn, qr, kn, kr, vv)
    token = early_forward("up1", att)
    y2 = _mm_nn("mla_o", att, w_o_f, "row")
    h3 = post_fwd("mla_post", h2, y2, after_token(mix[1][2], token), vec(gqm[1]))
    hn3, s3, y3 = mlp_fwd(1, h3)
    h4 = post_fwd("mlp_post_1", h3, y3, mlp[1][2], vec(gql[1]))

    dh4, loss_part = _loss_grad(h4, loss_target[0])

    gw = {}
    gvec = {}

    dw_mm = functools.partial(_mm_tn, o_dtype=BF)

    def mlp_bwd(l, h_in, hn, s, y, dh, token=None):
        sh, sc, gate = mlp[l]
        if token is not None:
            gate = after_token(gate, token)
        (dy,), (dgate, dgq) = _rw_bwd("mlp_post_bwd_%d" % l, _f_post, [y], [gate, vec(gql[l])], [dh], [BF])
        gw["down%d" % l] = dw_mm("mlp_down_dw_%d" % l, s, dy, "row", tiles=(512, 2048, 2048))
        da = _mm_nt("mlp_down_dx_%d" % l, dy, w_down_f(l), "row", o_dtypes=(BF,), extras=(s,),
                    epilogue=_relu2_bwd_epilogue, tiles=WIDE)
        gw["up%d" % l] = dw_mm("mlp_up_dw_%d" % l, hn, da, "col", tiles=WIDE)
        dhn = _mm_nt("mlp_up_dx_%d" % l, da, w_up_f(l), "col", tiles=DEEP)
        (dh_in,), (dgp, dsh, dsc) = _rw_bwd("mlp_pre_bwd_%d" % l, _f_pre, [h_in], [vec(gpl[l]), sh, sc], [dhn], [F32],
                                            add_rows={0: dh})
        gvec["mlp%d" % l] = (dsh, dsc, dgate)
        gvec["g_pre_mlp%d" % l], gvec["g_post_mlp%d" % l] = dgp, dgq
        return dh_in

    chunked = lambda a: a.reshape(N_CHIPS, a.shape[0] // N_CHIPS, a.shape[1])
    to_chunks = lambda a: a if a.ndim == 3 else chunked(a)
    red = {}

    def swap_begin(tag, names):
        return _swap_start("swap_start_" + tag, [to_chunks(gw[n]) for n in names])

    def reduce_begin(tag, names, after=None, swapped=None):
        state, token = _reduce_scatter_begin(tag, [to_chunks(gw[n]) for n in names], place_idx, after, swapped)
        return (names, state), token

    def reduce_end(handle, after):
        names, state = handle
        red.update(zip(names, _reduce_scatter_end(state, after)))

    dh3 = mlp_bwd(1, h3, hn3, s3, y3, dh4)
    sw1, token = swap_begin("r1", ["up1", "down1"])

    (dy2,), (dgate, dgq) = _rw_bwd("mla_post_bwd", _f_post, [y2], [after_token(mix[1][2], token), vec(gqm[1])],
                                   [dh3], [BF])
    gvec["g_post_mix1"] = dgq
    gw["w_o"] = dw_mm("mla_o_dw", att, dy2, "row")
    datt = _mm_nt("mla_o_dx", dy2, w_o_f, "row", o_dtypes=(BF,))
    rs1, token = reduce_begin("r1", ["up1", "down1"], after=datt, swapped=sw1)
    dqn, dqr, dkn, dvv, dkr = _attn_bwd(qn, qr, kn, kr + token[0:1, 0:1].astype(BF), vv, datt)
    (dqr_pre,), _ = _rw_bwd("mla_q_rope_bwd", _f_rope, [qr_pre, cos_t, sin_t], [], [dqr], [BF, None, None])
    g_qn = dw_mm("mla_uq_nope_dw", c_q, dqn, "col")
    g_qr = dw_mm("mla_uq_rope_dw", c_q, dqr_pre, "col")
    dc_q = _mm_nt("mla_uq_nope_dx", dqn, w_qn_f, "col")
    dc_q = _mm_nt("mla_uq_rope_dx", dqr_pre, w_qr_f, "col", extras=(dc_q,), epilogue=_add_epilogue)
    g_uk = dw_mm("mla_uk_dw", c_kv, dkn, "col")
    g_uv = dw_mm("mla_uv_dw", c_kv, dvv, "col")
    dc_kv = _mm_nt("mla_uk_dx", dkn, w_uk_f, "col")
    dc_kv = _mm_nt("mla_uv_dx", dvv, w_uv_f, "col", extras=(dc_kv,), epilogue=_add_epilogue)
    (dpre_kv, dpre_q), (dg_ckv, dg_cq) = _rw_bwd(
        "mla_latent_bwd", f_lat, [pre_kv, pre_q, cos_t, sin_t], [vec(w["g_ckv"]), vec(w["g_cq"][0])],
        [dc_kv, dkr, dc_q], [BF, BF, None, None])
    gvec["g_ckv"], gvec["g_cq"] = dg_ckv, dg_cq
    g_dkvkr = dw_mm("mla_dkv_dw", kvn, dpre_kv, "row")
    g_dq = dw_mm("mla_dq_dw", hn2, dpre_q, "row")
    dkvn = _mm_nt("mla_dkv_dx", dpre_kv, w_dkvkr, "row")
    dhn2 = _mm_nt("mla_dq_dx", dpre_q, w_dq_f, "row")
    (dh2,), (dg_kv, dkvsh, dkvsc, dgp, dsh, dsc) = _rw_bwd(
        "mla_pre_bwd", _f_pre2, [h2], [vec(w["g_kv"]), kv_shift, kv_scale, vec(gpm[1]), mix[1][0], mix[1][1]],
        [dkvn, dhn2], [F32], add_rows={0: dh3})
    gvec["mix1"] = (dsh, dsc, dgate)
    gvec["kv"] = (dkvsh, dkvsc)
    gvec["g_kv"], gvec["g_pre_mix1"] = dg_kv, dgp
    gw["small"] = jnp.concatenate([chunked(g_dkvkr), chunked(g_dq), g_uk, g_uv, g_qn, g_qr], axis=2)
    reduce_end(rs1, dh2)
    rs2, token = reduce_begin("r2", ["small", "w_o"])

    dh1 = mlp_bwd(0, h1, hn1, s1, y1, dh2, token=token)
    reduce_end(rs2, dh1)
    rs3, token = reduce_begin("r3", ["up0", "down0"])

    (dy0,), (dbout, dgate, dgq) = _rw_bwd("conv_post_bwd", _f_post_bias, [y0],
                                          [bout_full, after_token(mix[0][2], token), vec(gqm[0])], [dh1], [BF])
    gvec["g_post_mix0"] = dgq
    gw["conv_w_out"] = dw_mm("conv_out_dw", n0, dy0, "row")
    dn0 = _mm_nt("conv_out_dx", dy0, w_out_f, "row")
    (duc0,), (dlng, dlnb) = _rw_bwd("conv_ln_bwd", _f_ln_silu, [uc0], [lng_full, lnb_full], [dn0], [F32])
    du0, ddw = _conv_bwd(u0, duc0, dw_full)
    (dz0,), (dbin,) = _rw_bwd("conv_glu_bwd", _f_glu, [z0], [bin_full], [du0], [BF])
    gw["conv_w_in"] = dw_mm("conv_in_dw", hn0, dz0, "col")
    dhn0 = _mm_nt("conv_in_dx", dz0, w_in_f, "col")
    (dx,), (dgp, dsh, dsc) = _rw_bwd("conv_pre_bwd", _f_pre, [h0], [vec(gpm[0]), mix[0][0], mix[0][1]], [dhn0], [F32],
                                     add_rows={0: dh1})
    gvec["mix0"] = (dsh, dsc, dgate)
    gvec["g_pre_mix0"] = dgp
    reduce_end(rs3, [dx, gw["conv_w_in"], gw["conv_w_out"]] + early_updates(red))

    vec_list = [*gvec["mix0"], *gvec["mix1"], *gvec["mlp0"], *gvec["mlp1"], *gvec["kv"],
                gvec["g_pre_mix0"], gvec["g_pre_mix1"], gvec["g_post_mix0"], gvec["g_post_mix1"],
                gvec["g_pre_mlp0"], gvec["g_pre_mlp1"], gvec["g_post_mlp0"], gvec["g_post_mlp1"],
                gvec["g_kv"], gvec["g_ckv"], gvec["g_cq"], dbin, dlng, dlnb, dbout, ddw.reshape(1, -1),
                jnp.pad(loss_part, ((0, 0), (0, LANES - 1)))]
    vo = [0]
    for v_ in vec_list:
        vo.append(vo[-1] + v_.shape[1])
    vec_pad = (-vo[-1]) % (8 * LANES)
    n_vec = vo[-1] + vec_pad
    flat = jnp.concatenate(vec_list + [jnp.zeros((1, vec_pad), F32)], axis=1).reshape(8, n_vec // 8)
    all_vecs = _all_gather_small("gather_vector_grads", flat).reshape(N_DEV, 8, n_vec // 8)
    rs4, token = reduce_begin("r4", ["conv_w_in", "conv_w_out"], after=all_vecs)
    all_vecs = after_token(all_vecs, token)
    summed = _sum_devices(all_vecs).reshape(1, n_vec)
    per_dev = all_vecs.reshape(N_DEV, n_vec)
    seg = lambda a, i: a[:, vo[i]:vo[i + 1]]

    out = {"loss": seg(summed, 30)[0, 0], "x": dx.reshape(x.shape)}
    dm_mix = [jnp.concatenate([seg(per_dev, 3 * l + i) for i in range(3)], axis=1) for l in range(2)]
    dm_mlp = [jnp.concatenate([seg(per_dev, 6 + 3 * l + i) for i in range(3)], axis=1) for l in range(2)]
    dm_kv = jnp.concatenate([seg(per_dev, 12), seg(per_dev, 13)], axis=1)
    ada_dw = lambda name, dm, width: _mm_tn(name, sc_all, chip_cols(dm, width), "row")
    out["w_ada_mix"] = [ada_dw("ada_mix_dw_%d" % l, dm_mix[l], n_ada) for l in range(2)]
    out["w_ada_mlp"] = [ada_dw("ada_mlp_dw_%d" % l, dm_mlp[l], n_ada) for l in range(2)]
    out["w_ada_kv"] = ada_dw("ada_kv_dw", dm_kv, n_kvada)
    sum_seg = lambda i: seg(summed, i)
    out["b_ada_mix"] = jnp.concatenate([jnp.concatenate([sum_seg(3 * l + i) for i in range(3)], axis=1) for l in range(2)], axis=0)
    out["b_ada_mlp"] = jnp.concatenate([jnp.concatenate([sum_seg(6 + 3 * l + i) for i in range(3)], axis=1) for l in range(2)], axis=0)
    out["b_ada_kv"] = jnp.concatenate([sum_seg(12), sum_seg(13)], axis=1).reshape(-1)
    out["g_pre_mix"] = jnp.concatenate([sum_seg(14), sum_seg(15)], axis=0)
    out["g_post_mix"] = jnp.concatenate([sum_seg(16), sum_seg(17)], axis=0)
    out["g_pre_mlp"] = jnp.concatenate([sum_seg(18), sum_seg(19)], axis=0)
    out["g_post_mlp"] = jnp.concatenate([sum_seg(20), sum_seg(21)], axis=0)
    out["g_kv"] = sum_seg(22).reshape(-1)
    out["g_ckv"] = sum_seg(23).reshape(-1)
    out["g_cq"] = sum_seg(24)
    out["conv_b_in"] = chip_cols(sum_seg(25), 2 * dl)
    out["conv_ln_g"] = chip_cols(sum_seg(26), dl)
    out["conv_ln_b"] = chip_cols(sum_seg(27), dl)
    out["conv_b_out"] = chip_cols(sum_seg(28), dl)
    ddw_sum = chip_cols(sum_seg(29).reshape(CONV_WIDTH + 1, d), dl)
    out["conv_dw"] = ddw_sum[:CONV_WIDTH].reshape(1, CONV_WIDTH, dl)
    out["conv_dw_b"] = ddw_sum[CONV_WIDTH:]

    def last_group(after):
        reduce_end(rs4, after)
        return {"conv_w_in": red["conv_w_in"][None], "conv_w_out": red["conv_w_out"][None]}

    out["w_o"] = red["w_o"][None]
    out["mlp_w_up"] = [red["up0"], red["up1"]]
    out["mlp_w_down"] = [red["down0"], red["down1"]]
    rs = red["small"]
    piece = lambda i: rs[:, so[i]:so[i + 1]]
    out["w_dkv"] = piece(0)
    out["w_kr"] = piece(1)[:, :QK_ROPE]
    out["w_dq"] = piece(2)[None]
    out["w_uk"], out["w_uv"] = piece(3), piece(4)
    out["w_uq"] = jnp.concatenate([piece(5).reshape(r_q, heads_l, QK_NOPE), piece(6).reshape(r_q, heads_l, QK_ROPE)],
                                  axis=2).reshape(1, r_q, -1)
    return out, last_group
```

```python
import functools

import jax
import jax.numpy as jnp
from jax import lax
from jax.experimental import pallas as pl
from jax.experimental.pallas import tpu as pltpu

F32 = jnp.float32
BF = jnp.bfloat16
MXU_DTYPE = BF

EPS = 1e-6
NEG = -1e30
ROPE_THETA = 10000.0
QK_NOPE = 128
QK_ROPE = 64
V_HEAD = 128
CONV_WIDTH = 31
ADAM_LR, ADAM_B1, ADAM_B2, ADAM_EPS, ADAM_WD, ADAM_STEP = 0.001, 0.9, 0.999, 1e-08, 0.01, 10

N_CHIPS = 4
N_DEV = 8
LANES = 128
SUBLANES = 8
VMEM_LIMIT = 56 * 2 ** 20
BLOCK_BYTES = 2 ** 21
MESH = pl.DeviceIdType.MESH
ANY = pl.BlockSpec(memory_space=pl.ANY)


def _params(sem=None):
    return pltpu.CompilerParams(dimension_semantics=sem, vmem_limit_bytes=VMEM_LIMIT)


def _tile(n, pref, unit=LANES):
    if n <= pref:
        return n
    t = (pref // unit) * unit
    while t > unit and n % t:
        t -= unit
    assert n % t == 0, (n, pref)
    return t


def _dg(a, b, ca, cb):
    return lax.dot_general(a.astype(MXU_DTYPE), b.astype(MXU_DTYPE), (((ca,), (cb,)), ((), ())),
                           preferred_element_type=F32)


@jax.custom_vjp
def _dot_nn(a, b):
    return _dg(a, b, 1, 0)


def _dot_nn_fwd(a, b):
    return _dg(a, b, 1, 0), (a, b)


def _dot_nn_bwd(res, g):
    a, b = res
    return _dg(g, b, 1, 1).astype(a.dtype), _dg(a, g, 0, 0).astype(b.dtype)


_dot_nn.defvjp(_dot_nn_fwd, _dot_nn_bwd)


@jax.custom_vjp
def _dot_nt(a, b):
    return _dg(a, b, 1, 1)


def _dot_nt_fwd(a, b):
    return _dg(a, b, 1, 1), (a, b)


def _dot_nt_bwd(res, g):
    a, b = res
    return _dg(g, b, 1, 0).astype(a.dtype), _dg(g, a, 0, 0).astype(b.dtype)


_dot_nt.defvjp(_dot_nt_fwd, _dot_nt_bwd)


def _matmul(name, a, b, *, ca, cb, grid, a_blk, a_map, b_blk, b_map, o_shape, o_dtypes, o_blk, o_map,
            extras=(), epilogue=None):
    nk = grid[-1]
    n_ex, n_out = len(extras), len(o_dtypes)

    def body(*refs):
        a_ref, b_ref = refs[0], refs[1]
        ex_refs = refs[2:2 + n_ex]
        out_refs = refs[2 + n_ex:2 + n_ex + n_out]
        kk = pl.program_id(len(grid) - 1)

        def finish(acc):
            outs = (acc,) if epilogue is None else epilogue(acc, *[r[...] for r in ex_refs])
            for r, o in zip(out_refs, outs):
                r[...] = o.astype(r.dtype)

        part = _dg(a_ref[...], b_ref[...], ca, cb)
        if nk == 1:
            finish(part)
        else:
            acc_ref = refs[-1]

            @pl.when(kk == 0)
            def _():
                acc_ref[...] = part

            @pl.when(kk > 0)
            def _():
                acc_ref[...] += part

            @pl.when(kk == nk - 1)
            def _():
                finish(acc_ref[...])

    o_spec = pl.BlockSpec(o_blk, o_map)
    acc_shape = tuple(d for d in o_blk if d is not None)
    res = pl.pallas_call(
        body, grid=grid,
        in_specs=[pl.BlockSpec(a_blk, a_map), pl.BlockSpec(b_blk, b_map)] + [o_spec] * n_ex,
        out_specs=[o_spec] * n_out,
        out_shape=[jax.ShapeDtypeStruct(o_shape, dt) for dt in o_dtypes],
        scratch_shapes=[] if nk == 1 else [pltpu.VMEM(acc_shape, F32)],
        compiler_params=_params(("parallel",) * (len(grid) - 1) + ("arbitrary",)),
        name=name,
    )(a, b, *extras)
    return res[0] if n_out == 1 else res


TM, TN, TK = 2048, 512, 2048


def _mm_nn(name, a, w, layout, o_dtypes=(F32,), extras=(), epilogue=None, tiles=(TM, TN, TK), w_layer=None):
    m, k = a.shape
    tm, tk = _tile(m, tiles[0], 8), _tile(k, tiles[2])
    if layout == "row":
        n = w.shape[-1]
        tn = _tile(n, tiles[1])
        grid = (m // tm, n // tn, k // tk)
        b_blk, b_map = (tk, tn), (lambda i, j, kk: (kk, j))
        if w_layer is not None:
            b_blk, b_map = (None, tk, tn), (lambda i, j, kk: (w_layer, kk, j))
    else:
        nl = w.shape[2]
        n = nl * w.shape[0]
        tn = _tile(nl, tiles[1])
        nb = nl // tn
        grid = (m // tm, n // tn, k // tk)
        b_blk, b_map = (None, tk, tn), (lambda i, j, kk: (j // nb, kk, j % nb))
    return _matmul(name, a, w, ca=1, cb=0, grid=grid, a_blk=(tm, tk), a_map=lambda i, j, kk: (i, kk),
                   b_blk=b_blk, b_map=b_map, o_shape=(m, n), o_dtypes=o_dtypes, o_blk=(tm, tn),
                   o_map=lambda i, j, kk: (i, j), extras=extras, epilogue=epilogue)


def _mm_nt(name, g, w, layout, o_dtypes=(F32,), extras=(), epilogue=None, tiles=(TM, TN, TK)):
    m, n = g.shape
    tm, tn = _tile(m, tiles[0], 8), None
    if layout == "row":
        k = w.shape[0]
        tn = _tile(n, tiles[2])
        tk = _tile(k, tiles[1])
        grid = (m // tm, k // tk, n // tn)
        b_blk, b_map = (tk, tn), (lambda i, j, kk: (j, kk))
    else:
        k, nl = w.shape[1], w.shape[2]
        tn = _tile(nl, tiles[2])
        nb = nl // tn
        tk = _tile(k, tiles[1])
        grid = (m // tm, k // tk, n // tn)
        b_blk, b_map = (None, tk, tn), (lambda i, j, kk: (kk // nb, j, kk % nb))
    return _matmul(name, g, w, ca=1, cb=1, grid=grid, a_blk=(tm, tn), a_map=lambda i, j, kk: (i, kk),
                   b_blk=b_blk, b_map=b_map, o_shape=(m, k), o_dtypes=o_dtypes, o_blk=(tm, tk),
                   o_map=lambda i, j, kk: (i, j), extras=extras, epilogue=epilogue)


def _mm_tn(name, x, g, layout, o_dtype=F32, tiles=(TM, TN, TK)):
    t, k = x.shape
    n = g.shape[1]
    tt = _tile(t, tiles[2], 8)
    tk = _tile(k, tiles[0])
    if layout == "row":
        tn = _tile(n, tiles[1])
        o_shape, o_blk, o_map = (k, n), (tk, tn), (lambda i, j, kk: (i, j))
    else:
        nl = n // N_CHIPS
        tn = _tile(nl, tiles[1])
        nb = nl // tn
        o_shape, o_blk, o_map = (N_CHIPS, k, nl), (None, tk, tn), (lambda i, j, kk: (j // nb, i, j % nb))
    grid = (k // tk, n // tn, t // tt)
    return _matmul(name, x, g, ca=0, cb=0, grid=grid, a_blk=(tt, tk), a_map=lambda i, j, kk: (kk, i),
                   b_blk=(tt, tn), b_map=lambda i, j, kk: (kk, j), o_shape=o_shape, o_dtypes=(o_dtype,),
                   o_blk=o_blk, o_map=o_map)


TR = 256


def _rw_fwd(name, fn, rows, vecs, o_dtypes, tr=TR):
    t = rows[0].shape[0]
    tr = min(tr, t)
    n_r, n_v = len(rows), len(vecs)
    o_sds = jax.eval_shape(fn, *[jax.ShapeDtypeStruct((tr, r.shape[1]), r.dtype) for r in rows],
                           *[jax.ShapeDtypeStruct(v.shape, v.dtype) for v in vecs])

    def body(*refs):
        outs = fn(*[r[...] for r in refs[:n_r + n_v]])
        for r, o in zip(refs[n_r + n_v:], outs):
            r[...] = o.astype(r.dtype)

    res = pl.pallas_call(
        body, grid=(t // tr,),
        in_specs=[pl.BlockSpec((tr, r.shape[1]), lambda i: (i, 0)) for r in rows]
        + [pl.BlockSpec(v.shape, lambda i: (0, 0)) for v in vecs],
        out_specs=[pl.BlockSpec((tr, o.shape[1]), lambda i: (i, 0)) for o in o_sds],
        out_shape=[jax.ShapeDtypeStruct((t, o.shape[1]), dt) for o, dt in zip(o_sds, o_dtypes)],
        compiler_params=_params(("parallel",)), name=name,
    )(*rows, *vecs)
    return res


def _rw_bwd(name, fn, rows, vecs, cots, row_grad_dtypes, add_rows=None, tr=TR):
    t = rows[0].shape[0]
    tr = min(tr, t)
    add_rows = add_rows or {}
    n_r, n_v, n_c = len(rows), len(vecs), len(cots)
    diff = [i for i, dt in enumerate(row_grad_dtypes) if dt is not None]
    adds = [add_rows[i] for i in diff if i in add_rows]
    n_a = len(adds)

    def body(*refs):
        rvals = [r[...] for r in refs[:n_r]]
        vvals = [r[...] for r in refs[n_r:n_r + n_v]]
        c_refs = refs[n_r + n_v:n_r + n_v + n_c]
        a_refs = list(refs[n_r + n_v + n_c:n_r + n_v + n_c + n_a])
        o_refs = refs[n_r + n_v + n_c + n_a:]

        def f(*d):
            full = list(rvals)
            for i, x in zip(diff, d[:len(diff)]):
                full[i] = x
            return fn(*full, *d[len(diff):])

        outs, vjp = jax.vjp(f, *[rvals[i] for i in diff], *vvals)
        grads = vjp(tuple(c[...].astype(o.dtype) for c, o in zip(c_refs, outs)))
        for n, i in enumerate(diff):
            gr = grads[n].astype(F32)
            if i in add_rows:
                gr = gr + a_refs.pop(0)[...].astype(F32)
            o_refs[n][...] = gr.astype(o_refs[n].dtype)
        first = pl.program_id(0) == 0
        for r, gv in zip(o_refs[len(diff):], grads[len(diff):]):
            @pl.when(first)
            def _(r=r, gv=gv):
                r[...] = gv

            @pl.when(jnp.logical_not(first))
            def _(r=r, gv=gv):
                r[...] += gv

    row_spec = lambda a: pl.BlockSpec((tr, a.shape[1]), lambda i: (i, 0))
    vec_spec = lambda a: pl.BlockSpec(a.shape, lambda i: (0, 0))
    res = pl.pallas_call(
        body, grid=(t // tr,),
        in_specs=[row_spec(r) for r in rows] + [vec_spec(v) for v in vecs] + [row_spec(c) for c in cots]
        + [row_spec(a) for a in adds],
        out_specs=[row_spec(rows[i]) for i in diff] + [vec_spec(v) for v in vecs],
        out_shape=[jax.ShapeDtypeStruct(rows[i].shape, row_grad_dtypes[i]) for i in diff]
        + [jax.ShapeDtypeStruct(v.shape, F32) for v in vecs],
        compiler_params=_params(("arbitrary",)), name=name,
    )(*rows, *vecs, *cots, *adds)
    return res[:len(diff)], res[len(diff):]


def _rms(x, g):
    return x * lax.rsqrt(jnp.mean(x * x, axis=-1, keepdims=True) + EPS) * g


def _f_pre(h, g, shift, scale):
    return (_rms(h, g) * (1.0 + scale) + shift,)


def _f_pre2(h, g1, sh1, sc1, g2, sh2, sc2):
    return _f_pre(h, g1, sh1, sc1) + _f_pre(h, g2, sh2, sc2)


def _f_post(y, gate, g):
    return (gate * _rms(y, g),)


def _f_post_bias(y, bias, gate, g):
    return (gate * _rms(y + bias, g),)


def _f_glu(z, bias):
    z = z + bias
    half = z.shape[1] // 2
    return (z[:, :half] * jax.nn.sigmoid(z[:, half:]),)


def _f_ln_silu(u, g, b):
    mu = jnp.mean(u, axis=-1, keepdims=True)
    var = jnp.mean(jnp.square(u - mu), axis=-1, keepdims=True)
    y = (u - mu) * lax.rsqrt(var + EPS) * g + b
    return (y * jax.nn.sigmoid(y),)


def _rope_raw(x, cos, sin):
    n = x.shape[1]
    reps = n // LANES
    if reps > 1:
        cos = jnp.concatenate([cos] * reps, axis=1)
        sin = jnp.concatenate([sin] * reps, axis=1)
    lane = lax.broadcasted_iota(jnp.int32, x.shape, 1)
    half = QK_ROPE // 2
    partner = jnp.where((lane % QK_ROPE) < half, pltpu.roll(x, n - half, 1), pltpu.roll(x, half, 1))
    return x * cos + partner * sin


@jax.custom_vjp
def _rope(x, cos, sin):
    return _rope_raw(x, cos, sin)


def _rope_fwd(x, cos, sin):
    return _rope_raw(x, cos, sin), (cos, sin)


def _rope_bwd(res, g):
    cos, sin = res
    return _rope_raw(g, cos, -sin), jnp.zeros_like(cos), jnp.zeros_like(sin)


_rope.defvjp(_rope_fwd, _rope_bwd)


def _make_f_lat(r_kv):
    def f(pre_kv, pre_q, cos, sin, g_ckv, g_cq):
        c_kv = _rms(pre_kv[:, :r_kv], g_ckv)
        kr = _rope(pre_kv[:, r_kv:], cos, sin)
        return c_kv, kr, _rms(pre_q, g_cq)
    return f


def _f_rope(x, cos, sin):
    return (_rope(x, cos, sin),)


CONV_TC = 128
CONV_TT = 256
PADR = 32


def _conv_fwd(u, dw, dw_b):
    t, d = u.shape
    tc, tt = min(CONV_TC, d), min(CONV_TT, t)
    off = PADR - (CONV_WIDTH - 1)

    def body(u_ref, w_ref, b_ref, o_ref, pad_ref):
        pad_ref[pl.ds(0, PADR), :] = jnp.zeros((PADR, tc), F32)
        pad_ref[pl.ds(PADR, t), :] = u_ref[...]
        for t0 in range(0, t, tt):
            acc = jnp.zeros((tt, tc), F32) + b_ref[...]
            for j in range(CONV_WIDTH):
                acc = acc + pad_ref[pl.ds(t0 + off + j, tt), :] * w_ref[pl.ds(j, 1), :]
            o_ref[pl.ds(t0, tt), :] = acc

    return pl.pallas_call(
        body, grid=(d // tc,),
        in_specs=[pl.BlockSpec((t, tc), lambda j: (0, j)), pl.BlockSpec((CONV_WIDTH, tc), lambda j: (0, j)),
                  pl.BlockSpec((1, tc), lambda j: (0, j))],
        out_specs=pl.BlockSpec((t, tc), lambda j: (0, j)),
        out_shape=jax.ShapeDtypeStruct((t, d), F32),
        scratch_shapes=[pltpu.VMEM((t + PADR, tc), F32)],
        compiler_params=_params(("parallel",)), name="conv_fwd",
    )(u, dw, dw_b)


def _conv_bwd(u, duc, dw):
    t, d = u.shape
    tc, tt = min(CONV_TC, d), min(CONV_TT, t)
    off = PADR - (CONV_WIDTH - 1)

    def body(u_ref, g_ref, w_ref, du_ref, dwt_ref, padu_ref, padg_ref):
        padu_ref[pl.ds(0, PADR), :] = jnp.zeros((PADR, tc), F32)
        padu_ref[pl.ds(PADR, t), :] = u_ref[...]
        padg_ref[pl.ds(t, PADR), :] = jnp.zeros((PADR, tc), F32)
        padg_ref[pl.ds(0, t), :] = g_ref[...]
        for t0 in range(0, t, tt):
            acc = jnp.zeros((tt, tc), F32)
            for j in range(CONV_WIDTH):
                acc = acc + padg_ref[pl.ds(t0 + (CONV_WIDTH - 1) - j, tt), :] * w_ref[pl.ds(j, 1), :]
            du_ref[pl.ds(t0, tt), :] = acc
        for j in range(CONV_WIDTH):
            acc = jnp.zeros((tt, tc), F32)
            for t0 in range(0, t, tt):
                acc = acc + g_ref[pl.ds(t0, tt), :] * padu_ref[pl.ds(t0 + off + j, tt), :]
            dwt_ref[pl.ds(j, 1), :] = jnp.sum(acc, axis=0, keepdims=True)
        acc = jnp.zeros((tt, tc), F32)
        for t0 in range(0, t, tt):
            acc = acc + g_ref[pl.ds(t0, tt), :]
        dwt_ref[pl.ds(CONV_WIDTH, 1), :] = jnp.sum(acc, axis=0, keepdims=True)

    col = lambda r: pl.BlockSpec((r, tc), lambda j: (0, j))
    return pl.pallas_call(
        body, grid=(d // tc,),
        in_specs=[col(t), col(t), col(CONV_WIDTH)],
        out_specs=[col(t), col(CONV_WIDTH + 1)],
        out_shape=[jax.ShapeDtypeStruct((t, d), F32), jax.ShapeDtypeStruct((CONV_WIDTH + 1, d), F32)],
        scratch_shapes=[pltpu.VMEM((t + PADR, tc), F32), pltpu.VMEM((t + PADR, tc), F32)],
        compiler_params=_params(("parallel",)), name="conv_bwd",
    )(u, duc, dw)


@jax.custom_vjp
def _swap_halves(x):
    return pltpu.roll(x, LANES // 2, 1)


_swap_halves.defvjp(lambda x: (pltpu.roll(x, LANES // 2, 1), None), lambda _, g: (pltpu.roll(g, LANES // 2, 1),))


def _attn_block(qn, qr, kn, kr, v, q0):
    scale = (QK_NOPE + QK_ROPE) ** -0.5
    lane = lax.broadcasted_iota(jnp.int32, kr.shape, 1)
    kr_a = kr * (lane < QK_ROPE).astype(kr.dtype)
    kr_b = _swap_halves(kr_a)
    outs = []
    for hh, kr_h in ((0, kr_a), (1, kr_b)):
        sl = slice(hh * QK_NOPE, (hh + 1) * QK_NOPE)
        s = _dot_nt(jnp.concatenate([qn[:, sl], qr], axis=1), jnp.concatenate([kn[:, sl], kr_h], axis=1)) * scale
        diag = s[:, q0:]
        row = lax.broadcasted_iota(jnp.int32, diag.shape, 0)
        col = lax.broadcasted_iota(jnp.int32, diag.shape, 1)
        diag = jnp.where(col <= row, diag, NEG)
        s = diag if q0 == 0 else jnp.concatenate([s[:, :q0], diag], axis=1)
        e = jnp.exp(s - jnp.max(s, axis=-1, keepdims=True))
        p = e * (1.0 / jnp.sum(e, axis=-1, keepdims=True))
        outs.append(_dot_nn(p, v[:, sl]))
    return jnp.concatenate(outs, axis=1)


def _attn_fwd(qn, qr, kn, kr, v, tq=512):
    t, w = qn.shape
    pairs = w // (2 * QK_NOPE)
    tq = min(tq, t)
    pw = 2 * QK_NOPE

    def body(qn_ref, qr_ref, kn_ref, kr_ref, v_ref, o_ref):
        for q0 in range(0, t, tq):
            l = q0 + tq
            o_ref[pl.ds(q0, tq), :] = _attn_block(
                qn_ref[pl.ds(q0, tq), :], qr_ref[pl.ds(q0, tq), :], kn_ref[pl.ds(0, l), :], kr_ref[pl.ds(0, l), :],
                v_ref[pl.ds(0, l), :], q0).astype(o_ref.dtype)

    pair = lambda wd: pl.BlockSpec((t, wd), lambda p: (0, p))
    return pl.pallas_call(
        body, grid=(pairs,),
        in_specs=[pair(pw), pair(LANES), pair(pw), pl.BlockSpec((t, LANES), lambda p: (0, 0)), pair(pw)],
        out_specs=pair(pw), out_shape=jax.ShapeDtypeStruct((t, w), BF),
        compiler_params=_params(("parallel",)), name="attn_fwd",
    )(qn, qr, kn, kr, v)


def _attn_bwd(qn, qr, kn, kr, v, do, tq=256):
    t, w = qn.shape
    pairs = w // (2 * QK_NOPE)
    tq = min(tq, t)
    pw = 2 * QK_NOPE

    def body(qn_ref, qr_ref, kn_ref, kr_ref, v_ref, do_ref, dqn_ref, dqr_ref, dkn_ref, dv_ref, dkr_ref,
             akn_ref, av_ref, akr_ref):
        akn_ref[...] = jnp.zeros_like(akn_ref)
        av_ref[...] = jnp.zeros_like(av_ref)
        akr_ref[...] = jnp.zeros_like(akr_ref)
        for q0 in range(0, t, tq):
            l = q0 + tq
            rows, keys = pl.ds(q0, tq), pl.ds(0, l)
            _, vjp = jax.vjp(functools.partial(_attn_block, q0=q0), qn_ref[rows, :], qr_ref[rows, :],
                             kn_ref[keys, :], kr_ref[keys, :], v_ref[keys, :])
            dqn, dqr, dkn, dkr, dv = vjp(do_ref[rows, :].astype(F32))
            dqn_ref[rows, :] = dqn.astype(dqn_ref.dtype)
            dqr_ref[rows, :] = dqr.astype(dqr_ref.dtype)
            akn_ref[keys, :] += dkn.astype(F32)
            av_ref[keys, :] += dv.astype(F32)
            akr_ref[keys, :] += dkr.astype(F32)
        dkn_ref[...] = akn_ref[...].astype(dkn_ref.dtype)
        dv_ref[...] = av_ref[...].astype(dv_ref.dtype)
        first = pl.program_id(0) == 0

        @pl.when(first)
        def _():
            dkr_ref[...] = akr_ref[...]

        @pl.when(jnp.logical_not(first))
        def _():
            dkr_ref[...] += akr_ref[...]

    pair = lambda wd: pl.BlockSpec((t, wd), lambda p: (0, p))
    shared = pl.BlockSpec((t, LANES), lambda p: (0, 0))
    sds = jax.ShapeDtypeStruct
    return pl.pallas_call(
        body, grid=(pairs,),
        in_specs=[pair(pw), pair(LANES), pair(pw), shared, pair(pw), pair(pw)],
        out_specs=[pair(pw), pair(LANES), pair(pw), pair(pw), shared],
        out_shape=[sds((t, w), BF), sds((t, pairs * LANES), F32), sds((t, w), BF), sds((t, w), BF), sds((t, LANES), F32)],
        scratch_shapes=[pltpu.VMEM((t, pw), F32), pltpu.VMEM((t, pw), F32), pltpu.VMEM((t, LANES), F32)],
        compiler_params=_params(("arbitrary",)), name="attn_bwd",
    )(qn, qr, kn, kr, v, do)


def _ew(name, fn, ins, o_dtypes, max_bytes=BLOCK_BYTES):
    r, c = ins[0].shape
    tr = r
    if r * c * 4 > max_bytes:
        tr = max(16, (max_bytes // (c * 4)) // 16 * 16)
        while r % tr:
            tr -= 16
    n_in = len(ins)

    def body(*refs):
        outs = fn(*[x[...] for x in refs[:n_in]])
        for o_ref, o in zip(refs[n_in:], outs):
            o_ref[...] = o.astype(o_ref.dtype)

    spec = pl.BlockSpec((tr, c), lambda i: (i, 0))
    return pl.pallas_call(
        body, grid=(r // tr,), in_specs=[spec] * n_in, out_specs=[spec] * len(o_dtypes),
        out_shape=[jax.ShapeDtypeStruct((r, c), dt) for dt in o_dtypes],
        compiler_params=_params(("parallel",)), name=name,
    )(*ins)


def _adamw_math(w, g, m, v):
    m = ADAM_B1 * m + (1.0 - ADAM_B1) * g
    v = ADAM_B2 * v + (1.0 - ADAM_B2) * jnp.square(g)
    m_hat = m / (1.0 - ADAM_B1 ** ADAM_STEP)
    v_hat = v / (1.0 - ADAM_B2 ** ADAM_STEP)
    delta = -ADAM_LR * (m_hat / (jnp.sqrt(v_hat) + ADAM_EPS) + ADAM_WD * w)
    return delta, m, v


def _adamw(name, w, g, m, v):
    shape = w.shape
    to2 = lambda a: a.reshape(-1, shape[-1]) if a.ndim > 1 else a.reshape(1, -1)
    d, nm, nv = _ew("adamw_" + name, _adamw_math, [to2(w), to2(g), to2(m), to2(v)], (F32, F32, F32))
    return d.reshape(shape), nm.reshape(shape), nv.reshape(shape)


def _adamw_layer(name, w, g, m, v, layer, prev=()):
    _, r, n = w.shape
    tr = r
    while tr * n * 4 > BLOCK_BYTES and tr % 16 == 0:
        tr //= 2

    def body(*refs):
        w_ref, g_ref, m_ref, v_ref = refs[:4]
        og, od, om, ov = refs[-4:]
        gval = g_ref[...].astype(F32)
        d, nm, nv = _adamw_math(w_ref[...], gval, m_ref[...], v_ref[...])
        og[...] = gval
        od[...] = d
        om[...] = nm
        ov[...] = nv

    lay = pl.BlockSpec((None, tr, n), lambda i: (layer, i, 0))
    return pl.pallas_call(
        body, grid=(r // tr,),
        in_specs=[lay, pl.BlockSpec((tr, n), lambda i: (i, 0)), lay, lay] + [ANY] * len(prev),
        out_specs=[lay] * 4, out_shape=[jax.ShapeDtypeStruct(w.shape, F32)] * 4,
        input_output_aliases={4 + k: k for k in range(len(prev))},
        compiler_params=_params(("parallel",)), name="adamw_%s_%d" % (name, layer),
    )(w, g, m, v, *prev)


def _rope_tables(positions, t):
    half = QK_ROPE // 2
    inv = 1.0 / (ROPE_THETA ** (jnp.arange(0, QK_ROPE, 2, dtype=F32) / QK_ROPE))
    inv_l = jnp.tile(inv, LANES // half).reshape(1, LANES)
    sign = jnp.tile(jnp.concatenate([-jnp.ones((half,), F32), jnp.ones((half,), F32)]), LANES // QK_ROPE).reshape(1, LANES)

    def body(p_ref, inv_ref, sg_ref, c_ref, s_ref):
        ang = p_ref[...].astype(F32) * inv_ref[...]
        c_ref[...] = jnp.cos(ang)
        s_ref[...] = jnp.sin(ang) * sg_ref[...]

    return pl.pallas_call(body, out_shape=[jax.ShapeDtypeStruct((t, LANES), F32)] * 2, name="rope_tables")(
        positions.reshape(t, 1), inv_l, sign)


def _loss_grad(h, target):
    t, d = h.shape
    tr = min(TR, t)

    def body(h_ref, y_ref, dh_ref, l_ref):
        err = h_ref[...] - y_ref[...]
        dh_ref[...] = err * (1.0 / d)
        part = 0.5 * jnp.sum(jnp.mean(jnp.square(err), axis=-1, keepdims=True), axis=0, keepdims=True)
        first = pl.program_id(0) == 0

        @pl.when(first)
        def _():
            l_ref[...] = part

        @pl.when(jnp.logical_not(first))
        def _():
            l_ref[...] += part

    row = pl.BlockSpec((tr, d), lambda i: (i, 0))
    return pl.pallas_call(
        body, grid=(t // tr,), in_specs=[row, row],
        out_specs=[row, pl.BlockSpec((1, 1), lambda i: (0, 0))],
        out_shape=[jax.ShapeDtypeStruct((t, d), F32), jax.ShapeDtypeStruct((1, 1), F32)],
        compiler_params=_params(("arbitrary",)), name="loss_grad",
    )(h, target)


def _sum_devices(g):
    def fn(*parts):
        acc = parts[0]
        for p in parts[1:]:
            acc = acc + p
        return (acc,)
    return _ew("sum_devices", fn, [g[i] for i in range(g.shape[0])], (F32,))[0]


def _place():
    x, y, c = lax.axis_index("x"), lax.axis_index("y"), lax.axis_index("c")
    return x, y, c, [(1 - x, y), (x, 1 - y), (1 - x, 1 - y)]


def _all_gather_small(name, v):
    r, n = v.shape

    def body(x_ref, out_ref, send_sems, recv_sems, local_sem):
        x, y, c, chips = _place()
        me, sibling = (x, y, c), (x, y, 1 - c)

        def rows(px, py, pc):
            return out_ref.at[4 * px + 2 * py + pc]

        def copy(k, block, to, src=None):
            return pltpu.make_async_remote_copy(
                src_ref=rows(*block) if src is None else src, dst_ref=rows(*block), send_sem=send_sems.at[k],
                recv_sem=recv_sems.at[k], device_id=to, device_id_type=MESH)

        mine = pltpu.make_async_copy(x_ref, rows(*me), local_sem)
        mine.start()
        first = [copy(0, me, sibling, src=x_ref)]
        first += [copy(1 + j, me, (*chip, c), src=x_ref) for j, chip in enumerate(chips)]
        for cp in first:
            cp.start()
        passed = [copy(4 + j, (*chip, c), sibling) for j, chip in enumerate(chips)]
        for j, chip in enumerate(chips):
            copy(1 + j, (*chip, c), me).wait_recv()
            passed[j].start()
        copy(0, sibling, me).wait_recv()
        for j, chip in enumerate(chips):
            copy(4 + j, (*chip, 1 - c), me).wait_recv()
        for cp in first + passed:
            cp.wait_send()
        mine.wait()

    return pl.pallas_call(
        body, out_shape=jax.ShapeDtypeStruct((N_DEV, r, n), v.dtype),
        in_specs=[pl.BlockSpec(memory_space=pltpu.VMEM)], out_specs=pl.BlockSpec(memory_space=pltpu.VMEM),
        scratch_shapes=[pltpu.SemaphoreType.DMA((7,)), pltpu.SemaphoreType.DMA((7,)), pltpu.SemaphoreType.DMA],
        compiler_params=pltpu.CompilerParams(vmem_limit_bytes=VMEM_LIMIT), name=name,
    )(v)


def _cast_into_slot(name, w, layer, idx, after=None):
    _, r, n = w.shape
    tr = r
    while tr * n * 4 > BLOCK_BYTES and tr % 32 == 0:
        tr //= 2
    order = [] if after is None else [after]

    def body(idx_ref, w_ref, *rest):
        o_ref = rest[-1]
        o_ref[...] = w_ref[...].astype(o_ref.dtype)

    return pl.pallas_call(
        body,
        grid_spec=pltpu.PrefetchScalarGridSpec(
            num_scalar_prefetch=1, grid=(r // tr,),
            in_specs=[pl.BlockSpec((None, tr, n), lambda i, idx_ref: (layer, i, 0))] + [ANY] * len(order),
            out_specs=pl.BlockSpec((None, tr, n), lambda i, idx_ref: (idx_ref[1], i, 0))),
        out_shape=jax.ShapeDtypeStruct((N_CHIPS, r, n), BF),
        compiler_params=_params(("parallel",)), name="cast_" + name,
    )(idx, w, *order)


def _swap_other_half(name, grads, after=None):
    n_w = len(grads)
    order = [] if after is None else [after]
    n_in = n_w + len(order)

    def body(*refs):
        ins, outs = refs[:n_w], refs[n_in:n_in + n_w]
        send_sems, recv_sems = refs[n_in + n_w:]
        x, y, c, _ = _place()
        cps = []
        for i in range(n_w):
            h = ins[i].shape[1] // 2
            cp = pltpu.make_async_remote_copy(
                src_ref=ins[i].at[:, pl.ds((1 - c) * h, h), :], dst_ref=outs[i], send_sem=send_sems.at[i],
                recv_sem=recv_sems.at[i], device_id=(x, y, 1 - c), device_id_type=MESH)
            cp.start()
            cps.append(cp)
        for cp in cps:
            cp.wait()

    return pl.pallas_call(
        body, in_specs=[ANY] * n_in, out_specs=[ANY] * n_w,
        out_shape=[jax.ShapeDtypeStruct((g.shape[0], g.shape[1] // 2, g.shape[2]), g.dtype) for g in grads],
        scratch_shapes=[pltpu.SemaphoreType.DMA((n_w,)), pltpu.SemaphoreType.DMA((n_w,))],
        name=name,
    )(*grads, *order)


def _add_my_half(name, g, s, idx, o_dtype):
    nc, r, n = g.shape
    h = r // 2
    tr = h
    while tr * n * 4 > BLOCK_BYTES and tr % 32 == 0:
        tr //= 2
    nb = h // tr

    def body(idx_ref, g_ref, s_ref, o_ref, own_ref):
        val = (g_ref[...].astype(F32) + s_ref[...].astype(F32)).astype(o_ref.dtype)
        o_ref[...] = val

        @pl.when(pl.program_id(1) == idx_ref[1])
        def _():
            own_ref[...] = val

    return pl.pallas_call(
        body,
        grid_spec=pltpu.PrefetchScalarGridSpec(
            num_scalar_prefetch=1, grid=(nb, nc),
            in_specs=[pl.BlockSpec((None, tr, n), lambda i, j, idx_ref: (j, idx_ref[0] * nb + i, 0)),
                      pl.BlockSpec((None, tr, n), lambda i, j, idx_ref: (j, i, 0))],
            out_specs=[pl.BlockSpec((None, tr, n), lambda i, j, idx_ref: (j, i, 0)),
                       pl.BlockSpec((None, tr, n), lambda i, j, idx_ref: (idx_ref[1], i, 0))]),
        out_shape=[jax.ShapeDtypeStruct((nc, h, n), o_dtype)] * 2,
        compiler_params=_params(("parallel", "arbitrary")), name=name,
    )(idx, g, s)


def _sum_chips(name, q, idx):
    nc, h, n = q.shape
    tr = h
    while tr * n * 4 > BLOCK_BYTES // 2 and tr % 32 == 0:
        tr //= 2
    nb = h // tr

    def body(idx_ref, q_ref, o_ref):
        acc = q_ref[0].astype(F32)
        for j in range(1, nc):
            acc = acc + q_ref[j].astype(F32)
        o_ref[...] = acc

    return pl.pallas_call(
        body,
        grid_spec=pltpu.PrefetchScalarGridSpec(
            num_scalar_prefetch=1, grid=(nb,),
            in_specs=[pl.BlockSpec((nc, tr, n), lambda i, idx_ref: (0, i, 0))],
            out_specs=pl.BlockSpec((tr, n), lambda i, idx_ref: (idx_ref[0] * nb + i, 0))),
        out_shape=jax.ShapeDtypeStruct((2 * h, n), F32),
        compiler_params=_params(("parallel",)), name=name,
    )(idx, q)


def _join_halves(name, bufs):
    n_w = len(bufs)

    def body(*refs):
        outs = refs[n_w:2 * n_w]
        send_sems, recv_sems = refs[2 * n_w:]
        x, y, c, _ = _place()

        def copy(i, which):
            h = outs[i].shape[0] // 2
            rows = outs[i].at[pl.ds(which * h, h)]
            return pltpu.make_async_remote_copy(
                src_ref=rows, dst_ref=rows, send_sem=send_sems.at[i], recv_sem=recv_sems.at[i],
                device_id=(x, y, 1 - c), device_id_type=MESH)

        cps = [copy(i, c) for i in range(n_w)]
        for cp in cps:
            cp.start()
        for i, cp in enumerate(cps):
            cp.wait_send()
            copy(i, 1 - c).wait_recv()

    return pl.pallas_call(
        body, in_specs=[ANY] * n_w, out_specs=[ANY] * n_w,
        out_shape=[jax.ShapeDtypeStruct(b.shape, b.dtype) for b in bufs],
        input_output_aliases={i: i for i in range(n_w)},
        scratch_shapes=[pltpu.SemaphoreType.DMA((n_w,)), pltpu.SemaphoreType.DMA((n_w,))],
        name=name,
    )(*bufs)


HBM_SPEC = pl.BlockSpec(memory_space=pltpu.HBM)
SEM_SPEC = pl.BlockSpec(memory_space=pltpu.SEMAPHORE)
VMEM_SPEC = pl.BlockSpec(memory_space=pltpu.VMEM)
SIDE_EFFECT = pltpu.SideEffectType.DATAFLOW_SIDE_EFFECTING


def _in_hbm(arrays):
    return [pltpu.with_memory_space_constraint(a, pltpu.HBM) for a in arrays]


def _ici_gather_copy(ref, i, j, chip, send_sems, recv_sems):
    x, y, c, _ = _place()
    h = ref.shape[1] // 2
    mine = ref.at[2 * x + y, pl.ds(c * h, h)]
    return pltpu.make_async_remote_copy(
        src_ref=mine, dst_ref=mine, send_sem=send_sems.at[3 * i + j], recv_sem=recv_sems.at[3 * i + j],
        device_id=(*chip, c), device_id_type=MESH)


def _gather_start(name, bufs, after=None):
    n_w = len(bufs)
    order = [] if after is None else [after]

    def body(*refs):
        ins, token = refs[:n_w], refs[-1]
        send_sems, recv_sems = refs[n_w + len(order)], refs[n_w + len(order) + 1]
        chips = _place()[3]
        for i in range(n_w):
            for j, chip in enumerate(chips):
                _ici_gather_copy(ins[i], i, j, chip, send_sems, recv_sems).start()
        token[...] = jnp.zeros_like(token)

    res = pl.pallas_call(
        body, name=name, in_specs=[HBM_SPEC] * n_w + [ANY] * len(order),
        out_shape=(pltpu.SemaphoreType.DMA((3 * n_w,)), pltpu.SemaphoreType.DMA((3 * n_w,)),
                   *[pltpu.HBM(b.shape, b.dtype) for b in bufs], jax.ShapeDtypeStruct((SUBLANES, LANES), F32)),
        out_specs=(SEM_SPEC, SEM_SPEC, *[HBM_SPEC] * n_w, VMEM_SPEC),
        input_output_aliases={i: 2 + i for i in range(n_w)},
        compiler_params=pltpu.CompilerParams(has_side_effects=SIDE_EFFECT),
    )(*_in_hbm(bufs), *order)
    return res[0], res[1], list(res[2:2 + n_w]), res[-1]


def _gather_wait(name, send_sems, recv_sems, bufs, after):
    n_w = len(bufs)

    def body(*refs):
        ins, s_sems, r_sems = refs[:n_w], refs[n_w], refs[n_w + 1]
        chips = _place()[3]
        for i in range(n_w):
            for j, chip in enumerate(chips):
                cp = _ici_gather_copy(ins[i], i, j, chip, s_sems, r_sems)
                cp.wait_send()
                cp.wait_recv()

    return pl.pallas_call(
        body, name=name, in_specs=[HBM_SPEC] * n_w + [SEM_SPEC, SEM_SPEC, ANY],
        out_shape=[pltpu.HBM(b.shape, b.dtype) for b in bufs], out_specs=[HBM_SPEC] * n_w,
        input_output_aliases={i: i for i in range(n_w)},
        compiler_params=pltpu.CompilerParams(has_side_effects=SIDE_EFFECT),
    )(*bufs, send_sems, recv_sems, after)


def _forward_to_sibling(name, bufs):
    n_w = len(bufs)

    def body(*refs):
        outs = refs[n_w:2 * n_w]
        send_sems, recv_sems = refs[2 * n_w:]
        x, y, c, chips = _place()

        def copy(i, j, chip, which):
            h = outs[i].shape[1] // 2
            blk = outs[i].at[2 * chip[0] + chip[1], pl.ds(which * h, h)]
            return pltpu.make_async_remote_copy(
                src_ref=blk, dst_ref=blk, send_sem=send_sems.at[3 * i + j], recv_sem=recv_sems.at[3 * i + j],
                device_id=(x, y, 1 - c), device_id_type=MESH)

        sends = [copy(i, j, chip, c) for i in range(n_w) for j, chip in enumerate(chips)]
        for cp in sends:
            cp.start()
        for i in range(n_w):
            for j, chip in enumerate(chips):
                copy(i, j, chip, 1 - c).wait_recv()
        for cp in sends:
            cp.wait_send()

    return pl.pallas_call(
        body, in_specs=[ANY] * n_w, out_specs=[ANY] * n_w,
        out_shape=[jax.ShapeDtypeStruct(b.shape, b.dtype) for b in bufs],
        input_output_aliases={i: i for i in range(n_w)},
        scratch_shapes=[pltpu.SemaphoreType.DMA((3 * n_w,)), pltpu.SemaphoreType.DMA((3 * n_w,))],
        name=name,
    )(*bufs)


def _forward_copy(buf, i, j, chip, which, send_sems, recv_sems):
    x, y, c, _ = _place()
    h = buf.shape[1] // 2
    blk = buf.at[2 * chip[0] + chip[1], pl.ds(which * h, h)]
    return pltpu.make_async_remote_copy(
        src_ref=blk, dst_ref=blk, send_sem=send_sems.at[3 * i + j], recv_sem=recv_sems.at[3 * i + j],
        device_id=(x, y, 1 - c), device_id_type=MESH)


def _forward_start(name, bufs):
    n_w = len(bufs)

    def body(*refs):
        ins, send_sems, recv_sems, token = refs[:n_w], refs[n_w], refs[n_w + 1], refs[-1]
        x, y, c, chips = _place()
        for i in range(n_w):
            for j, chip in enumerate(chips):
                _forward_copy(ins[i], i, j, chip, c, send_sems, recv_sems).start()
        token[...] = jnp.zeros_like(token)

    res = pl.pallas_call(
        body, name=name, in_specs=[HBM_SPEC] * n_w,
        out_shape=(pltpu.SemaphoreType.DMA((3 * n_w,)), pltpu.SemaphoreType.DMA((3 * n_w,)),
                   *[pltpu.HBM(b.shape, b.dtype) for b in bufs], jax.ShapeDtypeStruct((SUBLANES, LANES), F32)),
        out_specs=(SEM_SPEC, SEM_SPEC, *[HBM_SPEC] * n_w, VMEM_SPEC),
        input_output_aliases={i: 2 + i for i in range(n_w)},
        compiler_params=pltpu.CompilerParams(has_side_effects=SIDE_EFFECT),
    )(*_in_hbm(bufs))
    return (res[0], res[1], list(res[2:2 + n_w])), res[-1]


def _forward_wait(name, state, after):
    send_sems, recv_sems, bufs = state
    n_w = len(bufs)

    def body(*refs):
        ins, s_sems, r_sems = refs[:n_w], refs[n_w], refs[n_w + 1]
        x, y, c, chips = _place()
        for i in range(n_w):
            for j, chip in enumerate(chips):
                _forward_copy(ins[i], i, j, chip, c, s_sems, r_sems).wait_send()
                _forward_copy(ins[i], i, j, chip, 1 - c, s_sems, r_sems).wait_recv()

    return pl.pallas_call(
        body, name=name, in_specs=[HBM_SPEC] * n_w + [SEM_SPEC, SEM_SPEC, ANY],
        out_shape=[pltpu.HBM(b.shape, b.dtype) for b in bufs], out_specs=[HBM_SPEC] * n_w,
        input_output_aliases={i: i for i in range(n_w)},
        compiler_params=pltpu.CompilerParams(has_side_effects=SIDE_EFFECT),
    )(*bufs, send_sems, recv_sems, after)


def _ici_scatter_copy(part, buf, i, j, chip, send_sems, recv_sems):
    x, y, c, _ = _place()
    return pltpu.make_async_remote_copy(
        src_ref=part.at[2 * chip[0] + chip[1]], dst_ref=buf.at[2 * x + y], send_sem=send_sems.at[3 * i + j],
        recv_sem=recv_sems.at[3 * i + j], device_id=(*chip, c), device_id_type=MESH)


def _ici_scatter_landing(buf, i, j, chip, send_sems, recv_sems):
    x, y, c, _ = _place()
    blk = buf.at[2 * chip[0] + chip[1]]
    return pltpu.make_async_remote_copy(
        src_ref=blk, dst_ref=blk, send_sem=send_sems.at[3 * i + j], recv_sem=recv_sems.at[3 * i + j],
        device_id=(*chip, c), device_id_type=MESH)


def _scatter_start(name, parts, bufs):
    n_w = len(parts)

    def body(*refs):
        ps, bs = refs[:n_w], refs[n_w:2 * n_w]
        send_sems, recv_sems, token = refs[2 * n_w], refs[2 * n_w + 1], refs[-1]
        chips = _place()[3]
        for i in range(n_w):
            for j, chip in enumerate(chips):
                _ici_scatter_copy(ps[i], bs[i], i, j, chip, send_sems, recv_sems).start()
        token[...] = jnp.zeros_like(token)

    both = list(parts) + list(bufs)
    res = pl.pallas_call(
        body, name=name, in_specs=[HBM_SPEC] * (2 * n_w),
        out_shape=(pltpu.SemaphoreType.DMA((3 * n_w,)), pltpu.SemaphoreType.DMA((3 * n_w,)),
                   *[pltpu.HBM(b.shape, b.dtype) for b in both], jax.ShapeDtypeStruct((SUBLANES, LANES), F32)),
        out_specs=(SEM_SPEC, SEM_SPEC, *[HBM_SPEC] * (2 * n_w), VMEM_SPEC),
        input_output_aliases={i: 2 + i for i in range(2 * n_w)},
        compiler_params=pltpu.CompilerParams(has_side_effects=SIDE_EFFECT),
    )(*_in_hbm(both))
    return res[0], res[1], list(res[2:2 + n_w]), list(res[2 + n_w:2 + 2 * n_w]), res[-1]


def _scatter_wait(name, send_sems, recv_sems, parts, bufs, after):
    n_w = len(parts)
    after = list(after) if isinstance(after, (list, tuple)) else [after]

    def body(*refs):
        ps, bs = refs[:n_w], refs[n_w:2 * n_w]
        s_sems, r_sems = refs[2 * n_w], refs[2 * n_w + 1]
        chips = _place()[3]
        for i in range(n_w):
            for j, chip in enumerate(chips):
                _ici_scatter_copy(ps[i], bs[i], i, j, chip, s_sems, r_sems).wait_send()
                _ici_scatter_landing(bs[i], i, j, chip, s_sems, r_sems).wait_recv()

    both = list(parts) + list(bufs)
    res = pl.pallas_call(
        body, name=name, in_specs=[HBM_SPEC] * (2 * n_w) + [SEM_SPEC, SEM_SPEC] + [ANY] * len(after),
        out_shape=[pltpu.HBM(b.shape, b.dtype) for b in both], out_specs=[HBM_SPEC] * (2 * n_w),
        input_output_aliases={i: i for i in range(2 * n_w)},
        compiler_params=pltpu.CompilerParams(has_side_effects=SIDE_EFFECT),
    )(*both, send_sems, recv_sems, *after)
    return list(res[n_w:])


def _swap_copy(g, land, i, send_sems, recv_sems):
    x, y, c, _ = _place()
    h = g.shape[1] // 2
    return pltpu.make_async_remote_copy(
        src_ref=g.at[:, pl.ds((1 - c) * h, h), :], dst_ref=land, send_sem=send_sems.at[i], recv_sem=recv_sems.at[i],
        device_id=(x, y, 1 - c), device_id_type=MESH)


def _swap_start(name, grads):
    n_w = len(grads)
    lands = [lax.empty((g.shape[0], g.shape[1] // 2, g.shape[2]), g.dtype) for g in grads]

    def body(*refs):
        gs, ls = refs[:n_w], refs[n_w:2 * n_w]
        send_sems, recv_sems, token = refs[2 * n_w], refs[2 * n_w + 1], refs[-1]
        for i in range(n_w):
            _swap_copy(gs[i], ls[i], i, send_sems, recv_sems).start()
        token[...] = jnp.zeros_like(token)

    both = list(grads) + lands
    res = pl.pallas_call(
        body, name=name, in_specs=[HBM_SPEC] * (2 * n_w),
        out_shape=(pltpu.SemaphoreType.DMA((n_w,)), pltpu.SemaphoreType.DMA((n_w,)),
                   *[pltpu.HBM(b.shape, b.dtype) for b in both], jax.ShapeDtypeStruct((SUBLANES, LANES), F32)),
        out_specs=(SEM_SPEC, SEM_SPEC, *[HBM_SPEC] * (2 * n_w), VMEM_SPEC),
        input_output_aliases={i: 2 + i for i in range(2 * n_w)},
        compiler_params=pltpu.CompilerParams(has_side_effects=SIDE_EFFECT),
    )(*_in_hbm(both))
    return (res[0], res[1], list(res[2:2 + n_w]), list(res[2 + n_w:2 + 2 * n_w])), res[-1]


def _swap_wait(name, swapped, after):
    send_sems, recv_sems, grads, lands = swapped
    n_w = len(grads)

    def body(*refs):
        gs, ls = refs[:n_w], refs[n_w:2 * n_w]
        s_sems, r_sems = refs[2 * n_w], refs[2 * n_w + 1]
        for i in range(n_w):
            cp = _swap_copy(gs[i], ls[i], i, s_sems, r_sems)
            cp.wait_send()
            cp.wait_recv()

    both = list(grads) + list(lands)
    res = pl.pallas_call(
        body, name=name, in_specs=[HBM_SPEC] * (2 * n_w) + [SEM_SPEC, SEM_SPEC, ANY],
        out_shape=[pltpu.HBM(b.shape, b.dtype) for b in both], out_specs=[HBM_SPEC] * (2 * n_w),
        input_output_aliases={i: i for i in range(2 * n_w)},
        compiler_params=pltpu.CompilerParams(has_side_effects=SIDE_EFFECT),
    )(*both, send_sems, recv_sems, after)
    return list(res[:n_w]), list(res[n_w:])


def _reduce_scatter_begin(tag, grads, idx, after=None, swapped=None):
    if swapped is None:
        from_sibling = _swap_other_half("swap_other_half_" + tag, grads, after)
    else:
        grads, from_sibling = _swap_wait("swap_wait_" + tag, swapped, after)
    pairs = [_add_my_half("add_my_half_%s_%d" % (tag, i), g, s, idx, BF)
             for i, (g, s) in enumerate(zip(grads, from_sibling))]
    s_sems, r_sems, parts, bufs, token = _scatter_start("scatter_start_" + tag, [p[0] for p in pairs],
                                                        [p[1] for p in pairs])
    return (tag, s_sems, r_sems, parts, bufs, idx), token


def _reduce_scatter_end(state, after):
    tag, s_sems, r_sems, parts, bufs, idx = state
    received = _scatter_wait("scatter_wait_" + tag, s_sems, r_sems, parts, bufs, after)
    halves = [_sum_chips("sum_chips_%s_%d" % (tag, i), q, idx) for i, q in enumerate(received)]
    return _join_halves("join_halves_" + tag, halves)


def _relu2_epilogue(acc):
    r = jnp.maximum(acc, 0.0)
    return (r * r,)


def _relu2_bwd_epilogue(acc, s):
    return (acc * (2.0 * jnp.sqrt(s.astype(F32))),)


WIDE = (2048, 512, 2048)
DEEP = (2048, 512, 2048)


def _add_epilogue(acc, other):
    return (acc + other,)


def kernel(x, c, positions, w_ada_mix, b_ada_mix, w_ada_mlp, b_ada_mlp, g_pre_mix, g_post_mix, g_pre_mlp, g_post_mlp, conv_w_in, conv_b_in, conv_dw, conv_dw_b, conv_ln_g, conv_ln_b, conv_w_out, conv_b_out, w_ada_kv, b_ada_kv, g_kv, w_dkv, g_ckv, w_kr, w_uk, w_uv, w_dq, g_cq, w_uq, w_o, mlp_w_up, mlp_w_down, loss_target, m_w_ada_mix, m_b_ada_mix, m_w_ada_mlp, m_b_ada_mlp, m_g_pre_mix, m_g_post_mix, m_g_pre_mlp, m_g_post_mlp, m_conv_w_in, m_conv_b_in, m_conv_dw, m_conv_dw_b, m_conv_ln_g, m_conv_ln_b, m_conv_w_out, m_conv_b_out, m_w_ada_kv, m_b_ada_kv, m_g_kv, m_w_dkv, m_g_ckv, m_w_kr, m_w_uk, m_w_uv, m_w_dq, m_g_cq, m_w_uq, m_w_o, m_mlp_w_up, m_mlp_w_down, v_w_ada_mix, v_b_ada_mix, v_w_ada_mlp, v_b_ada_mlp, v_g_pre_mix, v_g_post_mix, v_g_pre_mlp, v_g_post_mlp, v_conv_w_in, v_conv_b_in, v_conv_dw, v_conv_dw_b, v_conv_ln_g, v_conv_ln_b, v_conv_w_out, v_conv_b_out, v_w_ada_kv, v_b_ada_kv, v_g_kv, v_w_dkv, v_g_ckv, v_w_kr, v_w_uk, v_w_uv, v_w_dq, v_g_cq, v_w_uq, v_w_o, v_mlp_w_up, v_mlp_w_down):
    weights = dict(w_ada_mix=w_ada_mix, b_ada_mix=b_ada_mix, w_ada_mlp=w_ada_mlp, b_ada_mlp=b_ada_mlp, g_pre_mix=g_pre_mix, g_post_mix=g_post_mix, g_pre_mlp=g_pre_mlp, g_post_mlp=g_post_mlp, conv_w_in=conv_w_in, conv_b_in=conv_b_in, conv_dw=conv_dw, conv_dw_b=conv_dw_b, conv_ln_g=conv_ln_g, conv_ln_b=conv_ln_b, conv_w_out=conv_w_out, conv_b_out=conv_b_out, w_ada_kv=w_ada_kv, b_ada_kv=b_ada_kv, g_kv=g_kv, w_dkv=w_dkv, g_ckv=g_ckv, w_kr=w_kr, w_uk=w_uk, w_uv=w_uv, w_dq=w_dq, g_cq=g_cq, w_uq=w_uq, w_o=w_o, mlp_w_up=mlp_w_up, mlp_w_down=mlp_w_down)
    m_in = dict(w_ada_mix=m_w_ada_mix, b_ada_mix=m_b_ada_mix, w_ada_mlp=m_w_ada_mlp, b_ada_mlp=m_b_ada_mlp, g_pre_mix=m_g_pre_mix, g_post_mix=m_g_post_mix, g_pre_mlp=m_g_pre_mlp, g_post_mlp=m_g_post_mlp, conv_w_in=m_conv_w_in, conv_b_in=m_conv_b_in, conv_dw=m_conv_dw, conv_dw_b=m_conv_dw_b, conv_ln_g=m_conv_ln_g, conv_ln_b=m_conv_ln_b, conv_w_out=m_conv_w_out, conv_b_out=m_conv_b_out, w_ada_kv=m_w_ada_kv, b_ada_kv=m_b_ada_kv, g_kv=m_g_kv, w_dkv=m_w_dkv, g_ckv=m_g_ckv, w_kr=m_w_kr, w_uk=m_w_uk, w_uv=m_w_uv, w_dq=m_w_dq, g_cq=m_g_cq, w_uq=m_w_uq, w_o=m_w_o, mlp_w_up=m_mlp_w_up, mlp_w_down=m_mlp_w_down)
    v_in = dict(w_ada_mix=v_w_ada_mix, b_ada_mix=v_b_ada_mix, w_ada_mlp=v_w_ada_mlp, b_ada_mlp=v_b_ada_mlp, g_pre_mix=v_g_pre_mix, g_post_mix=v_g_post_mix, g_pre_mlp=v_g_pre_mlp, g_post_mlp=v_g_post_mlp, conv_w_in=v_conv_w_in, conv_b_in=v_conv_b_in, conv_dw=v_conv_dw, conv_dw_b=v_conv_dw_b, conv_ln_g=v_conv_ln_g, conv_ln_b=v_conv_ln_b, conv_w_out=v_conv_w_out, conv_b_out=v_conv_b_out, w_ada_kv=v_w_ada_kv, b_ada_kv=v_b_ada_kv, g_kv=v_g_kv, w_dkv=v_w_dkv, g_ckv=v_g_ckv, w_kr=v_w_kr, w_uk=v_w_uk, w_uv=v_w_uv, w_dq=v_w_dq, g_cq=v_g_cq, w_uq=v_w_uq, w_o=v_w_o, mlp_w_up=v_mlp_w_up, mlp_w_down=v_mlp_w_down)
    layer1 = {}

    def early_updates(red):
        for n, key in (("mlp_w_up", "up1"), ("mlp_w_down", "down1")):
            layer1[n] = _adamw_layer(n, weights[n], red[key], m_in[n], v_in[n], 1)
        return [layer1[n][1] for n in layer1]

    grads, last_group = _step_grads(x, c, positions, loss_target, weights, early_updates)
    loss, grad_x = grads.pop("loss"), grads.pop("x")
    names = list(weights)
    upd = {}
    for n in names:
        if isinstance(grads.get(n), list):
            res = layer1.get(n) or _adamw_layer(n, weights[n], grads[n][1], m_in[n], v_in[n], 1)
            grads[n], *upd[n] = _adamw_layer(n, weights[n], grads[n][0], m_in[n], v_in[n], 0, prev=res)
        elif n in grads:
            upd[n] = _adamw(n, weights[n], grads[n], m_in[n], v_in[n])
    grads.update(last_group([upd[n][0] for n in ("w_ada_mix", "w_ada_mlp", "w_ada_kv", "mlp_w_down")]))
    upd.update({n: _adamw(n, weights[n], grads[n], m_in[n], v_in[n]) for n in names if n not in upd})
    return (loss, grad_x, *[grads[n] for n in names], *[upd[n][0] for n in names], *[upd[n][1] for n in names],
            *[upd[n][2] for n in names])


def _step_grads(x, c, positions, loss_target, w, early_updates):
    xi, yi, ci = lax.axis_index("x"), lax.axis_index("y"), lax.axis_index("c")
    chip = 2 * xi + yi
    dev = 2 * chip + ci
    place_idx = jnp.stack([ci, chip]).astype(jnp.int32)
    t, d = x.shape[1], x.shape[2]
    dl = d // N_CHIPS
    r_kv, r_q = w["w_dkv"].shape[1], w["w_dq"].shape[2]
    n_ada, n_kvada = w["w_ada_mix"].shape[2], w["w_ada_kv"].shape[1]
    heads_l = w["w_uq"].shape[2] // (QK_NOPE + QK_ROPE)
    assert dl == r_kv == r_q and dl % LANES == 0 and heads_l * N_CHIPS % 2 == 0

    def chip_cols(a, width):
        return lax.dynamic_slice_in_dim(a, chip * width, width, axis=a.ndim - 1)

    uq = w["w_uq"][0].reshape(r_q, heads_l, QK_NOPE + QK_ROPE)
    uq_nope, uq_rope = uq[:, :, :QK_NOPE].reshape(r_q, -1), uq[:, :, QK_NOPE:].reshape(r_q, -1)
    kr_pad = jnp.pad(w["w_kr"], ((0, 0), (0, LANES - QK_ROPE)))
    small = jnp.concatenate([w["w_dkv"], kr_pad, w["w_dq"][0], w["w_uk"], w["w_uv"], uq_nope, uq_rope], axis=1)
    widths = [r_kv, LANES, r_q, w["w_uk"].shape[1], w["w_uv"].shape[1], uq_nope.shape[1], uq_rope.shape[1]]
    so = [0]
    for wd in widths:
        so.append(so[-1] + wd)
    big = dict(conv_w_in=(w["conv_w_in"], 0), conv_w_out=(w["conv_w_out"], 0), w_o=(w["w_o"], 0),
               up0=(w["mlp_w_up"], 0), up1=(w["mlp_w_up"], 1), down0=(w["mlp_w_down"], 0),
               down1=(w["mlp_w_down"], 1), small=(small[None], 0))
    groups = dict(conv_in=["conv_w_in"], conv_out=["conv_w_out"], up0=["up0"], down0=["down0"],
                  mla=["small", "w_o"], up1=["up1"], down1=["down1"])
    gathered, pending = {}, {}

    def start_gathers(gnames, after):
        token = after
        for gname in gnames:
            slots = [_cast_into_slot(n, *big[n], place_idx, after=token) for n in groups[gname]]
            s_sems, r_sems, bufs, token = _gather_start("gather_start_" + gname, slots, after=token)
            pending[gname] = (s_sems, r_sems, bufs)
        return token

    forwarding = {}

    def early_forward(gname, after):
        s_sems, r_sems, bufs = pending[gname]
        bufs = _gather_wait("gather_wait_" + gname, s_sems, r_sems, bufs, after)
        forwarding[gname], token = _forward_start("forward_start_" + gname, bufs)
        return token

    def finish_gather(gname, after):
        if gname in forwarding:
            bufs = _forward_wait("forward_wait_" + gname, forwarding[gname], after)
        else:
            s_sems, r_sems, bufs = pending[gname]
            bufs = _gather_wait("gather_wait_" + gname, s_sems, r_sems, bufs, after)
            bufs = _forward_to_sibling("forward_" + gname, bufs)
        gathered.update(zip(groups[gname], bufs))

    rowl = lambda a: a.reshape(-1, a.shape[2])
    w_up_f = lambda l: gathered["up%d" % l]
    w_down_f = lambda l: rowl(gathered["down%d" % l])

    pack_rows = [w["conv_dw"][0], w["conv_dw_b"], w["conv_ln_g"], w["conv_ln_b"], w["conv_b_out"],
                 w["conv_b_in"].reshape(2, dl), c.reshape(N_CHIPS, dl)]
    pack_rows = [jnp.pad(p, ((0, (-p.shape[0]) % 8), (0, 0))) for p in pack_rows]
    po = [0]
    for p in pack_rows:
        po.append(po[-1] + p.shape[0])
    packs = _all_gather_small("gather_params", jnp.concatenate(pack_rows, axis=0))
    by_chip = packs[0::2]

    def full_width(r0, nr):
        return jnp.transpose(by_chip[:, r0:r0 + nr, :], (1, 0, 2)).reshape(nr, d)

    dw_full, dwb_full = full_width(po[0], CONV_WIDTH), full_width(po[1], 1)
    lng_full, lnb_full, bout_full = full_width(po[2], 1), full_width(po[3], 1), full_width(po[4], 1)
    bin_full = by_chip[:, po[5]:po[5] + 2, :].reshape(1, 2 * d)
    first_start = start_gathers(("conv_in",), packs)[0:1, 0:1]
    c_all = packs[:, po[6]:po[6] + N_CHIPS, :].reshape(N_DEV, d) + first_start

    sc_all = _ew("silu_c", lambda a: (a * jax.nn.sigmoid(a),), [c_all], (F32,))[0]
    ada_w = [(w["w_ada_mix"], 0), (w["w_ada_mix"], 1), (w["w_ada_mlp"], 0), (w["w_ada_mlp"], 1),
             (w["w_ada_kv"][None], 0)]
    ada_b = [chip_cols(w["b_ada_mix"][0:1], n_ada), chip_cols(w["b_ada_mix"][1:2], n_ada),
             chip_cols(w["b_ada_mlp"][0:1], n_ada), chip_cols(w["b_ada_mlp"][1:2], n_ada),
             chip_cols(w["b_ada_kv"].reshape(1, -1), n_kvada)]
    mods = [_mm_nn("ada_fwd_%d" % i, sc_all, wi, "row", extras=(jnp.broadcast_to(bi, (N_DEV, bi.shape[1])),),
                   epilogue=_add_epilogue, w_layer=li) for i, ((wi, li), bi) in enumerate(zip(ada_w, ada_b))]
    mods_all = _all_gather_small("gather_mods", jnp.concatenate(mods, axis=1))
    started = start_gathers(("conv_out", "up0", "down0", "mla", "up1", "down1"),
                            mods_all[0, 0:1, 0:1] + first_start)[0:1, 0:1]
    mine = lax.dynamic_index_in_dim(mods_all[0::2], dev, axis=1, keepdims=False)
    offs = [0]
    for m_ in mods:
        offs.append(offs[-1] + m_.shape[1])
    mod_vec = [mine[:, offs[i]:offs[i + 1]].reshape(1, -1) for i in range(5)]
    split3 = lambda v: (v[:, :d], v[:, d:2 * d], v[:, 2 * d:])
    mix = [split3(mod_vec[0]), split3(mod_vec[1])]
    mlp = [split3(mod_vec[2]), split3(mod_vec[3])]
    kv_shift, kv_scale = mod_vec[4][:, :d], mod_vec[4][:, d:]

    cos_t, sin_t = _rope_tables(positions, t)
    vec = lambda a: a.reshape(1, -1)
    gpm, gqm = w["g_pre_mix"], w["g_post_mix"]
    gpl, gql = w["g_pre_mlp"], w["g_post_mlp"]
    h0 = x[0]
    after_token = lambda v, token: v + token[0:1, 0:1]

    def mlp_fwd(l, h):
        sh, sc, _ = mlp[l]
        (hn,) = _rw_fwd("mlp_pre_%d" % l, _f_pre, [h], [vec(gpl[l]), sh, sc], (BF,))
        finish_gather("up%d" % l, hn)
        s = _mm_nn("mlp_up_%d" % l, hn, w_up_f(l), "col", o_dtypes=(BF,), epilogue=_relu2_epilogue, tiles=WIDE)
        finish_gather("down%d" % l, s)
        y = _mm_nn("mlp_down_%d" % l, s, w_down_f(l), "row", tiles=DEEP)
        return hn, s, y

    def post_fwd(name, h, y, gate, g, bias=None):
        if bias is None:
            return _rw_fwd(name, lambda h_, y_, gt, g_: (h_ + _f_post(y_, gt, g_)[0],), [h, y], [gate, g], (F32,))[0]
        return _rw_fwd(name, lambda h_, y_, b_, gt, g_: (h_ + _f_post_bias(y_, b_, gt, g_)[0],), [h, y],
                       [bias, gate, g], (F32,))[0]

    (hn0,) = _rw_fwd("conv_pre", _f_pre, [h0], [vec(gpm[0]), mix[0][0] + started, mix[0][1]], (BF,))
    finish_gather("conv_in", hn0)
    w_in_f = gathered["conv_w_in"]
    z0 = _mm_nn("conv_in", hn0, w_in_f, "col")
    token = early_forward("conv_out", z0)
    (u0,) = _rw_fwd("conv_glu", _f_glu, [z0], [after_token(bin_full, token)], (F32,))
    uc0 = _conv_fwd(u0, dw_full, dwb_full)
    (n0,) = _rw_fwd("conv_ln", _f_ln_silu, [uc0], [lng_full, lnb_full], (BF,))
    finish_gather("conv_out", n0)
    w_out_f = rowl(gathered["conv_w_out"])
    y0 = _mm_nn("conv_out", n0, w_out_f, "row")
    token = early_forward("up0", y0)
    h1 = post_fwd("conv_post", h0, y0, after_token(mix[0][2], token), vec(gqm[0]), bias=bout_full)
    hn1, s1, y1 = mlp_fwd(0, h1)
    h2 = post_fwd("mlp_post_0", h1, y1, mlp[0][2], vec(gql[0]))

    finish_gather("mla", h2)
    gs = gathered["small"]
    w_dkvkr = rowl(gs[:, :, so[0]:so[2]])
    w_dq_f = rowl(gs[:, :, so[2]:so[3]])
    w_uk_f, w_uv_f = gs[:, :, so[3]:so[4]], gs[:, :, so[4]:so[5]]
    w_qn_f, w_qr_f = gs[:, :, so[5]:so[6]], gs[:, :, so[6]:so[7]]
    w_o_f = rowl(gathered["w_o"])
    kvn, hn2 = _rw_fwd("mla_pre", _f_pre2, [h2], [vec(w["g_kv"]), kv_shift, kv_scale, vec(gpm[1]), mix[1][0], mix[1][1]],
                       (BF, BF))
    pre_kv = _mm_nn("mla_dkv", kvn, w_dkvkr, "row")
    pre_q = _mm_nn("mla_dq", hn2, w_dq_f, "row")
    f_lat = _make_f_lat(r_kv)
    c_kv, kr, c_q = _rw_fwd("mla_latent", f_lat, [pre_kv, pre_q, cos_t, sin_t], [vec(w["g_ckv"]), vec(w["g_cq"][0])],
                            (BF, BF, BF))
    kn = _mm_nn("mla_uk", c_kv, w_uk_f, "col", o_dtypes=(BF,))
    vv = _mm_nn("mla_uv", c_kv, w_uv_f, "col", o_dtypes=(BF,))
    qn = _mm_nn("mla_uq_nope", c_q, w_qn_f, "col", o_dtypes=(BF,))
    qr_pre = _mm_nn("mla_uq_rope", c_q, w_qr_f, "col")
    (qr,) = _rw_fwd("mla_q_rope", _f_rope, [qr_pre, cos_t, sin_t], [], (BF,))
    att = _attn_fwd(qn, qr, kn, kr, vv)
    token = early_forward("up1", att)
    y2 = _mm_nn("mla_o", att, w_o_f, "row")
    h3 = post_fwd("mla_post", h2, y2, after_token(mix[1][2], token), vec(gqm[1]))
    hn3, s3, y3 = mlp_fwd(1, h3)
    h4 = post_fwd("mlp_post_1", h3, y3, mlp[1][2], vec(gql[1]))

    dh4, loss_part = _loss_grad(h4, loss_target[0])

    gw = {}
    gvec = {}

    dw_mm = functools.partial(_mm_tn, o_dtype=BF)

    def mlp_bwd(l, h_in, hn, s, y, dh, token=None):
        sh, sc, gate = mlp[l]
        if token is not None:
            gate = after_token(gate, token)
        (dy,), (dgate, dgq) = _rw_bwd("mlp_post_bwd_%d" % l, _f_post, [y], [gate, vec(gql[l])], [dh], [BF])
        gw["down%d" % l] = dw_mm("mlp_down_dw_%d" % l, s, dy, "row", tiles=(512, 2048, 2048))
        da = _mm_nt("mlp_down_dx_%d" % l, dy, w_down_f(l), "row", o_dtypes=(BF,), extras=(s,),
                    epilogue=_relu2_bwd_epilogue, tiles=WIDE)
        gw["up%d" % l] = dw_mm("mlp_up_dw_%d" % l, hn, da, "col", tiles=WIDE)
        dhn = _mm_nt("mlp_up_dx_%d" % l, da, w_up_f(l), "col", tiles=DEEP)
        (dh_in,), (dgp, dsh, dsc) = _rw_bwd("mlp_pre_bwd_%d" % l, _f_pre, [h_in], [vec(gpl[l]), sh, sc], [dhn], [F32],
                                            add_rows={0: dh})
        gvec["mlp%d" % l] = (dsh, dsc, dgate)
        gvec["g_pre_mlp%d" % l], gvec["g_post_mlp%d" % l] = dgp, dgq
        return dh_in

    chunked = lambda a: a.reshape(N_CHIPS, a.shape[0] // N_CHIPS, a.shape[1])
    to_chunks = lambda a: a if a.ndim == 3 else chunked(a)
    red = {}

    def swap_begin(tag, names):
        return _swap_start("swap_start_" + tag, [to_chunks(gw[n]) for n in names])

    def reduce_begin(tag, names, after=None, swapped=None):
        state, token = _reduce_scatter_begin(tag, [to_chunks(gw[n]) for n in names], place_idx, after, swapped)
        return (names, state), token

    def reduce_end(handle, after):
        names, state = handle
        red.update(zip(names, _reduce_scatter_end(state, after)))

    dh3 = mlp_bwd(1, h3, hn3, s3, y3, dh4)
    sw1, token = swap_begin("r1", ["up1", "down1"])

    (dy2,), (dgate, dgq) = _rw_bwd("mla_post_bwd", _f_post, [y2], [after_token(mix[1][2], token), vec(gqm[1])],
                                   [dh3], [BF])
    gvec["g_post_mix1"] = dgq
    gw["w_o"] = dw_mm("mla_o_dw", att, dy2, "row")
    datt = _mm_nt("mla_o_dx", dy2, w_o_f, "row", o_dtypes=(BF,))
    rs1, token = reduce_begin("r1", ["up1", "down1"], after=datt, swapped=sw1)
    dqn, dqr, dkn, dvv, dkr = _attn_bwd(qn, qr, kn, kr + token[0:1, 0:1].astype(BF), vv, datt)
    (dqr_pre,), _ = _rw_bwd("mla_q_rope_bwd", _f_rope, [qr_pre, cos_t, sin_t], [], [dqr], [BF, None, None])
    g_qn = dw_mm("mla_uq_nope_dw", c_q, dqn, "col")
    g_qr = dw_mm("mla_uq_rope_dw", c_q, dqr_pre, "col")
    dc_q = _mm_nt("mla_uq_nope_dx", dqn, w_qn_f, "col")
    dc_q = _mm_nt("mla_uq_rope_dx", dqr_pre, w_qr_f, "col", extras=(dc_q,), epilogue=_add_epilogue)
    g_uk = dw_mm("mla_uk_dw", c_kv, dkn, "col")
    g_uv = dw_mm("mla_uv_dw", c_kv, dvv, "col")
    dc_kv = _mm_nt("mla_uk_dx", dkn, w_uk_f, "col")
    dc_kv = _mm_nt("mla_uv_dx", dvv, w_uv_f, "col", extras=(dc_kv,), epilogue=_add_epilogue)
    (dpre_kv, dpre_q), (dg_ckv, dg_cq) = _rw_bwd(
        "mla_latent_bwd", f_lat, [pre_kv, pre_q, cos_t, sin_t], [vec(w["g_ckv"]), vec(w["g_cq"][0])],
        [dc_kv, dkr, dc_q], [BF, BF, None, None])
    gvec["g_ckv"], gvec["g_cq"] = dg_ckv, dg_cq
    g_dkvkr = dw_mm("mla_dkv_dw", kvn, dpre_kv, "row")
    g_dq = dw_mm("mla_dq_dw", hn2, dpre_q, "row")
    dkvn = _mm_nt("mla_dkv_dx", dpre_kv, w_dkvkr, "row")
    dhn2 = _mm_nt("mla_dq_dx", dpre_q, w_dq_f, "row")
    (dh2,), (dg_kv, dkvsh, dkvsc, dgp, dsh, dsc) = _rw_bwd(
        "mla_pre_bwd", _f_pre2, [h2], [vec(w["g_kv"]), kv_shift, kv_scale, vec(gpm[1]), mix[1][0], mix[1][1]],
        [dkvn, dhn2], [F32], add_rows={0: dh3})
    gvec["mix1"] = (dsh, dsc, dgate)
    gvec["kv"] = (dkvsh, dkvsc)
    gvec["g_kv"], gvec["g_pre_mix1"] = dg_kv, dgp
    gw["small"] = jnp.concatenate([chunked(g_dkvkr), chunked(g_dq), g_uk, g_uv, g_qn, g_qr], axis=2)
    reduce_end(rs1, dh2)
    rs2, token = reduce_begin("r2", ["small", "w_o"])

    dh1 = mlp_bwd(0, h1, hn1, s1, y1, dh2, token=token)
    reduce_end(rs2, dh1)
    rs3, token = reduce_begin("r3", ["up0", "down0"])

    (dy0,), (dbout, dgate, dgq) = _rw_bwd("conv_post_bwd", _f_post_bias, [y0],
                                          [bout_full, after_token(mix[0][2], token), vec(gqm[0])], [dh1], [BF])
    gvec["g_post_mix0"] = dgq
    gw["conv_w_out"] = dw_mm("conv_out_dw", n0, dy0, "row")
    dn0 = _mm_nt("conv_out_dx", dy0, w_out_f, "row")
    (duc0,), (dlng, dlnb) = _rw_bwd("conv_ln_bwd", _f_ln_silu, [uc0], [lng_full, lnb_full], [dn0], [F32])
    du0, ddw = _conv_bwd(u0, duc0, dw_full)
    (dz0,), (dbin,) = _rw_bwd("conv_glu_bwd", _f_glu, [z0], [bin_full], [du0], [BF])
    gw["conv_w_in"] = dw_mm("conv_in_dw", hn0, dz0, "col")
    dhn0 = _mm_nt("conv_in_dx", dz0, w_in_f, "col")
    (dx,), (dgp, dsh, dsc) = _rw_bwd("conv_pre_bwd", _f_pre, [h0], [vec(gpm[0]), mix[0][0], mix[0][1]], [dhn0], [F32],
                                     add_rows={0: dh1})
    gvec["mix0"] = (dsh, dsc, dgate)
    gvec["g_pre_mix0"] = dgp
    reduce_end(rs3, [dx, gw["conv_w_in"], gw["conv_w_out"]] + early_updates(red))

    vec_list = [*gvec["mix0"], *gvec["mix1"], *gvec["mlp0"], *gvec["mlp1"], *gvec["kv"],
                gvec["g_pre_mix0"], gvec["g_pre_mix1"], gvec["g_post_mix0"], gvec["g_post_mix1"],
                gvec["g_pre_mlp0"], gvec["g_pre_mlp1"], gvec["g_post_mlp0"], gvec["g_post_mlp1"],
                gvec["g_kv"], gvec["g_ckv"], gvec["g_cq"], dbin, dlng, dlnb, dbout, ddw.reshape(1, -1),
                jnp.pad(loss_part, ((0, 0), (0, LANES - 1)))]
    vo = [0]
    for v_ in vec_list:
        vo.append(vo[-1] + v_.shape[1])
    vec_pad = (-vo[-1]) % (8 * LANES)
    n_vec = vo[-1] + vec_pad
    flat = jnp.concatenate(vec_list + [jnp.zeros((1, vec_pad), F32)], axis=1).reshape(8, n_vec // 8)
    all_vecs = _all_gather_small("gather_vector_grads", flat).reshape(N_DEV, 8, n_vec // 8)
    rs4, token = reduce_begin("r4", ["conv_w_in", "conv_w_out"], after=all_vecs)
    all_vecs = after_token(all_vecs, token)
    summed = _sum_devices(all_vecs).reshape(1, n_vec)
    per_dev = all_vecs.reshape(N_DEV, n_vec)
    seg = lambda a, i: a[:, vo[i]:vo[i + 1]]

    out = {"loss": seg(summed, 30)[0, 0], "x": dx.reshape(x.shape)}
    dm_mix = [jnp.concatenate([seg(per_dev, 3 * l + i) for i in range(3)], axis=1) for l in range(2)]
    dm_mlp = [jnp.concatenate([seg(per_dev, 6 + 3 * l + i) for i in range(3)], axis=1) for l in range(2)]
    dm_kv = jnp.concatenate([seg(per_dev, 12), seg(per_dev, 13)], axis=1)
    ada_dw = lambda name, dm, width: _mm_tn(name, sc_all, chip_cols(dm, width), "row")
    out["w_ada_mix"] = [ada_dw("ada_mix_dw_%d" % l, dm_mix[l], n_ada) for l in range(2)]
    out["w_ada_mlp"] = [ada_dw("ada_mlp_dw_%d" % l, dm_mlp[l], n_ada) for l in range(2)]
    out["w_ada_kv"] = ada_dw("ada_kv_dw", dm_kv, n_kvada)
    sum_seg = lambda i: seg(summed, i)
    out["b_ada_mix"] = jnp.concatenate([jnp.concatenate([sum_seg(3 * l + i) for i in range(3)], axis=1) for l in range(2)], axis=0)
    out["b_ada_mlp"] = jnp.concatenate([jnp.concatenate([sum_seg(6 + 3 * l + i) for i in range(3)], axis=1) for l in range(2)], axis=0)
    out["b_ada_kv"] = jnp.concatenate([sum_seg(12), sum_seg(13)], axis=1).reshape(-1)
    out["g_pre_mix"] = jnp.concatenate([sum_seg(14), sum_seg(15)], axis=0)
    out["g_post_mix"] = jnp.concatenate([sum_seg(16), sum_seg(17)], axis=0)
    out["g_pre_mlp"] = jnp.concatenate([sum_seg(18), sum_seg(19)], axis=0)
    out["g_post_mlp"] = jnp.concatenate([sum_seg(20), sum_seg(21)], axis=0)
    out["g_kv"] = sum_seg(22).reshape(-1)
    out["g_ckv"] = sum_seg(23).reshape(-1)
    out["g_cq"] = sum_seg(24)
    out["conv_b_in"] = chip_cols(sum_seg(25), 2 * dl)
    out["conv_ln_g"] = chip_cols(sum_seg(26), dl)
    out["conv_ln_b"] = chip_cols(sum_seg(27), dl)
    out["conv_b_out"] = chip_cols(sum_seg(28), dl)
    ddw_sum = chip_cols(sum_seg(29).reshape(CONV_WIDTH + 1, d), dl)
    out["conv_dw"] = ddw_sum[:CONV_WIDTH].reshape(1, CONV_WIDTH, dl)
    out["conv_dw_b"] = ddw_sum[CONV_WIDTH:]

    def last_group(after):
        reduce_end(rs4, after)
        return {"conv_w_in": red["conv_w_in"][None], "conv_w_out": red["conv_w_out"][None]}

    out["w_o"] = red["w_o"][None]
    out["mlp_w_up"] = [red["up0"], red["up1"]]
    out["mlp_w_down"] = [red["down0"], red["down1"]]
    rs = red["small"]
    piece = lambda i: rs[:, so[i]:so[i + 1]]
    out["w_dkv"] = piece(0)
    out["w_kr"] = piece(1)[:, :QK_ROPE]
    out["w_dq"] = piece(2)[None]
    out["w_uk"], out["w_uv"] = piece(3), piece(4)
    out["w_uq"] = jnp.concatenate([piece(5).reshape(r_q, heads_l, QK_NOPE), piece(6).reshape(r_q, heads_l, QK_ROPE)],
                                  axis=2).reshape(1, r_q, -1)
    return out, last_group
```

```python
import functools

import jax
import jax.numpy as jnp
from jax import lax
from jax.experimental import pallas as pl
from jax.experimental.pallas import tpu as pltpu

F32 = jnp.float32
BF = jnp.bfloat16
MXU_DTYPE = BF

EPS = 1e-6
NEG = -1e30
ROPE_THETA = 10000.0
QK_NOPE = 128
QK_ROPE = 64
V_HEAD = 128
CONV_WIDTH = 31
ADAM_LR, ADAM_B1, ADAM_B2, ADAM_EPS, ADAM_WD, ADAM_STEP = 0.001, 0.9, 0.999, 1e-08, 0.01, 10

N_CHIPS = 4
N_DEV = 8
LANES = 128
SUBLANES = 8
VMEM_LIMIT = 56 * 2 ** 20
BLOCK_BYTES = 2 ** 21
MESH = pl.DeviceIdType.MESH
ANY = pl.BlockSpec(memory_space=pl.ANY)


def _params(sem=None):
    return pltpu.CompilerParams(dimension_semantics=sem, vmem_limit_bytes=VMEM_LIMIT)


def _tile(n, pref, unit=LANES):
    if n <= pref:
        return n
    t = (pref // unit) * unit
    while t > unit and n % t:
        t -= unit
    assert n % t == 0, (n, pref)
    return t


def _dg(a, b, ca, cb):
    return lax.dot_general(a.astype(MXU_DTYPE), b.astype(MXU_DTYPE), (((ca,), (cb,)), ((), ())),
                           preferred_element_type=F32)


@jax.custom_vjp
def _dot_nn(a, b):
    return _dg(a, b, 1, 0)


def _dot_nn_fwd(a, b):
    return _dg(a, b, 1, 0), (a, b)


def _dot_nn_bwd(res, g):
    a, b = res
    return _dg(g, b, 1, 1).astype(a.dtype), _dg(a, g, 0, 0).astype(b.dtype)


_dot_nn.defvjp(_dot_nn_fwd, _dot_nn_bwd)


@jax.custom_vjp
def _dot_nt(a, b):
    return _dg(a, b, 1, 1)


def _dot_nt_fwd(a, b):
    return _dg(a, b, 1, 1), (a, b)


def _dot_nt_bwd(res, g):
    a, b = res
    return _dg(g, b, 1, 0).astype(a.dtype), _dg(g, a, 0, 0).astype(b.dtype)


_dot_nt.defvjp(_dot_nt_fwd, _dot_nt_bwd)


def _matmul(name, a, b, *, ca, cb, grid, a_blk, a_map, b_blk, b_map, o_shape, o_dtypes, o_blk, o_map,
            extras=(), epilogue=None):
    nk = grid[-1]
    n_ex, n_out = len(extras), len(o_dtypes)

    def body(*refs):
        a_ref, b_ref = refs[0], refs[1]
        ex_refs = refs[2:2 + n_ex]
        out_refs = refs[2 + n_ex:2 + n_ex + n_out]
        kk = pl.program_id(len(grid) - 1)

        def finish(acc):
            outs = (acc,) if epilogue is None else epilogue(acc, *[r[...] for r in ex_refs])
            for r, o in zip(out_refs, outs):
                r[...] = o.astype(r.dtype)

        part = _dg(a_ref[...], b_ref[...], ca, cb)
        if nk == 1:
            finish(part)
        else:
            acc_ref = refs[-1]

            @pl.when(kk == 0)
            def _():
                acc_ref[...] = part

            @pl.when(kk > 0)
            def _():
                acc_ref[...] += part

            @pl.when(kk == nk - 1)
            def _():
                finish(acc_ref[...])

    o_spec = pl.BlockSpec(o_blk, o_map)
    acc_shape = tuple(d for d in o_blk if d is not None)
    res = pl.pallas_call(
        body, grid=grid,
        in_specs=[pl.BlockSpec(a_blk, a_map), pl.BlockSpec(b_blk, b_map)] + [o_spec] * n_ex,
        out_specs=[o_spec] * n_out,
        out_shape=[jax.ShapeDtypeStruct(o_shape, dt) for dt in o_dtypes],
        scratch_shapes=[] if nk == 1 else [pltpu.VMEM(acc_shape, F32)],
        compiler_params=_params(("parallel",) * (len(grid) - 1) + ("arbitrary",)),
        name=name,
    )(a, b, *extras)
    return res[0] if n_out == 1 else res


TM, TN, TK = 2048, 512, 2048


def _mm_nn(name, a, w, layout, o_dtypes=(F32,), extras=(), epilogue=None, tiles=(TM, TN, TK), w_layer=None):
    m, k = a.shape
    tm, tk = _tile(m, tiles[0], 8), _tile(k, tiles[2])
    if layout == "row":
        n = w.shape[-1]
        tn = _tile(n, tiles[1])
        grid = (m // tm, n // tn, k // tk)
        b_blk, b_map = (tk, tn), (lambda i, j, kk: (kk, j))
        if w_layer is not None:
            b_blk, b_map = (None, tk, tn), (lambda i, j, kk: (w_layer, kk, j))
    else:
        nl = w.shape[2]
        n = nl * w.shape[0]
        tn = _tile(nl, tiles[1])
        nb = nl // tn
        grid = (m // tm, n // tn, k // tk)
        b_blk, b_map = (None, tk, tn), (lambda i, j, kk: (j // nb, kk, j % nb))
    return _matmul(name, a, w, ca=1, cb=0, grid=grid, a_blk=(tm, tk), a_map=lambda i, j, kk: (i, kk),
                   b_blk=b_blk, b_map=b_map, o_shape=(m, n), o_dtypes=o_dtypes, o_blk=(tm, tn),
                   o_map=lambda i, j, kk: (i, j), extras=extras, epilogue=epilogue)


def _mm_nt(name, g, w, layout, o_dtypes=(F32,), extras=(), epilogue=None, tiles=(TM, TN, TK)):
    m, n = g.shape
    tm, tn = _tile(m, tiles[0], 8), None
    if layout == "row":
        k = w.shape[0]
        tn = _tile(n, tiles[2])
        tk = _tile(k, tiles[1])
        grid = (m // tm, k // tk, n // tn)
        b_blk, b_map = (tk, tn), (lambda i, j, kk: (j, kk))
    else:
        k, nl = w.shape[1], w.shape[2]
        tn = _tile(nl, tiles[2])
        nb = nl // tn
        tk = _tile(k, tiles[1])
        grid = (m // tm, k // tk, n // tn)
        b_blk, b_map = (None, tk, tn), (lambda i, j, kk: (kk // nb, j, kk % nb))
    return _matmul(name, g, w, ca=1, cb=1, grid=grid, a_blk=(tm, tn), a_map=lambda i, j, kk: (i, kk),
                   b_blk=b_blk, b_map=b_map, o_shape=(m, k), o_dtypes=o_dtypes, o_blk=(tm, tk),
                   o_map=lambda i, j, kk: (i, j), extras=extras, epilogue=epilogue)


def _mm_tn(name, x, g, layout, o_dtype=F32, tiles=(TM, TN, TK)):
    t, k = x.shape
    n = g.shape[1]
    tt = _tile(t, tiles[2], 8)
    tk = _tile(k, tiles[0])
    if layout == "row":
        tn = _tile(n, tiles[1])
        o_shape, o_blk, o_map = (k, n), (tk, tn), (lambda i, j, kk: (i, j))
    else:
        nl = n // N_CHIPS
        tn = _tile(nl, tiles[1])
        nb = nl // tn
        o_shape, o_blk, o_map = (N_CHIPS, k, nl), (None, tk, tn), (lambda i, j, kk: (j // nb, i, j % nb))
    grid = (k // tk, n // tn, t // tt)
    return _matmul(name, x, g, ca=0, cb=0, grid=grid, a_blk=(tt, tk), a_map=lambda i, j, kk: (kk, i),
                   b_blk=(tt, tn), b_map=lambda i, j, kk: (kk, j), o_shape=o_shape, o_dtypes=(o_dtype,),
                   o_blk=o_blk, o_map=o_map)


TR = 256


def _rw_fwd(name, fn, rows, vecs, o_dtypes, tr=TR):
    t = rows[0].shape[0]
    tr = min(tr, t)
    n_r, n_v = len(rows), len(vecs)
    o_sds = jax.eval_shape(fn, *[jax.ShapeDtypeStruct((tr, r.shape[1]), r.dtype) for r in rows],
                           *[jax.ShapeDtypeStruct(v.shape, v.dtype) for v in vecs])

    def body(*refs):
        outs = fn(*[r[...] for r in refs[:n_r + n_v]])
        for r, o in zip(refs[n_r + n_v:], outs):
            r[...] = o.astype(r.dtype)

    res = pl.pallas_call(
        body, grid=(t // tr,),
        in_specs=[pl.BlockSpec((tr, r.shape[1]), lambda i: (i, 0)) for r in rows]
        + [pl.BlockSpec(v.shape, lambda i: (0, 0)) for v in vecs],
        out_specs=[pl.BlockSpec((tr, o.shape[1]), lambda i: (i, 0)) for o in o_sds],
        out_shape=[jax.ShapeDtypeStruct((t, o.shape[1]), dt) for o, dt in zip(o_sds, o_dtypes)],
        compiler_params=_params(("parallel",)), name=name,
    )(*rows, *vecs)
    return res


def _rw_bwd(name, fn, rows, vecs, cots, row_grad_dtypes, add_rows=None, tr=TR):
    t = rows[0].shape[0]
    tr = min(tr, t)
    add_rows = add_rows or {}
    n_r, n_v, n_c = len(rows), len(vecs), len(cots)
    diff = [i for i, dt in enumerate(row_grad_dtypes) if dt is not None]
    adds = [add_rows[i] for i in diff if i in add_rows]
    n_a = len(adds)

    def body(*refs):
        rvals = [r[...] for r in refs[:n_r]]
        vvals = [r[...] for r in refs[n_r:n_r + n_v]]
        c_refs = refs[n_r + n_v:n_r + n_v + n_c]
        a_refs = list(refs[n_r + n_v + n_c:n_r + n_v + n_c + n_a])
        o_refs = refs[n_r + n_v + n_c + n_a:]

        def f(*d):
            full = list(rvals)
            for i, x in zip(diff, d[:len(diff)]):
                full[i] = x
            return fn(*full, *d[len(diff):])

        outs, vjp = jax.vjp(f, *[rvals[i] for i in diff], *vvals)
        grads = vjp(tuple(c[...].astype(o.dtype) for c, o in zip(c_refs, outs)))
        for n, i in enumerate(diff):
            gr = grads[n].astype(F32)
            if i in add_rows:
                gr = gr + a_refs.pop(0)[...].astype(F32)
            o_refs[n][...] = gr.astype(o_refs[n].dtype)
        first = pl.program_id(0) == 0
        for r, gv in zip(o_refs[len(diff):], grads[len(diff):]):
            @pl.when(first)
            def _(r=r, gv=gv):
                r[...] = gv

            @pl.when(jnp.logical_not(first))
            def _(r=r, gv=gv):
                r[...] += gv

    row_spec = lambda a: pl.BlockSpec((tr, a.shape[1]), lambda i: (i, 0))
    vec_spec = lambda a: pl.BlockSpec(a.shape, lambda i: (0, 0))
    res = pl.pallas_call(
        body, grid=(t // tr,),
        in_specs=[row_spec(r) for r in rows] + [vec_spec(v) for v in vecs] + [row_spec(c) for c in cots]
        + [row_spec(a) for a in adds],
        out_specs=[row_spec(rows[i]) for i in diff] + [vec_spec(v) for v in vecs],
        out_shape=[jax.ShapeDtypeStruct(rows[i].shape, row_grad_dtypes[i]) for i in diff]
        + [jax.ShapeDtypeStruct(v.shape, F32) for v in vecs],
        compiler_params=_params(("arbitrary",)), name=name,
    )(*rows, *vecs, *cots, *adds)
    return res[:len(diff)], res[len(diff):]


def _rms(x, g):
    return x * lax.rsqrt(jnp.mean(x * x, axis=-1, keepdims=True) + EPS) * g


def _f_pre(h, g, shift, scale):
    return (_rms(h, g) * (1.0 + scale) + shift,)


def _f_pre2(h, g1, sh1, sc1, g2, sh2, sc2):
    return _f_pre(h, g1, sh1, sc1) + _f_pre(h, g2, sh2, sc2)


def _f_post(y, gate, g):
    return (gate * _rms(y, g),)


def _f_post_bias(y, bias, gate, g):
    return (gate * _rms(y + bias, g),)


def _f_glu(z, bias):
    z = z + bias
    half = z.shape[1] // 2
    return (z[:, :half] * jax.nn.sigmoid(z[:, half:]),)


def _f_ln_silu(u, g, b):
    mu = jnp.mean(u, axis=-1, keepdims=True)
    var = jnp.mean(jnp.square(u - mu), axis=-1, keepdims=True)
    y = (u - mu) * lax.rsqrt(var + EPS) * g + b
    return (y * jax.nn.sigmoid(y),)


def _rope_raw(x, cos, sin):
    n = x.shape[1]
    reps = n // LANES
    if reps > 1:
        cos = jnp.concatenate([cos] * reps, axis=1)
        sin = jnp.concatenate([sin] * reps, axis=1)
    lane = lax.broadcasted_iota(jnp.int32, x.shape, 1)
    half = QK_ROPE // 2
    partner = jnp.where((lane % QK_ROPE) < half, pltpu.roll(x, n - half, 1), pltpu.roll(x, half, 1))
    return x * cos + partner * sin


@jax.custom_vjp
def _rope(x, cos, sin):
    return _rope_raw(x, cos, sin)


def _rope_fwd(x, cos, sin):
    return _rope_raw(x, cos, sin), (cos, sin)


def _rope_bwd(res, g):
    cos, sin = res
    return _rope_raw(g, cos, -sin), jnp.zeros_like(cos), jnp.zeros_like(sin)


_rope.defvjp(_rope_fwd, _rope_bwd)


def _make_f_lat(r_kv):
    def f(pre_kv, pre_q, cos, sin, g_ckv, g_cq):
        c_kv = _rms(pre_kv[:, :r_kv], g_ckv)
        kr = _rope(pre_kv[:, r_kv:], cos, sin)
        return c_kv, kr, _rms(pre_q, g_cq)
    return f


def _f_rope(x, cos, sin):
    return (_rope(x, cos, sin),)


CONV_TC = 128
CONV_TT = 256
PADR = 32


def _conv_fwd(u, dw, dw_b):
    t, d = u.shape
    tc, tt = min(CONV_TC, d), min(CONV_TT, t)
    off = PADR - (CONV_WIDTH - 1)

    def body(u_ref, w_ref, b_ref, o_ref, pad_ref):
        pad_ref[pl.ds(0, PADR), :] = jnp.zeros((PADR, tc), F32)
        pad_ref[pl.ds(PADR, t), :] = u_ref[...]
        for t0 in range(0, t, tt):
            acc = jnp.zeros((tt, tc), F32) + b_ref[...]
            for j in range(CONV_WIDTH):
                acc = acc + pad_ref[pl.ds(t0 + off + j, tt), :] * w_ref[pl.ds(j, 1), :]
            o_ref[pl.ds(t0, tt), :] = acc

    return pl.pallas_call(
        body, grid=(d // tc,),
        in_specs=[pl.BlockSpec((t, tc), lambda j: (0, j)), pl.BlockSpec((CONV_WIDTH, tc), lambda j: (0, j)),
                  pl.BlockSpec((1, tc), lambda j: (0, j))],
        out_specs=pl.BlockSpec((t, tc), lambda j: (0, j)),
        out_shape=jax.ShapeDtypeStruct((t, d), F32),
        scratch_shapes=[pltpu.VMEM((t + PADR, tc), F32)],
        compiler_params=_params(("parallel",)), name="conv_fwd",
    )(u, dw, dw_b)


def _conv_bwd(u, duc, dw):
    t, d = u.shape
    tc, tt = min(CONV_TC, d), min(CONV_TT, t)
    off = PADR - (CONV_WIDTH - 1)

    def body(u_ref, g_ref, w_ref, du_ref, dwt_ref, padu_ref, padg_ref):
        padu_ref[pl.ds(0, PADR), :] = jnp.zeros((PADR, tc), F32)
        padu_ref[pl.ds(PADR, t), :] = u_ref[...]
        padg_ref[pl.ds(t, PADR), :] = jnp.zeros((PADR, tc), F32)
        padg_ref[pl.ds(0, t), :] = g_ref[...]
        for t0 in range(0, t, tt):
            acc = jnp.zeros((tt, tc), F32)
            for j in range(CONV_WIDTH):
                acc = acc + padg_ref[pl.ds(t0 + (CONV_WIDTH - 1) - j, tt), :] * w_ref[pl.ds(j, 1), :]
            du_ref[pl.ds(t0, tt), :] = acc
        for j in range(CONV_WIDTH):
            acc = jnp.zeros((tt, tc), F32)
            for t0 in range(0, t, tt):
                acc = acc + g_ref[pl.ds(t0, tt), :] * padu_ref[pl.ds(t0 + off + j, tt), :]
            dwt_ref[pl.ds(j, 1), :] = jnp.sum(acc, axis=0, keepdims=True)
        acc = jnp.zeros((tt, tc), F32)
        for t0 in range(0, t, tt):
            acc = acc + g_ref[pl.ds(t0, tt), :]
        dwt_ref[pl.ds(CONV_WIDTH, 1), :] = jnp.sum(acc, axis=0, keepdims=True)

    col = lambda r: pl.BlockSpec((r, tc), lambda j: (0, j))
    return pl.pallas_call(
        body, grid=(d // tc,),
        in_specs=[col(t), col(t), col(CONV_WIDTH)],
        out_specs=[col(t), col(CONV_WIDTH + 1)],
        out_shape=[jax.ShapeDtypeStruct((t, d), F32), jax.ShapeDtypeStruct((CONV_WIDTH + 1, d), F32)],
        scratch_shapes=[pltpu.VMEM((t + PADR, tc), F32), pltpu.VMEM((t + PADR, tc), F32)],
        compiler_params=_params(("parallel",)), name="conv_bwd",
    )(u, duc, dw)


@jax.custom_vjp
def _swap_halves(x):
    return pltpu.roll(x, LANES // 2, 1)


_swap_halves.defvjp(lambda x: (pltpu.roll(x, LANES // 2, 1), None), lambda _, g: (pltpu.roll(g, LANES // 2, 1),))


def _attn_block(qn, qr, kn, kr, v, q0):
    scale = (QK_NOPE + QK_ROPE) ** -0.5
    lane = lax.broadcasted_iota(jnp.int32, kr.shape, 1)
    kr_a = kr * (lane < QK_ROPE).astype(kr.dtype)
    kr_b = _swap_halves(kr_a)
    outs = []
    for hh, kr_h in ((0, kr_a), (1, kr_b)):
        sl = slice(hh * QK_NOPE, (hh + 1) * QK_NOPE)
        s = _dot_nt(jnp.concatenate([qn[:, sl], qr], axis=1), jnp.concatenate([kn[:, sl], kr_h], axis=1)) * scale
        diag = s[:, q0:]
        row = lax.broadcasted_iota(jnp.int32, diag.shape, 0)
        col = lax.broadcasted_iota(jnp.int32, diag.shape, 1)
        diag = jnp.where(col <= row, diag, NEG)
        s = diag if q0 == 0 else jnp.concatenate([s[:, :q0], diag], axis=1)
        e = jnp.exp(s - jnp.max(s, axis=-1, keepdims=True))
        p = e * (1.0 / jnp.sum(e, axis=-1, keepdims=True))
        outs.append(_dot_nn(p, v[:, sl]))
    return jnp.concatenate(outs, axis=1)


def _attn_fwd(qn, qr, kn, kr, v, tq=512):
    t, w = qn.shape
    pairs = w // (2 * QK_NOPE)
    tq = min(tq, t)
    pw = 2 * QK_NOPE

    def body(qn_ref, qr_ref, kn_ref, kr_ref, v_ref, o_ref):
        for q0 in range(0, t, tq):
            l = q0 + tq
            o_ref[pl.ds(q0, tq), :] = _attn_block(
                qn_ref[pl.ds(q0, tq), :], qr_ref[pl.ds(q0, tq), :], kn_ref[pl.ds(0, l), :], kr_ref[pl.ds(0, l), :],
                v_ref[pl.ds(0, l), :], q0).astype(o_ref.dtype)

    pair = lambda wd: pl.BlockSpec((t, wd), lambda p: (0, p))
    return pl.pallas_call(
        body, grid=(pairs,),
        in_specs=[pair(pw), pair(LANES), pair(pw), pl.BlockSpec((t, LANES), lambda p: (0, 0)), pair(pw)],
        out_specs=pair(pw), out_shape=jax.ShapeDtypeStruct((t, w), BF),
        compiler_params=_params(("parallel",)), name="attn_fwd",
    )(qn, qr, kn, kr, v)


def _attn_bwd(qn, qr, kn, kr, v, do, tq=256):
    t, w = qn.shape
    pairs = w // (2 * QK_NOPE)
    tq = min(tq, t)
    pw = 2 * QK_NOPE

    def body(qn_ref, qr_ref, kn_ref, kr_ref, v_ref, do_ref, dqn_ref, dqr_ref, dkn_ref, dv_ref, dkr_ref,
             akn_ref, av_ref, akr_ref):
        akn_ref[...] = jnp.zeros_like(akn_ref)
        av_ref[...] = jnp.zeros_like(av_ref)
        akr_ref[...] = jnp.zeros_like(akr_ref)
        for q0 in range(0, t, tq):
            l = q0 + tq
            rows, keys = pl.ds(q0, tq), pl.ds(0, l)
            _, vjp = jax.vjp(functools.partial(_attn_block, q0=q0), qn_ref[rows, :], qr_ref[rows, :],
                             kn_ref[keys, :], kr_ref[keys, :], v_ref[keys, :])
            dqn, dqr, dkn, dkr, dv = vjp(do_ref[rows, :].astype(F32))
            dqn_ref[rows, :] = dqn.astype(dqn_ref.dtype)
            dqr_ref[rows, :] = dqr.astype(dqr_ref.dtype)
            akn_ref[keys, :] += dkn.astype(F32)
            av_ref[keys, :] += dv.astype(F32)
            akr_ref[keys, :] += dkr.astype(F32)
        dkn_ref[...] = akn_ref[...].astype(dkn_ref.dtype)
        dv_ref[...] = av_ref[...].astype(dv_ref.dtype)
        first = pl.program_id(0) == 0

        @pl.when(first)
        def _():
            dkr_ref[...] = akr_ref[...]

        @pl.when(jnp.logical_not(first))
        def _():
            dkr_ref[...] += akr_ref[...]

    pair = lambda wd: pl.BlockSpec((t, wd), lambda p: (0, p))
    shared = pl.BlockSpec((t, LANES), lambda p: (0, 0))
    sds = jax.ShapeDtypeStruct
    return pl.pallas_call(
        body, grid=(pairs,),
        in_specs=[pair(pw), pair(LANES), pair(pw), shared, pair(pw), pair(pw)],
        out_specs=[pair(pw), pair(LANES), pair(pw), pair(pw), shared],
        out_shape=[sds((t, w), BF), sds((t, pairs * LANES), F32), sds((t, w), BF), sds((t, w), BF), sds((t, LANES), F32)],
        scratch_shapes=[pltpu.VMEM((t, pw), F32), pltpu.VMEM((t, pw), F32), pltpu.VMEM((t, LANES), F32)],
        compiler_params=_params(("arbitrary",)), name="attn_bwd",
    )(qn, qr, kn, kr, v, do)


def _ew(name, fn, ins, o_dtypes, max_bytes=BLOCK_BYTES):
    r, c = ins[0].shape
    tr = r
    if r * c * 4 > max_bytes:
        tr = max(16, (max_bytes // (c * 4)) // 16 * 16)
        while r % tr:
            tr -= 16
    n_in = len(ins)

    def body(*refs):
        outs = fn(*[x[...] for x in refs[:n_in]])
        for o_ref, o in zip(refs[n_in:], outs):
            o_ref[...] = o.astype(o_ref.dtype)

    spec = pl.BlockSpec((tr, c), lambda i: (i, 0))
    return pl.pallas_call(
        body, grid=(r // tr,), in_specs=[spec] * n_in, out_specs=[spec] * len(o_dtypes),
        out_shape=[jax.ShapeDtypeStruct((r, c), dt) for dt in o_dtypes],
        compiler_params=_params(("parallel",)), name=name,
    )(*ins)


def _adamw_math(w, g, m, v):
    m = ADAM_B1 * m + (1.0 - ADAM_B1) * g
    v = ADAM_B2 * v + (1.0 - ADAM_B2) * jnp.square(g)
    m_hat = m / (1.0 - ADAM_B1 ** ADAM_STEP)
    v_hat = v / (1.0 - ADAM_B2 ** ADAM_STEP)
    delta = -ADAM_LR * (m_hat / (jnp.sqrt(v_hat) + ADAM_EPS) + ADAM_WD * w)
    return delta, m, v


def _adamw(name, w, g, m, v):
    shape = w.shape
    to2 = lambda a: a.reshape(-1, shape[-1]) if a.ndim > 1 else a.reshape(1, -1)
    d, nm, nv = _ew("adamw_" + name, _adamw_math, [to2(w), to2(g), to2(m), to2(v)], (F32, F32, F32))
    return d.reshape(shape), nm.reshape(shape), nv.reshape(shape)


def _adamw_layer(name, w, g, m, v, layer, prev=()):
    _, r, n = w.shape
    tr = r
    while tr * n * 4 > BLOCK_BYTES and tr % 16 == 0:
        tr //= 2

    def body(*refs):
        w_ref, g_ref, m_ref, v_ref = refs[:4]
        og, od, om, ov = refs[-4:]
        gval = g_ref[...].astype(F32)
        d, nm, nv = _adamw_math(w_ref[...], gval, m_ref[...], v_ref[...])
        og[...] = gval
        od[...] = d
        om[...] = nm
        ov[...] = nv

    lay = pl.BlockSpec((None, tr, n), lambda i: (layer, i, 0))
    return pl.pallas_call(
        body, grid=(r // tr,),
        in_specs=[lay, pl.BlockSpec((tr, n), lambda i: (i, 0)), lay, lay] + [ANY] * len(prev),
        out_specs=[lay] * 4, out_shape=[jax.ShapeDtypeStruct(w.shape, F32)] * 4,
        input_output_aliases={4 + k: k for k in range(len(prev))},
        compiler_params=_params(("parallel",)), name="adamw_%s_%d" % (name, layer),
    )(w, g, m, v, *prev)


def _rope_tables(positions, t):
    half = QK_ROPE // 2
    inv = 1.0 / (ROPE_THETA ** (jnp.arange(0, QK_ROPE, 2, dtype=F32) / QK_ROPE))
    inv_l = jnp.tile(inv, LANES // half).reshape(1, LANES)
    sign = jnp.tile(jnp.concatenate([-jnp.ones((half,), F32), jnp.ones((half,), F32)]), LANES // QK_ROPE).reshape(1, LANES)

    def body(p_ref, inv_ref, sg_ref, c_ref, s_ref):
        ang = p_ref[...].astype(F32) * inv_ref[...]
        c_ref[...] = jnp.cos(ang)
        s_ref[...] = jnp.sin(ang) * sg_ref[...]

    return pl.pallas_call(body, out_shape=[jax.ShapeDtypeStruct((t, LANES), F32)] * 2, name="rope_tables")(
        positions.reshape(t, 1), inv_l, sign)


def _loss_grad(h, target):
    t, d = h.shape
    tr = min(TR, t)

    def body(h_ref, y_ref, dh_ref, l_ref):
        err = h_ref[...] - y_ref[...]
        dh_ref[...] = err * (1.0 / d)
        part = 0.5 * jnp.sum(jnp.mean(jnp.square(err), axis=-1, keepdims=True), axis=0, keepdims=True)
        first = pl.program_id(0) == 0

        @pl.when(first)
        def _():
            l_ref[...] = part

        @pl.when(jnp.logical_not(first))
        def _():
            l_ref[...] += part

    row = pl.BlockSpec((tr, d), lambda i: (i, 0))
    return pl.pallas_call(
        body, grid=(t // tr,), in_specs=[row, row],
        out_specs=[row, pl.BlockSpec((1, 1), lambda i: (0, 0))],
        out_shape=[jax.ShapeDtypeStruct((t, d), F32), jax.ShapeDtypeStruct((1, 1), F32)],
        compiler_params=_params(("arbitrary",)), name="loss_grad",
    )(h, target)


def _sum_devices(g):
    def fn(*parts):
        acc = parts[0]
        for p in parts[1:]:
            acc = acc + p
        return (acc,)
    return _ew("sum_devices", fn, [g[i] for i in range(g.shape[0])], (F32,))[0]


def _place():
    x, y, c = lax.axis_index("x"), lax.axis_index("y"), lax.axis_index("c")
    return x, y, c, [(1 - x, y), (x, 1 - y), (1 - x, 1 - y)]


def _all_gather_small(name, v):
    r, n = v.shape

    def body(x_ref, out_ref, send_sems, recv_sems, local_sem):
        x, y, c, chips = _place()
        me, sibling = (x, y, c), (x, y, 1 - c)

        def rows(px, py, pc):
            return out_ref.at[4 * px + 2 * py + pc]

        def copy(k, block, to, src=None):
            return pltpu.make_async_remote_copy(
                src_ref=rows(*block) if src is None else src, dst_ref=rows(*block), send_sem=send_sems.at[k],
                recv_sem=recv_sems.at[k], device_id=to, device_id_type=MESH)

        mine = pltpu.make_async_copy(x_ref, rows(*me), local_sem)
        mine.start()
        first = [copy(0, me, sibling, src=x_ref)]
        first += [copy(1 + j, me, (*chip, c), src=x_ref) for j, chip in enumerate(chips)]
        for cp in first:
            cp.start()
        passed = [copy(4 + j, (*chip, c), sibling) for j, chip in enumerate(chips)]
        for j, chip in enumerate(chips):
            copy(1 + j, (*chip, c), me).wait_recv()
            passed[j].start()
        copy(0, sibling, me).wait_recv()
        for j, chip in enumerate(chips):
            copy(4 + j, (*chip, 1 - c), me).wait_recv()
        for cp in first + passed:
            cp.wait_send()
        mine.wait()

    return pl.pallas_call(
        body, out_shape=jax.ShapeDtypeStruct((N_DEV, r, n), v.dtype),
        in_specs=[pl.BlockSpec(memory_space=pltpu.VMEM)], out_specs=pl.BlockSpec(memory_space=pltpu.VMEM),
        scratch_shapes=[pltpu.SemaphoreType.DMA((7,)), pltpu.SemaphoreType.DMA((7,)), pltpu.SemaphoreType.DMA],
        compiler_params=pltpu.CompilerParams(vmem_limit_bytes=VMEM_LIMIT), name=name,
    )(v)


def _cast_into_slot(name, w, layer, idx, after=None):
    _, r, n = w.shape
    tr = r
    while tr * n * 4 > BLOCK_BYTES and tr % 32 == 0:
        tr //= 2
    order = [] if after is None else [after]

    def body(idx_ref, w_ref, *rest):
        o_ref = rest[-1]
        o_ref[...] = w_ref[...].astype(o_ref.dtype)

    return pl.pallas_call(
        body,
        grid_spec=pltpu.PrefetchScalarGridSpec(
            num_scalar_prefetch=1, grid=(r // tr,),
            in_specs=[pl.BlockSpec((None, tr, n), lambda i, idx_ref: (layer, i, 0))] + [ANY] * len(order),
            out_specs=pl.BlockSpec((None, tr, n), lambda i, idx_ref: (idx_ref[1], i, 0))),
        out_shape=jax.ShapeDtypeStruct((N_CHIPS, r, n), BF),
        compiler_params=_params(("parallel",)), name="cast_" + name,
    )(idx, w, *order)


def _swap_other_half(name, grads, after=None):
    n_w = len(grads)
    order = [] if after is None else [after]
    n_in = n_w + len(order)

    def body(*refs):
        ins, outs = refs[:n_w], refs[n_in:n_in + n_w]
        send_sems, recv_sems = refs[n_in + n_w:]
        x, y, c, _ = _place()
        cps = []
        for i in range(n_w):
            h = ins[i].shape[1] // 2
            cp = pltpu.make_async_remote_copy(
                src_ref=ins[i].at[:, pl.ds((1 - c) * h, h), :], dst_ref=outs[i], send_sem=send_sems.at[i],
                recv_sem=recv_sems.at[i], device_id=(x, y, 1 - c), device_id_type=MESH)
            cp.start()
            cps.append(cp)
        for cp in cps:
            cp.wait()

    return pl.pallas_call(
        body, in_specs=[ANY] * n_in, out_specs=[ANY] * n_w,
        out_shape=[jax.ShapeDtypeStruct((g.shape[0], g.shape[1] // 2, g.shape[2]), g.dtype) for g in grads],
        scratch_shapes=[pltpu.SemaphoreType.DMA((n_w,)), pltpu.SemaphoreType.DMA((n_w,))],
        name=name,
    )(*grads, *order)


def _add_my_half(name, g, s, idx, o_dtype):
    nc, r, n = g.shape
    h = r // 2
    tr = h
    while tr * n * 4 > BLOCK_BYTES and tr % 32 == 0:
        tr //= 2
    nb = h // tr

    def body(idx_ref, g_ref, s_ref, o_ref, own_ref):
        val = (g_ref[...].astype(F32) + s_ref[...].astype(F32)).astype(o_ref.dtype)
        o_ref[...] = val

        @pl.when(pl.program_id(1) == idx_ref[1])
        def _():
            own_ref[...] = val

    return pl.pallas_call(
        body,
        grid_spec=pltpu.PrefetchScalarGridSpec(
            num_scalar_prefetch=1, grid=(nb, nc),
            in_specs=[pl.BlockSpec((None, tr, n), lambda i, j, idx_ref: (j, idx_ref[0] * nb + i, 0)),
                      pl.BlockSpec((None, tr, n), lambda i, j, idx_ref: (j, i, 0))],
            out_specs=[pl.BlockSpec((None, tr, n), lambda i, j, idx_ref: (j, i, 0)),
                       pl.BlockSpec((None, tr, n), lambda i, j, idx_ref: (idx_ref[1], i, 0))]),
        out_shape=[jax.ShapeDtypeStruct((nc, h, n), o_dtype)] * 2,
        compiler_params=_params(("parallel", "arbitrary")), name=name,
    )(idx, g, s)


def _sum_chips(name, q, idx):
    nc, h, n = q.shape
    tr = h
    while tr * n * 4 > BLOCK_BYTES // 2 and tr % 32 == 0:
        tr //= 2
    nb = h // tr

    def body(idx_ref, q_ref, o_ref):
        acc = q_ref[0].astype(F32)
        for j in range(1, nc):
            acc = acc + q_ref[j].astype(F32)
        o_ref[...] = acc

    return pl.pallas_call(
        body,
        grid_spec=pltpu.PrefetchScalarGridSpec(
            num_scalar_prefetch=1, grid=(nb,),
            in_specs=[pl.BlockSpec((nc, tr, n), lambda i, idx_ref: (0, i, 0))],
            out_specs=pl.BlockSpec((tr, n), lambda i, idx_ref: (idx_ref[0] * nb + i, 0))),
        out_shape=jax.ShapeDtypeStruct((2 * h, n), F32),
        compiler_params=_params(("parallel",)), name=name,
    )(idx, q)


def _join_halves(name, bufs):
    n_w = len(bufs)

    def body(*refs):
        outs = refs[n_w:2 * n_w]
        send_sems, recv_sems = refs[2 * n_w:]
        x, y, c, _ = _place()

        def copy(i, which):
            h = outs[i].shape[0] // 2
            rows = outs[i].at[pl.ds(which * h, h)]
            return pltpu.make_async_remote_copy(
                src_ref=rows, dst_ref=rows, send_sem=send_sems.at[i], recv_sem=recv_sems.at[i],
                device_id=(x, y, 1 - c), device_id_type=MESH)

        cps = [copy(i, c) for i in range(n_w)]
        for cp in cps:
            cp.start()
        for i, cp in enumerate(cps):
            cp.wait_send()
            copy(i, 1 - c).wait_recv()

    return pl.pallas_call(
        body, in_specs=[ANY] * n_w, out_specs=[ANY] * n_w,
        out_shape=[jax.ShapeDtypeStruct(b.shape, b.dtype) for b in bufs],
        input_output_aliases={i: i for i in range(n_w)},
        scratch_shapes=[pltpu.SemaphoreType.DMA((n_w,)), pltpu.SemaphoreType.DMA((n_w,))],
        name=name,
    )(*bufs)


HBM_SPEC = pl.BlockSpec(memory_space=pltpu.HBM)
SEM_SPEC = pl.BlockSpec(memory_space=pltpu.SEMAPHORE)
VMEM_SPEC = pl.BlockSpec(memory_space=pltpu.VMEM)
SIDE_EFFECT = pltpu.SideEffectType.DATAFLOW_SIDE_EFFECTING


def _in_hbm(arrays):
    return [pltpu.with_memory_space_constraint(a, pltpu.HBM) for a in arrays]


def _ici_gather_copy(ref, i, j, chip, send_sems, recv_sems):
    x, y, c, _ = _place()
    h = ref.shape[1] // 2
    mine = ref.at[2 * x + y, pl.ds(c * h, h)]
    return pltpu.make_async_remote_copy(
        src_ref=mine, dst_ref=mine, send_sem=send_sems.at[3 * i + j], recv_sem=recv_sems.at[3 * i + j],
        device_id=(*chip, c), device_id_type=MESH)


def _gather_start(name, bufs, after=None):
    n_w = len(bufs)
    order = [] if after is None else [after]

    def body(*refs):
        ins, token = refs[:n_w], refs[-1]
        send_sems, recv_sems = refs[n_w + len(order)], refs[n_w + len(order) + 1]
        chips = _place()[3]
        for i in range(n_w):
            for j, chip in enumerate(chips):
                _ici_gather_copy(ins[i], i, j, chip, send_sems, recv_sems).start()
        token[...] = jnp.zeros_like(token)

    res = pl.pallas_call(
        body, name=name, in_specs=[HBM_SPEC] * n_w + [ANY] * len(order),
        out_shape=(pltpu.SemaphoreType.DMA((3 * n_w,)), pltpu.SemaphoreType.DMA((3 * n_w,)),
                   *[pltpu.HBM(b.shape, b.dtype) for b in bufs], jax.ShapeDtypeStruct((SUBLANES, LANES), F32)),
        out_specs=(SEM_SPEC, SEM_SPEC, *[HBM_SPEC] * n_w, VMEM_SPEC),
        input_output_aliases={i: 2 + i for i in range(n_w)},
        compiler_params=pltpu.CompilerParams(has_side_effects=SIDE_EFFECT),
    )(*_in_hbm(bufs), *order)
    return res[0], res[1], list(res[2:2 + n_w]), res[-1]


def _gather_wait(name, send_sems, recv_sems, bufs, after):
    n_w = len(bufs)

    def body(*refs):
        ins, s_sems, r_sems = refs[:n_w], refs[n_w], refs[n_w + 1]
        chips = _place()[3]
        for i in range(n_w):
            for j, chip in enumerate(chips):
                cp = _ici_gather_copy(ins[i], i, j, chip, s_sems, r_sems)
                cp.wait_send()
                cp.wait_recv()

    return pl.pallas_call(
        body, name=name, in_specs=[HBM_SPEC] * n_w + [SEM_SPEC, SEM_SPEC, ANY],
        out_shape=[pltpu.HBM(b.shape, b.dtype) for b in bufs], out_specs=[HBM_SPEC] * n_w,
        input_output_aliases={i: i for i in range(n_w)},
        compiler_params=pltpu.CompilerParams(has_side_effects=SIDE_EFFECT),
    )(*bufs, send_sems, recv_sems, after)


def _forward_to_sibling(name, bufs):
    n_w = len(bufs)

    def body(*refs):
        outs = refs[n_w:2 * n_w]
        send_sems, recv_sems = refs[2 * n_w:]
        x, y, c, chips = _place()

        def copy(i, j, chip, which):
            h = outs[i].shape[1] // 2
            blk = outs[i].at[2 * chip[0] + chip[1], pl.ds(which * h, h)]
            return pltpu.make_async_remote_copy(
                src_ref=blk, dst_ref=blk, send_sem=send_sems.at[3 * i + j], recv_sem=recv_sems.at[3 * i + j],
                device_id=(x, y, 1 - c), device_id_type=MESH)

        sends = [copy(i, j, chip, c) for i in range(n_w) for j, chip in enumerate(chips)]
        for cp in sends:
            cp.start()
        for i in range(n_w):
            for j, chip in enumerate(chips):
                copy(i, j, chip, 1 - c).wait_recv()
        for cp in sends:
            cp.wait_send()

    return pl.pallas_call(
        body, in_specs=[ANY] * n_w, out_specs=[ANY] * n_w,
        out_shape=[jax.ShapeDtypeStruct(b.shape, b.dtype) for b in bufs],
        input_output_aliases={i: i for i in range(n_w)},
        scratch_shapes=[pltpu.SemaphoreType.DMA((3 * n_w,)), pltpu.SemaphoreType.DMA((3 * n_w,))],
        name=name,
    )(*bufs)


def _forward_copy(buf, i, j, chip, which, send_sems, recv_sems):
    x, y, c, _ = _place()
    h = buf.shape[1] // 2
    blk = buf.at[2 * chip[0] + chip[1], pl.ds(which * h, h)]
    return pltpu.make_async_remote_copy(
        src_ref=blk, dst_ref=blk, send_sem=send_sems.at[3 * i + j], recv_sem=recv_sems.at[3 * i + j],
        device_id=(x, y, 1 - c), device_id_type=MESH)


def _forward_start(name, bufs):
    n_w = len(bufs)

    def body(*refs):
        ins, send_sems, recv_sems, token = refs[:n_w], refs[n_w], refs[n_w + 1], refs[-1]
        x, y, c, chips = _place()
        for i in range(n_w):
            for j, chip in enumerate(chips):
                _forward_copy(ins[i], i, j, chip, c, send_sems, recv_sems).start()
        token[...] = jnp.zeros_like(token)

    res = pl.pallas_call(
        body, name=name, in_specs=[HBM_SPEC] * n_w,
        out_shape=(pltpu.SemaphoreType.DMA((3 * n_w,)), pltpu.SemaphoreType.DMA((3 * n_w,)),
                   *[pltpu.HBM(b.shape, b.dtype) for b in bufs], jax.ShapeDtypeStruct((SUBLANES, LANES), F32)),
        out_specs=(SEM_SPEC, SEM_SPEC, *[HBM_SPEC] * n_w, VMEM_SPEC),
        input_output_aliases={i: 2 + i for i in range(n_w)},
        compiler_params=pltpu.CompilerParams(has_side_effects=SIDE_EFFECT),
    )(*_in_hbm(bufs))
    return (res[0], res[1], list(res[2:2 + n_w])), res[-1]


def _forward_wait(name, state, after):
    send_sems, recv_sems, bufs = state
    n_w = len(bufs)

    def body(*refs):
        ins, s_sems, r_sems = refs[:n_w], refs[n_w], refs[n_w + 1]
        x, y, c, chips = _place()
        for i in range(n_w):
            for j, chip in enumerate(chips):
                _forward_copy(ins[i], i, j, chip, c, s_sems, r_sems).wait_send()
                _forward_copy(ins[i], i, j, chip, 1 - c, s_sems, r_sems).wait_recv()

    return pl.pallas_call(
        body, name=name, in_specs=[HBM_SPEC] * n_w + [SEM_SPEC, SEM_SPEC, ANY],
        out_shape=[pltpu.HBM(b.shape, b.dtype) for b in bufs], out_specs=[HBM_SPEC] * n_w,
        input_output_aliases={i: i for i in range(n_w)},
        compiler_params=pltpu.CompilerParams(has_side_effects=SIDE_EFFECT),
    )(*bufs, send_sems, recv_sems, after)


def _ici_scatter_copy(part, buf, i, j, chip, send_sems, recv_sems):
    x, y, c, _ = _place()
    return pltpu.make_async_remote_copy(
        src_ref=part.at[2 * chip[0] + chip[1]], dst_ref=buf.at[2 * x + y], send_sem=send_sems.at[3 * i + j],
        recv_sem=recv_sems.at[3 * i + j], device_id=(*chip, c), device_id_type=MESH)


def _ici_scatter_landing(buf, i, j, chip, send_sems, recv_sems):
    x, y, c, _ = _place()
    blk = buf.at[2 * chip[0] + chip[1]]
    return pltpu.make_async_remote_copy(
        src_ref=blk, dst_ref=blk, send_sem=send_sems.at[3 * i + j], recv_sem=recv_sems.at[3 * i + j],
        device_id=(*chip, c), device_id_type=MESH)


def _scatter_start(name, parts, bufs):
    n_w = len(parts)

    def body(*refs):
        ps, bs = refs[:n_w], refs[n_w:2 * n_w]
        send_sems, recv_sems, token = refs[2 * n_w], refs[2 * n_w + 1], refs[-1]
        chips = _place()[3]
        for i in range(n_w):
            for j, chip in enumerate(chips):
                _ici_scatter_copy(ps[i], bs[i], i, j, chip, send_sems, recv_sems).start()
        token[...] = jnp.zeros_like(token)

    both = list(parts) + list(bufs)
    res = pl.pallas_call(
        body, name=name, in_specs=[HBM_SPEC] * (2 * n_w),
        out_shape=(pltpu.SemaphoreType.DMA((3 * n_w,)), pltpu.SemaphoreType.DMA((3 * n_w,)),
                   *[pltpu.HBM(b.shape, b.dtype) for b in both], jax.ShapeDtypeStruct((SUBLANES, LANES), F32)),
        out_specs=(SEM_SPEC, SEM_SPEC, *[HBM_SPEC] * (2 * n_w), VMEM_SPEC),
        input_output_aliases={i: 2 + i for i in range(2 * n_w)},
        compiler_params=pltpu.CompilerParams(has_side_effects=SIDE_EFFECT),
    )(*_in_hbm(both))
    return res[0], res[1], list(res[2:2 + n_w]), list(res[2 + n_w:2 + 2 * n_w]), res[-1]


def _scatter_wait(name, send_sems, recv_sems, parts, bufs, after):
    n_w = len(parts)
    after = list(after) if isinstance(after, (list, tuple)) else [after]

    def body(*refs):
        ps, bs = refs[:n_w], refs[n_w:2 * n_w]
        s_sems, r_sems = refs[2 * n_w], refs[2 * n_w + 1]
        chips = _place()[3]
        for i in range(n_w):
            for j, chip in enumerate(chips):
                _ici_scatter_copy(ps[i], bs[i], i, j, chip, s_sems, r_sems).wait_send()
                _ici_scatter_landing(bs[i], i, j, chip, s_sems, r_sems).wait_recv()

    both = list(parts) + list(bufs)
    res = pl.pallas_call(
        body, name=name, in_specs=[HBM_SPEC] * (2 * n_w) + [SEM_SPEC, SEM_SPEC] + [ANY] * len(after),
        out_shape=[pltpu.HBM(b.shape, b.dtype) for b in both], out_specs=[HBM_SPEC] * (2 * n_w),
        input_output_aliases={i: i for i in range(2 * n_w)},
        compiler_params=pltpu.CompilerParams(has_side_effects=SIDE_EFFECT),
    )(*both, send_sems, recv_sems, *after)
    return list(res[n_w:])


def _swap_copy(g, land, i, send_sems, recv_sems):
    x, y, c, _ = _place()
    h = g.shape[1] // 2
    return pltpu.make_async_remote_copy(
        src_ref=g.at[:, pl.ds((1 - c) * h, h), :], dst_ref=land, send_sem=send_sems.at[i], recv_sem=recv_sems.at[i],
        device_id=(x, y, 1 - c), device_id_type=MESH)


def _swap_start(name, grads):
    n_w = len(grads)
    lands = [lax.empty((g.shape[0], g.shape[1] // 2, g.shape[2]), g.dtype) for g in grads]

    def body(*refs):
        gs, ls = refs[:n_w], refs[n_w:2 * n_w]
        send_sems, recv_sems, token = refs[2 * n_w], refs[2 * n_w + 1], refs[-1]
        for i in range(n_w):
            _swap_copy(gs[i], ls[i], i, send_sems, recv_sems).start()
        token[...] = jnp.zeros_like(token)

    both = list(grads) + lands
    res = pl.pallas_call(
        body, name=name, in_specs=[HBM_SPEC] * (2 * n_w),
        out_shape=(pltpu.SemaphoreType.DMA((n_w,)), pltpu.SemaphoreType.DMA((n_w,)),
                   *[pltpu.HBM(b.shape, b.dtype) for b in both], jax.ShapeDtypeStruct((SUBLANES, LANES), F32)),
        out_specs=(SEM_SPEC, SEM_SPEC, *[HBM_SPEC] * (2 * n_w), VMEM_SPEC),
        input_output_aliases={i: 2 + i for i in range(2 * n_w)},
        compiler_params=pltpu.CompilerParams(has_side_effects=SIDE_EFFECT),
    )(*_in_hbm(both))
    return (res[0], res[1], list(res[2:2 + n_w]), list(res[2 + n_w:2 + 2 * n_w])), res[-1]


def _swap_wait(name, swapped, after):
    send_sems, recv_sems, grads, lands = swapped
    n_w = len(grads)

    def body(*refs):
        gs, ls = refs[:n_w], refs[n_w:2 * n_w]
        s_sems, r_sems = refs[2 * n_w], refs[2 * n_w + 1]
        for i in range(n_w):
            cp = _swap_copy(gs[i], ls[i], i, s_sems, r_sems)
            cp.wait_send()
            cp.wait_recv()

    both = list(grads) + list(lands)
    res = pl.pallas_call(
        body, name=name, in_specs=[HBM_SPEC] * (2 * n_w) + [SEM_SPEC, SEM_SPEC, ANY],
        out_shape=[pltpu.HBM(b.shape, b.dtype) for b in both], out_specs=[HBM_SPEC] * (2 * n_w),
        input_output_aliases={i: i for i in range(2 * n_w)},
        compiler_params=pltpu.CompilerParams(has_side_effects=SIDE_EFFECT),
    )(*both, send_sems, recv_sems, after)
    return list(res[:n_w]), list(res[n_w:])


def _reduce_scatter_begin(tag, grads, idx, after=None, swapped=None):
    if swapped is None:
        from_sibling = _swap_other_half("swap_other_half_" + tag, grads, after)
    else:
        grads, from_sibling = _swap_wait("swap_wait_" + tag, swapped, after)
    pairs = [_add_my_half("add_my_half_%s_%d" % (tag, i), g, s, idx, BF)
             for i, (g, s) in enumerate(zip(grads, from_sibling))]
    s_sems, r_sems, parts, bufs, token = _scatter_start("scatter_start_" + tag, [p[0] for p in pairs],
                                                        [p[1] for p in pairs])
    return (tag, s_sems, r_sems, parts, bufs, idx), token


def _reduce_scatter_end(state, after):
    tag, s_sems, r_sems, parts, bufs, idx = state
    received = _scatter_wait("scatter_wait_" + tag, s_sems, r_sems, parts, bufs, after)
    halves = [_sum_chips("sum_chips_%s_%d" % (tag, i), q, idx) for i, q in enumerate(received)]
    return _join_halves("join_halves_" + tag, halves)


def _relu2_epilogue(acc):
    r = jnp.maximum(acc, 0.0)
    return (r * r,)


def _relu2_bwd_epilogue(acc, s):
    return (acc * (2.0 * jnp.sqrt(s.astype(F32))),)


WIDE = (2048, 512, 2048)
DEEP = (2048, 512, 2048)


def _add_epilogue(acc, other):
    return (acc + other,)


def kernel(x, c, positions, w_ada_mix, b_ada_mix, w_ada_mlp, b_ada_mlp, g_pre_mix, g_post_mix, g_pre_mlp, g_post_mlp, conv_w_in, conv_b_in, conv_dw, conv_dw_b, conv_ln_g, conv_ln_b, conv_w_out, conv_b_out, w_ada_kv, b_ada_kv, g_kv, w_dkv, g_ckv, w_kr, w_uk, w_uv, w_dq, g_cq, w_uq, w_o, mlp_w_up, mlp_w_down, loss_target, m_w_ada_mix, m_b_ada_mix, m_w_ada_mlp, m_b_ada_mlp, m_g_pre_mix, m_g_post_mix, m_g_pre_mlp, m_g_post_mlp, m_conv_w_in, m_conv_b_in, m_conv_dw, m_conv_dw_b, m_conv_ln_g, m_conv_ln_b, m_conv_w_out, m_conv_b_out, m_w_ada_kv, m_b_ada_kv, m_g_kv, m_w_dkv, m_g_ckv, m_w_kr, m_w_uk, m_w_uv, m_w_dq, m_g_cq, m_w_uq, m_w_o, m_mlp_w_up, m_mlp_w_down, v_w_ada_mix, v_b_ada_mix, v_w_ada_mlp, v_b_ada_mlp, v_g_pre_mix, v_g_post_mix, v_g_pre_mlp, v_g_post_mlp, v_conv_w_in, v_conv_b_in, v_conv_dw, v_conv_dw_b, v_conv_ln_g, v_conv_ln_b, v_conv_w_out, v_conv_b_out, v_w_ada_kv, v_b_ada_kv, v_g_kv, v_w_dkv, v_g_ckv, v_w_kr, v_w_uk, v_w_uv, v_w_dq, v_g_cq, v_w_uq, v_w_o, v_mlp_w_up, v_mlp_w_down):
    weights = dict(w_ada_mix=w_ada_mix, b_ada_mix=b_ada_mix, w_ada_mlp=w_ada_mlp, b_ada_mlp=b_ada_mlp, g_pre_mix=g_pre_mix, g_post_mix=g_post_mix, g_pre_mlp=g_pre_mlp, g_post_mlp=g_post_mlp, conv_w_in=conv_w_in, conv_b_in=conv_b_in, conv_dw=conv_dw, conv_dw_b=conv_dw_b, conv_ln_g=conv_ln_g, conv_ln_b=conv_ln_b, conv_w_out=conv_w_out, conv_b_out=conv_b_out, w_ada_kv=w_ada_kv, b_ada_kv=b_ada_kv, g_kv=g_kv, w_dkv=w_dkv, g_ckv=g_ckv, w_kr=w_kr, w_uk=w_uk, w_uv=w_uv, w_dq=w_dq, g_cq=g_cq, w_uq=w_uq, w_o=w_o, mlp_w_up=mlp_w_up, mlp_w_down=mlp_w_down)
    m_in = dict(w_ada_mix=m_w_ada_mix, b_ada_mix=m_b_ada_mix, w_ada_mlp=m_w_ada_mlp, b_ada_mlp=m_b_ada_mlp, g_pre_mix=m_g_pre_mix, g_post_mix=m_g_post_mix, g_pre_mlp=m_g_pre_mlp, g_post_mlp=m_g_post_mlp, conv_w_in=m_conv_w_in, conv_b_in=m_conv_b_in, conv_dw=m_conv_dw, conv_dw_b=m_conv_dw_b, conv_ln_g=m_conv_ln_g, conv_ln_b=m_conv_ln_b, conv_w_out=m_conv_w_out, conv_b_out=m_conv_b_out, w_ada_kv=m_w_ada_kv, b_ada_kv=m_b_ada_kv, g_kv=m_g_kv, w_dkv=m_w_dkv, g_ckv=m_g_ckv, w_kr=m_w_kr, w_uk=m_w_uk, w_uv=m_w_uv, w_dq=m_w_dq, g_cq=m_g_cq, w_uq=m_w_uq, w_o=m_w_o, mlp_w_up=m_mlp_w_up, mlp_w_down=m_mlp_w_down)
    v_in = dict(w_ada_mix=v_w_ada_mix, b_ada_mix=v_b_ada_mix, w_ada_mlp=v_w_ada_mlp, b_ada_mlp=v_b_ada_mlp, g_pre_mix=v_g_pre_mix, g_post_mix=v_g_post_mix, g_pre_mlp=v_g_pre_mlp, g_post_mlp=v_g_post_mlp, conv_w_in=v_conv_w_in, conv_b_in=v_conv_b_in, conv_dw=v_conv_dw, conv_dw_b=v_conv_dw_b, conv_ln_g=v_conv_ln_g, conv_ln_b=v_conv_ln_b, conv_w_out=v_conv_w_out, conv_b_out=v_conv_b_out, w_ada_kv=v_w_ada_kv, b_ada_kv=v_b_ada_kv, g_kv=v_g_kv, w_dkv=v_w_dkv, g_ckv=v_g_ckv, w_kr=v_w_kr, w_uk=v_w_uk, w_uv=v_w_uv, w_dq=v_w_dq, g_cq=v_g_cq, w_uq=v_w_uq, w_o=v_w_o, mlp_w_up=v_mlp_w_up, mlp_w_down=v_mlp_w_down)
    layer1 = {}

    def early_updates(red):
        for n, key in (("mlp_w_up", "up1"), ("mlp_w_down", "down1")):
            layer1[n] = _adamw_layer(n, weights[n], red[key], m_in[n], v_in[n], 1)
        return [layer1[n][1] for n in layer1]

    grads, last_group = _step_grads(x, c, positions, loss_target, weights, early_updates)
    loss, grad_x = grads.pop("loss"), grads.pop("x")
    names = list(weights)
    upd = {}
    for n in names:
        if isinstance(grads.get(n), list):
            res = layer1.get(n) or _adamw_layer(n, weights[n], grads[n][1], m_in[n], v_in[n], 1)
            grads[n], *upd[n] = _adamw_layer(n, weights[n], grads[n][0], m_in[n], v_in[n], 0, prev=res)
        elif n in grads:
            upd[n] = _adamw(n, weights[n], grads[n], m_in[n], v_in[n])
    grads.update(last_group([upd[n][0] for n in ("w_ada_mix", "w_ada_mlp", "w_ada_kv", "mlp_w_down")]))
    upd.update({n: _adamw(n, weights[n], grads[n], m_in[n], v_in[n]) for n in names if n not in upd})
    return (loss, grad_x, *[grads[n] for n in names], *[upd[n][0] for n in names], *[upd[n][1] for n in names],
            *[upd[n][2] for n in names])


def _step_grads(x, c, positions, loss_target, w, early_updates):
    xi, yi, ci = lax.axis_index("x"), lax.axis_index("y"), lax.axis_index("c")
    chip = 2 * xi + yi
    dev = 2 * chip + ci
    place_idx = jnp.stack([ci, chip]).astype(jnp.int32)
    t, d = x.shape[1], x.shape[2]
    dl = d // N_CHIPS
    r_kv, r_q = w["w_dkv"].shape[1], w["w_dq"].shape[2]
    n_ada, n_kvada = w["w_ada_mix"].shape[2], w["w_ada_kv"].shape[1]
    heads_l = w["w_uq"].shape[2] // (QK_NOPE + QK_ROPE)
    assert dl == r_kv == r_q and dl % LANES == 0 and heads_l * N_CHIPS % 2 == 0

    def chip_cols(a, width):
        return lax.dynamic_slice_in_dim(a, chip * width, width, axis=a.ndim - 1)

    uq = w["w_uq"][0].reshape(r_q, heads_l, QK_NOPE + QK_ROPE)
    uq_nope, uq_rope = uq[:, :, :QK_NOPE].reshape(r_q, -1), uq[:, :, QK_NOPE:].reshape(r_q, -1)
    kr_pad = jnp.pad(w["w_kr"], ((0, 0), (0, LANES - QK_ROPE)))
    small = jnp.concatenate([w["w_dkv"], kr_pad, w["w_dq"][0], w["w_uk"], w["w_uv"], uq_nope, uq_rope], axis=1)
    widths = [r_kv, LANES, r_q, w["w_uk"].shape[1], w["w_uv"].shape[1], uq_nope.shape[1], uq_rope.shape[1]]
    so = [0]
    for wd in widths:
        so.append(so[-1] + wd)
    big = dict(conv_w_in=(w["conv_w_in"], 0), conv_w_out=(w["conv_w_out"], 0), w_o=(w["w_o"], 0),
               up0=(w["mlp_w_up"], 0), up1=(w["mlp_w_up"], 1), down0=(w["mlp_w_down"], 0),
               down1=(w["mlp_w_down"], 1), small=(small[None], 0))
    groups = dict(conv_in=["conv_w_in"], conv_out=["conv_w_out"], up0=["up0"], down0=["down0"],
                  mla=["small", "w_o"], up1=["up1"], down1=["down1"])
    gathered, pending = {}, {}

    def start_gathers(gnames, after):
        token = after
        for gname in gnames:
            slots = [_cast_into_slot(n, *big[n], place_idx, after=token) for n in groups[gname]]
            s_sems, r_sems, bufs, token = _gather_start("gather_start_" + gname, slots, after=token)
            pending[gname] = (s_sems, r_sems, bufs)
        return token

    forwarding = {}

    def early_forward(gname, after):
        s_sems, r_sems, bufs = pending[gname]
        bufs = _gather_wait("gather_wait_" + gname, s_sems, r_sems, bufs, after)
        forwarding[gname], token = _forward_start("forward_start_" + gname, bufs)
        return token

    def finish_gather(gname, after):
        if gname in forwarding:
            bufs = _forward_wait("forward_wait_" + gname, forwarding[gname], after)
        else:
            s_sems, r_sems, bufs = pending[gname]
            bufs = _gather_wait("gather_wait_" + gname, s_sems, r_sems, bufs, after)
            bufs = _forward_to_sibling("forward_" + gname, bufs)
        gathered.update(zip(groups[gname], bufs))

    rowl = lambda a: a.reshape(-1, a.shape[2])
    w_up_f = lambda l: gathered["up%d" % l]
    w_down_f = lambda l: rowl(gathered["down%d" % l])

    pack_rows = [w["conv_dw"][0], w["conv_dw_b"], w["conv_ln_g"], w["conv_ln_b"], w["conv_b_out"],
                 w["conv_b_in"].reshape(2, dl), c.reshape(N_CHIPS, dl)]
    pack_rows = [jnp.pad(p, ((0, (-p.shape[0]) % 8), (0, 0))) for p in pack_rows]
    po = [0]
    for p in pack_rows:
        po.append(po[-1] + p.shape[0])
    packs = _all_gather_small("gather_params", jnp.concatenate(pack_rows, axis=0))
    by_chip = packs[0::2]

    def full_width(r0, nr):
        return jnp.transpose(by_chip[:, r0:r0 + nr, :], (1, 0, 2)).reshape(nr, d)

    dw_full, dwb_full = full_width(po[0], CONV_WIDTH), full_width(po[1], 1)
    lng_full, lnb_full, bout_full = full_width(po[2], 1), full_width(po[3], 1), full_width(po[4], 1)
    bin_full = by_chip[:, po[5]:po[5] + 2, :].reshape(1, 2 * d)
    first_start = start_gathers(("conv_in",), packs)[0:1, 0:1]
    c_all = packs[:, po[6]:po[6] + N_CHIPS, :].reshape(N_DEV, d) + first_start

    sc_all = _ew("silu_c", lambda a: (a * jax.nn.sigmoid(a),), [c_all], (F32,))[0]
    ada_w = [(w["w_ada_mix"], 0), (w["w_ada_mix"], 1), (w["w_ada_mlp"], 0), (w["w_ada_mlp"], 1),
             (w["w_ada_kv"][None], 0)]
    ada_b = [chip_cols(w["b_ada_mix"][0:1], n_ada), chip_cols(w["b_ada_mix"][1:2], n_ada),
             chip_cols(w["b_ada_mlp"][0:1], n_ada), chip_cols(w["b_ada_mlp"][1:2], n_ada),
             chip_cols(w["b_ada_kv"].reshape(1, -1), n_kvada)]
    mods = [_mm_nn("ada_fwd_%d" % i, sc_all, wi, "row", extras=(jnp.broadcast_to(bi, (N_DEV, bi.shape[1])),),
                   epilogue=_add_epilogue, w_layer=li) for i, ((wi, li), bi) in enumerate(zip(ada_w, ada_b))]
    mods_all = _all_gather_small("gather_mods", jnp.concatenate(mods, axis=1))
    started = start_gathers(("conv_out", "up0", "down0", "mla", "up1", "down1"),
                            mods_all[0, 0:1, 0:1] + first_start)[0:1, 0:1]
    mine = lax.dynamic_index_in_dim(mods_all[0::2], dev, axis=1, keepdims=False)
    offs = [0]
    for m_ in mods:
        offs.append(offs[-1] + m_.shape[1])
    mod_vec = [mine[:, offs[i]:offs[i + 1]].reshape(1, -1) for i in range(5)]
    split3 = lambda v: (v[:, :d], v[:, d:2 * d], v[:, 2 * d:])
    mix = [split3(mod_vec[0]), split3(mod_vec[1])]
    mlp = [split3(mod_vec[2]), split3(mod_vec[3])]
    kv_shift, kv_scale = mod_vec[4][:, :d], mod_vec[4][:, d:]

    cos_t, sin_t = _rope_tables(positions, t)
    vec = lambda a: a.reshape(1, -1)
    gpm, gqm = w["g_pre_mix"], w["g_post_mix"]
    gpl, gql = w["g_pre_mlp"], w["g_post_mlp"]
    h0 = x[0]
    after_token = lambda v, token: v + token[0:1, 0:1]

    def post_and_mlp_pre(name, l, h, y, gate, g, bias=None):
        sh, sc, _ = mlp[l]
        if bias is None:
            def fn(h_, y_, gt, g_, gp, sh_, sc_):
                hn_ = h_ + _f_post(y_, gt, g_)[0]
                return hn_, _f_pre(hn_, gp, sh_, sc_)[0]
            vecs = [gate, g, vec(gpl[l]), sh, sc]
        else:
            def fn(h_, y_, b_, gt, g_, gp, sh_, sc_):
                hn_ = h_ + _f_post_bias(y_, b_, gt, g_)[0]
                return hn_, _f_pre(hn_, gp, sh_, sc_)[0]
            vecs = [bias, gate, g, vec(gpl[l]), sh, sc]
        return _rw_fwd(name, fn, [h, y], vecs, (F32, BF))

    def mlp_fwd(l, hn):
        finish_gather("up%d" % l, hn)
        s = _mm_nn("mlp_up_%d" % l, hn, w_up_f(l), "col", o_dtypes=(BF,), epilogue=_relu2_epilogue, tiles=WIDE)
        finish_gather("down%d" % l, s)
        y = _mm_nn("mlp_down_%d" % l, s, w_down_f(l), "row", tiles=DEEP)
        return hn, s, y

    def post_fwd(name, h, y, gate, g, bias=None):
        if bias is None:
            return _rw_fwd(name, lambda h_, y_, gt, g_: (h_ + _f_post(y_, gt, g_)[0],), [h, y], [gate, g], (F32,))[0]
        return _rw_fwd(name, lambda h_, y_, b_, gt, g_: (h_ + _f_post_bias(y_, b_, gt, g_)[0],), [h, y],
                       [bias, gate, g], (F32,))[0]

    (hn0,) = _rw_fwd("conv_pre", _f_pre, [h0], [vec(gpm[0]), mix[0][0] + started, mix[0][1]], (BF,))
    finish_gather("conv_in", hn0)
    w_in_f = gathered["conv_w_in"]
    z0 = _mm_nn("conv_in", hn0, w_in_f, "col")
    token = early_forward("conv_out", z0)
    (u0,) = _rw_fwd("conv_glu", _f_glu, [z0], [after_token(bin_full, token)], (F32,))
    uc0 = _conv_fwd(u0, dw_full, dwb_full)
    (n0,) = _rw_fwd("conv_ln", _f_ln_silu, [uc0], [lng_full, lnb_full], (BF,))
    finish_gather("conv_out", n0)
    w_out_f = rowl(gathered["conv_w_out"])
    y0 = _mm_nn("conv_out", n0, w_out_f, "row")
    token = early_forward("up0", y0)
    h1, hn1 = post_and_mlp_pre("conv_post_mlp_pre", 0, h0, y0, after_token(mix[0][2], token), vec(gqm[0]),
                               bias=bout_full)
    hn1, s1, y1 = mlp_fwd(0, hn1)

    def post_and_mla_pre(h_, y_, gt, g_, g1, sh1, sc1, g2, sh2, sc2):
        hn_ = h_ + _f_post(y_, gt, g_)[0]
        return (hn_,) + _f_pre2(hn_, g1, sh1, sc1, g2, sh2, sc2)

    h2, kvn, hn2 = _rw_fwd("mlp_post_mla_pre", post_and_mla_pre, [h1, y1],
                           [mlp[0][2], vec(gql[0]), vec(w["g_kv"]), kv_shift, kv_scale, vec(gpm[1]), mix[1][0], mix[1][1]],
                           (F32, BF, BF))
    finish_gather("mla", kvn)
    gs = gathered["small"]
    w_dkvkr = rowl(gs[:, :, so[0]:so[2]])
    w_dq_f = rowl(gs[:, :, so[2]:so[3]])
    w_uk_f, w_uv_f = gs[:, :, so[3]:so[4]], gs[:, :, so[4]:so[5]]
    w_qn_f, w_qr_f = gs[:, :, so[5]:so[6]], gs[:, :, so[6]:so[7]]
    w_o_f = rowl(gathered["w_o"])
    pre_kv =_mm_nn("mla_dkv", kvn, w_dkvkr, "row")
    pre_q = _mm_nn("mla_dq", hn2, w_dq_f, "row")
    f_lat = _make_f_lat(r_kv)
    c_kv, kr, c_q = _rw_fwd("mla_latent", f_lat, [pre_kv, pre_q, cos_t, sin_t], [vec(w["g_ckv"]), vec(w["g_cq"][0])],
                            (BF, BF, BF))
    kn = _mm_nn("mla_uk", c_kv, w_uk_f, "col", o_dtypes=(BF,))
    vv = _mm_nn("mla_uv", c_kv, w_uv_f, "col", o_dtypes=(BF,))
    qn = _mm_nn("mla_uq_nope", c_q, w_qn_f, "col", o_dtypes=(BF,))
    qr_pre = _mm_nn("mla_uq_rope", c_q, w_qr_f, "col")
    (qr,) = _rw_fwd("mla_q_rope", _f_rope, [qr_pre, cos_t, sin_t], [], (BF,))
    att = _attn_fwd(qn, qr, kn, kr, vv)
    token = early_forward("up1", att)
    y2 = _mm_nn("mla_o", att, w_o_f, "row")
    h3, hn3 = post_and_mlp_pre("mla_post_mlp_pre", 1, h2, y2, after_token(mix[1][2], token), vec(gqm[1]))
    hn3, s3, y3 = mlp_fwd(1, hn3)
    h4 = post_fwd("mlp_post_1", h3, y3, mlp[1][2], vec(gql[1]))

    dh4, loss_part = _loss_grad(h4, loss_target[0])

    gw = {}
    gvec = {}

    dw_mm = functools.partial(_mm_tn, o_dtype=BF)

    def mlp_bwd(l, h_in, hn, s, y, dh, token=None):
        sh, sc, gate = mlp[l]
        if token is not None:
            gate = after_token(gate, token)
        (dy,), (dgate, dgq) = _rw_bwd("mlp_post_bwd_%d" % l, _f_post, [y], [gate, vec(gql[l])], [dh], [BF])
        gw["down%d" % l] = dw_mm("mlp_down_dw_%d" % l, s, dy, "row", tiles=(512, 2048, 2048))
        da = _mm_nt("mlp_down_dx_%d" % l, dy, w_down_f(l), "row", o_dtypes=(BF,), extras=(s,),
                    epilogue=_relu2_bwd_epilogue, tiles=WIDE)
        gw["up%d" % l] = dw_mm("mlp_up_dw_%d" % l, hn, da, "col", tiles=WIDE)
        dhn = _mm_nt("mlp_up_dx_%d" % l, da, w_up_f(l), "col", tiles=DEEP)
        (dh_in,), (dgp, dsh, dsc) = _rw_bwd("mlp_pre_bwd_%d" % l, _f_pre, [h_in], [vec(gpl[l]), sh, sc], [dhn], [F32],
                                            add_rows={0: dh})
        gvec["mlp%d" % l] = (dsh, dsc, dgate)
        gvec["g_pre_mlp%d" % l], gvec["g_post_mlp%d" % l] = dgp, dgq
        return dh_in

    chunked = lambda a: a.reshape(N_CHIPS, a.shape[0] // N_CHIPS, a.shape[1])
    to_chunks = lambda a: a if a.ndim == 3 else chunked(a)
    red = {}

    def swap_begin(tag, names):
        return _swap_start("swap_start_" + tag, [to_chunks(gw[n]) for n in names])

    def reduce_begin(tag, names, after=None, swapped=None):
        state, token = _reduce_scatter_begin(tag, [to_chunks(gw[n]) for n in names], place_idx, after, swapped)
        return (names, state), token

    def reduce_end(handle, after):
        names, state = handle
        red.update(zip(names, _reduce_scatter_end(state, after)))

    dh3 = mlp_bwd(1, h3, hn3, s3, y3, dh4)
    sw1, token = swap_begin("r1", ["up1", "down1"])

    (dy2,), (dgate, dgq) = _rw_bwd("mla_post_bwd", _f_post, [y2], [after_token(mix[1][2], token), vec(gqm[1])],
                                   [dh3], [BF])
    gvec["g_post_mix1"] = dgq
    gw["w_o"] = dw_mm("mla_o_dw", att, dy2, "row")
    datt = _mm_nt("mla_o_dx", dy2, w_o_f, "row", o_dtypes=(BF,))
    rs1, token = reduce_begin("r1", ["up1", "down1"], after=datt, swapped=sw1)
    dqn, dqr, dkn, dvv, dkr = _attn_bwd(qn, qr, kn, kr + token[0:1, 0:1].astype(BF), vv, datt)
    (dqr_pre,), _ = _rw_bwd("mla_q_rope_bwd", _f_rope, [qr_pre, cos_t, sin_t], [], [dqr], [BF, None, None])
    g_qn = dw_mm("mla_uq_nope_dw", c_q, dqn, "col")
    g_qr = dw_mm("mla_uq_rope_dw", c_q, dqr_pre, "col")
    dc_q = _mm_nt("mla_uq_nope_dx", dqn, w_qn_f, "col")
    dc_q = _mm_nt("mla_uq_rope_dx", dqr_pre, w_qr_f, "col", extras=(dc_q,), epilogue=_add_epilogue)
    g_uk = dw_mm("mla_uk_dw", c_kv, dkn, "col")
    g_uv = dw_mm("mla_uv_dw", c_kv, dvv, "col")
    dc_kv = _mm_nt("mla_uk_dx", dkn, w_uk_f, "col")
    dc_kv = _mm_nt("mla_uv_dx", dvv, w_uv_f, "col", extras=(dc_kv,), epilogue=_add_epilogue)
    (dpre_kv, dpre_q), (dg_ckv, dg_cq) = _rw_bwd(
        "mla_latent_bwd", f_lat, [pre_kv, pre_q, cos_t, sin_t], [vec(w["g_ckv"]), vec(w["g_cq"][0])],
        [dc_kv, dkr, dc_q], [BF, BF, None, None])
    gvec["g_ckv"], gvec["g_cq"] = dg_ckv, dg_cq
    g_dkvkr = dw_mm("mla_dkv_dw", kvn, dpre_kv, "row")
    g_dq = dw_mm("mla_dq_dw", hn2, dpre_q, "row")
    dkvn = _mm_nt("mla_dkv_dx", dpre_kv, w_dkvkr, "row")
    dhn2 = _mm_nt("mla_dq_dx", dpre_q, w_dq_f, "row")
    (dh2,), (dg_kv, dkvsh, dkvsc, dgp, dsh, dsc) = _rw_bwd(
        "mla_pre_bwd", _f_pre2, [h2], [vec(w["g_kv"]), kv_shift, kv_scale, vec(gpm[1]), mix[1][0], mix[1][1]],
        [dkvn, dhn2], [F32], add_rows={0: dh3})
    gvec["mix1"] = (dsh, dsc, dgate)
    gvec["kv"] = (dkvsh, dkvsc)
    gvec["g_kv"], gvec["g_pre_mix1"] = dg_kv, dgp
    gw["small"] = jnp.concatenate([chunked(g_dkvkr), chunked(g_dq), g_uk, g_uv, g_qn, g_qr], axis=2)
    reduce_end(rs1, dh2)
    rs2, token = reduce_begin("r2", ["small", "w_o"])

    dh1 = mlp_bwd(0, h1, hn1, s1, y1, dh2, token=token)
    reduce_end(rs2, dh1)
    rs3, token = reduce_begin("r3", ["up0", "down0"])

    (dy0,), (dbout, dgate, dgq) = _rw_bwd("conv_post_bwd", _f_post_bias, [y0],
                                          [bout_full, after_token(mix[0][2], token), vec(gqm[0])], [dh1], [BF])
    gvec["g_post_mix0"] = dgq
    gw["conv_w_out"] = dw_mm("conv_out_dw", n0, dy0, "row")
    dn0 = _mm_nt("conv_out_dx", dy0, w_out_f, "row")
    (duc0,), (dlng, dlnb) = _rw_bwd("conv_ln_bwd", _f_ln_silu, [uc0], [lng_full, lnb_full], [dn0], [F32])
    du0, ddw = _conv_bwd(u0, duc0, dw_full)
    (dz0,), (dbin,) = _rw_bwd("conv_glu_bwd", _f_glu, [z0], [bin_full], [du0], [BF])
    gw["conv_w_in"] = dw_mm("conv_in_dw", hn0, dz0, "col")
    dhn0 = _mm_nt("conv_in_dx", dz0, w_in_f, "col")
    (dx,), (dgp, dsh, dsc) = _rw_bwd("conv_pre_bwd", _f_pre, [h0], [vec(gpm[0]), mix[0][0], mix[0][1]], [dhn0], [F32],
                                     add_rows={0: dh1})
    gvec["mix0"] = (dsh, dsc, dgate)
    gvec["g_pre_mix0"] = dgp
    reduce_end(rs3, [dx, gw["conv_w_in"], gw["conv_w_out"]] + early_updates(red))

    vec_list = [*gvec["mix0"], *gvec["mix1"], *gvec["mlp0"], *gvec["mlp1"], *gvec["kv"],
                gvec["g_pre_mix0"], gvec["g_pre_mix1"], gvec["g_post_mix0"], gvec["g_post_mix1"],
                gvec["g_pre_mlp0"], gvec["g_pre_mlp1"], gvec["g_post_mlp0"], gvec["g_post_mlp1"],
                gvec["g_kv"], gvec["g_ckv"], gvec["g_cq"], dbin, dlng, dlnb, dbout, ddw.reshape(1, -1),
                jnp.pad(loss_part, ((0, 0), (0, LANES - 1)))]
    vo = [0]
    for v_ in vec_list:
        vo.append(vo[-1] + v_.shape[1])
    vec_pad = (-vo[-1]) % (8 * LANES)
    n_vec = vo[-1] + vec_pad
    flat = jnp.concatenate(vec_list + [jnp.zeros((1, vec_pad), F32)], axis=1).reshape(8, n_vec // 8)
    all_vecs = _all_gather_small("gather_vector_grads", flat).reshape(N_DEV, 8, n_vec // 8)
    rs4, token = reduce_begin("r4", ["conv_w_in", "conv_w_out"], after=all_vecs)
    all_vecs = after_token(all_vecs, token)
    summed = _sum_devices(all_vecs).reshape(1, n_vec)
    per_dev = all_vecs.reshape(N_DEV, n_vec)
    seg = lambda a, i: a[:, vo[i]:vo[i + 1]]

    out = {"loss": seg(summed, 30)[0, 0], "x": dx.reshape(x.shape)}
    dm_mix = [jnp.concatenate([seg(per_dev, 3 * l + i) for i in range(3)], axis=1) for l in range(2)]
    dm_mlp = [jnp.concatenate([seg(per_dev, 6 + 3 * l + i) for i in range(3)], axis=1) for l in range(2)]
    dm_kv = jnp.concatenate([seg(per_dev, 12), seg(per_dev, 13)], axis=1)
    ada_dw = lambda name, dm, width: _mm_tn(name, sc_all, chip_cols(dm, width), "row")
    out["w_ada_mix"] = [ada_dw("ada_mix_dw_%d" % l, dm_mix[l], n_ada) for l in range(2)]
    out["w_ada_mlp"] = [ada_dw("ada_mlp_dw_%d" % l, dm_mlp[l], n_ada) for l in range(2)]
    out["w_ada_kv"] = ada_dw("ada_kv_dw", dm_kv, n_kvada)
    sum_seg = lambda i: seg(summed, i)
    out["b_ada_mix"] = jnp.concatenate([jnp.concatenate([sum_seg(3 * l + i) for i in range(3)], axis=1) for l in range(2)], axis=0)
    out["b_ada_mlp"] = jnp.concatenate([jnp.concatenate([sum_seg(6 + 3 * l + i) for i in range(3)], axis=1) for l in range(2)], axis=0)
    out["b_ada_kv"] = jnp.concatenate([sum_seg(12), sum_seg(13)], axis=1).reshape(-1)
    out["g_pre_mix"] = jnp.concatenate([sum_seg(14), sum_seg(15)], axis=0)
    out["g_post_mix"] = jnp.concatenate([sum_seg(16), sum_seg(17)], axis=0)
    out["g_pre_mlp"] = jnp.concatenate([sum_seg(18), sum_seg(19)], axis=0)
    out["g_post_mlp"] = jnp.concatenate([sum_seg(20), sum_seg(21)], axis=0)
    out["g_kv"] = sum_seg(22).reshape(-1)
    out["g_ckv"] = sum_seg(23).reshape(-1)
    out["g_cq"] = sum_seg(24)
    out["conv_b_in"] = chip_cols(sum_seg(25), 2 * dl)
    out["conv_ln_g"] = chip_cols(sum_seg(26), dl)
    out["conv_ln_b"] = chip_cols(sum_seg(27), dl)
    out["conv_b_out"] = chip_cols(sum_seg(28), dl)
    ddw_sum = chip_cols(sum_seg(29).reshape(CONV_WIDTH + 1, d), dl)
    out["conv_dw"] = ddw_sum[:CONV_WIDTH].reshape(1, CONV_WIDTH, dl)
    out["conv_dw_b"] = ddw_sum[CONV_WIDTH:]

    def last_group(after):
        reduce_end(rs4, after)
        return {"conv_w_in": red["conv_w_in"][None], "conv_w_out": red["conv_w_out"][None]}

    out["w_o"] = red["w_o"][None]
    out["mlp_w_up"] = [red["up0"], red["up1"]]
    out["mlp_w_down"] = [red["down0"], red["down1"]]
    rs = red["small"]
    piece = lambda i: rs[:, so[i]:so[i + 1]]
    out["w_dkv"] = piece(0)
    out["w_kr"] = piece(1)[:, :QK_ROPE]
    out["w_dq"] = piece(2)[None]
    out["w_uk"], out["w_uv"] = piece(3), piece(4)
    out["w_uq"] = jnp.concatenate([piece(5).reshape(r_q, heads_l, QK_NOPE), piece(6).reshape(r_q, heads_l, QK_ROPE)],
                                  axis=2).reshape(1, r_q, -1)
    return out, last_group
```
